```python
import jax
import jax.numpy as jnp
from jax import lax
import numpy as np

D_MODEL = 1024
BATCH = 16
SEQ = 256
DEPTH = 2
DEC_BATCH = 2
DEC_SEQ = 4096
PAST_LEN = 512

HEAD_DIM = 64
H_ATT = D_MODEL // (2 * HEAD_DIM)
G_FOURIER = D_MODEL // (4 * HEAD_DIM)
H_MLSTM = D_MODEL // (4 * HEAD_DIM)
N_GROUPS = H_ATT + G_FOURIER + H_MLSTM
A_W = H_ATT * HEAD_DIM
B_W = G_FOURIER * HEAD_DIM
C_W = H_MLSTM * HEAD_DIM
D_MIX = A_W + B_W + C_W
N_GATE_COLS = 2 * 2 * H_MLSTM
IN_COLS = 3 * A_W + B_W + 4 * C_W + N_GATE_COLS
GRID_W = 64
WIN_R = 8
WIN_C = 16
Q_BLOCK = 128
MLSTM_CHUNK = 128
N_EXPERTS = 16
CAPACITY_FACTOR = 2
EXPERT_FF = 2 * D_MODEL
EPS = 1e-6
NEG = -1e30

kernel_name = 'hybrid_natten_fnet_mlstm_ec_diffusion_step'


def rmsnorm(x, g):
    xf = x.astype(jnp.float32)
    y = xf * lax.rsqrt(jnp.mean(xf * xf, axis=-1, keepdims=True) + EPS)
    return (y * g.astype(jnp.float32)).astype(x.dtype)


def modulation(cvec, w_ada, b_ada, dtype):
    m = jax.nn.silu(cvec.astype(jnp.float32)) @ w_ada.astype(jnp.float32) + b_ada.astype(jnp.float32)
    return jnp.split(m.astype(dtype), 6, axis=-1)


def project(h, w_in, b_gates):
    B_, N_, _ = h.shape
    p = h @ w_in
    o3 = 3 * A_W
    o4 = o3 + B_W
    cuts = [A_W, 2 * A_W, o3, o4, o4 + C_W, o4 + 2 * C_W, o4 + 3 * C_W, o4 + 4 * C_W]
    qa, ka, va, ub, qc, kc, vc, oc, gt = jnp.split(p, cuts, axis=-1)
    heads = lambda t: t.reshape(B_, N_, -1, HEAD_DIM)
    gates = gt.astype(jnp.float32).reshape(B_, N_, 2, 2, H_MLSTM) + b_gates.astype(jnp.float32)
    return heads(qa), heads(ka), heads(va), heads(ub), heads(qc), heads(kc), heads(vc), oc, gates


def dense_attention(q, k, v):
    B_, S_, H_, dh = q.shape
    nb = S_ // Q_BLOCK
    qb = jnp.moveaxis(q.reshape(B_, nb, Q_BLOCK, H_, dh), 1, 0)
    scale = dh ** -0.5

    def one(qblk):
        s = jnp.einsum('bqhd,bkhd->bhqk', qblk, k).astype(jnp.float32) * scale
        w = jax.nn.softmax(s, axis=-1).astype(v.dtype)
        return jnp.einsum('bhqk,bkhd->bqhd', w, v)

    o = lax.map(one, qb)
    return jnp.moveaxis(o, 0, 1).reshape(B_, S_, H_, dh)


def neighbourhood_attention(q, k, v, ck, cv, rpb):
    B_, N_, H_, dh = q.shape
    rows = N_ // GRID_W
    kr = min(WIN_R, rows)
    scale = dh ** -0.5
    qg = q.reshape(B_, rows, GRID_W, H_, dh)
    kg = k.reshape(B_, rows, GRID_W, H_, dh)
    vg = v.reshape(B_, rows, GRID_W, H_, dh)
    col = np.arange(GRID_W)
    cs = np.clip(col - WIN_C // 2, 0, GRID_W - WIN_C)
    col_ok = jnp.asarray((col[None, :] >= cs[:, None]) & (col[None, :] < cs[:, None] + WIN_C))
    dc_idx = jnp.asarray(np.clip(col[None, :] - col[:, None] + WIN_C - 1, 0, 2 * WIN_C - 2))
    n_loc = kr * GRID_W

    def one_row(r):
        rs = jnp.clip(r - kr // 2, 0, rows - kr)
        kb = lax.dynamic_slice_in_dim(kg, rs, kr, axis=1)
        vb = lax.dynamic_slice_in_dim(vg, rs, kr, axis=1)
        qr = lax.dynamic_index_in_dim(qg, r, axis=1, keepdims=False)
        dr_idx = rs + jnp.arange(kr) - r + WIN_R - 1
        bias = rpb[:, dr_idx[None, :, None], dc_idx[:, None, :]].astype(jnp.float32)
        s_loc = jnp.einsum('bqhd,brkhd->bhqrk', qr, kb).astype(jnp.float32) * scale + bias[None]
        s_loc = jnp.where(col_ok[None, None, :, None, :], s_loc, NEG).reshape(B_, H_, GRID_W, n_loc)
        s_ctx = jnp.einsum('bqhd,bkhd->bhqk', qr, ck).astype(jnp.float32) * scale
        w = jax.nn.softmax(jnp.concatenate([s_loc, s_ctx], axis=-1), axis=-1).astype(v.dtype)
        out = jnp.einsum('bhqk,bkhd->bqhd', w[..., :n_loc], vb.reshape(B_, n_loc, H_, dh))
        return out + jnp.einsum('bhqk,bkhd->bqhd', w[..., n_loc:], cv.astype(v.dtype))

    o = lax.map(one_row, jnp.arange(rows))
    return jnp.moveaxis(o, 0, 1).reshape(B_, N_, H_, dh)


def fourier_mix(u, w_fourier):
    z = jnp.fft.fft2(u.astype(jnp.float32), axes=(1, 3), norm='ortho').real
    return jnp.einsum('bngc,gcd->bngd', z.astype(u.dtype), w_fourier)


def mlstm_direction(q, k, v, ig, fg, C0, n0, m0):
    B_, H_, N_, dh = q.shape
    L = MLSTM_CHUNK
    nc = N_ // L
    chunks = lambda a: jnp.moveaxis(a.reshape(B_, H_, nc, L, *a.shape[3:]), 2, 0)
    xs = (chunks(q), chunks(k), chunks(v), chunks(ig), chunks(jax.nn.log_sigmoid(fg)))
    tril = jnp.tril(jnp.ones((L, L), dtype=bool))

    def step(carry, inp):
        C, n, m = carry
        qc, kc, vc, ic, lfc = inp
        b = jnp.cumsum(lfc, axis=-1)
        D = jnp.where(tril, b[..., :, None] - b[..., None, :] + ic[..., None, :], NEG)
        inter = b + m[..., None]
        m_t = jnp.maximum(inter, jnp.max(D, axis=-1))
        S = jnp.einsum('bhtd,bhsd->bhts', qc, kc) * jnp.exp(D - m_t[..., None])
        w_in = jnp.exp(inter - m_t)
        num = jnp.einsum('bhts,bhsd->bhtd', S, vc) + w_in[..., None] * jnp.einsum('bhvk,bhtk->bhtv', C, qc)
        den = jnp.sum(S, axis=-1) + w_in * jnp.einsum('bhk,bhtk->bht', n, qc)
        h = num / jnp.maximum(jnp.abs(den), jnp.exp(-m_t))[..., None]
        bL = b[..., -1]
        g = bL[..., None] - b + ic
        m_new = jnp.maximum(bL + m, jnp.max(g, axis=-1))
        wc = jnp.exp(bL + m - m_new)
        ws = jnp.exp(g - m_new[..., None])
        C_new = wc[..., None, None] * C + jnp.einsum('bhs,bhsv,bhsk->bhvk', ws, vc, kc)
        n_new = wc[..., None] * n + jnp.einsum('bhs,bhsk->bhk', ws, kc)
        return (C_new, n_new, m_new), h

    (C, n, m), hs = lax.scan(step, (C0, n0, m0), xs)
    h = jnp.moveaxis(hs, 0, 2).reshape(B_, H_, N_, dh)
    return h, C, n, m


def mlstm_bidir(q, k, v, gates, C0, n0, m0):
    dt = q.dtype
    f32 = jnp.float32
    tr = lambda t: jnp.transpose(t, (0, 2, 1, 3)).astype(f32)
    qf, kf, vf = tr(q), tr(k) * (HEAD_DIM ** -0.5), tr(v)
    g = jnp.transpose(gates, (2, 3, 0, 4, 1))
    C0, n0, m0 = C0.astype(f32), n0.astype(f32), m0.astype(f32)
    h_f, Cf, nf, mf = mlstm_direction(qf, kf, vf, g[0, 0], g[0, 1], C0[:, 0], n0[:, 0], m0[:, 0])
    fl = lambda t: jnp.flip(t, axis=2)
    h_b, Cb, nb, mb = mlstm_direction(fl(qf), fl(kf), fl(vf), jnp.flip(g[1, 0], -1), jnp.flip(g[1, 1], -1),
                                      C0[:, 1], n0[:, 1], m0[:, 1])
    h = jnp.transpose(h_f + fl(h_b), (0, 2, 1, 3)).astype(dt)
    return h, jnp.stack([Cf, Cb], axis=1), jnp.stack([nf, nb], axis=1), jnp.stack([mf, mb], axis=1)


def merge(att, four, mem, o_pre, g_head, w_out):
    y = rmsnorm(jnp.concatenate([att, four, mem], axis=2), g_head)
    B_, N_ = y.shape[:2]
    y = y.reshape(B_, N_, D_MIX)
    y = jnp.concatenate([y[..., :A_W + B_W], y[..., A_W + B_W:] * jax.nn.sigmoid(o_pre)], axis=-1)
    return y @ w_out


def expert_choice_ffn(h, w_router, w_g, w_u, w_d):
    B_, N_, _ = h.shape
    cap = CAPACITY_FACTOR * N_ // N_EXPERTS
    aff = jax.nn.softmax((h @ w_router).astype(jnp.float32), axis=-1)
    gval, idx = lax.top_k(jnp.swapaxes(aff, 1, 2), cap)
    bidx = jnp.arange(B_)[:, None, None]
    xs = h[bidx, idx]
    a = jnp.einsum('becd,edf->becf', xs, w_g)
    u = jnp.einsum('becd,edf->becf', xs, w_u)
    y = jnp.einsum('becf,efd->becd', jax.nn.silu(a) * u, w_d) * gval[..., None].astype(h.dtype)
    return jnp.zeros_like(h).at[bidx, idx].add(y)


def context_layer(x, mod, g1, g2, w_in, b_gates, w_fourier, g_head, w_out, w_router, w_g, w_u, w_d):
    sh1, sc1, ga1, sh2, sc2, ga2 = mod
    h = rmsnorm(x, g1) * (1 + sc1) + sh1
    qa, ka, va, ub, qc, kc, vc, oc, gates = project(h, w_in, b_gates)
    att = dense_attention(qa, ka, va)
    four = fourier_mix(ub, w_fourier)
    B_ = x.shape[0]
    zC = jnp.zeros((B_, 2, H_MLSTM, HEAD_DIM, HEAD_DIM), jnp.float32)
    zn = jnp.zeros((B_, 2, H_MLSTM, HEAD_DIM), jnp.float32)
    zm = jnp.zeros((B_, 2, H_MLSTM), jnp.float32)
    mem, C, n, m = mlstm_bidir(qc, kc, vc, gates, zC, zn, zm)
    x = x + ga1 * merge(att, four, mem, oc, g_head, w_out)
    h2 = rmsnorm(x, g2) * (1 + sc2) + sh2
    x = x + ga2 * expert_choice_ffn(h2, w_router, w_g, w_u, w_d)
    return x, ka, va, C, n, m


def latent_layer(x, mod, ck, cv, C0, n0, m0, g1, g2, w_in, b_gates, rpb, w_fourier, g_head, w_out,
                 w_router, w_g, w_u, w_d):
    sh1, sc1, ga1, sh2, sc2, ga2 = mod
    h = rmsnorm(x, g1) * (1 + sc1) + sh1
    qa, ka, va, ub, qc, kc, vc, oc, gates = project(h, w_in, b_gates)
    att = neighbourhood_attention(qa, ka, va, ck, cv, rpb)
    four = fourier_mix(ub, w_fourier)
    mem, _, _, _ = mlstm_bidir(qc, kc, vc, gates, C0, n0, m0)
    x = x + ga1 * merge(att, four, mem, oc, g_head, w_out)
    h2 = rmsnorm(x, g2) * (1 + sc2) + sh2
    return x + ga2 * expert_choice_ffn(h2, w_router, w_g, w_u, w_d)


def setup_inputs(seed: int = 0) -> dict:
    key = jax.random.key(seed)
    ks = jax.random.split(key, 24)
    f32 = jnp.float32
    nrm = lambda k, shape, s: jax.random.normal(k, shape, f32) * s
    L = DEPTH
    return {
        'x_prompt': nrm(ks[0], (BATCH, SEQ, D_MODEL), 1.0),
        'x_sample': nrm(ks[1], (DEC_BATCH, DEC_SEQ, D_MODEL), 1.0),
        'c': nrm(ks[2], (DEC_BATCH, D_MODEL), 1.0),
        'cache_k': nrm(ks[3], (DEC_BATCH, L, PAST_LEN, H_ATT, HEAD_DIM), 1.0),
        'cache_v': nrm(ks[4], (DEC_BATCH, L, PAST_LEN, H_ATT, HEAD_DIM), 1.0),
        'state_C': nrm(ks[5], (DEC_BATCH, L, 2, H_MLSTM, HEAD_DIM, HEAD_DIM), 0.3),
        'state_n': nrm(ks[6], (DEC_BATCH, L, 2, H_MLSTM, HEAD_DIM), 0.3),
        'state_m': nrm(ks[7], (DEC_BATCH, L, 2, H_MLSTM), 0.5),
        'c_ctx': nrm(ks[8], (D_MODEL,), 1.0),
        'w_ada': nrm(ks[9], (L, D_MODEL, 6 * D_MODEL), 0.5 * D_MODEL ** -0.5),
        'b_ada': nrm(ks[10], (L, 6 * D_MODEL), 0.02),
        'g_norm1': 1.0 + nrm(ks[11], (L, D_MODEL), 0.05),
        'g_norm2': 1.0 + nrm(ks[12], (L, D_MODEL), 0.05),
        'w_in': nrm(ks[13], (L, D_MODEL, IN_COLS), D_MODEL ** -0.5),
        'b_gates': nrm(ks[14], (L, 2, 2, H_MLSTM), 0.1) + jnp.array([0.0, 3.0], f32)[None, None, :, None],
        'rpb': nrm(ks[15], (L, H_ATT, 2 * WIN_R - 1, 2 * WIN_C - 1), 0.1),
        'w_fourier': nrm(ks[16], (L, G_FOURIER, HEAD_DIM, HEAD_DIM), HEAD_DIM ** -0.5),
        'g_head': 1.0 + nrm(ks[17], (L, N_GROUPS, HEAD_DIM), 0.05),
        'w_out': nrm(ks[18], (L, D_MIX, D_MODEL), D_MIX ** -0.5),
        'w_router': nrm(ks[19], (L, D_MODEL, N_EXPERTS), D_MODEL ** -0.5),
        'w_exp_gate': nrm(ks[20], (L, N_EXPERTS, D_MODEL, EXPERT_FF), D_MODEL ** -0.5),
        'w_exp_up': nrm(ks[21], (L, N_EXPERTS, D_MODEL, EXPERT_FF), D_MODEL ** -0.5),
        'w_exp_down': nrm(ks[22], (L, N_EXPERTS, EXPERT_FF, D_MODEL), EXPERT_FF ** -0.5),
        'g_final': 1.0 + nrm(ks[23], (D_MODEL,), 0.05),
    }


def reference(x_prompt, x_sample, c, cache_k, cache_v, state_C, state_n, state_m, c_ctx,
              w_ada, b_ada, g_norm1, g_norm2, w_in, b_gates, rpb, w_fourier, g_head, w_out,
              w_router, w_exp_gate, w_exp_up, w_exp_down, g_final):
    xp = x_prompt
    ks, vs, Cs, ns, ms = [], [], [], [], []
    for l in range(DEPTH):
        mod = modulation(c_ctx[None, None, :], w_ada[l], b_ada[l], xp.dtype)
        xp, k_l, v_l, C_l, n_l, m_l = context_layer(
            xp, mod, g_norm1[l], g_norm2[l], w_in[l], b_gates[l], w_fourier[l], g_head[l], w_out[l],
            w_router[l], w_exp_gate[l], w_exp_up[l], w_exp_down[l])
        ks.append(k_l)
        vs.append(v_l)
        Cs.append(C_l)
        ns.append(n_l)
        ms.append(m_l)
    y_prompt = rmsnorm(xp, g_final)
    new_k = jnp.stack(ks, axis=1)
    new_v = jnp.stack(vs, axis=1)
    new_C = jnp.stack(Cs, axis=1)
    new_n = jnp.stack(ns, axis=1)
    new_m = jnp.stack(ms, axis=1)

    xs = x_sample
    for l in range(DEPTH):
        mod = modulation(c[:, None, :], w_ada[l], b_ada[l], xs.dtype)
        xs = latent_layer(
            xs, mod, cache_k[:, l], cache_v[:, l], state_C[:, l], state_n[:, l], state_m[:, l],
            g_norm1[l], g_norm2[l], w_in[l], b_gates[l], rpb[l], w_fourier[l], g_head[l], w_out[l],
            w_router[l], w_exp_gate[l], w_exp_up[l], w_exp_down[l])
    y_sample = rmsnorm(xs, g_final)
    return (y_prompt, y_sample, new_k, new_v, new_C, new_n, new_m)
```

```python
import functools

import numpy as np
import jax
import jax.numpy as jnp
from jax import lax
from jax.experimental import pallas as pl
from jax.experimental.pallas import tpu as pltpu

F32 = jnp.float32
BF16 = jnp.bfloat16

D_MODEL = 1024
DEPTH = 2
HEAD_DIM = 64
H_ATT = 8
G_FOURIER = 4
H_MLSTM = 4
A_W = H_ATT * HEAD_DIM
B_W = G_FOURIER * HEAD_DIM
C_W = H_MLSTM * HEAD_DIM
N_GATE_COLS = 16
P_COLS = 3 * A_W + B_W + 4 * C_W
IN_COLS = P_COLS + N_GATE_COLS
GRID_W = 64
WIN_R = 8
WIN_C = 16
MLSTM_CHUNK = 128
N_EXPERTS = 16
CAPACITY_FACTOR = 2
EXPERT_FF = 2 * D_MODEL
EPS = 1e-6
NEG = -1e30

UB_OFF = 3 * A_W
QC_BLK, KC_BLK, VC_BLK, OC_BLK = 7, 8, 9, 10

NT_DIMS = (((1,), (1,)), ((), ()))
TN_DIMS = (((0,), (0,)), ((), ()))
MIB = 1024 * 1024


def _dot(a, b, precision=None):
    return jnp.dot(a, b, preferred_element_type=F32, precision=precision)


def _dot_nt(a, b, precision=None):
    return lax.dot_general(a, b, NT_DIMS, preferred_element_type=F32, precision=precision)


def _dot_tn(a, b):
    return lax.dot_general(a, b, TN_DIMS, preferred_element_type=F32)


def _params(sem, vmem_mib=48):
    return pltpu.CompilerParams(dimension_semantics=sem, vmem_limit_bytes=vmem_mib * MIB)


def _silu(x):
    return x * jax.nn.sigmoid(x)


def _log_sigmoid(x):
    return jnp.minimum(x, 0.0) - jnp.log1p(jnp.exp(-jnp.abs(x)))


def _mod_kernel(c_ref, w_ref, b_ref, o_ref):
    s = _silu(c_ref[...]).astype(BF16)
    o_ref[0] = _dot(s, w_ref[0].astype(BF16)) + b_ref[0]


def _modulation(cvecs, w_ada, b_ada):
    depth = w_ada.shape[0]
    tn = 1024
    return pl.pallas_call(
        _mod_kernel,
        grid=(depth, 6 * D_MODEL // tn),
        in_specs=[
            pl.BlockSpec((8, D_MODEL), lambda l, j: (0, 0)),
            pl.BlockSpec((1, D_MODEL, tn), lambda l, j: (l, 0, j)),
            pl.BlockSpec((1, 1, tn), lambda l, j: (l, 0, j)),
        ],
        out_specs=pl.BlockSpec((1, 8, tn), lambda l, j: (l, 0, j)),
        out_shape=jax.ShapeDtypeStruct((depth, 8, 6 * D_MODEL), F32),
        compiler_params=_params(("arbitrary", "arbitrary")),
        name="modulation",
    )(cvecs, w_ada, b_ada.reshape(depth, 1, 6 * D_MODEL))


def _inproj_kernel(x_ref, mod_ref, g1_ref, w_ref, wgt_ref, bgr_ref, bgc_ref, csc_ref,
                   p_ref, g_ref, gt_ref, ab_ref):
    x = x_ref[...]
    y = x * lax.rsqrt(jnp.mean(x * x, axis=-1, keepdims=True) + EPS) * g1_ref[0]
    h = (y * (1.0 + mod_ref[0, 1:2, :]) + mod_ref[0, 0:1, :]).astype(BF16)
    for j in range(0, P_COLS, 256):
        pj = _dot(h, w_ref[0, :, j:j + 256])
        p_ref[:, j:j + 256] = pj
        if j == UB_OFF:
            ab_ref[...] = _dot(pj.astype(BF16), csc_ref[...]).astype(BF16)
    g_ref[...] = _dot(h, w_ref[0, :, P_COLS:IN_COLS]) + bgr_ref[0]
    gt_ref[...] = _dot_nt(wgt_ref[0], h) + bgc_ref[0]


def _inproj(x2d, mod, layer, g1, w_in_bf, w_gt_bf, bg_row, bg_col, csc, seq_len):
    t = x2d.shape[0]
    tm = 256
    tiles_per_seq = seq_len // tm
    return pl.pallas_call(
        _inproj_kernel,
        grid=(t // tm,),
        in_specs=[
            pl.BlockSpec((tm, D_MODEL), lambda i: (i, 0)),
            pl.BlockSpec((1, 6, D_MODEL), lambda i: (i // tiles_per_seq, 0, 0)),
            pl.BlockSpec((1, 1, D_MODEL), lambda i: (layer, 0, 0)),
            pl.BlockSpec((1, D_MODEL, IN_COLS), lambda i: (layer, 0, 0)),
            pl.BlockSpec((1, N_GATE_COLS, D_MODEL), lambda i: (layer, 0, 0)),
            pl.BlockSpec((1, 1, N_GATE_COLS), lambda i: (layer, 0, 0)),
            pl.BlockSpec((1, N_GATE_COLS, 1), lambda i: (layer, 0, 0)),
            pl.BlockSpec((B_W, 2 * B_W), lambda i: (0, 0)),
        ],
        out_specs=[
            pl.BlockSpec((tm, P_COLS), lambda i: (i, 0)),
            pl.BlockSpec((tm, N_GATE_COLS), lambda i: (i, 0)),
            pl.BlockSpec((N_GATE_COLS, tm), lambda i: (0, i)),
            pl.BlockSpec((tm, 2 * B_W), lambda i: (i, 0)),
        ],
        out_shape=[
            jax.ShapeDtypeStruct((t, P_COLS), F32),
            jax.ShapeDtypeStruct((t, N_GATE_COLS), F32),
            jax.ShapeDtypeStruct((N_GATE_COLS, t), F32),
            jax.ShapeDtypeStruct((t, 2 * B_W), BF16),
        ],
        compiler_params=_params(("arbitrary",)),
        name="inproj",
    )(x2d, mod, g1, w_in_bf, w_gt_bf, bg_row, bg_col, csc)


def _ctx_attn_kernel(q_ref, k_ref, v_ref, o_ref):
    scale = HEAD_DIM ** -0.5
    for h in range(H_ATT):
        sl = slice(HEAD_DIM * h, HEAD_DIM * (h + 1))
        q = q_ref[:, sl].astype(BF16)
        k = k_ref[:, sl].astype(BF16)
        v = v_ref[:, sl].astype(BF16)
        s = _dot_nt(q, k) * scale
        e = jnp.exp(s - jnp.max(s, axis=-1, keepdims=True))
        w = e * (1.0 / jnp.sum(e, axis=-1, keepdims=True))
        o_ref[:, sl] = _dot(w.astype(BF16), v)


def _ctx_attention(p, n_seq, seq_len):
    return pl.pallas_call(
        _ctx_attn_kernel,
        grid=(n_seq,),
        in_specs=[
            pl.BlockSpec((seq_len, A_W), lambda b: (b, 0)),
            pl.BlockSpec((seq_len, A_W), lambda b: (b, 1)),
            pl.BlockSpec((seq_len, A_W), lambda b: (b, 2)),
        ],
        out_specs=pl.BlockSpec((seq_len, A_W), lambda b: (b, 0)),
        out_shape=jax.ShapeDtypeStruct((n_seq * seq_len, A_W), F32),
        compiler_params=_params(("arbitrary",)),
        name="ctx_attention",
    )(p, p, p)


Q_ROWS = 8
K_ROWS = 16
KEY_BLK = 256


def _nbr_bias_table(rpb_l):
    cq = np.arange(GRID_W)[:, None]
    ck = np.arange(GRID_W)[None, :]
    cs = np.clip(cq - WIN_C // 2, 0, GRID_W - WIN_C)
    col_ok = (ck >= cs) & (ck < cs + WIN_C)
    dc = np.clip(ck - cq + WIN_C - 1, 0, 2 * WIN_C - 2)
    toe = jnp.where(jnp.asarray(col_ok)[None, None], rpb_l[:, :, dc].astype(F32), NEG)
    rows = GRID_W
    dr_all, ok_all = [], []
    for r0, k0 in ((0, 0), (Q_ROWS, Q_ROWS - WIN_R // 2), (rows - Q_ROWS, rows - K_ROWS)):
        r = r0 + np.arange(Q_ROWS)[:, None]
        rk = k0 + np.arange(K_ROWS)[None, :]
        rs = np.clip(r - WIN_R // 2, 0, rows - WIN_R)
        ok_all.append((rk >= rs) & (rk < rs + WIN_R))
        dr_all.append(np.clip(rk - r + WIN_R - 1, 0, 2 * WIN_R - 2))
    dr_all = np.stack(dr_all)
    ok_all = np.stack(ok_all)
    tab = toe[:, dr_all]
    tab = jnp.where(jnp.asarray(ok_all)[None, :, :, :, None, None], tab, NEG)
    tab = jnp.transpose(tab, (1, 0, 2, 4, 3, 5))
    return tab.reshape(3, H_ATT, Q_ROWS * GRID_W, K_ROWS * GRID_W)


def _nbr_attn_kernel(q_ref, k0_ref, k1_ref, k2_ref, k3_ref, v0_ref, v1_ref, v2_ref, v3_ref,
                     ck_ref, cv_ref, bias_ref, o_ref):
    scale = HEAD_DIM ** -0.5
    k_refs = (k0_ref, k1_ref, k2_ref, k3_ref)
    v_refs = (v0_ref, v1_ref, v2_ref, v3_ref)
    for hh in range(2):
        sl = slice(HEAD_DIM * hh, HEAD_DIM * (hh + 1))
        q = q_ref[:, sl].astype(BF16)
        s_loc = [
            _dot_nt(q, k_refs[j][:, sl].astype(BF16)) * scale
            + bias_ref[0, hh, :, KEY_BLK * j:KEY_BLK * (j + 1)]
            for j in range(4)
        ]
        s_ctx = _dot_nt(q, ck_ref[0, 0, :, sl].astype(BF16)) * scale
        m = jnp.max(s_ctx, axis=-1, keepdims=True)
        for s in s_loc:
            m = jnp.maximum(m, jnp.max(s, axis=-1, keepdims=True))
        e_ctx = jnp.exp(s_ctx - m)
        den = jnp.sum(e_ctx, axis=-1, keepdims=True)
        num = _dot(e_ctx.astype(BF16), cv_ref[0, 0, :, sl].astype(BF16))
        for j in range(4):
            e = jnp.exp(s_loc[j] - m)
            den = den + jnp.sum(e, axis=-1, keepdims=True)
            num = num + _dot(e.astype(BF16), v_refs[j][:, sl].astype(BF16))
        o_ref[:, sl] = num / den


def _nbr_attention(p, cache_k4, cache_v4, bias_tab, layer, n_seq, seq_len):
    q_tok = Q_ROWS * GRID_W
    n_rb = seq_len // q_tok
    kb_per_seq = seq_len // KEY_BLK
    max_base = kb_per_seq - 4

    def kmap(j, col0):
        def f(hp, rb, b):
            base = jnp.clip(2 * rb - 1, 0, max_base)
            return (b * kb_per_seq + base + j, col0 + hp)
        return f

    def variant(rb):
        return jnp.where(rb == 0, 0, jnp.where(rb == n_rb - 1, 2, 1))

    past = cache_k4.shape[2]
    in_specs = [pl.BlockSpec((q_tok, 128), lambda hp, rb, b: (b * n_rb + rb, hp))]
    in_specs += [pl.BlockSpec((KEY_BLK, 128), kmap(j, A_W // 128)) for j in range(4)]
    in_specs += [pl.BlockSpec((KEY_BLK, 128), kmap(j, 2 * A_W // 128)) for j in range(4)]
    in_specs += [
        pl.BlockSpec((1, 1, past, 128), lambda hp, rb, b: (b, layer, 0, hp)),
        pl.BlockSpec((1, 1, past, 128), lambda hp, rb, b: (b, layer, 0, hp)),
        pl.BlockSpec((1, 2, q_tok, K_ROWS * GRID_W), lambda hp, rb, b: (variant(rb), hp, 0, 0)),
    ]
    return pl.pallas_call(
        _nbr_attn_kernel,
        grid=(H_ATT // 2, n_rb, n_seq),
        in_specs=in_specs,
        out_specs=pl.BlockSpec((q_tok, 128), lambda hp, rb, b: (b * n_rb + rb, hp)),
        out_shape=jax.ShapeDtypeStruct((n_seq * seq_len, A_W), F32),
        compiler_params=_params(("arbitrary", "arbitrary", "arbitrary")),
        name="nbr_attention",
    )(p, p, p, p, p, p, p, p, p, cache_k4, cache_v4, bias_tab)


def _dft_mats(n):
    idx = jnp.arange(n, dtype=jnp.int32)
    ang = ((idx[:, None] * idx[None, :]) % n).astype(F32) * (2.0 * np.pi / n)
    return jnp.cos(ang).astype(BF16), jnp.sin(ang).astype(BF16)


def _channel_dft():
    c = np.arange(HEAD_DIM)
    ang = 2.0 * np.pi * ((c[:, None] * c[None, :]) % HEAD_DIM) / HEAD_DIM
    eye = np.eye(G_FOURIER)
    mats = np.concatenate([np.kron(eye, np.cos(ang)), np.kron(eye, np.sin(ang))], axis=1)
    return jnp.asarray(mats, F32).astype(BF16)


def _fourier_kernel(c_ref, s_ref, ab_ref, wf_ref, o_ref, acc_ref, *, scale, n_k):
    k = pl.program_id(2)

    @pl.when(k == 0)
    def _():
        acc_ref[...] = jnp.zeros_like(acc_ref)

    acc_ref[...] += _dot(c_ref[...], ab_ref[:, :B_W]) - _dot(s_ref[...], ab_ref[:, B_W:])

    @pl.when(k == n_k - 1)
    def _():
        z = (acc_ref[...] * scale).astype(BF16)
        o_ref[...] = _dot(z, wf_ref[0])


def _fourier(ab, cmat, smat, wf_blk, layer, n_seq, seq_len):
    ti = min(seq_len, 512)
    tk = min(seq_len, 1024)
    n_i, n_k = seq_len // ti, seq_len // tk
    scale = float((seq_len * HEAD_DIM) ** -0.5)
    return pl.pallas_call(
        functools.partial(_fourier_kernel, scale=scale, n_k=n_k),
        grid=(n_seq, n_i, n_k),
        in_specs=[
            pl.BlockSpec((ti, tk), lambda s, i, k: (i, k)),
            pl.BlockSpec((ti, tk), lambda s, i, k: (i, k)),
            pl.BlockSpec((tk, 2 * B_W), lambda s, i, k: (s * n_k + k, 0)),
            pl.BlockSpec((1, B_W, B_W), lambda s, i, k: (layer, 0, 0)),
        ],
        out_specs=pl.BlockSpec((ti, B_W), lambda s, i, k: (s * n_i + i, 0)),
        out_shape=jax.ShapeDtypeStruct((n_seq * seq_len, B_W), F32),
        scratch_shapes=[pltpu.VMEM((ti, B_W), F32)],
        compiler_params=_params(("arbitrary", "arbitrary", "arbitrary")),
        name="fourier",
    )(cmat, smat, ab, wf_blk)


def _mlstm_kernel(qf_ref, kf_ref, vf_ref, gf_ref, gtf_ref, qb_ref, kb_ref, vb_ref, gb_ref, gtb_ref,
                  c0_ref, n0_ref, m0_ref, tril_ref, triu_ref,
                  hf_ref, hb_ref, cout_ref, nout_ref, mout_ref, c_s, n_s, m_s, *, n_chunks):
    c = pl.program_id(1)
    hi = lax.Precision.HIGHEST

    @pl.when(c == 0)
    def _():
        c_s[...] = c0_ref[0]
        n_s[...] = n0_ref[0]
        m_s[...] = m0_ref[0]

    dirs = (
        (qf_ref, kf_ref, vf_ref, gf_ref, gtf_ref, hf_ref, tril_ref),
        (qb_ref, kb_ref, vb_ref, gb_ref, gtb_ref, hb_ref, triu_ref),
    )
    for d, (q_ref, k_ref, v_ref, g_ref, gt_ref, h_ref, mask_ref) in enumerate(dirs):
        mask = mask_ref[...]
        keep = mask > 0.5
        go = 2 * H_MLSTM * d
        ig_cols = g_ref[:, go:go + H_MLSTM]
        lf_cols = _log_sigmoid(g_ref[:, go + H_MLSTM:go + 2 * H_MLSTM])
        ig_rows = gt_ref[go:go + H_MLSTM, :]
        lf_rows = _log_sigmoid(gt_ref[go + H_MLSTM:go + 2 * H_MLSTM, :])
        b_cols = _dot(mask, lf_cols, precision=hi)
        b_rows = _dot_nt(lf_rows, mask, precision=hi)
        b_last = jnp.sum(lf_rows, axis=1, keepdims=True)
        for hd in range(H_MLSTM):
            i = H_MLSTM * d + hd
            sl = slice(HEAD_DIM * hd, HEAD_DIM * (hd + 1))
            q = q_ref[:, sl]
            k = k_ref[:, sl] * (HEAD_DIM ** -0.5)
            v = v_ref[:, sl]
            qh = q.astype(BF16)
            kh = k.astype(BF16)
            c_st = c_s[d, hd]
            n_st = n_s[i:i + 1, :]
            m_st = m_s[i:i + 1, :]
            b_col = b_cols[:, hd:hd + 1]
            b_row = b_rows[hd:hd + 1, :]
            ig_row = ig_rows[hd:hd + 1, :]
            dmat = jnp.where(keep, b_col - b_row + ig_row, NEG)
            inter = b_col + m_st
            m_t = jnp.maximum(inter, jnp.max(dmat, axis=-1, keepdims=True))
            s = _dot_nt(qh, kh) * jnp.exp(dmat - m_t)
            w_in = jnp.exp(inter - m_t)
            num = _dot(s.astype(BF16), v.astype(BF16)) + w_in * _dot_nt(qh, c_st.astype(BF16))
            den = jnp.sum(s, axis=-1, keepdims=True) + w_in * jnp.sum(q * n_st, axis=-1, keepdims=True)
            h_ref[:, sl] = num / jnp.maximum(jnp.abs(den), jnp.exp(-m_t))
            bl = b_last[hd:hd + 1, :]
            g_row = bl - b_row + ig_row
            g_col = bl - b_col + ig_cols[:, hd:hd + 1]
            m_new = jnp.maximum(bl + m_st, jnp.max(g_row, axis=-1, keepdims=True))
            wc = jnp.exp(bl + m_st - m_new)
            ws_col = jnp.exp(g_col - m_new)
            c_s[d, hd] = wc * c_st + _dot_tn((v * ws_col).astype(BF16), kh)
            n_s[i:i + 1, :] = wc * n_st + jnp.sum(k * ws_col, axis=0, keepdims=True)
            m_s[i:i + 1, :] = m_new

    @pl.when(c == n_chunks - 1)
    def _():
        cout_ref[0] = c_s[...]
        nout_ref[0] = n_s[...]
        mout_ref[0] = m_s[...]


def _mlstm(p, g, gt, c0, n0, m0, tril, triu, n_seq, seq_len):
    lc = MLSTM_CHUNK
    nc = seq_len // lc
    n_st = 2 * H_MLSTM

    def fwd(col):
        return lambda b, c: (b * nc + c, col)

    def bwd(col):
        return lambda b, c: (b * nc + nc - 1 - c, col)

    def side(mk):
        return [
            pl.BlockSpec((lc, C_W), mk(QC_BLK)),
            pl.BlockSpec((lc, C_W), mk(KC_BLK)),
            pl.BlockSpec((lc, C_W), mk(VC_BLK)),
            pl.BlockSpec((lc, N_GATE_COLS), mk(0)),
            pl.BlockSpec((N_GATE_COLS, lc), lambda b, c, mk=mk: mk(0)(b, c)[::-1]),
        ]

    state_specs = [
        pl.BlockSpec((1, 2, H_MLSTM, HEAD_DIM, HEAD_DIM), lambda b, c: (b, 0, 0, 0, 0)),
        pl.BlockSpec((1, n_st, HEAD_DIM), lambda b, c: (b, 0, 0)),
        pl.BlockSpec((1, n_st, 1), lambda b, c: (b, 0, 0)),
    ]
    tri_spec = pl.BlockSpec((lc, lc), lambda b, c: (0, 0))
    t = n_seq * seq_len
    return pl.pallas_call(
        functools.partial(_mlstm_kernel, n_chunks=nc),
        grid=(n_seq, nc),
        in_specs=side(fwd) + side(bwd) + state_specs + [tri_spec, tri_spec],
        out_specs=[
            pl.BlockSpec((lc, C_W), fwd(0)),
            pl.BlockSpec((lc, C_W), bwd(0)),
        ] + state_specs,
        out_shape=[
            jax.ShapeDtypeStruct((t, C_W), F32),
            jax.ShapeDtypeStruct((t, C_W), F32),
            jax.ShapeDtypeStruct((n_seq, 2, H_MLSTM, HEAD_DIM, HEAD_DIM), F32),
            jax.ShapeDtypeStruct((n_seq, n_st, HEAD_DIM), F32),
            jax.ShapeDtypeStruct((n_seq, n_st, 1), F32),
        ],
        scratch_shapes=[
            pltpu.VMEM((2, H_MLSTM, HEAD_DIM, HEAD_DIM), F32),
            pltpu.VMEM((n_st, HEAD_DIM), F32),
            pltpu.VMEM((n_st, 1), F32),
        ],
        compiler_params=_params(("arbitrary", "arbitrary")),
        name="mlstm",
    )(p, p, p, g, gt, p, p, p, g, gt, c0, n0, m0, tril, triu)


def _head_norm(y, g, ones_blk):
    ysq = y * y
    hi = ysq.astype(BF16)
    lo = (ysq - hi.astype(F32)).astype(BF16)
    ss = _dot(hi, ones_blk) + _dot(lo, ones_blk)
    return y * lax.rsqrt(ss * (1.0 / HEAD_DIM) + EPS) * g


def _merge_kernel(att_ref, four_ref, hf_ref, hb_ref, oc_ref, x_ref, mod_ref, gh_ref, wo_ref, g2_ref, wrt_ref,
                  ones_ref, xo_ref, h2_ref, afft_ref):
    gh = gh_ref[0]
    ya = _head_norm(att_ref[...], gh[:, :A_W], ones_ref[...])
    yf = _head_norm(four_ref[...], gh[:, A_W:A_W + B_W], ones_ref[:B_W, :B_W])
    ym = _head_norm(hf_ref[...] + hb_ref[...], gh[:, A_W + B_W:], ones_ref[:C_W, :C_W])
    ym = ym * jax.nn.sigmoid(oc_ref[...])
    out = (_dot(ya.astype(BF16), wo_ref[0, :A_W, :])
           + _dot(yf.astype(BF16), wo_ref[0, A_W:A_W + B_W, :])
           + _dot(ym.astype(BF16), wo_ref[0, A_W + B_W:, :]))
    x = x_ref[...] + mod_ref[0, 2:3, :] * out
    xo_ref[...] = x
    y2 = x * lax.rsqrt(jnp.mean(x * x, axis=-1, keepdims=True) + EPS) * g2_ref[0]
    h2 = (y2 * (1.0 + mod_ref[0, 4:5, :]) + mod_ref[0, 3:4, :]).astype(BF16)
    h2_ref[...] = h2
    logits = _dot_nt(wrt_ref[0], h2)
    e = jnp.exp(logits - jnp.max(logits, axis=0, keepdims=True))
    afft_ref[...] = e / jnp.sum(e, axis=0, keepdims=True)


def _merge(att, four, hf, hb, p, x2d, mod, layer, g_head, w_out_bf, g2, w_rt_bf, ones_blk, seq_len):
    t = x2d.shape[0]
    tm = 256
    tiles_per_seq = seq_len // tm
    row = lambda i: (i, 0)
    lay = lambda i: (layer, 0, 0)
    return pl.pallas_call(
        _merge_kernel,
        grid=(t // tm,),
        in_specs=[
            pl.BlockSpec((tm, A_W), row),
            pl.BlockSpec((tm, B_W), row),
            pl.BlockSpec((tm, C_W), row),
            pl.BlockSpec((tm, C_W), row),
            pl.BlockSpec((tm, C_W), lambda i: (i, OC_BLK)),
            pl.BlockSpec((tm, D_MODEL), row),
            pl.BlockSpec((1, 6, D_MODEL), lambda i: (i // tiles_per_seq, 0, 0)),
            pl.BlockSpec((1, 1, D_MODEL), lay),
            pl.BlockSpec((1, D_MODEL, D_MODEL), lay),
            pl.BlockSpec((1, 1, D_MODEL), lay),
            pl.BlockSpec((1, N_EXPERTS, D_MODEL), lay),
            pl.BlockSpec((A_W, A_W), lambda i: (0, 0)),
        ],
        out_specs=[
            pl.BlockSpec((tm, D_MODEL), row),
            pl.BlockSpec((tm, D_MODEL), row),
            pl.BlockSpec((N_EXPERTS, tm), lambda i: (0, i)),
        ],
        out_shape=[
            jax.ShapeDtypeStruct((t, D_MODEL), F32),
            jax.ShapeDtypeStruct((t, D_MODEL), BF16),
            jax.ShapeDtypeStruct((N_EXPERTS, t), F32),
        ],
        compiler_params=_params(("arbitrary",)),
        name="merge",
    )(att, four, hf, hb, p, x2d, mod, g_head, w_out_bf, g2, w_rt_bf, ones_blk)


BISECT_STEPS = 48


def _route_kernel(aff_ref, triu_ref, sp_ref, *, seq_len, cap):
    bits = aff_ref[...]

    def body(_, bounds):
        lo, hi = bounds
        mid = 0.5 * (lo + hi)
        ge = jnp.sum(jnp.where(bits >= mid, 1.0, 0.0), axis=1, keepdims=True) >= cap
        return jnp.where(ge, mid, lo), jnp.where(ge, hi, mid)

    _, hi = lax.fori_loop(0, BISECT_STEPS, body,
                          (jnp.zeros((N_EXPERTS, 1), F32), jnp.full((N_EXPERTS, 1), 2.0, F32)))
    thr = jnp.max(jnp.where(bits < hi, bits, -1.0), axis=1, keepdims=True)
    need = cap - jnp.sum(jnp.where(bits > thr, 1.0, 0.0), axis=1, keepdims=True)
    triu = triu_ref[...]
    eq_carry = jnp.zeros((N_EXPERTS, 1), F32)
    pos_carry = jnp.zeros((N_EXPERTS, 1), F32)
    for j in range(seq_len // 128):
        sl = slice(128 * j, 128 * (j + 1))
        blk = bits[:, sl]
        eq = blk == thr
        eq_f = jnp.where(eq, 1.0, 0.0)
        eq_inc = _dot(eq_f.astype(BF16), triu) + eq_carry
        sel = (blk > thr) | (eq & (eq_inc - eq_f < need))
        sel_f = jnp.where(sel, 1.0, 0.0)
        pos_inc = _dot(sel_f.astype(BF16), triu) + pos_carry
        sp_ref[:, sl] = jnp.where(sel, pos_inc - sel_f, -1.0).astype(jnp.int32)
        eq_carry = eq_inc[:, 127:128]
        pos_carry = pos_inc[:, 127:128]


def _route(afft, triu_bf, n_seq, seq_len, cap):
    return pl.pallas_call(
        functools.partial(_route_kernel, seq_len=seq_len, cap=cap),
        grid=(n_seq,),
        in_specs=[
            pl.BlockSpec((N_EXPERTS, seq_len), lambda s: (0, s)),
            pl.BlockSpec((128, 128), lambda s: (0, 0)),
        ],
        out_specs=pl.BlockSpec((N_EXPERTS, seq_len), lambda s: (0, s)),
        out_shape=jax.ShapeDtypeStruct((N_EXPERTS, n_seq * seq_len), jnp.int32),
        compiler_params=_params(("arbitrary",)),
        name="route",
    )(afft, triu_bf)


TOKEN_CHUNK = 1024


def _expert_row(ref, eb, ee, ei, t0, tc):
    if eb == N_EXPERTS:
        return ref[ee:ee + 1, t0:t0 + tc]
    return ref[pl.ds(ei * eb + ee, 1), t0:t0 + tc]


def _gather_kernel(sp_ref, aff_ref, h_ref, xs_ref, gs_ref, *, eb, seq_len, cap):
    ei = pl.program_id(1)
    tc = min(seq_len, TOKEN_CHUNK)
    slot = lax.broadcasted_iota(jnp.int32, (cap, tc), 0)
    for ee in range(eb):
        xs = jnp.zeros((cap, D_MODEL), F32)
        gs = jnp.zeros((cap, 1), F32)
        for t0 in range(0, seq_len, tc):
            onehot = slot == _expert_row(sp_ref, eb, ee, ei, t0, tc)
            xs = xs + _dot(jnp.where(onehot, 1.0, 0.0).astype(BF16), h_ref[t0:t0 + tc, :])
            gs = gs + jnp.sum(jnp.where(onehot, _expert_row(aff_ref, eb, ee, ei, t0, tc), 0.0),
                              axis=1, keepdims=True)
        xs_ref[ee] = xs.astype(BF16)
        gs_ref[ee] = gs


def _gather(sp, afft, h2, n_seq, seq_len, cap, eb):
    return pl.pallas_call(
        functools.partial(_gather_kernel, eb=eb, seq_len=seq_len, cap=cap),
        grid=(n_seq, N_EXPERTS // eb),
        in_specs=[
            pl.BlockSpec((N_EXPERTS, seq_len), lambda s, e: (0, s)),
            pl.BlockSpec((N_EXPERTS, seq_len), lambda s, e: (0, s)),
            pl.BlockSpec((seq_len, D_MODEL), lambda s, e: (s, 0)),
        ],
        out_specs=[
            pl.BlockSpec((eb, cap, D_MODEL), lambda s, e: (e, s, 0)),
            pl.BlockSpec((eb, cap, 1), lambda s, e: (e, s, 0)),
        ],
        out_shape=[
            jax.ShapeDtypeStruct((N_EXPERTS, n_seq * cap, D_MODEL), BF16),
            jax.ShapeDtypeStruct((N_EXPERTS, n_seq * cap, 1), F32),
        ],
        compiler_params=_params(("arbitrary", "arbitrary")),
        name="gather",
    )(sp, afft, h2)


def _expert_kernel(xc_ref, xl_ref, gc_ref, gl_ref, wg_ref, wu_ref, wd_ref, yc_ref, yl_ref, *, n_f):
    f = pl.program_id(1)

    @pl.when(f == 0)
    def _():
        yc_ref[...] = jnp.zeros_like(yc_ref)
        yl_ref[...] = jnp.zeros_like(yl_ref)

    wg = wg_ref[0, 0].astype(BF16)
    wu = wu_ref[0, 0].astype(BF16)
    wd = wd_ref[0, 0].astype(BF16)
    for x_ref, y_ref in ((xc_ref, yc_ref), (xl_ref, yl_ref)):
        x = x_ref[0]
        mid = (_silu(_dot(x, wg)) * _dot(x, wu)).astype(BF16)
        y_ref[0] += _dot(mid, wd)

    @pl.when(f == n_f - 1)
    def _():
        yc_ref[0] = yc_ref[0] * gc_ref[0]
        yl_ref[0] = yl_ref[0] * gl_ref[0]


def _experts(xs_c, xs_l, gs_c, gs_l, w_g, w_u, w_d, layer):
    rc, rl = xs_c.shape[1], xs_l.shape[1]
    tf = 512
    n_f = EXPERT_FF // tf
    return pl.pallas_call(
        functools.partial(_expert_kernel, n_f=n_f),
        grid=(N_EXPERTS, n_f),
        in_specs=[
            pl.BlockSpec((1, rc, D_MODEL), lambda e, f: (e, 0, 0)),
            pl.BlockSpec((1, rl, D_MODEL), lambda e, f: (e, 0, 0)),
            pl.BlockSpec((1, rc, 1), lambda e, f: (e, 0, 0)),
            pl.BlockSpec((1, rl, 1), lambda e, f: (e, 0, 0)),
            pl.BlockSpec((1, 1, D_MODEL, tf), lambda e, f: (layer, e, 0, f)),
            pl.BlockSpec((1, 1, D_MODEL, tf), lambda e, f: (layer, e, 0, f)),
            pl.BlockSpec((1, 1, tf, D_MODEL), lambda e, f: (layer, e, f, 0)),
        ],
        out_specs=[
            pl.BlockSpec((1, rc, D_MODEL), lambda e, f: (e, 0, 0)),
            pl.BlockSpec((1, rl, D_MODEL), lambda e, f: (e, 0, 0)),
        ],
        out_shape=[
            jax.ShapeDtypeStruct((N_EXPERTS, rc, D_MODEL), F32),
            jax.ShapeDtypeStruct((N_EXPERTS, rl, D_MODEL), F32),
        ],
        compiler_params=_params(("arbitrary", "arbitrary"), vmem_mib=56),
        name="experts",
    )(xs_c, xs_l, gs_c, gs_l, w_g, w_u, w_d)


def _scatter_kernel(sp_ref, y_ref, x_ref, mod_ref, o_ref, acc_ref, *, eb, seq_len, cap, n_e):
    ei = pl.program_id(2)

    @pl.when(ei == 0)
    def _():
        acc_ref[...] = jnp.zeros_like(acc_ref)

    tc = min(seq_len, TOKEN_CHUNK)
    slot = lax.broadcasted_iota(jnp.int32, (cap, tc), 0)
    for ee in range(eb):
        y = y_ref[ee]
        y_hi = y.astype(BF16)
        y_lo = (y - y_hi.astype(F32)).astype(BF16)
        for t0 in range(0, seq_len, tc):
            onehot = jnp.where(slot == _expert_row(sp_ref, eb, ee, ei, t0, tc), 1.0, 0.0).astype(BF16)
            acc_ref[t0:t0 + tc, :] += _dot_tn(onehot, y_hi) + _dot_tn(onehot, y_lo)

    @pl.when(ei == n_e - 1)
    def _():
        o_ref[...] = x_ref[...] + mod_ref[0, 5:6, :] * acc_ref[...]


def _scatter(sp, ys, x2d, mod, n_seq, seq_len, cap, eb, dw):
    n_e = N_EXPERTS // eb
    return pl.pallas_call(
        functools.partial(_scatter_kernel, eb=eb, seq_len=seq_len, cap=cap, n_e=n_e),
        grid=(n_seq, D_MODEL // dw, n_e),
        in_specs=[
            pl.BlockSpec((N_EXPERTS, seq_len), lambda s, d, e: (0, s)),
            pl.BlockSpec((eb, cap, dw), lambda s, d, e: (e, s, d)),
            pl.BlockSpec((seq_len, dw), lambda s, d, e: (s, d)),
            pl.BlockSpec((1, 6, dw), lambda s, d, e: (s, 0, d)),
        ],
        out_specs=pl.BlockSpec((seq_len, dw), lambda s, d, e: (s, d)),
        out_shape=jax.ShapeDtypeStruct((n_seq * seq_len, D_MODEL), F32),
        scratch_shapes=[pltpu.VMEM((seq_len, dw), F32)],
        compiler_params=_params(("arbitrary", "arbitrary", "arbitrary")),
        name="scatter",
    )(sp, ys, x2d, mod)


def _final_norm_kernel(x_ref, g_ref, o_ref):
    x = x_ref[...]
    o_ref[...] = x * lax.rsqrt(jnp.mean(x * x, axis=-1, keepdims=True) + EPS) * g_ref[...]


def _final_norm(x2d, g):
    t = x2d.shape[0]
    tm = 512
    return pl.pallas_call(
        _final_norm_kernel,
        grid=(t // tm,),
        in_specs=[pl.BlockSpec((tm, D_MODEL), lambda i: (i, 0)), pl.BlockSpec((1, D_MODEL), lambda i: (0, 0))],
        out_specs=pl.BlockSpec((tm, D_MODEL), lambda i: (i, 0)),
        out_shape=jax.ShapeDtypeStruct((t, D_MODEL), F32),
        compiler_params=_params(("arbitrary",)),
        name="final_norm",
    )(x2d, g)


def kernel(x_prompt, x_sample, c, cache_k, cache_v, state_C, state_n, state_m, c_ctx, w_ada, b_ada, g_norm1, g_norm2, w_in, b_gates, rpb, w_fourier, g_head, w_out, w_router, w_exp_gate, w_exp_up, w_exp_down, g_final):
    n_ctx, len_ctx, _ = x_prompt.shape
    n_lat, len_lat, _ = x_sample.shape
    past = cache_k.shape[2]
    cap_ctx = CAPACITY_FACTOR * len_ctx // N_EXPERTS
    cap_lat = CAPACITY_FACTOR * len_lat // N_EXPERTS
    n_st = 2 * H_MLSTM

    w_in_bf = w_in.astype(BF16)
    w_gt_bf = jnp.swapaxes(w_in[:, :, P_COLS:], 1, 2).astype(BF16)
    bg_row = b_gates.reshape(DEPTH, 1, N_GATE_COLS).astype(F32)
    bg_col = b_gates.reshape(DEPTH, N_GATE_COLS, 1).astype(F32)
    w_out_bf = w_out.astype(BF16)
    w_rt_bf = jnp.swapaxes(w_router, 1, 2).astype(BF16)
    g1 = g_norm1.reshape(DEPTH, 1, D_MODEL)
    g2 = g_norm2.reshape(DEPTH, 1, D_MODEL)
    gh = g_head.reshape(DEPTH, 1, D_MODEL)
    eye_g = jnp.eye(G_FOURIER, dtype=F32)
    wf_blk = jnp.einsum("lgcd,gh->lgchd", w_fourier, eye_g).reshape(DEPTH, B_W, B_W).astype(BF16)

    csc = _channel_dft()
    dft_ctx = _dft_mats(len_ctx)
    dft_lat = _dft_mats(len_lat)
    r = np.arange(MLSTM_CHUNK)
    tril = jnp.asarray(r[:, None] >= r[None, :], F32)
    triu = jnp.asarray(r[:, None] <= r[None, :], F32)
    triu_bf = triu.astype(BF16)
    hidx = np.arange(A_W) // HEAD_DIM
    ones_blk = jnp.asarray(hidx[:, None] == hidx[None, :], BF16)

    cvecs = jnp.concatenate([c_ctx[None, :], c, jnp.zeros((8 - 1 - n_lat, D_MODEL), F32)], axis=0)
    mod_all = _modulation(cvecs, w_ada, b_ada).reshape(DEPTH, 8, 6, D_MODEL)

    cache_k4 = cache_k.reshape(n_lat, DEPTH, past, A_W)
    cache_v4 = cache_v.reshape(n_lat, DEPTH, past, A_W)
    zero_c = jnp.zeros((n_ctx, 2, H_MLSTM, HEAD_DIM, HEAD_DIM), F32)
    zero_n = jnp.zeros((n_ctx, n_st, HEAD_DIM), F32)
    zero_m = jnp.zeros((n_ctx, n_st, 1), F32)

    xc = x_prompt.reshape(n_ctx * len_ctx, D_MODEL)
    xl = x_sample.reshape(n_lat * len_lat, D_MODEL)
    ks, vs, cs, ns, ms = [], [], [], [], []
    for l in range(DEPTH):
        mod_c = mod_all[l, 0:1]
        mod_l = mod_all[l, 1:1 + n_lat]
        bias_tab = _nbr_bias_table(rpb[l])

        pc, gc, gtc, abc = _inproj(xc, mod_c, l, g1, w_in_bf, w_gt_bf, bg_row, bg_col, csc, n_ctx * len_ctx)
        att_c = _ctx_attention(pc, n_ctx, len_ctx)
        four_c = _fourier(abc, dft_ctx[0], dft_ctx[1], wf_blk, l, n_ctx, len_ctx)
        hf_c, hb_c, c_new, n_new, m_new = _mlstm(pc, gc, gtc, zero_c, zero_n, zero_m, tril, triu, n_ctx, len_ctx)
        xc, h2c, affc = _merge(att_c, four_c, hf_c, hb_c, pc, xc, mod_c, l, gh, w_out_bf, g2, w_rt_bf, ones_blk,
                               n_ctx * len_ctx)
        ks.append(pc[:, A_W:2 * A_W].reshape(n_ctx, len_ctx, H_ATT, HEAD_DIM))
        vs.append(pc[:, 2 * A_W:3 * A_W].reshape(n_ctx, len_ctx, H_ATT, HEAD_DIM))
        cs.append(c_new)
        ns.append(n_new.reshape(n_ctx, 2, H_MLSTM, HEAD_DIM))
        ms.append(m_new.reshape(n_ctx, 2, H_MLSTM))

        pq, gq, gtq, abq = _inproj(xl, mod_l, l, g1, w_in_bf, w_gt_bf, bg_row, bg_col, csc, len_lat)
        att_l = _nbr_attention(pq, cache_k4, cache_v4, bias_tab, l, n_lat, len_lat)
        four_l = _fourier(abq, dft_lat[0], dft_lat[1], wf_blk, l, n_lat, len_lat)
        hf_l, hb_l, _, _, _ = _mlstm(pq, gq, gtq, state_C[:, l], state_n[:, l].reshape(n_lat, n_st, HEAD_DIM),
                                     state_m[:, l].reshape(n_lat, n_st, 1), tril, triu, n_lat, len_lat)
        xl, h2l, affl = _merge(att_l, four_l, hf_l, hb_l, pq, xl, mod_l, l, gh, w_out_bf, g2, w_rt_bf, ones_blk,
                               len_lat)

        sp_c = _route(affc, triu_bf, n_ctx, len_ctx, cap_ctx)
        sp_l = _route(affl, triu_bf, n_lat, len_lat, cap_lat)
        xs_c, gs_c = _gather(sp_c, affc, h2c, n_ctx, len_ctx, cap_ctx, N_EXPERTS)
        xs_l, gs_l = _gather(sp_l, affl, h2l, n_lat, len_lat, cap_lat, 1)
        ys_c, ys_l = _experts(xs_c, xs_l, gs_c, gs_l, w_exp_gate, w_exp_up, w_exp_down, l)
        xc = _scatter(sp_c, ys_c, xc, jnp.broadcast_to(mod_c, (n_ctx, 6, D_MODEL)), n_ctx, len_ctx, cap_ctx,
                      N_EXPERTS, D_MODEL)
        xl = _scatter(sp_l, ys_l, xl, mod_l, n_lat, len_lat, cap_lat, 1, D_MODEL // 4)

    gf = g_final.reshape(1, D_MODEL)
    y_prompt = _final_norm(xc, gf).reshape(n_ctx, len_ctx, D_MODEL)
    y_sample = _final_norm(xl, gf).reshape(n_lat, len_lat, D_MODEL)
    return (y_prompt, y_sample, jnp.stack(ks, axis=1), jnp.stack(vs, axis=1), jnp.stack(cs, axis=1),
            jnp.stack(ns, axis=1), jnp.stack(ms, axis=1))
```

```python
import functools

import numpy as np
import jax
import jax.numpy as jnp
from jax import lax
from jax.experimental import pallas as pl
from jax.experimental.pallas import tpu as pltpu

F32 = jnp.float32
BF16 = jnp.bfloat16

D_MODEL = 1024
DEPTH = 2
HEAD_DIM = 64
H_ATT = 8
G_FOURIER = 4
H_MLSTM = 4
A_W = H_ATT * HEAD_DIM
B_W = G_FOURIER * HEAD_DIM
C_W = H_MLSTM * HEAD_DIM
N_GATE_COLS = 16
P_COLS = 3 * A_W + B_W + 4 * C_W
IN_COLS = P_COLS + N_GATE_COLS
GRID_W = 64
WIN_R = 8
WIN_C = 16
MLSTM_CHUNK = 128
N_EXPERTS = 16
CAPACITY_FACTOR = 2
EXPERT_FF = 2 * D_MODEL
EPS = 1e-6
NEG = -1e30

UB_OFF = 3 * A_W
QC_BLK, KC_BLK, VC_BLK, OC_BLK = 7, 8, 9, 10

LANE = 128
N_SLAB = D_MODEL // LANE
ROW_PAD = 8

NT_DIMS = (((1,), (1,)), ((), ()))
TN_DIMS = (((0,), (0,)), ((), ()))
MIB = 1024 * 1024


def _dot(a, b, precision=None):
    return jnp.dot(a, b, preferred_element_type=F32, precision=precision)


def _dot_nt(a, b, precision=None):
    return lax.dot_general(a, b, NT_DIMS, preferred_element_type=F32, precision=precision)


def _dot_tn(a, b):
    return lax.dot_general(a, b, TN_DIMS, preferred_element_type=F32)


def _params(sem, vmem_mib=48):
    return pltpu.CompilerParams(dimension_semantics=sem, vmem_limit_bytes=vmem_mib * MIB)


def _silu(x):
    return x * jax.nn.sigmoid(x)


def _log_sigmoid(x):
    return jnp.minimum(x, 0.0) - jnp.log1p(jnp.exp(-jnp.abs(x)))


def _mod_kernel(c_ref, w_ref, b_ref, o_ref):
    s = _silu(c_ref[...]).astype(BF16)
    o_ref[0] = _dot(s, w_ref[0].astype(BF16)) + b_ref[0]


def _modulation(cvecs, w_ada, b_ada):
    depth = w_ada.shape[0]
    tn = 1024
    return pl.pallas_call(
        _mod_kernel,
        grid=(depth, 6 * D_MODEL // tn),
        in_specs=[
            pl.BlockSpec((8, D_MODEL), lambda l, j: (0, 0)),
            pl.BlockSpec((1, D_MODEL, tn), lambda l, j: (l, 0, j)),
            pl.BlockSpec((1, 1, tn), lambda l, j: (l, 0, j)),
        ],
        out_specs=pl.BlockSpec((1, 8, tn), lambda l, j: (l, 0, j)),
        out_shape=jax.ShapeDtypeStruct((depth, 8, 6 * D_MODEL), F32),
        compiler_params=_params(("arbitrary", "arbitrary")),
        name="modulation",
    )(cvecs, w_ada, b_ada.reshape(depth, 1, 6 * D_MODEL))


def _inproj_kernel(x_ref, mod_ref, g1_ref, w_ref, wgt_ref, bgr_ref, bgc_ref, csc_ref,
                   p_ref, g_ref, gt_ref, ab_ref):
    x = x_ref[...]
    y = x * lax.rsqrt(jnp.mean(x * x, axis=-1, keepdims=True) + EPS) * g1_ref[0]
    h = (y * (1.0 + mod_ref[0, 1:2, :]) + mod_ref[0, 0:1, :]).astype(BF16)
    for j in range(0, P_COLS, 256):
        pj = _dot(h, w_ref[0, :, j:j + 256])
        p_ref[:, j:j + 256] = pj
        if j == UB_OFF:
            ab_ref[...] = _dot(pj.astype(BF16), csc_ref[...]).astype(BF16)
    g_ref[...] = _dot(h, w_ref[0, :, P_COLS:IN_COLS]) + bgr_ref[0]
    gt_ref[...] = _dot_nt(wgt_ref[0], h) + bgc_ref[0]


def _inproj(x2d, mod, layer, g1, w_in_bf, w_gt_bf, bg_row, bg_col, csc, seq_len):
    t = x2d.shape[0]
    tm = 256
    tiles_per_seq = seq_len // tm
    return pl.pallas_call(
        _inproj_kernel,
        grid=(t // tm,),
        in_specs=[
            pl.BlockSpec((tm, D_MODEL), lambda i: (i, 0)),
            pl.BlockSpec((1, 6, D_MODEL), lambda i: (i // tiles_per_seq, 0, 0)),
            pl.BlockSpec((1, 1, D_MODEL), lambda i: (layer, 0, 0)),
            pl.BlockSpec((1, D_MODEL, IN_COLS), lambda i: (layer, 0, 0)),
            pl.BlockSpec((1, N_GATE_COLS, D_MODEL), lambda i: (layer, 0, 0)),
            pl.BlockSpec((1, 1, N_GATE_COLS), lambda i: (layer, 0, 0)),
            pl.BlockSpec((1, N_GATE_COLS, 1), lambda i: (layer, 0, 0)),
            pl.BlockSpec((B_W, 2 * B_W), lambda i: (0, 0)),
        ],
        out_specs=[
            pl.BlockSpec((tm, P_COLS), lambda i: (i, 0)),
            pl.BlockSpec((tm, N_GATE_COLS), lambda i: (i, 0)),
            pl.BlockSpec((N_GATE_COLS, tm), lambda i: (0, i)),
            pl.BlockSpec((tm, 2 * B_W), lambda i: (i, 0)),
        ],
        out_shape=[
            jax.ShapeDtypeStruct((t, P_COLS), F32),
            jax.ShapeDtypeStruct((t, N_GATE_COLS), F32),
            jax.ShapeDtypeStruct((N_GATE_COLS, t), F32),
            jax.ShapeDtypeStruct((t, 2 * B_W), BF16),
        ],
        compiler_params=_params(("arbitrary",)),
        name="inproj",
    )(x2d, mod, g1, w_in_bf, w_gt_bf, bg_row, bg_col, csc)


def _ctx_attn_kernel(q_ref, k_ref, v_ref, o_ref):
    scale = HEAD_DIM ** -0.5
    for h in range(H_ATT):
        sl = slice(HEAD_DIM * h, HEAD_DIM * (h + 1))
        q = q_ref[:, sl].astype(BF16)
        k = k_ref[:, sl].astype(BF16)
        v = v_ref[:, sl].astype(BF16)
        s = _dot_nt(q, k) * scale
        e = jnp.exp(s - jnp.max(s, axis=-1, keepdims=True))
        w = e * (1.0 / jnp.sum(e, axis=-1, keepdims=True))
        o_ref[:, sl] = _dot(w.astype(BF16), v)


def _ctx_attention(p, n_seq, seq_len):
    return pl.pallas_call(
        _ctx_attn_kernel,
        grid=(n_seq,),
        in_specs=[
            pl.BlockSpec((seq_len, A_W), lambda b: (b, 0)),
            pl.BlockSpec((seq_len, A_W), lambda b: (b, 1)),
            pl.BlockSpec((seq_len, A_W), lambda b: (b, 2)),
        ],
        out_specs=pl.BlockSpec((seq_len, A_W), lambda b: (b, 0)),
        out_shape=jax.ShapeDtypeStruct((n_seq * seq_len, A_W), F32),
        compiler_params=_params(("arbitrary",)),
        name="ctx_attention",
    )(p, p, p)


Q_ROWS = 8
K_ROWS = 16
KEY_BLK = 256


NO_ROW = 2 * WIN_R - 1


def _nbr_bias_tiles(rpb):
    cq = np.arange(GRID_W)[:, None]
    ck = np.arange(GRID_W)[None, :]
    cs = np.clip(cq - WIN_C // 2, 0, GRID_W - WIN_C)
    col_ok = (ck >= cs) & (ck < cs + WIN_C)
    dc = np.clip(ck - cq + WIN_C - 1, 0, 2 * WIN_C - 2)
    toe = jnp.where(jnp.asarray(col_ok), rpb[..., dc].astype(F32), NEG)
    toe = jnp.concatenate([toe, jnp.full(toe.shape[:2] + (1, GRID_W, GRID_W), NEG, F32)], axis=2)
    zero = jnp.zeros_like(toe)
    return jnp.stack([jnp.concatenate([toe, zero], axis=-1), jnp.concatenate([zero, toe], axis=-1)], axis=2)


def _nbr_attn_kernel(q_ref, k0_ref, k1_ref, k2_ref, k3_ref, v0_ref, v1_ref, v2_ref, v3_ref,
                     ck_ref, cv_ref, tab_ref, o_ref):
    scale = HEAD_DIM ** -0.5
    k_refs = (k0_ref, k1_ref, k2_ref, k3_ref)
    v_refs = (v0_ref, v1_ref, v2_ref, v3_ref)
    rb = pl.program_id(1)
    rows = GRID_W
    key_row0 = jnp.clip(Q_ROWS * rb - WIN_R // 2, 0, rows - K_ROWS)
    d = []
    for rq in range(Q_ROWS):
        r = Q_ROWS * rb + rq
        rs = jnp.clip(r - WIN_R // 2, 0, rows - WIN_R)
        d_row = []
        for rk in range(K_ROWS):
            rka = key_row0 + rk
            ok = (rka >= rs) & (rka < rs + WIN_R)
            d_row.append(jnp.where(ok, rka - r + WIN_R - 1, NO_ROW))
        d.append(d_row)
    rows_per_blk = KEY_BLK // GRID_W
    for hh in range(2):
        sl = slice(HEAD_DIM * hh, HEAD_DIM * (hh + 1))
        q = q_ref[:, sl].astype(BF16)
        s_loc = []
        for j in range(4):
            bias = jnp.concatenate([
                jnp.concatenate([
                    tab_ref[0, hh, 0, d[rq][rows_per_blk * j + 2 * p]]
                    + tab_ref[0, hh, 1, d[rq][rows_per_blk * j + 2 * p + 1]]
                    for p in range(rows_per_blk // 2)], axis=1)
                for rq in range(Q_ROWS)], axis=0)
            s_loc.append(_dot_nt(q, k_refs[j][:, sl].astype(BF16)) * scale + bias)
        s_ctx = _dot_nt(q, ck_ref[0, 0, :, sl].astype(BF16)) * scale
        m = jnp.max(s_ctx, axis=-1, keepdims=True)
        for s in s_loc:
            m = jnp.maximum(m, jnp.max(s, axis=-1, keepdims=True))
        e_ctx = jnp.exp(s_ctx - m)
        den = jnp.sum(e_ctx, axis=-1, keepdims=True)
        num = _dot(e_ctx.astype(BF16), cv_ref[0, 0, :, sl].astype(BF16))
        for j in range(4):
            e = jnp.exp(s_loc[j] - m)
            den = den + jnp.sum(e, axis=-1, keepdims=True)
            num = num + _dot(e.astype(BF16), v_refs[j][:, sl].astype(BF16))
        o_ref[:, sl] = num / den


def _nbr_attention(p, cache_k4, cache_v4, bias_tiles, layer, n_seq, seq_len):
    q_tok = Q_ROWS * GRID_W
    n_rb = seq_len // q_tok
    kb_per_seq = seq_len // KEY_BLK
    max_base = kb_per_seq - 4

    def kmap(j, col0):
        def f(hp, rb, b):
            base = jnp.clip(2 * rb - 1, 0, max_base)
            return (b * kb_per_seq + base + j, col0 + hp)
        return f

    past = cache_k4.shape[2]
    in_specs = [pl.BlockSpec((q_tok, 128), lambda hp, rb, b: (b * n_rb + rb, hp))]
    in_specs += [pl.BlockSpec((KEY_BLK, 128), kmap(j, A_W // 128)) for j in range(4)]
    in_specs += [pl.BlockSpec((KEY_BLK, 128), kmap(j, 2 * A_W // 128)) for j in range(4)]
    in_specs += [
        pl.BlockSpec((1, 1, past, 128), lambda hp, rb, b: (b, layer, 0, hp)),
        pl.BlockSpec((1, 1, past, 128), lambda hp, rb, b: (b, layer, 0, hp)),
        pl.BlockSpec((1, 2, 2, 2 * WIN_R, GRID_W, 2 * GRID_W), lambda hp, rb, b: (layer, hp, 0, 0, 0, 0)),
    ]
    return pl.pallas_call(
        _nbr_attn_kernel,
        grid=(H_ATT // 2, n_rb, n_seq),
        in_specs=in_specs,
        out_specs=pl.BlockSpec((q_tok, 128), lambda hp, rb, b: (b * n_rb + rb, hp)),
        out_shape=jax.ShapeDtypeStruct((n_seq * seq_len, A_W), F32),
        compiler_params=_params(("arbitrary", "arbitrary", "arbitrary")),
        name="nbr_attention",
    )(p, p, p, p, p, p, p, p, p, cache_k4, cache_v4, bias_tiles)


def _dft_mats(n):
    idx = jnp.arange(n, dtype=jnp.int32)
    ang = ((idx[:, None] * idx[None, :]) % n).astype(F32) * (2.0 * np.pi / n)
    return jnp.cos(ang).astype(BF16), jnp.sin(ang).astype(BF16)


def _channel_dft():
    c = np.arange(HEAD_DIM)
    ang = 2.0 * np.pi * ((c[:, None] * c[None, :]) % HEAD_DIM) / HEAD_DIM
    eye = np.eye(G_FOURIER)
    mats = np.concatenate([np.kron(eye, np.cos(ang)), np.kron(eye, np.sin(ang))], axis=1)
    return jnp.asarray(mats, F32).astype(BF16)


def _fourier_kernel(c_ref, s_ref, ab_ref, wf_ref, o_ref, acc_ref, *, scale, n_k):
    k = pl.program_id(2)

    @pl.when(k == 0)
    def _():
        acc_ref[...] = jnp.zeros_like(acc_ref)

    acc_ref[...] += _dot(c_ref[...], ab_ref[:, :B_W]) - _dot(s_ref[...], ab_ref[:, B_W:])

    @pl.when(k == n_k - 1)
    def _():
        z = (acc_ref[...] * scale).astype(BF16)
        o_ref[...] = _dot(z, wf_ref[0])


def _fourier(ab, cmat, smat, wf_blk, layer, n_seq, seq_len):
    ti = min(seq_len, 512)
    tk = min(seq_len, 1024)
    n_i, n_k = seq_len // ti, seq_len // tk
    scale = float((seq_len * HEAD_DIM) ** -0.5)
    return pl.pallas_call(
        functools.partial(_fourier_kernel, scale=scale, n_k=n_k),
        grid=(n_seq, n_i, n_k),
        in_specs=[
            pl.BlockSpec((ti, tk), lambda s, i, k: (i, k)),
            pl.BlockSpec((ti, tk), lambda s, i, k: (i, k)),
            pl.BlockSpec((tk, 2 * B_W), lambda s, i, k: (s * n_k + k, 0)),
            pl.BlockSpec((1, B_W, B_W), lambda s, i, k: (layer, 0, 0)),
        ],
        out_specs=pl.BlockSpec((ti, B_W), lambda s, i, k: (s * n_i + i, 0)),
        out_shape=jax.ShapeDtypeStruct((n_seq * seq_len, B_W), F32),
        scratch_shapes=[pltpu.VMEM((ti, B_W), F32)],
        compiler_params=_params(("arbitrary", "arbitrary", "arbitrary")),
        name="fourier",
    )(cmat, smat, ab, wf_blk)


FS_GROUP = 8


def _twiddles(side, n):
    k1 = jnp.arange(side, dtype=jnp.int32)[:, None]
    n2 = jnp.arange(side, dtype=jnp.int32)[None, :]
    ang = (k1 * n2).astype(F32) * (2.0 * np.pi / n)
    wide = lambda t: jnp.broadcast_to(t[:, :, None], (side, side, B_W)).reshape(side, side * B_W)
    return wide(jnp.cos(ang)), wide(jnp.sin(ang))


def _fourier_stage1_kernel(ab_ref, c_ref, s_ref, tc_ref, ts_ref, o_ref):
    ab = ab_ref[...]
    m1 = _dot(c_ref[...], ab)
    m2 = _dot(s_ref[...], ab)
    for t in range(FS_GROUP):
        a0 = 2 * B_W * t
        yr = m1[:, a0:a0 + B_W] - m2[:, a0 + B_W:a0 + 2 * B_W]
        yi = -(m1[:, a0 + B_W:a0 + 2 * B_W] + m2[:, a0:a0 + B_W])
        ct = tc_ref[:, B_W * t:B_W * (t + 1)]
        st = ts_ref[:, B_W * t:B_W * (t + 1)]
        o_ref[:, a0:a0 + B_W] = (yr * ct + yi * st).astype(BF16)
        o_ref[:, a0 + B_W:a0 + 2 * B_W] = (yi * ct - yr * st).astype(BF16)


def _fourier_stage2_kernel(y_ref, c_ref, s_ref, wf_ref, o_ref, *, side, scale):
    zs = []
    for j in range(FS_GROUP):
        rows = slice(side * j, side * (j + 1))
        zs.append(_dot(c_ref[...], y_ref[rows, :B_W]) + _dot(s_ref[...], y_ref[rows, B_W:]))
    z = (jnp.concatenate(zs, axis=0) * scale).astype(BF16)
    o = _dot(z, wf_ref[0])
    for j in range(FS_GROUP):
        o_ref[:, B_W * j:B_W * (j + 1)] = o[side * j:side * (j + 1), :]


def _fourier_grid(ab, cmat, smat, tw_cos, tw_sin, wf_blk, layer, n_seq, side):
    seq_len = side * side
    n_g = side // FS_GROUP
    scale = float((seq_len * HEAD_DIM) ** -0.5)
    small = pl.BlockSpec((side, side), lambda s, j: (0, 0))
    y = pl.pallas_call(
        _fourier_stage1_kernel,
        grid=(n_seq, n_g),
        in_specs=[
            pl.BlockSpec((side, 2 * B_W * FS_GROUP), lambda s, j: (s, j)),
            small,
            small,
            pl.BlockSpec((side, B_W * FS_GROUP), lambda s, j: (0, j)),
            pl.BlockSpec((side, B_W * FS_GROUP), lambda s, j: (0, j)),
        ],
        out_specs=pl.BlockSpec((side, 2 * B_W * FS_GROUP), lambda s, j: (s, j)),
        out_shape=jax.ShapeDtypeStruct((n_seq * side, side * 2 * B_W), BF16),
        compiler_params=_params(("arbitrary", "arbitrary")),
        name="fourier_stage1",
    )(ab.reshape(n_seq * side, side * 2 * B_W), cmat, smat, tw_cos, tw_sin)
    out = pl.pallas_call(
        functools.partial(_fourier_stage2_kernel, side=side, scale=scale),
        grid=(n_seq, n_g),
        in_specs=[
            pl.BlockSpec((side * FS_GROUP, 2 * B_W), lambda s, j: (s * n_g + j, 0)),
            small,
            small,
            pl.BlockSpec((1, B_W, B_W), lambda s, j: (layer, 0, 0)),
        ],
        out_specs=pl.BlockSpec((side, B_W * FS_GROUP), lambda s, j: (s, j)),
        out_shape=jax.ShapeDtypeStruct((n_seq * side, side * B_W), F32),
        compiler_params=_params(("arbitrary", "arbitrary")),
        name="fourier_stage2",
    )(y.reshape(n_seq * seq_len, 2 * B_W), cmat, smat, wf_blk)
    return out.reshape(n_seq * seq_len, B_W)


def _mlstm_kernel(qf_ref, kf_ref, vf_ref, gf_ref, gtf_ref, qb_ref, kb_ref, vb_ref, gb_ref, gtb_ref,
                  c0_ref, n0_ref, m0_ref, tril_ref, triu_ref,
                  hf_ref, hb_ref, cout_ref, nout_ref, mout_ref, c_s, n_s, m_s, *, n_chunks):
    c = pl.program_id(1)
    hi = lax.Precision.HIGHEST

    @pl.when(c == 0)
    def _():
        c_s[...] = c0_ref[0]
        n_s[...] = n0_ref[0]
        m_s[...] = m0_ref[0]

    dirs = (
        (qf_ref, kf_ref, vf_ref, gf_ref, gtf_ref, hf_ref, tril_ref),
        (qb_ref, kb_ref, vb_ref, gb_ref, gtb_ref, hb_ref, triu_ref),
    )
    for d, (q_ref, k_ref, v_ref, g_ref, gt_ref, h_ref, mask_ref) in enumerate(dirs):
        mask = mask_ref[...]
        keep = mask > 0.5
        go = 2 * H_MLSTM * d
        ig_cols = g_ref[:, go:go + H_MLSTM]
        lf_cols = _log_sigmoid(g_ref[:, go + H_MLSTM:go + 2 * H_MLSTM])
        ig_rows = gt_ref[go:go + H_MLSTM, :]
        lf_rows = _log_sigmoid(gt_ref[go + H_MLSTM:go + 2 * H_MLSTM, :])
        b_cols = _dot(mask, lf_cols, precision=hi)
        b_rows = _dot_nt(lf_rows, mask, precision=hi)
        b_last = jnp.sum(lf_rows, axis=1, keepdims=True)
        for hd in range(H_MLSTM):
            i = H_MLSTM * d + hd
            sl = slice(HEAD_DIM * hd, HEAD_DIM * (hd + 1))
            q = q_ref[:, sl]
            k = k_ref[:, sl] * (HEAD_DIM ** -0.5)
            v = v_ref[:, sl]
            qh = q.astype(BF16)
            kh = k.astype(BF16)
            c_st = c_s[d, hd]
            n_st = n_s[i:i + 1, :]
            m_st = m_s[i:i + 1, :]
            b_col = b_cols[:, hd:hd + 1]
            b_row = b_rows[hd:hd + 1, :]
            ig_row = ig_rows[hd:hd + 1, :]
            dmat = jnp.where(keep, b_col - b_row + ig_row, NEG)
            inter = b_col + m_st
            m_t = jnp.maximum(inter, jnp.max(dmat, axis=-1, keepdims=True))
            s = _dot_nt(qh, kh) * jnp.exp(dmat - m_t)
            w_in = jnp.exp(inter - m_t)
            num = _dot(s.astype(BF16), v.astype(BF16)) + w_in * _dot_nt(qh, c_st.astype(BF16))
            den = jnp.sum(s, axis=-1, keepdims=True) + w_in * jnp.sum(q * n_st, axis=-1, keepdims=True)
            h_ref[:, sl] = num / jnp.maximum(jnp.abs(den), jnp.exp(-m_t))
            bl = b_last[hd:hd + 1, :]
            g_row = bl - b_row + ig_row
            g_col = bl - b_col + ig_cols[:, hd:hd + 1]
            m_new = jnp.maximum(bl + m_st, jnp.max(g_row, axis=-1, keepdims=True))
            wc = jnp.exp(bl + m_st - m_new)
            ws_col = jnp.exp(g_col - m_new)
            c_s[d, hd] = wc * c_st + _dot_tn((v * ws_col).astype(BF16), kh)
            n_s[i:i + 1, :] = wc * n_st + jnp.sum(k * ws_col, axis=0, keepdims=True)
            m_s[i:i + 1, :] = m_new

    @pl.when(c == n_chunks - 1)
    def _():
        cout_ref[0] = c_s[...]
        nout_ref[0] = n_s[...]
        mout_ref[0] = m_s[...]


def _mlstm(p, g, gt, c0, n0, m0, tril, triu, n_seq, seq_len):
    lc = MLSTM_CHUNK
    nc = seq_len // lc
    n_st = 2 * H_MLSTM

    def fwd(col):
        return lambda b, c: (b * nc + c, col)

    def bwd(col):
        return lambda b, c: (b * nc + nc - 1 - c, col)

    def side(mk):
        return [
            pl.BlockSpec((lc, C_W), mk(QC_BLK)),
            pl.BlockSpec((lc, C_W), mk(KC_BLK)),
            pl.BlockSpec((lc, C_W), mk(VC_BLK)),
            pl.BlockSpec((lc, N_GATE_COLS), mk(0)),
            pl.BlockSpec((N_GATE_COLS, lc), lambda b, c, mk=mk: mk(0)(b, c)[::-1]),
        ]

    state_specs = [
        pl.BlockSpec((1, 2, H_MLSTM, HEAD_DIM, HEAD_DIM), lambda b, c: (b, 0, 0, 0, 0)),
        pl.BlockSpec((1, n_st, HEAD_DIM), lambda b, c: (b, 0, 0)),
        pl.BlockSpec((1, n_st, 1), lambda b, c: (b, 0, 0)),
    ]
    tri_spec = pl.BlockSpec((lc, lc), lambda b, c: (0, 0))
    t = n_seq * seq_len
    return pl.pallas_call(
        functools.partial(_mlstm_kernel, n_chunks=nc),
        grid=(n_seq, nc),
        in_specs=side(fwd) + side(bwd) + state_specs + [tri_spec, tri_spec],
        out_specs=[
            pl.BlockSpec((lc, C_W), fwd(0)),
            pl.BlockSpec((lc, C_W), bwd(0)),
        ] + state_specs,
        out_shape=[
            jax.ShapeDtypeStruct((t, C_W), F32),
            jax.ShapeDtypeStruct((t, C_W), F32),
            jax.ShapeDtypeStruct((n_seq, 2, H_MLSTM, HEAD_DIM, HEAD_DIM), F32),
            jax.ShapeDtypeStruct((n_seq, n_st, HEAD_DIM), F32),
            jax.ShapeDtypeStruct((n_seq, n_st, 1), F32),
        ],
        scratch_shapes=[
            pltpu.VMEM((2, H_MLSTM, HEAD_DIM, HEAD_DIM), F32),
            pltpu.VMEM((n_st, HEAD_DIM), F32),
            pltpu.VMEM((n_st, 1), F32),
        ],
        compiler_params=_params(("arbitrary", "arbitrary")),
        name="mlstm",
    )(p, p, p, g, gt, p, p, p, g, gt, c0, n0, m0, tril, triu)


def _head_norm(y, g, ones_blk):
    ysq = y * y
    hi = ysq.astype(BF16)
    lo = (ysq - hi.astype(F32)).astype(BF16)
    ss = _dot(hi, ones_blk) + _dot(lo, ones_blk)
    return y * lax.rsqrt(ss * (1.0 / HEAD_DIM) + EPS) * g


def _merge_kernel(att_ref, four_ref, hf_ref, hb_ref, oc_ref, x_ref, mod_ref, gh_ref, wo_ref, g2_ref, wrt_ref,
                  ones_ref, xo_ref, h2_ref, afft_ref):
    gh = gh_ref[0]
    ya = _head_norm(att_ref[...], gh[:, :A_W], ones_ref[...])
    yf = _head_norm(four_ref[...], gh[:, A_W:A_W + B_W], ones_ref[:B_W, :B_W])
    ym = _head_norm(hf_ref[...] + hb_ref[...], gh[:, A_W + B_W:], ones_ref[:C_W, :C_W])
    ym = ym * jax.nn.sigmoid(oc_ref[...])
    out = (_dot(ya.astype(BF16), wo_ref[0, :A_W, :])
           + _dot(yf.astype(BF16), wo_ref[0, A_W:A_W + B_W, :])
           + _dot(ym.astype(BF16), wo_ref[0, A_W + B_W:, :]))
    x = x_ref[...] + mod_ref[0, 2:3, :] * out
    xo_ref[...] = x
    y2 = x * lax.rsqrt(jnp.mean(x * x, axis=-1, keepdims=True) + EPS) * g2_ref[0]
    h2 = (y2 * (1.0 + mod_ref[0, 4:5, :]) + mod_ref[0, 3:4, :]).astype(BF16)
    h2_wide = h2.astype(F32)
    for s in range(N_SLAB):
        h2_ref[0, s] = h2_wide[:, LANE * s:LANE * (s + 1)]
    logits = _dot_nt(wrt_ref[0], h2)
    e = jnp.exp(logits - jnp.max(logits, axis=0, keepdims=True))
    afft_ref[...] = e / jnp.sum(e, axis=0, keepdims=True)


def _merge(att, four, hf, hb, p, x2d, mod, layer, g_head, w_out_bf, g2, w_rt_bf, ones_blk, n_seq, seq_len, mod_seq_len):
    t = x2d.shape[0]
    tm = 256
    tiles_per_mod = mod_seq_len // tm
    tiles_per_seq = seq_len // tm
    row = lambda i: (i, 0)
    lay = lambda i: (layer, 0, 0)
    return pl.pallas_call(
        _merge_kernel,
        grid=(t // tm,),
        in_specs=[
            pl.BlockSpec((tm, A_W), row),
            pl.BlockSpec((tm, B_W), row),
            pl.BlockSpec((tm, C_W), row),
            pl.BlockSpec((tm, C_W), row),
            pl.BlockSpec((tm, C_W), lambda i: (i, OC_BLK)),
            pl.BlockSpec((tm, D_MODEL), row),
            pl.BlockSpec((1, 6, D_MODEL), lambda i: (i // tiles_per_mod, 0, 0)),
            pl.BlockSpec((1, 1, D_MODEL), lay),
            pl.BlockSpec((1, D_MODEL, D_MODEL), lay),
            pl.BlockSpec((1, 1, D_MODEL), lay),
            pl.BlockSpec((1, N_EXPERTS, D_MODEL), lay),
            pl.BlockSpec((A_W, A_W), lambda i: (0, 0)),
        ],
        out_specs=[
            pl.BlockSpec((tm, D_MODEL), row),
            pl.BlockSpec((1, N_SLAB, tm, LANE), lambda i: (i // tiles_per_seq, 0, i % tiles_per_seq, 0)),
            pl.BlockSpec((N_EXPERTS, tm), lambda i: (0, i)),
        ],
        out_shape=[
            jax.ShapeDtypeStruct((t, D_MODEL), F32),
            jax.ShapeDtypeStruct((n_seq, N_SLAB, seq_len + ROW_PAD, LANE), F32),
            jax.ShapeDtypeStruct((N_EXPERTS, t), F32),
        ],
        compiler_params=_params(("arbitrary",)),
        name="merge",
    )(att, four, hf, hb, p, x2d, mod, g_head, w_out_bf, g2, w_rt_bf, ones_blk)


BISECT_STEPS = 48
TOKEN_CHUNK = 1024


def _route_kernel(aff_ref, triu_ref, idx_ref, gs_ref, sp_ref, *, seq_len, cap):
    bits = aff_ref[...]

    def body(_, bounds):
        lo, hi = bounds
        mid = 0.5 * (lo + hi)
        ge = jnp.sum(jnp.where(bits >= mid, 1.0, 0.0), axis=1, keepdims=True) >= cap
        return jnp.where(ge, mid, lo), jnp.where(ge, hi, mid)

    _, hi = lax.fori_loop(0, BISECT_STEPS, body,
                          (jnp.zeros((N_EXPERTS, 1), F32), jnp.full((N_EXPERTS, 1), 2.0, F32)))
    thr = jnp.max(jnp.where(bits < hi, bits, -1.0), axis=1, keepdims=True)
    need = cap - jnp.sum(jnp.where(bits > thr, 1.0, 0.0), axis=1, keepdims=True)
    triu = triu_ref[...]
    eq_carry = jnp.zeros((N_EXPERTS, 1), F32)
    pos_carry = jnp.zeros((N_EXPERTS, 1), F32)
    for j in range(seq_len // 128):
        sl = slice(128 * j, 128 * (j + 1))
        blk = bits[:, sl]
        eq = blk == thr
        eq_f = jnp.where(eq, 1.0, 0.0)
        eq_inc = _dot(eq_f.astype(BF16), triu) + eq_carry
        sel = (blk > thr) | (eq & (eq_inc - eq_f < need))
        sel_f = jnp.where(sel, 1.0, 0.0)
        pos_inc = _dot(sel_f.astype(BF16), triu) + pos_carry
        sp_ref[:, sl] = jnp.where(sel, pos_inc - sel_f, -1.0).astype(jnp.int32)
        eq_carry = eq_inc[:, 127:128]
        pos_carry = pos_inc[:, 127:128]

    tc = min(seq_len, TOKEN_CHUNK)
    slot = lax.broadcasted_iota(jnp.int32, (cap, tc), 0)
    tok = lax.broadcasted_iota(jnp.int32, (1, tc), 1).astype(F32)

    def per_expert(e, carry):
        idx = jnp.zeros((cap, 1), F32)
        gs = jnp.zeros((cap, 1), F32)
        for t0 in range(0, seq_len, tc):
            onehot = slot == sp_ref[pl.ds(e, 1), t0:t0 + tc]
            idx = idx + jnp.sum(jnp.where(onehot, tok + float(t0), 0.0), axis=1, keepdims=True)
            gs = gs + jnp.sum(jnp.where(onehot, aff_ref[pl.ds(e, 1), t0:t0 + tc], 0.0), axis=1, keepdims=True)
        idx_ref[e] = idx.astype(jnp.int32)
        gs_ref[e] = gs
        return carry

    lax.fori_loop(0, N_EXPERTS, per_expert, 0)


def _route(afft, triu_bf, n_seq, seq_len, cap):
    idx, gs = pl.pallas_call(
        functools.partial(_route_kernel, seq_len=seq_len, cap=cap),
        grid=(n_seq,),
        in_specs=[
            pl.BlockSpec((N_EXPERTS, seq_len), lambda s: (0, s)),
            pl.BlockSpec((128, 128), lambda s: (0, 0)),
        ],
        out_specs=[
            pl.BlockSpec((N_EXPERTS, cap, 1), lambda s: (s, 0, 0)),
            pl.BlockSpec((N_EXPERTS, cap, 1), lambda s: (0, s, 0)),
        ],
        out_shape=[
            jax.ShapeDtypeStruct((n_seq * N_EXPERTS, cap, 1), jnp.int32),
            jax.ShapeDtypeStruct((N_EXPERTS, n_seq * cap, 1), F32),
        ],
        scratch_shapes=[pltpu.VMEM((N_EXPERTS, seq_len), jnp.int32)],
        compiler_params=_params(("arbitrary",)),
        name="route",
    )(afft, triu_bf)
    return idx.reshape(n_seq * N_EXPERTS, cap), gs


ROW_COPIES = 8


def _gather_kernel(idx_ref, src_ref, xs_ref, tile_ref, *, eb, pitch, cap):
    ei = pl.program_id(1)
    tp = cap + ROW_PAD

    def per_expert(ee, carry):
        e = ei * eb + ee

        def rows(g, c):
            for u in range(ROW_COPIES):
                slot = g * ROW_COPIES + u
                tile_ref[pl.ds(slot, N_SLAB, stride=tp), :] = src_ref[pl.ds(idx_ref[e, slot], N_SLAB, stride=pitch), :]
            return c

        lax.fori_loop(0, cap // ROW_COPIES, rows, 0)
        for s in range(N_SLAB):
            xs_ref[ee, :, LANE * s:LANE * (s + 1)] = tile_ref[tp * s:tp * s + cap, :].astype(BF16)
        return carry

    lax.fori_loop(0, eb, per_expert, 0)


def _gather(idx, h2_slabs, n_seq, seq_len, cap, eb):
    pitch = seq_len + ROW_PAD
    return pl.pallas_call(
        functools.partial(_gather_kernel, eb=eb, pitch=pitch, cap=cap),
        grid=(n_seq, N_EXPERTS // eb),
        in_specs=[
            pl.BlockSpec((N_EXPERTS, cap), lambda s, e: (s, 0), memory_space=pltpu.SMEM),
            pl.BlockSpec((N_SLAB * pitch, LANE), lambda s, e: (s, 0)),
        ],
        out_specs=pl.BlockSpec((eb, cap, D_MODEL), lambda s, e: (e, s, 0)),
        out_shape=jax.ShapeDtypeStruct((N_EXPERTS, n_seq * cap, D_MODEL), BF16),
        scratch_shapes=[pltpu.VMEM((N_SLAB * (cap + ROW_PAD), LANE), F32)],
        compiler_params=_params(("arbitrary", "arbitrary"), vmem_mib=56),
        name="gather",
    )(idx, h2_slabs.reshape(n_seq * N_SLAB * pitch, LANE))


def _expert_kernel(xc_ref, xl_ref, gc_ref, gl_ref, wg_ref, wu_ref, wd_ref, yc_ref, yl_ref, *, n_f):
    f = pl.program_id(1)

    @pl.when(f == 0)
    def _():
        yc_ref[...] = jnp.zeros_like(yc_ref)
        yl_ref[...] = jnp.zeros_like(yl_ref)

    wg = wg_ref[0, 0].astype(BF16)
    wu = wu_ref[0, 0].astype(BF16)
    wd = wd_ref[0, 0].astype(BF16)
    for x_ref, y_ref in ((xc_ref, yc_ref), (xl_ref, yl_ref)):
        x = x_ref[0]
        mid = (_silu(_dot(x, wg)) * _dot(x, wu)).astype(BF16)
        y_ref[0] += _dot(mid, wd)

    @pl.when(f == n_f - 1)
    def _():
        yc_ref[0] = yc_ref[0] * gc_ref[0]
        yl_ref[0] = yl_ref[0] * gl_ref[0]


def _experts(xs_c, xs_l, gs_c, gs_l, w_g, w_u, w_d, layer):
    rc, rl = xs_c.shape[1], xs_l.shape[1]
    tf = 512
    n_f = EXPERT_FF // tf
    return pl.pallas_call(
        functools.partial(_expert_kernel, n_f=n_f),
        grid=(N_EXPERTS, n_f),
        in_specs=[
            pl.BlockSpec((1, rc, D_MODEL), lambda e, f: (e, 0, 0)),
            pl.BlockSpec((1, rl, D_MODEL), lambda e, f: (e, 0, 0)),
            pl.BlockSpec((1, rc, 1), lambda e, f: (e, 0, 0)),
            pl.BlockSpec((1, rl, 1), lambda e, f: (e, 0, 0)),
            pl.BlockSpec((1, 1, D_MODEL, tf), lambda e, f: (layer, e, 0, f)),
            pl.BlockSpec((1, 1, D_MODEL, tf), lambda e, f: (layer, e, 0, f)),
            pl.BlockSpec((1, 1, tf, D_MODEL), lambda e, f: (layer, e, f, 0)),
        ],
        out_specs=[
            pl.BlockSpec((1, rc, D_MODEL), lambda e, f: (e, 0, 0)),
            pl.BlockSpec((1, rl, D_MODEL), lambda e, f: (e, 0, 0)),
        ],
        out_shape=[
            jax.ShapeDtypeStruct((N_EXPERTS, rc, D_MODEL), F32),
            jax.ShapeDtypeStruct((N_EXPERTS, rl, D_MODEL), F32),
        ],
        compiler_params=_params(("arbitrary", "arbitrary"), vmem_mib=56),
        name="experts",
    )(xs_c, xs_l, gs_c, gs_l, w_g, w_u, w_d)


def _scatter_kernel(idx_ref, y_ref, x_ref, mod_ref, o_ref, acc_ref, tile_ref, *, eb, n_e, pitch, cap, tm):
    step = pl.program_id(1)
    tp = cap + ROW_PAD

    @pl.when(step == 0)
    def _():
        acc_ref[...] = jnp.zeros_like(acc_ref)

    @pl.when(step < n_e)
    def _():
        def per_expert(ee, carry):
            e = step * eb + ee
            for s in range(N_SLAB):
                tile_ref[tp * s:tp * s + cap, :] = y_ref[ee, :, LANE * s:LANE * (s + 1)]

            def rows(g, c):
                dst = [idx_ref[e, g * ROW_COPIES + u] for u in range(ROW_COPIES)]
                new = [acc_ref[pl.ds(dst[u], N_SLAB, stride=pitch), :]
                       + tile_ref[pl.ds(g * ROW_COPIES + u, N_SLAB, stride=tp), :] for u in range(ROW_COPIES)]
                for u in range(ROW_COPIES):
                    acc_ref[pl.ds(dst[u], N_SLAB, stride=pitch), :] = new[u]
                return c

            lax.fori_loop(0, cap // ROW_COPIES, rows, 0)
            return carry

        lax.fori_loop(0, eb, per_expert, 0)

    @pl.when(step >= n_e)
    def _():
        base = (step - n_e) * tm
        for s in range(N_SLAB):
            cols = slice(LANE * s, LANE * (s + 1))
            rows_s = pl.ds(pl.multiple_of(pitch * s + base, 8), tm)
            o_ref[:, cols] = x_ref[:, cols] + mod_ref[0, 5:6, cols] * acc_ref[rows_s, :]


def _scatter(idx, ys, x2d, mod, n_seq, seq_len, cap, eb, tm):
    n_e = N_EXPERTS // eb
    n_out = seq_len // tm
    pitch = seq_len + ROW_PAD
    out_blk = lambda s, j: (s * n_out + jnp.maximum(j - n_e, 0), 0)
    return pl.pallas_call(
        functools.partial(_scatter_kernel, eb=eb, n_e=n_e, pitch=pitch, cap=cap, tm=tm),
        grid=(n_seq, n_e + n_out),
        in_specs=[
            pl.BlockSpec((N_EXPERTS, cap), lambda s, j: (s, 0), memory_space=pltpu.SMEM),
            pl.BlockSpec((eb, cap, D_MODEL), lambda s, j: (jnp.minimum(j, n_e - 1), s, 0)),
            pl.BlockSpec((tm, D_MODEL), out_blk),
            pl.BlockSpec((1, 6, D_MODEL), lambda s, j: (s, 0, 0)),
        ],
        out_specs=pl.BlockSpec((tm, D_MODEL), out_blk),
        out_shape=jax.ShapeDtypeStruct((n_seq * seq_len, D_MODEL), F32),
        scratch_shapes=[
            pltpu.VMEM((N_SLAB * pitch, LANE), F32),
            pltpu.VMEM((N_SLAB * (cap + ROW_PAD), LANE), F32),
        ],
        compiler_params=_params(("arbitrary", "arbitrary"), vmem_mib=56),
        name="scatter",
    )(idx, ys, x2d, mod)


def _final_norm_kernel(x_ref, g_ref, o_ref):
    x = x_ref[...]
    o_ref[...] = x * lax.rsqrt(jnp.mean(x * x, axis=-1, keepdims=True) + EPS) * g_ref[...]


def _final_norm(x2d, g):
    t = x2d.shape[0]
    tm = 512
    return pl.pallas_call(
        _final_norm_kernel,
        grid=(t // tm,),
        in_specs=[pl.BlockSpec((tm, D_MODEL), lambda i: (i, 0)), pl.BlockSpec((1, D_MODEL), lambda i: (0, 0))],
        out_specs=pl.BlockSpec((tm, D_MODEL), lambda i: (i, 0)),
        out_shape=jax.ShapeDtypeStruct((t, D_MODEL), F32),
        compiler_params=_params(("arbitrary",)),
        name="final_norm",
    )(x2d, g)


def kernel(x_prompt, x_sample, c, cache_k, cache_v, state_C, state_n, state_m, c_ctx, w_ada, b_ada, g_norm1, g_norm2, w_in, b_gates, rpb, w_fourier, g_head, w_out, w_router, w_exp_gate, w_exp_up, w_exp_down, g_final):
    n_ctx, len_ctx, _ = x_prompt.shape
    n_lat, len_lat, _ = x_sample.shape
    past = cache_k.shape[2]
    cap_ctx = CAPACITY_FACTOR * len_ctx // N_EXPERTS
    cap_lat = CAPACITY_FACTOR * len_lat // N_EXPERTS
    n_st = 2 * H_MLSTM

    w_in_bf = w_in.astype(BF16)
    w_gt_bf = jnp.swapaxes(w_in[:, :, P_COLS:], 1, 2).astype(BF16)
    bg_row = b_gates.reshape(DEPTH, 1, N_GATE_COLS).astype(F32)
    bg_col = b_gates.reshape(DEPTH, N_GATE_COLS, 1).astype(F32)
    w_out_bf = w_out.astype(BF16)
    w_rt_bf = jnp.swapaxes(w_router, 1, 2).astype(BF16)
    g1 = g_norm1.reshape(DEPTH, 1, D_MODEL)
    g2 = g_norm2.reshape(DEPTH, 1, D_MODEL)
    gh = g_head.reshape(DEPTH, 1, D_MODEL)
    eye_g = jnp.eye(G_FOURIER, dtype=F32)
    wf_blk = jnp.einsum("lgcd,gh->lgchd", w_fourier, eye_g).reshape(DEPTH, B_W, B_W).astype(BF16)

    csc = _channel_dft()
    dft_ctx = _dft_mats(len_ctx)
    dft_side = _dft_mats(GRID_W)
    tw_cos, tw_sin = _twiddles(GRID_W, len_lat)
    bias_tiles = _nbr_bias_tiles(rpb)
    r = np.arange(MLSTM_CHUNK)
    tril = jnp.asarray(r[:, None] >= r[None, :], F32)
    triu = jnp.asarray(r[:, None] <= r[None, :], F32)
    triu_bf = triu.astype(BF16)
    hidx = np.arange(A_W) // HEAD_DIM
    ones_blk = jnp.asarray(hidx[:, None] == hidx[None, :], BF16)

    cvecs = jnp.concatenate([c_ctx[None, :], c, jnp.zeros((8 - 1 - n_lat, D_MODEL), F32)], axis=0)
    mod_all = _modulation(cvecs, w_ada, b_ada).reshape(DEPTH, 8, 6, D_MODEL)

    cache_k4 = cache_k.reshape(n_lat, DEPTH, past, A_W)
    cache_v4 = cache_v.reshape(n_lat, DEPTH, past, A_W)
    zero_c = jnp.zeros((n_ctx, 2, H_MLSTM, HEAD_DIM, HEAD_DIM), F32)
    zero_n = jnp.zeros((n_ctx, n_st, HEAD_DIM), F32)
    zero_m = jnp.zeros((n_ctx, n_st, 1), F32)

    xc = x_prompt.reshape(n_ctx * len_ctx, D_MODEL)
    xl = x_sample.reshape(n_lat * len_lat, D_MODEL)
    ks, vs, cs, ns, ms = [], [], [], [], []
    for l in range(DEPTH):
        mod_c = mod_all[l, 0:1]
        mod_l = mod_all[l, 1:1 + n_lat]

        pc, gc, gtc, abc = _inproj(xc, mod_c, l, g1, w_in_bf, w_gt_bf, bg_row, bg_col, csc, n_ctx * len_ctx)
        att_c = _ctx_attention(pc, n_ctx, len_ctx)
        four_c = _fourier(abc, dft_ctx[0], dft_ctx[1], wf_blk, l, n_ctx, len_ctx)
        hf_c, hb_c, c_new, n_new, m_new = _mlstm(pc, gc, gtc, zero_c, zero_n, zero_m, tril, triu, n_ctx, len_ctx)
        xc, h2c, affc = _merge(att_c, four_c, hf_c, hb_c, pc, xc, mod_c, l, gh, w_out_bf, g2, w_rt_bf, ones_blk,
                               n_ctx, len_ctx, n_ctx * len_ctx)
        ks.append(pc[:, A_W:2 * A_W].reshape(n_ctx, len_ctx, H_ATT, HEAD_DIM))
        vs.append(pc[:, 2 * A_W:3 * A_W].reshape(n_ctx, len_ctx, H_ATT, HEAD_DIM))
        cs.append(c_new)
        ns.append(n_new.reshape(n_ctx, 2, H_MLSTM, HEAD_DIM))
        ms.append(m_new.reshape(n_ctx, 2, H_MLSTM))

        pq, gq, gtq, abq = _inproj(xl, mod_l, l, g1, w_in_bf, w_gt_bf, bg_row, bg_col, csc, len_lat)
        att_l = _nbr_attention(pq, cache_k4, cache_v4, bias_tiles, l, n_lat, len_lat)
        four_l = _fourier_grid(abq, dft_side[0], dft_side[1], tw_cos, tw_sin, wf_blk, l, n_lat, GRID_W)
        hf_l, hb_l, _, _, _ = _mlstm(pq, gq, gtq, state_C[:, l], state_n[:, l].reshape(n_lat, n_st, HEAD_DIM),
                                     state_m[:, l].reshape(n_lat, n_st, 1), tril, triu, n_lat, len_lat)
        xl, h2l, affl = _merge(att_l, four_l, hf_l, hb_l, pq, xl, mod_l, l, gh, w_out_bf, g2, w_rt_bf, ones_blk,
                               n_lat, len_lat, len_lat)

        idx_c, gs_c = _route(affc, triu_bf, n_ctx, len_ctx, cap_ctx)
        idx_l, gs_l = _route(affl, triu_bf, n_lat, len_lat, cap_lat)
        xs_c = _gather(idx_c, h2c, n_ctx, len_ctx, cap_ctx, N_EXPERTS)
        xs_l = _gather(idx_l, h2l, n_lat, len_lat, cap_lat, 4)
        ys_c, ys_l = _experts(xs_c, xs_l, gs_c, gs_l, w_exp_gate, w_exp_up, w_exp_down, l)
        xc = _scatter(idx_c, ys_c, xc, jnp.broadcast_to(mod_c, (n_ctx, 6, D_MODEL)), n_ctx, len_ctx, cap_ctx,
                      N_EXPERTS, len_ctx)
        xl = _scatter(idx_l, ys_l, xl, mod_l, n_lat, len_lat, cap_lat, 2, 512)

    gf = g_final.reshape(1, D_MODEL)
    y_prompt = _final_norm(xc, gf).reshape(n_ctx, len_ctx, D_MODEL)
    y_sample = _final_norm(xl, gf).reshape(n_lat, len_lat, D_MODEL)
    return (y_prompt, y_sample, jnp.stack(ks, axis=1), jnp.stack(vs, axis=1), jnp.stack(cs, axis=1),
            jnp.stack(ns, axis=1), jnp.stack(ms, axis=1))
```

```python
import functools

import numpy as np
import jax
import jax.numpy as jnp
from jax import lax
from jax.experimental import pallas as pl
from jax.experimental.pallas import tpu as pltpu

F32 = jnp.float32
BF16 = jnp.bfloat16

D_MODEL = 1024
DEPTH = 2
HEAD_DIM = 64
H_ATT = 8
G_FOURIER = 4
H_MLSTM = 4
A_W = H_ATT * HEAD_DIM
B_W = G_FOURIER * HEAD_DIM
C_W = H_MLSTM * HEAD_DIM
N_GATE_COLS = 16
P_COLS = 3 * A_W + B_W + 4 * C_W
IN_COLS = P_COLS + N_GATE_COLS
GRID_W = 64
WIN_R = 8
WIN_C = 16
MLSTM_CHUNK = 128
N_EXPERTS = 16
CAPACITY_FACTOR = 2
EXPERT_FF = 2 * D_MODEL
EPS = 1e-6
NEG = -1e30

UB_OFF = 3 * A_W
QC_BLK, KC_BLK, VC_BLK, OC_BLK = 7, 8, 9, 10

LANE = 128
N_SLAB = D_MODEL // LANE
ROW_PAD = 8

NT_DIMS = (((1,), (1,)), ((), ()))
TN_DIMS = (((0,), (0,)), ((), ()))
MIB = 1024 * 1024


def _dot(a, b, precision=None):
    return jnp.dot(a, b, preferred_element_type=F32, precision=precision)


def _dot_nt(a, b, precision=None):
    return lax.dot_general(a, b, NT_DIMS, preferred_element_type=F32, precision=precision)


def _dot_tn(a, b):
    return lax.dot_general(a, b, TN_DIMS, preferred_element_type=F32)


def _params(sem, vmem_mib=48):
    return pltpu.CompilerParams(dimension_semantics=sem, vmem_limit_bytes=vmem_mib * MIB)


def _silu(x):
    return x * jax.nn.sigmoid(x)


def _log_sigmoid(x):
    return jnp.minimum(x, 0.0) - jnp.log1p(jnp.exp(-jnp.abs(x)))


def _mod_kernel(c_ref, w_ref, b_ref, o_ref):
    s = _silu(c_ref[...]).astype(BF16)
    o_ref[0] = _dot(s, w_ref[0].astype(BF16)) + b_ref[0]


def _modulation(cvecs, w_ada, b_ada):
    depth = w_ada.shape[0]
    tn = 1024
    return pl.pallas_call(
        _mod_kernel,
        grid=(depth, 6 * D_MODEL // tn),
        in_specs=[
            pl.BlockSpec((8, D_MODEL), lambda l, j: (0, 0)),
            pl.BlockSpec((1, D_MODEL, tn), lambda l, j: (l, 0, j)),
            pl.BlockSpec((1, 1, tn), lambda l, j: (l, 0, j)),
        ],
        out_specs=pl.BlockSpec((1, 8, tn), lambda l, j: (l, 0, j)),
        out_shape=jax.ShapeDtypeStruct((depth, 8, 6 * D_MODEL), F32),
        compiler_params=_params(("arbitrary", "arbitrary")),
        name="modulation",
    )(cvecs, w_ada, b_ada.reshape(depth, 1, 6 * D_MODEL))


def _inproj_kernel(x_ref, mod_ref, g1_ref, w_ref, wgt_ref, bgr_ref, bgc_ref, csc_ref,
                   p_ref, g_ref, gt_ref, ab_ref):
    x = x_ref[...]
    y = x * lax.rsqrt(jnp.mean(x * x, axis=-1, keepdims=True) + EPS) * g1_ref[0]
    h = (y * (1.0 + mod_ref[0, 1:2, :]) + mod_ref[0, 0:1, :]).astype(BF16)
    for j in range(0, P_COLS, 256):
        pj = _dot(h, w_ref[0, :, j:j + 256])
        p_ref[:, j:j + 256] = pj
        if j == UB_OFF:
            ab_ref[...] = _dot(pj.astype(BF16), csc_ref[...]).astype(BF16)
    g_ref[...] = _dot(h, w_ref[0, :, P_COLS:IN_COLS]) + bgr_ref[0]
    gt_ref[...] = _dot_nt(wgt_ref[0], h) + bgc_ref[0]


def _inproj(x2d, mod, layer, g1, w_in_bf, w_gt_bf, bg_row, bg_col, csc, seq_len):
    t = x2d.shape[0]
    tm = 256
    tiles_per_seq = seq_len // tm
    return pl.pallas_call(
        _inproj_kernel,
        grid=(t // tm,),
        in_specs=[
            pl.BlockSpec((tm, D_MODEL), lambda i: (i, 0)),
            pl.BlockSpec((1, 6, D_MODEL), lambda i: (i // tiles_per_seq, 0, 0)),
            pl.BlockSpec((1, 1, D_MODEL), lambda i: (layer, 0, 0)),
            pl.BlockSpec((1, D_MODEL, IN_COLS), lambda i: (layer, 0, 0)),
            pl.BlockSpec((1, N_GATE_COLS, D_MODEL), lambda i: (layer, 0, 0)),
            pl.BlockSpec((1, 1, N_GATE_COLS), lambda i: (layer, 0, 0)),
            pl.BlockSpec((1, N_GATE_COLS, 1), lambda i: (layer, 0, 0)),
            pl.BlockSpec((B_W, 2 * B_W), lambda i: (0, 0)),
        ],
        out_specs=[
            pl.BlockSpec((tm, P_COLS), lambda i: (i, 0)),
            pl.BlockSpec((tm, N_GATE_COLS), lambda i: (i, 0)),
            pl.BlockSpec((N_GATE_COLS, tm), lambda i: (0, i)),
            pl.BlockSpec((tm, 2 * B_W), lambda i: (i, 0)),
        ],
        out_shape=[
            jax.ShapeDtypeStruct((t, P_COLS), F32),
            jax.ShapeDtypeStruct((t, N_GATE_COLS), F32),
            jax.ShapeDtypeStruct((N_GATE_COLS, t), F32),
            jax.ShapeDtypeStruct((t, 2 * B_W), BF16),
        ],
        compiler_params=_params(("arbitrary",)),
        name="inproj",
    )(x2d, mod, g1, w_in_bf, w_gt_bf, bg_row, bg_col, csc)


def _ctx_attn_kernel(q_ref, k_ref, v_ref, o_ref):
    scale = HEAD_DIM ** -0.5
    for h in range(H_ATT):
        sl = slice(HEAD_DIM * h, HEAD_DIM * (h + 1))
        q = q_ref[:, sl].astype(BF16)
        k = k_ref[:, sl].astype(BF16)
        v = v_ref[:, sl].astype(BF16)
        s = _dot_nt(q, k) * scale
        e = jnp.exp(s - jnp.max(s, axis=-1, keepdims=True))
        w = e * (1.0 / jnp.sum(e, axis=-1, keepdims=True))
        o_ref[:, sl] = _dot(w.astype(BF16), v)


def _ctx_attention(p, n_seq, seq_len):
    return pl.pallas_call(
        _ctx_attn_kernel,
        grid=(n_seq,),
        in_specs=[
            pl.BlockSpec((seq_len, A_W), lambda b: (b, 0)),
            pl.BlockSpec((seq_len, A_W), lambda b: (b, 1)),
            pl.BlockSpec((seq_len, A_W), lambda b: (b, 2)),
        ],
        out_specs=pl.BlockSpec((seq_len, A_W), lambda b: (b, 0)),
        out_shape=jax.ShapeDtypeStruct((n_seq * seq_len, A_W), F32),
        compiler_params=_params(("arbitrary",)),
        name="ctx_attention",
    )(p, p, p)


Q_ROWS = 8
K_ROWS = 16
KEY_BLK = 256


NO_ROW = 2 * WIN_R - 1


def _nbr_bias_tiles(rpb):
    cq = np.arange(GRID_W)[:, None]
    ck = np.arange(GRID_W)[None, :]
    cs = np.clip(cq - WIN_C // 2, 0, GRID_W - WIN_C)
    col_ok = (ck >= cs) & (ck < cs + WIN_C)
    dc = np.clip(ck - cq + WIN_C - 1, 0, 2 * WIN_C - 2)
    toe = jnp.where(jnp.asarray(col_ok), rpb[..., dc].astype(F32), NEG)
    toe = jnp.concatenate([toe, jnp.full(toe.shape[:2] + (1, GRID_W, GRID_W), NEG, F32)], axis=2)
    zero = jnp.zeros_like(toe)
    return jnp.stack([jnp.concatenate([toe, zero], axis=-1), jnp.concatenate([zero, toe], axis=-1)], axis=2)


def _nbr_attn_kernel(q_ref, k0_ref, k1_ref, k2_ref, k3_ref, v0_ref, v1_ref, v2_ref, v3_ref,
                     ck_ref, cv_ref, tab_ref, o_ref):
    scale = HEAD_DIM ** -0.5
    k_refs = (k0_ref, k1_ref, k2_ref, k3_ref)
    v_refs = (v0_ref, v1_ref, v2_ref, v3_ref)
    rb = pl.program_id(1)
    rows = GRID_W
    key_row0 = jnp.clip(Q_ROWS * rb - WIN_R // 2, 0, rows - K_ROWS)
    d = []
    for rq in range(Q_ROWS):
        r = Q_ROWS * rb + rq
        rs = jnp.clip(r - WIN_R // 2, 0, rows - WIN_R)
        d_row = []
        for rk in range(K_ROWS):
            rka = key_row0 + rk
            ok = (rka >= rs) & (rka < rs + WIN_R)
            d_row.append(jnp.where(ok, rka - r + WIN_R - 1, NO_ROW))
        d.append(d_row)
    rows_per_blk = KEY_BLK // GRID_W
    heads = range(2)
    sl = [slice(HEAD_DIM * hh, HEAD_DIM * (hh + 1)) for hh in heads]
    q = [(q_ref[:, sl[hh]] * scale).astype(BF16) for hh in heads]
    s_ctx = [_dot_nt(q[hh], ck_ref[0, 0, :, sl[hh]].astype(BF16)) for hh in heads]
    s_loc = [[] for _ in heads]
    for j in range(4):
        for hh in heads:
            bias = jnp.concatenate([
                jnp.concatenate([
                    tab_ref[0, hh, 0, d[rq][rows_per_blk * j + 2 * p]]
                    + tab_ref[0, hh, 1, d[rq][rows_per_blk * j + 2 * p + 1]]
                    for p in range(rows_per_blk // 2)], axis=1)
                for rq in range(Q_ROWS)], axis=0)
            s_loc[hh].append(_dot_nt(q[hh], k_refs[j][:, sl[hh]].astype(BF16)) + bias)
    m = [jnp.max(s_ctx[hh], axis=-1, keepdims=True) for hh in heads]
    for j in range(4):
        m = [jnp.maximum(m[hh], jnp.max(s_loc[hh][j], axis=-1, keepdims=True)) for hh in heads]
    e_ctx = [jnp.exp(s_ctx[hh] - m[hh]) for hh in heads]
    den = [jnp.sum(e_ctx[hh], axis=-1, keepdims=True) for hh in heads]
    num = [_dot(e_ctx[hh].astype(BF16), cv_ref[0, 0, :, sl[hh]].astype(BF16)) for hh in heads]
    for j in range(4):
        e = [jnp.exp(s_loc[hh][j] - m[hh]) for hh in heads]
        den = [den[hh] + jnp.sum(e[hh], axis=-1, keepdims=True) for hh in heads]
        num = [num[hh] + _dot(e[hh].astype(BF16), v_refs[j][:, sl[hh]].astype(BF16)) for hh in heads]
    o_ref[...] = jnp.concatenate([num[hh] / den[hh] for hh in heads], axis=1)


def _nbr_attention(p, cache_k4, cache_v4, bias_tiles, layer, n_seq, seq_len):
    q_tok = Q_ROWS * GRID_W
    n_rb = seq_len // q_tok
    kb_per_seq = seq_len // KEY_BLK
    max_base = kb_per_seq - 4

    def kmap(j, col0):
        def f(hp, rb, b):
            base = jnp.clip(2 * rb - 1, 0, max_base)
            return (b * kb_per_seq + base + j, col0 + hp)
        return f

    past = cache_k4.shape[2]
    in_specs = [pl.BlockSpec((q_tok, 128), lambda hp, rb, b: (b * n_rb + rb, hp))]
    in_specs += [pl.BlockSpec((KEY_BLK, 128), kmap(j, A_W // 128)) for j in range(4)]
    in_specs += [pl.BlockSpec((KEY_BLK, 128), kmap(j, 2 * A_W // 128)) for j in range(4)]
    in_specs += [
        pl.BlockSpec((1, 1, past, 128), lambda hp, rb, b: (b, layer, 0, hp)),
        pl.BlockSpec((1, 1, past, 128), lambda hp, rb, b: (b, layer, 0, hp)),
        pl.BlockSpec((1, 2, 2, 2 * WIN_R, GRID_W, 2 * GRID_W), lambda hp, rb, b: (layer, hp, 0, 0, 0, 0)),
    ]
    return pl.pallas_call(
        _nbr_attn_kernel,
        grid=(H_ATT // 2, n_rb, n_seq),
        in_specs=in_specs,
        out_specs=pl.BlockSpec((q_tok, 128), lambda hp, rb, b: (b * n_rb + rb, hp)),
        out_shape=jax.ShapeDtypeStruct((n_seq * seq_len, A_W), F32),
        compiler_params=_params(("arbitrary", "arbitrary", "arbitrary")),
        name="nbr_attention",
    )(p, p, p, p, p, p, p, p, p, cache_k4, cache_v4, bias_tiles)


def _dft_mats(n):
    idx = jnp.arange(n, dtype=jnp.int32)
    ang = ((idx[:, None] * idx[None, :]) % n).astype(F32) * (2.0 * np.pi / n)
    return jnp.cos(ang).astype(BF16), jnp.sin(ang).astype(BF16)


def _channel_dft():
    c = np.arange(HEAD_DIM)
    ang = 2.0 * np.pi * ((c[:, None] * c[None, :]) % HEAD_DIM) / HEAD_DIM
    eye = np.eye(G_FOURIER)
    mats = np.concatenate([np.kron(eye, np.cos(ang)), np.kron(eye, np.sin(ang))], axis=1)
    return jnp.asarray(mats, F32).astype(BF16)


def _fourier_kernel(c_ref, s_ref, ab_ref, wf_ref, o_ref, acc_ref, *, scale, n_k):
    k = pl.program_id(2)

    @pl.when(k == 0)
    def _():
        acc_ref[...] = jnp.zeros_like(acc_ref)

    acc_ref[...] += _dot(c_ref[...], ab_ref[:, :B_W]) - _dot(s_ref[...], ab_ref[:, B_W:])

    @pl.when(k == n_k - 1)
    def _():
        z = (acc_ref[...] * scale).astype(BF16)
        o_ref[...] = _dot(z, wf_ref[0])


def _fourier(ab, cmat, smat, wf_blk, layer, n_seq, seq_len):
    ti = min(seq_len, 512)
    tk = min(seq_len, 1024)
    n_i, n_k = seq_len // ti, seq_len // tk
    scale = float((seq_len * HEAD_DIM) ** -0.5)
    return pl.pallas_call(
        functools.partial(_fourier_kernel, scale=scale, n_k=n_k),
        grid=(n_seq, n_i, n_k),
        in_specs=[
            pl.BlockSpec((ti, tk), lambda s, i, k: (i, k)),
            pl.BlockSpec((ti, tk), lambda s, i, k: (i, k)),
            pl.BlockSpec((tk, 2 * B_W), lambda s, i, k: (s * n_k + k, 0)),
            pl.BlockSpec((1, B_W, B_W), lambda s, i, k: (layer, 0, 0)),
        ],
        out_specs=pl.BlockSpec((ti, B_W), lambda s, i, k: (s * n_i + i, 0)),
        out_shape=jax.ShapeDtypeStruct((n_seq * seq_len, B_W), F32),
        scratch_shapes=[pltpu.VMEM((ti, B_W), F32)],
        compiler_params=_params(("arbitrary", "arbitrary", "arbitrary")),
        name="fourier",
    )(cmat, smat, ab, wf_blk)


FS_GROUP = 8


def _twiddles(side, n):
    k1 = jnp.arange(side, dtype=jnp.int32)[:, None]
    n2 = jnp.arange(side, dtype=jnp.int32)[None, :]
    ang = (k1 * n2).astype(F32) * (2.0 * np.pi / n)
    wide = lambda t: jnp.broadcast_to(t[:, :, None], (side, side, B_W)).reshape(side, side * B_W)
    return wide(jnp.cos(ang)), wide(jnp.sin(ang))


def _fourier_stage1_kernel(ab_ref, c_ref, s_ref, tc_ref, ts_ref, o_ref):
    ab = ab_ref[...]
    m1 = _dot(c_ref[...], ab)
    m2 = _dot(s_ref[...], ab)
    for t in range(FS_GROUP):
        a0 = 2 * B_W * t
        yr = m1[:, a0:a0 + B_W] - m2[:, a0 + B_W:a0 + 2 * B_W]
        yi = -(m1[:, a0 + B_W:a0 + 2 * B_W] + m2[:, a0:a0 + B_W])
        ct = tc_ref[:, B_W * t:B_W * (t + 1)]
        st = ts_ref[:, B_W * t:B_W * (t + 1)]
        o_ref[:, a0:a0 + B_W] = (yr * ct + yi * st).astype(BF16)
        o_ref[:, a0 + B_W:a0 + 2 * B_W] = (yi * ct - yr * st).astype(BF16)


def _fourier_stage2_kernel(y_ref, c_ref, s_ref, wf_ref, o_ref, *, side, scale):
    zs = []
    for j in range(FS_GROUP):
        rows = slice(side * j, side * (j + 1))
        zs.append(_dot(c_ref[...], y_ref[rows, :B_W]) + _dot(s_ref[...], y_ref[rows, B_W:]))
    z = (jnp.concatenate(zs, axis=0) * scale).astype(BF16)
    o = _dot(z, wf_ref[0])
    for j in range(FS_GROUP):
        o_ref[:, B_W * j:B_W * (j + 1)] = o[side * j:side * (j + 1), :]


def _fourier_grid(ab, cmat, smat, tw_cos, tw_sin, wf_blk, layer, n_seq, side):
    seq_len = side * side
    n_g = side // FS_GROUP
    scale = float((seq_len * HEAD_DIM) ** -0.5)
    small = pl.BlockSpec((side, side), lambda s, j: (0, 0))
    y = pl.pallas_call(
        _fourier_stage1_kernel,
        grid=(n_seq, n_g),
        in_specs=[
            pl.BlockSpec((side, 2 * B_W * FS_GROUP), lambda s, j: (s, j)),
            small,
            small,
            pl.BlockSpec((side, B_W * FS_GROUP), lambda s, j: (0, j)),
            pl.BlockSpec((side, B_W * FS_GROUP), lambda s, j: (0, j)),
        ],
        out_specs=pl.BlockSpec((side, 2 * B_W * FS_GROUP), lambda s, j: (s, j)),
        out_shape=jax.ShapeDtypeStruct((n_seq * side, side * 2 * B_W), BF16),
        compiler_params=_params(("arbitrary", "arbitrary")),
        name="fourier_stage1",
    )(ab.reshape(n_seq * side, side * 2 * B_W), cmat, smat, tw_cos, tw_sin)
    out = pl.pallas_call(
        functools.partial(_fourier_stage2_kernel, side=side, scale=scale),
        grid=(n_seq, n_g),
        in_specs=[
            pl.BlockSpec((side * FS_GROUP, 2 * B_W), lambda s, j: (s * n_g + j, 0)),
            small,
            small,
            pl.BlockSpec((1, B_W, B_W), lambda s, j: (layer, 0, 0)),
        ],
        out_specs=pl.BlockSpec((side, B_W * FS_GROUP), lambda s, j: (s, j)),
        out_shape=jax.ShapeDtypeStruct((n_seq * side, side * B_W), F32),
        compiler_params=_params(("arbitrary", "arbitrary")),
        name="fourier_stage2",
    )(y.reshape(n_seq * seq_len, 2 * B_W), cmat, smat, wf_blk)
    return out.reshape(n_seq * seq_len, B_W)


SEQS_PER_STEP = 2


def _mlstm_kernel(*refs, n_chunks):
    sps = SEQS_PER_STEP
    n_side = 4 + sps
    fwd, bwd = refs[:n_side], refs[n_side:2 * n_side]
    c0_ref, n0_ref, m0_ref, tril_ref, triu_ref = refs[2 * n_side:2 * n_side + 5]
    hf_ref, hb_ref, cout_ref, nout_ref, mout_ref, c_s, n_s, m_s = refs[2 * n_side + 5:]
    c = pl.program_id(1)
    hi = lax.Precision.HIGHEST

    @pl.when(c == 0)
    def _():
        c_s[...] = c0_ref[...]
        n_s[...] = n0_ref[...]
        m_s[...] = m0_ref[...]

    dirs = ((fwd, tril_ref), (bwd, triu_ref))
    heads = [(j, d, hd) for j in range(sps) for d in range(2) for hd in range(H_MLSTM)]
    rng = range(len(heads))
    keep = [ref[...] > 0.5 for _, ref in dirs]
    pre = {}
    for j in range(sps):
        for d, (side, mask_ref) in enumerate(dirs):
            q_ref, k_ref, v_ref, g_ref = side[:4]
            gt_ref = side[4 + j]
            mask = mask_ref[...]
            go = 2 * H_MLSTM * d
            lf_cols = _log_sigmoid(g_ref[j, :, go + H_MLSTM:go + 2 * H_MLSTM])
            lf_rows = _log_sigmoid(gt_ref[go + H_MLSTM:go + 2 * H_MLSTM, :])
            pre[j, d] = dict(
                ig_cols=g_ref[j, :, go:go + H_MLSTM],
                ig_rows=gt_ref[go:go + H_MLSTM, :],
                b_cols=_dot(mask, lf_cols, precision=hi),
                b_rows=_dot_nt(lf_rows, mask, precision=hi),
                b_last=jnp.sum(lf_rows, axis=1, keepdims=True),
                q=q_ref[j], k=k_ref[j] * (HEAD_DIM ** -0.5), v=v_ref[j])

    def cols(hd):
        return slice(HEAD_DIM * hd, HEAD_DIM * (hd + 1))

    q = [pre[j, d]["q"][:, cols(hd)] for j, d, hd in heads]
    k = [pre[j, d]["k"][:, cols(hd)] for j, d, hd in heads]
    v = [pre[j, d]["v"][:, cols(hd)] for j, d, hd in heads]
    qh = [t.astype(BF16) for t in q]
    kh = [t.astype(BF16) for t in k]
    c_st = [c_s[j, d, hd] for j, d, hd in heads]
    n_st = [n_s[j, H_MLSTM * d + hd:H_MLSTM * d + hd + 1, :] for j, d, hd in heads]
    m_st = [m_s[j, H_MLSTM * d + hd:H_MLSTM * d + hd + 1, :] for j, d, hd in heads]
    b_col = [pre[j, d]["b_cols"][:, hd:hd + 1] for j, d, hd in heads]
    b_row = [pre[j, d]["b_rows"][hd:hd + 1, :] for j, d, hd in heads]
    ig_row = [pre[j, d]["ig_rows"][hd:hd + 1, :] for j, d, hd in heads]
    ig_col = [pre[j, d]["ig_cols"][:, hd:hd + 1] for j, d, hd in heads]
    bl = [pre[j, d]["b_last"][hd:hd + 1, :] for j, d, hd in heads]

    qk = [_dot_nt(qh[i], kh[i]) for i in rng]
    qc = [_dot_nt(qh[i], c_st[i].astype(BF16)) for i in rng]
    dmat = [jnp.where(keep[heads[i][1]], b_col[i] - b_row[i] + ig_row[i], NEG) for i in rng]
    inter = [b_col[i] + m_st[i] for i in rng]
    m_t = [jnp.maximum(inter[i], jnp.max(dmat[i], axis=-1, keepdims=True)) for i in rng]
    s = [qk[i] * jnp.exp(dmat[i] - m_t[i]) for i in rng]
    w_in = [jnp.exp(inter[i] - m_t[i]) for i in rng]
    num = [_dot(s[i].astype(BF16), v[i].astype(BF16)) + w_in[i] * qc[i] for i in rng]
    den = [jnp.sum(s[i], axis=-1, keepdims=True) + w_in[i] * jnp.sum(q[i] * n_st[i], axis=-1, keepdims=True)
           for i in rng]
    h = [num[i] / jnp.maximum(jnp.abs(den[i]), jnp.exp(-m_t[i])) for i in rng]
    per_seq = 2 * H_MLSTM
    for j in range(sps):
        hf_ref[j] = jnp.concatenate(h[per_seq * j:per_seq * j + H_MLSTM], axis=1)
        hb_ref[j] = jnp.concatenate(h[per_seq * j + H_MLSTM:per_seq * (j + 1)], axis=1)

    g_row = [bl[i] - b_row[i] + ig_row[i] for i in rng]
    g_col = [bl[i] - b_col[i] + ig_col[i] for i in rng]
    m_new = [jnp.maximum(bl[i] + m_st[i], jnp.max(g_row[i], axis=-1, keepdims=True)) for i in rng]
    wc = [jnp.exp(bl[i] + m_st[i] - m_new[i]) for i in rng]
    ws_col = [jnp.exp(g_col[i] - m_new[i]) for i in rng]
    c_new = [wc[i] * c_st[i] + _dot_tn((v[i] * ws_col[i]).astype(BF16), kh[i]) for i in rng]
    n_new = [wc[i] * n_st[i] + jnp.sum(k[i] * ws_col[i], axis=0, keepdims=True) for i in rng]
    for i, (j, d, hd) in enumerate(heads):
        c_s[j, d, hd] = c_new[i]
    for j in range(sps):
        n_s[j] = jnp.concatenate(n_new[per_seq * j:per_seq * (j + 1)], axis=0)
        m_s[j] = jnp.concatenate(m_new[per_seq * j:per_seq * (j + 1)], axis=0)

    @pl.when(c == n_chunks - 1)
    def _():
        cout_ref[...] = c_s[...]
        nout_ref[...] = n_s[...]
        mout_ref[...] = m_s[...]


def _mlstm(p, g, gt, c0, n0, m0, tril, triu, n_seq, seq_len):
    lc = MLSTM_CHUNK
    nc = seq_len // lc
    n_st = 2 * H_MLSTM
    sps = SEQS_PER_STEP
    p3 = p.reshape(n_seq, seq_len, P_COLS)
    g3 = g.reshape(n_seq, seq_len, N_GATE_COLS)

    def fwd(c):
        return c

    def bwd(c):
        return nc - 1 - c

    def side(chunk):
        tok = lambda col: (lambda b, c: (b, chunk(c), col))
        return [
            pl.BlockSpec((sps, lc, C_W), tok(QC_BLK)),
            pl.BlockSpec((sps, lc, C_W), tok(KC_BLK)),
            pl.BlockSpec((sps, lc, C_W), tok(VC_BLK)),
            pl.BlockSpec((sps, lc, N_GATE_COLS), tok(0)),
        ] + [pl.BlockSpec((N_GATE_COLS, lc), lambda b, c, j=j: (0, (b * sps + j) * nc + chunk(c))) for j in range(sps)]

    state_specs = [
        pl.BlockSpec((sps, 2, H_MLSTM, HEAD_DIM, HEAD_DIM), lambda b, c: (b, 0, 0, 0, 0)),
        pl.BlockSpec((sps, n_st, HEAD_DIM), lambda b, c: (b, 0, 0)),
        pl.BlockSpec((sps, n_st, 1), lambda b, c: (b, 0, 0)),
    ]
    tri_spec = pl.BlockSpec((lc, lc), lambda b, c: (0, 0))
    t = n_seq * seq_len
    operands = [p3, p3, p3, g3] + [gt] * sps
    hf, hb, c_out, n_out, m_out = pl.pallas_call(
        functools.partial(_mlstm_kernel, n_chunks=nc),
        grid=(n_seq // sps, nc),
        in_specs=side(fwd) + side(bwd) + state_specs + [tri_spec, tri_spec],
        out_specs=[
            pl.BlockSpec((sps, lc, C_W), lambda b, c: (b, c, 0)),
            pl.BlockSpec((sps, lc, C_W), lambda b, c: (b, nc - 1 - c, 0)),
        ] + state_specs,
        out_shape=[
            jax.ShapeDtypeStruct((n_seq, seq_len, C_W), F32),
            jax.ShapeDtypeStruct((n_seq, seq_len, C_W), F32),
            jax.ShapeDtypeStruct((n_seq, 2, H_MLSTM, HEAD_DIM, HEAD_DIM), F32),
            jax.ShapeDtypeStruct((n_seq, n_st, HEAD_DIM), F32),
            jax.ShapeDtypeStruct((n_seq, n_st, 1), F32),
        ],
        scratch_shapes=[
            pltpu.VMEM((sps, 2, H_MLSTM, HEAD_DIM, HEAD_DIM), F32),
            pltpu.VMEM((sps, n_st, HEAD_DIM), F32),
            pltpu.VMEM((sps, n_st, 1), F32),
        ],
        compiler_params=_params(("arbitrary", "arbitrary")),
        name="mlstm",
    )(*operands, *operands, c0, n0, m0, tril, triu)
    return hf.reshape(t, C_W), hb.reshape(t, C_W), c_out, n_out, m_out


def _head_norm(y, g, ones_blk):
    ysq = y * y
    hi = ysq.astype(BF16)
    lo = (ysq - hi.astype(F32)).astype(BF16)
    ss = _dot(hi, ones_blk) + _dot(lo, ones_blk)
    return y * lax.rsqrt(ss * (1.0 / HEAD_DIM) + EPS) * g


def _merge_kernel(att_ref, four_ref, hf_ref, hb_ref, oc_ref, x_ref, mod_ref, gh_ref, wo_ref, g2_ref, wrt_ref,
                  ones_ref, xo_ref, h2_ref, afft_ref):
    gh = gh_ref[0]
    ya = _head_norm(att_ref[...], gh[:, :A_W], ones_ref[...])
    yf = _head_norm(four_ref[...], gh[:, A_W:A_W + B_W], ones_ref[:B_W, :B_W])
    ym = _head_norm(hf_ref[...] + hb_ref[...], gh[:, A_W + B_W:], ones_ref[:C_W, :C_W])
    ym = ym * jax.nn.sigmoid(oc_ref[...])
    out = (_dot(ya.astype(BF16), wo_ref[0, :A_W, :])
           + _dot(yf.astype(BF16), wo_ref[0, A_W:A_W + B_W, :])
           + _dot(ym.astype(BF16), wo_ref[0, A_W + B_W:, :]))
    x = x_ref[...] + mod_ref[0, 2:3, :] * out
    xo_ref[...] = x
    y2 = x * lax.rsqrt(jnp.mean(x * x, axis=-1, keepdims=True) + EPS) * g2_ref[0]
    h2 = (y2 * (1.0 + mod_ref[0, 4:5, :]) + mod_ref[0, 3:4, :]).astype(BF16)
    h2_wide = h2.astype(F32)
    for s in range(N_SLAB):
        h2_ref[0, s] = h2_wide[:, LANE * s:LANE * (s + 1)]
    logits = _dot_nt(wrt_ref[0], h2)
    e = jnp.exp(logits - jnp.max(logits, axis=0, keepdims=True))
    afft_ref[...] = e / jnp.sum(e, axis=0, keepdims=True)


def _merge(att, four, hf, hb, p, x2d, mod, layer, g_head, w_out_bf, g2, w_rt_bf, ones_blk, n_seq, seq_len, mod_seq_len):
    t = x2d.shape[0]
    tm = 256
    tiles_per_mod = mod_seq_len // tm
    tiles_per_seq = seq_len // tm
    row = lambda i: (i, 0)
    lay = lambda i: (layer, 0, 0)
    return pl.pallas_call(
        _merge_kernel,
        grid=(t // tm,),
        in_specs=[
            pl.BlockSpec((tm, A_W), row),
            pl.BlockSpec((tm, B_W), row),
            pl.BlockSpec((tm, C_W), row),
            pl.BlockSpec((tm, C_W), row),
            pl.BlockSpec((tm, C_W), lambda i: (i, OC_BLK)),
            pl.BlockSpec((tm, D_MODEL), row),
            pl.BlockSpec((1, 6, D_MODEL), lambda i: (i // tiles_per_mod, 0, 0)),
            pl.BlockSpec((1, 1, D_MODEL), lay),
            pl.BlockSpec((1, D_MODEL, D_MODEL), lay),
            pl.BlockSpec((1, 1, D_MODEL), lay),
            pl.BlockSpec((1, N_EXPERTS, D_MODEL), lay),
            pl.BlockSpec((A_W, A_W), lambda i: (0, 0)),
        ],
        out_specs=[
            pl.BlockSpec((tm, D_MODEL), row),
            pl.BlockSpec((1, N_SLAB, tm, LANE), lambda i: (i // tiles_per_seq, 0, i % tiles_per_seq, 0)),
            pl.BlockSpec((N_EXPERTS, tm), lambda i: (0, i)),
        ],
        out_shape=[
            jax.ShapeDtypeStruct((t, D_MODEL), F32),
            jax.ShapeDtypeStruct((n_seq, N_SLAB, seq_len + ROW_PAD, LANE), F32),
            jax.ShapeDtypeStruct((N_EXPERTS, t), F32),
        ],
        compiler_params=_params(("arbitrary",)),
        name="merge",
    )(att, four, hf, hb, p, x2d, mod, g_head, w_out_bf, g2, w_rt_bf, ones_blk)


BISECT_STEPS = 48
TOKEN_CHUNK = 1024


TOKEN_SPLIT = 64


def _route_kernel(aff_ref, triu_ref, idx_ref, gs_ref, sp_ref, *, ns, seq_len, cap):
    seqs = range(ns)
    aff = [aff_ref[:, seq_len * j:seq_len * (j + 1)] for j in seqs]

    def body(_, bounds):
        out = []
        for j in seqs:
            lo, hi = bounds[j]
            mid = 0.5 * (lo + hi)
            ge = jnp.sum(jnp.where(aff[j] >= mid, 1.0, 0.0), axis=1, keepdims=True) >= cap
            out.append((jnp.where(ge, mid, lo), jnp.where(ge, hi, mid)))
        return tuple(out)

    start = (jnp.zeros((N_EXPERTS, 1), F32), jnp.full((N_EXPERTS, 1), 2.0, F32))
    bounds = lax.fori_loop(0, BISECT_STEPS, body, tuple(start for _ in seqs))
    thr = [jnp.max(jnp.where(aff[j] < bounds[j][1], aff[j], -1.0), axis=1, keepdims=True) for j in seqs]
    need = [cap - jnp.sum(jnp.where(aff[j] > thr[j], 1.0, 0.0), axis=1, keepdims=True) for j in seqs]
    triu = triu_ref[...]
    eq_carry = [jnp.zeros((N_EXPERTS, 1), F32) for _ in seqs]
    pos_carry = [jnp.zeros((N_EXPERTS, 1), F32) for _ in seqs]
    for b in range(seq_len // 128):
        for j in seqs:
            blk = aff[j][:, 128 * b:128 * (b + 1)]
            eq = blk == thr[j]
            eq_f = jnp.where(eq, 1.0, 0.0)
            eq_inc = _dot(eq_f.astype(BF16), triu) + eq_carry[j]
            sel = (blk > thr[j]) | (eq & (eq_inc - eq_f < need[j]))
            sel_f = jnp.where(sel, 1.0, 0.0)
            pos_inc = _dot(sel_f.astype(BF16), triu) + pos_carry[j]
            t0 = seq_len * j + 128 * b
            sp_ref[:, t0:t0 + 128] = jnp.where(sel, pos_inc - sel_f, -1.0).astype(jnp.int32)
            eq_carry[j] = eq_inc[:, 127:128]
            pos_carry[j] = pos_inc[:, 127:128]

    tc = min(seq_len, TOKEN_CHUNK)
    slot = lax.broadcasted_iota(jnp.int32, (cap, tc), 0)
    part = lax.broadcasted_iota(jnp.int32, (8, tc), 0)
    tok = lax.broadcasted_iota(jnp.int32, (1, tc), 1).astype(F32)
    chunks = range(0, seq_len, tc)
    tok_hi = [jnp.floor((tok + float(t0)) * (1.0 / TOKEN_SPLIT)) for t0 in chunks]
    tok_lo = [tok + float(t0) - TOKEN_SPLIT * hi for t0, hi in zip(chunks, tok_hi)]

    def per_expert(e, carry):
        for j in seqs:
            acc = jnp.zeros((cap, 8), F32)
            for ci, t0 in enumerate(chunks):
                cols = slice(seq_len * j + t0, seq_len * j + t0 + tc)
                onehot = jnp.where(slot == sp_ref[pl.ds(e, 1), cols], 1.0, 0.0).astype(BF16)
                a = aff_ref[pl.ds(e, 1), cols]
                a_hi = a.astype(BF16).astype(F32)
                a_mid = (a - a_hi).astype(BF16).astype(F32)
                a_lo = a - a_hi - a_mid
                vals = jnp.where(part == 0, tok_hi[ci], jnp.where(part == 1, tok_lo[ci], jnp.where(
                    part == 2, a_hi, jnp.where(part == 3, a_mid, jnp.where(part == 4, a_lo, 0.0)))))
                acc = acc + _dot_nt(onehot, vals.astype(BF16))
            idx_ref[N_EXPERTS * j + e] = (TOKEN_SPLIT * acc[:, 0:1] + acc[:, 1:2]).astype(jnp.int32)
            gs_ref[e, cap * j:cap * (j + 1)] = acc[:, 2:3] + acc[:, 3:4] + acc[:, 4:5]
        return carry

    lax.fori_loop(0, N_EXPERTS, per_expert, 0)


def _route(afft, triu_bf, n_seq, seq_len, cap, ns):
    idx, gs = pl.pallas_call(
        functools.partial(_route_kernel, ns=ns, seq_len=seq_len, cap=cap),
        grid=(n_seq // ns,),
        in_specs=[
            pl.BlockSpec((N_EXPERTS, ns * seq_len), lambda s: (0, s)),
            pl.BlockSpec((128, 128), lambda s: (0, 0)),
        ],
        out_specs=[
            pl.BlockSpec((ns * N_EXPERTS, cap, 1), lambda s: (s, 0, 0)),
            pl.BlockSpec((N_EXPERTS, ns * cap, 1), lambda s: (0, s, 0)),
        ],
        out_shape=[
            jax.ShapeDtypeStruct((n_seq * N_EXPERTS, cap, 1), jnp.int32),
            jax.ShapeDtypeStruct((N_EXPERTS, n_seq * cap, 1), F32),
        ],
        scratch_shapes=[pltpu.VMEM((N_EXPERTS, ns * seq_len), jnp.int32)],
        compiler_params=_params(("arbitrary",)),
        name="route",
    )(afft, triu_bf)
    return idx.reshape(n_seq * N_EXPERTS, cap), gs


ROW_COPIES = 8


def _gather_kernel(idx_ref, src_ref, xs_ref, tile_ref, *, eb, pitch, cap):
    ei = pl.program_id(1)
    tp = cap + ROW_PAD

    def per_expert(ee, carry):
        e = ei * eb + ee

        def rows(g, c):
            for u in range(ROW_COPIES):
                slot = g * ROW_COPIES + u
                tile_ref[pl.ds(slot, N_SLAB, stride=tp), :] = src_ref[pl.ds(idx_ref[e, slot], N_SLAB, stride=pitch), :]
            return c

        lax.fori_loop(0, cap // ROW_COPIES, rows, 0)
        for s in range(N_SLAB):
            xs_ref[ee, :, LANE * s:LANE * (s + 1)] = tile_ref[tp * s:tp * s + cap, :].astype(BF16)
        return carry

    lax.fori_loop(0, eb, per_expert, 0)


def _gather(idx, h2_slabs, n_seq, seq_len, cap, eb):
    pitch = seq_len + ROW_PAD
    return pl.pallas_call(
        functools.partial(_gather_kernel, eb=eb, pitch=pitch, cap=cap),
        grid=(n_seq, N_EXPERTS // eb),
        in_specs=[
            pl.BlockSpec((N_EXPERTS, cap), lambda s, e: (s, 0), memory_space=pltpu.SMEM),
            pl.BlockSpec((N_SLAB * pitch, LANE), lambda s, e: (s, 0)),
        ],
        out_specs=pl.BlockSpec((eb, cap, D_MODEL), lambda s, e: (e, s, 0)),
        out_shape=jax.ShapeDtypeStruct((N_EXPERTS, n_seq * cap, D_MODEL), BF16),
        scratch_shapes=[pltpu.VMEM((N_SLAB * (cap + ROW_PAD), LANE), F32)],
        compiler_params=_params(("arbitrary", "arbitrary"), vmem_mib=56),
        name="gather",
    )(idx, h2_slabs.reshape(n_seq * N_SLAB * pitch, LANE))


def _expert_kernel(xc_ref, xl_ref, gc_ref, gl_ref, wg_ref, wu_ref, wd_ref, yc_ref, yl_ref, *, n_f):
    f = pl.program_id(1)

    @pl.when(f == 0)
    def _():
        yc_ref[...] = jnp.zeros_like(yc_ref)
        yl_ref[...] = jnp.zeros_like(yl_ref)

    wg = wg_ref[0, 0].astype(BF16)
    wu = wu_ref[0, 0].astype(BF16)
    wd = wd_ref[0, 0].astype(BF16)
    for x_ref, y_ref in ((xc_ref, yc_ref), (xl_ref, yl_ref)):
        x = x_ref[0]
        mid = (_silu(_dot(x, wg)) * _dot(x, wu)).astype(BF16)
        y_ref[0] += _dot(mid, wd)

    @pl.when(f == n_f - 1)
    def _():
        yc_ref[0] = yc_ref[0] * gc_ref[0]
        yl_ref[0] = yl_ref[0] * gl_ref[0]


def _experts(xs_c, xs_l, gs_c, gs_l, w_g, w_u, w_d, layer):
    rc, rl = xs_c.shape[1], xs_l.shape[1]
    tf = 512
    n_f = EXPERT_FF // tf
    return pl.pallas_call(
        functools.partial(_expert_kernel, n_f=n_f),
        grid=(N_EXPERTS, n_f),
        in_specs=[
            pl.BlockSpec((1, rc, D_MODEL), lambda e, f: (e, 0, 0)),
            pl.BlockSpec((1, rl, D_MODEL), lambda e, f: (e, 0, 0)),
            pl.BlockSpec((1, rc, 1), lambda e, f: (e, 0, 0)),
            pl.BlockSpec((1, rl, 1), lambda e, f: (e, 0, 0)),
            pl.BlockSpec((1, 1, D_MODEL, tf), lambda e, f: (layer, e, 0, f)),
            pl.BlockSpec((1, 1, D_MODEL, tf), lambda e, f: (layer, e, 0, f)),
            pl.BlockSpec((1, 1, tf, D_MODEL), lambda e, f: (layer, e, f, 0)),
        ],
        out_specs=[
            pl.BlockSpec((1, rc, D_MODEL), lambda e, f: (e, 0, 0)),
            pl.BlockSpec((1, rl, D_MODEL), lambda e, f: (e, 0, 0)),
        ],
        out_shape=[
            jax.ShapeDtypeStruct((N_EXPERTS, rc, D_MODEL), F32),
            jax.ShapeDtypeStruct((N_EXPERTS, rl, D_MODEL), F32),
        ],
        compiler_params=_params(("arbitrary", "arbitrary"), vmem_mib=56),
        name="experts",
    )(xs_c, xs_l, gs_c, gs_l, w_g, w_u, w_d)


def _scatter_kernel(idx_ref, y_ref, x_ref, mod_ref, o_ref, acc_ref, tile_ref, *, eb, n_e, pitch, cap, tm):
    step = pl.program_id(1)
    tp = cap + ROW_PAD

    @pl.when(step == 0)
    def _():
        acc_ref[...] = jnp.zeros_like(acc_ref)

    @pl.when(step < n_e)
    def _():
        def per_expert(ee, carry):
            e = step * eb + ee
            for s in range(N_SLAB):
                tile_ref[tp * s:tp * s + cap, :] = y_ref[ee, :, LANE * s:LANE * (s + 1)]

            def rows(g, c):
                dst = [idx_ref[e, g * ROW_COPIES + u] for u in range(ROW_COPIES)]
                new = [acc_ref[pl.ds(dst[u], N_SLAB, stride=pitch), :]
                       + tile_ref[pl.ds(g * ROW_COPIES + u, N_SLAB, stride=tp), :] for u in range(ROW_COPIES)]
                for u in range(ROW_COPIES):
                    acc_ref[pl.ds(dst[u], N_SLAB, stride=pitch), :] = new[u]
                return c

            lax.fori_loop(0, cap // ROW_COPIES, rows, 0)
            return carry

        lax.fori_loop(0, eb, per_expert, 0)

    @pl.when(step >= n_e)
    def _():
        base = (step - n_e) * tm
        for s in range(N_SLAB):
            cols = slice(LANE * s, LANE * (s + 1))
            rows_s = pl.ds(pl.multiple_of(pitch * s + base, 8), tm)
            o_ref[:, cols] = x_ref[:, cols] + mod_ref[0, 5:6, cols] * acc_ref[rows_s, :]


def _scatter(idx, ys, x2d, mod, n_seq, seq_len, cap, eb, tm):
    n_e = N_EXPERTS // eb
    n_out = seq_len // tm
    pitch = seq_len + ROW_PAD
    out_blk = lambda s, j: (s * n_out + jnp.maximum(j - n_e, 0), 0)
    return pl.pallas_call(
        functools.partial(_scatter_kernel, eb=eb, n_e=n_e, pitch=pitch, cap=cap, tm=tm),
        grid=(n_seq, n_e + n_out),
        in_specs=[
            pl.BlockSpec((N_EXPERTS, cap), lambda s, j: (s, 0), memory_space=pltpu.SMEM),
            pl.BlockSpec((eb, cap, D_MODEL), lambda s, j: (jnp.minimum(j, n_e - 1), s, 0)),
            pl.BlockSpec((tm, D_MODEL), out_blk),
            pl.BlockSpec((1, 6, D_MODEL), lambda s, j: (s, 0, 0)),
        ],
        out_specs=pl.BlockSpec((tm, D_MODEL), out_blk),
        out_shape=jax.ShapeDtypeStruct((n_seq * seq_len, D_MODEL), F32),
        scratch_shapes=[
            pltpu.VMEM((N_SLAB * pitch, LANE), F32),
            pltpu.VMEM((N_SLAB * (cap + ROW_PAD), LANE), F32),
        ],
        compiler_params=_params(("arbitrary", "arbitrary"), vmem_mib=56),
        name="scatter",
    )(idx, ys, x2d, mod)


def _final_norm_kernel(x_ref, g_ref, o_ref):
    x = x_ref[...]
    o_ref[...] = x * lax.rsqrt(jnp.mean(x * x, axis=-1, keepdims=True) + EPS) * g_ref[...]


def _final_norm(x2d, g):
    t = x2d.shape[0]
    tm = 512
    return pl.pallas_call(
        _final_norm_kernel,
        grid=(t // tm,),
        in_specs=[pl.BlockSpec((tm, D_MODEL), lambda i: (i, 0)), pl.BlockSpec((1, D_MODEL), lambda i: (0, 0))],
        out_specs=pl.BlockSpec((tm, D_MODEL), lambda i: (i, 0)),
        out_shape=jax.ShapeDtypeStruct((t, D_MODEL), F32),
        compiler_params=_params(("arbitrary",)),
        name="final_norm",
    )(x2d, g)


def kernel(x_prompt, x_sample, c, cache_k, cache_v, state_C, state_n, state_m, c_ctx, w_ada, b_ada, g_norm1, g_norm2, w_in, b_gates, rpb, w_fourier, g_head, w_out, w_router, w_exp_gate, w_exp_up, w_exp_down, g_final):
    n_ctx, len_ctx, _ = x_prompt.shape
    n_lat, len_lat, _ = x_sample.shape
    past = cache_k.shape[2]
    cap_ctx = CAPACITY_FACTOR * len_ctx // N_EXPERTS
    cap_lat = CAPACITY_FACTOR * len_lat // N_EXPERTS
    n_st = 2 * H_MLSTM

    w_in_bf = w_in.astype(BF16)
    w_gt_bf = jnp.swapaxes(w_in[:, :, P_COLS:], 1, 2).astype(BF16)
    bg_row = b_gates.reshape(DEPTH, 1, N_GATE_COLS).astype(F32)
    bg_col = b_gates.reshape(DEPTH, N_GATE_COLS, 1).astype(F32)
    w_out_bf = w_out.astype(BF16)
    w_rt_bf = jnp.swapaxes(w_router, 1, 2).astype(BF16)
    g1 = g_norm1.reshape(DEPTH, 1, D_MODEL)
    g2 = g_norm2.reshape(DEPTH, 1, D_MODEL)
    gh = g_head.reshape(DEPTH, 1, D_MODEL)
    eye_g = jnp.eye(G_FOURIER, dtype=F32)
    wf_blk = jnp.einsum("lgcd,gh->lgchd", w_fourier, eye_g).reshape(DEPTH, B_W, B_W).astype(BF16)

    csc = _channel_dft()
    dft_ctx = _dft_mats(len_ctx)
    dft_side = _dft_mats(GRID_W)
    tw_cos, tw_sin = _twiddles(GRID_W, len_lat)
    bias_tiles = _nbr_bias_tiles(rpb)
    r = np.arange(MLSTM_CHUNK)
    tril = jnp.asarray(r[:, None] >= r[None, :], F32)
    triu = jnp.asarray(r[:, None] <= r[None, :], F32)
    triu_bf = triu.astype(BF16)
    hidx = np.arange(A_W) // HEAD_DIM
    ones_blk = jnp.asarray(hidx[:, None] == hidx[None, :], BF16)

    cvecs = jnp.concatenate([c_ctx[None, :], c, jnp.zeros((8 - 1 - n_lat, D_MODEL), F32)], axis=0)
    mod_all = _modulation(cvecs, w_ada, b_ada).reshape(DEPTH, 8, 6, D_MODEL)

    cache_k4 = cache_k.reshape(n_lat, DEPTH, past, A_W)
    cache_v4 = cache_v.reshape(n_lat, DEPTH, past, A_W)
    zero_c = jnp.zeros((n_ctx, 2, H_MLSTM, HEAD_DIM, HEAD_DIM), F32)
    zero_n = jnp.zeros((n_ctx, n_st, HEAD_DIM), F32)
    zero_m = jnp.zeros((n_ctx, n_st, 1), F32)

    xc = x_prompt.reshape(n_ctx * len_ctx, D_MODEL)
    xl = x_sample.reshape(n_lat * len_lat, D_MODEL)
    ks, vs, cs, ns, ms = [], [], [], [], []
    for l in range(DEPTH):
        mod_c = mod_all[l, 0:1]
        mod_l = mod_all[l, 1:1 + n_lat]

        pc, gc, gtc, abc = _inproj(xc, mod_c, l, g1, w_in_bf, w_gt_bf, bg_row, bg_col, csc, n_ctx * len_ctx)
        att_c = _ctx_attention(pc, n_ctx, len_ctx)
        four_c = _fourier(abc, dft_ctx[0], dft_ctx[1], wf_blk, l, n_ctx, len_ctx)
        hf_c, hb_c, c_new, n_new, m_new = _mlstm(pc, gc, gtc, zero_c, zero_n, zero_m, tril, triu, n_ctx, len_ctx)
        xc, h2c, affc = _merge(att_c, four_c, hf_c, hb_c, pc, xc, mod_c, l, gh, w_out_bf, g2, w_rt_bf, ones_blk,
                               n_ctx, len_ctx, n_ctx * len_ctx)
        ks.append(pc[:, A_W:2 * A_W].reshape(n_ctx, len_ctx, H_ATT, HEAD_DIM))
        vs.append(pc[:, 2 * A_W:3 * A_W].reshape(n_ctx, len_ctx, H_ATT, HEAD_DIM))
        cs.append(c_new)
        ns.append(n_new.reshape(n_ctx, 2, H_MLSTM, HEAD_DIM))
        ms.append(m_new.reshape(n_ctx, 2, H_MLSTM))

        pq, gq, gtq, abq = _inproj(xl, mod_l, l, g1, w_in_bf, w_gt_bf, bg_row, bg_col, csc, len_lat)
        att_l = _nbr_attention(pq, cache_k4, cache_v4, bias_tiles, l, n_lat, len_lat)
        four_l = _fourier_grid(abq, dft_side[0], dft_side[1], tw_cos, tw_sin, wf_blk, l, n_lat, GRID_W)
        hf_l, hb_l, _, _, _ = _mlstm(pq, gq, gtq, state_C[:, l], state_n[:, l].reshape(n_lat, n_st, HEAD_DIM),
                                     state_m[:, l].reshape(n_lat, n_st, 1), tril, triu, n_lat, len_lat)
        xl, h2l, affl = _merge(att_l, four_l, hf_l, hb_l, pq, xl, mod_l, l, gh, w_out_bf, g2, w_rt_bf, ones_blk,
                               n_lat, len_lat, len_lat)

        idx_c, gs_c = _route(affc, triu_bf, n_ctx, len_ctx, cap_ctx, n_ctx)
        idx_l, gs_l = _route(affl, triu_bf, n_lat, len_lat, cap_lat, 1)
        xs_c = _gather(idx_c, h2c, n_ctx, len_ctx, cap_ctx, N_EXPERTS)
        xs_l = _gather(idx_l, h2l, n_lat, len_lat, cap_lat, 4)
        ys_c, ys_l = _experts(xs_c, xs_l, gs_c, gs_l, w_exp_gate, w_exp_up, w_exp_down, l)
        xc = _scatter(idx_c, ys_c, xc, jnp.broadcast_to(mod_c, (n_ctx, 6, D_MODEL)), n_ctx, len_ctx, cap_ctx,
                      N_EXPERTS, len_ctx)
        xl = _scatter(idx_l, ys_l, xl, mod_l, n_lat, len_lat, cap_lat, 2, 512)

    gf = g_final.reshape(1, D_MODEL)
    y_prompt = _final_norm(xc, gf).reshape(n_ctx, len_ctx, D_MODEL)
    y_sample = _final_norm(xl, gf).reshape(n_lat, len_lat, D_MODEL)
    return (y_prompt, y_sample, jnp.stack(ks, axis=1), jnp.stack(vs, axis=1), jnp.stack(cs, axis=1),
            jnp.stack(ns, axis=1), jnp.stack(ms, axis=1))
```

```python
import functools

import numpy as np
import jax
import jax.numpy as jnp
from jax import lax
from jax.experimental import pallas as pl
from jax.experimental.pallas import tpu as pltpu

F32 = jnp.float32
BF16 = jnp.bfloat16

D_MODEL = 1024
DEPTH = 2
HEAD_DIM = 64
H_ATT = 8
G_FOURIER = 4
H_MLSTM = 4
A_W = H_ATT * HEAD_DIM
B_W = G_FOURIER * HEAD_DIM
C_W = H_MLSTM * HEAD_DIM
N_GATE_COLS = 16
P_COLS = 3 * A_W + B_W + 4 * C_W
IN_COLS = P_COLS + N_GATE_COLS
GRID_W = 64
WIN_R = 8
WIN_C = 16
MLSTM_CHUNK = 128
N_EXPERTS = 16
CAPACITY_FACTOR = 2
EXPERT_FF = 2 * D_MODEL
EPS = 1e-6
NEG = -1e30

UB_OFF = 3 * A_W
QC_BLK, KC_BLK, VC_BLK, OC_BLK = 7, 8, 9, 10

LANE = 128
N_SLAB = D_MODEL // LANE
ROW_PAD = 8
MERGE_TILE = 256

NT_DIMS = (((1,), (1,)), ((), ()))
TN_DIMS = (((0,), (0,)), ((), ()))
MIB = 1024 * 1024


def _dot(a, b, precision=None):
    return jnp.dot(a, b, preferred_element_type=F32, precision=precision)


def _dot_nt(a, b, precision=None):
    return lax.dot_general(a, b, NT_DIMS, preferred_element_type=F32, precision=precision)


def _dot_tn(a, b):
    return lax.dot_general(a, b, TN_DIMS, preferred_element_type=F32)


def _params(sem, vmem_mib=48):
    return pltpu.CompilerParams(dimension_semantics=sem, vmem_limit_bytes=vmem_mib * MIB)


def _silu(x):
    return x * jax.nn.sigmoid(x)


def _log_sigmoid(x):
    return jnp.minimum(x, 0.0) - jnp.log1p(jnp.exp(-jnp.abs(x)))


def _mod_kernel(c_ref, w_ref, b_ref, o_ref):
    s = _silu(c_ref[...]).astype(BF16)
    o_ref[0] = _dot(s, w_ref[0].astype(BF16)) + b_ref[0]


def _modulation(cvecs, w_ada, b_ada):
    depth = w_ada.shape[0]
    tn = 1024
    return pl.pallas_call(
        _mod_kernel,
        grid=(depth, 6 * D_MODEL // tn),
        in_specs=[
            pl.BlockSpec((8, D_MODEL), lambda l, j: (0, 0)),
            pl.BlockSpec((1, D_MODEL, tn), lambda l, j: (l, 0, j)),
            pl.BlockSpec((1, 1, tn), lambda l, j: (l, 0, j)),
        ],
        out_specs=pl.BlockSpec((1, 8, tn), lambda l, j: (l, 0, j)),
        out_shape=jax.ShapeDtypeStruct((depth, 8, 6 * D_MODEL), F32),
        compiler_params=_params(("arbitrary", "arbitrary")),
        name="modulation",
    )(cvecs, w_ada, b_ada.reshape(depth, 1, 6 * D_MODEL))


def _inproj_kernel(x_ref, mod_ref, g1_ref, w_ref, wgt_ref, wvo_ref, bgr_ref, bgc_ref, csc_ref,
                   p_ref, g_ref, gt_ref, vo_ref, ab_ref):
    x = x_ref[...]
    y = x * lax.rsqrt(jnp.mean(x * x, axis=-1, keepdims=True) + EPS) * g1_ref[0]
    h = (y * (1.0 + mod_ref[0, 1:2, :]) + mod_ref[0, 0:1, :]).astype(BF16)
    for j in range(0, P_COLS, 256):
        pj = _dot(h, w_ref[0, :, j:j + 256])
        p_ref[:, j:j + 256] = pj
        if j == UB_OFF:
            ab_ref[...] = _dot(pj.astype(BF16), csc_ref[...]).astype(BF16)
    g_ref[...] = _dot(h, w_ref[0, :, P_COLS:IN_COLS]) + bgr_ref[0]
    gt_ref[...] = _dot_nt(wgt_ref[0], h) + bgc_ref[0]
    vo_ref[...] = _dot_nt(wvo_ref[0], h)


def _inproj(x2d, mod, layer, g1, w_in_bf, w_gt_bf, w_vo_bf, bg_row, bg_col, csc, seq_len):
    t = x2d.shape[0]
    tm = 256
    tiles_per_seq = seq_len // tm
    return pl.pallas_call(
        _inproj_kernel,
        grid=(t // tm,),
        in_specs=[
            pl.BlockSpec((tm, D_MODEL), lambda i: (i, 0)),
            pl.BlockSpec((1, 6, D_MODEL), lambda i: (i // tiles_per_seq, 0, 0)),
            pl.BlockSpec((1, 1, D_MODEL), lambda i: (layer, 0, 0)),
            pl.BlockSpec((1, D_MODEL, IN_COLS), lambda i: (layer, 0, 0)),
            pl.BlockSpec((1, N_GATE_COLS, D_MODEL), lambda i: (layer, 0, 0)),
            pl.BlockSpec((1, 2 * C_W, D_MODEL), lambda i: (layer, 0, 0)),
            pl.BlockSpec((1, 1, N_GATE_COLS), lambda i: (layer, 0, 0)),
            pl.BlockSpec((1, N_GATE_COLS, 1), lambda i: (layer, 0, 0)),
            pl.BlockSpec((B_W, 2 * B_W), lambda i: (0, 0)),
        ],
        out_specs=[
            pl.BlockSpec((tm, P_COLS), lambda i: (i, 0)),
            pl.BlockSpec((tm, N_GATE_COLS), lambda i: (i, 0)),
            pl.BlockSpec((N_GATE_COLS, tm), lambda i: (0, i)),
            pl.BlockSpec((2 * C_W, tm), lambda i: (0, i)),
            pl.BlockSpec((tm, 2 * B_W), lambda i: (i, 0)),
        ],
        out_shape=[
            jax.ShapeDtypeStruct((t, P_COLS), F32),
            jax.ShapeDtypeStruct((t, N_GATE_COLS), F32),
            jax.ShapeDtypeStruct((N_GATE_COLS, t), F32),
            jax.ShapeDtypeStruct((2 * C_W, t), F32),
            jax.ShapeDtypeStruct((t, 2 * B_W), BF16),
        ],
        compiler_params=_params(("arbitrary",)),
        name="inproj",
    )(x2d, mod, g1, w_in_bf, w_gt_bf, w_vo_bf, bg_row, bg_col, csc)


def _ctx_attn_kernel(q_ref, k_ref, v_ref, o_ref):
    scale = HEAD_DIM ** -0.5
    for h in range(H_ATT):
        sl = slice(HEAD_DIM * h, HEAD_DIM * (h + 1))
        q = q_ref[:, sl].astype(BF16)
        k = k_ref[:, sl].astype(BF16)
        v = v_ref[:, sl].astype(BF16)
        s = _dot_nt(q, k) * scale
        e = jnp.exp(s - jnp.max(s, axis=-1, keepdims=True))
        w = e * (1.0 / jnp.sum(e, axis=-1, keepdims=True))
        o_ref[:, sl] = _dot(w.astype(BF16), v)


def _ctx_attention(p, n_seq, seq_len):
    return pl.pallas_call(
        _ctx_attn_kernel,
        grid=(n_seq,),
        in_specs=[
            pl.BlockSpec((seq_len, A_W), lambda b: (b, 0)),
            pl.BlockSpec((seq_len, A_W), lambda b: (b, 1)),
            pl.BlockSpec((seq_len, A_W), lambda b: (b, 2)),
        ],
        out_specs=pl.BlockSpec((seq_len, A_W), lambda b: (b, 0)),
        out_shape=jax.ShapeDtypeStruct((n_seq * seq_len, A_W), F32),
        compiler_params=_params(("arbitrary",)),
        name="ctx_attention",
    )(p, p, p)


Q_ROWS = 8
K_ROWS = 16
KEY_BLK = 256


NO_ROW = 2 * WIN_R - 1


def _nbr_bias_tiles(rpb):
    n_c = 2 * WIN_C - 1
    cq = np.arange(GRID_W)[:, None]
    ck = np.arange(GRID_W)[None, :]
    cs = np.clip(cq - WIN_C // 2, 0, GRID_W - WIN_C)
    col_ok = (ck >= cs) & (ck < cs + WIN_C)
    pick = np.where(col_ok, np.clip(ck - cq + WIN_C - 1, 0, n_c - 1), n_c)
    sel = np.zeros((2, GRID_W, 2 * GRID_W, n_c + 1), np.float32)
    for side in range(2):
        sel[side, cq, side * GRID_W + ck, pick] = 1.0
    ext = jnp.full(rpb.shape[:2] + (2 * WIN_R, n_c + 1), NEG, F32)
    ext = ext.at[:, :, :2 * WIN_R - 1, :n_c].set(rpb.astype(F32))
    return jnp.einsum("lhdm,sqkm->lhdsqk", ext, jnp.asarray(sel), precision=lax.Precision.HIGHEST)


def _nbr_attn_kernel(q_ref, k0_ref, k1_ref, k2_ref, k3_ref, v0_ref, v1_ref, v2_ref, v3_ref,
                     ck_ref, cv_ref, tab_ref, o_ref):
    scale = HEAD_DIM ** -0.5
    k_refs = (k0_ref, k1_ref, k2_ref, k3_ref)
    v_refs = (v0_ref, v1_ref, v2_ref, v3_ref)
    rb = pl.program_id(1)
    rows = GRID_W
    key_row0 = jnp.clip(Q_ROWS * rb - WIN_R // 2, 0, rows - K_ROWS)
    d = []
    for rq in range(Q_ROWS):
        r = Q_ROWS * rb + rq
        rs = jnp.clip(r - WIN_R // 2, 0, rows - WIN_R)
        d_row = []
        for rk in range(K_ROWS):
            rka = key_row0 + rk
            ok = (rka >= rs) & (rka < rs + WIN_R)
            d_row.append(jnp.where(ok, rka - r + WIN_R - 1, NO_ROW))
        d.append(d_row)
    rows_per_blk = KEY_BLK // GRID_W
    heads = range(2)
    sl = [slice(HEAD_DIM * hh, HEAD_DIM * (hh + 1)) for hh in heads]
    q = [(q_ref[:, sl[hh]] * scale).astype(BF16) for hh in heads]
    s_ctx = [_dot_nt(q[hh], ck_ref[0, 0, :, sl[hh]].astype(BF16)) for hh in heads]
    s_loc = [[] for _ in heads]
    for j in range(4):
        for hh in heads:
            bias = jnp.concatenate([
                jnp.concatenate([
                    tab_ref[0, hh, d[rq][rows_per_blk * j + 2 * p], 0]
                    + tab_ref[0, hh, d[rq][rows_per_blk * j + 2 * p + 1], 1]
                    for p in range(rows_per_blk // 2)], axis=1)
                for rq in range(Q_ROWS)], axis=0)
            s_loc[hh].append(_dot_nt(q[hh], k_refs[j][:, sl[hh]].astype(BF16)) + bias)
    m = [jnp.max(s_ctx[hh], axis=-1, keepdims=True) for hh in heads]
    for j in range(4):
        m = [jnp.maximum(m[hh], jnp.max(s_loc[hh][j], axis=-1, keepdims=True)) for hh in heads]
    e_ctx = [jnp.exp(s_ctx[hh] - m[hh]) for hh in heads]
    den = [jnp.sum(e_ctx[hh], axis=-1, keepdims=True) for hh in heads]
    num = [_dot(e_ctx[hh].astype(BF16), cv_ref[0, 0, :, sl[hh]].astype(BF16)) for hh in heads]
    for j in range(4):
        e = [jnp.exp(s_loc[hh][j] - m[hh]) for hh in heads]
        den = [den[hh] + jnp.sum(e[hh], axis=-1, keepdims=True) for hh in heads]
        num = [num[hh] + _dot(e[hh].astype(BF16), v_refs[j][:, sl[hh]].astype(BF16)) for hh in heads]
    o_ref[...] = jnp.concatenate([num[hh] / den[hh] for hh in heads], axis=1)


def _nbr_attention(p, cache_k4, cache_v4, bias_tiles, layer, n_seq, seq_len):
    q_tok = Q_ROWS * GRID_W
    n_rb = seq_len // q_tok
    kb_per_seq = seq_len // KEY_BLK
    max_base = kb_per_seq - 4

    def kmap(j, col0):
        def f(hp, rb, b):
            base = jnp.clip(2 * rb - 1, 0, max_base)
            return (b * kb_per_seq + base + j, col0 + hp)
        return f

    past = cache_k4.shape[2]
    in_specs = [pl.BlockSpec((q_tok, 128), lambda hp, rb, b: (b * n_rb + rb, hp))]
    in_specs += [pl.BlockSpec((KEY_BLK, 128), kmap(j, A_W // 128)) for j in range(4)]
    in_specs += [pl.BlockSpec((KEY_BLK, 128), kmap(j, 2 * A_W // 128)) for j in range(4)]
    in_specs += [
        pl.BlockSpec((1, 1, past, 128), lambda hp, rb, b: (b, layer, 0, hp)),
        pl.BlockSpec((1, 1, past, 128), lambda hp, rb, b: (b, layer, 0, hp)),
        pl.BlockSpec((1, 2, 2 * WIN_R, 2, GRID_W, 2 * GRID_W), lambda hp, rb, b: (layer, hp, 0, 0, 0, 0)),
    ]
    return pl.pallas_call(
        _nbr_attn_kernel,
        grid=(H_ATT // 2, n_rb, n_seq),
        in_specs=in_specs,
        out_specs=pl.BlockSpec((q_tok, 128), lambda hp, rb, b: (b * n_rb + rb, hp)),
        out_shape=jax.ShapeDtypeStruct((n_seq * seq_len, A_W), F32),
        compiler_params=_params(("arbitrary", "arbitrary", "arbitrary")),
        name="nbr_attention",
    )(p, p, p, p, p, p, p, p, p, cache_k4, cache_v4, bias_tiles)


def _dft_mats(n):
    idx = jnp.arange(n, dtype=jnp.int32)
    ang = ((idx[:, None] * idx[None, :]) % n).astype(F32) * (2.0 * np.pi / n)
    return jnp.cos(ang).astype(BF16), jnp.sin(ang).astype(BF16)


def _channel_dft():
    c = np.arange(HEAD_DIM)
    ang = 2.0 * np.pi * ((c[:, None] * c[None, :]) % HEAD_DIM) / HEAD_DIM
    eye = np.eye(G_FOURIER)
    mats = np.concatenate([np.kron(eye, np.cos(ang)), np.kron(eye, np.sin(ang))], axis=1)
    return jnp.asarray(mats, F32).astype(BF16)


def _fourier_kernel(c_ref, s_ref, ab_ref, wf_ref, o_ref, acc_ref, *, scale, n_k):
    k = pl.program_id(2)

    @pl.when(k == 0)
    def _():
        acc_ref[...] = jnp.zeros_like(acc_ref)

    acc_ref[...] += _dot(c_ref[...], ab_ref[:, :B_W]) - _dot(s_ref[...], ab_ref[:, B_W:])

    @pl.when(k == n_k - 1)
    def _():
        z = (acc_ref[...] * scale).astype(BF16)
        o_ref[...] = _dot(z, wf_ref[0])


def _fourier(ab, cmat, smat, wf_blk, layer, n_seq, seq_len):
    ti = min(seq_len, 512)
    tk = min(seq_len, 1024)
    n_i, n_k = seq_len // ti, seq_len // tk
    scale = float((seq_len * HEAD_DIM) ** -0.5)
    return pl.pallas_call(
        functools.partial(_fourier_kernel, scale=scale, n_k=n_k),
        grid=(n_seq, n_i, n_k),
        in_specs=[
            pl.BlockSpec((ti, tk), lambda s, i, k: (i, k)),
            pl.BlockSpec((ti, tk), lambda s, i, k: (i, k)),
            pl.BlockSpec((tk, 2 * B_W), lambda s, i, k: (s * n_k + k, 0)),
            pl.BlockSpec((1, B_W, B_W), lambda s, i, k: (layer, 0, 0)),
        ],
        out_specs=pl.BlockSpec((ti, B_W), lambda s, i, k: (s * n_i + i, 0)),
        out_shape=jax.ShapeDtypeStruct((n_seq * seq_len, B_W), F32),
        scratch_shapes=[pltpu.VMEM((ti, B_W), F32)],
        compiler_params=_params(("arbitrary", "arbitrary", "arbitrary")),
        name="fourier",
    )(cmat, smat, ab, wf_blk)


FS_GROUP = 8


def _twiddles(side, n):
    k1 = jnp.arange(side, dtype=jnp.int32)[:, None]
    n2 = jnp.arange(side, dtype=jnp.int32)[None, :]
    ang = (k1 * n2).astype(F32) * (2.0 * np.pi / n)
    wide = lambda t: jnp.broadcast_to(t[:, :, None], (side, side, B_W)).reshape(side, side * B_W)
    return wide(jnp.cos(ang)), wide(jnp.sin(ang))


def _fourier_stage1_kernel(ab_ref, c_ref, s_ref, tc_ref, ts_ref, o_ref):
    ab = ab_ref[...]
    m1 = _dot(c_ref[...], ab)
    m2 = _dot(s_ref[...], ab)
    for t in range(FS_GROUP):
        a0 = 2 * B_W * t
        yr = m1[:, a0:a0 + B_W] - m2[:, a0 + B_W:a0 + 2 * B_W]
        yi = -(m1[:, a0 + B_W:a0 + 2 * B_W] + m2[:, a0:a0 + B_W])
        ct = tc_ref[:, B_W * t:B_W * (t + 1)]
        st = ts_ref[:, B_W * t:B_W * (t + 1)]
        o_ref[:, a0:a0 + B_W] = (yr * ct + yi * st).astype(BF16)
        o_ref[:, a0 + B_W:a0 + 2 * B_W] = (yi * ct - yr * st).astype(BF16)


def _fourier_stage2_kernel(y_ref, c_ref, s_ref, wf_ref, o_ref, *, side, scale):
    zs = []
    for j in range(FS_GROUP):
        rows = slice(side * j, side * (j + 1))
        zs.append(_dot(c_ref[...], y_ref[rows, :B_W]) + _dot(s_ref[...], y_ref[rows, B_W:]))
    z = (jnp.concatenate(zs, axis=0) * scale).astype(BF16)
    o = _dot(z, wf_ref[0])
    for j in range(FS_GROUP):
        o_ref[:, B_W * j:B_W * (j + 1)] = o[side * j:side * (j + 1), :]


def _fourier_grid(ab, cmat, smat, tw_cos, tw_sin, wf_blk, layer, n_seq, side):
    seq_len = side * side
    n_g = side // FS_GROUP
    scale = float((seq_len * HEAD_DIM) ** -0.5)
    small = pl.BlockSpec((side, side), lambda s, j: (0, 0))
    y = pl.pallas_call(
        _fourier_stage1_kernel,
        grid=(n_seq, n_g),
        in_specs=[
            pl.BlockSpec((side, 2 * B_W * FS_GROUP), lambda s, j: (s, j)),
            small,
            small,
            pl.BlockSpec((side, B_W * FS_GROUP), lambda s, j: (0, j)),
            pl.BlockSpec((side, B_W * FS_GROUP), lambda s, j: (0, j)),
        ],
        out_specs=pl.BlockSpec((side, 2 * B_W * FS_GROUP), lambda s, j: (s, j)),
        out_shape=jax.ShapeDtypeStruct((n_seq * side, side * 2 * B_W), BF16),
        compiler_params=_params(("arbitrary", "arbitrary")),
        name="fourier_stage1",
    )(ab.reshape(n_seq * side, side * 2 * B_W), cmat, smat, tw_cos, tw_sin)
    out = pl.pallas_call(
        functools.partial(_fourier_stage2_kernel, side=side, scale=scale),
        grid=(n_seq, n_g),
        in_specs=[
            pl.BlockSpec((side * FS_GROUP, 2 * B_W), lambda s, j: (s * n_g + j, 0)),
            small,
            small,
            pl.BlockSpec((1, B_W, B_W), lambda s, j: (layer, 0, 0)),
        ],
        out_specs=pl.BlockSpec((side, B_W * FS_GROUP), lambda s, j: (s, j)),
        out_shape=jax.ShapeDtypeStruct((n_seq * side, side * B_W), F32),
        compiler_params=_params(("arbitrary", "arbitrary")),
        name="fourier_stage2",
    )(y.reshape(n_seq * seq_len, 2 * B_W), cmat, smat, wf_blk)
    return out.reshape(n_seq * seq_len, B_W)


SEQS_PER_STEP = 2


def _mlstm_kernel(*refs, n_chunks):
    sps = SEQS_PER_STEP
    n_side = 3 + 2 * sps
    fwd, bwd = refs[:n_side], refs[n_side:2 * n_side]
    c0_ref, n0_ref, m0_ref, tril_ref, triu_ref = refs[2 * n_side:2 * n_side + 5]
    hf_ref, hb_ref, cout_ref, nout_ref, mout_ref, c_s, n_s, m_s = refs[2 * n_side + 5:]
    c = pl.program_id(1)
    hi = lax.Precision.HIGHEST
    lc = MLSTM_CHUNK
    pair_w = 2 * HEAD_DIM
    n_pairs = H_MLSTM // 2

    @pl.when(c == 0)
    def _():
        c_s[...] = c0_ref[...]
        n_s[...] = n0_ref[...]
        m_s[...] = m0_ref[...]

    lo_lane = lax.broadcasted_iota(jnp.int32, (1, pair_w), 1) < HEAD_DIM
    lo_row = lax.broadcasted_iota(jnp.int32, (pair_w, 1), 0) < HEAD_DIM
    row8 = lax.broadcasted_iota(jnp.int32, (8, 1), 0)
    cum_mask = [tril_ref[...], triu_ref[...]]
    keep_t = [triu_ref[...] > 0.5, tril_ref[...] > 0.5]
    pairs = [(j, d, hp) for j in range(sps) for d in range(2) for hp in range(n_pairs)]
    heads = [(pi, hh) for pi in range(len(pairs)) for hh in range(2)]
    rng = range(len(heads))

    pre = {}
    for j in range(sps):
        for d, side in enumerate((fwd, bwd)):
            q_ref, k_ref, g_ref = side[:3]
            gt_ref, vt_ref = side[3 + 2 * j], side[4 + 2 * j]
            go = 2 * H_MLSTM * d
            lf_cols = _log_sigmoid(g_ref[j, :, go + H_MLSTM:go + 2 * H_MLSTM])
            lf_rows = _log_sigmoid(gt_ref[go + H_MLSTM:go + 2 * H_MLSTM, :])
            b_cols = _dot(cum_mask[d], lf_cols, precision=hi)
            pre[j, d] = dict(
                a_cols=g_ref[j, :, go:go + H_MLSTM] - b_cols,
                ig_rows=gt_ref[go:go + H_MLSTM, :],
                b_rows=_dot_nt(lf_rows, cum_mask[d], precision=hi),
                q=q_ref[j].astype(BF16), k=(k_ref[j] * (HEAD_DIM ** -0.5)).astype(BF16), vt=vt_ref[...])

    def pair_cols(hp):
        return slice(pair_w * hp, pair_w * (hp + 1))

    q_p = [pre[j, d]["q"][:, pair_cols(hp)] for j, d, hp in pairs]
    k_p = [pre[j, d]["k"][:, pair_cols(hp)] for j, d, hp in pairs]
    vt_p = [pre[j, d]["vt"][pair_cols(hp), :] for j, d, hp in pairs]
    c_p = [c_s[j, d, hp] for j, d, hp in pairs]
    n_p = [n_s[j, n_pairs * d + hp:n_pairs * d + hp + 1, :] for j, d, hp in pairs]
    zero_k = jnp.zeros((lc, pair_w), BF16)
    k_h = [jnp.where(lo_lane, k_p[pi], zero_k) if hh == 0 else jnp.where(lo_lane, zero_k, k_p[pi])
           for pi, hh in heads]

    def head_of(i):
        pi, hh = heads[i]
        j, d, hp = pairs[pi]
        return j, d, 2 * hp + hh

    b_row = [pre[head_of(i)[0], head_of(i)[1]]["b_rows"][head_of(i)[2]:head_of(i)[2] + 1, :] for i in rng]
    ig_row = [pre[head_of(i)[0], head_of(i)[1]]["ig_rows"][head_of(i)[2]:head_of(i)[2] + 1, :] for i in rng]
    a_col = [pre[head_of(i)[0], head_of(i)[1]]["a_cols"][:, head_of(i)[2]:head_of(i)[2] + 1] for i in rng]
    m_st = [m_s[head_of(i)[0], H_MLSTM * head_of(i)[1] + head_of(i)[2]:H_MLSTM * head_of(i)[1] + head_of(i)[2] + 1, :]
            for i in rng]
    bl = [b_row[i][:, lc - 1:lc] if head_of(i)[1] == 0 else b_row[i][:, 0:1] for i in rng]

    d_t = [jnp.where(keep_t[head_of(i)[1]], b_row[i] + a_col[i], NEG) for i in rng]
    inter = [b_row[i] + m_st[i] for i in rng]
    m_t = [jnp.maximum(inter[i], jnp.max(d_t[i], axis=0, keepdims=True)) for i in rng]
    s_t = [_dot_nt(k_h[i], q_p[heads[i][0]]) * jnp.exp(d_t[i] - m_t[i]) for i in rng]
    w_in = [jnp.exp(inter[i] - m_t[i]) for i in rng]
    num_t = [_dot(vt_p[heads[i][0]].astype(BF16), s_t[i].astype(BF16)) for i in rng]
    qc_t = [_dot_nt(c_p[pi].astype(BF16), q_p[pi]) for pi in range(len(pairs))]
    n_mat = [jnp.where((row8 == 0) & lo_lane, n_p[pi], jnp.where((row8 == 1) & ~lo_lane, n_p[pi], 0.0))
             for pi in range(len(pairs))]
    nq = [_dot_nt(n_mat[pi].astype(BF16), q_p[pi]) for pi in range(len(pairs))]
    den = [jnp.sum(s_t[i], axis=0, keepdims=True) + w_in[i] * nq[heads[i][0]][heads[i][1]:heads[i][1] + 1, :]
           for i in rng]
    inv = [1.0 / jnp.maximum(jnp.abs(den[i]), jnp.exp(-m_t[i])) for i in rng]
    h_t = []
    for pi in range(len(pairs)):
        i0, i1 = 2 * pi, 2 * pi + 1
        num = jnp.where(lo_row, num_t[i0], num_t[i1])
        h_t.append((num + jnp.where(lo_row, w_in[i0], w_in[i1]) * qc_t[pi]) * jnp.where(lo_row, inv[i0], inv[i1]))
    for j in range(sps):
        base = 2 * n_pairs * j
        hf_ref[j] = jnp.concatenate(h_t[base:base + n_pairs], axis=0)
        hb_ref[j] = jnp.concatenate(h_t[base + n_pairs:base + 2 * n_pairs], axis=0)

    g_row = [bl[i] - b_row[i] + ig_row[i] for i in rng]
    m_new = [jnp.maximum(bl[i] + m_st[i], jnp.max(g_row[i], axis=-1, keepdims=True)) for i in rng]
    wc = [jnp.exp(bl[i] + m_st[i] - m_new[i]) for i in rng]
    ws_row = [jnp.exp(g_row[i] - m_new[i]) for i in rng]
    upd = [_dot((vt_p[heads[i][0]] * ws_row[i]).astype(BF16), k_p[heads[i][0]]) for i in rng]
    for pi, (j, d, hp) in enumerate(pairs):
        i0, i1 = 2 * pi, 2 * pi + 1
        block = jnp.where(lo_row & lo_lane, upd[i0], jnp.where(~lo_row & ~lo_lane, upd[i1], 0.0))
        c_s[j, d, hp] = jnp.where(lo_row, wc[i0], wc[i1]) * c_p[pi] + block
        ws_mat = jnp.where(row8 == 0, ws_row[i0], jnp.where(row8 == 1, ws_row[i1], 0.0))
        k_sum = _dot(ws_mat.astype(BF16), k_p[pi])
        row = n_pairs * d + hp
        n_s[j, row:row + 1, :] = (jnp.where(lo_lane, wc[i0], wc[i1]) * n_p[pi]
                                  + jnp.where(lo_lane, k_sum[0:1, :], k_sum[1:2, :]))
    for i in rng:
        j, d, hd = head_of(i)
        m_s[j, H_MLSTM * d + hd:H_MLSTM * d + hd + 1, :] = m_new[i]

    @pl.when(c == n_chunks - 1)
    def _():
        cout_ref[...] = c_s[...]
        nout_ref[...] = n_s[...]
        mout_ref[...] = m_s[...]


def _pair_states(c):
    b = c.shape[0]
    c = c.reshape(b, 2, H_MLSTM // 2, 2, HEAD_DIM, HEAD_DIM)
    zero = jnp.zeros_like(c[:, :, :, 0])
    top = jnp.concatenate([c[:, :, :, 0], zero], axis=-1)
    bottom = jnp.concatenate([zero, c[:, :, :, 1]], axis=-1)
    return jnp.concatenate([top, bottom], axis=-2)


def _unpair_states(cp):
    b = cp.shape[0]
    first = cp[:, :, :, :HEAD_DIM, :HEAD_DIM]
    second = cp[:, :, :, HEAD_DIM:, HEAD_DIM:]
    return jnp.stack([first, second], axis=3).reshape(b, 2, H_MLSTM, HEAD_DIM, HEAD_DIM)


def _mlstm(p, g, gt, vo, c0, n0, m0, tril, triu, n_seq, seq_len):
    lc = MLSTM_CHUNK
    nc = seq_len // lc
    n_st = 2 * H_MLSTM
    n_pairs = H_MLSTM // 2
    pair_w = 2 * HEAD_DIM
    sps = SEQS_PER_STEP
    p3 = p.reshape(n_seq, seq_len, P_COLS)
    g3 = g.reshape(n_seq, seq_len, N_GATE_COLS)

    def fwd(c):
        return c

    def bwd(c):
        return nc - 1 - c

    def side(chunk):
        tok = lambda col: (lambda b, c: (b, chunk(c), col))
        specs = [
            pl.BlockSpec((sps, lc, C_W), tok(QC_BLK)),
            pl.BlockSpec((sps, lc, C_W), tok(KC_BLK)),
            pl.BlockSpec((sps, lc, N_GATE_COLS), tok(0)),
        ]
        for j in range(sps):
            lanes = lambda b, c, j=j: (0, (b * sps + j) * nc + chunk(c))
            specs += [pl.BlockSpec((N_GATE_COLS, lc), lanes), pl.BlockSpec((C_W, lc), lanes)]
        return specs

    state_specs = [
        pl.BlockSpec((sps, 2, n_pairs, pair_w, pair_w), lambda b, c: (b, 0, 0, 0, 0)),
        pl.BlockSpec((sps, 2 * n_pairs, pair_w), lambda b, c: (b, 0, 0)),
        pl.BlockSpec((sps, n_st, 1), lambda b, c: (b, 0, 0)),
    ]
    tri_spec = pl.BlockSpec((lc, lc), lambda b, c: (0, 0))
    operands = [p3, p3, g3] + [gt, vo] * sps
    hf, hb, c_out, n_out, m_out = pl.pallas_call(
        functools.partial(_mlstm_kernel, n_chunks=nc),
        grid=(n_seq // sps, nc),
        in_specs=side(fwd) + side(bwd) + state_specs + [tri_spec, tri_spec],
        out_specs=[
            pl.BlockSpec((sps, C_W, lc), lambda b, c: (b, 0, c)),
            pl.BlockSpec((sps, C_W, lc), lambda b, c: (b, 0, nc - 1 - c)),
        ] + state_specs,
        out_shape=[
            jax.ShapeDtypeStruct((n_seq, C_W, seq_len), F32),
            jax.ShapeDtypeStruct((n_seq, C_W, seq_len), F32),
            jax.ShapeDtypeStruct((n_seq, 2, n_pairs, pair_w, pair_w), F32),
            jax.ShapeDtypeStruct((n_seq, 2 * n_pairs, pair_w), F32),
            jax.ShapeDtypeStruct((n_seq, n_st, 1), F32),
        ],
        scratch_shapes=[
            pltpu.VMEM((sps, 2, n_pairs, pair_w, pair_w), F32),
            pltpu.VMEM((sps, 2 * n_pairs, pair_w), F32),
            pltpu.VMEM((sps, n_st, 1), F32),
        ],
        compiler_params=_params(("arbitrary", "arbitrary")),
        name="mlstm",
    )(*operands, *operands, _pair_states(c0), n0.reshape(n_seq, 2 * n_pairs, pair_w), m0.reshape(n_seq, n_st, 1),
      tril, triu)
    return (hf, hb, _unpair_states(c_out), n_out.reshape(n_seq, 2, H_MLSTM, HEAD_DIM),
            m_out.reshape(n_seq, 2, H_MLSTM))


def _head_norm(y, g, ones_blk):
    ysq = y * y
    hi = ysq.astype(BF16)
    lo = (ysq - hi.astype(F32)).astype(BF16)
    ss = _dot(hi, ones_blk) + _dot(lo, ones_blk)
    return y * lax.rsqrt(ss * (1.0 / HEAD_DIM) + EPS) * g


def _merge_kernel(att_ref, four_ref, hf_ref, hb_ref, oc_ref, x_ref, mod_ref, gh_ref, ghm_ref, wo_ref, g2_ref, wrt_ref,
                  ones_ref, xo_ref, h2_ref, afft_ref):
    gh = gh_ref[0]
    ya = _head_norm(att_ref[...], gh[:, :A_W], ones_ref[...])
    yf = _head_norm(four_ref[...], gh[:, A_W:A_W + B_W], ones_ref[:B_W, :B_W])
    mem = hf_ref[0] + hb_ref[0]
    heads = []
    for hd in range(H_MLSTM):
        y = mem[HEAD_DIM * hd:HEAD_DIM * (hd + 1), :]
        heads.append(y * lax.rsqrt(jnp.mean(y * y, axis=0, keepdims=True) + EPS))
    ym_t = jnp.concatenate(heads, axis=0) * ghm_ref[0] * jax.nn.sigmoid(oc_ref[...])
    out = (_dot(ya.astype(BF16), wo_ref[0, :A_W, :])
           + _dot(yf.astype(BF16), wo_ref[0, A_W:A_W + B_W, :])
           + _dot_tn(ym_t.astype(BF16), wo_ref[0, A_W + B_W:, :]))
    x = x_ref[...] + mod_ref[0, 2:3, :] * out
    xo_ref[...] = x
    y2 = x * lax.rsqrt(jnp.mean(x * x, axis=-1, keepdims=True) + EPS) * g2_ref[0]
    h2 = (y2 * (1.0 + mod_ref[0, 4:5, :]) + mod_ref[0, 3:4, :]).astype(BF16)
    h2_wide = h2.astype(F32)
    tm = h2_wide.shape[0]
    for s in range(N_SLAB):
        r0 = (tm + ROW_PAD) * s
        h2_ref[r0:r0 + tm, :] = h2_wide[:, LANE * s:LANE * (s + 1)]
        h2_ref[r0 + tm:r0 + tm + ROW_PAD, :] = jnp.zeros((ROW_PAD, LANE), F32)
    logits = _dot_nt(wrt_ref[0], h2)
    e = jnp.exp(logits - jnp.max(logits, axis=0, keepdims=True))
    afft_ref[...] = e / jnp.sum(e, axis=0, keepdims=True)


def _merge(att, four, hf, hb, vo, x2d, mod, layer, g_head, g_mem, w_out_bf, g2, w_rt_bf, ones_blk, n_seq, seq_len,
           mod_seq_len):
    t = x2d.shape[0]
    tm = MERGE_TILE
    tiles_per_mod = mod_seq_len // tm
    tiles_per_seq = seq_len // tm
    row = lambda i: (i, 0)
    lay = lambda i: (layer, 0, 0)
    mem = lambda i: (i // tiles_per_seq, 0, i % tiles_per_seq)
    return pl.pallas_call(
        _merge_kernel,
        grid=(t // tm,),
        in_specs=[
            pl.BlockSpec((tm, A_W), row),
            pl.BlockSpec((tm, B_W), row),
            pl.BlockSpec((1, C_W, tm), mem),
            pl.BlockSpec((1, C_W, tm), mem),
            pl.BlockSpec((C_W, tm), lambda i: (1, i)),
            pl.BlockSpec((tm, D_MODEL), row),
            pl.BlockSpec((1, 6, D_MODEL), lambda i: (i // tiles_per_mod, 0, 0)),
            pl.BlockSpec((1, 1, D_MODEL), lay),
            pl.BlockSpec((1, C_W, tm), lay),
            pl.BlockSpec((1, D_MODEL, D_MODEL), lay),
            pl.BlockSpec((1, 1, D_MODEL), lay),
            pl.BlockSpec((1, N_EXPERTS, D_MODEL), lay),
            pl.BlockSpec((A_W, A_W), lambda i: (0, 0)),
        ],
        out_specs=[
            pl.BlockSpec((tm, D_MODEL), row),
            pl.BlockSpec((N_SLAB * (tm + ROW_PAD), LANE), row),
            pl.BlockSpec((N_EXPERTS, tm), lambda i: (0, i)),
        ],
        out_shape=[
            jax.ShapeDtypeStruct((t, D_MODEL), F32),
            jax.ShapeDtypeStruct((n_seq * tiles_per_seq * N_SLAB * (tm + ROW_PAD), LANE), F32),
            jax.ShapeDtypeStruct((N_EXPERTS, t), F32),
        ],
        compiler_params=_params(("arbitrary",)),
        name="merge",
    )(att, four, hf, hb, vo, x2d, mod, g_head, g_mem, w_out_bf, g2, w_rt_bf, ones_blk)


BISECT_STEPS = 48
TOKEN_CHUNK = 1024


TOKEN_SPLIT = 64


def _route_kernel(aff_ref, triu_ref, idx_ref, off_ref, gs_ref, sp_ref, *, ns, seq_len, cap):
    seqs = range(ns)
    aff = [aff_ref[:, seq_len * j:seq_len * (j + 1)] for j in seqs]

    def body(_, bounds):
        out = []
        for j in seqs:
            lo, hi = bounds[j]
            mid = 0.5 * (lo + hi)
            ge = jnp.sum(jnp.where(aff[j] >= mid, 1.0, 0.0), axis=1, keepdims=True) >= cap
            out.append((jnp.where(ge, mid, lo), jnp.where(ge, hi, mid)))
        return tuple(out)

    start = (jnp.zeros((N_EXPERTS, 1), F32), jnp.full((N_EXPERTS, 1), 2.0, F32))
    bounds = lax.fori_loop(0, BISECT_STEPS, body, tuple(start for _ in seqs))
    thr = [jnp.max(jnp.where(aff[j] < bounds[j][1], aff[j], -1.0), axis=1, keepdims=True) for j in seqs]
    need = [cap - jnp.sum(jnp.where(aff[j] > thr[j], 1.0, 0.0), axis=1, keepdims=True) for j in seqs]
    triu = triu_ref[...]
    eq_carry = [jnp.zeros((N_EXPERTS, 1), F32) for _ in seqs]
    pos_carry = [jnp.zeros((N_EXPERTS, 1), F32) for _ in seqs]
    for b in range(seq_len // 128):
        for j in seqs:
            blk = aff[j][:, 128 * b:128 * (b + 1)]
            eq = blk == thr[j]
            eq_f = jnp.where(eq, 1.0, 0.0)
            eq_inc = _dot(eq_f.astype(BF16), triu) + eq_carry[j]
            sel = (blk > thr[j]) | (eq & (eq_inc - eq_f < need[j]))
            sel_f = jnp.where(sel, 1.0, 0.0)
            pos_inc = _dot(sel_f.astype(BF16), triu) + pos_carry[j]
            t0 = seq_len * j + 128 * b
            sp_ref[:, t0:t0 + 128] = jnp.where(sel, pos_inc - sel_f, -1.0).astype(jnp.int32)
            eq_carry[j] = eq_inc[:, 127:128]
            pos_carry[j] = pos_inc[:, 127:128]

    tc = min(seq_len, TOKEN_CHUNK)
    slot = lax.broadcasted_iota(jnp.int32, (cap, tc), 0)
    part = lax.broadcasted_iota(jnp.int32, (8, tc), 0)
    tok = lax.broadcasted_iota(jnp.int32, (1, tc), 1).astype(F32)
    chunks = range(0, seq_len, tc)
    tok_hi = [jnp.floor((tok + float(t0)) * (1.0 / TOKEN_SPLIT)) for t0 in chunks]
    tok_lo = [tok + float(t0) - TOKEN_SPLIT * hi for t0, hi in zip(chunks, tok_hi)]

    def per_expert(e, carry):
        for j in seqs:
            acc = jnp.zeros((cap, 8), F32)
            for ci, t0 in enumerate(chunks):
                cols = slice(seq_len * j + t0, seq_len * j + t0 + tc)
                onehot = jnp.where(slot == sp_ref[pl.ds(e, 1), cols], 1.0, 0.0).astype(BF16)
                a = aff_ref[pl.ds(e, 1), cols]
                a_hi = a.astype(BF16).astype(F32)
                a_mid = (a - a_hi).astype(BF16).astype(F32)
                a_lo = a - a_hi - a_mid
                vals = jnp.where(part == 0, tok_hi[ci], jnp.where(part == 1, tok_lo[ci], jnp.where(
                    part == 2, a_hi, jnp.where(part == 3, a_mid, jnp.where(part == 4, a_lo, 0.0)))))
                acc = acc + _dot_nt(onehot, vals.astype(BF16))
            token = TOKEN_SPLIT * acc[:, 0:1] + acc[:, 1:2]
            tile = jnp.floor(token * (1.0 / MERGE_TILE))
            idx_ref[N_EXPERTS * j + e] = token.astype(jnp.int32)
            off_ref[N_EXPERTS * j + e] = (tile * (N_SLAB * (MERGE_TILE + ROW_PAD) - MERGE_TILE) + token).astype(jnp.int32)
            gs_ref[e, cap * j:cap * (j + 1)] = acc[:, 2:3] + acc[:, 3:4] + acc[:, 4:5]
        return carry

    lax.fori_loop(0, N_EXPERTS, per_expert, 0)


def _route(afft, triu_bf, n_seq, seq_len, cap, ns):
    idx, off, gs = pl.pallas_call(
        functools.partial(_route_kernel, ns=ns, seq_len=seq_len, cap=cap),
        grid=(n_seq // ns,),
        in_specs=[
            pl.BlockSpec((N_EXPERTS, ns * seq_len), lambda s: (0, s)),
            pl.BlockSpec((128, 128), lambda s: (0, 0)),
        ],
        out_specs=[
            pl.BlockSpec((ns * N_EXPERTS, cap, 1), lambda s: (s, 0, 0)),
            pl.BlockSpec((ns * N_EXPERTS, cap, 1), lambda s: (s, 0, 0)),
            pl.BlockSpec((N_EXPERTS, ns * cap, 1), lambda s: (0, s, 0)),
        ],
        out_shape=[
            jax.ShapeDtypeStruct((n_seq * N_EXPERTS, cap, 1), jnp.int32),
            jax.ShapeDtypeStruct((n_seq * N_EXPERTS, cap, 1), jnp.int32),
            jax.ShapeDtypeStruct((N_EXPERTS, n_seq * cap, 1), F32),
        ],
        scratch_shapes=[pltpu.VMEM((N_EXPERTS, ns * seq_len), jnp.int32)],
        compiler_params=_params(("arbitrary",)),
        name="route",
    )(afft, triu_bf)
    return idx.reshape(n_seq * N_EXPERTS, cap), off.reshape(n_seq * N_EXPERTS, cap), gs


ROW_COPIES = 8


def _gather_kernel(off_ref, src_ref, xs_ref, tile_ref, *, eb, cap):
    ei = pl.program_id(1)
    tp = cap + ROW_PAD

    def per_expert(ee, carry):
        e = ei * eb + ee

        def rows(g, c):
            for u in range(ROW_COPIES):
                slot = g * ROW_COPIES + u
                row = src_ref[pl.ds(off_ref[e, slot], N_SLAB, stride=MERGE_TILE + ROW_PAD), :]
                tile_ref[pl.ds(slot, N_SLAB, stride=tp), :] = row
            return c

        lax.fori_loop(0, cap // ROW_COPIES, rows, 0)
        for s in range(N_SLAB):
            xs_ref[ee, :, LANE * s:LANE * (s + 1)] = tile_ref[tp * s:tp * s + cap, :].astype(BF16)
        return carry

    lax.fori_loop(0, eb, per_expert, 0)


def _gather(off, h2_slabs, n_seq, cap, eb):
    return pl.pallas_call(
        functools.partial(_gather_kernel, eb=eb, cap=cap),
        grid=(n_seq, N_EXPERTS // eb),
        in_specs=[
            pl.BlockSpec((N_EXPERTS, cap), lambda s, e: (s, 0), memory_space=pltpu.SMEM),
            pl.BlockSpec((h2_slabs.shape[0] // n_seq, LANE), lambda s, e: (s, 0)),
        ],
        out_specs=pl.BlockSpec((eb, cap, D_MODEL), lambda s, e: (e, s, 0)),
        out_shape=jax.ShapeDtypeStruct((N_EXPERTS, n_seq * cap, D_MODEL), BF16),
        scratch_shapes=[pltpu.VMEM((N_SLAB * (cap + ROW_PAD), LANE), F32)],
        compiler_params=_params(("arbitrary", "arbitrary"), vmem_mib=56),
        name="gather",
    )(off, h2_slabs)


def _expert_kernel(xc_ref, xl_ref, gc_ref, gl_ref, wg_ref, wu_ref, wd_ref, yc_ref, yl_ref, *, n_f):
    f = pl.program_id(1)

    @pl.when(f == 0)
    def _():
        yc_ref[...] = jnp.zeros_like(yc_ref)
        yl_ref[...] = jnp.zeros_like(yl_ref)

    wg = wg_ref[0, 0].astype(BF16)
    wu = wu_ref[0, 0].astype(BF16)
    wd = wd_ref[0, 0].astype(BF16)
    for x_ref, y_ref in ((xc_ref, yc_ref), (xl_ref, yl_ref)):
        x = x_ref[0]
        mid = (_silu(_dot(x, wg)) * _dot(x, wu)).astype(BF16)
        y_ref[0] += _dot(mid, wd)

    @pl.when(f == n_f - 1)
    def _():
        yc_ref[0] = yc_ref[0] * gc_ref[0]
        yl_ref[0] = yl_ref[0] * gl_ref[0]


def _experts(xs_c, xs_l, gs_c, gs_l, w_g, w_u, w_d, layer):
    rc, rl = xs_c.shape[1], xs_l.shape[1]
    tf = 512
    n_f = EXPERT_FF // tf
    return pl.pallas_call(
        functools.partial(_expert_kernel, n_f=n_f),
        grid=(N_EXPERTS, n_f),
        in_specs=[
            pl.BlockSpec((1, rc, D_MODEL), lambda e, f: (e, 0, 0)),
            pl.BlockSpec((1, rl, D_MODEL), lambda e, f: (e, 0, 0)),
            pl.BlockSpec((1, rc, 1), lambda e, f: (e, 0, 0)),
            pl.BlockSpec((1, rl, 1), lambda e, f: (e, 0, 0)),
            pl.BlockSpec((1, 1, D_MODEL, tf), lambda e, f: (layer, e, 0, f)),
            pl.BlockSpec((1, 1, D_MODEL, tf), lambda e, f: (layer, e, 0, f)),
            pl.BlockSpec((1, 1, tf, D_MODEL), lambda e, f: (layer, e, f, 0)),
        ],
        out_specs=[
            pl.BlockSpec((1, rc, D_MODEL), lambda e, f: (e, 0, 0)),
            pl.BlockSpec((1, rl, D_MODEL), lambda e, f: (e, 0, 0)),
        ],
        out_shape=[
            jax.ShapeDtypeStruct((N_EXPERTS, rc, D_MODEL), F32),
            jax.ShapeDtypeStruct((N_EXPERTS, rl, D_MODEL), F32),
        ],
        compiler_params=_params(("arbitrary", "arbitrary"), vmem_mib=56),
        name="experts",
    )(xs_c, xs_l, gs_c, gs_l, w_g, w_u, w_d)


def _scatter_kernel(idx_ref, y_ref, x_ref, mod_ref, gf_ref, o_ref, acc_ref, tile_ref, *, eb, n_e, pitch, cap, tm,
                    final):
    step = pl.program_id(1)
    tp = cap + ROW_PAD

    @pl.when(step == 0)
    def _():
        acc_ref[...] = jnp.zeros_like(acc_ref)

    @pl.when(step < n_e)
    def _():
        def per_expert(ee, carry):
            e = step * eb + ee
            for s in range(N_SLAB):
                tile_ref[tp * s:tp * s + cap, :] = y_ref[ee, :, LANE * s:LANE * (s + 1)]

            def rows(g, c):
                dst = [idx_ref[e, g * ROW_COPIES + u] for u in range(ROW_COPIES)]
                new = [acc_ref[pl.ds(dst[u], N_SLAB, stride=pitch), :]
                       + tile_ref[pl.ds(g * ROW_COPIES + u, N_SLAB, stride=tp), :] for u in range(ROW_COPIES)]
                for u in range(ROW_COPIES):
                    acc_ref[pl.ds(dst[u], N_SLAB, stride=pitch), :] = new[u]
                return c

            lax.fori_loop(0, cap // ROW_COPIES, rows, 0)
            return carry

        lax.fori_loop(0, eb, per_expert, 0)

    @pl.when(step >= n_e)
    def _():
        base = (step - n_e) * tm
        moe = jnp.concatenate(
            [acc_ref[pl.ds(pl.multiple_of(pitch * s + base, 8), tm), :] for s in range(N_SLAB)], axis=1)
        x = x_ref[...] + mod_ref[0, 5:6, :] * moe
        if final:
            x = x * lax.rsqrt(jnp.mean(x * x, axis=-1, keepdims=True) + EPS) * gf_ref[...]
        o_ref[...] = x


def _scatter(idx, ys, x2d, mod, g_final, n_seq, seq_len, cap, eb, tm, final):
    n_e = N_EXPERTS // eb
    n_out = seq_len // tm
    pitch = seq_len + ROW_PAD
    out_blk = lambda s, j: (s * n_out + jnp.maximum(j - n_e, 0), 0)
    return pl.pallas_call(
        functools.partial(_scatter_kernel, eb=eb, n_e=n_e, pitch=pitch, cap=cap, tm=tm, final=final),
        grid=(n_seq, n_e + n_out),
        in_specs=[
            pl.BlockSpec((N_EXPERTS, cap), lambda s, j: (s, 0), memory_space=pltpu.SMEM),
            pl.BlockSpec((eb, cap, D_MODEL), lambda s, j: (jnp.minimum(j, n_e - 1), s, 0)),
            pl.BlockSpec((tm, D_MODEL), out_blk),
            pl.BlockSpec((1, 6, D_MODEL), lambda s, j: (s, 0, 0)),
            pl.BlockSpec((1, D_MODEL), lambda s, j: (0, 0)),
        ],
        out_specs=pl.BlockSpec((tm, D_MODEL), out_blk),
        out_shape=jax.ShapeDtypeStruct((n_seq * seq_len, D_MODEL), F32),
        scratch_shapes=[
            pltpu.VMEM((N_SLAB * pitch, LANE), F32),
            pltpu.VMEM((N_SLAB * (cap + ROW_PAD), LANE), F32),
        ],
        compiler_params=_params(("arbitrary", "arbitrary"), vmem_mib=56),
        name="scatter",
    )(idx, ys, x2d, mod, g_final)


def kernel(x_prompt, x_sample, c, cache_k, cache_v, state_C, state_n, state_m, c_ctx, w_ada, b_ada, g_norm1, g_norm2, w_in, b_gates, rpb, w_fourier, g_head, w_out, w_router, w_exp_gate, w_exp_up, w_exp_down, g_final):
    n_ctx, len_ctx, _ = x_prompt.shape
    n_lat, len_lat, _ = x_sample.shape
    past = cache_k.shape[2]
    cap_ctx = CAPACITY_FACTOR * len_ctx // N_EXPERTS
    cap_lat = CAPACITY_FACTOR * len_lat // N_EXPERTS

    w_in_bf = w_in.astype(BF16)
    w_gt_bf = jnp.swapaxes(w_in[:, :, P_COLS:], 1, 2).astype(BF16)
    vc0, oc0 = VC_BLK * C_W, OC_BLK * C_W
    w_vo_bf = jnp.swapaxes(jnp.concatenate([w_in[:, :, vc0:vc0 + C_W], w_in[:, :, oc0:oc0 + C_W]], axis=2),
                           1, 2).astype(BF16)
    bg_row = b_gates.reshape(DEPTH, 1, N_GATE_COLS).astype(F32)
    bg_col = b_gates.reshape(DEPTH, N_GATE_COLS, 1).astype(F32)
    w_out_bf = w_out.astype(BF16)
    w_rt_bf = jnp.swapaxes(w_router, 1, 2).astype(BF16)
    g1 = g_norm1.reshape(DEPTH, 1, D_MODEL)
    g2 = g_norm2.reshape(DEPTH, 1, D_MODEL)
    gh = g_head.reshape(DEPTH, 1, D_MODEL)
    g_mem = jnp.broadcast_to(gh[:, 0, A_W + B_W:, None], (DEPTH, C_W, MERGE_TILE))
    eye_g = jnp.eye(G_FOURIER, dtype=F32)
    wf_blk = jnp.einsum("lgcd,gh->lgchd", w_fourier, eye_g).reshape(DEPTH, B_W, B_W).astype(BF16)

    csc = _channel_dft()
    dft_ctx = _dft_mats(len_ctx)
    dft_side = _dft_mats(GRID_W)
    tw_cos, tw_sin = _twiddles(GRID_W, len_lat)
    bias_tiles = _nbr_bias_tiles(rpb)
    r = np.arange(MLSTM_CHUNK)
    tril = jnp.asarray(r[:, None] >= r[None, :], F32)
    triu = jnp.asarray(r[:, None] <= r[None, :], F32)
    triu_bf = triu.astype(BF16)
    hidx = np.arange(A_W) // HEAD_DIM
    ones_blk = jnp.asarray(hidx[:, None] == hidx[None, :], BF16)

    cvecs = jnp.concatenate([c_ctx[None, :], c, jnp.zeros((8 - 1 - n_lat, D_MODEL), F32)], axis=0)
    mod_all = _modulation(cvecs, w_ada, b_ada).reshape(DEPTH, 8, 6, D_MODEL)

    cache_k4 = cache_k.reshape(n_lat, DEPTH, past, A_W)
    cache_v4 = cache_v.reshape(n_lat, DEPTH, past, A_W)
    zero_c = jnp.zeros((n_ctx, 2, H_MLSTM, HEAD_DIM, HEAD_DIM), F32)
    zero_n = jnp.zeros((n_ctx, 2, H_MLSTM, HEAD_DIM), F32)
    zero_m = jnp.zeros((n_ctx, 2, H_MLSTM), F32)

    xc = x_prompt.reshape(n_ctx * len_ctx, D_MODEL)
    xl = x_sample.reshape(n_lat * len_lat, D_MODEL)
    gf = g_final.reshape(1, D_MODEL)
    ks, vs, cs, ns, ms = [], [], [], [], []
    for l in range(DEPTH):
        mod_c = mod_all[l, 0:1]
        mod_l = mod_all[l, 1:1 + n_lat]

        pc, gc, gtc, voc, abc = _inproj(xc, mod_c, l, g1, w_in_bf, w_gt_bf, w_vo_bf, bg_row, bg_col, csc,
                                        n_ctx * len_ctx)
        att_c = _ctx_attention(pc, n_ctx, len_ctx)
        four_c = _fourier(abc, dft_ctx[0], dft_ctx[1], wf_blk, l, n_ctx, len_ctx)
        hf_c, hb_c, c_new, n_new, m_new = _mlstm(pc, gc, gtc, voc, zero_c, zero_n, zero_m, tril, triu, n_ctx, len_ctx)
        xc, h2c, affc = _merge(att_c, four_c, hf_c, hb_c, voc, xc, mod_c, l, gh, g_mem, w_out_bf, g2, w_rt_bf,
                               ones_blk, n_ctx, len_ctx, n_ctx * len_ctx)
        ks.append(pc[:, A_W:2 * A_W].reshape(n_ctx, len_ctx, H_ATT, HEAD_DIM))
        vs.append(pc[:, 2 * A_W:3 * A_W].reshape(n_ctx, len_ctx, H_ATT, HEAD_DIM))
        cs.append(c_new)
        ns.append(n_new)
        ms.append(m_new)

        pq, gq, gtq, voq, abq = _inproj(xl, mod_l, l, g1, w_in_bf, w_gt_bf, w_vo_bf, bg_row, bg_col, csc, len_lat)
        att_l = _nbr_attention(pq, cache_k4, cache_v4, bias_tiles, l, n_lat, len_lat)
        four_l = _fourier_grid(abq, dft_side[0], dft_side[1], tw_cos, tw_sin, wf_blk, l, n_lat, GRID_W)
        hf_l, hb_l, _, _, _ = _mlstm(pq, gq, gtq, voq, state_C[:, l], state_n[:, l], state_m[:, l], tril, triu,
                                     n_lat, len_lat)
        xl, h2l, affl = _merge(att_l, four_l, hf_l, hb_l, voq, xl, mod_l, l, gh, g_mem, w_out_bf, g2, w_rt_bf,
                               ones_blk, n_lat, len_lat, len_lat)

        last = l == DEPTH - 1
        idx_c, off_c, gs_c = _route(affc, triu_bf, n_ctx, len_ctx, cap_ctx, n_ctx)
        idx_l, off_l, gs_l = _route(affl, triu_bf, n_lat, len_lat, cap_lat, 1)
        xs_c = _gather(off_c, h2c, n_ctx, cap_ctx, N_EXPERTS)
        xs_l = _gather(off_l, h2l, n_lat, cap_lat, 4)
        ys_c, ys_l = _experts(xs_c, xs_l, gs_c, gs_l, w_exp_gate, w_exp_up, w_exp_down, l)
        xc = _scatter(idx_c, ys_c, xc, jnp.broadcast_to(mod_c, (n_ctx, 6, D_MODEL)), gf, n_ctx, len_ctx, cap_ctx,
                      N_EXPERTS, len_ctx, last)
        xl = _scatter(idx_l, ys_l, xl, mod_l, gf, n_lat, len_lat, cap_lat, 2, 512, last)

    y_prompt = xc.reshape(n_ctx, len_ctx, D_MODEL)
    y_sample = xl.reshape(n_lat, len_lat, D_MODEL)
    return (y_prompt, y_sample, jnp.stack(ks, axis=1), jnp.stack(vs, axis=1), jnp.stack(cs, axis=1),
            jnp.stack(ns, axis=1), jnp.stack(ms, axis=1))
```

```python
import functools

import numpy as np
import jax
import jax.numpy as jnp
from jax import lax
from jax.experimental import pallas as pl
from jax.experimental.pallas import tpu as pltpu

F32 = jnp.float32
BF16 = jnp.bfloat16

D_MODEL = 1024
DEPTH = 2
HEAD_DIM = 64
H_ATT = 8
G_FOURIER = 4
H_MLSTM = 4
A_W = H_ATT * HEAD_DIM
B_W = G_FOURIER * HEAD_DIM
C_W = H_MLSTM * HEAD_DIM
N_GATE_COLS = 16
P_COLS = 3 * A_W + B_W + 4 * C_W
IN_COLS = P_COLS + N_GATE_COLS
GRID_W = 64
WIN_R = 8
WIN_C = 16
MLSTM_CHUNK = 128
N_EXPERTS = 16
CAPACITY_FACTOR = 2
EXPERT_FF = 2 * D_MODEL
EPS = 1e-6
NEG = -1e30

UB_OFF = 3 * A_W
QC_BLK, KC_BLK, VC_BLK, OC_BLK = 7, 8, 9, 10

LANE = 128
N_SLAB = D_MODEL // LANE
MERGE_TILE = 256

NT_DIMS = (((1,), (1,)), ((), ()))
TN_DIMS = (((0,), (0,)), ((), ()))
MIB = 1024 * 1024


def _dot(a, b, precision=None):
    return jnp.dot(a, b, preferred_element_type=F32, precision=precision)


def _dot_nt(a, b, precision=None):
    return lax.dot_general(a, b, NT_DIMS, preferred_element_type=F32, precision=precision)


def _dot_tn(a, b):
    return lax.dot_general(a, b, TN_DIMS, preferred_element_type=F32)


def _params(sem, vmem_mib=48):
    return pltpu.CompilerParams(dimension_semantics=sem, vmem_limit_bytes=vmem_mib * MIB)


def _silu(x):
    return x * jax.nn.sigmoid(x)


def _log_sigmoid(x):
    return jnp.minimum(x, 0.0) - jnp.log1p(jnp.exp(-jnp.abs(x)))


def _mod_kernel(c_ref, w_ref, b_ref, o_ref):
    s = _silu(c_ref[...]).astype(BF16)
    o_ref[0] = _dot(s, w_ref[0].astype(BF16)) + b_ref[0]


def _modulation(cvecs, w_ada, b_ada):
    depth = w_ada.shape[0]
    tn = 1024
    return pl.pallas_call(
        _mod_kernel,
        grid=(depth, 6 * D_MODEL // tn),
        in_specs=[
            pl.BlockSpec((8, D_MODEL), lambda l, j: (0, 0)),
            pl.BlockSpec((1, D_MODEL, tn), lambda l, j: (l, 0, j)),
            pl.BlockSpec((1, 1, tn), lambda l, j: (l, 0, j)),
        ],
        out_specs=pl.BlockSpec((1, 8, tn), lambda l, j: (l, 0, j)),
        out_shape=jax.ShapeDtypeStruct((depth, 8, 6 * D_MODEL), F32),
        compiler_params=_params(("arbitrary", "arbitrary")),
        name="modulation",
    )(cvecs, w_ada, b_ada.reshape(depth, 1, 6 * D_MODEL))


def _inproj_kernel(x_ref, mod_ref, g1_ref, w_ref, wgt_ref, wvo_ref, bgr_ref, bgc_ref, csc_ref,
                   p_ref, g_ref, gt_ref, vo_ref, ab_ref):
    x = x_ref[...]
    y = x * lax.rsqrt(jnp.mean(x * x, axis=-1, keepdims=True) + EPS) * g1_ref[0]
    h = (y * (1.0 + mod_ref[0, 1:2, :]) + mod_ref[0, 0:1, :]).astype(BF16)
    for j in range(0, P_COLS, 256):
        pj = _dot(h, w_ref[0, :, j:j + 256])
        p_ref[:, j:j + 256] = pj
        if j == UB_OFF:
            ab_ref[...] = _dot(pj.astype(BF16), csc_ref[...]).astype(BF16)
    g_ref[...] = _dot(h, w_ref[0, :, P_COLS:IN_COLS]) + bgr_ref[0]
    gt_ref[...] = _dot_nt(wgt_ref[0], h) + bgc_ref[0]
    vo_ref[...] = _dot_nt(wvo_ref[0], h)


def _inproj(x2d, mod, layer, g1, w_in_bf, w_gt_bf, w_vo_bf, bg_row, bg_col, csc, seq_len):
    t = x2d.shape[0]
    tm = 256
    tiles_per_seq = seq_len // tm
    return pl.pallas_call(
        _inproj_kernel,
        grid=(t // tm,),
        in_specs=[
            pl.BlockSpec((tm, D_MODEL), lambda i: (i, 0)),
            pl.BlockSpec((1, 6, D_MODEL), lambda i: (i // tiles_per_seq, 0, 0)),
            pl.BlockSpec((1, 1, D_MODEL), lambda i: (layer, 0, 0)),
            pl.BlockSpec((1, D_MODEL, IN_COLS), lambda i: (layer, 0, 0)),
            pl.BlockSpec((1, N_GATE_COLS, D_MODEL), lambda i: (layer, 0, 0)),
            pl.BlockSpec((1, 2 * C_W, D_MODEL), lambda i: (layer, 0, 0)),
            pl.BlockSpec((1, 1, N_GATE_COLS), lambda i: (layer, 0, 0)),
            pl.BlockSpec((1, N_GATE_COLS, 1), lambda i: (layer, 0, 0)),
            pl.BlockSpec((B_W, 2 * B_W), lambda i: (0, 0)),
        ],
        out_specs=[
            pl.BlockSpec((tm, P_COLS), lambda i: (i, 0)),
            pl.BlockSpec((tm, N_GATE_COLS), lambda i: (i, 0)),
            pl.BlockSpec((N_GATE_COLS, tm), lambda i: (0, i)),
            pl.BlockSpec((2 * C_W, tm), lambda i: (0, i)),
            pl.BlockSpec((tm, 2 * B_W), lambda i: (i, 0)),
        ],
        out_shape=[
            jax.ShapeDtypeStruct((t, P_COLS), F32),
            jax.ShapeDtypeStruct((t, N_GATE_COLS), F32),
            jax.ShapeDtypeStruct((N_GATE_COLS, t), F32),
            jax.ShapeDtypeStruct((2 * C_W, t), F32),
            jax.ShapeDtypeStruct((t, 2 * B_W), BF16),
        ],
        compiler_params=_params(("arbitrary",)),
        name="inproj",
    )(x2d, mod, g1, w_in_bf, w_gt_bf, w_vo_bf, bg_row, bg_col, csc)


def _ctx_attn_kernel(q_ref, k_ref, v_ref, o_ref):
    scale = HEAD_DIM ** -0.5
    heads = range(H_ATT)
    sl = [slice(HEAD_DIM * h, HEAD_DIM * (h + 1)) for h in heads]
    s = [_dot_nt((q_ref[:, sl[h]] * scale).astype(BF16), k_ref[:, sl[h]].astype(BF16)) for h in heads]
    e = [jnp.exp(s[h] - jnp.max(s[h], axis=-1, keepdims=True)) for h in heads]
    w = [e[h] * (1.0 / jnp.sum(e[h], axis=-1, keepdims=True)) for h in heads]
    o_ref[...] = jnp.concatenate([_dot(w[h].astype(BF16), v_ref[:, sl[h]].astype(BF16)) for h in heads], axis=1)


def _ctx_attention(p, n_seq, seq_len):
    return pl.pallas_call(
        _ctx_attn_kernel,
        grid=(n_seq,),
        in_specs=[
            pl.BlockSpec((seq_len, A_W), lambda b: (b, 0)),
            pl.BlockSpec((seq_len, A_W), lambda b: (b, 1)),
            pl.BlockSpec((seq_len, A_W), lambda b: (b, 2)),
        ],
        out_specs=pl.BlockSpec((seq_len, A_W), lambda b: (b, 0)),
        out_shape=jax.ShapeDtypeStruct((n_seq * seq_len, A_W), F32),
        compiler_params=_params(("arbitrary",)),
        name="ctx_attention",
    )(p, p, p)


Q_ROWS = 8
K_ROWS = 16
KEY_BLK = 256


NO_ROW = 2 * WIN_R - 1


def _nbr_bias_tiles(rpb):
    n_c = 2 * WIN_C - 1
    cq = np.arange(GRID_W)[:, None]
    ck = np.arange(GRID_W)[None, :]
    cs = np.clip(cq - WIN_C // 2, 0, GRID_W - WIN_C)
    col_ok = (ck >= cs) & (ck < cs + WIN_C)
    pick = np.where(col_ok, np.clip(ck - cq + WIN_C - 1, 0, n_c - 1), n_c)
    sel = np.zeros((2, GRID_W, 2 * GRID_W, n_c + 1), np.float32)
    for side in range(2):
        sel[side, cq, side * GRID_W + ck, pick] = 1.0
    ext = jnp.full(rpb.shape[:2] + (2 * WIN_R, n_c + 1), NEG, F32)
    ext = ext.at[:, :, :2 * WIN_R - 1, :n_c].set(rpb.astype(F32))
    return jnp.einsum("lhdm,sqkm->lhdsqk", ext, jnp.asarray(sel), precision=lax.Precision.HIGHEST)


def _nbr_attn_kernel(q_ref, k0_ref, k1_ref, k2_ref, k3_ref, v0_ref, v1_ref, v2_ref, v3_ref,
                     ck_ref, cv_ref, tab_ref, o_ref):
    scale = HEAD_DIM ** -0.5
    k_refs = (k0_ref, k1_ref, k2_ref, k3_ref)
    v_refs = (v0_ref, v1_ref, v2_ref, v3_ref)
    rb = pl.program_id(1)
    rows = GRID_W
    key_row0 = jnp.clip(Q_ROWS * rb - WIN_R // 2, 0, rows - K_ROWS)
    d = []
    for rq in range(Q_ROWS):
        r = Q_ROWS * rb + rq
        rs = jnp.clip(r - WIN_R // 2, 0, rows - WIN_R)
        d_row = []
        for rk in range(K_ROWS):
            rka = key_row0 + rk
            ok = (rka >= rs) & (rka < rs + WIN_R)
            d_row.append(jnp.where(ok, rka - r + WIN_R - 1, NO_ROW))
        d.append(d_row)
    rows_per_blk = KEY_BLK // GRID_W
    heads = range(2)
    sl = [slice(HEAD_DIM * hh, HEAD_DIM * (hh + 1)) for hh in heads]
    q = [(q_ref[:, sl[hh]] * scale).astype(BF16) for hh in heads]
    s_ctx = [_dot_nt(q[hh], ck_ref[0, 0, :, sl[hh]].astype(BF16)) for hh in heads]
    s_loc = [[] for _ in heads]
    for j in range(4):
        for hh in heads:
            bias = jnp.concatenate([
                jnp.concatenate([
                    tab_ref[0, hh, d[rq][rows_per_blk * j + 2 * p], 0]
                    + tab_ref[0, hh, d[rq][rows_per_blk * j + 2 * p + 1], 1]
                    for p in range(rows_per_blk // 2)], axis=1)
                for rq in range(Q_ROWS)], axis=0)
            s_loc[hh].append(_dot_nt(q[hh], k_refs[j][:, sl[hh]].astype(BF16)) + bias)
    m = [jnp.max(s_ctx[hh], axis=-1, keepdims=True) for hh in heads]
    for j in range(4):
        m = [jnp.maximum(m[hh], jnp.max(s_loc[hh][j], axis=-1, keepdims=True)) for hh in heads]
    e_ctx = [jnp.exp(s_ctx[hh] - m[hh]) for hh in heads]
    den = [jnp.sum(e_ctx[hh], axis=-1, keepdims=True) for hh in heads]
    num = [_dot(e_ctx[hh].astype(BF16), cv_ref[0, 0, :, sl[hh]].astype(BF16)) for hh in heads]
    for j in range(4):
        e = [jnp.exp(s_loc[hh][j] - m[hh]) for hh in heads]
        den = [den[hh] + jnp.sum(e[hh], axis=-1, keepdims=True) for hh in heads]
        num = [num[hh] + _dot(e[hh].astype(BF16), v_refs[j][:, sl[hh]].astype(BF16)) for hh in heads]
    o_ref[...] = jnp.concatenate([num[hh] / den[hh] for hh in heads], axis=1)


def _nbr_attention(p, cache_k4, cache_v4, bias_tiles, layer, n_seq, seq_len):
    q_tok = Q_ROWS * GRID_W
    n_rb = seq_len // q_tok
    kb_per_seq = seq_len // KEY_BLK
    max_base = kb_per_seq - 4

    def kmap(j, col0):
        def f(hp, rb, b):
            base = jnp.clip(2 * rb - 1, 0, max_base)
            return (b * kb_per_seq + base + j, col0 + hp)
        return f

    past = cache_k4.shape[2]
    in_specs = [pl.BlockSpec((q_tok, 128), lambda hp, rb, b: (b * n_rb + rb, hp))]
    in_specs += [pl.BlockSpec((KEY_BLK, 128), kmap(j, A_W // 128)) for j in range(4)]
    in_specs += [pl.BlockSpec((KEY_BLK, 128), kmap(j, 2 * A_W // 128)) for j in range(4)]
    in_specs += [
        pl.BlockSpec((1, 1, past, 128), lambda hp, rb, b: (b, layer, 0, hp)),
        pl.BlockSpec((1, 1, past, 128), lambda hp, rb, b: (b, layer, 0, hp)),
        pl.BlockSpec((1, 2, 2 * WIN_R, 2, GRID_W, 2 * GRID_W), lambda hp, rb, b: (layer, hp, 0, 0, 0, 0)),
    ]
    return pl.pallas_call(
        _nbr_attn_kernel,
        grid=(H_ATT // 2, n_rb, n_seq),
        in_specs=in_specs,
        out_specs=pl.BlockSpec((q_tok, 128), lambda hp, rb, b: (b * n_rb + rb, hp)),
        out_shape=jax.ShapeDtypeStruct((n_seq * seq_len, A_W), F32),
        compiler_params=_params(("arbitrary", "arbitrary", "arbitrary")),
        name="nbr_attention",
    )(p, p, p, p, p, p, p, p, p, cache_k4, cache_v4, bias_tiles)


def _dft_mats(n):
    idx = jnp.arange(n, dtype=jnp.int32)
    ang = ((idx[:, None] * idx[None, :]) % n).astype(F32) * (2.0 * np.pi / n)
    return jnp.cos(ang).astype(BF16), jnp.sin(ang).astype(BF16)


def _channel_dft():
    c = np.arange(HEAD_DIM)
    ang = 2.0 * np.pi * ((c[:, None] * c[None, :]) % HEAD_DIM) / HEAD_DIM
    eye = np.eye(G_FOURIER)
    mats = np.concatenate([np.kron(eye, np.cos(ang)), np.kron(eye, np.sin(ang))], axis=1)
    return jnp.asarray(mats, F32).astype(BF16)


def _fourier_kernel(c_ref, s_ref, ab_ref, wf_ref, o_ref, acc_ref, *, scale, n_k):
    k = pl.program_id(2)

    @pl.when(k == 0)
    def _():
        acc_ref[...] = jnp.zeros_like(acc_ref)

    acc_ref[...] += _dot(c_ref[...], ab_ref[:, :B_W]) - _dot(s_ref[...], ab_ref[:, B_W:])

    @pl.when(k == n_k - 1)
    def _():
        z = (acc_ref[...] * scale).astype(BF16)
        o_ref[...] = _dot(z, wf_ref[0])


def _fourier(ab, cmat, smat, wf_blk, layer, n_seq, seq_len):
    ti = min(seq_len, 512)
    tk = min(seq_len, 1024)
    n_i, n_k = seq_len // ti, seq_len // tk
    scale = float((seq_len * HEAD_DIM) ** -0.5)
    return pl.pallas_call(
        functools.partial(_fourier_kernel, scale=scale, n_k=n_k),
        grid=(n_seq, n_i, n_k),
        in_specs=[
            pl.BlockSpec((ti, tk), lambda s, i, k: (i, k)),
            pl.BlockSpec((ti, tk), lambda s, i, k: (i, k)),
            pl.BlockSpec((tk, 2 * B_W), lambda s, i, k: (s * n_k + k, 0)),
            pl.BlockSpec((1, B_W, B_W), lambda s, i, k: (layer, 0, 0)),
        ],
        out_specs=pl.BlockSpec((ti, B_W), lambda s, i, k: (s * n_i + i, 0)),
        out_shape=jax.ShapeDtypeStruct((n_seq * seq_len, B_W), F32),
        scratch_shapes=[pltpu.VMEM((ti, B_W), F32)],
        compiler_params=_params(("arbitrary", "arbitrary", "arbitrary")),
        name="fourier",
    )(cmat, smat, ab, wf_blk)


FS_GROUP = 8


def _twiddles(side, n):
    k1 = jnp.arange(side, dtype=jnp.int32)[:, None]
    n2 = jnp.arange(side, dtype=jnp.int32)[None, :]
    ang = (k1 * n2).astype(F32) * (2.0 * np.pi / n)
    wide = lambda t: jnp.broadcast_to(t[:, :, None], (side, side, B_W)).reshape(side, side * B_W)
    return wide(jnp.cos(ang)), wide(jnp.sin(ang))


def _fourier_stage1_kernel(ab_ref, c_ref, s_ref, tc_ref, ts_ref, o_ref):
    ab = ab_ref[...]
    m1 = _dot(c_ref[...], ab)
    m2 = _dot(s_ref[...], ab)
    for t in range(FS_GROUP):
        a0 = 2 * B_W * t
        yr = m1[:, a0:a0 + B_W] - m2[:, a0 + B_W:a0 + 2 * B_W]
        yi = -(m1[:, a0 + B_W:a0 + 2 * B_W] + m2[:, a0:a0 + B_W])
        ct = tc_ref[:, B_W * t:B_W * (t + 1)]
        st = ts_ref[:, B_W * t:B_W * (t + 1)]
        o_ref[:, a0:a0 + B_W] = (yr * ct + yi * st).astype(BF16)
        o_ref[:, a0 + B_W:a0 + 2 * B_W] = (yi * ct - yr * st).astype(BF16)


def _fourier_stage2_kernel(y_ref, c_ref, s_ref, wf_ref, o_ref, *, side, scale):
    zs = []
    for j in range(FS_GROUP):
        rows = slice(side * j, side * (j + 1))
        zs.append(_dot(c_ref[...], y_ref[rows, :B_W]) + _dot(s_ref[...], y_ref[rows, B_W:]))
    z = (jnp.concatenate(zs, axis=0) * scale).astype(BF16)
    o = _dot(z, wf_ref[0])
    for j in range(FS_GROUP):
        o_ref[:, B_W * j:B_W * (j + 1)] = o[side * j:side * (j + 1), :]


def _fourier_grid(ab, cmat, smat, tw_cos, tw_sin, wf_blk, layer, n_seq, side):
    seq_len = side * side
    n_g = side // FS_GROUP
    scale = float((seq_len * HEAD_DIM) ** -0.5)
    small = pl.BlockSpec((side, side), lambda s, j: (0, 0))
    y = pl.pallas_call(
        _fourier_stage1_kernel,
        grid=(n_seq, n_g),
        in_specs=[
            pl.BlockSpec((side, 2 * B_W * FS_GROUP), lambda s, j: (s, j)),
            small,
            small,
            pl.BlockSpec((side, B_W * FS_GROUP), lambda s, j: (0, j)),
            pl.BlockSpec((side, B_W * FS_GROUP), lambda s, j: (0, j)),
        ],
        out_specs=pl.BlockSpec((side, 2 * B_W * FS_GROUP), lambda s, j: (s, j)),
        out_shape=jax.ShapeDtypeStruct((n_seq * side, side * 2 * B_W), BF16),
        compiler_params=_params(("arbitrary", "arbitrary")),
        name="fourier_stage1",
    )(ab.reshape(n_seq * side, side * 2 * B_W), cmat, smat, tw_cos, tw_sin)
    out = pl.pallas_call(
        functools.partial(_fourier_stage2_kernel, side=side, scale=scale),
        grid=(n_seq, n_g),
        in_specs=[
            pl.BlockSpec((side * FS_GROUP, 2 * B_W), lambda s, j: (s * n_g + j, 0)),
            small,
            small,
            pl.BlockSpec((1, B_W, B_W), lambda s, j: (layer, 0, 0)),
        ],
        out_specs=pl.BlockSpec((side, B_W * FS_GROUP), lambda s, j: (s, j)),
        out_shape=jax.ShapeDtypeStruct((n_seq * side, side * B_W), F32),
        compiler_params=_params(("arbitrary", "arbitrary")),
        name="fourier_stage2",
    )(y.reshape(n_seq * seq_len, 2 * B_W), cmat, smat, wf_blk)
    return out.reshape(n_seq * seq_len, B_W)


SEQS_PER_STEP = 2


def _mlstm_kernel(*refs, n_chunks):
    sps = SEQS_PER_STEP
    n_side = 3 + 2 * sps
    fwd, bwd = refs[:n_side], refs[n_side:2 * n_side]
    c0_ref, n0_ref, m0_ref, tril_ref, triu_ref = refs[2 * n_side:2 * n_side + 5]
    hf_ref, hb_ref, cout_ref, nout_ref, mout_ref, c_s, n_s, m_s = refs[2 * n_side + 5:]
    c = pl.program_id(1)
    hi = lax.Precision.HIGHEST
    lc = MLSTM_CHUNK
    pair_w = 2 * HEAD_DIM
    n_pairs = H_MLSTM // 2

    @pl.when(c == 0)
    def _():
        c_s[...] = c0_ref[...]
        n_s[...] = n0_ref[...]
        m_s[...] = m0_ref[...]

    lo_lane = lax.broadcasted_iota(jnp.int32, (1, pair_w), 1) < HEAD_DIM
    lo_row = lax.broadcasted_iota(jnp.int32, (pair_w, 1), 0) < HEAD_DIM
    row8 = lax.broadcasted_iota(jnp.int32, (8, 1), 0)
    cum_mask = [tril_ref[...], triu_ref[...]]
    keep_t = [triu_ref[...] > 0.5, tril_ref[...] > 0.5]
    pairs = [(j, d, hp) for j in range(sps) for d in range(2) for hp in range(n_pairs)]
    heads = [(pi, hh) for pi in range(len(pairs)) for hh in range(2)]
    rng = range(len(heads))

    pre = {}
    for j in range(sps):
        for d, side in enumerate((fwd, bwd)):
            q_ref, k_ref, g_ref = side[:3]
            gt_ref, vt_ref = side[3 + 2 * j], side[4 + 2 * j]
            go = 2 * H_MLSTM * d
            lf_cols = _log_sigmoid(g_ref[j, :, go + H_MLSTM:go + 2 * H_MLSTM])
            lf_rows = _log_sigmoid(gt_ref[go + H_MLSTM:go + 2 * H_MLSTM, :])
            b_cols = _dot(cum_mask[d], lf_cols, precision=hi)
            pre[j, d] = dict(
                a_cols=g_ref[j, :, go:go + H_MLSTM] - b_cols,
                ig_rows=gt_ref[go:go + H_MLSTM, :],
                b_rows=_dot_nt(lf_rows, cum_mask[d], precision=hi),
                q=q_ref[j].astype(BF16), k=(k_ref[j] * (HEAD_DIM ** -0.5)).astype(BF16), vt=vt_ref[...])

    def pair_cols(hp):
        return slice(pair_w * hp, pair_w * (hp + 1))

    q_p = [pre[j, d]["q"][:, pair_cols(hp)] for j, d, hp in pairs]
    k_p = [pre[j, d]["k"][:, pair_cols(hp)] for j, d, hp in pairs]
    vt_p = [pre[j, d]["vt"][pair_cols(hp), :] for j, d, hp in pairs]
    c_p = [c_s[j, d, hp] for j, d, hp in pairs]
    n_p = [n_s[j, n_pairs * d + hp:n_pairs * d + hp + 1, :] for j, d, hp in pairs]
    zero_k = jnp.zeros((lc, pair_w), BF16)
    k_h = [jnp.where(lo_lane, k_p[pi], zero_k) if hh == 0 else jnp.where(lo_lane, zero_k, k_p[pi])
           for pi, hh in heads]

    def head_of(i):
        pi, hh = heads[i]
        j, d, hp = pairs[pi]
        return j, d, 2 * hp + hh

    b_row = [pre[head_of(i)[0], head_of(i)[1]]["b_rows"][head_of(i)[2]:head_of(i)[2] + 1, :] for i in rng]
    ig_row = [pre[head_of(i)[0], head_of(i)[1]]["ig_rows"][head_of(i)[2]:head_of(i)[2] + 1, :] for i in rng]
    a_col = [pre[head_of(i)[0], head_of(i)[1]]["a_cols"][:, head_of(i)[2]:head_of(i)[2] + 1] for i in rng]
    m_st = [m_s[head_of(i)[0], H_MLSTM * head_of(i)[1] + head_of(i)[2]:H_MLSTM * head_of(i)[1] + head_of(i)[2] + 1, :]
            for i in rng]
    bl = [b_row[i][:, lc - 1:lc] if head_of(i)[1] == 0 else b_row[i][:, 0:1] for i in rng]

    d_t = [jnp.where(keep_t[head_of(i)[1]], b_row[i] + a_col[i], NEG) for i in rng]
    inter = [b_row[i] + m_st[i] for i in rng]
    m_t = [jnp.maximum(inter[i], jnp.max(d_t[i], axis=0, keepdims=True)) for i in rng]
    s_t = [_dot_nt(k_h[i], q_p[heads[i][0]]) * jnp.exp(d_t[i] - m_t[i]) for i in rng]
    w_in = [jnp.exp(inter[i] - m_t[i]) for i in rng]
    num_t = [_dot(vt_p[heads[i][0]].astype(BF16), s_t[i].astype(BF16)) for i in rng]
    qc_t = [_dot_nt(c_p[pi].astype(BF16), q_p[pi]) for pi in range(len(pairs))]
    n_mat = [jnp.where((row8 == 0) & lo_lane, n_p[pi], jnp.where((row8 == 1) & ~lo_lane, n_p[pi], 0.0))
             for pi in range(len(pairs))]
    nq = [_dot_nt(n_mat[pi].astype(BF16), q_p[pi]) for pi in range(len(pairs))]
    den = [jnp.sum(s_t[i], axis=0, keepdims=True) + w_in[i] * nq[heads[i][0]][heads[i][1]:heads[i][1] + 1, :]
           for i in rng]
    inv = [1.0 / jnp.maximum(jnp.abs(den[i]), jnp.exp(-m_t[i])) for i in rng]
    h_t = []
    for pi in range(len(pairs)):
        i0, i1 = 2 * pi, 2 * pi + 1
        num = jnp.where(lo_row, num_t[i0], num_t[i1])
        h_t.append((num + jnp.where(lo_row, w_in[i0], w_in[i1]) * qc_t[pi]) * jnp.where(lo_row, inv[i0], inv[i1]))
    for j in range(sps):
        base = 2 * n_pairs * j
        hf_ref[j] = jnp.concatenate(h_t[base:base + n_pairs], axis=0)
        hb_ref[j] = jnp.concatenate(h_t[base + n_pairs:base + 2 * n_pairs], axis=0)

    g_row = [bl[i] - b_row[i] + ig_row[i] for i in rng]
    m_new = [jnp.maximum(bl[i] + m_st[i], jnp.max(g_row[i], axis=-1, keepdims=True)) for i in rng]
    wc = [jnp.exp(bl[i] + m_st[i] - m_new[i]) for i in rng]
    ws_row = [jnp.exp(g_row[i] - m_new[i]) for i in rng]
    upd = [_dot((vt_p[heads[i][0]] * ws_row[i]).astype(BF16), k_p[heads[i][0]]) for i in rng]
    for pi, (j, d, hp) in enumerate(pairs):
        i0, i1 = 2 * pi, 2 * pi + 1
        block = jnp.where(lo_row & lo_lane, upd[i0], jnp.where(~lo_row & ~lo_lane, upd[i1], 0.0))
        c_s[j, d, hp] = jnp.where(lo_row, wc[i0], wc[i1]) * c_p[pi] + block
        ws_mat = jnp.where(row8 == 0, ws_row[i0], jnp.where(row8 == 1, ws_row[i1], 0.0))
        k_sum = _dot(ws_mat.astype(BF16), k_p[pi])
        row = n_pairs * d + hp
        n_s[j, row:row + 1, :] = (jnp.where(lo_lane, wc[i0], wc[i1]) * n_p[pi]
                                  + jnp.where(lo_lane, k_sum[0:1, :], k_sum[1:2, :]))
    for i in rng:
        j, d, hd = head_of(i)
        m_s[j, H_MLSTM * d + hd:H_MLSTM * d + hd + 1, :] = m_new[i]

    @pl.when(c == n_chunks - 1)
    def _():
        cout_ref[...] = c_s[...]
        nout_ref[...] = n_s[...]
        mout_ref[...] = m_s[...]


def _pair_states(c):
    b = c.shape[0]
    c = c.reshape(b, 2, H_MLSTM // 2, 2, HEAD_DIM, HEAD_DIM)
    zero = jnp.zeros_like(c[:, :, :, 0])
    top = jnp.concatenate([c[:, :, :, 0], zero], axis=-1)
    bottom = jnp.concatenate([zero, c[:, :, :, 1]], axis=-1)
    return jnp.concatenate([top, bottom], axis=-2)


def _unpair_states(cp):
    b = cp.shape[0]
    first = cp[:, :, :, :HEAD_DIM, :HEAD_DIM]
    second = cp[:, :, :, HEAD_DIM:, HEAD_DIM:]
    return jnp.stack([first, second], axis=3).reshape(b, 2, H_MLSTM, HEAD_DIM, HEAD_DIM)


def _mlstm(p, g, gt, vo, c0, n0, m0, tril, triu, n_seq, seq_len):
    lc = MLSTM_CHUNK
    nc = seq_len // lc
    n_st = 2 * H_MLSTM
    n_pairs = H_MLSTM // 2
    pair_w = 2 * HEAD_DIM
    sps = SEQS_PER_STEP
    p3 = p.reshape(n_seq, seq_len, P_COLS)
    g3 = g.reshape(n_seq, seq_len, N_GATE_COLS)

    def fwd(c):
        return c

    def bwd(c):
        return nc - 1 - c

    def side(chunk):
        tok = lambda col: (lambda b, c: (b, chunk(c), col))
        specs = [
            pl.BlockSpec((sps, lc, C_W), tok(QC_BLK)),
            pl.BlockSpec((sps, lc, C_W), tok(KC_BLK)),
            pl.BlockSpec((sps, lc, N_GATE_COLS), tok(0)),
        ]
        for j in range(sps):
            lanes = lambda b, c, j=j: (0, (b * sps + j) * nc + chunk(c))
            specs += [pl.BlockSpec((N_GATE_COLS, lc), lanes), pl.BlockSpec((C_W, lc), lanes)]
        return specs

    state_specs = [
        pl.BlockSpec((sps, 2, n_pairs, pair_w, pair_w), lambda b, c: (b, 0, 0, 0, 0)),
        pl.BlockSpec((sps, 2 * n_pairs, pair_w), lambda b, c: (b, 0, 0)),
        pl.BlockSpec((sps, n_st, 1), lambda b, c: (b, 0, 0)),
    ]
    tri_spec = pl.BlockSpec((lc, lc), lambda b, c: (0, 0))
    operands = [p3, p3, g3] + [gt, vo] * sps
    hf, hb, c_out, n_out, m_out = pl.pallas_call(
        functools.partial(_mlstm_kernel, n_chunks=nc),
        grid=(n_seq // sps, nc),
        in_specs=side(fwd) + side(bwd) + state_specs + [tri_spec, tri_spec],
        out_specs=[
            pl.BlockSpec((sps, C_W, lc), lambda b, c: (b, 0, c)),
            pl.BlockSpec((sps, C_W, lc), lambda b, c: (b, 0, nc - 1 - c)),
        ] + state_specs,
        out_shape=[
            jax.ShapeDtypeStruct((n_seq, C_W, seq_len), F32),
            jax.ShapeDtypeStruct((n_seq, C_W, seq_len), F32),
            jax.ShapeDtypeStruct((n_seq, 2, n_pairs, pair_w, pair_w), F32),
            jax.ShapeDtypeStruct((n_seq, 2 * n_pairs, pair_w), F32),
            jax.ShapeDtypeStruct((n_seq, n_st, 1), F32),
        ],
        scratch_shapes=[
            pltpu.VMEM((sps, 2, n_pairs, pair_w, pair_w), F32),
            pltpu.VMEM((sps, 2 * n_pairs, pair_w), F32),
            pltpu.VMEM((sps, n_st, 1), F32),
        ],
        compiler_params=_params(("arbitrary", "arbitrary")),
        name="mlstm",
    )(*operands, *operands, _pair_states(c0), n0.reshape(n_seq, 2 * n_pairs, pair_w), m0.reshape(n_seq, n_st, 1),
      tril, triu)
    return (hf, hb, _unpair_states(c_out), n_out.reshape(n_seq, 2, H_MLSTM, HEAD_DIM),
            m_out.reshape(n_seq, 2, H_MLSTM))


def _head_norm(y, g, ones_blk):
    ysq = y * y
    hi = ysq.astype(BF16)
    lo = (ysq - hi.astype(F32)).astype(BF16)
    ss = _dot(hi, ones_blk) + _dot(lo, ones_blk)
    return y * lax.rsqrt(ss * (1.0 / HEAD_DIM) + EPS) * g


def _merge_kernel(att_ref, four_ref, hf_ref, hb_ref, oc_ref, x_ref, mod_ref, gh_ref, ghm_ref, wo_ref, g2_ref, wrt_ref,
                  ones_ref, xo_ref, h2_ref, afft_ref):
    gh = gh_ref[0]
    ya = _head_norm(att_ref[...], gh[:, :A_W], ones_ref[...])
    yf = _head_norm(four_ref[...], gh[:, A_W:A_W + B_W], ones_ref[:B_W, :B_W])
    mem = hf_ref[0] + hb_ref[0]
    heads = []
    for hd in range(H_MLSTM):
        y = mem[HEAD_DIM * hd:HEAD_DIM * (hd + 1), :]
        heads.append(y * lax.rsqrt(jnp.mean(y * y, axis=0, keepdims=True) + EPS))
    ym_t = jnp.concatenate(heads, axis=0) * ghm_ref[0] * jax.nn.sigmoid(oc_ref[...])
    out = (_dot(ya.astype(BF16), wo_ref[0, :A_W, :])
           + _dot(yf.astype(BF16), wo_ref[0, A_W:A_W + B_W, :])
           + _dot_tn(ym_t.astype(BF16), wo_ref[0, A_W + B_W:, :]))
    x = x_ref[...] + mod_ref[0, 2:3, :] * out
    xo_ref[...] = x
    y2 = x * lax.rsqrt(jnp.mean(x * x, axis=-1, keepdims=True) + EPS) * g2_ref[0]
    h2 = (y2 * (1.0 + mod_ref[0, 4:5, :]) + mod_ref[0, 3:4, :]).astype(BF16)
    h2_wide = h2.astype(F32)
    for s in range(N_SLAB):
        h2_ref[pl.ds(s, h2_wide.shape[0], stride=N_SLAB), :] = h2_wide[:, LANE * s:LANE * (s + 1)]
    logits = _dot_nt(wrt_ref[0], h2)
    e = jnp.exp(logits - jnp.max(logits, axis=0, keepdims=True))
    afft_ref[...] = e / jnp.sum(e, axis=0, keepdims=True)


def _merge(att, four, hf, hb, vo, x2d, mod, layer, g_head, g_mem, w_out_bf, g2, w_rt_bf, ones_blk, n_seq, seq_len,
           mod_seq_len):
    t = x2d.shape[0]
    tm = MERGE_TILE
    tiles_per_mod = mod_seq_len // tm
    tiles_per_seq = seq_len // tm
    row = lambda i: (i, 0)
    lay = lambda i: (layer, 0, 0)
    mem = lambda i: (i // tiles_per_seq, 0, i % tiles_per_seq)
    return pl.pallas_call(
        _merge_kernel,
        grid=(t // tm,),
        in_specs=[
            pl.BlockSpec((tm, A_W), row),
            pl.BlockSpec((tm, B_W), row),
            pl.BlockSpec((1, C_W, tm), mem),
            pl.BlockSpec((1, C_W, tm), mem),
            pl.BlockSpec((C_W, tm), lambda i: (1, i)),
            pl.BlockSpec((tm, D_MODEL), row),
            pl.BlockSpec((1, 6, D_MODEL), lambda i: (i // tiles_per_mod, 0, 0)),
            pl.BlockSpec((1, 1, D_MODEL), lay),
            pl.BlockSpec((1, C_W, tm), lay),
            pl.BlockSpec((1, D_MODEL, D_MODEL), lay),
            pl.BlockSpec((1, 1, D_MODEL), lay),
            pl.BlockSpec((1, N_EXPERTS, D_MODEL), lay),
            pl.BlockSpec((A_W, A_W), lambda i: (0, 0)),
        ],
        out_specs=[
            pl.BlockSpec((tm, D_MODEL), row),
            pl.BlockSpec((N_SLAB * tm, LANE), row),
            pl.BlockSpec((N_EXPERTS, tm), lambda i: (0, i)),
        ],
        out_shape=[
            jax.ShapeDtypeStruct((t, D_MODEL), F32),
            jax.ShapeDtypeStruct((N_SLAB * t, LANE), F32),
            jax.ShapeDtypeStruct((N_EXPERTS, t), F32),
        ],
        compiler_params=_params(("arbitrary",)),
        name="merge",
    )(att, four, hf, hb, vo, x2d, mod, g_head, g_mem, w_out_bf, g2, w_rt_bf, ones_blk)


BISECT_STEPS = 48
TOKEN_CHUNK = 1024


TOKEN_SPLIT = 64


def _route_kernel(aff_ref, triu_ref, idx_ref, gs_ref, sp_ref, *, ns, seq_len, cap):
    seqs = range(ns)
    aff = [aff_ref[:, seq_len * j:seq_len * (j + 1)] for j in seqs]

    def body(_, bounds):
        out = []
        for j in seqs:
            lo, hi = bounds[j]
            mid = 0.5 * (lo + hi)
            ge = jnp.sum(jnp.where(aff[j] >= mid, 1.0, 0.0), axis=1, keepdims=True) >= cap
            out.append((jnp.where(ge, mid, lo), jnp.where(ge, hi, mid)))
        return tuple(out)

    start = (jnp.zeros((N_EXPERTS, 1), F32), jnp.full((N_EXPERTS, 1), 2.0, F32))
    bounds = lax.fori_loop(0, BISECT_STEPS, body, tuple(start for _ in seqs))
    thr = [jnp.max(jnp.where(aff[j] < bounds[j][1], aff[j], -1.0), axis=1, keepdims=True) for j in seqs]
    need = [cap - jnp.sum(jnp.where(aff[j] > thr[j], 1.0, 0.0), axis=1, keepdims=True) for j in seqs]
    triu = triu_ref[...]
    eq_carry = [jnp.zeros((N_EXPERTS, 1), F32) for _ in seqs]
    pos_carry = [jnp.zeros((N_EXPERTS, 1), F32) for _ in seqs]
    for b in range(seq_len // 128):
        for j in seqs:
            blk = aff[j][:, 128 * b:128 * (b + 1)]
            eq = blk == thr[j]
            eq_f = jnp.where(eq, 1.0, 0.0)
            eq_inc = _dot(eq_f.astype(BF16), triu) + eq_carry[j]
            sel = (blk > thr[j]) | (eq & (eq_inc - eq_f < need[j]))
            sel_f = jnp.where(sel, 1.0, 0.0)
            pos_inc = _dot(sel_f.astype(BF16), triu) + pos_carry[j]
            t0 = seq_len * j + 128 * b
            sp_ref[:, t0:t0 + 128] = jnp.where(sel, pos_inc - sel_f, -1.0).astype(jnp.int32)
            eq_carry[j] = eq_inc[:, 127:128]
            pos_carry[j] = pos_inc[:, 127:128]

    tc = min(seq_len, TOKEN_CHUNK)
    slot = lax.broadcasted_iota(jnp.int32, (cap, tc), 0)
    part = lax.broadcasted_iota(jnp.int32, (8, tc), 0)
    tok = lax.broadcasted_iota(jnp.int32, (1, tc), 1).astype(F32)
    chunks = range(0, seq_len, tc)
    tok_hi = [jnp.floor((tok + float(t0)) * (1.0 / TOKEN_SPLIT)) for t0 in chunks]
    tok_lo = [tok + float(t0) - TOKEN_SPLIT * hi for t0, hi in zip(chunks, tok_hi)]

    def per_expert(e, carry):
        for j in seqs:
            acc = jnp.zeros((cap, 8), F32)
            for ci, t0 in enumerate(chunks):
                cols = slice(seq_len * j + t0, seq_len * j + t0 + tc)
                onehot = jnp.where(slot == sp_ref[pl.ds(e, 1), cols], 1.0, 0.0).astype(BF16)
                a = aff_ref[pl.ds(e, 1), cols]
                a_hi = a.astype(BF16).astype(F32)
                a_mid = (a - a_hi).astype(BF16).astype(F32)
                a_lo = a - a_hi - a_mid
                vals = jnp.where(part == 0, tok_hi[ci], jnp.where(part == 1, tok_lo[ci], jnp.where(
                    part == 2, a_hi, jnp.where(part == 3, a_mid, jnp.where(part == 4, a_lo, 0.0)))))
                acc = acc + _dot_nt(onehot, vals.astype(BF16))
            idx_ref[N_EXPERTS * j + e] = (N_SLAB * (TOKEN_SPLIT * acc[:, 0:1] + acc[:, 1:2])).astype(jnp.int32)
            gs_ref[e, cap * j:cap * (j + 1)] = acc[:, 2:3] + acc[:, 3:4] + acc[:, 4:5]
        return carry

    lax.fori_loop(0, N_EXPERTS, per_expert, 0)


def _route(afft, triu_bf, n_seq, seq_len, cap, ns):
    idx, gs = pl.pallas_call(
        functools.partial(_route_kernel, ns=ns, seq_len=seq_len, cap=cap),
        grid=(n_seq // ns,),
        in_specs=[
            pl.BlockSpec((N_EXPERTS, ns * seq_len), lambda s: (0, s)),
            pl.BlockSpec((128, 128), lambda s: (0, 0)),
        ],
        out_specs=[
            pl.BlockSpec((ns * N_EXPERTS, cap, 1), lambda s: (s, 0, 0)),
            pl.BlockSpec((N_EXPERTS, ns * cap, 1), lambda s: (0, s, 0)),
        ],
        out_shape=[
            jax.ShapeDtypeStruct((n_seq * N_EXPERTS, cap, 1), jnp.int32),
            jax.ShapeDtypeStruct((N_EXPERTS, n_seq * cap, 1), F32),
        ],
        scratch_shapes=[pltpu.VMEM((N_EXPERTS, ns * seq_len), jnp.int32)],
        compiler_params=_params(("arbitrary",)),
        name="route",
    )(afft, triu_bf)
    return idx.reshape(n_seq * N_EXPERTS * cap), gs


ROW_COPIES = 8


def _row_tile(first_row):
    return pl.ds(pl.multiple_of(first_row, N_SLAB), N_SLAB)


def _gather_kernel(idx_ref, src_ref, xs_ref, tile_ref, *, eb, cap):
    ei = pl.program_id(1)

    def per_expert(ee, carry):
        e = ei * eb + ee

        def rows(g, c):
            slot0 = g * ROW_COPIES
            for u in range(ROW_COPIES):
                tile_ref[_row_tile((slot0 + u) * N_SLAB), :] = src_ref[_row_tile(idx_ref[e * cap + slot0 + u]), :]
            return c

        lax.fori_loop(0, cap // ROW_COPIES, rows, 0)
        for s in range(N_SLAB):
            xs_ref[ee, :, LANE * s:LANE * (s + 1)] = tile_ref[pl.ds(s, cap, stride=N_SLAB), :].astype(BF16)
        return carry

    lax.fori_loop(0, eb, per_expert, 0)


def _gather(idx, h2_rows, n_seq, cap, eb):
    return pl.pallas_call(
        functools.partial(_gather_kernel, eb=eb, cap=cap),
        grid=(n_seq, N_EXPERTS // eb),
        in_specs=[
            pl.BlockSpec((N_EXPERTS * cap,), lambda s, e: (s,), memory_space=pltpu.SMEM),
            pl.BlockSpec((h2_rows.shape[0] // n_seq, LANE), lambda s, e: (s, 0)),
        ],
        out_specs=pl.BlockSpec((eb, cap, D_MODEL), lambda s, e: (e, s, 0)),
        out_shape=jax.ShapeDtypeStruct((N_EXPERTS, n_seq * cap, D_MODEL), BF16),
        scratch_shapes=[pltpu.VMEM((N_SLAB * cap, LANE), F32)],
        compiler_params=_params(("arbitrary", "arbitrary"), vmem_mib=56),
        name="gather",
    )(idx, h2_rows)


def _expert_kernel(xc_ref, xl_ref, gc_ref, gl_ref, wg_ref, wu_ref, wd_ref, yc_ref, yl_ref, *, n_f):
    f = pl.program_id(1)

    @pl.when(f == 0)
    def _():
        yc_ref[...] = jnp.zeros_like(yc_ref)
        yl_ref[...] = jnp.zeros_like(yl_ref)

    wg = wg_ref[0, 0].astype(BF16)
    wu = wu_ref[0, 0].astype(BF16)
    wd = wd_ref[0, 0].astype(BF16)
    for x_ref, y_ref in ((xc_ref, yc_ref), (xl_ref, yl_ref)):
        x = x_ref[0]
        mid = (_silu(_dot(x, wg)) * _dot(x, wu)).astype(BF16)
        y_ref[0] += _dot(mid, wd)

    @pl.when(f == n_f - 1)
    def _():
        yc_ref[0] = yc_ref[0] * gc_ref[0]
        yl_ref[0] = yl_ref[0] * gl_ref[0]


def _experts(xs_c, xs_l, gs_c, gs_l, w_g, w_u, w_d, layer):
    rc, rl = xs_c.shape[1], xs_l.shape[1]
    tf = 1024
    n_f = EXPERT_FF // tf
    return pl.pallas_call(
        functools.partial(_expert_kernel, n_f=n_f),
        grid=(N_EXPERTS, n_f),
        in_specs=[
            pl.BlockSpec((1, rc, D_MODEL), lambda e, f: (e, 0, 0)),
            pl.BlockSpec((1, rl, D_MODEL), lambda e, f: (e, 0, 0)),
            pl.BlockSpec((1, rc, 1), lambda e, f: (e, 0, 0)),
            pl.BlockSpec((1, rl, 1), lambda e, f: (e, 0, 0)),
            pl.BlockSpec((1, 1, D_MODEL, tf), lambda e, f: (layer, e, 0, f)),
            pl.BlockSpec((1, 1, D_MODEL, tf), lambda e, f: (layer, e, 0, f)),
            pl.BlockSpec((1, 1, tf, D_MODEL), lambda e, f: (layer, e, f, 0)),
        ],
        out_specs=[
            pl.BlockSpec((1, rc, D_MODEL), lambda e, f: (e, 0, 0)),
            pl.BlockSpec((1, rl, D_MODEL), lambda e, f: (e, 0, 0)),
        ],
        out_shape=[
            jax.ShapeDtypeStruct((N_EXPERTS, rc, D_MODEL), F32),
            jax.ShapeDtypeStruct((N_EXPERTS, rl, D_MODEL), F32),
        ],
        compiler_params=_params(("arbitrary", "arbitrary"), vmem_mib=56),
        name="experts",
    )(xs_c, xs_l, gs_c, gs_l, w_g, w_u, w_d)


def _scatter_kernel(idx_ref, y_ref, x_ref, mod_ref, gf_ref, o_ref, acc_ref, tile_ref, *, eb, n_e, cap, tm, final):
    step = pl.program_id(1)

    @pl.when(step == 0)
    def _():
        acc_ref[...] = jnp.zeros_like(acc_ref)

    @pl.when(step < n_e)
    def _():
        def per_expert(ee, carry):
            e = step * eb + ee
            for s in range(N_SLAB):
                tile_ref[pl.ds(s, cap, stride=N_SLAB), :] = y_ref[ee, :, LANE * s:LANE * (s + 1)]

            def rows(g, c):
                slot0 = g * ROW_COPIES
                dst = [idx_ref[e * cap + slot0 + u] for u in range(ROW_COPIES)]
                new = [acc_ref[_row_tile(dst[u]), :] + tile_ref[_row_tile((slot0 + u) * N_SLAB), :]
                       for u in range(ROW_COPIES)]
                for u in range(ROW_COPIES):
                    acc_ref[_row_tile(dst[u]), :] = new[u]
                return c

            lax.fori_loop(0, cap // ROW_COPIES, rows, 0)
            return carry

        lax.fori_loop(0, eb, per_expert, 0)

    @pl.when(step >= n_e)
    def _():
        base = pl.multiple_of((step - n_e) * tm * N_SLAB, N_SLAB)
        moe = jnp.concatenate([acc_ref[pl.ds(base + s, tm, stride=N_SLAB), :] for s in range(N_SLAB)], axis=1)
        x = x_ref[...] + mod_ref[0, 5:6, :] * moe
        if final:
            x = x * lax.rsqrt(jnp.mean(x * x, axis=-1, keepdims=True) + EPS) * gf_ref[...]
        o_ref[...] = x


def _scatter(idx, ys, x2d, mod, g_final, n_seq, seq_len, cap, eb, tm, final):
    n_e = N_EXPERTS // eb
    n_out = seq_len // tm
    out_blk = lambda s, j: (s * n_out + jnp.maximum(j - n_e, 0), 0)
    return pl.pallas_call(
        functools.partial(_scatter_kernel, eb=eb, n_e=n_e, cap=cap, tm=tm, final=final),
        grid=(n_seq, n_e + n_out),
        in_specs=[
            pl.BlockSpec((N_EXPERTS * cap,), lambda s, j: (s,), memory_space=pltpu.SMEM),
            pl.BlockSpec((eb, cap, D_MODEL), lambda s, j: (jnp.minimum(j, n_e - 1), s, 0)),
            pl.BlockSpec((tm, D_MODEL), out_blk),
            pl.BlockSpec((1, 6, D_MODEL), lambda s, j: (s, 0, 0)),
            pl.BlockSpec((1, D_MODEL), lambda s, j: (0, 0)),
        ],
        out_specs=pl.BlockSpec((tm, D_MODEL), out_blk),
        out_shape=jax.ShapeDtypeStruct((n_seq * seq_len, D_MODEL), F32),
        scratch_shapes=[
            pltpu.VMEM((N_SLAB * seq_len, LANE), F32),
            pltpu.VMEM((N_SLAB * cap, LANE), F32),
        ],
        compiler_params=_params(("arbitrary", "arbitrary"), vmem_mib=56),
        name="scatter",
    )(idx, ys, x2d, mod, g_final)


def kernel(x_prompt, x_sample, c, cache_k, cache_v, state_C, state_n, state_m, c_ctx, w_ada, b_ada, g_norm1, g_norm2, w_in, b_gates, rpb, w_fourier, g_head, w_out, w_router, w_exp_gate, w_exp_up, w_exp_down, g_final):
    n_ctx, len_ctx, _ = x_prompt.shape
    n_lat, len_lat, _ = x_sample.shape
    past = cache_k.shape[2]
    cap_ctx = CAPACITY_FACTOR * len_ctx // N_EXPERTS
    cap_lat = CAPACITY_FACTOR * len_lat // N_EXPERTS

    w_in_bf = w_in.astype(BF16)
    w_gt_bf = jnp.swapaxes(w_in[:, :, P_COLS:], 1, 2).astype(BF16)
    vc0, oc0 = VC_BLK * C_W, OC_BLK * C_W
    w_vo_bf = jnp.swapaxes(jnp.concatenate([w_in[:, :, vc0:vc0 + C_W], w_in[:, :, oc0:oc0 + C_W]], axis=2),
                           1, 2).astype(BF16)
    bg_row = b_gates.reshape(DEPTH, 1, N_GATE_COLS).astype(F32)
    bg_col = b_gates.reshape(DEPTH, N_GATE_COLS, 1).astype(F32)
    w_out_bf = w_out.astype(BF16)
    w_rt_bf = jnp.swapaxes(w_router, 1, 2).astype(BF16)
    g1 = g_norm1.reshape(DEPTH, 1, D_MODEL)
    g2 = g_norm2.reshape(DEPTH, 1, D_MODEL)
    gh = g_head.reshape(DEPTH, 1, D_MODEL)
    g_mem = jnp.broadcast_to(gh[:, 0, A_W + B_W:, None], (DEPTH, C_W, MERGE_TILE))
    eye_g = jnp.eye(G_FOURIER, dtype=F32)
    wf_blk = jnp.einsum("lgcd,gh->lgchd", w_fourier, eye_g).reshape(DEPTH, B_W, B_W).astype(BF16)

    csc = _channel_dft()
    dft_ctx = _dft_mats(len_ctx)
    dft_side = _dft_mats(GRID_W)
    tw_cos, tw_sin = _twiddles(GRID_W, len_lat)
    bias_tiles = _nbr_bias_tiles(rpb)
    r = np.arange(MLSTM_CHUNK)
    tril = jnp.asarray(r[:, None] >= r[None, :], F32)
    triu = jnp.asarray(r[:, None] <= r[None, :], F32)
    triu_bf = triu.astype(BF16)
    hidx = np.arange(A_W) // HEAD_DIM
    ones_blk = jnp.asarray(hidx[:, None] == hidx[None, :], BF16)

    cvecs = jnp.concatenate([c_ctx[None, :], c, jnp.zeros((8 - 1 - n_lat, D_MODEL), F32)], axis=0)
    mod_all = _modulation(cvecs, w_ada, b_ada).reshape(DEPTH, 8, 6, D_MODEL)

    cache_k4 = cache_k.reshape(n_lat, DEPTH, past, A_W)
    cache_v4 = cache_v.reshape(n_lat, DEPTH, past, A_W)
    zero_c = jnp.zeros((n_ctx, 2, H_MLSTM, HEAD_DIM, HEAD_DIM), F32)
    zero_n = jnp.zeros((n_ctx, 2, H_MLSTM, HEAD_DIM), F32)
    zero_m = jnp.zeros((n_ctx, 2, H_MLSTM), F32)

    xc = x_prompt.reshape(n_ctx * len_ctx, D_MODEL)
    xl = x_sample.reshape(n_lat * len_lat, D_MODEL)
    gf = g_final.reshape(1, D_MODEL)
    ks, vs, cs, ns, ms = [], [], [], [], []
    for l in range(DEPTH):
        mod_c = mod_all[l, 0:1]
        mod_l = mod_all[l, 1:1 + n_lat]

        pc, gc, gtc, voc, abc = _inproj(xc, mod_c, l, g1, w_in_bf, w_gt_bf, w_vo_bf, bg_row, bg_col, csc,
                                        n_ctx * len_ctx)
        att_c = _ctx_attention(pc, n_ctx, len_ctx)
        four_c = _fourier(abc, dft_ctx[0], dft_ctx[1], wf_blk, l, n_ctx, len_ctx)
        hf_c, hb_c, c_new, n_new, m_new = _mlstm(pc, gc, gtc, voc, zero_c, zero_n, zero_m, tril, triu, n_ctx, len_ctx)
        xc, h2c, affc = _merge(att_c, four_c, hf_c, hb_c, voc, xc, mod_c, l, gh, g_mem, w_out_bf, g2, w_rt_bf,
                               ones_blk, n_ctx, len_ctx, n_ctx * len_ctx)
        ks.append(pc[:, A_W:2 * A_W].reshape(n_ctx, len_ctx, H_ATT, HEAD_DIM))
        vs.append(pc[:, 2 * A_W:3 * A_W].reshape(n_ctx, len_ctx, H_ATT, HEAD_DIM))
        cs.append(c_new)
        ns.append(n_new)
        ms.append(m_new)

        pq, gq, gtq, voq, abq = _inproj(xl, mod_l, l, g1, w_in_bf, w_gt_bf, w_vo_bf, bg_row, bg_col, csc, len_lat)
        att_l = _nbr_attention(pq, cache_k4, cache_v4, bias_tiles, l, n_lat, len_lat)
        four_l = _fourier_grid(abq, dft_side[0], dft_side[1], tw_cos, tw_sin, wf_blk, l, n_lat, GRID_W)
        hf_l, hb_l, _, _, _ = _mlstm(pq, gq, gtq, voq, state_C[:, l], state_n[:, l], state_m[:, l], tril, triu,
                                     n_lat, len_lat)
        xl, h2l, affl = _merge(att_l, four_l, hf_l, hb_l, voq, xl, mod_l, l, gh, g_mem, w_out_bf, g2, w_rt_bf,
                               ones_blk, n_lat, len_lat, len_lat)

        last = l == DEPTH - 1
        idx_c, gs_c = _route(affc, triu_bf, n_ctx, len_ctx, cap_ctx, n_ctx)
        idx_l, gs_l = _route(affl, triu_bf, n_lat, len_lat, cap_lat, 1)
        xs_c = _gather(idx_c, h2c, n_ctx, cap_ctx, N_EXPERTS)
        xs_l = _gather(idx_l, h2l, n_lat, cap_lat, 4)
        ys_c, ys_l = _experts(xs_c, xs_l, gs_c, gs_l, w_exp_gate, w_exp_up, w_exp_down, l)
        xc = _scatter(idx_c, ys_c, xc, jnp.broadcast_to(mod_c, (n_ctx, 6, D_MODEL)), gf, n_ctx, len_ctx, cap_ctx,
                      N_EXPERTS, len_ctx, last)
        xl = _scatter(idx_l, ys_l, xl, mod_l, gf, n_lat, len_lat, cap_lat, 2, 512, last)

    y_prompt = xc.reshape(n_ctx, len_ctx, D_MODEL)
    y_sample = xl.reshape(n_lat, len_lat, D_MODEL)
    return (y_prompt, y_sample, jnp.stack(ks, axis=1), jnp.stack(vs, axis=1), jnp.stack(cs, axis=1),
            jnp.stack(ns, axis=1), jnp.stack(ms, axis=1))
```

```python
import functools

import numpy as np
import jax
import jax.numpy as jnp
from jax import lax
from jax.experimental import pallas as pl
from jax.experimental.pallas import tpu as pltpu

F32 = jnp.float32
BF16 = jnp.bfloat16

D_MODEL = 1024
DEPTH = 2
HEAD_DIM = 64
H_ATT = 8
G_FOURIER = 4
H_MLSTM = 4
A_W = H_ATT * HEAD_DIM
B_W = G_FOURIER * HEAD_DIM
C_W = H_MLSTM * HEAD_DIM
N_GATE_COLS = 16
P_COLS = 3 * A_W + B_W + 4 * C_W
IN_COLS = P_COLS + N_GATE_COLS
GRID_W = 64
WIN_R = 8
WIN_C = 16
MLSTM_CHUNK = 128
N_EXPERTS = 16
CAPACITY_FACTOR = 2
EXPERT_FF = 2 * D_MODEL
EPS = 1e-6
NEG = -1e30

UB_OFF = 3 * A_W
QC_BLK, KC_BLK, VC_BLK, OC_BLK = 7, 8, 9, 10

LANE = 128
N_SLAB = D_MODEL // LANE
MERGE_TILE = 256
INPROJ_TILE = 512

NT_DIMS = (((1,), (1,)), ((), ()))
TN_DIMS = (((0,), (0,)), ((), ()))
MIB = 1024 * 1024


def _dot(a, b, precision=None):
    return jnp.dot(a, b, preferred_element_type=F32, precision=precision)


def _dot_nt(a, b, precision=None):
    return lax.dot_general(a, b, NT_DIMS, preferred_element_type=F32, precision=precision)


def _dot_tn(a, b):
    return lax.dot_general(a, b, TN_DIMS, preferred_element_type=F32)


def _params(sem, vmem_mib=48):
    return pltpu.CompilerParams(dimension_semantics=sem, vmem_limit_bytes=vmem_mib * MIB)


def _silu(x):
    return x * jax.nn.sigmoid(x)


def _log_sigmoid(x):
    return jnp.minimum(x, 0.0) - jnp.log1p(jnp.exp(-jnp.abs(x)))


def _mod_kernel(c_ref, w_ref, b_ref, o_ref):
    s = _silu(c_ref[...]).astype(BF16)
    o_ref[0] = _dot(s, w_ref[0].astype(BF16)) + b_ref[0]


def _modulation(cvecs, w_ada, b_ada):
    depth = w_ada.shape[0]
    tn = 1024
    return pl.pallas_call(
        _mod_kernel,
        grid=(depth, 6 * D_MODEL // tn),
        in_specs=[
            pl.BlockSpec((8, D_MODEL), lambda l, j: (0, 0)),
            pl.BlockSpec((1, D_MODEL, tn), lambda l, j: (l, 0, j)),
            pl.BlockSpec((1, 1, tn), lambda l, j: (l, 0, j)),
        ],
        out_specs=pl.BlockSpec((1, 8, tn), lambda l, j: (l, 0, j)),
        out_shape=jax.ShapeDtypeStruct((depth, 8, 6 * D_MODEL), F32),
        compiler_params=_params(("arbitrary", "arbitrary")),
        name="modulation",
    )(cvecs, w_ada, b_ada.reshape(depth, 1, 6 * D_MODEL))


def _inproj_kernel(x_ref, mod_ref, g1_ref, w_ref, wgt_ref, wvo_ref, bgr_ref, bgc_ref, csc_ref,
                   p_ref, g_ref, gt_ref, vo_ref, ab_ref, *scratch, grid_rows):
    x = x_ref[...]
    y = x * lax.rsqrt(jnp.mean(x * x, axis=-1, keepdims=True) + EPS) * g1_ref[0]
    h = (y * (1.0 + mod_ref[0, 1:2, :]) + mod_ref[0, 0:1, :]).astype(BF16)
    for j in range(0, P_COLS, 256):
        pj = _dot(h, w_ref[0, :, j:j + 256])
        p_ref[:, j:j + 256] = pj
        if j == UB_OFF:
            ab = _dot(pj.astype(BF16), csc_ref[...])
            if grid_rows:
                stage_ref, = scratch
                n_lt = 2 * B_W // LANE
                for c in range(n_lt):
                    stage_ref[c] = ab[:, LANE * c:LANE * (c + 1)]
                for n2 in range(GRID_W):
                    for c in range(n_lt):
                        col = 2 * B_W * n2 + LANE * c
                        ab_ref[:, col:col + LANE] = stage_ref[c, pl.ds(n2, grid_rows, stride=GRID_W), :]
            else:
                ab_ref[...] = ab.astype(BF16)
    g_ref[...] = _dot(h, w_ref[0, :, P_COLS:IN_COLS]) + bgr_ref[0]
    gt_ref[...] = _dot_nt(wgt_ref[0], h) + bgc_ref[0]
    vo_ref[...] = _dot_nt(wvo_ref[0], h)


def _inproj(x2d, mod, layer, g1, w_in_bf, w_gt_bf, w_vo_bf, bg_row, bg_col, csc, seq_len, grid_ab):
    t = x2d.shape[0]
    tm = INPROJ_TILE
    tiles_per_seq = seq_len // tm
    grid_rows = tm // GRID_W if grid_ab else 0
    if grid_ab:
        ab_spec = pl.BlockSpec((grid_rows, GRID_W * 2 * B_W), lambda i: (i, 0))
        ab_shape = jax.ShapeDtypeStruct((t // GRID_W, GRID_W * 2 * B_W), F32)
        scratch = [pltpu.VMEM((2 * B_W // LANE, tm, LANE), F32)]
    else:
        ab_spec = pl.BlockSpec((tm, 2 * B_W), lambda i: (i, 0))
        ab_shape = jax.ShapeDtypeStruct((t, 2 * B_W), BF16)
        scratch = []
    return pl.pallas_call(
        functools.partial(_inproj_kernel, grid_rows=grid_rows),
        grid=(t // tm,),
        in_specs=[
            pl.BlockSpec((tm, D_MODEL), lambda i: (i, 0)),
            pl.BlockSpec((1, 6, D_MODEL), lambda i: (i // tiles_per_seq, 0, 0)),
            pl.BlockSpec((1, 1, D_MODEL), lambda i: (layer, 0, 0)),
            pl.BlockSpec((1, D_MODEL, IN_COLS), lambda i: (layer, 0, 0)),
            pl.BlockSpec((1, N_GATE_COLS, D_MODEL), lambda i: (layer, 0, 0)),
            pl.BlockSpec((1, 2 * C_W, D_MODEL), lambda i: (layer, 0, 0)),
            pl.BlockSpec((1, 1, N_GATE_COLS), lambda i: (layer, 0, 0)),
            pl.BlockSpec((1, N_GATE_COLS, 1), lambda i: (layer, 0, 0)),
            pl.BlockSpec((B_W, 2 * B_W), lambda i: (0, 0)),
        ],
        out_specs=[
            pl.BlockSpec((tm, P_COLS), lambda i: (i, 0)),
            pl.BlockSpec((tm, N_GATE_COLS), lambda i: (i, 0)),
            pl.BlockSpec((N_GATE_COLS, tm), lambda i: (0, i)),
            pl.BlockSpec((2 * C_W, tm), lambda i: (0, i)),
            ab_spec,
        ],
        out_shape=[
            jax.ShapeDtypeStruct((t, P_COLS), F32),
            jax.ShapeDtypeStruct((t, N_GATE_COLS), F32),
            jax.ShapeDtypeStruct((N_GATE_COLS, t), F32),
            jax.ShapeDtypeStruct((2 * C_W, t), F32),
            ab_shape,
        ],
        scratch_shapes=scratch,
        compiler_params=_params(("arbitrary",)),
        name="inproj",
    )(x2d, mod, g1, w_in_bf, w_gt_bf, w_vo_bf, bg_row, bg_col, csc)


def _ctx_attn_kernel(q_ref, k_ref, v_ref, o_ref):
    scale = HEAD_DIM ** -0.5
    heads = range(H_ATT)
    sl = [slice(HEAD_DIM * h, HEAD_DIM * (h + 1)) for h in heads]
    s = [_dot_nt((q_ref[:, sl[h]] * scale).astype(BF16), k_ref[:, sl[h]].astype(BF16)) for h in heads]
    e = [jnp.exp(s[h] - jnp.max(s[h], axis=-1, keepdims=True)) for h in heads]
    w = [e[h] * (1.0 / jnp.sum(e[h], axis=-1, keepdims=True)) for h in heads]
    o_ref[...] = jnp.concatenate([_dot(w[h].astype(BF16), v_ref[:, sl[h]].astype(BF16)) for h in heads], axis=1)


def _ctx_attention(p, n_seq, seq_len):
    return pl.pallas_call(
        _ctx_attn_kernel,
        grid=(n_seq,),
        in_specs=[
            pl.BlockSpec((seq_len, A_W), lambda b: (b, 0)),
            pl.BlockSpec((seq_len, A_W), lambda b: (b, 1)),
            pl.BlockSpec((seq_len, A_W), lambda b: (b, 2)),
        ],
        out_specs=pl.BlockSpec((seq_len, A_W), lambda b: (b, 0)),
        out_shape=jax.ShapeDtypeStruct((n_seq * seq_len, A_W), F32),
        compiler_params=_params(("arbitrary",)),
        name="ctx_attention",
    )(p, p, p)


Q_ROWS = 8
K_ROWS = 16
KEY_BLK = 256


NO_ROW = 2 * WIN_R - 1


def _nbr_bias_tiles(rpb):
    n_c = 2 * WIN_C - 1
    cq = np.arange(GRID_W)[:, None]
    ck = np.arange(GRID_W)[None, :]
    cs = np.clip(cq - WIN_C // 2, 0, GRID_W - WIN_C)
    col_ok = (ck >= cs) & (ck < cs + WIN_C)
    pick = np.where(col_ok, np.clip(ck - cq + WIN_C - 1, 0, n_c - 1), n_c)
    sel = np.zeros((2, GRID_W, 2 * GRID_W, n_c + 1), np.float32)
    for side in range(2):
        sel[side, cq, side * GRID_W + ck, pick] = 1.0
    ext = jnp.full(rpb.shape[:2] + (2 * WIN_R, n_c + 1), NEG, F32)
    ext = ext.at[:, :, :2 * WIN_R - 1, :n_c].set(rpb.astype(F32))
    return jnp.einsum("lhdm,sqkm->lhdsqk", ext, jnp.asarray(sel), precision=lax.Precision.HIGHEST)


def _nbr_attn_kernel(q_ref, k0_ref, k1_ref, k2_ref, k3_ref, v0_ref, v1_ref, v2_ref, v3_ref,
                     ck_ref, cv_ref, tab_ref, o_ref):
    scale = HEAD_DIM ** -0.5
    k_refs = (k0_ref, k1_ref, k2_ref, k3_ref)
    v_refs = (v0_ref, v1_ref, v2_ref, v3_ref)
    rb = pl.program_id(1)
    rows = GRID_W
    key_row0 = jnp.clip(Q_ROWS * rb - WIN_R // 2, 0, rows - K_ROWS)
    d = []
    for rq in range(Q_ROWS):
        r = Q_ROWS * rb + rq
        rs = jnp.clip(r - WIN_R // 2, 0, rows - WIN_R)
        d_row = []
        for rk in range(K_ROWS):
            rka = key_row0 + rk
            ok = (rka >= rs) & (rka < rs + WIN_R)
            d_row.append(jnp.where(ok, rka - r + WIN_R - 1, NO_ROW))
        d.append(d_row)
    rows_per_blk = KEY_BLK // GRID_W
    heads = range(2)
    sl = [slice(HEAD_DIM * hh, HEAD_DIM * (hh + 1)) for hh in heads]
    q = [(q_ref[:, sl[hh]] * scale).astype(BF16) for hh in heads]
    s_ctx = [_dot_nt(q[hh], ck_ref[0, 0, :, sl[hh]].astype(BF16)) for hh in heads]
    s_loc = [[] for _ in heads]
    for j in range(4):
        for hh in heads:
            bias = jnp.concatenate([
                jnp.concatenate([
                    tab_ref[0, hh, d[rq][rows_per_blk * j + 2 * p], 0]
                    + tab_ref[0, hh, d[rq][rows_per_blk * j + 2 * p + 1], 1]
                    for p in range(rows_per_blk // 2)], axis=1)
                for rq in range(Q_ROWS)], axis=0)
            s_loc[hh].append(_dot_nt(q[hh], k_refs[j][:, sl[hh]].astype(BF16)) + bias)
    m = [jnp.max(s_ctx[hh], axis=-1, keepdims=True) for hh in heads]
    for j in range(4):
        m = [jnp.maximum(m[hh], jnp.max(s_loc[hh][j], axis=-1, keepdims=True)) for hh in heads]
    e_ctx = [jnp.exp(s_ctx[hh] - m[hh]) for hh in heads]
    den = [jnp.sum(e_ctx[hh], axis=-1, keepdims=True) for hh in heads]
    num = [_dot(e_ctx[hh].astype(BF16), cv_ref[0, 0, :, sl[hh]].astype(BF16)) for hh in heads]
    for j in range(4):
        e = [jnp.exp(s_loc[hh][j] - m[hh]) for hh in heads]
        den = [den[hh] + jnp.sum(e[hh], axis=-1, keepdims=True) for hh in heads]
        num = [num[hh] + _dot(e[hh].astype(BF16), v_refs[j][:, sl[hh]].astype(BF16)) for hh in heads]
    o_ref[...] = jnp.concatenate([num[hh] / den[hh] for hh in heads], axis=1)


def _nbr_attention(p, cache_k4, cache_v4, bias_tiles, layer, n_seq, seq_len):
    q_tok = Q_ROWS * GRID_W
    n_rb = seq_len // q_tok
    kb_per_seq = seq_len // KEY_BLK
    max_base = kb_per_seq - 4

    def kmap(j, col0):
        def f(hp, rb, b):
            base = jnp.clip(2 * rb - 1, 0, max_base)
            return (b * kb_per_seq + base + j, col0 + hp)
        return f

    past = cache_k4.shape[2]
    in_specs = [pl.BlockSpec((q_tok, 128), lambda hp, rb, b: (b * n_rb + rb, hp))]
    in_specs += [pl.BlockSpec((KEY_BLK, 128), kmap(j, A_W // 128)) for j in range(4)]
    in_specs += [pl.BlockSpec((KEY_BLK, 128), kmap(j, 2 * A_W // 128)) for j in range(4)]
    in_specs += [
        pl.BlockSpec((1, 1, past, 128), lambda hp, rb, b: (b, layer, 0, hp)),
        pl.BlockSpec((1, 1, past, 128), lambda hp, rb, b: (b, layer, 0, hp)),
        pl.BlockSpec((1, 2, 2 * WIN_R, 2, GRID_W, 2 * GRID_W), lambda hp, rb, b: (layer, hp, 0, 0, 0, 0)),
    ]
    return pl.pallas_call(
        _nbr_attn_kernel,
        grid=(H_ATT // 2, n_rb, n_seq),
        in_specs=in_specs,
        out_specs=pl.BlockSpec((q_tok, 128), lambda hp, rb, b: (b * n_rb + rb, hp)),
        out_shape=jax.ShapeDtypeStruct((n_seq * seq_len, A_W), F32),
        compiler_params=_params(("arbitrary", "arbitrary", "arbitrary")),
        name="nbr_attention",
    )(p, p, p, p, p, p, p, p, p, cache_k4, cache_v4, bias_tiles)


def _dft_mats(n):
    idx = jnp.arange(n, dtype=jnp.int32)
    ang = ((idx[:, None] * idx[None, :]) % n).astype(F32) * (2.0 * np.pi / n)
    return jnp.cos(ang).astype(BF16), jnp.sin(ang).astype(BF16)


def _channel_dft():
    c = np.arange(HEAD_DIM)
    ang = 2.0 * np.pi * ((c[:, None] * c[None, :]) % HEAD_DIM) / HEAD_DIM
    eye = np.eye(G_FOURIER)
    mats = np.concatenate([np.kron(eye, np.cos(ang)), np.kron(eye, np.sin(ang))], axis=1)
    return jnp.asarray(mats, F32).astype(BF16)


def _fourier_kernel(c_ref, s_ref, ab_ref, wf_ref, o_ref, acc_ref, *, scale, n_k):
    k = pl.program_id(2)

    @pl.when(k == 0)
    def _():
        acc_ref[...] = jnp.zeros_like(acc_ref)

    acc_ref[...] += _dot(c_ref[...], ab_ref[:, :B_W]) - _dot(s_ref[...], ab_ref[:, B_W:])

    @pl.when(k == n_k - 1)
    def _():
        z = (acc_ref[...] * scale).astype(BF16)
        o = _dot(z, wf_ref[0])
        for c in range(B_W // LANE):
            o_ref[c] = o[:, LANE * c:LANE * (c + 1)]


def _fourier(ab, cmat, smat, wf_blk, layer, n_seq, seq_len):
    ti = min(seq_len, 512)
    tk = min(seq_len, 1024)
    n_i, n_k = seq_len // ti, seq_len // tk
    scale = float((seq_len * HEAD_DIM) ** -0.5)
    return pl.pallas_call(
        functools.partial(_fourier_kernel, scale=scale, n_k=n_k),
        grid=(n_seq, n_i, n_k),
        in_specs=[
            pl.BlockSpec((ti, tk), lambda s, i, k: (i, k)),
            pl.BlockSpec((ti, tk), lambda s, i, k: (i, k)),
            pl.BlockSpec((tk, 2 * B_W), lambda s, i, k: (s * n_k + k, 0)),
            pl.BlockSpec((1, B_W, B_W), lambda s, i, k: (layer, 0, 0)),
        ],
        out_specs=pl.BlockSpec((B_W // LANE, ti, LANE), lambda s, i, k: (0, s * n_i + i, 0)),
        out_shape=jax.ShapeDtypeStruct((B_W // LANE, n_seq * seq_len, LANE), F32),
        scratch_shapes=[pltpu.VMEM((ti, B_W), F32)],
        compiler_params=_params(("arbitrary", "arbitrary", "arbitrary")),
        name="fourier",
    )(cmat, smat, ab, wf_blk)


FS_GROUP = 8


def _twiddles(side, n):
    k1 = jnp.arange(side, dtype=jnp.int32)[:, None]
    n2 = jnp.arange(side, dtype=jnp.int32)[None, :]
    ang = (k1 * n2).astype(F32) * (2.0 * np.pi / n)
    return jnp.cos(ang), jnp.sin(ang)


def _fourier_grid_kernel(ab_ref, c_ref, s_ref, tc_ref, ts_ref, wf_ref, o_ref, y_ref, *, side, scale):
    cmat = c_ref[...]
    smat = s_ref[...]
    for g in range(side // FS_GROUP):
        ab = ab_ref[:, 2 * B_W * FS_GROUP * g:2 * B_W * FS_GROUP * (g + 1)].astype(BF16)
        m1 = _dot(cmat, ab)
        m2 = _dot(smat, ab)
        for t in range(FS_GROUP):
            n2 = FS_GROUP * g + t
            a0 = 2 * B_W * t
            yr = m1[:, a0:a0 + B_W] - m2[:, a0 + B_W:a0 + 2 * B_W]
            yi = -(m1[:, a0 + B_W:a0 + 2 * B_W] + m2[:, a0:a0 + B_W])
            ct = tc_ref[:, n2:n2 + 1]
            st = ts_ref[:, n2:n2 + 1]
            y = jnp.concatenate([yr * ct + yi * st, yi * ct - yr * st], axis=1)
            for c in range(2 * B_W // LANE):
                y_ref[c, pl.ds(n2, side, stride=side), :] = y[:, LANE * c:LANE * (c + 1)]
    for g in range(side // FS_GROUP):
        zs = []
        for j in range(FS_GROUP):
            rows = slice(side * (FS_GROUP * g + j), side * (FS_GROUP * g + j + 1))
            n_lt = B_W // LANE
            y_re = jnp.concatenate([y_ref[c, rows, :] for c in range(n_lt)], axis=1).astype(BF16)
            y_im = jnp.concatenate([y_ref[n_lt + c, rows, :] for c in range(n_lt)], axis=1).astype(BF16)
            zs.append(_dot(cmat, y_re) + _dot(smat, y_im))
        z = (jnp.concatenate(zs, axis=0) * scale).astype(BF16)
        o = _dot(z, wf_ref[0])
        for j in range(FS_GROUP):
            for c in range(B_W // LANE):
                o_ref[c, pl.ds(FS_GROUP * g + j, side, stride=side), :] = o[side * j:side * (j + 1),
                                                                            LANE * c:LANE * (c + 1)]


def _fourier_grid(ab_grid, cmat, smat, tw_cos, tw_sin, wf_blk, layer, n_seq, side):
    seq_len = side * side
    scale = float((seq_len * HEAD_DIM) ** -0.5)
    small = pl.BlockSpec((side, side), lambda s: (0, 0))
    return pl.pallas_call(
        functools.partial(_fourier_grid_kernel, side=side, scale=scale),
        grid=(n_seq,),
        in_specs=[
            pl.BlockSpec((side, side * 2 * B_W), lambda s: (s, 0)),
            small,
            small,
            small,
            small,
            pl.BlockSpec((1, B_W, B_W), lambda s: (layer, 0, 0)),
        ],
        out_specs=pl.BlockSpec((B_W // LANE, seq_len, LANE), lambda s: (0, s, 0)),
        out_shape=jax.ShapeDtypeStruct((B_W // LANE, n_seq * seq_len, LANE), F32),
        scratch_shapes=[pltpu.VMEM((2 * B_W // LANE, seq_len, LANE), F32)],
        compiler_params=_params(("arbitrary",), vmem_mib=56),
        name="fourier_grid",
    )(ab_grid, cmat, smat, tw_cos, tw_sin, wf_blk)


SEQS_PER_STEP = 2


def _mlstm_kernel(*refs, n_chunks):
    sps = SEQS_PER_STEP
    n_side = 3 + 2 * sps
    fwd, bwd = refs[:n_side], refs[n_side:2 * n_side]
    c0_ref, n0_ref, m0_ref, tril_ref, triu_ref = refs[2 * n_side:2 * n_side + 5]
    hf_ref, hb_ref, cout_ref, nout_ref, mout_ref, c_s, n_s, m_s = refs[2 * n_side + 5:]
    c = pl.program_id(1)
    hi = lax.Precision.HIGHEST
    lc = MLSTM_CHUNK
    pair_w = 2 * HEAD_DIM
    n_pairs = H_MLSTM // 2

    @pl.when(c == 0)
    def _():
        c_s[...] = c0_ref[...]
        n_s[...] = n0_ref[...]
        m_s[...] = m0_ref[...]

    lo_lane = lax.broadcasted_iota(jnp.int32, (1, pair_w), 1) < HEAD_DIM
    lo_row = lax.broadcasted_iota(jnp.int32, (pair_w, 1), 0) < HEAD_DIM
    row8 = lax.broadcasted_iota(jnp.int32, (8, 1), 0)
    cum_mask = [tril_ref[...], triu_ref[...]]
    keep_t = [triu_ref[...] > 0.5, tril_ref[...] > 0.5]
    pairs = [(j, d, hp) for j in range(sps) for d in range(2) for hp in range(n_pairs)]
    heads = [(pi, hh) for pi in range(len(pairs)) for hh in range(2)]
    rng = range(len(heads))

    pre = {}
    for j in range(sps):
        for d, side in enumerate((fwd, bwd)):
            q_ref, k_ref, g_ref = side[:3]
            gt_ref, vt_ref = side[3 + 2 * j], side[4 + 2 * j]
            go = 2 * H_MLSTM * d
            lf_cols = _log_sigmoid(g_ref[j, :, go + H_MLSTM:go + 2 * H_MLSTM])
            lf_rows = _log_sigmoid(gt_ref[go + H_MLSTM:go + 2 * H_MLSTM, :])
            b_cols = _dot(cum_mask[d], lf_cols, precision=hi)
            pre[j, d] = dict(
                a_cols=g_ref[j, :, go:go + H_MLSTM] - b_cols,
                ig_rows=gt_ref[go:go + H_MLSTM, :],
                b_rows=_dot_nt(lf_rows, cum_mask[d], precision=hi),
                q=q_ref[j].astype(BF16), k=(k_ref[j] * (HEAD_DIM ** -0.5)).astype(BF16), vt=vt_ref[...])

    def pair_cols(hp):
        return slice(pair_w * hp, pair_w * (hp + 1))

    q_p = [pre[j, d]["q"][:, pair_cols(hp)] for j, d, hp in pairs]
    k_p = [pre[j, d]["k"][:, pair_cols(hp)] for j, d, hp in pairs]
    vt_p = [pre[j, d]["vt"][pair_cols(hp), :] for j, d, hp in pairs]
    c_p = [c_s[j, d, hp] for j, d, hp in pairs]
    n_p = [n_s[j, n_pairs * d + hp:n_pairs * d + hp + 1, :] for j, d, hp in pairs]
    zero_k = jnp.zeros((lc, pair_w), BF16)
    k_h = [jnp.where(lo_lane, k_p[pi], zero_k) if hh == 0 else jnp.where(lo_lane, zero_k, k_p[pi])
           for pi, hh in heads]

    def head_of(i):
        pi, hh = heads[i]
        j, d, hp = pairs[pi]
        return j, d, 2 * hp + hh

    b_row = [pre[head_of(i)[0], head_of(i)[1]]["b_rows"][head_of(i)[2]:head_of(i)[2] + 1, :] for i in rng]
    ig_row = [pre[head_of(i)[0], head_of(i)[1]]["ig_rows"][head_of(i)[2]:head_of(i)[2] + 1, :] for i in rng]
    a_col = [pre[head_of(i)[0], head_of(i)[1]]["a_cols"][:, head_of(i)[2]:head_of(i)[2] + 1] for i in rng]
    m_st = [m_s[head_of(i)[0], H_MLSTM * head_of(i)[1] + head_of(i)[2]:H_MLSTM * head_of(i)[1] + head_of(i)[2] + 1, :]
            for i in rng]
    bl = [b_row[i][:, lc - 1:lc] if head_of(i)[1] == 0 else b_row[i][:, 0:1] for i in rng]

    d_t = [jnp.where(keep_t[head_of(i)[1]], b_row[i] + a_col[i], NEG) for i in rng]
    inter = [b_row[i] + m_st[i] for i in rng]
    m_t = [jnp.maximum(inter[i], jnp.max(d_t[i], axis=0, keepdims=True)) for i in rng]
    s_t = [_dot_nt(k_h[i], q_p[heads[i][0]]) * jnp.exp(d_t[i] - m_t[i]) for i in rng]
    w_in = [jnp.exp(inter[i] - m_t[i]) for i in rng]
    num_t = [_dot(vt_p[heads[i][0]].astype(BF16), s_t[i].astype(BF16)) for i in rng]
    qc_t = [_dot_nt(c_p[pi].astype(BF16), q_p[pi]) for pi in range(len(pairs))]
    n_mat = [jnp.where((row8 == 0) & lo_lane, n_p[pi], jnp.where((row8 == 1) & ~lo_lane, n_p[pi], 0.0))
             for pi in range(len(pairs))]
    nq = [_dot_nt(n_mat[pi].astype(BF16), q_p[pi]) for pi in range(len(pairs))]
    den = [jnp.sum(s_t[i], axis=0, keepdims=True) + w_in[i] * nq[heads[i][0]][heads[i][1]:heads[i][1] + 1, :]
           for i in rng]
    inv = [1.0 / jnp.maximum(jnp.abs(den[i]), jnp.exp(-m_t[i])) for i in rng]
    h_t = []
    for pi in range(len(pairs)):
        i0, i1 = 2 * pi, 2 * pi + 1
        num = jnp.where(lo_row, num_t[i0], num_t[i1])
        h_t.append((num + jnp.where(lo_row, w_in[i0], w_in[i1]) * qc_t[pi]) * jnp.where(lo_row, inv[i0], inv[i1]))
    for j in range(sps):
        base = 2 * n_pairs * j
        hf_ref[j] = jnp.concatenate(h_t[base:base + n_pairs], axis=0)
        hb_ref[j] = jnp.concatenate(h_t[base + n_pairs:base + 2 * n_pairs], axis=0)

    g_row = [bl[i] - b_row[i] + ig_row[i] for i in rng]
    m_new = [jnp.maximum(bl[i] + m_st[i], jnp.max(g_row[i], axis=-1, keepdims=True)) for i in rng]
    wc = [jnp.exp(bl[i] + m_st[i] - m_new[i]) for i in rng]
    ws_row = [jnp.exp(g_row[i] - m_new[i]) for i in rng]
    upd = [_dot((vt_p[heads[i][0]] * ws_row[i]).astype(BF16), k_p[heads[i][0]]) for i in rng]
    for pi, (j, d, hp) in enumerate(pairs):
        i0, i1 = 2 * pi, 2 * pi + 1
        block = jnp.where(lo_row & lo_lane, upd[i0], jnp.where(~lo_row & ~lo_lane, upd[i1], 0.0))
        c_s[j, d, hp] = jnp.where(lo_row, wc[i0], wc[i1]) * c_p[pi] + block
        ws_mat = jnp.where(row8 == 0, ws_row[i0], jnp.where(row8 == 1, ws_row[i1], 0.0))
        k_sum = _dot(ws_mat.astype(BF16), k_p[pi])
        row = n_pairs * d + hp
        n_s[j, row:row + 1, :] = (jnp.where(lo_lane, wc[i0], wc[i1]) * n_p[pi]
                                  + jnp.where(lo_lane, k_sum[0:1, :], k_sum[1:2, :]))
    for i in rng:
        j, d, hd = head_of(i)
        m_s[j, H_MLSTM * d + hd:H_MLSTM * d + hd + 1, :] = m_new[i]

    @pl.when(c == n_chunks - 1)
    def _():
        cout_ref[...] = c_s[...]
        nout_ref[...] = n_s[...]
        mout_ref[...] = m_s[...]


def _pair_states(c):
    b = c.shape[0]
    c = c.reshape(b, 2, H_MLSTM // 2, 2, HEAD_DIM, HEAD_DIM)
    zero = jnp.zeros_like(c[:, :, :, 0])
    top = jnp.concatenate([c[:, :, :, 0], zero], axis=-1)
    bottom = jnp.concatenate([zero, c[:, :, :, 1]], axis=-1)
    return jnp.concatenate([top, bottom], axis=-2)


def _unpair_states(cp):
    b = cp.shape[0]
    first = cp[:, :, :, :HEAD_DIM, :HEAD_DIM]
    second = cp[:, :, :, HEAD_DIM:, HEAD_DIM:]
    return jnp.stack([first, second], axis=3).reshape(b, 2, H_MLSTM, HEAD_DIM, HEAD_DIM)


def _mlstm(p, g, gt, vo, c0, n0, m0, tril, triu, n_seq, seq_len):
    lc = MLSTM_CHUNK
    nc = seq_len // lc
    n_st = 2 * H_MLSTM
    n_pairs = H_MLSTM // 2
    pair_w = 2 * HEAD_DIM
    sps = SEQS_PER_STEP
    p3 = p.reshape(n_seq, seq_len, P_COLS)
    g3 = g.reshape(n_seq, seq_len, N_GATE_COLS)

    def fwd(c):
        return c

    def bwd(c):
        return nc - 1 - c

    def side(chunk):
        tok = lambda col: (lambda b, c: (b, chunk(c), col))
        specs = [
            pl.BlockSpec((sps, lc, C_W), tok(QC_BLK)),
            pl.BlockSpec((sps, lc, C_W), tok(KC_BLK)),
            pl.BlockSpec((sps, lc, N_GATE_COLS), tok(0)),
        ]
        for j in range(sps):
            lanes = lambda b, c, j=j: (0, (b * sps + j) * nc + chunk(c))
            specs += [pl.BlockSpec((N_GATE_COLS, lc), lanes), pl.BlockSpec((C_W, lc), lanes)]
        return specs

    state_specs = [
        pl.BlockSpec((sps, 2, n_pairs, pair_w, pair_w), lambda b, c: (b, 0, 0, 0, 0)),
        pl.BlockSpec((sps, 2 * n_pairs, pair_w), lambda b, c: (b, 0, 0)),
        pl.BlockSpec((sps, n_st, 1), lambda b, c: (b, 0, 0)),
    ]
    tri_spec = pl.BlockSpec((lc, lc), lambda b, c: (0, 0))
    operands = [p3, p3, g3] + [gt, vo] * sps
    hf, hb, c_out, n_out, m_out = pl.pallas_call(
        functools.partial(_mlstm_kernel, n_chunks=nc),
        grid=(n_seq // sps, nc),
        in_specs=side(fwd) + side(bwd) + state_specs + [tri_spec, tri_spec],
        out_specs=[
            pl.BlockSpec((sps, C_W, lc), lambda b, c: (b, 0, c)),
            pl.BlockSpec((sps, C_W, lc), lambda b, c: (b, 0, nc - 1 - c)),
        ] + state_specs,
        out_shape=[
            jax.ShapeDtypeStruct((n_seq, C_W, seq_len), F32),
            jax.ShapeDtypeStruct((n_seq, C_W, seq_len), F32),
            jax.ShapeDtypeStruct((n_seq, 2, n_pairs, pair_w, pair_w), F32),
            jax.ShapeDtypeStruct((n_seq, 2 * n_pairs, pair_w), F32),
            jax.ShapeDtypeStruct((n_seq, n_st, 1), F32),
        ],
        scratch_shapes=[
            pltpu.VMEM((sps, 2, n_pairs, pair_w, pair_w), F32),
            pltpu.VMEM((sps, 2 * n_pairs, pair_w), F32),
            pltpu.VMEM((sps, n_st, 1), F32),
        ],
        compiler_params=_params(("arbitrary", "arbitrary")),
        name="mlstm",
    )(*operands, *operands, _pair_states(c0), n0.reshape(n_seq, 2 * n_pairs, pair_w), m0.reshape(n_seq, n_st, 1),
      tril, triu)
    return (hf, hb, _unpair_states(c_out), n_out.reshape(n_seq, 2, H_MLSTM, HEAD_DIM),
            m_out.reshape(n_seq, 2, H_MLSTM))


def _head_norm(y, g, ones_blk):
    ysq = y * y
    hi = ysq.astype(BF16)
    lo = (ysq - hi.astype(F32)).astype(BF16)
    ss = _dot(hi, ones_blk) + _dot(lo, ones_blk)
    return y * lax.rsqrt(ss * (1.0 / HEAD_DIM) + EPS) * g


def _merge_kernel(att_ref, four_ref, hf_ref, hb_ref, oc_ref, x_ref, mod_ref, gh_ref, ghm_ref, wo_ref, g2_ref, wrt_ref,
                  ones_ref, xo_ref, h2_ref, afft_ref):
    gh = gh_ref[0]
    ya = _head_norm(att_ref[...], gh[:, :A_W], ones_ref[...])
    four = jnp.concatenate([four_ref[c] for c in range(B_W // LANE)], axis=1)
    yf = _head_norm(four, gh[:, A_W:A_W + B_W], ones_ref[:B_W, :B_W])
    mem = hf_ref[0] + hb_ref[0]
    heads = []
    for hd in range(H_MLSTM):
        y = mem[HEAD_DIM * hd:HEAD_DIM * (hd + 1), :]
        heads.append(y * lax.rsqrt(jnp.mean(y * y, axis=0, keepdims=True) + EPS))
    ym_t = jnp.concatenate(heads, axis=0) * ghm_ref[0] * jax.nn.sigmoid(oc_ref[...])
    out = (_dot(ya.astype(BF16), wo_ref[0, :A_W, :])
           + _dot(yf.astype(BF16), wo_ref[0, A_W:A_W + B_W, :])
           + _dot_tn(ym_t.astype(BF16), wo_ref[0, A_W + B_W:, :]))
    x = x_ref[...] + mod_ref[0, 2:3, :] * out
    xo_ref[...] = x
    y2 = x * lax.rsqrt(jnp.mean(x * x, axis=-1, keepdims=True) + EPS) * g2_ref[0]
    h2 = (y2 * (1.0 + mod_ref[0, 4:5, :]) + mod_ref[0, 3:4, :]).astype(BF16)
    h2_wide = h2.astype(F32)
    for s in range(N_SLAB):
        h2_ref[pl.ds(s, h2_wide.shape[0], stride=N_SLAB), :] = h2_wide[:, LANE * s:LANE * (s + 1)]
    logits = _dot_nt(wrt_ref[0], h2)
    e = jnp.exp(logits - jnp.max(logits, axis=0, keepdims=True))
    afft_ref[...] = e / jnp.sum(e, axis=0, keepdims=True)


def _merge(att, four, hf, hb, vo, x2d, mod, layer, g_head, g_mem, w_out_bf, g2, w_rt_bf, ones_blk, n_seq, seq_len,
           mod_seq_len):
    t = x2d.shape[0]
    tm = MERGE_TILE
    tiles_per_mod = mod_seq_len // tm
    tiles_per_seq = seq_len // tm
    row = lambda i: (i, 0)
    lay = lambda i: (layer, 0, 0)
    mem = lambda i: (i // tiles_per_seq, 0, i % tiles_per_seq)
    return pl.pallas_call(
        _merge_kernel,
        grid=(t // tm,),
        in_specs=[
            pl.BlockSpec((tm, A_W), row),
            pl.BlockSpec((B_W // LANE, tm, LANE), lambda i: (0, i, 0)),
            pl.BlockSpec((1, C_W, tm), mem),
            pl.BlockSpec((1, C_W, tm), mem),
            pl.BlockSpec((C_W, tm), lambda i: (1, i)),
            pl.BlockSpec((tm, D_MODEL), row),
            pl.BlockSpec((1, 6, D_MODEL), lambda i: (i // tiles_per_mod, 0, 0)),
            pl.BlockSpec((1, 1, D_MODEL), lay),
            pl.BlockSpec((1, C_W, tm), lay),
            pl.BlockSpec((1, D_MODEL, D_MODEL), lay),
            pl.BlockSpec((1, 1, D_MODEL), lay),
            pl.BlockSpec((1, N_EXPERTS, D_MODEL), lay),
            pl.BlockSpec((A_W, A_W), lambda i: (0, 0)),
        ],
        out_specs=[
            pl.BlockSpec((tm, D_MODEL), row),
            pl.BlockSpec((N_SLAB * tm, LANE), row),
            pl.BlockSpec((N_EXPERTS, tm), lambda i: (0, i)),
        ],
        out_shape=[
            jax.ShapeDtypeStruct((t, D_MODEL), F32),
            jax.ShapeDtypeStruct((N_SLAB * t, LANE), F32),
            jax.ShapeDtypeStruct((N_EXPERTS, t), F32),
        ],
        compiler_params=_params(("arbitrary",)),
        name="merge",
    )(att, four, hf, hb, vo, x2d, mod, g_head, g_mem, w_out_bf, g2, w_rt_bf, ones_blk)


BISECT_STEPS = 48
TOKEN_CHUNK = 1024


TOKEN_SPLIT = 64


def _route_kernel(aff_ref, triu_ref, idx_ref, gs_ref, sp_ref, *, ns, seq_len, cap):
    seqs = range(ns)
    aff = [aff_ref[:, seq_len * j:seq_len * (j + 1)] for j in seqs]

    def body(_, bounds):
        out = []
        for j in seqs:
            lo, hi = bounds[j]
            mid = 0.5 * (lo + hi)
            ge = jnp.sum(jnp.where(aff[j] >= mid, 1.0, 0.0), axis=1, keepdims=True) >= cap
            out.append((jnp.where(ge, mid, lo), jnp.where(ge, hi, mid)))
        return tuple(out)

    start = (jnp.zeros((N_EXPERTS, 1), F32), jnp.full((N_EXPERTS, 1), 2.0, F32))
    bounds = lax.fori_loop(0, BISECT_STEPS, body, tuple(start for _ in seqs))
    thr = [jnp.max(jnp.where(aff[j] < bounds[j][1], aff[j], -1.0), axis=1, keepdims=True) for j in seqs]
    need = [cap - jnp.sum(jnp.where(aff[j] > thr[j], 1.0, 0.0), axis=1, keepdims=True) for j in seqs]
    triu = triu_ref[...]
    eq_carry = [jnp.zeros((N_EXPERTS, 1), F32) for _ in seqs]
    pos_carry = [jnp.zeros((N_EXPERTS, 1), F32) for _ in seqs]
    for b in range(seq_len // 128):
        for j in seqs:
            blk = aff[j][:, 128 * b:128 * (b + 1)]
            eq = blk == thr[j]
            eq_f = jnp.where(eq, 1.0, 0.0)
            eq_inc = _dot(eq_f.astype(BF16), triu) + eq_carry[j]
            sel = (blk > thr[j]) | (eq & (eq_inc - eq_f < need[j]))
            sel_f = jnp.where(sel, 1.0, 0.0)
            pos_inc = _dot(sel_f.astype(BF16), triu) + pos_carry[j]
            t0 = seq_len * j + 128 * b
            sp_ref[:, t0:t0 + 128] = jnp.where(sel, pos_inc - sel_f, -1.0).astype(jnp.int32)
            eq_carry[j] = eq_inc[:, 127:128]
            pos_carry[j] = pos_inc[:, 127:128]

    tc = min(seq_len, TOKEN_CHUNK)
    slot = lax.broadcasted_iota(jnp.int32, (cap, tc), 0)
    part = lax.broadcasted_iota(jnp.int32, (8, tc), 0)
    tok = lax.broadcasted_iota(jnp.int32, (1, tc), 1).astype(F32)
    chunks = range(0, seq_len, tc)
    tok_hi = [jnp.floor((tok + float(t0)) * (1.0 / TOKEN_SPLIT)) for t0 in chunks]
    tok_lo = [tok + float(t0) - TOKEN_SPLIT * hi for t0, hi in zip(chunks, tok_hi)]

    def per_expert(e, carry):
        for j in seqs:
            acc = jnp.zeros((cap, 8), F32)
            for ci, t0 in enumerate(chunks):
                cols = slice(seq_len * j + t0, seq_len * j + t0 + tc)
                onehot = jnp.where(slot == sp_ref[pl.ds(e, 1), cols], 1.0, 0.0).astype(BF16)
                a = aff_ref[pl.ds(e, 1), cols]
                a_hi = a.astype(BF16).astype(F32)
                a_mid = (a - a_hi).astype(BF16).astype(F32)
                a_lo = a - a_hi - a_mid
                vals = jnp.where(part == 0, tok_hi[ci], jnp.where(part == 1, tok_lo[ci], jnp.where(
                    part == 2, a_hi, jnp.where(part == 3, a_mid, jnp.where(part == 4, a_lo, 0.0)))))
                acc = acc + _dot_nt(onehot, vals.astype(BF16))
            idx_ref[N_EXPERTS * j + e] = (N_SLAB * (TOKEN_SPLIT * acc[:, 0:1] + acc[:, 1:2])).astype(jnp.int32)
            gs_ref[e, cap * j:cap * (j + 1)] = acc[:, 2:3] + acc[:, 3:4] + acc[:, 4:5]
        return carry

    lax.fori_loop(0, N_EXPERTS, per_expert, 0)


def _route(afft, triu_bf, n_seq, seq_len, cap, ns):
    idx, gs = pl.pallas_call(
        functools.partial(_route_kernel, ns=ns, seq_len=seq_len, cap=cap),
        grid=(n_seq // ns,),
        in_specs=[
            pl.BlockSpec((N_EXPERTS, ns * seq_len), lambda s: (0, s)),
            pl.BlockSpec((128, 128), lambda s: (0, 0)),
        ],
        out_specs=[
            pl.BlockSpec((ns * N_EXPERTS, cap, 1), lambda s: (s, 0, 0)),
            pl.BlockSpec((N_EXPERTS, ns * cap, 1), lambda s: (0, s, 0)),
        ],
        out_shape=[
            jax.ShapeDtypeStruct((n_seq * N_EXPERTS, cap, 1), jnp.int32),
            jax.ShapeDtypeStruct((N_EXPERTS, n_seq * cap, 1), F32),
        ],
        scratch_shapes=[pltpu.VMEM((N_EXPERTS, ns * seq_len), jnp.int32)],
        compiler_params=_params(("arbitrary",)),
        name="route",
    )(afft, triu_bf)
    return idx.reshape(n_seq * N_EXPERTS * cap), gs


ROW_COPIES = 8


def _row_tile(first_row):
    return pl.ds(pl.multiple_of(first_row, N_SLAB), N_SLAB)


def _gather_kernel(idx_ref, src_ref, xs_ref, tile_ref, *, eb, cap):
    ei = pl.program_id(1)

    def per_expert(ee, carry):
        e = ei * eb + ee

        def rows(g, c):
            slot0 = g * ROW_COPIES
            for u in range(ROW_COPIES):
                tile_ref[_row_tile((slot0 + u) * N_SLAB), :] = src_ref[_row_tile(idx_ref[e * cap + slot0 + u]), :]
            return c

        lax.fori_loop(0, cap // ROW_COPIES, rows, 0)
        for s in range(N_SLAB):
            xs_ref[ee, :, LANE * s:LANE * (s + 1)] = tile_ref[pl.ds(s, cap, stride=N_SLAB), :].astype(BF16)
        return carry

    lax.fori_loop(0, eb, per_expert, 0)


def _gather(idx, h2_rows, n_seq, cap, eb):
    return pl.pallas_call(
        functools.partial(_gather_kernel, eb=eb, cap=cap),
        grid=(n_seq, N_EXPERTS // eb),
        in_specs=[
            pl.BlockSpec((N_EXPERTS * cap,), lambda s, e: (s,), memory_space=pltpu.SMEM),
            pl.BlockSpec((h2_rows.shape[0] // n_seq, LANE), lambda s, e: (s, 0)),
        ],
        out_specs=pl.BlockSpec((eb, cap, D_MODEL), lambda s, e: (e, s, 0)),
        out_shape=jax.ShapeDtypeStruct((N_EXPERTS, n_seq * cap, D_MODEL), BF16),
        scratch_shapes=[pltpu.VMEM((N_SLAB * cap, LANE), F32)],
        compiler_params=_params(("arbitrary", "arbitrary"), vmem_mib=56),
        name="gather",
    )(idx, h2_rows)


def _expert_kernel(xc_ref, xl_ref, gc_ref, gl_ref, wg_ref, wu_ref, wd_ref, yc_ref, yl_ref, *, n_f):
    f = pl.program_id(1)

    @pl.when(f == 0)
    def _():
        yc_ref[...] = jnp.zeros_like(yc_ref)
        yl_ref[...] = jnp.zeros_like(yl_ref)

    wg = wg_ref[0, 0].astype(BF16)
    wu = wu_ref[0, 0].astype(BF16)
    wd = wd_ref[0, 0].astype(BF16)
    for x_ref, y_ref in ((xc_ref, yc_ref), (xl_ref, yl_ref)):
        x = x_ref[0]
        mid = (_silu(_dot(x, wg)) * _dot(x, wu)).astype(BF16)
        y_ref[0] += _dot(mid, wd)

    @pl.when(f == n_f - 1)
    def _():
        yc_ref[0] = yc_ref[0] * gc_ref[0]
        yl_ref[0] = yl_ref[0] * gl_ref[0]


def _experts(xs_c, xs_l, gs_c, gs_l, w_g, w_u, w_d, layer):
    rc, rl = xs_c.shape[1], xs_l.shape[1]
    tf = 1024
    n_f = EXPERT_FF // tf
    return pl.pallas_call(
        functools.partial(_expert_kernel, n_f=n_f),
        grid=(N_EXPERTS, n_f),
        in_specs=[
            pl.BlockSpec((1, rc, D_MODEL), lambda e, f: (e, 0, 0)),
            pl.BlockSpec((1, rl, D_MODEL), lambda e, f: (e, 0, 0)),
            pl.BlockSpec((1, rc, 1), lambda e, f: (e, 0, 0)),
            pl.BlockSpec((1, rl, 1), lambda e, f: (e, 0, 0)),
            pl.BlockSpec((1, 1, D_MODEL, tf), lambda e, f: (layer, e, 0, f)),
            pl.BlockSpec((1, 1, D_MODEL, tf), lambda e, f: (layer, e, 0, f)),
            pl.BlockSpec((1, 1, tf, D_MODEL), lambda e, f: (layer, e, f, 0)),
        ],
        out_specs=[
            pl.BlockSpec((1, rc, D_MODEL), lambda e, f: (e, 0, 0)),
            pl.BlockSpec((1, rl, D_MODEL), lambda e, f: (e, 0, 0)),
        ],
        out_shape=[
            jax.ShapeDtypeStruct((N_EXPERTS, rc, D_MODEL), F32),
            jax.ShapeDtypeStruct((N_EXPERTS, rl, D_MODEL), F32),
        ],
        compiler_params=_params(("arbitrary", "arbitrary"), vmem_mib=56),
        name="experts",
    )(xs_c, xs_l, gs_c, gs_l, w_g, w_u, w_d)


def _scatter_kernel(idx_ref, y_ref, x_ref, mod_ref, gf_ref, o_ref, acc_ref, tile_ref, *, eb, n_e, cap, tm, final):
    step = pl.program_id(1)

    @pl.when(step == 0)
    def _():
        acc_ref[...] = jnp.zeros_like(acc_ref)

    @pl.when(step < n_e)
    def _():
        def per_expert(ee, carry):
            e = step * eb + ee
            for s in range(N_SLAB):
                tile_ref[pl.ds(s, cap, stride=N_SLAB), :] = y_ref[ee, :, LANE * s:LANE * (s + 1)]

            def rows(g, c):
                slot0 = g * ROW_COPIES
                dst = [idx_ref[e * cap + slot0 + u] for u in range(ROW_COPIES)]
                new = [acc_ref[_row_tile(dst[u]), :] + tile_ref[_row_tile((slot0 + u) * N_SLAB), :]
                       for u in range(ROW_COPIES)]
                for u in range(ROW_COPIES):
                    acc_ref[_row_tile(dst[u]), :] = new[u]
                return c

            lax.fori_loop(0, cap // ROW_COPIES, rows, 0)
            return carry

        lax.fori_loop(0, eb, per_expert, 0)

    @pl.when(step >= n_e)
    def _():
        base = pl.multiple_of((step - n_e) * tm * N_SLAB, N_SLAB)
        moe = jnp.concatenate([acc_ref[pl.ds(base + s, tm, stride=N_SLAB), :] for s in range(N_SLAB)], axis=1)
        x = x_ref[...] + mod_ref[0, 5:6, :] * moe
        if final:
            x = x * lax.rsqrt(jnp.mean(x * x, axis=-1, keepdims=True) + EPS) * gf_ref[...]
        o_ref[...] = x


def _scatter(idx, ys, x2d, mod, g_final, n_seq, seq_len, cap, eb, tm, final):
    n_e = N_EXPERTS // eb
    n_out = seq_len // tm
    out_blk = lambda s, j: (s * n_out + jnp.maximum(j - n_e, 0), 0)
    return pl.pallas_call(
        functools.partial(_scatter_kernel, eb=eb, n_e=n_e, cap=cap, tm=tm, final=final),
        grid=(n_seq, n_e + n_out),
        in_specs=[
            pl.BlockSpec((N_EXPERTS * cap,), lambda s, j: (s,), memory_space=pltpu.SMEM),
            pl.BlockSpec((eb, cap, D_MODEL), lambda s, j: (jnp.minimum(j, n_e - 1), s, 0)),
            pl.BlockSpec((tm, D_MODEL), out_blk),
            pl.BlockSpec((1, 6, D_MODEL), lambda s, j: (s, 0, 0)),
            pl.BlockSpec((1, D_MODEL), lambda s, j: (0, 0)),
        ],
        out_specs=pl.BlockSpec((tm, D_MODEL), out_blk),
        out_shape=jax.ShapeDtypeStruct((n_seq * seq_len, D_MODEL), F32),
        scratch_shapes=[
            pltpu.VMEM((N_SLAB * seq_len, LANE), F32),
            pltpu.VMEM((N_SLAB * cap, LANE), F32),
        ],
        compiler_params=_params(("arbitrary", "arbitrary"), vmem_mib=56),
        name="scatter",
    )(idx, ys, x2d, mod, g_final)


def kernel(x_prompt, x_sample, c, cache_k, cache_v, state_C, state_n, state_m, c_ctx, w_ada, b_ada, g_norm1, g_norm2, w_in, b_gates, rpb, w_fourier, g_head, w_out, w_router, w_exp_gate, w_exp_up, w_exp_down, g_final):
    n_ctx, len_ctx, _ = x_prompt.shape
    n_lat, len_lat, _ = x_sample.shape
    past = cache_k.shape[2]
    cap_ctx = CAPACITY_FACTOR * len_ctx // N_EXPERTS
    cap_lat = CAPACITY_FACTOR * len_lat // N_EXPERTS

    w_in_bf = w_in.astype(BF16)
    w_gt_bf = jnp.swapaxes(w_in[:, :, P_COLS:], 1, 2).astype(BF16)
    vc0, oc0 = VC_BLK * C_W, OC_BLK * C_W
    w_vo_bf = jnp.swapaxes(jnp.concatenate([w_in[:, :, vc0:vc0 + C_W], w_in[:, :, oc0:oc0 + C_W]], axis=2),
                           1, 2).astype(BF16)
    bg_row = b_gates.reshape(DEPTH, 1, N_GATE_COLS).astype(F32)
    bg_col = b_gates.reshape(DEPTH, N_GATE_COLS, 1).astype(F32)
    w_out_bf = w_out.astype(BF16)
    w_rt_bf = jnp.swapaxes(w_router, 1, 2).astype(BF16)
    g1 = g_norm1.reshape(DEPTH, 1, D_MODEL)
    g2 = g_norm2.reshape(DEPTH, 1, D_MODEL)
    gh = g_head.reshape(DEPTH, 1, D_MODEL)
    g_mem = jnp.broadcast_to(gh[:, 0, A_W + B_W:, None], (DEPTH, C_W, MERGE_TILE))
    eye_g = jnp.eye(G_FOURIER, dtype=F32)
    wf_blk = jnp.einsum("lgcd,gh->lgchd", w_fourier, eye_g).reshape(DEPTH, B_W, B_W).astype(BF16)

    csc = _channel_dft()
    dft_ctx = _dft_mats(len_ctx)
    dft_side = _dft_mats(GRID_W)
    tw_cos, tw_sin = _twiddles(GRID_W, len_lat)
    bias_tiles = _nbr_bias_tiles(rpb)
    r = np.arange(MLSTM_CHUNK)
    tril = jnp.asarray(r[:, None] >= r[None, :], F32)
    triu = jnp.asarray(r[:, None] <= r[None, :], F32)
    triu_bf = triu.astype(BF16)
    hidx = np.arange(A_W) // HEAD_DIM
    ones_blk = jnp.asarray(hidx[:, None] == hidx[None, :], BF16)

    cvecs = jnp.concatenate([c_ctx[None, :], c, jnp.zeros((8 - 1 - n_lat, D_MODEL), F32)], axis=0)
    mod_all = _modulation(cvecs, w_ada, b_ada).reshape(DEPTH, 8, 6, D_MODEL)

    cache_k4 = cache_k.reshape(n_lat, DEPTH, past, A_W)
    cache_v4 = cache_v.reshape(n_lat, DEPTH, past, A_W)
    zero_c = jnp.zeros((n_ctx, 2, H_MLSTM, HEAD_DIM, HEAD_DIM), F32)
    zero_n = jnp.zeros((n_ctx, 2, H_MLSTM, HEAD_DIM), F32)
    zero_m = jnp.zeros((n_ctx, 2, H_MLSTM), F32)

    xc = x_prompt.reshape(n_ctx * len_ctx, D_MODEL)
    xl = x_sample.reshape(n_lat * len_lat, D_MODEL)
    gf = g_final.reshape(1, D_MODEL)
    ks, vs, cs, ns, ms = [], [], [], [], []
    for l in range(DEPTH):
        mod_c = mod_all[l, 0:1]
        mod_l = mod_all[l, 1:1 + n_lat]

        pc, gc, gtc, voc, abc = _inproj(xc, mod_c, l, g1, w_in_bf, w_gt_bf, w_vo_bf, bg_row, bg_col, csc,
                                        n_ctx * len_ctx, False)
        att_c = _ctx_attention(pc, n_ctx, len_ctx)
        four_c = _fourier(abc, dft_ctx[0], dft_ctx[1], wf_blk, l, n_ctx, len_ctx)
        hf_c, hb_c, c_new, n_new, m_new = _mlstm(pc, gc, gtc, voc, zero_c, zero_n, zero_m, tril, triu, n_ctx, len_ctx)
        xc, h2c, affc = _merge(att_c, four_c, hf_c, hb_c, voc, xc, mod_c, l, gh, g_mem, w_out_bf, g2, w_rt_bf,
                               ones_blk, n_ctx, len_ctx, n_ctx * len_ctx)
        ks.append(pc[:, A_W:2 * A_W].reshape(n_ctx, len_ctx, H_ATT, HEAD_DIM))
        vs.append(pc[:, 2 * A_W:3 * A_W].reshape(n_ctx, len_ctx, H_ATT, HEAD_DIM))
        cs.append(c_new)
        ns.append(n_new)
        ms.append(m_new)

        pq, gq, gtq, voq, abq = _inproj(xl, mod_l, l, g1, w_in_bf, w_gt_bf, w_vo_bf, bg_row, bg_col, csc, len_lat,
                                        True)
        att_l = _nbr_attention(pq, cache_k4, cache_v4, bias_tiles, l, n_lat, len_lat)
        four_l = _fourier_grid(abq, dft_side[0], dft_side[1], tw_cos, tw_sin, wf_blk, l, n_lat, GRID_W)
        hf_l, hb_l, _, _, _ = _mlstm(pq, gq, gtq, voq, state_C[:, l], state_n[:, l], state_m[:, l], tril, triu,
                                     n_lat, len_lat)
        xl, h2l, affl = _merge(att_l, four_l, hf_l, hb_l, voq, xl, mod_l, l, gh, g_mem, w_out_bf, g2, w_rt_bf,
                               ones_blk, n_lat, len_lat, len_lat)

        last = l == DEPTH - 1
        idx_c, gs_c = _route(affc, triu_bf, n_ctx, len_ctx, cap_ctx, n_ctx)
        idx_l, gs_l = _route(affl, triu_bf, n_lat, len_lat, cap_lat, 1)
        xs_c = _gather(idx_c, h2c, n_ctx, cap_ctx, N_EXPERTS)
        xs_l = _gather(idx_l, h2l, n_lat, cap_lat, 4)
        ys_c, ys_l = _experts(xs_c, xs_l, gs_c, gs_l, w_exp_gate, w_exp_up, w_exp_down, l)
        xc = _scatter(idx_c, ys_c, xc, jnp.broadcast_to(mod_c, (n_ctx, 6, D_MODEL)), gf, n_ctx, len_ctx, cap_ctx,
                      N_EXPERTS, len_ctx, last)
        xl = _scatter(idx_l, ys_l, xl, mod_l, gf, n_lat, len_lat, cap_lat, 2, 512, last)

    y_prompt = xc.reshape(n_ctx, len_ctx, D_MODEL)
    y_sample = xl.reshape(n_lat, len_lat, D_MODEL)
    return (y_prompt, y_sample, jnp.stack(ks, axis=1), jnp.stack(vs, axis=1), jnp.stack(cs, axis=1),
            jnp.stack(ns, axis=1), jnp.stack(ms, axis=1))
```

```python
import functools

import numpy as np
import jax
import jax.numpy as jnp
from jax import lax
from jax.experimental import pallas as pl
from jax.experimental.pallas import tpu as pltpu

F32 = jnp.float32
BF16 = jnp.bfloat16

D_MODEL = 1024
DEPTH = 2
HEAD_DIM = 64
H_ATT = 8
G_FOURIER = 4
H_MLSTM = 4
A_W = H_ATT * HEAD_DIM
B_W = G_FOURIER * HEAD_DIM
C_W = H_MLSTM * HEAD_DIM
N_GATE_COLS = 16
P_COLS = 3 * A_W + B_W + 4 * C_W
IN_COLS = P_COLS + N_GATE_COLS
GRID_W = 64
WIN_R = 8
WIN_C = 16
MLSTM_CHUNK = 128
N_EXPERTS = 16
CAPACITY_FACTOR = 2
EXPERT_FF = 2 * D_MODEL
EPS = 1e-6
NEG = -1e30

UB_OFF = 3 * A_W
QC_BLK, KC_BLK, VC_BLK, OC_BLK = 7, 8, 9, 10

LANE = 128
N_SLAB = D_MODEL // LANE
MERGE_TILE = 256
INPROJ_TILE = 512

NT_DIMS = (((1,), (1,)), ((), ()))
TN_DIMS = (((0,), (0,)), ((), ()))
MIB = 1024 * 1024


def _dot(a, b, precision=None):
    return jnp.dot(a, b, preferred_element_type=F32, precision=precision)


def _dot_nt(a, b, precision=None):
    return lax.dot_general(a, b, NT_DIMS, preferred_element_type=F32, precision=precision)


def _dot_tn(a, b):
    return lax.dot_general(a, b, TN_DIMS, preferred_element_type=F32)


def _params(sem, vmem_mib=48):
    return pltpu.CompilerParams(dimension_semantics=sem, vmem_limit_bytes=vmem_mib * MIB)


def _silu(x):
    return x * jax.nn.sigmoid(x)


def _log_sigmoid(x):
    return jnp.minimum(x, 0.0) - jnp.log1p(jnp.exp(-jnp.abs(x)))


def _mod_kernel(c_ref, w_ref, b_ref, o_ref):
    s = _silu(c_ref[...]).astype(BF16)
    o_ref[0] = _dot(s, w_ref[0].astype(BF16)) + b_ref[0]


def _modulation(cvecs, w_ada, b_ada):
    depth = w_ada.shape[0]
    tn = 1024
    return pl.pallas_call(
        _mod_kernel,
        grid=(depth, 6 * D_MODEL // tn),
        in_specs=[
            pl.BlockSpec((8, D_MODEL), lambda l, j: (0, 0)),
            pl.BlockSpec((1, D_MODEL, tn), lambda l, j: (l, 0, j)),
            pl.BlockSpec((1, 1, tn), lambda l, j: (l, 0, j)),
        ],
        out_specs=pl.BlockSpec((1, 8, tn), lambda l, j: (l, 0, j)),
        out_shape=jax.ShapeDtypeStruct((depth, 8, 6 * D_MODEL), F32),
        compiler_params=_params(("arbitrary", "arbitrary")),
        name="modulation",
    )(cvecs, w_ada, b_ada.reshape(depth, 1, 6 * D_MODEL))


def _inproj_kernel(x_ref, mod_ref, g1_ref, w_ref, wgt_ref, wvo_ref, bgr_ref, bgc_ref, csc_ref,
                   p_ref, g_ref, gt_ref, vo_ref, ab_ref, *scratch, grid_rows):
    x = x_ref[...]
    y = x * lax.rsqrt(jnp.mean(x * x, axis=-1, keepdims=True) + EPS) * g1_ref[0]
    h = (y * (1.0 + mod_ref[0, 1:2, :]) + mod_ref[0, 0:1, :]).astype(BF16)
    for j in range(0, P_COLS, 256):
        pj = _dot(h, w_ref[0, :, j:j + 256])
        p_ref[:, j:j + 256] = pj
        if j == UB_OFF:
            ab = _dot(pj.astype(BF16), csc_ref[...])
            if grid_rows:
                stage_ref, = scratch
                n_lt = 2 * B_W // LANE
                for c in range(n_lt):
                    stage_ref[c] = ab[:, LANE * c:LANE * (c + 1)]
                for n2 in range(GRID_W):
                    for c in range(n_lt):
                        col = 2 * B_W * n2 + LANE * c
                        ab_ref[:, col:col + LANE] = stage_ref[c, pl.ds(n2, grid_rows, stride=GRID_W), :]
            else:
                ab_ref[...] = ab.astype(BF16)
    g_ref[...] = _dot(h, w_ref[0, :, P_COLS:IN_COLS]) + bgr_ref[0]
    gt_ref[...] = _dot_nt(wgt_ref[0], h) + bgc_ref[0]
    vo_ref[...] = _dot_nt(wvo_ref[0], h)


def _inproj(x2d, mod, layer, g1, w_in_bf, w_gt_bf, w_vo_bf, bg_row, bg_col, csc, seq_len, grid_ab):
    t = x2d.shape[0]
    tm = INPROJ_TILE
    tiles_per_seq = seq_len // tm
    grid_rows = tm // GRID_W if grid_ab else 0
    if grid_ab:
        ab_spec = pl.BlockSpec((grid_rows, GRID_W * 2 * B_W), lambda i: (i, 0))
        ab_shape = jax.ShapeDtypeStruct((t // GRID_W, GRID_W * 2 * B_W), F32)
        scratch = [pltpu.VMEM((2 * B_W // LANE, tm, LANE), F32)]
    else:
        ab_spec = pl.BlockSpec((tm, 2 * B_W), lambda i: (i, 0))
        ab_shape = jax.ShapeDtypeStruct((t, 2 * B_W), BF16)
        scratch = []
    return pl.pallas_call(
        functools.partial(_inproj_kernel, grid_rows=grid_rows),
        grid=(t // tm,),
        in_specs=[
            pl.BlockSpec((tm, D_MODEL), lambda i: (i, 0)),
            pl.BlockSpec((1, 6, D_MODEL), lambda i: (i // tiles_per_seq, 0, 0)),
            pl.BlockSpec((1, 1, D_MODEL), lambda i: (layer, 0, 0)),
            pl.BlockSpec((1, D_MODEL, IN_COLS), lambda i: (layer, 0, 0)),
            pl.BlockSpec((1, N_GATE_COLS, D_MODEL), lambda i: (layer, 0, 0)),
            pl.BlockSpec((1, 2 * C_W, D_MODEL), lambda i: (layer, 0, 0)),
            pl.BlockSpec((1, 1, N_GATE_COLS), lambda i: (layer, 0, 0)),
            pl.BlockSpec((1, N_GATE_COLS, 1), lambda i: (layer, 0, 0)),
            pl.BlockSpec((B_W, 2 * B_W), lambda i: (0, 0)),
        ],
        out_specs=[
            pl.BlockSpec((tm, P_COLS), lambda i: (i, 0)),
            pl.BlockSpec((tm, N_GATE_COLS), lambda i: (i, 0)),
            pl.BlockSpec((N_GATE_COLS, tm), lambda i: (0, i)),
            pl.BlockSpec((2 * C_W, tm), lambda i: (0, i)),
            ab_spec,
        ],
        out_shape=[
            jax.ShapeDtypeStruct((t, P_COLS), F32),
            jax.ShapeDtypeStruct((t, N_GATE_COLS), F32),
            jax.ShapeDtypeStruct((N_GATE_COLS, t), F32),
            jax.ShapeDtypeStruct((2 * C_W, t), F32),
            ab_shape,
        ],
        scratch_shapes=scratch,
        compiler_params=_params(("arbitrary",)),
        name="inproj",
    )(x2d, mod, g1, w_in_bf, w_gt_bf, w_vo_bf, bg_row, bg_col, csc)


def _ctx_attn_kernel(q_ref, k_ref, v_ref, o_ref):
    scale = HEAD_DIM ** -0.5
    heads = range(H_ATT)
    sl = [slice(HEAD_DIM * h, HEAD_DIM * (h + 1)) for h in heads]
    s = [_dot_nt((q_ref[:, sl[h]] * scale).astype(BF16), k_ref[:, sl[h]].astype(BF16)) for h in heads]
    e = [jnp.exp(s[h] - jnp.max(s[h], axis=-1, keepdims=True)) for h in heads]
    w = [e[h] * (1.0 / jnp.sum(e[h], axis=-1, keepdims=True)) for h in heads]
    o_ref[...] = jnp.concatenate([_dot(w[h].astype(BF16), v_ref[:, sl[h]].astype(BF16)) for h in heads], axis=1)


def _ctx_attention(p, n_seq, seq_len):
    return pl.pallas_call(
        _ctx_attn_kernel,
        grid=(n_seq,),
        in_specs=[
            pl.BlockSpec((seq_len, A_W), lambda b: (b, 0)),
            pl.BlockSpec((seq_len, A_W), lambda b: (b, 1)),
            pl.BlockSpec((seq_len, A_W), lambda b: (b, 2)),
        ],
        out_specs=pl.BlockSpec((seq_len, A_W), lambda b: (b, 0)),
        out_shape=jax.ShapeDtypeStruct((n_seq * seq_len, A_W), F32),
        compiler_params=_params(("arbitrary",)),
        name="ctx_attention",
    )(p, p, p)


Q_ROWS = 8
K_ROWS = 16
KEY_BLK = 256


NO_ROW = 2 * WIN_R - 1


def _nbr_bias_tiles(rpb):
    n_c = 2 * WIN_C - 1
    cq = np.arange(GRID_W)[:, None]
    ck = np.arange(GRID_W)[None, :]
    cs = np.clip(cq - WIN_C // 2, 0, GRID_W - WIN_C)
    col_ok = (ck >= cs) & (ck < cs + WIN_C)
    pick = np.where(col_ok, np.clip(ck - cq + WIN_C - 1, 0, n_c - 1), n_c)
    sel = np.zeros((2, GRID_W, 2 * GRID_W, n_c + 1), np.float32)
    for side in range(2):
        sel[side, cq, side * GRID_W + ck, pick] = 1.0
    ext = jnp.full(rpb.shape[:2] + (2 * WIN_R, n_c + 1), NEG, F32)
    ext = ext.at[:, :, :2 * WIN_R - 1, :n_c].set(rpb.astype(F32))
    return jnp.einsum("lhdm,sqkm->lhdsqk", ext, jnp.asarray(sel), precision=lax.Precision.HIGHEST)


def _nbr_attn_kernel(q_ref, k0_ref, k1_ref, k2_ref, k3_ref, v0_ref, v1_ref, v2_ref, v3_ref,
                     ck_ref, cv_ref, tab_ref, o_ref):
    scale = HEAD_DIM ** -0.5
    k_refs = (k0_ref, k1_ref, k2_ref, k3_ref)
    v_refs = (v0_ref, v1_ref, v2_ref, v3_ref)
    rb = pl.program_id(1)
    rows = GRID_W
    key_row0 = jnp.clip(Q_ROWS * rb - WIN_R // 2, 0, rows - K_ROWS)
    d = []
    for rq in range(Q_ROWS):
        r = Q_ROWS * rb + rq
        rs = jnp.clip(r - WIN_R // 2, 0, rows - WIN_R)
        d_row = []
        for rk in range(K_ROWS):
            rka = key_row0 + rk
            ok = (rka >= rs) & (rka < rs + WIN_R)
            d_row.append(jnp.where(ok, rka - r + WIN_R - 1, NO_ROW))
        d.append(d_row)
    rows_per_blk = KEY_BLK // GRID_W
    heads = range(2)
    sl = [slice(HEAD_DIM * hh, HEAD_DIM * (hh + 1)) for hh in heads]
    q = [(q_ref[:, sl[hh]] * scale).astype(BF16) for hh in heads]
    s_ctx = [_dot_nt(q[hh], ck_ref[0, 0, :, sl[hh]].astype(BF16)) for hh in heads]
    s_loc = [[] for _ in heads]
    for j in range(4):
        for hh in heads:
            bias = jnp.concatenate([
                jnp.concatenate([
                    tab_ref[0, hh, d[rq][rows_per_blk * j + 2 * p], 0]
                    + tab_ref[0, hh, d[rq][rows_per_blk * j + 2 * p + 1], 1]
                    for p in range(rows_per_blk // 2)], axis=1)
                for rq in range(Q_ROWS)], axis=0)
            s_loc[hh].append(_dot_nt(q[hh], k_refs[j][:, sl[hh]].astype(BF16)) + bias)
    m = [jnp.max(s_ctx[hh], axis=-1, keepdims=True) for hh in heads]
    for j in range(4):
        m = [jnp.maximum(m[hh], jnp.max(s_loc[hh][j], axis=-1, keepdims=True)) for hh in heads]
    e_ctx = [jnp.exp(s_ctx[hh] - m[hh]) for hh in heads]
    den = [jnp.sum(e_ctx[hh], axis=-1, keepdims=True) for hh in heads]
    num = [_dot(e_ctx[hh].astype(BF16), cv_ref[0, 0, :, sl[hh]].astype(BF16)) for hh in heads]
    for j in range(4):
        e = [jnp.exp(s_loc[hh][j] - m[hh]) for hh in heads]
        den = [den[hh] + jnp.sum(e[hh], axis=-1, keepdims=True) for hh in heads]
        num = [num[hh] + _dot(e[hh].astype(BF16), v_refs[j][:, sl[hh]].astype(BF16)) for hh in heads]
    o_ref[...] = jnp.concatenate([num[hh] / den[hh] for hh in heads], axis=1)


def _nbr_attention(p, cache_k4, cache_v4, bias_tiles, layer, n_seq, seq_len):
    q_tok = Q_ROWS * GRID_W
    n_rb = seq_len // q_tok
    kb_per_seq = seq_len // KEY_BLK
    max_base = kb_per_seq - 4

    def kmap(j, col0):
        def f(hp, rb, b):
            base = jnp.clip(2 * rb - 1, 0, max_base)
            return (b * kb_per_seq + base + j, col0 + hp)
        return f

    past = cache_k4.shape[2]
    in_specs = [pl.BlockSpec((q_tok, 128), lambda hp, rb, b: (b * n_rb + rb, hp))]
    in_specs += [pl.BlockSpec((KEY_BLK, 128), kmap(j, A_W // 128)) for j in range(4)]
    in_specs += [pl.BlockSpec((KEY_BLK, 128), kmap(j, 2 * A_W // 128)) for j in range(4)]
    in_specs += [
        pl.BlockSpec((1, 1, past, 128), lambda hp, rb, b: (b, layer, 0, hp)),
        pl.BlockSpec((1, 1, past, 128), lambda hp, rb, b: (b, layer, 0, hp)),
        pl.BlockSpec((1, 2, 2 * WIN_R, 2, GRID_W, 2 * GRID_W), lambda hp, rb, b: (layer, hp, 0, 0, 0, 0)),
    ]
    return pl.pallas_call(
        _nbr_attn_kernel,
        grid=(H_ATT // 2, n_rb, n_seq),
        in_specs=in_specs,
        out_specs=pl.BlockSpec((q_tok, 128), lambda hp, rb, b: (b * n_rb + rb, hp)),
        out_shape=jax.ShapeDtypeStruct((n_seq * seq_len, A_W), F32),
        compiler_params=_params(("arbitrary", "arbitrary", "arbitrary")),
        name="nbr_attention",
    )(p, p, p, p, p, p, p, p, p, cache_k4, cache_v4, bias_tiles)


def _dft_mats(n):
    idx = jnp.arange(n, dtype=jnp.int32)
    ang = ((idx[:, None] * idx[None, :]) % n).astype(F32) * (2.0 * np.pi / n)
    return jnp.cos(ang).astype(BF16), jnp.sin(ang).astype(BF16)


def _channel_dft():
    c = np.arange(HEAD_DIM)
    ang = 2.0 * np.pi * ((c[:, None] * c[None, :]) % HEAD_DIM) / HEAD_DIM
    eye = np.eye(G_FOURIER)
    mats = np.concatenate([np.kron(eye, np.cos(ang)), np.kron(eye, np.sin(ang))], axis=1)
    return jnp.asarray(mats, F32).astype(BF16)


def _fourier_kernel(c_ref, s_ref, ab_ref, wf_ref, o_ref, acc_ref, *, scale, n_k):
    k = pl.program_id(2)

    @pl.when(k == 0)
    def _():
        acc_ref[...] = jnp.zeros_like(acc_ref)

    acc_ref[...] += _dot(c_ref[...], ab_ref[:, :B_W]) - _dot(s_ref[...], ab_ref[:, B_W:])

    @pl.when(k == n_k - 1)
    def _():
        z = (acc_ref[...] * scale).astype(BF16)
        o = _dot(z, wf_ref[0])
        for c in range(B_W // LANE):
            o_ref[c] = o[:, LANE * c:LANE * (c + 1)]


def _fourier(ab, cmat, smat, wf_blk, layer, n_seq, seq_len):
    ti = min(seq_len, 512)
    tk = min(seq_len, 1024)
    n_i, n_k = seq_len // ti, seq_len // tk
    scale = float((seq_len * HEAD_DIM) ** -0.5)
    return pl.pallas_call(
        functools.partial(_fourier_kernel, scale=scale, n_k=n_k),
        grid=(n_seq, n_i, n_k),
        in_specs=[
            pl.BlockSpec((ti, tk), lambda s, i, k: (i, k)),
            pl.BlockSpec((ti, tk), lambda s, i, k: (i, k)),
            pl.BlockSpec((tk, 2 * B_W), lambda s, i, k: (s * n_k + k, 0)),
            pl.BlockSpec((1, B_W, B_W), lambda s, i, k: (layer, 0, 0)),
        ],
        out_specs=pl.BlockSpec((B_W // LANE, ti, LANE), lambda s, i, k: (0, s * n_i + i, 0)),
        out_shape=jax.ShapeDtypeStruct((B_W // LANE, n_seq * seq_len, LANE), F32),
        scratch_shapes=[pltpu.VMEM((ti, B_W), F32)],
        compiler_params=_params(("arbitrary", "arbitrary", "arbitrary")),
        name="fourier",
    )(cmat, smat, ab, wf_blk)


FS_GROUP = 8


def _twiddles(side, n):
    k1 = jnp.arange(side, dtype=jnp.int32)[:, None]
    n2 = jnp.arange(side, dtype=jnp.int32)[None, :]
    ang = (k1 * n2).astype(F32) * (2.0 * np.pi / n)
    return jnp.cos(ang), jnp.sin(ang)


def _fourier_grid_kernel(ab_ref, c_ref, s_ref, tc_ref, ts_ref, wf_ref, o_ref, y_ref, *, side, scale):
    cmat = c_ref[...]
    smat = s_ref[...]
    for g in range(side // FS_GROUP):
        ab = ab_ref[:, 2 * B_W * FS_GROUP * g:2 * B_W * FS_GROUP * (g + 1)].astype(BF16)
        m1 = _dot(cmat, ab)
        m2 = _dot(smat, ab)
        for t in range(FS_GROUP):
            n2 = FS_GROUP * g + t
            a0 = 2 * B_W * t
            yr = m1[:, a0:a0 + B_W] - m2[:, a0 + B_W:a0 + 2 * B_W]
            yi = -(m1[:, a0 + B_W:a0 + 2 * B_W] + m2[:, a0:a0 + B_W])
            ct = tc_ref[:, n2:n2 + 1]
            st = ts_ref[:, n2:n2 + 1]
            y = jnp.concatenate([yr * ct + yi * st, yi * ct - yr * st], axis=1)
            for c in range(2 * B_W // LANE):
                y_ref[c, pl.ds(n2, side, stride=side), :] = y[:, LANE * c:LANE * (c + 1)]
    for g in range(side // FS_GROUP):
        zs = []
        for j in range(FS_GROUP):
            rows = slice(side * (FS_GROUP * g + j), side * (FS_GROUP * g + j + 1))
            n_lt = B_W // LANE
            y_re = jnp.concatenate([y_ref[c, rows, :] for c in range(n_lt)], axis=1).astype(BF16)
            y_im = jnp.concatenate([y_ref[n_lt + c, rows, :] for c in range(n_lt)], axis=1).astype(BF16)
            zs.append(_dot(cmat, y_re) + _dot(smat, y_im))
        z = (jnp.concatenate(zs, axis=0) * scale).astype(BF16)
        o = _dot(z, wf_ref[0])
        for j in range(FS_GROUP):
            for c in range(B_W // LANE):
                o_ref[c, pl.ds(FS_GROUP * g + j, side, stride=side), :] = o[side * j:side * (j + 1),
                                                                            LANE * c:LANE * (c + 1)]


def _fourier_grid(ab_grid, cmat, smat, tw_cos, tw_sin, wf_blk, layer, n_seq, side):
    seq_len = side * side
    scale = float((seq_len * HEAD_DIM) ** -0.5)
    small = pl.BlockSpec((side, side), lambda s: (0, 0))
    return pl.pallas_call(
        functools.partial(_fourier_grid_kernel, side=side, scale=scale),
        grid=(n_seq,),
        in_specs=[
            pl.BlockSpec((side, side * 2 * B_W), lambda s: (s, 0)),
            small,
            small,
            small,
            small,
            pl.BlockSpec((1, B_W, B_W), lambda s: (layer, 0, 0)),
        ],
        out_specs=pl.BlockSpec((B_W // LANE, seq_len, LANE), lambda s: (0, s, 0)),
        out_shape=jax.ShapeDtypeStruct((B_W // LANE, n_seq * seq_len, LANE), F32),
        scratch_shapes=[pltpu.VMEM((2 * B_W // LANE, seq_len, LANE), F32)],
        compiler_params=_params(("arbitrary",), vmem_mib=56),
        name="fourier_grid",
    )(ab_grid, cmat, smat, tw_cos, tw_sin, wf_blk)


SEQS_PER_STEP = 2


def _mlstm_kernel(*refs, n_chunks):
    sps = SEQS_PER_STEP
    n_side = 3 + 2 * sps
    fwd, bwd = refs[:n_side], refs[n_side:2 * n_side]
    c0_ref, n0_ref, m0_ref, tril_ref, triu_ref = refs[2 * n_side:2 * n_side + 5]
    hf_ref, hb_ref, cout_ref, nout_ref, mout_ref, c_s, n_s, m_s = refs[2 * n_side + 5:]
    c = pl.program_id(1)
    hi = lax.Precision.HIGHEST
    lc = MLSTM_CHUNK
    pair_w = 2 * HEAD_DIM
    n_pairs = H_MLSTM // 2

    @pl.when(c == 0)
    def _():
        c_s[...] = c0_ref[...]
        n_s[...] = n0_ref[...]
        m_s[...] = m0_ref[...]

    lo_lane = lax.broadcasted_iota(jnp.int32, (1, pair_w), 1) < HEAD_DIM
    lo_row = lax.broadcasted_iota(jnp.int32, (pair_w, 1), 0) < HEAD_DIM
    row8 = lax.broadcasted_iota(jnp.int32, (8, 1), 0)
    cum_mask = [tril_ref[...], triu_ref[...]]
    keep_t = [triu_ref[...] > 0.5, tril_ref[...] > 0.5]
    pairs = [(j, d, hp) for j in range(sps) for d in range(2) for hp in range(n_pairs)]
    heads = [(pi, hh) for pi in range(len(pairs)) for hh in range(2)]
    rng = range(len(heads))

    pre = {}
    for j in range(sps):
        for d, side in enumerate((fwd, bwd)):
            q_ref, k_ref, g_ref = side[:3]
            gt_ref, vt_ref = side[3 + 2 * j], side[4 + 2 * j]
            go = 2 * H_MLSTM * d
            lf_cols = _log_sigmoid(g_ref[j, :, go + H_MLSTM:go + 2 * H_MLSTM])
            lf_rows = _log_sigmoid(gt_ref[go + H_MLSTM:go + 2 * H_MLSTM, :])
            b_cols = _dot(cum_mask[d], lf_cols, precision=hi)
            pre[j, d] = dict(
                a_cols=g_ref[j, :, go:go + H_MLSTM] - b_cols,
                ig_rows=gt_ref[go:go + H_MLSTM, :],
                b_rows=_dot_nt(lf_rows, cum_mask[d], precision=hi),
                q=q_ref[j].astype(BF16), k=(k_ref[j] * (HEAD_DIM ** -0.5)).astype(BF16), vt=vt_ref[...])

    def pair_cols(hp):
        return slice(pair_w * hp, pair_w * (hp + 1))

    q_p = [pre[j, d]["q"][:, pair_cols(hp)] for j, d, hp in pairs]
    k_p = [pre[j, d]["k"][:, pair_cols(hp)] for j, d, hp in pairs]
    vt_p = [pre[j, d]["vt"][pair_cols(hp), :] for j, d, hp in pairs]
    c_p = [c_s[j, d, hp] for j, d, hp in pairs]
    n_p = [n_s[j, n_pairs * d + hp:n_pairs * d + hp + 1, :] for j, d, hp in pairs]
    zero_k = jnp.zeros((lc, pair_w), BF16)
    k_h = [jnp.where(lo_lane, k_p[pi], zero_k) if hh == 0 else jnp.where(lo_lane, zero_k, k_p[pi])
           for pi, hh in heads]

    def head_of(i):
        pi, hh = heads[i]
        j, d, hp = pairs[pi]
        return j, d, 2 * hp + hh

    b_row = [pre[head_of(i)[0], head_of(i)[1]]["b_rows"][head_of(i)[2]:head_of(i)[2] + 1, :] for i in rng]
    ig_row = [pre[head_of(i)[0], head_of(i)[1]]["ig_rows"][head_of(i)[2]:head_of(i)[2] + 1, :] for i in rng]
    a_col = [pre[head_of(i)[0], head_of(i)[1]]["a_cols"][:, head_of(i)[2]:head_of(i)[2] + 1] for i in rng]
    m_st = [m_s[head_of(i)[0], H_MLSTM * head_of(i)[1] + head_of(i)[2]:H_MLSTM * head_of(i)[1] + head_of(i)[2] + 1, :]
            for i in rng]
    bl = [b_row[i][:, lc - 1:lc] if head_of(i)[1] == 0 else b_row[i][:, 0:1] for i in rng]

    d_t = [jnp.where(keep_t[head_of(i)[1]], b_row[i] + a_col[i], NEG) for i in rng]
    inter = [b_row[i] + m_st[i] for i in rng]
    m_t = [jnp.maximum(inter[i], jnp.max(d_t[i], axis=0, keepdims=True)) for i in rng]
    s_t = [_dot_nt(k_h[i], q_p[heads[i][0]]) * jnp.exp(d_t[i] - m_t[i]) for i in rng]
    w_in = [jnp.exp(inter[i] - m_t[i]) for i in rng]
    num_t = [_dot(vt_p[heads[i][0]].astype(BF16), s_t[i].astype(BF16)) for i in rng]
    qc_t = [_dot_nt(c_p[pi].astype(BF16), q_p[pi]) for pi in range(len(pairs))]
    n_mat = [jnp.where((row8 == 0) & lo_lane, n_p[pi], jnp.where((row8 == 1) & ~lo_lane, n_p[pi], 0.0))
             for pi in range(len(pairs))]
    nq = [_dot_nt(n_mat[pi].astype(BF16), q_p[pi]) for pi in range(len(pairs))]
    den = [jnp.sum(s_t[i], axis=0, keepdims=True) + w_in[i] * nq[heads[i][0]][heads[i][1]:heads[i][1] + 1, :]
           for i in rng]
    inv = [1.0 / jnp.maximum(jnp.abs(den[i]), jnp.exp(-m_t[i])) for i in rng]
    h_t = []
    for pi in range(len(pairs)):
        i0, i1 = 2 * pi, 2 * pi + 1
        num = jnp.where(lo_row, num_t[i0], num_t[i1])
        h_t.append((num + jnp.where(lo_row, w_in[i0], w_in[i1]) * qc_t[pi]) * jnp.where(lo_row, inv[i0], inv[i1]))
    for j in range(sps):
        base = 2 * n_pairs * j
        hf_ref[j] = jnp.concatenate(h_t[base:base + n_pairs], axis=0)
        hb_ref[j] = jnp.concatenate(h_t[base + n_pairs:base + 2 * n_pairs], axis=0)

    g_row = [bl[i] - b_row[i] + ig_row[i] for i in rng]
    m_new = [jnp.maximum(bl[i] + m_st[i], jnp.max(g_row[i], axis=-1, keepdims=True)) for i in rng]
    wc = [jnp.exp(bl[i] + m_st[i] - m_new[i]) for i in rng]
    ws_row = [jnp.exp(g_row[i] - m_new[i]) for i in rng]
    upd = [_dot((vt_p[heads[i][0]] * ws_row[i]).astype(BF16), k_p[heads[i][0]]) for i in rng]
    for pi, (j, d, hp) in enumerate(pairs):
        i0, i1 = 2 * pi, 2 * pi + 1
        block = jnp.where(lo_row & lo_lane, upd[i0], jnp.where(~lo_row & ~lo_lane, upd[i1], 0.0))
        c_s[j, d, hp] = jnp.where(lo_row, wc[i0], wc[i1]) * c_p[pi] + block
        ws_mat = jnp.where(row8 == 0, ws_row[i0], jnp.where(row8 == 1, ws_row[i1], 0.0))
        k_sum = _dot(ws_mat.astype(BF16), k_p[pi])
        row = n_pairs * d + hp
        n_s[j, row:row + 1, :] = (jnp.where(lo_lane, wc[i0], wc[i1]) * n_p[pi]
                                  + jnp.where(lo_lane, k_sum[0:1, :], k_sum[1:2, :]))
    for i in rng:
        j, d, hd = head_of(i)
        m_s[j, H_MLSTM * d + hd:H_MLSTM * d + hd + 1, :] = m_new[i]

    @pl.when(c == n_chunks - 1)
    def _():
        cout_ref[...] = c_s[...]
        nout_ref[...] = n_s[...]
        mout_ref[...] = m_s[...]


def _pair_states(c):
    b = c.shape[0]
    c = c.reshape(b, 2, H_MLSTM // 2, 2, HEAD_DIM, HEAD_DIM)
    zero = jnp.zeros_like(c[:, :, :, 0])
    top = jnp.concatenate([c[:, :, :, 0], zero], axis=-1)
    bottom = jnp.concatenate([zero, c[:, :, :, 1]], axis=-1)
    return jnp.concatenate([top, bottom], axis=-2)


def _unpair_states(cp):
    b = cp.shape[0]
    first = cp[:, :, :, :HEAD_DIM, :HEAD_DIM]
    second = cp[:, :, :, HEAD_DIM:, HEAD_DIM:]
    return jnp.stack([first, second], axis=3).reshape(b, 2, H_MLSTM, HEAD_DIM, HEAD_DIM)


def _mlstm(p, g, gt, vo, c0, n0, m0, tril, triu, n_seq, seq_len):
    lc = MLSTM_CHUNK
    nc = seq_len // lc
    n_st = 2 * H_MLSTM
    n_pairs = H_MLSTM // 2
    pair_w = 2 * HEAD_DIM
    sps = SEQS_PER_STEP
    p3 = p.reshape(n_seq, seq_len, P_COLS)
    g3 = g.reshape(n_seq, seq_len, N_GATE_COLS)

    def fwd(c):
        return c

    def bwd(c):
        return nc - 1 - c

    def side(chunk):
        tok = lambda col: (lambda b, c: (b, chunk(c), col))
        specs = [
            pl.BlockSpec((sps, lc, C_W), tok(QC_BLK)),
            pl.BlockSpec((sps, lc, C_W), tok(KC_BLK)),
            pl.BlockSpec((sps, lc, N_GATE_COLS), tok(0)),
        ]
        for j in range(sps):
            lanes = lambda b, c, j=j: (0, (b * sps + j) * nc + chunk(c))
            specs += [pl.BlockSpec((N_GATE_COLS, lc), lanes), pl.BlockSpec((C_W, lc), lanes)]
        return specs

    state_specs = [
        pl.BlockSpec((sps, 2, n_pairs, pair_w, pair_w), lambda b, c: (b, 0, 0, 0, 0)),
        pl.BlockSpec((sps, 2 * n_pairs, pair_w), lambda b, c: (b, 0, 0)),
        pl.BlockSpec((sps, n_st, 1), lambda b, c: (b, 0, 0)),
    ]
    tri_spec = pl.BlockSpec((lc, lc), lambda b, c: (0, 0))
    operands = [p3, p3, g3] + [gt, vo] * sps
    hf, hb, c_out, n_out, m_out = pl.pallas_call(
        functools.partial(_mlstm_kernel, n_chunks=nc),
        grid=(n_seq // sps, nc),
        in_specs=side(fwd) + side(bwd) + state_specs + [tri_spec, tri_spec],
        out_specs=[
            pl.BlockSpec((sps, C_W, lc), lambda b, c: (b, 0, c)),
            pl.BlockSpec((sps, C_W, lc), lambda b, c: (b, 0, nc - 1 - c)),
        ] + state_specs,
        out_shape=[
            jax.ShapeDtypeStruct((n_seq, C_W, seq_len), F32),
            jax.ShapeDtypeStruct((n_seq, C_W, seq_len), F32),
            jax.ShapeDtypeStruct((n_seq, 2, n_pairs, pair_w, pair_w), F32),
            jax.ShapeDtypeStruct((n_seq, 2 * n_pairs, pair_w), F32),
            jax.ShapeDtypeStruct((n_seq, n_st, 1), F32),
        ],
        scratch_shapes=[
            pltpu.VMEM((sps, 2, n_pairs, pair_w, pair_w), F32),
            pltpu.VMEM((sps, 2 * n_pairs, pair_w), F32),
            pltpu.VMEM((sps, n_st, 1), F32),
        ],
        compiler_params=_params(("arbitrary", "arbitrary")),
        name="mlstm",
    )(*operands, *operands, _pair_states(c0), n0.reshape(n_seq, 2 * n_pairs, pair_w), m0.reshape(n_seq, n_st, 1),
      tril, triu)
    return (hf, hb, _unpair_states(c_out), n_out.reshape(n_seq, 2, H_MLSTM, HEAD_DIM),
            m_out.reshape(n_seq, 2, H_MLSTM))


def _head_norm(y, g, ones_blk):
    ysq = y * y
    hi = ysq.astype(BF16)
    lo = (ysq - hi.astype(F32)).astype(BF16)
    ss = _dot(hi, ones_blk) + _dot(lo, ones_blk)
    return y * lax.rsqrt(ss * (1.0 / HEAD_DIM) + EPS) * g


def _merge_kernel(att_ref, four_ref, hf_ref, hb_ref, oc_ref, x_ref, mod_ref, gh_ref, ghm_ref, wo_ref, g2_ref, wrt_ref,
                  ones_ref, xo_ref, h2_ref, afft_ref, *, row_tiles):
    gh = gh_ref[0]
    ya = _head_norm(att_ref[...], gh[:, :A_W], ones_ref[...])
    four = jnp.concatenate([four_ref[c] for c in range(B_W // LANE)], axis=1)
    yf = _head_norm(four, gh[:, A_W:A_W + B_W], ones_ref[:B_W, :B_W])
    mem = hf_ref[0] + hb_ref[0]
    heads = []
    for hd in range(H_MLSTM):
        y = mem[HEAD_DIM * hd:HEAD_DIM * (hd + 1), :]
        heads.append(y * lax.rsqrt(jnp.mean(y * y, axis=0, keepdims=True) + EPS))
    ym_t = jnp.concatenate(heads, axis=0) * ghm_ref[0] * jax.nn.sigmoid(oc_ref[...])
    out = (_dot(ya.astype(BF16), wo_ref[0, :A_W, :])
           + _dot(yf.astype(BF16), wo_ref[0, A_W:A_W + B_W, :])
           + _dot_tn(ym_t.astype(BF16), wo_ref[0, A_W + B_W:, :]))
    x = x_ref[...] + mod_ref[0, 2:3, :] * out
    xo_ref[...] = x
    y2 = x * lax.rsqrt(jnp.mean(x * x, axis=-1, keepdims=True) + EPS) * g2_ref[0]
    h2 = (y2 * (1.0 + mod_ref[0, 4:5, :]) + mod_ref[0, 3:4, :]).astype(BF16)
    if row_tiles:
        h2_wide = h2.astype(F32)
        for s in range(N_SLAB):
            h2_ref[pl.ds(s, h2_wide.shape[0], stride=N_SLAB), :] = h2_wide[:, LANE * s:LANE * (s + 1)]
    else:
        h2_ref[...] = h2
    logits = _dot_nt(wrt_ref[0], h2)
    e = jnp.exp(logits - jnp.max(logits, axis=0, keepdims=True))
    afft_ref[...] = e / jnp.sum(e, axis=0, keepdims=True)


def _merge(att, four, hf, hb, vo, x2d, mod, layer, g_head, g_mem, w_out_bf, g2, w_rt_bf, ones_blk, n_seq, seq_len,
           mod_seq_len, row_tiles):
    t = x2d.shape[0]
    tm = MERGE_TILE
    tiles_per_mod = mod_seq_len // tm
    tiles_per_seq = seq_len // tm
    row = lambda i: (i, 0)
    lay = lambda i: (layer, 0, 0)
    mem = lambda i: (i // tiles_per_seq, 0, i % tiles_per_seq)
    if row_tiles:
        h2_spec, h2_shape = pl.BlockSpec((N_SLAB * tm, LANE), row), jax.ShapeDtypeStruct((N_SLAB * t, LANE), F32)
    else:
        h2_spec, h2_shape = pl.BlockSpec((tm, D_MODEL), row), jax.ShapeDtypeStruct((t, D_MODEL), BF16)
    return pl.pallas_call(
        functools.partial(_merge_kernel, row_tiles=row_tiles),
        grid=(t // tm,),
        in_specs=[
            pl.BlockSpec((tm, A_W), row),
            pl.BlockSpec((B_W // LANE, tm, LANE), lambda i: (0, i, 0)),
            pl.BlockSpec((1, C_W, tm), mem),
            pl.BlockSpec((1, C_W, tm), mem),
            pl.BlockSpec((C_W, tm), lambda i: (1, i)),
            pl.BlockSpec((tm, D_MODEL), row),
            pl.BlockSpec((1, 6, D_MODEL), lambda i: (i // tiles_per_mod, 0, 0)),
            pl.BlockSpec((1, 1, D_MODEL), lay),
            pl.BlockSpec((1, C_W, tm), lay),
            pl.BlockSpec((1, D_MODEL, D_MODEL), lay),
            pl.BlockSpec((1, 1, D_MODEL), lay),
            pl.BlockSpec((1, N_EXPERTS, D_MODEL), lay),
            pl.BlockSpec((A_W, A_W), lambda i: (0, 0)),
        ],
        out_specs=[
            pl.BlockSpec((tm, D_MODEL), row),
            h2_spec,
            pl.BlockSpec((N_EXPERTS, tm), lambda i: (0, i)),
        ],
        out_shape=[
            jax.ShapeDtypeStruct((t, D_MODEL), F32),
            h2_shape,
            jax.ShapeDtypeStruct((N_EXPERTS, t), F32),
        ],
        compiler_params=_params(("arbitrary",)),
        name="merge",
    )(att, four, hf, hb, vo, x2d, mod, g_head, g_mem, w_out_bf, g2, w_rt_bf, ones_blk)


BISECT_STEPS = 48
TOKEN_CHUNK = 1024


TOKEN_SPLIT = 64


def _route_kernel(aff_ref, triu_ref, idx_ref, gs_ref, sp_ref, *, ns, seq_len, cap):
    seqs = range(ns)
    aff = [aff_ref[:, seq_len * j:seq_len * (j + 1)] for j in seqs]

    def body(_, bounds):
        out = []
        for j in seqs:
            lo, hi = bounds[j]
            mid = 0.5 * (lo + hi)
            ge = jnp.sum(jnp.where(aff[j] >= mid, 1.0, 0.0), axis=1, keepdims=True) >= cap
            out.append((jnp.where(ge, mid, lo), jnp.where(ge, hi, mid)))
        return tuple(out)

    start = (jnp.zeros((N_EXPERTS, 1), F32), jnp.full((N_EXPERTS, 1), 2.0, F32))
    bounds = lax.fori_loop(0, BISECT_STEPS, body, tuple(start for _ in seqs))
    thr = [jnp.max(jnp.where(aff[j] < bounds[j][1], aff[j], -1.0), axis=1, keepdims=True) for j in seqs]
    need = [cap - jnp.sum(jnp.where(aff[j] > thr[j], 1.0, 0.0), axis=1, keepdims=True) for j in seqs]
    triu = triu_ref[...]
    eq_carry = [jnp.zeros((N_EXPERTS, 1), F32) for _ in seqs]
    pos_carry = [jnp.zeros((N_EXPERTS, 1), F32) for _ in seqs]
    for b in range(seq_len // 128):
        for j in seqs:
            blk = aff[j][:, 128 * b:128 * (b + 1)]
            eq = blk == thr[j]
            eq_f = jnp.where(eq, 1.0, 0.0)
            eq_inc = _dot(eq_f.astype(BF16), triu) + eq_carry[j]
            sel = (blk > thr[j]) | (eq & (eq_inc - eq_f < need[j]))
            sel_f = jnp.where(sel, 1.0, 0.0)
            pos_inc = _dot(sel_f.astype(BF16), triu) + pos_carry[j]
            t0 = seq_len * j + 128 * b
            sp_ref[:, t0:t0 + 128] = jnp.where(sel, pos_inc - sel_f, -1.0).astype(jnp.int32)
            eq_carry[j] = eq_inc[:, 127:128]
            pos_carry[j] = pos_inc[:, 127:128]

    tc = min(seq_len, TOKEN_CHUNK)
    slot = lax.broadcasted_iota(jnp.int32, (cap, tc), 0)
    part = lax.broadcasted_iota(jnp.int32, (8, tc), 0)
    tok = lax.broadcasted_iota(jnp.int32, (1, tc), 1).astype(F32)
    chunks = range(0, seq_len, tc)
    tok_hi = [jnp.floor((tok + float(t0)) * (1.0 / TOKEN_SPLIT)) for t0 in chunks]
    tok_lo = [tok + float(t0) - TOKEN_SPLIT * hi for t0, hi in zip(chunks, tok_hi)]

    def per_expert(e, carry):
        for j in seqs:
            acc = jnp.zeros((cap, 8), F32)
            for ci, t0 in enumerate(chunks):
                cols = slice(seq_len * j + t0, seq_len * j + t0 + tc)
                onehot = jnp.where(slot == sp_ref[pl.ds(e, 1), cols], 1.0, 0.0).astype(BF16)
                a = aff_ref[pl.ds(e, 1), cols]
                a_hi = a.astype(BF16).astype(F32)
                a_mid = (a - a_hi).astype(BF16).astype(F32)
                a_lo = a - a_hi - a_mid
                vals = jnp.where(part == 0, tok_hi[ci], jnp.where(part == 1, tok_lo[ci], jnp.where(
                    part == 2, a_hi, jnp.where(part == 3, a_mid, jnp.where(part == 4, a_lo, 0.0)))))
                acc = acc + _dot_nt(onehot, vals.astype(BF16))
            idx_ref[N_EXPERTS * j + e] = (N_SLAB * (TOKEN_SPLIT * acc[:, 0:1] + acc[:, 1:2])).astype(jnp.int32)
            gs_ref[e, cap * j:cap * (j + 1)] = acc[:, 2:3] + acc[:, 3:4] + acc[:, 4:5]
        return carry

    lax.fori_loop(0, N_EXPERTS, per_expert, 0)


def _route(afft, triu_bf, n_seq, seq_len, cap, ns):
    idx, gs, sp = pl.pallas_call(
        functools.partial(_route_kernel, ns=ns, seq_len=seq_len, cap=cap),
        grid=(n_seq // ns,),
        in_specs=[
            pl.BlockSpec((N_EXPERTS, ns * seq_len), lambda s: (0, s)),
            pl.BlockSpec((128, 128), lambda s: (0, 0)),
        ],
        out_specs=[
            pl.BlockSpec((ns * N_EXPERTS, cap, 1), lambda s: (s, 0, 0)),
            pl.BlockSpec((N_EXPERTS, ns * cap, 1), lambda s: (0, s, 0)),
            pl.BlockSpec((N_EXPERTS, ns * seq_len), lambda s: (0, s)),
        ],
        out_shape=[
            jax.ShapeDtypeStruct((n_seq * N_EXPERTS, cap, 1), jnp.int32),
            jax.ShapeDtypeStruct((N_EXPERTS, n_seq * cap, 1), F32),
            jax.ShapeDtypeStruct((N_EXPERTS, n_seq * seq_len), jnp.int32),
        ],
        compiler_params=_params(("arbitrary",)),
        name="route",
    )(afft, triu_bf)
    return idx.reshape(n_seq * N_EXPERTS * cap), gs, sp


ROW_COPIES = 8


def _row_tile(first_row):
    return pl.ds(pl.multiple_of(first_row, N_SLAB), N_SLAB)


def _gather_kernel(idx_ref, src_ref, xs_ref, tile_ref, *, eb, cap):
    ei = pl.program_id(1)

    def per_expert(ee, carry):
        e = ei * eb + ee

        def rows(g, c):
            slot0 = g * ROW_COPIES
            for u in range(ROW_COPIES):
                tile_ref[_row_tile((slot0 + u) * N_SLAB), :] = src_ref[_row_tile(idx_ref[e * cap + slot0 + u]), :]
            return c

        lax.fori_loop(0, cap // ROW_COPIES, rows, 0)
        for s in range(N_SLAB):
            xs_ref[ee, :, LANE * s:LANE * (s + 1)] = tile_ref[pl.ds(s, cap, stride=N_SLAB), :].astype(BF16)
        return carry

    lax.fori_loop(0, eb, per_expert, 0)


def _gather(idx, h2_rows, n_seq, cap, eb):
    return pl.pallas_call(
        functools.partial(_gather_kernel, eb=eb, cap=cap),
        grid=(n_seq, N_EXPERTS // eb),
        in_specs=[
            pl.BlockSpec((N_EXPERTS * cap,), lambda s, e: (s,), memory_space=pltpu.SMEM),
            pl.BlockSpec((h2_rows.shape[0] // n_seq, LANE), lambda s, e: (s, 0)),
        ],
        out_specs=pl.BlockSpec((eb, cap, D_MODEL), lambda s, e: (e, s, 0)),
        out_shape=jax.ShapeDtypeStruct((N_EXPERTS, n_seq * cap, D_MODEL), BF16),
        scratch_shapes=[pltpu.VMEM((N_SLAB * cap, LANE), F32)],
        compiler_params=_params(("arbitrary", "arbitrary"), vmem_mib=56),
        name="gather",
    )(idx, h2_rows)


def _selection(sp_ref, cap):
    seq_len = sp_ref.shape[1]
    slot = lax.broadcasted_iota(jnp.int32, (cap, seq_len), 0)
    return jnp.concatenate([jnp.where(slot == sp_ref[e:e + 1, :], 1.0, 0.0) for e in range(N_EXPERTS)],
                           axis=0).astype(BF16)


def _gather_short_kernel(sp_ref, h_ref, xs_ref, *, cap):
    rows = _dot(_selection(sp_ref, cap), h_ref[...])
    for e in range(N_EXPERTS):
        xs_ref[e] = rows[cap * e:cap * (e + 1), :].astype(BF16)


def _gather_short(sp, h2, n_seq, seq_len, cap):
    return pl.pallas_call(
        functools.partial(_gather_short_kernel, cap=cap),
        grid=(n_seq,),
        in_specs=[
            pl.BlockSpec((N_EXPERTS, seq_len), lambda s: (0, s)),
            pl.BlockSpec((seq_len, D_MODEL), lambda s: (s, 0)),
        ],
        out_specs=pl.BlockSpec((N_EXPERTS, cap, D_MODEL), lambda s: (0, s, 0)),
        out_shape=jax.ShapeDtypeStruct((N_EXPERTS, n_seq * cap, D_MODEL), BF16),
        compiler_params=_params(("arbitrary",)),
        name="gather_short",
    )(sp, h2)


def _scatter_short_kernel(sp_ref, y_ref, x_ref, mod_ref, gf_ref, o_ref, *, cap, final):
    sel = _selection(sp_ref, cap)
    y = jnp.concatenate([y_ref[e] for e in range(N_EXPERTS)], axis=0)
    y_hi = y.astype(BF16)
    y_lo = (y - y_hi.astype(F32)).astype(BF16)
    moe = _dot_tn(sel, y_hi) + _dot_tn(sel, y_lo)
    x = x_ref[...] + mod_ref[0, 5:6, :] * moe
    if final:
        x = x * lax.rsqrt(jnp.mean(x * x, axis=-1, keepdims=True) + EPS) * gf_ref[...]
    o_ref[...] = x


def _scatter_short(sp, ys, x2d, mod, g_final, n_seq, seq_len, cap, final):
    return pl.pallas_call(
        functools.partial(_scatter_short_kernel, cap=cap, final=final),
        grid=(n_seq,),
        in_specs=[
            pl.BlockSpec((N_EXPERTS, seq_len), lambda s: (0, s)),
            pl.BlockSpec((N_EXPERTS, cap, D_MODEL), lambda s: (0, s, 0)),
            pl.BlockSpec((seq_len, D_MODEL), lambda s: (s, 0)),
            pl.BlockSpec((1, 6, D_MODEL), lambda s: (0, 0, 0)),
            pl.BlockSpec((1, D_MODEL), lambda s: (0, 0)),
        ],
        out_specs=pl.BlockSpec((seq_len, D_MODEL), lambda s: (s, 0)),
        out_shape=jax.ShapeDtypeStruct((n_seq * seq_len, D_MODEL), F32),
        compiler_params=_params(("arbitrary",)),
        name="scatter_short",
    )(sp, ys, x2d, mod, g_final)


def _expert_kernel(xc_ref, xl_ref, gc_ref, gl_ref, wg_ref, wu_ref, wd_ref, yc_ref, yl_ref, *, n_f):
    f = pl.program_id(1)

    @pl.when(f == 0)
    def _():
        yc_ref[...] = jnp.zeros_like(yc_ref)
        yl_ref[...] = jnp.zeros_like(yl_ref)

    wg = wg_ref[0, 0].astype(BF16)
    wu = wu_ref[0, 0].astype(BF16)
    wd = wd_ref[0, 0].astype(BF16)
    for x_ref, y_ref in ((xc_ref, yc_ref), (xl_ref, yl_ref)):
        x = x_ref[0]
        mid = (_silu(_dot(x, wg)) * _dot(x, wu)).astype(BF16)
        y_ref[0] += _dot(mid, wd)

    @pl.when(f == n_f - 1)
    def _():
        yc_ref[0] = yc_ref[0] * gc_ref[0]
        yl_ref[0] = yl_ref[0] * gl_ref[0]


def _experts(xs_c, xs_l, gs_c, gs_l, w_g, w_u, w_d, layer):
    rc, rl = xs_c.shape[1], xs_l.shape[1]
    tf = 1024
    n_f = EXPERT_FF // tf
    return pl.pallas_call(
        functools.partial(_expert_kernel, n_f=n_f),
        grid=(N_EXPERTS, n_f),
        in_specs=[
            pl.BlockSpec((1, rc, D_MODEL), lambda e, f: (e, 0, 0)),
            pl.BlockSpec((1, rl, D_MODEL), lambda e, f: (e, 0, 0)),
            pl.BlockSpec((1, rc, 1), lambda e, f: (e, 0, 0)),
            pl.BlockSpec((1, rl, 1), lambda e, f: (e, 0, 0)),
            pl.BlockSpec((1, 1, D_MODEL, tf), lambda e, f: (layer, e, 0, f)),
            pl.BlockSpec((1, 1, D_MODEL, tf), lambda e, f: (layer, e, 0, f)),
            pl.BlockSpec((1, 1, tf, D_MODEL), lambda e, f: (layer, e, f, 0)),
        ],
        out_specs=[
            pl.BlockSpec((1, rc, D_MODEL), lambda e, f: (e, 0, 0)),
            pl.BlockSpec((1, rl, D_MODEL), lambda e, f: (e, 0, 0)),
        ],
        out_shape=[
            jax.ShapeDtypeStruct((N_EXPERTS, rc, D_MODEL), F32),
            jax.ShapeDtypeStruct((N_EXPERTS, rl, D_MODEL), F32),
        ],
        compiler_params=_params(("arbitrary", "arbitrary"), vmem_mib=56),
        name="experts",
    )(xs_c, xs_l, gs_c, gs_l, w_g, w_u, w_d)


def _scatter_kernel(idx_ref, y_ref, x_ref, mod_ref, gf_ref, o_ref, acc_ref, tile_ref, *, eb, n_e, cap, tm, final):
    step = pl.program_id(1)

    @pl.when(step == 0)
    def _():
        acc_ref[...] = jnp.zeros_like(acc_ref)

    @pl.when(step < n_e)
    def _():
        def per_expert(ee, carry):
            e = step * eb + ee
            for s in range(N_SLAB):
                tile_ref[pl.ds(s, cap, stride=N_SLAB), :] = y_ref[ee, :, LANE * s:LANE * (s + 1)]

            def rows(g, c):
                slot0 = g * ROW_COPIES
                dst = [idx_ref[e * cap + slot0 + u] for u in range(ROW_COPIES)]
                new = [acc_ref[_row_tile(dst[u]), :] + tile_ref[_row_tile((slot0 + u) * N_SLAB), :]
                       for u in range(ROW_COPIES)]
                for u in range(ROW_COPIES):
                    acc_ref[_row_tile(dst[u]), :] = new[u]
                return c

            lax.fori_loop(0, cap // ROW_COPIES, rows, 0)
            return carry

        lax.fori_loop(0, eb, per_expert, 0)

    @pl.when(step >= n_e)
    def _():
        base = pl.multiple_of((step - n_e) * tm * N_SLAB, N_SLAB)
        moe = jnp.concatenate([acc_ref[pl.ds(base + s, tm, stride=N_SLAB), :] for s in range(N_SLAB)], axis=1)
        x = x_ref[...] + mod_ref[0, 5:6, :] * moe
        if final:
            x = x * lax.rsqrt(jnp.mean(x * x, axis=-1, keepdims=True) + EPS) * gf_ref[...]
        o_ref[...] = x


def _scatter(idx, ys, x2d, mod, g_final, n_seq, seq_len, cap, eb, tm, final):
    n_e = N_EXPERTS // eb
    n_out = seq_len // tm
    out_blk = lambda s, j: (s * n_out + jnp.maximum(j - n_e, 0), 0)
    return pl.pallas_call(
        functools.partial(_scatter_kernel, eb=eb, n_e=n_e, cap=cap, tm=tm, final=final),
        grid=(n_seq, n_e + n_out),
        in_specs=[
            pl.BlockSpec((N_EXPERTS * cap,), lambda s, j: (s,), memory_space=pltpu.SMEM),
            pl.BlockSpec((eb, cap, D_MODEL), lambda s, j: (jnp.minimum(j, n_e - 1), s, 0)),
            pl.BlockSpec((tm, D_MODEL), out_blk),
            pl.BlockSpec((1, 6, D_MODEL), lambda s, j: (s, 0, 0)),
            pl.BlockSpec((1, D_MODEL), lambda s, j: (0, 0)),
        ],
        out_specs=pl.BlockSpec((tm, D_MODEL), out_blk),
        out_shape=jax.ShapeDtypeStruct((n_seq * seq_len, D_MODEL), F32),
        scratch_shapes=[
            pltpu.VMEM((N_SLAB * seq_len, LANE), F32),
            pltpu.VMEM((N_SLAB * cap, LANE), F32),
        ],
        compiler_params=_params(("arbitrary", "arbitrary"), vmem_mib=56),
        name="scatter",
    )(idx, ys, x2d, mod, g_final)


def kernel(x_prompt, x_sample, c, cache_k, cache_v, state_C, state_n, state_m, c_ctx, w_ada, b_ada, g_norm1, g_norm2, w_in, b_gates, rpb, w_fourier, g_head, w_out, w_router, w_exp_gate, w_exp_up, w_exp_down, g_final):
    n_ctx, len_ctx, _ = x_prompt.shape
    n_lat, len_lat, _ = x_sample.shape
    past = cache_k.shape[2]
    cap_ctx = CAPACITY_FACTOR * len_ctx // N_EXPERTS
    cap_lat = CAPACITY_FACTOR * len_lat // N_EXPERTS

    w_in_bf = w_in.astype(BF16)
    w_gt_bf = jnp.swapaxes(w_in[:, :, P_COLS:], 1, 2).astype(BF16)
    vc0, oc0 = VC_BLK * C_W, OC_BLK * C_W
    w_vo_bf = jnp.swapaxes(jnp.concatenate([w_in[:, :, vc0:vc0 + C_W], w_in[:, :, oc0:oc0 + C_W]], axis=2),
                           1, 2).astype(BF16)
    bg_row = b_gates.reshape(DEPTH, 1, N_GATE_COLS).astype(F32)
    bg_col = b_gates.reshape(DEPTH, N_GATE_COLS, 1).astype(F32)
    w_out_bf = w_out.astype(BF16)
    w_rt_bf = jnp.swapaxes(w_router, 1, 2).astype(BF16)
    g1 = g_norm1.reshape(DEPTH, 1, D_MODEL)
    g2 = g_norm2.reshape(DEPTH, 1, D_MODEL)
    gh = g_head.reshape(DEPTH, 1, D_MODEL)
    g_mem = jnp.broadcast_to(gh[:, 0, A_W + B_W:, None], (DEPTH, C_W, MERGE_TILE))
    eye_g = jnp.eye(G_FOURIER, dtype=F32)
    wf_blk = jnp.einsum("lgcd,gh->lgchd", w_fourier, eye_g).reshape(DEPTH, B_W, B_W).astype(BF16)

    csc = _channel_dft()
    dft_ctx = _dft_mats(len_ctx)
    dft_side = _dft_mats(GRID_W)
    tw_cos, tw_sin = _twiddles(GRID_W, len_lat)
    bias_tiles = _nbr_bias_tiles(rpb)
    r = np.arange(MLSTM_CHUNK)
    tril = jnp.asarray(r[:, None] >= r[None, :], F32)
    triu = jnp.asarray(r[:, None] <= r[None, :], F32)
    triu_bf = triu.astype(BF16)
    hidx = np.arange(A_W) // HEAD_DIM
    ones_blk = jnp.asarray(hidx[:, None] == hidx[None, :], BF16)

    cvecs = jnp.concatenate([c_ctx[None, :], c, jnp.zeros((8 - 1 - n_lat, D_MODEL), F32)], axis=0)
    mod_all = _modulation(cvecs, w_ada, b_ada).reshape(DEPTH, 8, 6, D_MODEL)

    cache_k4 = cache_k.reshape(n_lat, DEPTH, past, A_W)
    cache_v4 = cache_v.reshape(n_lat, DEPTH, past, A_W)
    zero_c = jnp.zeros((n_ctx, 2, H_MLSTM, HEAD_DIM, HEAD_DIM), F32)
    zero_n = jnp.zeros((n_ctx, 2, H_MLSTM, HEAD_DIM), F32)
    zero_m = jnp.zeros((n_ctx, 2, H_MLSTM), F32)

    xc = x_prompt.reshape(n_ctx * len_ctx, D_MODEL)
    xl = x_sample.reshape(n_lat * len_lat, D_MODEL)
    gf = g_final.reshape(1, D_MODEL)
    ks, vs, cs, ns, ms = [], [], [], [], []
    for l in range(DEPTH):
        mod_c = mod_all[l, 0:1]
        mod_l = mod_all[l, 1:1 + n_lat]

        pc, gc, gtc, voc, abc = _inproj(xc, mod_c, l, g1, w_in_bf, w_gt_bf, w_vo_bf, bg_row, bg_col, csc,
                                        n_ctx * len_ctx, False)
        att_c = _ctx_attention(pc, n_ctx, len_ctx)
        four_c = _fourier(abc, dft_ctx[0], dft_ctx[1], wf_blk, l, n_ctx, len_ctx)
        hf_c, hb_c, c_new, n_new, m_new = _mlstm(pc, gc, gtc, voc, zero_c, zero_n, zero_m, tril, triu, n_ctx, len_ctx)
        xc, h2c, affc = _merge(att_c, four_c, hf_c, hb_c, voc, xc, mod_c, l, gh, g_mem, w_out_bf, g2, w_rt_bf,
                               ones_blk, n_ctx, len_ctx, n_ctx * len_ctx, False)
        ks.append(pc[:, A_W:2 * A_W].reshape(n_ctx, len_ctx, H_ATT, HEAD_DIM))
        vs.append(pc[:, 2 * A_W:3 * A_W].reshape(n_ctx, len_ctx, H_ATT, HEAD_DIM))
        cs.append(c_new)
        ns.append(n_new)
        ms.append(m_new)

        pq, gq, gtq, voq, abq = _inproj(xl, mod_l, l, g1, w_in_bf, w_gt_bf, w_vo_bf, bg_row, bg_col, csc, len_lat,
                                        True)
        att_l = _nbr_attention(pq, cache_k4, cache_v4, bias_tiles, l, n_lat, len_lat)
        four_l = _fourier_grid(abq, dft_side[0], dft_side[1], tw_cos, tw_sin, wf_blk, l, n_lat, GRID_W)
        hf_l, hb_l, _, _, _ = _mlstm(pq, gq, gtq, voq, state_C[:, l], state_n[:, l], state_m[:, l], tril, triu,
                                     n_lat, len_lat)
        xl, h2l, affl = _merge(att_l, four_l, hf_l, hb_l, voq, xl, mod_l, l, gh, g_mem, w_out_bf, g2, w_rt_bf,
                               ones_blk, n_lat, len_lat, len_lat, True)

        last = l == DEPTH - 1
        _, gs_c, sp_c = _route(affc, triu_bf, n_ctx, len_ctx, cap_ctx, n_ctx)
        idx_l, gs_l, _ = _route(affl, triu_bf, n_lat, len_lat, cap_lat, 1)
        xs_c = _gather_short(sp_c, h2c, n_ctx, len_ctx, cap_ctx)
        xs_l = _gather(idx_l, h2l, n_lat, cap_lat, 4)
        ys_c, ys_l = _experts(xs_c, xs_l, gs_c, gs_l, w_exp_gate, w_exp_up, w_exp_down, l)
        xc = _scatter_short(sp_c, ys_c, xc, mod_c, gf, n_ctx, len_ctx, cap_ctx, last)
        xl = _scatter(idx_l, ys_l, xl, mod_l, gf, n_lat, len_lat, cap_lat, 2, 512, last)

    y_prompt = xc.reshape(n_ctx, len_ctx, D_MODEL)
    y_sample = xl.reshape(n_lat, len_lat, D_MODEL)
    return (y_prompt, y_sample, jnp.stack(ks, axis=1), jnp.stack(vs, axis=1), jnp.stack(cs, axis=1),
            jnp.stack(ns, axis=1), jnp.stack(ms, axis=1))
```

```python
import functools

import numpy as np
import jax
import jax.numpy as jnp
from jax import lax
from jax.experimental import pallas as pl
from jax.experimental.pallas import tpu as pltpu

F32 = jnp.float32
BF16 = jnp.bfloat16

D_MODEL = 1024
DEPTH = 2
HEAD_DIM = 64
H_ATT = 8
G_FOURIER = 4
H_MLSTM = 4
A_W = H_ATT * HEAD_DIM
B_W = G_FOURIER * HEAD_DIM
C_W = H_MLSTM * HEAD_DIM
N_GATE_COLS = 16
P_COLS = 3 * A_W + B_W + 4 * C_W
IN_COLS = P_COLS + N_GATE_COLS
GRID_W = 64
WIN_R = 8
WIN_C = 16
MLSTM_CHUNK = 128
N_EXPERTS = 16
CAPACITY_FACTOR = 2
EXPERT_FF = 2 * D_MODEL
EPS = 1e-6
NEG = -1e30

UB_OFF = 3 * A_W
QC_BLK, KC_BLK, VC_BLK, OC_BLK = 7, 8, 9, 10

LANE = 128
N_SLAB = D_MODEL // LANE
MERGE_TILE = 256
INPROJ_TILE = 512

NT_DIMS = (((1,), (1,)), ((), ()))
TN_DIMS = (((0,), (0,)), ((), ()))
MIB = 1024 * 1024


def _dot(a, b, precision=None):
    return jnp.dot(a, b, preferred_element_type=F32, precision=precision)


def _dot_nt(a, b, precision=None):
    return lax.dot_general(a, b, NT_DIMS, preferred_element_type=F32, precision=precision)


def _dot_tn(a, b):
    return lax.dot_general(a, b, TN_DIMS, preferred_element_type=F32)


def _params(sem, vmem_mib=48):
    return pltpu.CompilerParams(dimension_semantics=sem, vmem_limit_bytes=vmem_mib * MIB)


def _silu(x):
    return x * jax.nn.sigmoid(x)


def _log_sigmoid(x):
    return jnp.minimum(x, 0.0) - jnp.log1p(jnp.exp(-jnp.abs(x)))


def _mod_kernel(c_ref, w_ref, b_ref, o_ref):
    s = _silu(c_ref[...]).astype(BF16)
    o_ref[0] = _dot(s, w_ref[0].astype(BF16)) + b_ref[0]


def _modulation(cvecs, w_ada, b_ada):
    depth = w_ada.shape[0]
    tn = 1024
    return pl.pallas_call(
        _mod_kernel,
        grid=(depth, 6 * D_MODEL // tn),
        in_specs=[
            pl.BlockSpec((8, D_MODEL), lambda l, j: (0, 0)),
            pl.BlockSpec((1, D_MODEL, tn), lambda l, j: (l, 0, j)),
            pl.BlockSpec((1, 1, tn), lambda l, j: (l, 0, j)),
        ],
        out_specs=pl.BlockSpec((1, 8, tn), lambda l, j: (l, 0, j)),
        out_shape=jax.ShapeDtypeStruct((depth, 8, 6 * D_MODEL), F32),
        compiler_params=_params(("arbitrary", "arbitrary")),
        name="modulation",
    )(cvecs, w_ada, b_ada.reshape(depth, 1, 6 * D_MODEL))


def _inproj_kernel(x_ref, mod_ref, g1_ref, w_ref, wgt_ref, wvo_ref, bgr_ref, bgc_ref, csc_ref,
                   *rest, grid_rows, kv_seq_len):
    n_in = 2 if kv_seq_len else 0
    p_ref, g_ref, gt_ref, vo_ref, ab_ref = rest[n_in:n_in + 5]
    kv_refs = rest[n_in + 5:n_in + 5 + n_in]
    scratch = rest[n_in + 5 + n_in:]
    x = x_ref[...]
    y = x * lax.rsqrt(jnp.mean(x * x, axis=-1, keepdims=True) + EPS) * g1_ref[0]
    h = (y * (1.0 + mod_ref[0, 1:2, :]) + mod_ref[0, 0:1, :]).astype(BF16)
    for j in range(0, P_COLS, 256):
        pj = _dot(h, w_ref[0, :, j:j + 256])
        p_ref[:, j:j + 256] = pj
        if kv_seq_len and A_W <= j < 3 * A_W:
            kv_ref = kv_refs[(j - A_W) // A_W]
            c0 = (j - A_W) % A_W
            for b in range(pj.shape[0] // kv_seq_len):
                kv_ref[b, 0, :, c0:c0 + 256] = pj[kv_seq_len * b:kv_seq_len * (b + 1), :]
        if j == UB_OFF:
            ab = _dot(pj.astype(BF16), csc_ref[...])
            if grid_rows:
                stage_ref, = scratch
                n_lt = 2 * B_W // LANE
                for c in range(n_lt):
                    stage_ref[c] = ab[:, LANE * c:LANE * (c + 1)]
                for n2 in range(GRID_W):
                    for c in range(n_lt):
                        col = 2 * B_W * n2 + LANE * c
                        ab_ref[:, col:col + LANE] = stage_ref[c, pl.ds(n2, grid_rows, stride=GRID_W), :]
            else:
                ab_ref[...] = ab.astype(BF16)
    g_ref[...] = _dot(h, w_ref[0, :, P_COLS:IN_COLS]) + bgr_ref[0]
    gt_ref[...] = _dot_nt(wgt_ref[0], h) + bgc_ref[0]
    vo_ref[...] = _dot_nt(wvo_ref[0], h)


def _inproj(x2d, mod, layer, g1, w_in_bf, w_gt_bf, w_vo_bf, bg_row, bg_col, csc, seq_len, grid_ab, kv_cache=None):
    t = x2d.shape[0]
    tm = INPROJ_TILE
    tiles_per_seq = seq_len // tm
    grid_rows = tm // GRID_W if grid_ab else 0
    if grid_ab:
        ab_spec = pl.BlockSpec((grid_rows, GRID_W * 2 * B_W), lambda i: (i, 0))
        ab_shape = jax.ShapeDtypeStruct((t // GRID_W, GRID_W * 2 * B_W), F32)
        scratch = [pltpu.VMEM((2 * B_W // LANE, tm, LANE), F32)]
    else:
        ab_spec = pl.BlockSpec((tm, 2 * B_W), lambda i: (i, 0))
        ab_shape = jax.ShapeDtypeStruct((t, 2 * B_W), BF16)
        scratch = []
    kv_seq_len = kv_cache[0].shape[2] if kv_cache else 0
    kv_in_specs, kv_out_specs, kv_shapes, aliases = [], [], [], {}
    if kv_cache:
        per_tile = tm // kv_seq_len
        kv_in_specs = [pl.BlockSpec(memory_space=pl.ANY)] * 2
        kv_out_specs = [pl.BlockSpec((per_tile, 1, kv_seq_len, A_W), lambda i: (i, layer, 0, 0))] * 2
        kv_shapes = [jax.ShapeDtypeStruct(a.shape, a.dtype) for a in kv_cache]
        aliases = {9: 5, 10: 6}
    return pl.pallas_call(
        functools.partial(_inproj_kernel, grid_rows=grid_rows, kv_seq_len=kv_seq_len),
        grid=(t // tm,),
        input_output_aliases=aliases,
        in_specs=[
            pl.BlockSpec((tm, D_MODEL), lambda i: (i, 0)),
            pl.BlockSpec((1, 6, D_MODEL), lambda i: (i // tiles_per_seq, 0, 0)),
            pl.BlockSpec((1, 1, D_MODEL), lambda i: (layer, 0, 0)),
            pl.BlockSpec((1, D_MODEL, IN_COLS), lambda i: (layer, 0, 0)),
            pl.BlockSpec((1, N_GATE_COLS, D_MODEL), lambda i: (layer, 0, 0)),
            pl.BlockSpec((1, 2 * C_W, D_MODEL), lambda i: (layer, 0, 0)),
            pl.BlockSpec((1, 1, N_GATE_COLS), lambda i: (layer, 0, 0)),
            pl.BlockSpec((1, N_GATE_COLS, 1), lambda i: (layer, 0, 0)),
            pl.BlockSpec((B_W, 2 * B_W), lambda i: (0, 0)),
        ] + kv_in_specs,
        out_specs=[
            pl.BlockSpec((tm, P_COLS), lambda i: (i, 0)),
            pl.BlockSpec((tm, N_GATE_COLS), lambda i: (i, 0)),
            pl.BlockSpec((N_GATE_COLS, tm), lambda i: (0, i)),
            pl.BlockSpec((2 * C_W, tm), lambda i: (0, i)),
            ab_spec,
        ] + kv_out_specs,
        out_shape=[
            jax.ShapeDtypeStruct((t, P_COLS), F32),
            jax.ShapeDtypeStruct((t, N_GATE_COLS), F32),
            jax.ShapeDtypeStruct((N_GATE_COLS, t), F32),
            jax.ShapeDtypeStruct((2 * C_W, t), F32),
            ab_shape,
        ] + kv_shapes,
        scratch_shapes=scratch,
        compiler_params=_params(("arbitrary",)),
        name="inproj",
    )(x2d, mod, g1, w_in_bf, w_gt_bf, w_vo_bf, bg_row, bg_col, csc, *(kv_cache or ()))


def _ctx_attn_kernel(q_ref, k_ref, v_ref, o_ref):
    scale = HEAD_DIM ** -0.5
    heads = range(H_ATT)
    sl = [slice(HEAD_DIM * h, HEAD_DIM * (h + 1)) for h in heads]
    s = [_dot_nt((q_ref[:, sl[h]] * scale).astype(BF16), k_ref[:, sl[h]].astype(BF16)) for h in heads]
    e = [jnp.exp(s[h] - jnp.max(s[h], axis=-1, keepdims=True)) for h in heads]
    w = [e[h] * (1.0 / jnp.sum(e[h], axis=-1, keepdims=True)) for h in heads]
    o_ref[...] = jnp.concatenate([_dot(w[h].astype(BF16), v_ref[:, sl[h]].astype(BF16)) for h in heads], axis=1)


def _ctx_attention(p, n_seq, seq_len):
    return pl.pallas_call(
        _ctx_attn_kernel,
        grid=(n_seq,),
        in_specs=[
            pl.BlockSpec((seq_len, A_W), lambda b: (b, 0)),
            pl.BlockSpec((seq_len, A_W), lambda b: (b, 1)),
            pl.BlockSpec((seq_len, A_W), lambda b: (b, 2)),
        ],
        out_specs=pl.BlockSpec((seq_len, A_W), lambda b: (b, 0)),
        out_shape=jax.ShapeDtypeStruct((n_seq * seq_len, A_W), F32),
        compiler_params=_params(("arbitrary",)),
        name="ctx_attention",
    )(p, p, p)


Q_ROWS = 8
K_ROWS = 16
KEY_BLK = 256


NO_ROW = 2 * WIN_R - 1


def _nbr_bias_tiles(rpb):
    n_c = 2 * WIN_C - 1
    cq = np.arange(GRID_W)[:, None]
    ck = np.arange(GRID_W)[None, :]
    cs = np.clip(cq - WIN_C // 2, 0, GRID_W - WIN_C)
    col_ok = (ck >= cs) & (ck < cs + WIN_C)
    pick = np.where(col_ok, np.clip(ck - cq + WIN_C - 1, 0, n_c - 1), n_c)
    sel = np.zeros((2, GRID_W, 2 * GRID_W, n_c + 1), np.float32)
    for side in range(2):
        sel[side, cq, side * GRID_W + ck, pick] = 1.0
    ext = jnp.full(rpb.shape[:2] + (2 * WIN_R, n_c + 1), NEG, F32)
    ext = ext.at[:, :, :2 * WIN_R - 1, :n_c].set(rpb.astype(F32))
    return jnp.einsum("lhdm,sqkm->lhdsqk", ext, jnp.asarray(sel), precision=lax.Precision.HIGHEST)


def _nbr_attn_kernel(q_ref, k0_ref, k1_ref, k2_ref, k3_ref, v0_ref, v1_ref, v2_ref, v3_ref,
                     ck_ref, cv_ref, tab_ref, o_ref):
    scale = HEAD_DIM ** -0.5
    k_refs = (k0_ref, k1_ref, k2_ref, k3_ref)
    v_refs = (v0_ref, v1_ref, v2_ref, v3_ref)
    rb = pl.program_id(1)
    rows = GRID_W
    key_row0 = jnp.clip(Q_ROWS * rb - WIN_R // 2, 0, rows - K_ROWS)
    d = []
    for rq in range(Q_ROWS):
        r = Q_ROWS * rb + rq
        rs = jnp.clip(r - WIN_R // 2, 0, rows - WIN_R)
        d_row = []
        for rk in range(K_ROWS):
            rka = key_row0 + rk
            ok = (rka >= rs) & (rka < rs + WIN_R)
            d_row.append(jnp.where(ok, rka - r + WIN_R - 1, NO_ROW))
        d.append(d_row)
    rows_per_blk = KEY_BLK // GRID_W
    heads = range(2)
    sl = [slice(HEAD_DIM * hh, HEAD_DIM * (hh + 1)) for hh in heads]
    q = [(q_ref[:, sl[hh]] * scale).astype(BF16) for hh in heads]
    s_ctx = [_dot_nt(q[hh], ck_ref[0, 0, :, sl[hh]].astype(BF16)) for hh in heads]
    s_loc = [[] for _ in heads]
    for j in range(4):
        for hh in heads:
            bias = jnp.concatenate([
                jnp.concatenate([
                    tab_ref[0, hh, d[rq][rows_per_blk * j + 2 * p], 0]
                    + tab_ref[0, hh, d[rq][rows_per_blk * j + 2 * p + 1], 1]
                    for p in range(rows_per_blk // 2)], axis=1)
                for rq in range(Q_ROWS)], axis=0)
            s_loc[hh].append(_dot_nt(q[hh], k_refs[j][:, sl[hh]].astype(BF16)) + bias)
    m = [jnp.max(s_ctx[hh], axis=-1, keepdims=True) for hh in heads]
    for j in range(4):
        m = [jnp.maximum(m[hh], jnp.max(s_loc[hh][j], axis=-1, keepdims=True)) for hh in heads]
    e_ctx = [jnp.exp(s_ctx[hh] - m[hh]) for hh in heads]
    den = [jnp.sum(e_ctx[hh], axis=-1, keepdims=True) for hh in heads]
    num = [_dot(e_ctx[hh].astype(BF16), cv_ref[0, 0, :, sl[hh]].astype(BF16)) for hh in heads]
    for j in range(4):
        e = [jnp.exp(s_loc[hh][j] - m[hh]) for hh in heads]
        den = [den[hh] + jnp.sum(e[hh], axis=-1, keepdims=True) for hh in heads]
        num = [num[hh] + _dot(e[hh].astype(BF16), v_refs[j][:, sl[hh]].astype(BF16)) for hh in heads]
    o_ref[...] = jnp.concatenate([num[hh] / den[hh] for hh in heads], axis=1)


def _nbr_attention(p, cache_k4, cache_v4, bias_tiles, layer, n_seq, seq_len):
    q_tok = Q_ROWS * GRID_W
    n_rb = seq_len // q_tok
    kb_per_seq = seq_len // KEY_BLK
    max_base = kb_per_seq - 4

    def kmap(j, col0):
        def f(hp, rb, b):
            base = jnp.clip(2 * rb - 1, 0, max_base)
            return (b * kb_per_seq + base + j, col0 + hp)
        return f

    past = cache_k4.shape[2]
    in_specs = [pl.BlockSpec((q_tok, 128), lambda hp, rb, b: (b * n_rb + rb, hp))]
    in_specs += [pl.BlockSpec((KEY_BLK, 128), kmap(j, A_W // 128)) for j in range(4)]
    in_specs += [pl.BlockSpec((KEY_BLK, 128), kmap(j, 2 * A_W // 128)) for j in range(4)]
    in_specs += [
        pl.BlockSpec((1, 1, past, 128), lambda hp, rb, b: (b, layer, 0, hp)),
        pl.BlockSpec((1, 1, past, 128), lambda hp, rb, b: (b, layer, 0, hp)),
        pl.BlockSpec((1, 2, 2 * WIN_R, 2, GRID_W, 2 * GRID_W), lambda hp, rb, b: (layer, hp, 0, 0, 0, 0)),
    ]
    return pl.pallas_call(
        _nbr_attn_kernel,
        grid=(H_ATT // 2, n_rb, n_seq),
        in_specs=in_specs,
        out_specs=pl.BlockSpec((q_tok, 128), lambda hp, rb, b: (b * n_rb + rb, hp)),
        out_shape=jax.ShapeDtypeStruct((n_seq * seq_len, A_W), F32),
        compiler_params=_params(("arbitrary", "arbitrary", "arbitrary")),
        name="nbr_attention",
    )(p, p, p, p, p, p, p, p, p, cache_k4, cache_v4, bias_tiles)


def _dft_mats(n):
    idx = jnp.arange(n, dtype=jnp.int32)
    ang = ((idx[:, None] * idx[None, :]) % n).astype(F32) * (2.0 * np.pi / n)
    return jnp.cos(ang).astype(BF16), jnp.sin(ang).astype(BF16)


def _channel_dft():
    c = np.arange(HEAD_DIM)
    ang = 2.0 * np.pi * ((c[:, None] * c[None, :]) % HEAD_DIM) / HEAD_DIM
    eye = np.eye(G_FOURIER)
    mats = np.concatenate([np.kron(eye, np.cos(ang)), np.kron(eye, np.sin(ang))], axis=1)
    return jnp.asarray(mats, F32).astype(BF16)


def _fourier_kernel(c_ref, s_ref, ab_ref, wf_ref, o_ref, acc_ref, *, scale, n_k):
    k = pl.program_id(2)

    @pl.when(k == 0)
    def _():
        acc_ref[...] = jnp.zeros_like(acc_ref)

    acc_ref[...] += _dot(c_ref[...], ab_ref[:, :B_W]) - _dot(s_ref[...], ab_ref[:, B_W:])

    @pl.when(k == n_k - 1)
    def _():
        z = (acc_ref[...] * scale).astype(BF16)
        o = _dot(z, wf_ref[0])
        for c in range(B_W // LANE):
            o_ref[c] = o[:, LANE * c:LANE * (c + 1)]


def _fourier(ab, cmat, smat, wf_blk, layer, n_seq, seq_len):
    ti = min(seq_len, 512)
    tk = min(seq_len, 1024)
    n_i, n_k = seq_len // ti, seq_len // tk
    scale = float((seq_len * HEAD_DIM) ** -0.5)
    return pl.pallas_call(
        functools.partial(_fourier_kernel, scale=scale, n_k=n_k),
        grid=(n_seq, n_i, n_k),
        in_specs=[
            pl.BlockSpec((ti, tk), lambda s, i, k: (i, k)),
            pl.BlockSpec((ti, tk), lambda s, i, k: (i, k)),
            pl.BlockSpec((tk, 2 * B_W), lambda s, i, k: (s * n_k + k, 0)),
            pl.BlockSpec((1, B_W, B_W), lambda s, i, k: (layer, 0, 0)),
        ],
        out_specs=pl.BlockSpec((B_W // LANE, ti, LANE), lambda s, i, k: (0, s * n_i + i, 0)),
        out_shape=jax.ShapeDtypeStruct((B_W // LANE, n_seq * seq_len, LANE), F32),
        scratch_shapes=[pltpu.VMEM((ti, B_W), F32)],
        compiler_params=_params(("arbitrary", "arbitrary", "arbitrary")),
        name="fourier",
    )(cmat, smat, ab, wf_blk)


FS_GROUP = 8


def _twiddles(side, n):
    k1 = jnp.arange(side, dtype=jnp.int32)[:, None]
    n2 = jnp.arange(side, dtype=jnp.int32)[None, :]
    ang = (k1 * n2).astype(F32) * (2.0 * np.pi / n)
    return jnp.cos(ang), jnp.sin(ang)


def _fourier_grid_kernel(ab_ref, c_ref, s_ref, tc_ref, ts_ref, wf_ref, o_ref, y_ref, *, side, scale):
    cmat = c_ref[...]
    smat = s_ref[...]
    for g in range(side // FS_GROUP):
        ab = ab_ref[:, 2 * B_W * FS_GROUP * g:2 * B_W * FS_GROUP * (g + 1)].astype(BF16)
        m1 = _dot(cmat, ab)
        m2 = _dot(smat, ab)
        for t in range(FS_GROUP):
            n2 = FS_GROUP * g + t
            a0 = 2 * B_W * t
            yr = m1[:, a0:a0 + B_W] - m2[:, a0 + B_W:a0 + 2 * B_W]
            yi = -(m1[:, a0 + B_W:a0 + 2 * B_W] + m2[:, a0:a0 + B_W])
            ct = tc_ref[:, n2:n2 + 1]
            st = ts_ref[:, n2:n2 + 1]
            y = jnp.concatenate([yr * ct + yi * st, yi * ct - yr * st], axis=1)
            for c in range(2 * B_W // LANE):
                y_ref[c, pl.ds(n2, side, stride=side), :] = y[:, LANE * c:LANE * (c + 1)]
    for g in range(side // FS_GROUP):
        zs = []
        for j in range(FS_GROUP):
            rows = slice(side * (FS_GROUP * g + j), side * (FS_GROUP * g + j + 1))
            n_lt = B_W // LANE
            y_re = jnp.concatenate([y_ref[c, rows, :] for c in range(n_lt)], axis=1).astype(BF16)
            y_im = jnp.concatenate([y_ref[n_lt + c, rows, :] for c in range(n_lt)], axis=1).astype(BF16)
            zs.append(_dot(cmat, y_re) + _dot(smat, y_im))
        z = (jnp.concatenate(zs, axis=0) * scale).astype(BF16)
        o = _dot(z, wf_ref[0])
        for j in range(FS_GROUP):
            for c in range(B_W // LANE):
                o_ref[c, pl.ds(FS_GROUP * g + j, side, stride=side), :] = o[side * j:side * (j + 1),
                                                                            LANE * c:LANE * (c + 1)]


def _fourier_grid(ab_grid, cmat, smat, tw_cos, tw_sin, wf_blk, layer, n_seq, side):
    seq_len = side * side
    scale = float((seq_len * HEAD_DIM) ** -0.5)
    small = pl.BlockSpec((side, side), lambda s: (0, 0))
    return pl.pallas_call(
        functools.partial(_fourier_grid_kernel, side=side, scale=scale),
        grid=(n_seq,),
        in_specs=[
            pl.BlockSpec((side, side * 2 * B_W), lambda s: (s, 0)),
            small,
            small,
            small,
            small,
            pl.BlockSpec((1, B_W, B_W), lambda s: (layer, 0, 0)),
        ],
        out_specs=pl.BlockSpec((B_W // LANE, seq_len, LANE), lambda s: (0, s, 0)),
        out_shape=jax.ShapeDtypeStruct((B_W // LANE, n_seq * seq_len, LANE), F32),
        scratch_shapes=[pltpu.VMEM((2 * B_W // LANE, seq_len, LANE), F32)],
        compiler_params=_params(("arbitrary",), vmem_mib=56),
        name="fourier_grid",
    )(ab_grid, cmat, smat, tw_cos, tw_sin, wf_blk)


SEQS_PER_STEP = 2


def _mlstm_kernel(*refs, n_chunks):
    sps = SEQS_PER_STEP
    n_side = 3 + 2 * sps
    fwd, bwd = refs[:n_side], refs[n_side:2 * n_side]
    c0_ref, n0_ref, m0_ref, tril_ref, triu_ref = refs[2 * n_side:2 * n_side + 5]
    hf_ref, hb_ref, cout_ref, nout_ref, mout_ref, c_s, n_s, m_s = refs[2 * n_side + 5:]
    c = pl.program_id(1)
    hi = lax.Precision.HIGHEST
    lc = MLSTM_CHUNK
    pair_w = 2 * HEAD_DIM
    n_pairs = H_MLSTM // 2

    @pl.when(c == 0)
    def _():
        c_s[...] = c0_ref[...]
        n_s[...] = n0_ref[...]
        m_s[...] = m0_ref[...]

    lo_lane = lax.broadcasted_iota(jnp.int32, (1, pair_w), 1) < HEAD_DIM
    lo_row = lax.broadcasted_iota(jnp.int32, (pair_w, 1), 0) < HEAD_DIM
    row8 = lax.broadcasted_iota(jnp.int32, (8, 1), 0)
    cum_mask = [tril_ref[...], triu_ref[...]]
    keep_t = [triu_ref[...] > 0.5, tril_ref[...] > 0.5]
    pairs = [(j, d, hp) for j in range(sps) for d in range(2) for hp in range(n_pairs)]
    heads = [(pi, hh) for pi in range(len(pairs)) for hh in range(2)]
    rng = range(len(heads))

    pre = {}
    for j in range(sps):
        for d, side in enumerate((fwd, bwd)):
            q_ref, k_ref, g_ref = side[:3]
            gt_ref, vt_ref = side[3 + 2 * j], side[4 + 2 * j]
            go = 2 * H_MLSTM * d
            lf_cols = _log_sigmoid(g_ref[j, :, go + H_MLSTM:go + 2 * H_MLSTM])
            lf_rows = _log_sigmoid(gt_ref[go + H_MLSTM:go + 2 * H_MLSTM, :])
            b_cols = _dot(cum_mask[d], lf_cols, precision=hi)
            pre[j, d] = dict(
                a_cols=g_ref[j, :, go:go + H_MLSTM] - b_cols,
                ig_rows=gt_ref[go:go + H_MLSTM, :],
                b_rows=_dot_nt(lf_rows, cum_mask[d], precision=hi),
                q=q_ref[j].astype(BF16), k=(k_ref[j] * (HEAD_DIM ** -0.5)).astype(BF16), vt=vt_ref[...])

    def pair_cols(hp):
        return slice(pair_w * hp, pair_w * (hp + 1))

    q_p = [pre[j, d]["q"][:, pair_cols(hp)] for j, d, hp in pairs]
    k_p = [pre[j, d]["k"][:, pair_cols(hp)] for j, d, hp in pairs]
    vt_p = [pre[j, d]["vt"][pair_cols(hp), :] for j, d, hp in pairs]
    c_p = [c_s[j, d, hp] for j, d, hp in pairs]
    n_p = [n_s[j, n_pairs * d + hp:n_pairs * d + hp + 1, :] for j, d, hp in pairs]
    zero_k = jnp.zeros((lc, pair_w), BF16)
    k_h = [jnp.where(lo_lane, k_p[pi], zero_k) if hh == 0 else jnp.where(lo_lane, zero_k, k_p[pi])
           for pi, hh in heads]

    def head_of(i):
        pi, hh = heads[i]
        j, d, hp = pairs[pi]
        return j, d, 2 * hp + hh

    b_row = [pre[head_of(i)[0], head_of(i)[1]]["b_rows"][head_of(i)[2]:head_of(i)[2] + 1, :] for i in rng]
    ig_row = [pre[head_of(i)[0], head_of(i)[1]]["ig_rows"][head_of(i)[2]:head_of(i)[2] + 1, :] for i in rng]
    a_col = [pre[head_of(i)[0], head_of(i)[1]]["a_cols"][:, head_of(i)[2]:head_of(i)[2] + 1] for i in rng]
    m_st = [m_s[head_of(i)[0], H_MLSTM * head_of(i)[1] + head_of(i)[2]:H_MLSTM * head_of(i)[1] + head_of(i)[2] + 1, :]
            for i in rng]
    bl = [b_row[i][:, lc - 1:lc] if head_of(i)[1] == 0 else b_row[i][:, 0:1] for i in rng]

    d_t = [jnp.where(keep_t[head_of(i)[1]], b_row[i] + a_col[i], NEG) for i in rng]
    inter = [b_row[i] + m_st[i] for i in rng]
    m_t = [jnp.maximum(inter[i], jnp.max(d_t[i], axis=0, keepdims=True)) for i in rng]
    s_t = [_dot_nt(k_h[i], q_p[heads[i][0]]) * jnp.exp(d_t[i] - m_t[i]) for i in rng]
    w_in = [jnp.exp(inter[i] - m_t[i]) for i in rng]
    num_t = [_dot(vt_p[heads[i][0]].astype(BF16), s_t[i].astype(BF16)) for i in rng]
    qc_t = [_dot_nt(c_p[pi].astype(BF16), q_p[pi]) for pi in range(len(pairs))]
    n_mat = [jnp.where((row8 == 0) & lo_lane, n_p[pi], jnp.where((row8 == 1) & ~lo_lane, n_p[pi], 0.0))
             for pi in range(len(pairs))]
    nq = [_dot_nt(n_mat[pi].astype(BF16), q_p[pi]) for pi in range(len(pairs))]
    den = [jnp.sum(s_t[i], axis=0, keepdims=True) + w_in[i] * nq[heads[i][0]][heads[i][1]:heads[i][1] + 1, :]
           for i in rng]
    inv = [1.0 / jnp.maximum(jnp.abs(den[i]), jnp.exp(-m_t[i])) for i in rng]
    h_t = []
    for pi in range(len(pairs)):
        i0, i1 = 2 * pi, 2 * pi + 1
        num = jnp.where(lo_row, num_t[i0], num_t[i1])
        h_t.append((num + jnp.where(lo_row, w_in[i0], w_in[i1]) * qc_t[pi]) * jnp.where(lo_row, inv[i0], inv[i1]))
    for j in range(sps):
        base = 2 * n_pairs * j
        hf_ref[j] = jnp.concatenate(h_t[base:base + n_pairs], axis=0)
        hb_ref[j] = jnp.concatenate(h_t[base + n_pairs:base + 2 * n_pairs], axis=0)

    g_row = [bl[i] - b_row[i] + ig_row[i] for i in rng]
    m_new = [jnp.maximum(bl[i] + m_st[i], jnp.max(g_row[i], axis=-1, keepdims=True)) for i in rng]
    wc = [jnp.exp(bl[i] + m_st[i] - m_new[i]) for i in rng]
    ws_row = [jnp.exp(g_row[i] - m_new[i]) for i in rng]
    upd = [_dot((vt_p[heads[i][0]] * ws_row[i]).astype(BF16), k_p[heads[i][0]]) for i in rng]
    for pi, (j, d, hp) in enumerate(pairs):
        i0, i1 = 2 * pi, 2 * pi + 1
        block = jnp.where(lo_row & lo_lane, upd[i0], jnp.where(~lo_row & ~lo_lane, upd[i1], 0.0))
        c_s[j, d, hp] = jnp.where(lo_row, wc[i0], wc[i1]) * c_p[pi] + block
        ws_mat = jnp.where(row8 == 0, ws_row[i0], jnp.where(row8 == 1, ws_row[i1], 0.0))
        k_sum = _dot(ws_mat.astype(BF16), k_p[pi])
        row = n_pairs * d + hp
        n_s[j, row:row + 1, :] = (jnp.where(lo_lane, wc[i0], wc[i1]) * n_p[pi]
                                  + jnp.where(lo_lane, k_sum[0:1, :], k_sum[1:2, :]))
    for i in rng:
        j, d, hd = head_of(i)
        m_s[j, H_MLSTM * d + hd:H_MLSTM * d + hd + 1, :] = m_new[i]

    @pl.when(c == n_chunks - 1)
    def _():
        cout_ref[...] = c_s[...]
        nout_ref[...] = n_s[...]
        mout_ref[...] = m_s[...]


def _pair_states(c):
    b = c.shape[0]
    c = c.reshape(b, 2, H_MLSTM // 2, 2, HEAD_DIM, HEAD_DIM)
    zero = jnp.zeros_like(c[:, :, :, 0])
    top = jnp.concatenate([c[:, :, :, 0], zero], axis=-1)
    bottom = jnp.concatenate([zero, c[:, :, :, 1]], axis=-1)
    return jnp.concatenate([top, bottom], axis=-2)


def _unpair_states(cp):
    b = cp.shape[0]
    first = cp[:, :, :, :HEAD_DIM, :HEAD_DIM]
    second = cp[:, :, :, HEAD_DIM:, HEAD_DIM:]
    return jnp.stack([first, second], axis=3).reshape(b, 2, H_MLSTM, HEAD_DIM, HEAD_DIM)


def _mlstm(p, g, gt, vo, c0, n0, m0, tril, triu, n_seq, seq_len):
    lc = MLSTM_CHUNK
    nc = seq_len // lc
    n_st = 2 * H_MLSTM
    n_pairs = H_MLSTM // 2
    pair_w = 2 * HEAD_DIM
    sps = SEQS_PER_STEP
    p3 = p.reshape(n_seq, seq_len, P_COLS)
    g3 = g.reshape(n_seq, seq_len, N_GATE_COLS)

    def fwd(c):
        return c

    def bwd(c):
        return nc - 1 - c

    def side(chunk):
        tok = lambda col: (lambda b, c: (b, chunk(c), col))
        specs = [
            pl.BlockSpec((sps, lc, C_W), tok(QC_BLK)),
            pl.BlockSpec((sps, lc, C_W), tok(KC_BLK)),
            pl.BlockSpec((sps, lc, N_GATE_COLS), tok(0)),
        ]
        for j in range(sps):
            lanes = lambda b, c, j=j: (0, (b * sps + j) * nc + chunk(c))
            specs += [pl.BlockSpec((N_GATE_COLS, lc), lanes), pl.BlockSpec((C_W, lc), lanes)]
        return specs

    state_specs = [
        pl.BlockSpec((sps, 2, n_pairs, pair_w, pair_w), lambda b, c: (b, 0, 0, 0, 0)),
        pl.BlockSpec((sps, 2 * n_pairs, pair_w), lambda b, c: (b, 0, 0)),
        pl.BlockSpec((sps, n_st, 1), lambda b, c: (b, 0, 0)),
    ]
    tri_spec = pl.BlockSpec((lc, lc), lambda b, c: (0, 0))
    operands = [p3, p3, g3] + [gt, vo] * sps
    hf, hb, c_out, n_out, m_out = pl.pallas_call(
        functools.partial(_mlstm_kernel, n_chunks=nc),
        grid=(n_seq // sps, nc),
        in_specs=side(fwd) + side(bwd) + state_specs + [tri_spec, tri_spec],
        out_specs=[
            pl.BlockSpec((sps, C_W, lc), lambda b, c: (b, 0, c)),
            pl.BlockSpec((sps, C_W, lc), lambda b, c: (b, 0, nc - 1 - c)),
        ] + state_specs,
        out_shape=[
            jax.ShapeDtypeStruct((n_seq, C_W, seq_len), F32),
            jax.ShapeDtypeStruct((n_seq, C_W, seq_len), F32),
            jax.ShapeDtypeStruct((n_seq, 2, n_pairs, pair_w, pair_w), F32),
            jax.ShapeDtypeStruct((n_seq, 2 * n_pairs, pair_w), F32),
            jax.ShapeDtypeStruct((n_seq, n_st, 1), F32),
        ],
        scratch_shapes=[
            pltpu.VMEM((sps, 2, n_pairs, pair_w, pair_w), F32),
            pltpu.VMEM((sps, 2 * n_pairs, pair_w), F32),
            pltpu.VMEM((sps, n_st, 1), F32),
        ],
        compiler_params=_params(("arbitrary", "arbitrary")),
        name="mlstm",
    )(*operands, *operands, _pair_states(c0), n0.reshape(n_seq, 2 * n_pairs, pair_w), m0.reshape(n_seq, n_st, 1),
      tril, triu)
    return (hf, hb, _unpair_states(c_out), n_out.reshape(n_seq, 2, H_MLSTM, HEAD_DIM),
            m_out.reshape(n_seq, 2, H_MLSTM))


def _head_norm(y, g, ones_blk):
    ysq = y * y
    hi = ysq.astype(BF16)
    lo = (ysq - hi.astype(F32)).astype(BF16)
    ss = _dot(hi, ones_blk) + _dot(lo, ones_blk)
    return y * lax.rsqrt(ss * (1.0 / HEAD_DIM) + EPS) * g


def _merge_kernel(att_ref, four_ref, hf_ref, hb_ref, oc_ref, x_ref, mod_ref, gh_ref, ghm_ref, wo_ref, g2_ref, wrt_ref,
                  ones_ref, xo_ref, h2_ref, afft_ref, *, row_tiles):
    gh = gh_ref[0]
    half = x_ref.shape[0] // 2
    parts = [slice(half * p, half * (p + 1)) for p in range(2)]
    ya = [_head_norm(att_ref[r, :], gh[:, :A_W], ones_ref[...]) for r in parts]
    four = [jnp.concatenate([four_ref[c, r, :] for c in range(B_W // LANE)], axis=1) for r in parts]
    yf = [_head_norm(f, gh[:, A_W:A_W + B_W], ones_ref[:B_W, :B_W]) for f in four]
    mem = [hf_ref[0, :, r] + hb_ref[0, :, r] for r in parts]
    ym_t = []
    for p, r in enumerate(parts):
        heads = []
        for hd in range(H_MLSTM):
            y = mem[p][HEAD_DIM * hd:HEAD_DIM * (hd + 1), :]
            heads.append(y * lax.rsqrt(jnp.mean(y * y, axis=0, keepdims=True) + EPS))
        ym_t.append(jnp.concatenate(heads, axis=0) * ghm_ref[0, :, r] * jax.nn.sigmoid(oc_ref[:, r]))
    out = [_dot(ya[p].astype(BF16), wo_ref[0, :A_W, :])
           + _dot(yf[p].astype(BF16), wo_ref[0, A_W:A_W + B_W, :])
           + _dot_tn(ym_t[p].astype(BF16), wo_ref[0, A_W + B_W:, :]) for p in range(2)]
    x = [x_ref[r, :] + mod_ref[0, 2:3, :] * out[p] for p, r in enumerate(parts)]
    y2 = [xp * lax.rsqrt(jnp.mean(xp * xp, axis=-1, keepdims=True) + EPS) * g2_ref[0] for xp in x]
    h2 = [(yp * (1.0 + mod_ref[0, 4:5, :]) + mod_ref[0, 3:4, :]).astype(BF16) for yp in y2]
    logits = [_dot_nt(wrt_ref[0], hp) for hp in h2]
    e = [jnp.exp(lg - jnp.max(lg, axis=0, keepdims=True)) for lg in logits]
    for p, r in enumerate(parts):
        xo_ref[r, :] = x[p]
        if row_tiles:
            h2_wide = h2[p].astype(F32)
            for s in range(N_SLAB):
                h2_ref[pl.ds(N_SLAB * half * p + s, half, stride=N_SLAB), :] = h2_wide[:, LANE * s:LANE * (s + 1)]
        else:
            h2_ref[r, :] = h2[p]
        afft_ref[:, r] = e[p] / jnp.sum(e[p], axis=0, keepdims=True)


def _merge(att, four, hf, hb, vo, x2d, mod, layer, g_head, g_mem, w_out_bf, g2, w_rt_bf, ones_blk, n_seq, seq_len,
           mod_seq_len, row_tiles):
    t = x2d.shape[0]
    tm = MERGE_TILE
    tiles_per_mod = mod_seq_len // tm
    tiles_per_seq = seq_len // tm
    row = lambda i: (i, 0)
    lay = lambda i: (layer, 0, 0)
    mem = lambda i: (i // tiles_per_seq, 0, i % tiles_per_seq)
    if row_tiles:
        h2_spec, h2_shape = pl.BlockSpec((N_SLAB * tm, LANE), row), jax.ShapeDtypeStruct((N_SLAB * t, LANE), F32)
    else:
        h2_spec, h2_shape = pl.BlockSpec((tm, D_MODEL), row), jax.ShapeDtypeStruct((t, D_MODEL), BF16)
    return pl.pallas_call(
        functools.partial(_merge_kernel, row_tiles=row_tiles),
        grid=(t // tm,),
        in_specs=[
            pl.BlockSpec((tm, A_W), row),
            pl.BlockSpec((B_W // LANE, tm, LANE), lambda i: (0, i, 0)),
            pl.BlockSpec((1, C_W, tm), mem),
            pl.BlockSpec((1, C_W, tm), mem),
            pl.BlockSpec((C_W, tm), lambda i: (1, i)),
            pl.BlockSpec((tm, D_MODEL), row),
            pl.BlockSpec((1, 6, D_MODEL), lambda i: (i // tiles_per_mod, 0, 0)),
            pl.BlockSpec((1, 1, D_MODEL), lay),
            pl.BlockSpec((1, C_W, tm), lay),
            pl.BlockSpec((1, D_MODEL, D_MODEL), lay),
            pl.BlockSpec((1, 1, D_MODEL), lay),
            pl.BlockSpec((1, N_EXPERTS, D_MODEL), lay),
            pl.BlockSpec((A_W, A_W), lambda i: (0, 0)),
        ],
        out_specs=[
            pl.BlockSpec((tm, D_MODEL), row),
            h2_spec,
            pl.BlockSpec((N_EXPERTS, tm), lambda i: (0, i)),
        ],
        out_shape=[
            jax.ShapeDtypeStruct((t, D_MODEL), F32),
            h2_shape,
            jax.ShapeDtypeStruct((N_EXPERTS, t), F32),
        ],
        compiler_params=_params(("arbitrary",)),
        name="merge",
    )(att, four, hf, hb, vo, x2d, mod, g_head, g_mem, w_out_bf, g2, w_rt_bf, ones_blk)


BISECT_STEPS = 48
TOKEN_CHUNK = 1024


TOKEN_SPLIT = 64


def _route_kernel(aff_ref, triu_ref, idx_ref, gs_ref, sp_ref, *, ns, seq_len, cap):
    seqs = range(ns)
    aff = [aff_ref[:, seq_len * j:seq_len * (j + 1)] for j in seqs]

    def body(_, bounds):
        out = []
        for j in seqs:
            lo, hi = bounds[j]
            mid = 0.5 * (lo + hi)
            ge = jnp.sum(jnp.where(aff[j] >= mid, 1.0, 0.0), axis=1, keepdims=True) >= cap
            out.append((jnp.where(ge, mid, lo), jnp.where(ge, hi, mid)))
        return tuple(out)

    start = (jnp.zeros((N_EXPERTS, 1), F32), jnp.full((N_EXPERTS, 1), 2.0, F32))
    bounds = lax.fori_loop(0, BISECT_STEPS, body, tuple(start for _ in seqs))
    thr = [jnp.max(jnp.where(aff[j] < bounds[j][1], aff[j], -1.0), axis=1, keepdims=True) for j in seqs]
    need = [cap - jnp.sum(jnp.where(aff[j] > thr[j], 1.0, 0.0), axis=1, keepdims=True) for j in seqs]
    triu = triu_ref[...]
    eq_carry = [jnp.zeros((N_EXPERTS, 1), F32) for _ in seqs]
    pos_carry = [jnp.zeros((N_EXPERTS, 1), F32) for _ in seqs]
    for b in range(seq_len // 128):
        for j in seqs:
            blk = aff[j][:, 128 * b:128 * (b + 1)]
            eq = blk == thr[j]
            eq_f = jnp.where(eq, 1.0, 0.0)
            eq_inc = _dot(eq_f.astype(BF16), triu) + eq_carry[j]
            sel = (blk > thr[j]) | (eq & (eq_inc - eq_f < need[j]))
            sel_f = jnp.where(sel, 1.0, 0.0)
            pos_inc = _dot(sel_f.astype(BF16), triu) + pos_carry[j]
            t0 = seq_len * j + 128 * b
            sp_ref[:, t0:t0 + 128] = jnp.where(sel, pos_inc - sel_f, -1.0).astype(jnp.int32)
            eq_carry[j] = eq_inc[:, 127:128]
            pos_carry[j] = pos_inc[:, 127:128]

    tc = min(seq_len, TOKEN_CHUNK)
    slot = lax.broadcasted_iota(jnp.int32, (cap, tc), 0)
    part = lax.broadcasted_iota(jnp.int32, (8, tc), 0)
    tok = lax.broadcasted_iota(jnp.int32, (1, tc), 1).astype(F32)
    chunks = range(0, seq_len, tc)
    tok_hi = [jnp.floor((tok + float(t0)) * (1.0 / TOKEN_SPLIT)) for t0 in chunks]
    tok_lo = [tok + float(t0) - TOKEN_SPLIT * hi for t0, hi in zip(chunks, tok_hi)]

    def per_expert(e, carry):
        for j in seqs:
            acc = jnp.zeros((cap, 8), F32)
            for ci, t0 in enumerate(chunks):
                cols = slice(seq_len * j + t0, seq_len * j + t0 + tc)
                onehot = jnp.where(slot == sp_ref[pl.ds(e, 1), cols], 1.0, 0.0).astype(BF16)
                a = aff_ref[pl.ds(e, 1), cols]
                a_hi = a.astype(BF16).astype(F32)
                a_mid = (a - a_hi).astype(BF16).astype(F32)
                a_lo = a - a_hi - a_mid
                vals = jnp.where(part == 0, tok_hi[ci], jnp.where(part == 1, tok_lo[ci], jnp.where(
                    part == 2, a_hi, jnp.where(part == 3, a_mid, jnp.where(part == 4, a_lo, 0.0)))))
                acc = acc + _dot_nt(onehot, vals.astype(BF16))
            idx_ref[N_EXPERTS * j + e] = (N_SLAB * (TOKEN_SPLIT * acc[:, 0:1] + acc[:, 1:2])).astype(jnp.int32)
            gs_ref[e, cap * j:cap * (j + 1)] = acc[:, 2:3] + acc[:, 3:4] + acc[:, 4:5]
        return carry

    lax.fori_loop(0, N_EXPERTS, per_expert, 0)


def _route(afft, triu_bf, n_seq, seq_len, cap, ns):
    idx, gs, sp = pl.pallas_call(
        functools.partial(_route_kernel, ns=ns, seq_len=seq_len, cap=cap),
        grid=(n_seq // ns,),
        in_specs=[
            pl.BlockSpec((N_EXPERTS, ns * seq_len), lambda s: (0, s)),
            pl.BlockSpec((128, 128), lambda s: (0, 0)),
        ],
        out_specs=[
            pl.BlockSpec((ns * N_EXPERTS, cap, 1), lambda s: (s, 0, 0)),
            pl.BlockSpec((N_EXPERTS, ns * cap, 1), lambda s: (0, s, 0)),
            pl.BlockSpec((N_EXPERTS, ns * seq_len), lambda s: (0, s)),
        ],
        out_shape=[
            jax.ShapeDtypeStruct((n_seq * N_EXPERTS, cap, 1), jnp.int32),
            jax.ShapeDtypeStruct((N_EXPERTS, n_seq * cap, 1), F32),
            jax.ShapeDtypeStruct((N_EXPERTS, n_seq * seq_len), jnp.int32),
        ],
        compiler_params=_params(("arbitrary",)),
        name="route",
    )(afft, triu_bf)
    return idx.reshape(n_seq * N_EXPERTS * cap), gs, sp


ROW_COPIES = 8


def _row_tile(first_row):
    return pl.ds(pl.multiple_of(first_row, N_SLAB), N_SLAB)


def _gather_kernel(idx_ref, src_ref, xs_ref, tile_ref, *, eb, cap):
    ei = pl.program_id(1)

    def per_expert(ee, carry):
        e = ei * eb + ee

        def rows(g, c):
            slot0 = g * ROW_COPIES
            for u in range(ROW_COPIES):
                tile_ref[_row_tile((slot0 + u) * N_SLAB), :] = src_ref[_row_tile(idx_ref[e * cap + slot0 + u]), :]
            return c

        lax.fori_loop(0, cap // ROW_COPIES, rows, 0)
        for s in range(N_SLAB):
            xs_ref[ee, :, LANE * s:LANE * (s + 1)] = tile_ref[pl.ds(s, cap, stride=N_SLAB), :].astype(BF16)
        return carry

    lax.fori_loop(0, eb, per_expert, 0)


def _gather(idx, h2_rows, n_seq, cap, eb):
    return pl.pallas_call(
        functools.partial(_gather_kernel, eb=eb, cap=cap),
        grid=(n_seq, N_EXPERTS // eb),
        in_specs=[
            pl.BlockSpec((N_EXPERTS * cap,), lambda s, e: (s,), memory_space=pltpu.SMEM),
            pl.BlockSpec((h2_rows.shape[0] // n_seq, LANE), lambda s, e: (s, 0)),
        ],
        out_specs=pl.BlockSpec((eb, cap, D_MODEL), lambda s, e: (e, s, 0)),
        out_shape=jax.ShapeDtypeStruct((N_EXPERTS, n_seq * cap, D_MODEL), BF16),
        scratch_shapes=[pltpu.VMEM((N_SLAB * cap, LANE), F32)],
        compiler_params=_params(("arbitrary", "arbitrary"), vmem_mib=56),
        name="gather",
    )(idx, h2_rows)


def _selection(sp_ref, cap):
    seq_len = sp_ref.shape[1]
    slot = lax.broadcasted_iota(jnp.int32, (cap, seq_len), 0)
    return jnp.concatenate([jnp.where(slot == sp_ref[e:e + 1, :], 1.0, 0.0) for e in range(N_EXPERTS)],
                           axis=0).astype(BF16)


def _gather_short_kernel(sp_ref, h_ref, xs_ref, *, cap):
    rows = _dot(_selection(sp_ref, cap), h_ref[...])
    for e in range(N_EXPERTS):
        xs_ref[e] = rows[cap * e:cap * (e + 1), :].astype(BF16)


def _gather_short(sp, h2, n_seq, seq_len, cap):
    return pl.pallas_call(
        functools.partial(_gather_short_kernel, cap=cap),
        grid=(n_seq,),
        in_specs=[
            pl.BlockSpec((N_EXPERTS, seq_len), lambda s: (0, s)),
            pl.BlockSpec((seq_len, D_MODEL), lambda s: (s, 0)),
        ],
        out_specs=pl.BlockSpec((N_EXPERTS, cap, D_MODEL), lambda s: (0, s, 0)),
        out_shape=jax.ShapeDtypeStruct((N_EXPERTS, n_seq * cap, D_MODEL), BF16),
        compiler_params=_params(("arbitrary",)),
        name="gather_short",
    )(sp, h2)


def _scatter_short_kernel(sp_ref, y_ref, x_ref, mod_ref, gf_ref, o_ref, *, cap, final):
    sel = _selection(sp_ref, cap)
    y = jnp.concatenate([y_ref[e] for e in range(N_EXPERTS)], axis=0)
    y_hi = y.astype(BF16)
    y_lo = (y - y_hi.astype(F32)).astype(BF16)
    moe = _dot_tn(sel, y_hi) + _dot_tn(sel, y_lo)
    x = x_ref[...] + mod_ref[0, 5:6, :] * moe
    if final:
        x = x * lax.rsqrt(jnp.mean(x * x, axis=-1, keepdims=True) + EPS) * gf_ref[...]
    o_ref[...] = x


def _scatter_short(sp, ys, x2d, mod, g_final, n_seq, seq_len, cap, final):
    return pl.pallas_call(
        functools.partial(_scatter_short_kernel, cap=cap, final=final),
        grid=(n_seq,),
        in_specs=[
            pl.BlockSpec((N_EXPERTS, seq_len), lambda s: (0, s)),
            pl.BlockSpec((N_EXPERTS, cap, D_MODEL), lambda s: (0, s, 0)),
            pl.BlockSpec((seq_len, D_MODEL), lambda s: (s, 0)),
            pl.BlockSpec((1, 6, D_MODEL), lambda s: (0, 0, 0)),
            pl.BlockSpec((1, D_MODEL), lambda s: (0, 0)),
        ],
        out_specs=pl.BlockSpec((seq_len, D_MODEL), lambda s: (s, 0)),
        out_shape=jax.ShapeDtypeStruct((n_seq * seq_len, D_MODEL), F32),
        compiler_params=_params(("arbitrary",)),
        name="scatter_short",
    )(sp, ys, x2d, mod, g_final)


def _expert_kernel(xc_ref, xl_ref, gc_ref, gl_ref, wg_ref, wu_ref, wd_ref, yc_ref, yl_ref, *, n_f):
    f = pl.program_id(1)

    @pl.when(f == 0)
    def _():
        yc_ref[...] = jnp.zeros_like(yc_ref)
        yl_ref[...] = jnp.zeros_like(yl_ref)

    wg = wg_ref[0, 0].astype(BF16)
    wu = wu_ref[0, 0].astype(BF16)
    wd = wd_ref[0, 0].astype(BF16)
    for x_ref, y_ref in ((xc_ref, yc_ref), (xl_ref, yl_ref)):
        x = x_ref[0]
        mid = (_silu(_dot(x, wg)) * _dot(x, wu)).astype(BF16)
        y_ref[0] += _dot(mid, wd)

    @pl.when(f == n_f - 1)
    def _():
        yc_ref[0] = yc_ref[0] * gc_ref[0]
        yl_ref[0] = yl_ref[0] * gl_ref[0]


def _experts(xs_c, xs_l, gs_c, gs_l, w_g, w_u, w_d, layer):
    rc, rl = xs_c.shape[1], xs_l.shape[1]
    tf = 1024
    n_f = EXPERT_FF // tf
    return pl.pallas_call(
        functools.partial(_expert_kernel, n_f=n_f),
        grid=(N_EXPERTS, n_f),
        in_specs=[
            pl.BlockSpec((1, rc, D_MODEL), lambda e, f: (e, 0, 0)),
            pl.BlockSpec((1, rl, D_MODEL), lambda e, f: (e, 0, 0)),
            pl.BlockSpec((1, rc, 1), lambda e, f: (e, 0, 0)),
            pl.BlockSpec((1, rl, 1), lambda e, f: (e, 0, 0)),
            pl.BlockSpec((1, 1, D_MODEL, tf), lambda e, f: (layer, e, 0, f)),
            pl.BlockSpec((1, 1, D_MODEL, tf), lambda e, f: (layer, e, 0, f)),
            pl.BlockSpec((1, 1, tf, D_MODEL), lambda e, f: (layer, e, f, 0)),
        ],
        out_specs=[
            pl.BlockSpec((1, rc, D_MODEL), lambda e, f: (e, 0, 0)),
            pl.BlockSpec((1, rl, D_MODEL), lambda e, f: (e, 0, 0)),
        ],
        out_shape=[
            jax.ShapeDtypeStruct((N_EXPERTS, rc, D_MODEL), F32),
            jax.ShapeDtypeStruct((N_EXPERTS, rl, D_MODEL), F32),
        ],
        compiler_params=_params(("arbitrary", "arbitrary"), vmem_mib=56),
        name="experts",
    )(xs_c, xs_l, gs_c, gs_l, w_g, w_u, w_d)


def _scatter_kernel(idx_ref, y_ref, x_ref, mod_ref, gf_ref, o_ref, acc_ref, tile_ref, *, eb, n_e, cap, tm, final):
    step = pl.program_id(1)

    @pl.when(step == 0)
    def _():
        acc_ref[...] = jnp.zeros_like(acc_ref)

    @pl.when(step < n_e)
    def _():
        def per_expert(ee, carry):
            e = step * eb + ee
            for s in range(N_SLAB):
                tile_ref[pl.ds(s, cap, stride=N_SLAB), :] = y_ref[ee, :, LANE * s:LANE * (s + 1)]

            def rows(g, c):
                slot0 = g * ROW_COPIES
                dst = [idx_ref[e * cap + slot0 + u] for u in range(ROW_COPIES)]
                new = [acc_ref[_row_tile(dst[u]), :] + tile_ref[_row_tile((slot0 + u) * N_SLAB), :]
                       for u in range(ROW_COPIES)]
                for u in range(ROW_COPIES):
                    acc_ref[_row_tile(dst[u]), :] = new[u]
                return c

            lax.fori_loop(0, cap // ROW_COPIES, rows, 0)
            return carry

        lax.fori_loop(0, eb, per_expert, 0)

    @pl.when(step >= n_e)
    def _():
        base = pl.multiple_of((step - n_e) * tm * N_SLAB, N_SLAB)
        moe = jnp.concatenate([acc_ref[pl.ds(base + s, tm, stride=N_SLAB), :] for s in range(N_SLAB)], axis=1)
        x = x_ref[...] + mod_ref[0, 5:6, :] * moe
        if final:
            x = x * lax.rsqrt(jnp.mean(x * x, axis=-1, keepdims=True) + EPS) * gf_ref[...]
        o_ref[...] = x


def _scatter(idx, ys, x2d, mod, g_final, n_seq, seq_len, cap, eb, tm, final):
    n_e = N_EXPERTS // eb
    n_out = seq_len // tm
    out_blk = lambda s, j: (s * n_out + jnp.maximum(j - n_e, 0), 0)
    return pl.pallas_call(
        functools.partial(_scatter_kernel, eb=eb, n_e=n_e, cap=cap, tm=tm, final=final),
        grid=(n_seq, n_e + n_out),
        in_specs=[
            pl.BlockSpec((N_EXPERTS * cap,), lambda s, j: (s,), memory_space=pltpu.SMEM),
            pl.BlockSpec((eb, cap, D_MODEL), lambda s, j: (jnp.minimum(j, n_e - 1), s, 0)),
            pl.BlockSpec((tm, D_MODEL), out_blk),
            pl.BlockSpec((1, 6, D_MODEL), lambda s, j: (s, 0, 0)),
            pl.BlockSpec((1, D_MODEL), lambda s, j: (0, 0)),
        ],
        out_specs=pl.BlockSpec((tm, D_MODEL), out_blk),
        out_shape=jax.ShapeDtypeStruct((n_seq * seq_len, D_MODEL), F32),
        scratch_shapes=[
            pltpu.VMEM((N_SLAB * seq_len, LANE), F32),
            pltpu.VMEM((N_SLAB * cap, LANE), F32),
        ],
        compiler_params=_params(("arbitrary", "arbitrary"), vmem_mib=56),
        name="scatter",
    )(idx, ys, x2d, mod, g_final)


def kernel(x_prompt, x_sample, c, cache_k, cache_v, state_C, state_n, state_m, c_ctx, w_ada, b_ada, g_norm1, g_norm2, w_in, b_gates, rpb, w_fourier, g_head, w_out, w_router, w_exp_gate, w_exp_up, w_exp_down, g_final):
    n_ctx, len_ctx, _ = x_prompt.shape
    n_lat, len_lat, _ = x_sample.shape
    past = cache_k.shape[2]
    cap_ctx = CAPACITY_FACTOR * len_ctx // N_EXPERTS
    cap_lat = CAPACITY_FACTOR * len_lat // N_EXPERTS

    w_in_bf = w_in.astype(BF16)
    w_gt_bf = jnp.swapaxes(w_in[:, :, P_COLS:], 1, 2).astype(BF16)
    vc0, oc0 = VC_BLK * C_W, OC_BLK * C_W
    w_vo_bf = jnp.swapaxes(jnp.concatenate([w_in[:, :, vc0:vc0 + C_W], w_in[:, :, oc0:oc0 + C_W]], axis=2),
                           1, 2).astype(BF16)
    bg_row = b_gates.reshape(DEPTH, 1, N_GATE_COLS).astype(F32)
    bg_col = b_gates.reshape(DEPTH, N_GATE_COLS, 1).astype(F32)
    w_out_bf = w_out.astype(BF16)
    w_rt_bf = jnp.swapaxes(w_router, 1, 2).astype(BF16)
    g1 = g_norm1.reshape(DEPTH, 1, D_MODEL)
    g2 = g_norm2.reshape(DEPTH, 1, D_MODEL)
    gh = g_head.reshape(DEPTH, 1, D_MODEL)
    g_mem = jnp.broadcast_to(gh[:, 0, A_W + B_W:, None], (DEPTH, C_W, MERGE_TILE))
    eye_g = jnp.eye(G_FOURIER, dtype=F32)
    wf_blk = jnp.einsum("lgcd,gh->lgchd", w_fourier, eye_g).reshape(DEPTH, B_W, B_W).astype(BF16)

    csc = _channel_dft()
    dft_ctx = _dft_mats(len_ctx)
    dft_side = _dft_mats(GRID_W)
    tw_cos, tw_sin = _twiddles(GRID_W, len_lat)
    bias_tiles = _nbr_bias_tiles(rpb)
    r = np.arange(MLSTM_CHUNK)
    tril = jnp.asarray(r[:, None] >= r[None, :], F32)
    triu = jnp.asarray(r[:, None] <= r[None, :], F32)
    triu_bf = triu.astype(BF16)
    hidx = np.arange(A_W) // HEAD_DIM
    ones_blk = jnp.asarray(hidx[:, None] == hidx[None, :], BF16)

    cvecs = jnp.concatenate([c_ctx[None, :], c, jnp.zeros((8 - 1 - n_lat, D_MODEL), F32)], axis=0)
    mod_all = _modulation(cvecs, w_ada, b_ada).reshape(DEPTH, 8, 6, D_MODEL)

    cache_k4 = cache_k.reshape(n_lat, DEPTH, past, A_W)
    cache_v4 = cache_v.reshape(n_lat, DEPTH, past, A_W)
    zero_c = jnp.zeros((n_ctx, 2, H_MLSTM, HEAD_DIM, HEAD_DIM), F32)
    zero_n = jnp.zeros((n_ctx, 2, H_MLSTM, HEAD_DIM), F32)
    zero_m = jnp.zeros((n_ctx, 2, H_MLSTM), F32)

    xc = x_prompt.reshape(n_ctx * len_ctx, D_MODEL)
    xl = x_sample.reshape(n_lat * len_lat, D_MODEL)
    gf = g_final.reshape(1, D_MODEL)
    new_k = jnp.zeros((n_ctx, DEPTH, len_ctx, A_W), F32)
    new_v = jnp.zeros((n_ctx, DEPTH, len_ctx, A_W), F32)
    cs, ns, ms = [], [], []
    for l in range(DEPTH):
        mod_c = mod_all[l, 0:1]
        mod_l = mod_all[l, 1:1 + n_lat]

        pc, gc, gtc, voc, abc, new_k, new_v = _inproj(xc, mod_c, l, g1, w_in_bf, w_gt_bf, w_vo_bf, bg_row, bg_col, csc,
                                                      n_ctx * len_ctx, False, (new_k, new_v))
        att_c = _ctx_attention(pc, n_ctx, len_ctx)
        four_c = _fourier(abc, dft_ctx[0], dft_ctx[1], wf_blk, l, n_ctx, len_ctx)
        hf_c, hb_c, c_new, n_new, m_new = _mlstm(pc, gc, gtc, voc, zero_c, zero_n, zero_m, tril, triu, n_ctx, len_ctx)
        xc, h2c, affc = _merge(att_c, four_c, hf_c, hb_c, voc, xc, mod_c, l, gh, g_mem, w_out_bf, g2, w_rt_bf,
                               ones_blk, n_ctx, len_ctx, n_ctx * len_ctx, False)
        cs.append(c_new)
        ns.append(n_new)
        ms.append(m_new)

        pq, gq, gtq, voq, abq = _inproj(xl, mod_l, l, g1, w_in_bf, w_gt_bf, w_vo_bf, bg_row, bg_col, csc, len_lat,
                                        True)
        att_l = _nbr_attention(pq, cache_k4, cache_v4, bias_tiles, l, n_lat, len_lat)
        four_l = _fourier_grid(abq, dft_side[0], dft_side[1], tw_cos, tw_sin, wf_blk, l, n_lat, GRID_W)
        hf_l, hb_l, _, _, _ = _mlstm(pq, gq, gtq, voq, state_C[:, l], state_n[:, l], state_m[:, l], tril, triu,
                                     n_lat, len_lat)
        xl, h2l, affl = _merge(att_l, four_l, hf_l, hb_l, voq, xl, mod_l, l, gh, g_mem, w_out_bf, g2, w_rt_bf,
                               ones_blk, n_lat, len_lat, len_lat, True)

        last = l == DEPTH - 1
        _, gs_c, sp_c = _route(affc, triu_bf, n_ctx, len_ctx, cap_ctx, n_ctx)
        idx_l, gs_l, _ = _route(affl, triu_bf, n_lat, len_lat, cap_lat, 1)
        xs_c = _gather_short(sp_c, h2c, n_ctx, len_ctx, cap_ctx)
        xs_l = _gather(idx_l, h2l, n_lat, cap_lat, 4)
        ys_c, ys_l = _experts(xs_c, xs_l, gs_c, gs_l, w_exp_gate, w_exp_up, w_exp_down, l)
        xc = _scatter_short(sp_c, ys_c, xc, mod_c, gf, n_ctx, len_ctx, cap_ctx, last)
        xl = _scatter(idx_l, ys_l, xl, mod_l, gf, n_lat, len_lat, cap_lat, 2, 512, last)

    y_prompt = xc.reshape(n_ctx, len_ctx, D_MODEL)
    y_sample = xl.reshape(n_lat, len_lat, D_MODEL)
    kv_shape = (n_ctx, DEPTH, len_ctx, H_ATT, HEAD_DIM)
    return (y_prompt, y_sample, new_k.reshape(kv_shape), new_v.reshape(kv_shape), jnp.stack(cs, axis=1),
            jnp.stack(ns, axis=1), jnp.stack(ms, axis=1))
```

```python
import functools

import numpy as np
import jax
import jax.numpy as jnp
from jax import lax
from jax.experimental import pallas as pl
from jax.experimental.pallas import tpu as pltpu

F32 = jnp.float32
BF16 = jnp.bfloat16

D_MODEL = 1024
DEPTH = 2
HEAD_DIM = 64
H_ATT = 8
G_FOURIER = 4
H_MLSTM = 4
A_W = H_ATT * HEAD_DIM
B_W = G_FOURIER * HEAD_DIM
C_W = H_MLSTM * HEAD_DIM
N_GATE_COLS = 16
P_COLS = 3 * A_W + B_W + 4 * C_W
IN_COLS = P_COLS + N_GATE_COLS
P_KEEP = 3 * A_W + B_W + 2 * C_W
GRID_W = 64
WIN_R = 8
WIN_C = 16
MLSTM_CHUNK = 128
N_EXPERTS = 16
CAPACITY_FACTOR = 2
EXPERT_FF = 2 * D_MODEL
EPS = 1e-6
NEG = -1e30

UB_OFF = 3 * A_W
QC_BLK, KC_BLK, VC_BLK, OC_BLK = 7, 8, 9, 10

LANE = 128
N_SLAB = D_MODEL // LANE
MERGE_TILE = 256
INPROJ_TILE = 512

NT_DIMS = (((1,), (1,)), ((), ()))
TN_DIMS = (((0,), (0,)), ((), ()))
MIB = 1024 * 1024


def _dot(a, b, precision=None):
    return jnp.dot(a, b, preferred_element_type=F32, precision=precision)


def _dot_nt(a, b, precision=None):
    return lax.dot_general(a, b, NT_DIMS, preferred_element_type=F32, precision=precision)


def _dot_tn(a, b):
    return lax.dot_general(a, b, TN_DIMS, preferred_element_type=F32)


def _params(sem, vmem_mib=48):
    return pltpu.CompilerParams(dimension_semantics=sem, vmem_limit_bytes=vmem_mib * MIB)


def _silu(x):
    return x * jax.nn.sigmoid(x)


def _log_sigmoid(x):
    return jnp.minimum(x, 0.0) - jnp.log1p(jnp.exp(-jnp.abs(x)))


def _mod_kernel(c_ref, w_ref, b_ref, o_ref):
    s = _silu(c_ref[...]).astype(BF16)
    o_ref[0] = _dot(s, w_ref[0].astype(BF16)) + b_ref[0]


def _modulation(cvecs, w_ada, b_ada):
    depth = w_ada.shape[0]
    tn = 1024
    return pl.pallas_call(
        _mod_kernel,
        grid=(depth, 6 * D_MODEL // tn),
        in_specs=[
            pl.BlockSpec((8, D_MODEL), lambda l, j: (0, 0)),
            pl.BlockSpec((1, D_MODEL, tn), lambda l, j: (l, 0, j)),
            pl.BlockSpec((1, 1, tn), lambda l, j: (l, 0, j)),
        ],
        out_specs=pl.BlockSpec((1, 8, tn), lambda l, j: (l, 0, j)),
        out_shape=jax.ShapeDtypeStruct((depth, 8, 6 * D_MODEL), F32),
        compiler_params=_params(("arbitrary", "arbitrary")),
        name="modulation",
    )(cvecs, w_ada, b_ada.reshape(depth, 1, 6 * D_MODEL))


def _inproj_kernel(x_ref, mod_ref, g1_ref, w_ref, wgt_ref, wvo_ref, bgr_ref, bgc_ref, csc_ref,
                   *rest, grid_rows, kv_seq_len):
    n_in = 2 if kv_seq_len else 0
    p_ref, g_ref, gt_ref, vo_ref, ab_ref = rest[n_in:n_in + 5]
    kv_refs = rest[n_in + 5:n_in + 5 + n_in]
    scratch = rest[n_in + 5 + n_in:]
    x = x_ref[...]
    y = x * lax.rsqrt(jnp.mean(x * x, axis=-1, keepdims=True) + EPS) * g1_ref[0]
    h = (y * (1.0 + mod_ref[0, 1:2, :]) + mod_ref[0, 0:1, :]).astype(BF16)
    for j in range(0, P_KEEP, 256):
        pj = _dot(h, w_ref[0, :, j:j + 256])
        p_ref[:, j:j + 256] = pj.astype(BF16)
        if kv_seq_len and A_W <= j < 3 * A_W:
            kv_ref = kv_refs[(j - A_W) // A_W]
            c0 = (j - A_W) % A_W
            for b in range(pj.shape[0] // kv_seq_len):
                kv_ref[b, 0, :, c0:c0 + 256] = pj[kv_seq_len * b:kv_seq_len * (b + 1), :]
        if j == UB_OFF:
            ab = _dot(pj.astype(BF16), csc_ref[...])
            if grid_rows:
                stage_ref, = scratch
                n_lt = 2 * B_W // LANE
                for c in range(n_lt):
                    stage_ref[c] = ab[:, LANE * c:LANE * (c + 1)]
                for n2 in range(GRID_W):
                    for c in range(n_lt):
                        col = 2 * B_W * n2 + LANE * c
                        ab_ref[:, col:col + LANE] = stage_ref[c, pl.ds(n2, grid_rows, stride=GRID_W), :]
            else:
                ab_ref[...] = ab.astype(BF16)
    g_ref[...] = _dot(h, w_ref[0, :, P_COLS:IN_COLS]) + bgr_ref[0]
    gt_ref[...] = _dot_nt(wgt_ref[0], h) + bgc_ref[0]
    vo_ref[...] = _dot_nt(wvo_ref[0], h)


def _inproj(x2d, mod, layer, g1, w_in_bf, w_gt_bf, w_vo_bf, bg_row, bg_col, csc, seq_len, grid_ab, kv_cache=None):
    t = x2d.shape[0]
    tm = INPROJ_TILE
    tiles_per_seq = seq_len // tm
    grid_rows = tm // GRID_W if grid_ab else 0
    if grid_ab:
        ab_spec = pl.BlockSpec((grid_rows, GRID_W * 2 * B_W), lambda i: (i, 0))
        ab_shape = jax.ShapeDtypeStruct((t // GRID_W, GRID_W * 2 * B_W), F32)
        scratch = [pltpu.VMEM((2 * B_W // LANE, tm, LANE), F32)]
    else:
        ab_spec = pl.BlockSpec((tm, 2 * B_W), lambda i: (i, 0))
        ab_shape = jax.ShapeDtypeStruct((t, 2 * B_W), BF16)
        scratch = []
    kv_seq_len = kv_cache[0].shape[2] if kv_cache else 0
    kv_in_specs, kv_out_specs, kv_shapes, aliases = [], [], [], {}
    if kv_cache:
        per_tile = tm // kv_seq_len
        kv_in_specs = [pl.BlockSpec(memory_space=pl.ANY)] * 2
        kv_out_specs = [pl.BlockSpec((per_tile, 1, kv_seq_len, A_W), lambda i: (i, layer, 0, 0))] * 2
        kv_shapes = [jax.ShapeDtypeStruct(a.shape, a.dtype) for a in kv_cache]
        aliases = {9: 5, 10: 6}
    return pl.pallas_call(
        functools.partial(_inproj_kernel, grid_rows=grid_rows, kv_seq_len=kv_seq_len),
        grid=(t // tm,),
        input_output_aliases=aliases,
        in_specs=[
            pl.BlockSpec((tm, D_MODEL), lambda i: (i, 0)),
            pl.BlockSpec((1, 6, D_MODEL), lambda i: (i // tiles_per_seq, 0, 0)),
            pl.BlockSpec((1, 1, D_MODEL), lambda i: (layer, 0, 0)),
            pl.BlockSpec((1, D_MODEL, IN_COLS), lambda i: (layer, 0, 0)),
            pl.BlockSpec((1, N_GATE_COLS, D_MODEL), lambda i: (layer, 0, 0)),
            pl.BlockSpec((1, 2 * C_W, D_MODEL), lambda i: (layer, 0, 0)),
            pl.BlockSpec((1, 1, N_GATE_COLS), lambda i: (layer, 0, 0)),
            pl.BlockSpec((1, N_GATE_COLS, 1), lambda i: (layer, 0, 0)),
            pl.BlockSpec((B_W, 2 * B_W), lambda i: (0, 0)),
        ] + kv_in_specs,
        out_specs=[
            pl.BlockSpec((tm, P_KEEP), lambda i: (i, 0)),
            pl.BlockSpec((tm, N_GATE_COLS), lambda i: (i, 0)),
            pl.BlockSpec((N_GATE_COLS, tm), lambda i: (0, i)),
            pl.BlockSpec((2 * C_W, tm), lambda i: (0, i)),
            ab_spec,
        ] + kv_out_specs,
        out_shape=[
            jax.ShapeDtypeStruct((t, P_KEEP), BF16),
            jax.ShapeDtypeStruct((t, N_GATE_COLS), F32),
            jax.ShapeDtypeStruct((N_GATE_COLS, t), F32),
            jax.ShapeDtypeStruct((2 * C_W, t), F32),
            ab_shape,
        ] + kv_shapes,
        scratch_shapes=scratch,
        compiler_params=_params(("arbitrary",)),
        name="inproj",
    )(x2d, mod, g1, w_in_bf, w_gt_bf, w_vo_bf, bg_row, bg_col, csc, *(kv_cache or ()))


def _ctx_attn_kernel(q_ref, k_ref, v_ref, o_ref):
    scale = HEAD_DIM ** -0.5
    heads = range(H_ATT)
    sl = [slice(HEAD_DIM * h, HEAD_DIM * (h + 1)) for h in heads]
    s = [_dot_nt((q_ref[:, sl[h]] * scale).astype(BF16), k_ref[:, sl[h]].astype(BF16)) for h in heads]
    e = [jnp.exp(s[h] - jnp.max(s[h], axis=-1, keepdims=True)) for h in heads]
    w = [e[h] * (1.0 / jnp.sum(e[h], axis=-1, keepdims=True)) for h in heads]
    o_ref[...] = jnp.concatenate([_dot(w[h].astype(BF16), v_ref[:, sl[h]].astype(BF16)) for h in heads], axis=1)


def _ctx_attention(p, n_seq, seq_len):
    return pl.pallas_call(
        _ctx_attn_kernel,
        grid=(n_seq,),
        in_specs=[
            pl.BlockSpec((seq_len, A_W), lambda b: (b, 0)),
            pl.BlockSpec((seq_len, A_W), lambda b: (b, 1)),
            pl.BlockSpec((seq_len, A_W), lambda b: (b, 2)),
        ],
        out_specs=pl.BlockSpec((seq_len, A_W), lambda b: (b, 0)),
        out_shape=jax.ShapeDtypeStruct((n_seq * seq_len, A_W), F32),
        compiler_params=_params(("arbitrary",)),
        name="ctx_attention",
    )(p, p, p)


Q_ROWS = 8
K_ROWS = 16
KEY_BLK = 256


NO_ROW = 2 * WIN_R - 1


def _nbr_bias_tiles(rpb):
    n_c = 2 * WIN_C - 1
    cq = np.arange(GRID_W)[:, None]
    ck = np.arange(GRID_W)[None, :]
    cs = np.clip(cq - WIN_C // 2, 0, GRID_W - WIN_C)
    col_ok = (ck >= cs) & (ck < cs + WIN_C)
    pick = np.where(col_ok, np.clip(ck - cq + WIN_C - 1, 0, n_c - 1), n_c)
    sel = np.zeros((2, GRID_W, 2 * GRID_W, n_c + 1), np.float32)
    for side in range(2):
        sel[side, cq, side * GRID_W + ck, pick] = 1.0
    ext = jnp.full(rpb.shape[:2] + (2 * WIN_R, n_c + 1), NEG, F32)
    ext = ext.at[:, :, :2 * WIN_R - 1, :n_c].set(rpb.astype(F32))
    return jnp.einsum("lhdm,sqkm->lhdsqk", ext, jnp.asarray(sel), precision=lax.Precision.HIGHEST)


def _nbr_attn_kernel(q_ref, k0_ref, k1_ref, k2_ref, k3_ref, v0_ref, v1_ref, v2_ref, v3_ref,
                     ck_ref, cv_ref, tab_ref, o_ref):
    scale = HEAD_DIM ** -0.5
    k_refs = (k0_ref, k1_ref, k2_ref, k3_ref)
    v_refs = (v0_ref, v1_ref, v2_ref, v3_ref)
    rb = pl.program_id(1)
    rows = GRID_W

    def window_offsets():
        key_row0 = jnp.clip(Q_ROWS * rb - WIN_R // 2, 0, rows - K_ROWS)
        d = []
        for rq in range(Q_ROWS):
            r = Q_ROWS * rb + rq
            rs = jnp.clip(r - WIN_R // 2, 0, rows - WIN_R)
            d_row = []
            for rk in range(K_ROWS):
                rka = key_row0 + rk
                ok = (rka >= rs) & (rka < rs + WIN_R)
                d_row.append(jnp.where(ok, rka - r + WIN_R - 1, NO_ROW))
            d.append(d_row)
        return d

    d_interior = [[rk - rq + WIN_R // 2 - 1 if 0 <= rk - rq < WIN_R else NO_ROW for rk in range(K_ROWS)]
                  for rq in range(Q_ROWS)]

    rows_per_blk = KEY_BLK // GRID_W
    n_blk = K_ROWS // rows_per_blk
    half_rows = Q_ROWS // 2
    half_tok = half_rows * GRID_W
    n_rb = pl.num_programs(1)

    def attend(active, d):
        units = [(hh, half) for hh in range(2) for half in range(2)]
        sl = [slice(HEAD_DIM * hh, HEAD_DIM * (hh + 1)) for hh in range(2)]
        rows_of = [slice(half_tok * half, half_tok * (half + 1)) for half in range(2)]
        q = [(q_ref[rows_of[half], sl[hh]] * scale).astype(BF16) for hh, half in units]
        s_ctx = [_dot_nt(q[u], ck_ref[0, 0, :, sl[hh]].astype(BF16)) for u, (hh, half) in enumerate(units)]
        s_loc = []
        for u, (hh, half) in enumerate(units):
            blocks = []
            for j in active[half]:
                bias = jnp.concatenate([
                    jnp.concatenate([
                        tab_ref[0, hh, d[rq][rows_per_blk * j + 2 * p], 0]
                        + tab_ref[0, hh, d[rq][rows_per_blk * j + 2 * p + 1], 1]
                        for p in range(rows_per_blk // 2)], axis=1)
                    for rq in range(half_rows * half, half_rows * (half + 1))], axis=0)
                blocks.append(_dot_nt(q[u], k_refs[j][:, sl[hh]].astype(BF16)) + bias)
            s_loc.append(blocks)
        m = [jnp.max(s_ctx[u], axis=-1, keepdims=True) for u in range(len(units))]
        for u in range(len(units)):
            for s_blk in s_loc[u]:
                m[u] = jnp.maximum(m[u], jnp.max(s_blk, axis=-1, keepdims=True))
        e_ctx = [jnp.exp(s_ctx[u] - m[u]) for u in range(len(units))]
        den = [jnp.sum(e_ctx[u], axis=-1, keepdims=True) for u in range(len(units))]
        num = [_dot(e_ctx[u].astype(BF16), cv_ref[0, 0, :, sl[hh]].astype(BF16)) for u, (hh, half) in enumerate(units)]
        for u, (hh, half) in enumerate(units):
            for j, s_blk in zip(active[half], s_loc[u]):
                e = jnp.exp(s_blk - m[u])
                den[u] = den[u] + jnp.sum(e, axis=-1, keepdims=True)
                num[u] = num[u] + _dot(e.astype(BF16), v_refs[j][:, sl[hh]].astype(BF16))
        out = [num[u] / den[u] for u in range(len(units))]
        o_ref[...] = jnp.concatenate([jnp.concatenate([out[2 * hh], out[2 * hh + 1]], axis=0) for hh in range(2)],
                                     axis=1)

    interior = (rb > 0) & (rb < n_rb - 1)
    every = tuple(range(n_blk))

    @pl.when(interior)
    def _():
        attend((every[:-1], every[1:]), d_interior)

    @pl.when(jnp.logical_not(interior))
    def _():
        attend((every, every), window_offsets())


def _nbr_attention(p, cache_k4, cache_v4, bias_tiles, layer, n_seq, seq_len):
    q_tok = Q_ROWS * GRID_W
    n_rb = seq_len // q_tok
    kb_per_seq = seq_len // KEY_BLK
    max_base = kb_per_seq - 4

    def kmap(j, col0):
        def f(hp, rb, b):
            base = jnp.clip(2 * rb - 1, 0, max_base)
            return (b * kb_per_seq + base + j, col0 + hp)
        return f

    past = cache_k4.shape[2]
    in_specs = [pl.BlockSpec((q_tok, 128), lambda hp, rb, b: (b * n_rb + rb, hp))]
    in_specs += [pl.BlockSpec((KEY_BLK, 128), kmap(j, A_W // 128)) for j in range(4)]
    in_specs += [pl.BlockSpec((KEY_BLK, 128), kmap(j, 2 * A_W // 128)) for j in range(4)]
    in_specs += [
        pl.BlockSpec((1, 1, past, 128), lambda hp, rb, b: (b, layer, 0, hp)),
        pl.BlockSpec((1, 1, past, 128), lambda hp, rb, b: (b, layer, 0, hp)),
        pl.BlockSpec((1, 2, 2 * WIN_R, 2, GRID_W, 2 * GRID_W), lambda hp, rb, b: (layer, hp, 0, 0, 0, 0)),
    ]
    return pl.pallas_call(
        _nbr_attn_kernel,
        grid=(H_ATT // 2, n_rb, n_seq),
        in_specs=in_specs,
        out_specs=pl.BlockSpec((q_tok, 128), lambda hp, rb, b: (b * n_rb + rb, hp)),
        out_shape=jax.ShapeDtypeStruct((n_seq * seq_len, A_W), F32),
        compiler_params=_params(("arbitrary", "arbitrary", "arbitrary")),
        name="nbr_attention",
    )(p, p, p, p, p, p, p, p, p, cache_k4, cache_v4, bias_tiles)


def _dft_mats(n):
    idx = jnp.arange(n, dtype=jnp.int32)
    ang = ((idx[:, None] * idx[None, :]) % n).astype(F32) * (2.0 * np.pi / n)
    return jnp.cos(ang).astype(BF16), jnp.sin(ang).astype(BF16)


def _channel_dft():
    c = np.arange(HEAD_DIM)
    ang = 2.0 * np.pi * ((c[:, None] * c[None, :]) % HEAD_DIM) / HEAD_DIM
    eye = np.eye(G_FOURIER)
    mats = np.concatenate([np.kron(eye, np.cos(ang)), np.kron(eye, np.sin(ang))], axis=1)
    return jnp.asarray(mats, F32).astype(BF16)


def _fourier_kernel(c_ref, s_ref, ab_ref, wf_ref, o_ref, acc_ref, *, scale, n_k):
    k = pl.program_id(2)

    @pl.when(k == 0)
    def _():
        acc_ref[...] = jnp.zeros_like(acc_ref)

    acc_ref[...] += _dot(c_ref[...], ab_ref[:, :B_W]) - _dot(s_ref[...], ab_ref[:, B_W:])

    @pl.when(k == n_k - 1)
    def _():
        z = (acc_ref[...] * scale).astype(BF16)
        o = _dot(z, wf_ref[0])
        for c in range(B_W // LANE):
            o_ref[c] = o[:, LANE * c:LANE * (c + 1)]


def _fourier(ab, cmat, smat, wf_blk, layer, n_seq, seq_len):
    ti = min(seq_len, 512)
    tk = min(seq_len, 1024)
    n_i, n_k = seq_len // ti, seq_len // tk
    scale = float((seq_len * HEAD_DIM) ** -0.5)
    return pl.pallas_call(
        functools.partial(_fourier_kernel, scale=scale, n_k=n_k),
        grid=(n_seq, n_i, n_k),
        in_specs=[
            pl.BlockSpec((ti, tk), lambda s, i, k: (i, k)),
            pl.BlockSpec((ti, tk), lambda s, i, k: (i, k)),
            pl.BlockSpec((tk, 2 * B_W), lambda s, i, k: (s * n_k + k, 0)),
            pl.BlockSpec((1, B_W, B_W), lambda s, i, k: (layer, 0, 0)),
        ],
        out_specs=pl.BlockSpec((B_W // LANE, ti, LANE), lambda s, i, k: (0, s * n_i + i, 0)),
        out_shape=jax.ShapeDtypeStruct((B_W // LANE, n_seq * seq_len, LANE), F32),
        scratch_shapes=[pltpu.VMEM((ti, B_W), F32)],
        compiler_params=_params(("arbitrary", "arbitrary", "arbitrary")),
        name="fourier",
    )(cmat, smat, ab, wf_blk)


FS_GROUP = 8


def _twiddles(side, n):
    k1 = jnp.arange(side, dtype=jnp.int32)[:, None]
    n2 = jnp.arange(side, dtype=jnp.int32)[None, :]
    ang = (k1 * n2).astype(F32) * (2.0 * np.pi / n)
    return jnp.cos(ang), jnp.sin(ang)


def _fourier_grid_kernel(ab_ref, c_ref, s_ref, tc_ref, ts_ref, wf_ref, o_ref, y_ref, *, side, scale):
    cmat = c_ref[...]
    smat = s_ref[...]
    for g in range(side // FS_GROUP):
        ab = ab_ref[:, 2 * B_W * FS_GROUP * g:2 * B_W * FS_GROUP * (g + 1)].astype(BF16)
        m1 = _dot(cmat, ab)
        m2 = _dot(smat, ab)
        for t in range(FS_GROUP):
            n2 = FS_GROUP * g + t
            a0 = 2 * B_W * t
            yr = m1[:, a0:a0 + B_W] - m2[:, a0 + B_W:a0 + 2 * B_W]
            yi = -(m1[:, a0 + B_W:a0 + 2 * B_W] + m2[:, a0:a0 + B_W])
            ct = tc_ref[:, n2:n2 + 1]
            st = ts_ref[:, n2:n2 + 1]
            y = jnp.concatenate([yr * ct + yi * st, yi * ct - yr * st], axis=1)
            for c in range(2 * B_W // LANE):
                y_ref[c, pl.ds(n2, side, stride=side), :] = y[:, LANE * c:LANE * (c + 1)]
    for g in range(side // FS_GROUP):
        zs = []
        for j in range(FS_GROUP):
            rows = slice(side * (FS_GROUP * g + j), side * (FS_GROUP * g + j + 1))
            n_lt = B_W // LANE
            y_re = jnp.concatenate([y_ref[c, rows, :] for c in range(n_lt)], axis=1).astype(BF16)
            y_im = jnp.concatenate([y_ref[n_lt + c, rows, :] for c in range(n_lt)], axis=1).astype(BF16)
            zs.append(_dot(cmat, y_re) + _dot(smat, y_im))
        z = (jnp.concatenate(zs, axis=0) * scale).astype(BF16)
        o = _dot(z, wf_ref[0])
        for j in range(FS_GROUP):
            for c in range(B_W // LANE):
                o_ref[c, pl.ds(FS_GROUP * g + j, side, stride=side), :] = o[side * j:side * (j + 1),
                                                                            LANE * c:LANE * (c + 1)]


def _fourier_grid(ab_grid, cmat, smat, tw_cos, tw_sin, wf_blk, layer, n_seq, side):
    seq_len = side * side
    scale = float((seq_len * HEAD_DIM) ** -0.5)
    small = pl.BlockSpec((side, side), lambda s: (0, 0))
    return pl.pallas_call(
        functools.partial(_fourier_grid_kernel, side=side, scale=scale),
        grid=(n_seq,),
        in_specs=[
            pl.BlockSpec((side, side * 2 * B_W), lambda s: (s, 0)),
            small,
            small,
            small,
            small,
            pl.BlockSpec((1, B_W, B_W), lambda s: (layer, 0, 0)),
        ],
        out_specs=pl.BlockSpec((B_W // LANE, seq_len, LANE), lambda s: (0, s, 0)),
        out_shape=jax.ShapeDtypeStruct((B_W // LANE, n_seq * seq_len, LANE), F32),
        scratch_shapes=[pltpu.VMEM((2 * B_W // LANE, seq_len, LANE), F32)],
        compiler_params=_params(("arbitrary",), vmem_mib=56),
        name="fourier_grid",
    )(ab_grid, cmat, smat, tw_cos, tw_sin, wf_blk)


SEQS_PER_STEP = 2


def _mlstm_kernel(*refs, n_chunks):
    sps = SEQS_PER_STEP
    n_side = 3 + 2 * sps
    fwd, bwd = refs[:n_side], refs[n_side:2 * n_side]
    c0_ref, n0_ref, m0_ref, tril_ref, triu_ref = refs[2 * n_side:2 * n_side + 5]
    hf_ref, hb_ref, cout_ref, nout_ref, mout_ref, c_s, n_s, m_s = refs[2 * n_side + 5:]
    c = pl.program_id(1)
    hi = lax.Precision.HIGHEST
    lc = MLSTM_CHUNK
    pair_w = 2 * HEAD_DIM
    n_pairs = H_MLSTM // 2

    @pl.when(c == 0)
    def _():
        c_s[...] = c0_ref[...]
        n_s[...] = n0_ref[...]
        m_s[...] = m0_ref[...]

    lo_lane = lax.broadcasted_iota(jnp.int32, (1, pair_w), 1) < HEAD_DIM
    lo_row = lax.broadcasted_iota(jnp.int32, (pair_w, 1), 0) < HEAD_DIM
    row8 = lax.broadcasted_iota(jnp.int32, (8, 1), 0)
    cum_mask = [tril_ref[...], triu_ref[...]]
    keep_t = [triu_ref[...] > 0.5, tril_ref[...] > 0.5]
    pairs = [(j, d, hp) for j in range(sps) for d in range(2) for hp in range(n_pairs)]
    heads = [(pi, hh) for pi in range(len(pairs)) for hh in range(2)]
    rng = range(len(heads))

    pre = {}
    for j in range(sps):
        for d, side in enumerate((fwd, bwd)):
            q_ref, k_ref, g_ref = side[:3]
            gt_ref, vt_ref = side[3 + 2 * j], side[4 + 2 * j]
            go = 2 * H_MLSTM * d
            lf_cols = _log_sigmoid(g_ref[j, :, go + H_MLSTM:go + 2 * H_MLSTM])
            lf_rows = _log_sigmoid(gt_ref[go + H_MLSTM:go + 2 * H_MLSTM, :])
            b_cols = _dot(cum_mask[d], lf_cols, precision=hi)
            pre[j, d] = dict(
                a_cols=g_ref[j, :, go:go + H_MLSTM] - b_cols,
                ig_rows=gt_ref[go:go + H_MLSTM, :],
                b_rows=_dot_nt(lf_rows, cum_mask[d], precision=hi),
                q=q_ref[j].astype(BF16), k=(k_ref[j] * (HEAD_DIM ** -0.5)).astype(BF16), vt=vt_ref[...])

    def pair_cols(hp):
        return slice(pair_w * hp, pair_w * (hp + 1))

    q_p = [pre[j, d]["q"][:, pair_cols(hp)] for j, d, hp in pairs]
    k_p = [pre[j, d]["k"][:, pair_cols(hp)] for j, d, hp in pairs]
    vt_p = [pre[j, d]["vt"][pair_cols(hp), :] for j, d, hp in pairs]
    c_p = [c_s[j, d, hp] for j, d, hp in pairs]
    n_p = [n_s[j, n_pairs * d + hp:n_pairs * d + hp + 1, :] for j, d, hp in pairs]
    zero_k = jnp.zeros((lc, pair_w), BF16)
    k_h = [jnp.where(lo_lane, k_p[pi], zero_k) if hh == 0 else jnp.where(lo_lane, zero_k, k_p[pi])
           for pi, hh in heads]

    def head_of(i):
        pi, hh = heads[i]
        j, d, hp = pairs[pi]
        return j, d, 2 * hp + hh

    b_row = [pre[head_of(i)[0], head_of(i)[1]]["b_rows"][head_of(i)[2]:head_of(i)[2] + 1, :] for i in rng]
    ig_row = [pre[head_of(i)[0], head_of(i)[1]]["ig_rows"][head_of(i)[2]:head_of(i)[2] + 1, :] for i in rng]
    a_col = [pre[head_of(i)[0], head_of(i)[1]]["a_cols"][:, head_of(i)[2]:head_of(i)[2] + 1] for i in rng]
    m_st = [m_s[head_of(i)[0], H_MLSTM * head_of(i)[1] + head_of(i)[2]:H_MLSTM * head_of(i)[1] + head_of(i)[2] + 1, :]
            for i in rng]
    bl = [b_row[i][:, lc - 1:lc] if head_of(i)[1] == 0 else b_row[i][:, 0:1] for i in rng]

    d_t = [jnp.where(keep_t[head_of(i)[1]], b_row[i] + a_col[i], NEG) for i in rng]
    inter = [b_row[i] + m_st[i] for i in rng]
    m_t = [jnp.maximum(inter[i], jnp.max(d_t[i], axis=0, keepdims=True)) for i in rng]
    s_t = [_dot_nt(k_h[i], q_p[heads[i][0]]) * jnp.exp(d_t[i] - m_t[i]) for i in rng]
    w_in = [jnp.exp(inter[i] - m_t[i]) for i in rng]
    num_t = [_dot(vt_p[heads[i][0]].astype(BF16), s_t[i].astype(BF16)) for i in rng]
    qc_t = [_dot_nt(c_p[pi].astype(BF16), q_p[pi]) for pi in range(len(pairs))]
    n_mat = [jnp.where((row8 == 0) & lo_lane, n_p[pi], jnp.where((row8 == 1) & ~lo_lane, n_p[pi], 0.0))
             for pi in range(len(pairs))]
    nq = [_dot_nt(n_mat[pi].astype(BF16), q_p[pi]) for pi in range(len(pairs))]
    den = [jnp.sum(s_t[i], axis=0, keepdims=True) + w_in[i] * nq[heads[i][0]][heads[i][1]:heads[i][1] + 1, :]
           for i in rng]
    inv = [1.0 / jnp.maximum(jnp.abs(den[i]), jnp.exp(-m_t[i])) for i in rng]
    h_t = []
    for pi in range(len(pairs)):
        i0, i1 = 2 * pi, 2 * pi + 1
        num = jnp.where(lo_row, num_t[i0], num_t[i1])
        h_t.append((num + jnp.where(lo_row, w_in[i0], w_in[i1]) * qc_t[pi]) * jnp.where(lo_row, inv[i0], inv[i1]))
    for j in range(sps):
        base = 2 * n_pairs * j
        hf_ref[j] = jnp.concatenate(h_t[base:base + n_pairs], axis=0)
        hb_ref[j] = jnp.concatenate(h_t[base + n_pairs:base + 2 * n_pairs], axis=0)

    g_row = [bl[i] - b_row[i] + ig_row[i] for i in rng]
    m_new = [jnp.maximum(bl[i] + m_st[i], jnp.max(g_row[i], axis=-1, keepdims=True)) for i in rng]
    wc = [jnp.exp(bl[i] + m_st[i] - m_new[i]) for i in rng]
    ws_row = [jnp.exp(g_row[i] - m_new[i]) for i in rng]
    upd = [_dot((vt_p[heads[i][0]] * ws_row[i]).astype(BF16), k_p[heads[i][0]]) for i in rng]
    for pi, (j, d, hp) in enumerate(pairs):
        i0, i1 = 2 * pi, 2 * pi + 1
        block = jnp.where(lo_row & lo_lane, upd[i0], jnp.where(~lo_row & ~lo_lane, upd[i1], 0.0))
        c_s[j, d, hp] = jnp.where(lo_row, wc[i0], wc[i1]) * c_p[pi] + block
        ws_mat = jnp.where(row8 == 0, ws_row[i0], jnp.where(row8 == 1, ws_row[i1], 0.0))
        k_sum = _dot(ws_mat.astype(BF16), k_p[pi])
        row = n_pairs * d + hp
        n_s[j, row:row + 1, :] = (jnp.where(lo_lane, wc[i0], wc[i1]) * n_p[pi]
                                  + jnp.where(lo_lane, k_sum[0:1, :], k_sum[1:2, :]))
    for i in rng:
        j, d, hd = head_of(i)
        m_s[j, H_MLSTM * d + hd:H_MLSTM * d + hd + 1, :] = m_new[i]

    @pl.when(c == n_chunks - 1)
    def _():
        cout_ref[...] = c_s[...]
        nout_ref[...] = n_s[...]
        mout_ref[...] = m_s[...]


def _pair_states(c):
    b = c.shape[0]
    c = c.reshape(b, 2, H_MLSTM // 2, 2, HEAD_DIM, HEAD_DIM)
    zero = jnp.zeros_like(c[:, :, :, 0])
    top = jnp.concatenate([c[:, :, :, 0], zero], axis=-1)
    bottom = jnp.concatenate([zero, c[:, :, :, 1]], axis=-1)
    return jnp.concatenate([top, bottom], axis=-2)


def _unpair_states(cp):
    b = cp.shape[0]
    first = cp[:, :, :, :HEAD_DIM, :HEAD_DIM]
    second = cp[:, :, :, HEAD_DIM:, HEAD_DIM:]
    return jnp.stack([first, second], axis=3).reshape(b, 2, H_MLSTM, HEAD_DIM, HEAD_DIM)


def _mlstm(p, g, gt, vo, c0, n0, m0, tril, triu, n_seq, seq_len):
    lc = MLSTM_CHUNK
    nc = seq_len // lc
    n_st = 2 * H_MLSTM
    n_pairs = H_MLSTM // 2
    pair_w = 2 * HEAD_DIM
    sps = SEQS_PER_STEP
    p3 = p.reshape(n_seq, seq_len, P_KEEP)
    g3 = g.reshape(n_seq, seq_len, N_GATE_COLS)

    def fwd(c):
        return c

    def bwd(c):
        return nc - 1 - c

    def side(chunk):
        tok = lambda col: (lambda b, c: (b, chunk(c), col))
        specs = [
            pl.BlockSpec((sps, lc, C_W), tok(QC_BLK)),
            pl.BlockSpec((sps, lc, C_W), tok(KC_BLK)),
            pl.BlockSpec((sps, lc, N_GATE_COLS), tok(0)),
        ]
        for j in range(sps):
            lanes = lambda b, c, j=j: (0, (b * sps + j) * nc + chunk(c))
            specs += [pl.BlockSpec((N_GATE_COLS, lc), lanes), pl.BlockSpec((C_W, lc), lanes)]
        return specs

    state_specs = [
        pl.BlockSpec((sps, 2, n_pairs, pair_w, pair_w), lambda b, c: (b, 0, 0, 0, 0)),
        pl.BlockSpec((sps, 2 * n_pairs, pair_w), lambda b, c: (b, 0, 0)),
        pl.BlockSpec((sps, n_st, 1), lambda b, c: (b, 0, 0)),
    ]
    tri_spec = pl.BlockSpec((lc, lc), lambda b, c: (0, 0))
    operands = [p3, p3, g3] + [gt, vo] * sps
    hf, hb, c_out, n_out, m_out = pl.pallas_call(
        functools.partial(_mlstm_kernel, n_chunks=nc),
        grid=(n_seq // sps, nc),
        in_specs=side(fwd) + side(bwd) + state_specs + [tri_spec, tri_spec],
        out_specs=[
            pl.BlockSpec((sps, C_W, lc), lambda b, c: (b, 0, c)),
            pl.BlockSpec((sps, C_W, lc), lambda b, c: (b, 0, nc - 1 - c)),
        ] + state_specs,
        out_shape=[
            jax.ShapeDtypeStruct((n_seq, C_W, seq_len), F32),
            jax.ShapeDtypeStruct((n_seq, C_W, seq_len), F32),
            jax.ShapeDtypeStruct((n_seq, 2, n_pairs, pair_w, pair_w), F32),
            jax.ShapeDtypeStruct((n_seq, 2 * n_pairs, pair_w), F32),
            jax.ShapeDtypeStruct((n_seq, n_st, 1), F32),
        ],
        scratch_shapes=[
            pltpu.VMEM((sps, 2, n_pairs, pair_w, pair_w), F32),
            pltpu.VMEM((sps, 2 * n_pairs, pair_w), F32),
            pltpu.VMEM((sps, n_st, 1), F32),
        ],
        compiler_params=_params(("arbitrary", "arbitrary")),
        name="mlstm",
    )(*operands, *operands, _pair_states(c0), n0.reshape(n_seq, 2 * n_pairs, pair_w), m0.reshape(n_seq, n_st, 1),
      tril, triu)
    return (hf, hb, _unpair_states(c_out), n_out.reshape(n_seq, 2, H_MLSTM, HEAD_DIM),
            m_out.reshape(n_seq, 2, H_MLSTM))


def _head_norm(y, g, ones_blk):
    ysq = y * y
    hi = ysq.astype(BF16)
    lo = (ysq - hi.astype(F32)).astype(BF16)
    ss = _dot(hi, ones_blk) + _dot(lo, ones_blk)
    return y * lax.rsqrt(ss * (1.0 / HEAD_DIM) + EPS) * g


def _merge_kernel(att_ref, four_ref, hf_ref, hb_ref, oc_ref, x_ref, mod_ref, gh_ref, ghm_ref, wo_ref, g2_ref, wrt_ref,
                  ones_ref, xo_ref, h2_ref, afft_ref, *, row_tiles):
    gh = gh_ref[0]
    half = x_ref.shape[0] // 2
    parts = [slice(half * p, half * (p + 1)) for p in range(2)]
    ya = [_head_norm(att_ref[r, :], gh[:, :A_W], ones_ref[...]) for r in parts]
    four = [jnp.concatenate([four_ref[c, r, :] for c in range(B_W // LANE)], axis=1) for r in parts]
    yf = [_head_norm(f, gh[:, A_W:A_W + B_W], ones_ref[:B_W, :B_W]) for f in four]
    mem = [hf_ref[0, :, r] + hb_ref[0, :, r] for r in parts]
    ym_t = []
    for p, r in enumerate(parts):
        heads = []
        for hd in range(H_MLSTM):
            y = mem[p][HEAD_DIM * hd:HEAD_DIM * (hd + 1), :]
            heads.append(y * lax.rsqrt(jnp.mean(y * y, axis=0, keepdims=True) + EPS))
        ym_t.append(jnp.concatenate(heads, axis=0) * ghm_ref[0, :, r] * jax.nn.sigmoid(oc_ref[:, r]))
    out = [_dot(ya[p].astype(BF16), wo_ref[0, :A_W, :])
           + _dot(yf[p].astype(BF16), wo_ref[0, A_W:A_W + B_W, :])
           + _dot_tn(ym_t[p].astype(BF16), wo_ref[0, A_W + B_W:, :]) for p in range(2)]
    x = [x_ref[r, :] + mod_ref[0, 2:3, :] * out[p] for p, r in enumerate(parts)]
    y2 = [xp * lax.rsqrt(jnp.mean(xp * xp, axis=-1, keepdims=True) + EPS) * g2_ref[0] for xp in x]
    h2 = [(yp * (1.0 + mod_ref[0, 4:5, :]) + mod_ref[0, 3:4, :]).astype(BF16) for yp in y2]
    logits = [_dot_nt(wrt_ref[0], hp) for hp in h2]
    e = [jnp.exp(lg - jnp.max(lg, axis=0, keepdims=True)) for lg in logits]
    for p, r in enumerate(parts):
        xo_ref[r, :] = x[p]
        if row_tiles:
            h2_wide = h2[p].astype(F32)
            for s in range(N_SLAB):
                h2_ref[pl.ds(N_SLAB * half * p + s, half, stride=N_SLAB), :] = h2_wide[:, LANE * s:LANE * (s + 1)]
        else:
            h2_ref[r, :] = h2[p]
        afft_ref[:, r] = e[p] / jnp.sum(e[p], axis=0, keepdims=True)


def _merge(att, four, hf, hb, vo, x2d, mod, layer, g_head, g_mem, w_out_bf, g2, w_rt_bf, ones_blk, n_seq, seq_len,
           mod_seq_len, row_tiles):
    t = x2d.shape[0]
    tm = MERGE_TILE
    tiles_per_mod = mod_seq_len // tm
    tiles_per_seq = seq_len // tm
    row = lambda i: (i, 0)
    lay = lambda i: (layer, 0, 0)
    mem = lambda i: (i // tiles_per_seq, 0, i % tiles_per_seq)
    if row_tiles:
        h2_spec, h2_shape = pl.BlockSpec((N_SLAB * tm, LANE), row), jax.ShapeDtypeStruct((N_SLAB * t, LANE), F32)
    else:
        h2_spec, h2_shape = pl.BlockSpec((tm, D_MODEL), row), jax.ShapeDtypeStruct((t, D_MODEL), BF16)
    return pl.pallas_call(
        functools.partial(_merge_kernel, row_tiles=row_tiles),
        grid=(t // tm,),
        in_specs=[
            pl.BlockSpec((tm, A_W), row),
            pl.BlockSpec((B_W // LANE, tm, LANE), lambda i: (0, i, 0)),
            pl.BlockSpec((1, C_W, tm), mem),
            pl.BlockSpec((1, C_W, tm), mem),
            pl.BlockSpec((C_W, tm), lambda i: (1, i)),
            pl.BlockSpec((tm, D_MODEL), row),
            pl.BlockSpec((1, 6, D_MODEL), lambda i: (i // tiles_per_mod, 0, 0)),
            pl.BlockSpec((1, 1, D_MODEL), lay),
            pl.BlockSpec((1, C_W, tm), lay),
            pl.BlockSpec((1, D_MODEL, D_MODEL), lay),
            pl.BlockSpec((1, 1, D_MODEL), lay),
            pl.BlockSpec((1, N_EXPERTS, D_MODEL), lay),
            pl.BlockSpec((A_W, A_W), lambda i: (0, 0)),
        ],
        out_specs=[
            pl.BlockSpec((tm, D_MODEL), row),
            h2_spec,
            pl.BlockSpec((N_EXPERTS, tm), lambda i: (0, i)),
        ],
        out_shape=[
            jax.ShapeDtypeStruct((t, D_MODEL), F32),
            h2_shape,
            jax.ShapeDtypeStruct((N_EXPERTS, t), F32),
        ],
        compiler_params=_params(("arbitrary",)),
        name="merge",
    )(att, four, hf, hb, vo, x2d, mod, g_head, g_mem, w_out_bf, g2, w_rt_bf, ones_blk)


BISECT_STEPS = 48
TOKEN_CHUNK = 1024


TOKEN_SPLIT = 64


def _route_kernel(aff_ref, triu_ref, idx_ref, gs_ref, sp_ref, *, ns, seq_len, cap):
    seqs = range(ns)
    aff = [aff_ref[:, seq_len * j:seq_len * (j + 1)] for j in seqs]

    def body(_, bounds):
        out = []
        for j in seqs:
            lo, hi = bounds[j]
            mid = 0.5 * (lo + hi)
            ge = jnp.sum(jnp.where(aff[j] >= mid, 1.0, 0.0), axis=1, keepdims=True) >= cap
            out.append((jnp.where(ge, mid, lo), jnp.where(ge, hi, mid)))
        return tuple(out)

    start = (jnp.zeros((N_EXPERTS, 1), F32), jnp.full((N_EXPERTS, 1), 2.0, F32))
    bounds = lax.fori_loop(0, BISECT_STEPS, body, tuple(start for _ in seqs))
    thr = [jnp.max(jnp.where(aff[j] < bounds[j][1], aff[j], -1.0), axis=1, keepdims=True) for j in seqs]
    need = [cap - jnp.sum(jnp.where(aff[j] > thr[j], 1.0, 0.0), axis=1, keepdims=True) for j in seqs]
    triu = triu_ref[...]
    eq_carry = [jnp.zeros((N_EXPERTS, 1), F32) for _ in seqs]
    pos_carry = [jnp.zeros((N_EXPERTS, 1), F32) for _ in seqs]
    for b in range(seq_len // 128):
        for j in seqs:
            blk = aff[j][:, 128 * b:128 * (b + 1)]
            eq = blk == thr[j]
            eq_f = jnp.where(eq, 1.0, 0.0)
            eq_inc = _dot(eq_f.astype(BF16), triu) + eq_carry[j]
            sel = (blk > thr[j]) | (eq & (eq_inc - eq_f < need[j]))
            sel_f = jnp.where(sel, 1.0, 0.0)
            pos_inc = _dot(sel_f.astype(BF16), triu) + pos_carry[j]
            t0 = seq_len * j + 128 * b
            sp_ref[:, t0:t0 + 128] = jnp.where(sel, pos_inc - sel_f, -1.0).astype(jnp.int32)
            eq_carry[j] = eq_inc[:, 127:128]
            pos_carry[j] = pos_inc[:, 127:128]

    tc = min(seq_len, TOKEN_CHUNK)
    slot = lax.broadcasted_iota(jnp.int32, (cap, tc), 0)
    part = lax.broadcasted_iota(jnp.int32, (8, tc), 0)
    tok = lax.broadcasted_iota(jnp.int32, (1, tc), 1).astype(F32)
    chunks = range(0, seq_len, tc)
    tok_hi = [jnp.floor((tok + float(t0)) * (1.0 / TOKEN_SPLIT)) for t0 in chunks]
    tok_lo = [tok + float(t0) - TOKEN_SPLIT * hi for t0, hi in zip(chunks, tok_hi)]

    def per_expert(e, carry):
        for j in seqs:
            acc = jnp.zeros((cap, 8), F32)
            for ci, t0 in enumerate(chunks):
                cols = slice(seq_len * j + t0, seq_len * j + t0 + tc)
                onehot = jnp.where(slot == sp_ref[pl.ds(e, 1), cols], 1.0, 0.0).astype(BF16)
                a = aff_ref[pl.ds(e, 1), cols]
                a_hi = a.astype(BF16).astype(F32)
                a_mid = (a - a_hi).astype(BF16).astype(F32)
                a_lo = a - a_hi - a_mid
                vals = jnp.where(part == 0, tok_hi[ci], jnp.where(part == 1, tok_lo[ci], jnp.where(
                    part == 2, a_hi, jnp.where(part == 3, a_mid, jnp.where(part == 4, a_lo, 0.0)))))
                acc = acc + _dot_nt(onehot, vals.astype(BF16))
            idx_ref[N_EXPERTS * j + e] = (N_SLAB * (TOKEN_SPLIT * acc[:, 0:1] + acc[:, 1:2])).astype(jnp.int32)
            gs_ref[e, cap * j:cap * (j + 1)] = acc[:, 2:3] + acc[:, 3:4] + acc[:, 4:5]
        return carry

    lax.fori_loop(0, N_EXPERTS, per_expert, 0)


def _route(afft, triu_bf, n_seq, seq_len, cap, ns):
    idx, gs, sp = pl.pallas_call(
        functools.partial(_route_kernel, ns=ns, seq_len=seq_len, cap=cap),
        grid=(n_seq // ns,),
        in_specs=[
            pl.BlockSpec((N_EXPERTS, ns * seq_len), lambda s: (0, s)),
            pl.BlockSpec((128, 128), lambda s: (0, 0)),
        ],
        out_specs=[
            pl.BlockSpec((ns * N_EXPERTS, cap, 1), lambda s: (s, 0, 0)),
            pl.BlockSpec((N_EXPERTS, ns * cap, 1), lambda s: (0, s, 0)),
            pl.BlockSpec((N_EXPERTS, ns * seq_len), lambda s: (0, s)),
        ],
        out_shape=[
            jax.ShapeDtypeStruct((n_seq * N_EXPERTS, cap, 1), jnp.int32),
            jax.ShapeDtypeStruct((N_EXPERTS, n_seq * cap, 1), F32),
            jax.ShapeDtypeStruct((N_EXPERTS, n_seq * seq_len), jnp.int32),
        ],
        compiler_params=_params(("arbitrary",)),
        name="route",
    )(afft, triu_bf)
    return idx.reshape(n_seq * N_EXPERTS * cap), gs, sp


ROW_COPIES = 8


def _row_tile(first_row):
    return pl.ds(pl.multiple_of(first_row, N_SLAB), N_SLAB)


def _gather_kernel(idx_ref, src_ref, xs_ref, tile_ref, *, eb, cap):
    ei = pl.program_id(1)

    def per_expert(ee, carry):
        e = ei * eb + ee

        def rows(g, c):
            slot0 = g * ROW_COPIES
            for u in range(ROW_COPIES):
                tile_ref[_row_tile((slot0 + u) * N_SLAB), :] = src_ref[_row_tile(idx_ref[e * cap + slot0 + u]), :]
            return c

        lax.fori_loop(0, cap // ROW_COPIES, rows, 0)
        for s in range(N_SLAB):
            xs_ref[ee, :, LANE * s:LANE * (s + 1)] = tile_ref[pl.ds(s, cap, stride=N_SLAB), :].astype(BF16)
        return carry

    lax.fori_loop(0, eb, per_expert, 0)


def _gather(idx, h2_rows, n_seq, cap, eb):
    return pl.pallas_call(
        functools.partial(_gather_kernel, eb=eb, cap=cap),
        grid=(n_seq, N_EXPERTS // eb),
        in_specs=[
            pl.BlockSpec((N_EXPERTS * cap,), lambda s, e: (s,), memory_space=pltpu.SMEM),
            pl.BlockSpec((h2_rows.shape[0] // n_seq, LANE), lambda s, e: (s, 0)),
        ],
        out_specs=pl.BlockSpec((eb, cap, D_MODEL), lambda s, e: (e, s, 0)),
        out_shape=jax.ShapeDtypeStruct((N_EXPERTS, n_seq * cap, D_MODEL), BF16),
        scratch_shapes=[pltpu.VMEM((N_SLAB * cap, LANE), F32)],
        compiler_params=_params(("arbitrary", "arbitrary"), vmem_mib=56),
        name="gather",
    )(idx, h2_rows)


def _selection(sp_ref, cap):
    seq_len = sp_ref.shape[1]
    slot = lax.broadcasted_iota(jnp.int32, (cap, seq_len), 0)
    return jnp.concatenate([jnp.where(slot == sp_ref[e:e + 1, :], 1.0, 0.0) for e in range(N_EXPERTS)],
                           axis=0).astype(BF16)


def _gather_short_kernel(sp_ref, h_ref, xs_ref, *, cap):
    rows = _dot(_selection(sp_ref, cap), h_ref[...])
    for e in range(N_EXPERTS):
        xs_ref[e] = rows[cap * e:cap * (e + 1), :].astype(BF16)


def _gather_short(sp, h2, n_seq, seq_len, cap):
    return pl.pallas_call(
        functools.partial(_gather_short_kernel, cap=cap),
        grid=(n_seq,),
        in_specs=[
            pl.BlockSpec((N_EXPERTS, seq_len), lambda s: (0, s)),
            pl.BlockSpec((seq_len, D_MODEL), lambda s: (s, 0)),
        ],
        out_specs=pl.BlockSpec((N_EXPERTS, cap, D_MODEL), lambda s: (0, s, 0)),
        out_shape=jax.ShapeDtypeStruct((N_EXPERTS, n_seq * cap, D_MODEL), BF16),
        compiler_params=_params(("arbitrary",)),
        name="gather_short",
    )(sp, h2)


def _scatter_short_kernel(sp_ref, y_ref, x_ref, mod_ref, gf_ref, o_ref, *, cap, final):
    sel = _selection(sp_ref, cap)
    y = jnp.concatenate([y_ref[e] for e in range(N_EXPERTS)], axis=0)
    y_hi = y.astype(BF16)
    y_lo = (y - y_hi.astype(F32)).astype(BF16)
    moe = _dot_tn(sel, y_hi) + _dot_tn(sel, y_lo)
    x = x_ref[...] + mod_ref[0, 5:6, :] * moe
    if final:
        x = x * lax.rsqrt(jnp.mean(x * x, axis=-1, keepdims=True) + EPS) * gf_ref[...]
    o_ref[...] = x


def _scatter_short(sp, ys, x2d, mod, g_final, n_seq, seq_len, cap, final):
    return pl.pallas_call(
        functools.partial(_scatter_short_kernel, cap=cap, final=final),
        grid=(n_seq,),
        in_specs=[
            pl.BlockSpec((N_EXPERTS, seq_len), lambda s: (0, s)),
            pl.BlockSpec((N_EXPERTS, cap, D_MODEL), lambda s: (0, s, 0)),
            pl.BlockSpec((seq_len, D_MODEL), lambda s: (s, 0)),
            pl.BlockSpec((1, 6, D_MODEL), lambda s: (0, 0, 0)),
            pl.BlockSpec((1, D_MODEL), lambda s: (0, 0)),
        ],
        out_specs=pl.BlockSpec((seq_len, D_MODEL), lambda s: (s, 0)),
        out_shape=jax.ShapeDtypeStruct((n_seq * seq_len, D_MODEL), F32),
        compiler_params=_params(("arbitrary",)),
        name="scatter_short",
    )(sp, ys, x2d, mod, g_final)


def _expert_kernel(xc_ref, xl_ref, gc_ref, gl_ref, wg_ref, wu_ref, wd_ref, yc_ref, yl_ref, *, n_f):
    f = pl.program_id(1)

    @pl.when(f == 0)
    def _():
        yc_ref[...] = jnp.zeros_like(yc_ref)
        yl_ref[...] = jnp.zeros_like(yl_ref)

    wg = wg_ref[0, 0].astype(BF16)
    wu = wu_ref[0, 0].astype(BF16)
    wd = wd_ref[0, 0].astype(BF16)
    for x_ref, y_ref in ((xc_ref, yc_ref), (xl_ref, yl_ref)):
        x = x_ref[0]
        mid = (_silu(_dot(x, wg)) * _dot(x, wu)).astype(BF16)
        y_ref[0] += _dot(mid, wd)

    @pl.when(f == n_f - 1)
    def _():
        yc_ref[0] = yc_ref[0] * gc_ref[0]
        yl_ref[0] = yl_ref[0] * gl_ref[0]


def _experts(xs_c, xs_l, gs_c, gs_l, w_g, w_u, w_d, layer):
    rc, rl = xs_c.shape[1], xs_l.shape[1]
    tf = 1024
    n_f = EXPERT_FF // tf
    return pl.pallas_call(
        functools.partial(_expert_kernel, n_f=n_f),
        grid=(N_EXPERTS, n_f),
        in_specs=[
            pl.BlockSpec((1, rc, D_MODEL), lambda e, f: (e, 0, 0)),
            pl.BlockSpec((1, rl, D_MODEL), lambda e, f: (e, 0, 0)),
            pl.BlockSpec((1, rc, 1), lambda e, f: (e, 0, 0)),
            pl.BlockSpec((1, rl, 1), lambda e, f: (e, 0, 0)),
            pl.BlockSpec((1, 1, D_MODEL, tf), lambda e, f: (layer, e, 0, f)),
            pl.BlockSpec((1, 1, D_MODEL, tf), lambda e, f: (layer, e, 0, f)),
            pl.BlockSpec((1, 1, tf, D_MODEL), lambda e, f: (layer, e, f, 0)),
        ],
        out_specs=[
            pl.BlockSpec((1, rc, D_MODEL), lambda e, f: (e, 0, 0)),
            pl.BlockSpec((1, rl, D_MODEL), lambda e, f: (e, 0, 0)),
        ],
        out_shape=[
            jax.ShapeDtypeStruct((N_EXPERTS, rc, D_MODEL), F32),
            jax.ShapeDtypeStruct((N_EXPERTS, rl, D_MODEL), F32),
        ],
        compiler_params=_params(("arbitrary", "arbitrary"), vmem_mib=56),
        name="experts",
    )(xs_c, xs_l, gs_c, gs_l, w_g, w_u, w_d)


def _scatter_kernel(idx_ref, y_ref, x_ref, mod_ref, gf_ref, o_ref, acc_ref, tile_ref, *, eb, n_e, cap, tm, final):
    step = pl.program_id(1)

    @pl.when(step == 0)
    def _():
        acc_ref[...] = jnp.zeros_like(acc_ref)

    @pl.when(step < n_e)
    def _():
        def per_expert(ee, carry):
            e = step * eb + ee
            for s in range(N_SLAB):
                tile_ref[pl.ds(s, cap, stride=N_SLAB), :] = y_ref[ee, :, LANE * s:LANE * (s + 1)]

            def rows(g, c):
                slot0 = g * ROW_COPIES
                dst = [idx_ref[e * cap + slot0 + u] for u in range(ROW_COPIES)]
                new = [acc_ref[_row_tile(dst[u]), :] + tile_ref[_row_tile((slot0 + u) * N_SLAB), :]
                       for u in range(ROW_COPIES)]
                for u in range(ROW_COPIES):
                    acc_ref[_row_tile(dst[u]), :] = new[u]
                return c

            lax.fori_loop(0, cap // ROW_COPIES, rows, 0)
            return carry

        lax.fori_loop(0, eb, per_expert, 0)

    @pl.when(step >= n_e)
    def _():
        base = pl.multiple_of((step - n_e) * tm * N_SLAB, N_SLAB)
        moe = jnp.concatenate([acc_ref[pl.ds(base + s, tm, stride=N_SLAB), :] for s in range(N_SLAB)], axis=1)
        x = x_ref[...] + mod_ref[0, 5:6, :] * moe
        if final:
            x = x * lax.rsqrt(jnp.mean(x * x, axis=-1, keepdims=True) + EPS) * gf_ref[...]
        o_ref[...] = x


def _scatter(idx, ys, x2d, mod, g_final, n_seq, seq_len, cap, eb, tm, final):
    n_e = N_EXPERTS // eb
    n_out = seq_len // tm
    out_blk = lambda s, j: (s * n_out + jnp.maximum(j - n_e, 0), 0)
    return pl.pallas_call(
        functools.partial(_scatter_kernel, eb=eb, n_e=n_e, cap=cap, tm=tm, final=final),
        grid=(n_seq, n_e + n_out),
        in_specs=[
            pl.BlockSpec((N_EXPERTS * cap,), lambda s, j: (s,), memory_space=pltpu.SMEM),
            pl.BlockSpec((eb, cap, D_MODEL), lambda s, j: (jnp.minimum(j, n_e - 1), s, 0)),
            pl.BlockSpec((tm, D_MODEL), out_blk),
            pl.BlockSpec((1, 6, D_MODEL), lambda s, j: (s, 0, 0)),
            pl.BlockSpec((1, D_MODEL), lambda s, j: (0, 0)),
        ],
        out_specs=pl.BlockSpec((tm, D_MODEL), out_blk),
        out_shape=jax.ShapeDtypeStruct((n_seq * seq_len, D_MODEL), F32),
        scratch_shapes=[
            pltpu.VMEM((N_SLAB * seq_len, LANE), F32),
            pltpu.VMEM((N_SLAB * cap, LANE), F32),
        ],
        compiler_params=_params(("arbitrary", "arbitrary"), vmem_mib=56),
        name="scatter",
    )(idx, ys, x2d, mod, g_final)


def kernel(x_prompt, x_sample, c, cache_k, cache_v, state_C, state_n, state_m, c_ctx, w_ada, b_ada, g_norm1, g_norm2, w_in, b_gates, rpb, w_fourier, g_head, w_out, w_router, w_exp_gate, w_exp_up, w_exp_down, g_final):
    n_ctx, len_ctx, _ = x_prompt.shape
    n_lat, len_lat, _ = x_sample.shape
    past = cache_k.shape[2]
    cap_ctx = CAPACITY_FACTOR * len_ctx // N_EXPERTS
    cap_lat = CAPACITY_FACTOR * len_lat // N_EXPERTS

    w_in_bf = w_in.astype(BF16)
    w_gt_bf = jnp.swapaxes(w_in[:, :, P_COLS:], 1, 2).astype(BF16)
    vc0, oc0 = VC_BLK * C_W, OC_BLK * C_W
    w_vo_bf = jnp.swapaxes(jnp.concatenate([w_in[:, :, vc0:vc0 + C_W], w_in[:, :, oc0:oc0 + C_W]], axis=2),
                           1, 2).astype(BF16)
    bg_row = b_gates.reshape(DEPTH, 1, N_GATE_COLS).astype(F32)
    bg_col = b_gates.reshape(DEPTH, N_GATE_COLS, 1).astype(F32)
    w_out_bf = w_out.astype(BF16)
    w_rt_bf = jnp.swapaxes(w_router, 1, 2).astype(BF16)
    g1 = g_norm1.reshape(DEPTH, 1, D_MODEL)
    g2 = g_norm2.reshape(DEPTH, 1, D_MODEL)
    gh = g_head.reshape(DEPTH, 1, D_MODEL)
    g_mem = jnp.broadcast_to(gh[:, 0, A_W + B_W:, None], (DEPTH, C_W, MERGE_TILE))
    eye_g = jnp.eye(G_FOURIER, dtype=F32)
    wf_blk = jnp.einsum("lgcd,gh->lgchd", w_fourier, eye_g).reshape(DEPTH, B_W, B_W).astype(BF16)

    csc = _channel_dft()
    dft_ctx = _dft_mats(len_ctx)
    dft_side = _dft_mats(GRID_W)
    tw_cos, tw_sin = _twiddles(GRID_W, len_lat)
    bias_tiles = _nbr_bias_tiles(rpb)
    r = np.arange(MLSTM_CHUNK)
    tril = jnp.asarray(r[:, None] >= r[None, :], F32)
    triu = jnp.asarray(r[:, None] <= r[None, :], F32)
    triu_bf = triu.astype(BF16)
    hidx = np.arange(A_W) // HEAD_DIM
    ones_blk = jnp.asarray(hidx[:, None] == hidx[None, :], BF16)

    cvecs = jnp.concatenate([c_ctx[None, :], c, jnp.zeros((8 - 1 - n_lat, D_MODEL), F32)], axis=0)
    mod_all = _modulation(cvecs, w_ada, b_ada).reshape(DEPTH, 8, 6, D_MODEL)

    cache_k4 = cache_k.reshape(n_lat, DEPTH, past, A_W)
    cache_v4 = cache_v.reshape(n_lat, DEPTH, past, A_W)
    zero_c = jnp.zeros((n_ctx, 2, H_MLSTM, HEAD_DIM, HEAD_DIM), F32)
    zero_n = jnp.zeros((n_ctx, 2, H_MLSTM, HEAD_DIM), F32)
    zero_m = jnp.zeros((n_ctx, 2, H_MLSTM), F32)

    xc = x_prompt.reshape(n_ctx * len_ctx, D_MODEL)
    xl = x_sample.reshape(n_lat * len_lat, D_MODEL)
    gf = g_final.reshape(1, D_MODEL)
    new_k = jnp.zeros((n_ctx, DEPTH, len_ctx, A_W), F32)
    new_v = jnp.zeros((n_ctx, DEPTH, len_ctx, A_W), F32)
    cs, ns, ms = [], [], []
    for l in range(DEPTH):
        mod_c = mod_all[l, 0:1]
        mod_l = mod_all[l, 1:1 + n_lat]

        pc, gc, gtc, voc, abc, new_k, new_v = _inproj(xc, mod_c, l, g1, w_in_bf, w_gt_bf, w_vo_bf, bg_row, bg_col, csc,
                                                      n_ctx * len_ctx, False, (new_k, new_v))
        att_c = _ctx_attention(pc, n_ctx, len_ctx)
        four_c = _fourier(abc, dft_ctx[0], dft_ctx[1], wf_blk, l, n_ctx, len_ctx)
        hf_c, hb_c, c_new, n_new, m_new = _mlstm(pc, gc, gtc, voc, zero_c, zero_n, zero_m, tril, triu, n_ctx, len_ctx)
        xc, h2c, affc = _merge(att_c, four_c, hf_c, hb_c, voc, xc, mod_c, l, gh, g_mem, w_out_bf, g2, w_rt_bf,
                               ones_blk, n_ctx, len_ctx, n_ctx * len_ctx, False)
        cs.append(c_new)
        ns.append(n_new)
        ms.append(m_new)

        pq, gq, gtq, voq, abq = _inproj(xl, mod_l, l, g1, w_in_bf, w_gt_bf, w_vo_bf, bg_row, bg_col, csc, len_lat,
                                        True)
        att_l = _nbr_attention(pq, cache_k4, cache_v4, bias_tiles, l, n_lat, len_lat)
        four_l = _fourier_grid(abq, dft_side[0], dft_side[1], tw_cos, tw_sin, wf_blk, l, n_lat, GRID_W)
        hf_l, hb_l, _, _, _ = _mlstm(pq, gq, gtq, voq, state_C[:, l], state_n[:, l], state_m[:, l], tril, triu,
                                     n_lat, len_lat)
        xl, h2l, affl = _merge(att_l, four_l, hf_l, hb_l, voq, xl, mod_l, l, gh, g_mem, w_out_bf, g2, w_rt_bf,
                               ones_blk, n_lat, len_lat, len_lat, True)

        last = l == DEPTH - 1
        _, gs_c, sp_c = _route(affc, triu_bf, n_ctx, len_ctx, cap_ctx, n_ctx)
        idx_l, gs_l, _ = _route(affl, triu_bf, n_lat, len_lat, cap_lat, 1)
        xs_c = _gather_short(sp_c, h2c, n_ctx, len_ctx, cap_ctx)
        xs_l = _gather(idx_l, h2l, n_lat, cap_lat, 4)
        ys_c, ys_l = _experts(xs_c, xs_l, gs_c, gs_l, w_exp_gate, w_exp_up, w_exp_down, l)
        xc = _scatter_short(sp_c, ys_c, xc, mod_c, gf, n_ctx, len_ctx, cap_ctx, last)
        xl = _scatter(idx_l, ys_l, xl, mod_l, gf, n_lat, len_lat, cap_lat, 2, 512, last)

    y_prompt = xc.reshape(n_ctx, len_ctx, D_MODEL)
    y_sample = xl.reshape(n_lat, len_lat, D_MODEL)
    kv_shape = (n_ctx, DEPTH, len_ctx, H_ATT, HEAD_DIM)
    return (y_prompt, y_sample, new_k.reshape(kv_shape), new_v.reshape(kv_shape), jnp.stack(cs, axis=1),
            jnp.stack(ns, axis=1), jnp.stack(ms, axis=1))
```

```python
import functools

import numpy as np
import jax
import jax.numpy as jnp
from jax import lax
from jax.experimental import pallas as pl
from jax.experimental.pallas import tpu as pltpu

F32 = jnp.float32
BF16 = jnp.bfloat16

D_MODEL = 1024
DEPTH = 2
HEAD_DIM = 64
H_ATT = 8
G_FOURIER = 4
H_MLSTM = 4
A_W = H_ATT * HEAD_DIM
B_W = G_FOURIER * HEAD_DIM
C_W = H_MLSTM * HEAD_DIM
N_GATE_COLS = 16
P_COLS = 3 * A_W + B_W + 4 * C_W
IN_COLS = P_COLS + N_GATE_COLS
P_KEEP = 3 * A_W + B_W + 2 * C_W
GRID_W = 64
WIN_R = 8
WIN_C = 16
MLSTM_CHUNK = 128
N_EXPERTS = 16
CAPACITY_FACTOR = 2
EXPERT_FF = 2 * D_MODEL
EPS = 1e-6
NEG = -1e30

UB_OFF = 3 * A_W
QC_BLK, KC_BLK, VC_BLK, OC_BLK = 7, 8, 9, 10

LANE = 128
N_SLAB = D_MODEL // LANE
MERGE_TILE = 256
INPROJ_TILE = 512

NT_DIMS = (((1,), (1,)), ((), ()))
TN_DIMS = (((0,), (0,)), ((), ()))
MIB = 1024 * 1024


def _dot(a, b, precision=None):
    return jnp.dot(a, b, preferred_element_type=F32, precision=precision)


def _dot_nt(a, b, precision=None):
    return lax.dot_general(a, b, NT_DIMS, preferred_element_type=F32, precision=precision)


def _dot_tn(a, b):
    return lax.dot_general(a, b, TN_DIMS, preferred_element_type=F32)


def _params(sem, vmem_mib=48):
    return pltpu.CompilerParams(dimension_semantics=sem, vmem_limit_bytes=vmem_mib * MIB)


def _silu(x):
    return x * jax.nn.sigmoid(x)


def _log_sigmoid(x):
    return jnp.minimum(x, 0.0) - jnp.log1p(jnp.exp(-jnp.abs(x)))


def _mod_kernel(c_ref, w_ref, b_ref, o_ref):
    s = _silu(c_ref[...]).astype(BF16)
    o_ref[0] = _dot(s, w_ref[0].astype(BF16)) + b_ref[0]


def _modulation(cvecs, w_ada, b_ada):
    depth = w_ada.shape[0]
    tn = 1024
    return pl.pallas_call(
        _mod_kernel,
        grid=(depth, 6 * D_MODEL // tn),
        in_specs=[
            pl.BlockSpec((8, D_MODEL), lambda l, j: (0, 0)),
            pl.BlockSpec((1, D_MODEL, tn), lambda l, j: (l, 0, j)),
            pl.BlockSpec((1, 1, tn), lambda l, j: (l, 0, j)),
        ],
        out_specs=pl.BlockSpec((1, 8, tn), lambda l, j: (l, 0, j)),
        out_shape=jax.ShapeDtypeStruct((depth, 8, 6 * D_MODEL), F32),
        compiler_params=_params(("arbitrary", "arbitrary")),
        name="modulation",
    )(cvecs, w_ada, b_ada.reshape(depth, 1, 6 * D_MODEL))


def _inproj_kernel(x_ref, mod_ref, g1_ref, w_ref, wgt_ref, wvo_ref, bgr_ref, bgc_ref, csc_ref,
                   *rest, grid_rows, kv_seq_len):
    n_in = 2 if kv_seq_len else 0
    p_ref, g_ref, gt_ref, vo_ref, ab_ref = rest[n_in:n_in + 5]
    kv_refs = rest[n_in + 5:n_in + 5 + n_in]
    scratch = rest[n_in + 5 + n_in:]
    x = x_ref[...]
    y = x * lax.rsqrt(jnp.mean(x * x, axis=-1, keepdims=True) + EPS) * g1_ref[0]
    h = (y * (1.0 + mod_ref[0, 1:2, :]) + mod_ref[0, 0:1, :]).astype(BF16)
    for j in range(0, P_KEEP, 256):
        pj = _dot(h, w_ref[0, :, j:j + 256])
        p_ref[:, j:j + 256] = pj.astype(BF16)
        if kv_seq_len and A_W <= j < 3 * A_W:
            kv_ref = kv_refs[(j - A_W) // A_W]
            c0 = (j - A_W) % A_W
            for b in range(pj.shape[0] // kv_seq_len):
                kv_ref[b, 0, :, c0:c0 + 256] = pj[kv_seq_len * b:kv_seq_len * (b + 1), :]
        if j == UB_OFF:
            ab = _dot(pj.astype(BF16), csc_ref[...])
            if grid_rows:
                stage_ref, = scratch
                n_lt = 2 * B_W // LANE
                for c in range(n_lt):
                    stage_ref[c] = ab[:, LANE * c:LANE * (c + 1)]
                for n2 in range(GRID_W):
                    for c in range(n_lt):
                        col = 2 * B_W * n2 + LANE * c
                        ab_ref[:, col:col + LANE] = stage_ref[c, pl.ds(n2, grid_rows, stride=GRID_W), :]
            else:
                ab_ref[...] = ab.astype(BF16)
    g_ref[...] = _dot(h, w_ref[0, :, P_COLS:IN_COLS]) + bgr_ref[0]
    gt_ref[...] = _dot_nt(wgt_ref[0], h) + bgc_ref[0]
    vo_ref[...] = _dot_nt(wvo_ref[0], h)


def _inproj(x2d, mod, layer, g1, w_in_bf, w_gt_bf, w_vo_bf, bg_row, bg_col, csc, seq_len, grid_ab, kv_cache=None):
    t = x2d.shape[0]
    tm = INPROJ_TILE
    tiles_per_seq = seq_len // tm
    grid_rows = tm // GRID_W if grid_ab else 0
    if grid_ab:
        ab_spec = pl.BlockSpec((grid_rows, GRID_W * 2 * B_W), lambda i: (i, 0))
        ab_shape = jax.ShapeDtypeStruct((t // GRID_W, GRID_W * 2 * B_W), F32)
        scratch = [pltpu.VMEM((2 * B_W // LANE, tm, LANE), F32)]
    else:
        ab_spec = pl.BlockSpec((tm, 2 * B_W), lambda i: (i, 0))
        ab_shape = jax.ShapeDtypeStruct((t, 2 * B_W), BF16)
        scratch = []
    kv_seq_len = kv_cache[0].shape[2] if kv_cache else 0
    kv_in_specs, kv_out_specs, kv_shapes, aliases = [], [], [], {}
    if kv_cache:
        per_tile = tm // kv_seq_len
        kv_in_specs = [pl.BlockSpec(memory_space=pl.ANY)] * 2
        kv_out_specs = [pl.BlockSpec((per_tile, 1, kv_seq_len, A_W), lambda i: (i, layer, 0, 0))] * 2
        kv_shapes = [jax.ShapeDtypeStruct(a.shape, a.dtype) for a in kv_cache]
        aliases = {9: 5, 10: 6}
    return pl.pallas_call(
        functools.partial(_inproj_kernel, grid_rows=grid_rows, kv_seq_len=kv_seq_len),
        grid=(t // tm,),
        input_output_aliases=aliases,
        in_specs=[
            pl.BlockSpec((tm, D_MODEL), lambda i: (i, 0)),
            pl.BlockSpec((1, 6, D_MODEL), lambda i: (i // tiles_per_seq, 0, 0)),
            pl.BlockSpec((1, 1, D_MODEL), lambda i: (layer, 0, 0)),
            pl.BlockSpec((1, D_MODEL, IN_COLS), lambda i: (layer, 0, 0)),
            pl.BlockSpec((1, N_GATE_COLS, D_MODEL), lambda i: (layer, 0, 0)),
            pl.BlockSpec((1, 2 * C_W, D_MODEL), lambda i: (layer, 0, 0)),
            pl.BlockSpec((1, 1, N_GATE_COLS), lambda i: (layer, 0, 0)),
            pl.BlockSpec((1, N_GATE_COLS, 1), lambda i: (layer, 0, 0)),
            pl.BlockSpec((B_W, 2 * B_W), lambda i: (0, 0)),
        ] + kv_in_specs,
        out_specs=[
            pl.BlockSpec((tm, P_KEEP), lambda i: (i, 0)),
            pl.BlockSpec((tm, N_GATE_COLS), lambda i: (i, 0)),
            pl.BlockSpec((N_GATE_COLS, tm), lambda i: (0, i)),
            pl.BlockSpec((2 * C_W, tm), lambda i: (0, i)),
            ab_spec,
        ] + kv_out_specs,
        out_shape=[
            jax.ShapeDtypeStruct((t, P_KEEP), BF16),
            jax.ShapeDtypeStruct((t, N_GATE_COLS), F32),
            jax.ShapeDtypeStruct((N_GATE_COLS, t), F32),
            jax.ShapeDtypeStruct((2 * C_W, t), F32),
            ab_shape,
        ] + kv_shapes,
        scratch_shapes=scratch,
        compiler_params=_params(("arbitrary",)),
        name="inproj",
    )(x2d, mod, g1, w_in_bf, w_gt_bf, w_vo_bf, bg_row, bg_col, csc, *(kv_cache or ()))


def _ctx_attn_kernel(q_ref, k_ref, v_ref, o_ref):
    scale = HEAD_DIM ** -0.5
    heads = range(H_ATT)
    sl = [slice(HEAD_DIM * h, HEAD_DIM * (h + 1)) for h in heads]
    s = [_dot_nt((q_ref[:, sl[h]] * scale).astype(BF16), k_ref[:, sl[h]].astype(BF16)) for h in heads]
    e = [jnp.exp(s[h] - jnp.max(s[h], axis=-1, keepdims=True)) for h in heads]
    w = [e[h] * (1.0 / jnp.sum(e[h], axis=-1, keepdims=True)) for h in heads]
    o_ref[...] = jnp.concatenate([_dot(w[h].astype(BF16), v_ref[:, sl[h]].astype(BF16)) for h in heads], axis=1)


def _ctx_attention(p, n_seq, seq_len):
    return pl.pallas_call(
        _ctx_attn_kernel,
        grid=(n_seq,),
        in_specs=[
            pl.BlockSpec((seq_len, A_W), lambda b: (b, 0)),
            pl.BlockSpec((seq_len, A_W), lambda b: (b, 1)),
            pl.BlockSpec((seq_len, A_W), lambda b: (b, 2)),
        ],
        out_specs=pl.BlockSpec((seq_len, A_W), lambda b: (b, 0)),
        out_shape=jax.ShapeDtypeStruct((n_seq * seq_len, A_W), F32),
        compiler_params=_params(("arbitrary",)),
        name="ctx_attention",
    )(p, p, p)


Q_ROWS = 8
K_ROWS = 16
KEY_BLK = 256


NO_ROW = 2 * WIN_R - 1


def _nbr_bias_tiles(rpb):
    n_c = 2 * WIN_C - 1
    cq = np.arange(GRID_W)[:, None]
    ck = np.arange(GRID_W)[None, :]
    cs = np.clip(cq - WIN_C // 2, 0, GRID_W - WIN_C)
    col_ok = (ck >= cs) & (ck < cs + WIN_C)
    pick = np.where(col_ok, np.clip(ck - cq + WIN_C - 1, 0, n_c - 1), n_c)
    sel = np.zeros((2, GRID_W, 2 * GRID_W, n_c + 1), np.float32)
    for side in range(2):
        sel[side, cq, side * GRID_W + ck, pick] = 1.0
    ext = jnp.full(rpb.shape[:2] + (2 * WIN_R, n_c + 1), NEG, F32)
    ext = ext.at[:, :, :2 * WIN_R - 1, :n_c].set(rpb.astype(F32))
    return jnp.einsum("lhdm,sqkm->lhdsqk", ext, jnp.asarray(sel), precision=lax.Precision.HIGHEST)


def _nbr_attn_kernel(q_ref, k0_ref, k1_ref, k2_ref, k3_ref, v0_ref, v1_ref, v2_ref, v3_ref,
                     ck_ref, cv_ref, tab_ref, o_ref):
    scale = HEAD_DIM ** -0.5
    k_refs = (k0_ref, k1_ref, k2_ref, k3_ref)
    v_refs = (v0_ref, v1_ref, v2_ref, v3_ref)
    rb = pl.program_id(1)
    rows = GRID_W

    def window_offsets():
        key_row0 = jnp.clip(Q_ROWS * rb - WIN_R // 2, 0, rows - K_ROWS)
        d = []
        for rq in range(Q_ROWS):
            r = Q_ROWS * rb + rq
            rs = jnp.clip(r - WIN_R // 2, 0, rows - WIN_R)
            d_row = []
            for rk in range(K_ROWS):
                rka = key_row0 + rk
                ok = (rka >= rs) & (rka < rs + WIN_R)
                d_row.append(jnp.where(ok, rka - r + WIN_R - 1, NO_ROW))
            d.append(d_row)
        return d

    d_interior = [[rk - rq + WIN_R // 2 - 1 if 0 <= rk - rq < WIN_R else NO_ROW for rk in range(K_ROWS)]
                  for rq in range(Q_ROWS)]

    rows_per_blk = KEY_BLK // GRID_W
    n_blk = K_ROWS // rows_per_blk
    half_rows = Q_ROWS // 2
    half_tok = half_rows * GRID_W
    n_rb = pl.num_programs(1)

    def attend(active, d):
        units = [(hh, half) for hh in range(2) for half in range(2)]
        sl = [slice(HEAD_DIM * hh, HEAD_DIM * (hh + 1)) for hh in range(2)]
        rows_of = [slice(half_tok * half, half_tok * (half + 1)) for half in range(2)]
        q = [(q_ref[rows_of[half], sl[hh]] * scale).astype(BF16) for hh, half in units]
        s_ctx = [_dot_nt(q[u], ck_ref[0, 0, :, sl[hh]].astype(BF16)) for u, (hh, half) in enumerate(units)]
        s_loc = []
        for u, (hh, half) in enumerate(units):
            blocks = []
            for j in active[half]:
                bias = jnp.concatenate([
                    jnp.concatenate([
                        tab_ref[0, hh, d[rq][rows_per_blk * j + 2 * p], 0]
                        + tab_ref[0, hh, d[rq][rows_per_blk * j + 2 * p + 1], 1]
                        for p in range(rows_per_blk // 2)], axis=1)
                    for rq in range(half_rows * half, half_rows * (half + 1))], axis=0)
                blocks.append(_dot_nt(q[u], k_refs[j][:, sl[hh]].astype(BF16)) + bias)
            s_loc.append(blocks)
        m = [jnp.max(s_ctx[u], axis=-1, keepdims=True) for u in range(len(units))]
        for u in range(len(units)):
            for s_blk in s_loc[u]:
                m[u] = jnp.maximum(m[u], jnp.max(s_blk, axis=-1, keepdims=True))
        e_ctx = [jnp.exp(s_ctx[u] - m[u]) for u in range(len(units))]
        den = [jnp.sum(e_ctx[u], axis=-1, keepdims=True) for u in range(len(units))]
        num = [_dot(e_ctx[u].astype(BF16), cv_ref[0, 0, :, sl[hh]].astype(BF16)) for u, (hh, half) in enumerate(units)]
        for u, (hh, half) in enumerate(units):
            for j, s_blk in zip(active[half], s_loc[u]):
                e = jnp.exp(s_blk - m[u])
                den[u] = den[u] + jnp.sum(e, axis=-1, keepdims=True)
                num[u] = num[u] + _dot(e.astype(BF16), v_refs[j][:, sl[hh]].astype(BF16))
        out = [num[u] / den[u] for u in range(len(units))]
        o_ref[...] = jnp.concatenate([jnp.concatenate([out[2 * hh], out[2 * hh + 1]], axis=0) for hh in range(2)],
                                     axis=1)

    interior = (rb > 0) & (rb < n_rb - 1)
    every = tuple(range(n_blk))

    @pl.when(interior)
    def _():
        attend((every[:-1], every[1:]), d_interior)

    @pl.when(jnp.logical_not(interior))
    def _():
        attend((every, every), window_offsets())


def _nbr_attention(p, cache_k4, cache_v4, bias_tiles, layer, n_seq, seq_len):
    q_tok = Q_ROWS * GRID_W
    n_rb = seq_len // q_tok
    kb_per_seq = seq_len // KEY_BLK
    max_base = kb_per_seq - 4

    def kmap(j, col0):
        def f(hp, rb, b):
            base = jnp.clip(2 * rb - 1, 0, max_base)
            return (b * kb_per_seq + base + j, col0 + hp)
        return f

    past = cache_k4.shape[2]
    in_specs = [pl.BlockSpec((q_tok, 128), lambda hp, rb, b: (b * n_rb + rb, hp))]
    in_specs += [pl.BlockSpec((KEY_BLK, 128), kmap(j, A_W // 128)) for j in range(4)]
    in_specs += [pl.BlockSpec((KEY_BLK, 128), kmap(j, 2 * A_W // 128)) for j in range(4)]
    in_specs += [
        pl.BlockSpec((1, 1, past, 128), lambda hp, rb, b: (b, layer, 0, hp)),
        pl.BlockSpec((1, 1, past, 128), lambda hp, rb, b: (b, layer, 0, hp)),
        pl.BlockSpec((1, 2, 2 * WIN_R, 2, GRID_W, 2 * GRID_W), lambda hp, rb, b: (layer, hp, 0, 0, 0, 0)),
    ]
    return pl.pallas_call(
        _nbr_attn_kernel,
        grid=(H_ATT // 2, n_rb, n_seq),
        in_specs=in_specs,
        out_specs=pl.BlockSpec((q_tok, 128), lambda hp, rb, b: (b * n_rb + rb, hp)),
        out_shape=jax.ShapeDtypeStruct((n_seq * seq_len, A_W), F32),
        compiler_params=_params(("arbitrary", "arbitrary", "arbitrary")),
        name="nbr_attention",
    )(p, p, p, p, p, p, p, p, p, cache_k4, cache_v4, bias_tiles)


def _dft_mats(n):
    idx = jnp.arange(n, dtype=jnp.int32)
    ang = ((idx[:, None] * idx[None, :]) % n).astype(F32) * (2.0 * np.pi / n)
    return jnp.cos(ang).astype(BF16), jnp.sin(ang).astype(BF16)


def _channel_dft():
    c = np.arange(HEAD_DIM)
    ang = 2.0 * np.pi * ((c[:, None] * c[None, :]) % HEAD_DIM) / HEAD_DIM
    eye = np.eye(G_FOURIER)
    mats = np.concatenate([np.kron(eye, np.cos(ang)), np.kron(eye, np.sin(ang))], axis=1)
    return jnp.asarray(mats, F32).astype(BF16)


def _fourier_kernel(c_ref, s_ref, ab_ref, wf_ref, o_ref, acc_ref, *, scale, n_k):
    k = pl.program_id(2)

    @pl.when(k == 0)
    def _():
        acc_ref[...] = jnp.zeros_like(acc_ref)

    acc_ref[...] += _dot(c_ref[...], ab_ref[:, :B_W]) - _dot(s_ref[...], ab_ref[:, B_W:])

    @pl.when(k == n_k - 1)
    def _():
        z = (acc_ref[...] * scale).astype(BF16)
        o = _dot(z, wf_ref[0])
        for c in range(B_W // LANE):
            o_ref[c] = o[:, LANE * c:LANE * (c + 1)]


def _fourier(ab, cmat, smat, wf_blk, layer, n_seq, seq_len):
    ti = min(seq_len, 512)
    tk = min(seq_len, 1024)
    n_i, n_k = seq_len // ti, seq_len // tk
    scale = float((seq_len * HEAD_DIM) ** -0.5)
    return pl.pallas_call(
        functools.partial(_fourier_kernel, scale=scale, n_k=n_k),
        grid=(n_seq, n_i, n_k),
        in_specs=[
            pl.BlockSpec((ti, tk), lambda s, i, k: (i, k)),
            pl.BlockSpec((ti, tk), lambda s, i, k: (i, k)),
            pl.BlockSpec((tk, 2 * B_W), lambda s, i, k: (s * n_k + k, 0)),
            pl.BlockSpec((1, B_W, B_W), lambda s, i, k: (layer, 0, 0)),
        ],
        out_specs=pl.BlockSpec((B_W // LANE, ti, LANE), lambda s, i, k: (0, s * n_i + i, 0)),
        out_shape=jax.ShapeDtypeStruct((B_W // LANE, n_seq * seq_len, LANE), F32),
        scratch_shapes=[pltpu.VMEM((ti, B_W), F32)],
        compiler_params=_params(("arbitrary", "arbitrary", "arbitrary")),
        name="fourier",
    )(cmat, smat, ab, wf_blk)


FS_GROUP = 8


def _twiddles(side, n):
    k1 = jnp.arange(side, dtype=jnp.int32)[:, None]
    n2 = jnp.arange(side, dtype=jnp.int32)[None, :]
    ang = (k1 * n2).astype(F32) * (2.0 * np.pi / n)
    return jnp.cos(ang), jnp.sin(ang)


def _fourier_grid_kernel(ab_ref, c_ref, s_ref, tc_ref, ts_ref, wf_ref, o_ref, y_ref, *, side, scale):
    cmat = c_ref[...]
    smat = s_ref[...]
    for g in range(side // FS_GROUP):
        ab = ab_ref[:, 2 * B_W * FS_GROUP * g:2 * B_W * FS_GROUP * (g + 1)].astype(BF16)
        m1 = _dot(cmat, ab)
        m2 = _dot(smat, ab)
        for t in range(FS_GROUP):
            n2 = FS_GROUP * g + t
            a0 = 2 * B_W * t
            yr = m1[:, a0:a0 + B_W] - m2[:, a0 + B_W:a0 + 2 * B_W]
            yi = -(m1[:, a0 + B_W:a0 + 2 * B_W] + m2[:, a0:a0 + B_W])
            ct = tc_ref[:, n2:n2 + 1]
            st = ts_ref[:, n2:n2 + 1]
            y = jnp.concatenate([yr * ct + yi * st, yi * ct - yr * st], axis=1)
            for c in range(2 * B_W // LANE):
                for hi in range(side // 8):
                    r0 = (hi * side + n2) * 8
                    y_ref[c, r0:r0 + 8, :] = y[8 * hi:8 * (hi + 1), LANE * c:LANE * (c + 1)]
    for g in range(side // FS_GROUP):
        zs = []
        for j in range(FS_GROUP):
            k1 = FS_GROUP * g + j
            rows = pl.ds((k1 // 8) * side * 8 + k1 % 8, side, stride=8)
            n_lt = B_W // LANE
            y_re = jnp.concatenate([y_ref[c, rows, :] for c in range(n_lt)], axis=1).astype(BF16)
            y_im = jnp.concatenate([y_ref[n_lt + c, rows, :] for c in range(n_lt)], axis=1).astype(BF16)
            zs.append(_dot(cmat, y_re) + _dot(smat, y_im))
        z = (jnp.concatenate(zs, axis=0) * scale).astype(BF16)
        o = _dot(z, wf_ref[0])
        for j in range(FS_GROUP):
            for c in range(B_W // LANE):
                o_ref[c, pl.ds(FS_GROUP * g + j, side, stride=side), :] = o[side * j:side * (j + 1),
                                                                            LANE * c:LANE * (c + 1)]


def _fourier_grid(ab_grid, cmat, smat, tw_cos, tw_sin, wf_blk, layer, n_seq, side):
    seq_len = side * side
    scale = float((seq_len * HEAD_DIM) ** -0.5)
    small = pl.BlockSpec((side, side), lambda s: (0, 0))
    return pl.pallas_call(
        functools.partial(_fourier_grid_kernel, side=side, scale=scale),
        grid=(n_seq,),
        in_specs=[
            pl.BlockSpec((side, side * 2 * B_W), lambda s: (s, 0)),
            small,
            small,
            small,
            small,
            pl.BlockSpec((1, B_W, B_W), lambda s: (layer, 0, 0)),
        ],
        out_specs=pl.BlockSpec((B_W // LANE, seq_len, LANE), lambda s: (0, s, 0)),
        out_shape=jax.ShapeDtypeStruct((B_W // LANE, n_seq * seq_len, LANE), F32),
        scratch_shapes=[pltpu.VMEM((2 * B_W // LANE, seq_len, LANE), F32)],
        compiler_params=_params(("arbitrary",), vmem_mib=56),
        name="fourier_grid",
    )(ab_grid, cmat, smat, tw_cos, tw_sin, wf_blk)


SEQS_PER_STEP = 2


def _mlstm_kernel(*refs, n_chunks):
    sps = SEQS_PER_STEP
    n_side = 3 + 2 * sps
    fwd, bwd = refs[:n_side], refs[n_side:2 * n_side]
    c0_ref, n0_ref, m0_ref, tril_ref, triu_ref = refs[2 * n_side:2 * n_side + 5]
    hf_ref, hb_ref, cout_ref, nout_ref, mout_ref, c_s, n_s, m_s = refs[2 * n_side + 5:]
    c = pl.program_id(1)
    hi = lax.Precision.HIGHEST
    lc = MLSTM_CHUNK
    pair_w = 2 * HEAD_DIM
    n_pairs = H_MLSTM // 2

    @pl.when(c == 0)
    def _():
        c_s[...] = c0_ref[...]
        n_s[...] = n0_ref[...]
        m_s[...] = m0_ref[...]

    lo_lane = lax.broadcasted_iota(jnp.int32, (1, pair_w), 1) < HEAD_DIM
    lo_row = lax.broadcasted_iota(jnp.int32, (pair_w, 1), 0) < HEAD_DIM
    row8 = lax.broadcasted_iota(jnp.int32, (8, 1), 0)
    cum_mask = [tril_ref[...], triu_ref[...]]
    keep_t = [triu_ref[...] > 0.5, tril_ref[...] > 0.5]
    pairs = [(j, d, hp) for j in range(sps) for d in range(2) for hp in range(n_pairs)]
    heads = [(pi, hh) for pi in range(len(pairs)) for hh in range(2)]
    rng = range(len(heads))

    pre = {}
    for j in range(sps):
        for d, side in enumerate((fwd, bwd)):
            q_ref, k_ref, g_ref = side[:3]
            gt_ref, vt_ref = side[3 + 2 * j], side[4 + 2 * j]
            go = 2 * H_MLSTM * d
            lf_cols = _log_sigmoid(g_ref[j, :, go + H_MLSTM:go + 2 * H_MLSTM])
            lf_rows = _log_sigmoid(gt_ref[go + H_MLSTM:go + 2 * H_MLSTM, :])
            b_cols = _dot(cum_mask[d], lf_cols, precision=hi)
            pre[j, d] = dict(
                a_cols=g_ref[j, :, go:go + H_MLSTM] - b_cols,
                ig_rows=gt_ref[go:go + H_MLSTM, :],
                b_rows=_dot_nt(lf_rows, cum_mask[d], precision=hi),
                q=q_ref[j].astype(BF16), k=(k_ref[j] * (HEAD_DIM ** -0.5)).astype(BF16), vt=vt_ref[...])

    def pair_cols(hp):
        return slice(pair_w * hp, pair_w * (hp + 1))

    q_p = [pre[j, d]["q"][:, pair_cols(hp)] for j, d, hp in pairs]
    k_p = [pre[j, d]["k"][:, pair_cols(hp)] for j, d, hp in pairs]
    vt_p = [pre[j, d]["vt"][pair_cols(hp), :] for j, d, hp in pairs]
    c_p = [c_s[j, d, hp] for j, d, hp in pairs]
    n_p = [n_s[j, n_pairs * d + hp:n_pairs * d + hp + 1, :] for j, d, hp in pairs]
    zero_k = jnp.zeros((lc, pair_w), BF16)
    k_h = [jnp.where(lo_lane, k_p[pi], zero_k) if hh == 0 else jnp.where(lo_lane, zero_k, k_p[pi])
           for pi, hh in heads]

    def head_of(i):
        pi, hh = heads[i]
        j, d, hp = pairs[pi]
        return j, d, 2 * hp + hh

    b_row = [pre[head_of(i)[0], head_of(i)[1]]["b_rows"][head_of(i)[2]:head_of(i)[2] + 1, :] for i in rng]
    ig_row = [pre[head_of(i)[0], head_of(i)[1]]["ig_rows"][head_of(i)[2]:head_of(i)[2] + 1, :] for i in rng]
    a_col = [pre[head_of(i)[0], head_of(i)[1]]["a_cols"][:, head_of(i)[2]:head_of(i)[2] + 1] for i in rng]
    m_st = [m_s[head_of(i)[0], H_MLSTM * head_of(i)[1] + head_of(i)[2]:H_MLSTM * head_of(i)[1] + head_of(i)[2] + 1, :]
            for i in rng]
    bl = [b_row[i][:, lc - 1:lc] if head_of(i)[1] == 0 else b_row[i][:, 0:1] for i in rng]

    d_t = [jnp.where(keep_t[head_of(i)[1]], b_row[i] + a_col[i], NEG) for i in rng]
    inter = [b_row[i] + m_st[i] for i in rng]
    m_t = [jnp.maximum(inter[i], jnp.max(d_t[i], axis=0, keepdims=True)) for i in rng]
    s_t = [_dot_nt(k_h[i], q_p[heads[i][0]]) * jnp.exp(d_t[i] - m_t[i]) for i in rng]
    w_in = [jnp.exp(inter[i] - m_t[i]) for i in rng]
    num_t = [_dot(vt_p[heads[i][0]].astype(BF16), s_t[i].astype(BF16)) for i in rng]
    qc_t = [_dot_nt(c_p[pi].astype(BF16), q_p[pi]) for pi in range(len(pairs))]
    n_mat = [jnp.where((row8 == 0) & lo_lane, n_p[pi], jnp.where((row8 == 1) & ~lo_lane, n_p[pi], 0.0))
             for pi in range(len(pairs))]
    nq = [_dot_nt(n_mat[pi].astype(BF16), q_p[pi]) for pi in range(len(pairs))]
    den = [jnp.sum(s_t[i], axis=0, keepdims=True) + w_in[i] * nq[heads[i][0]][heads[i][1]:heads[i][1] + 1, :]
           for i in rng]
    inv = [1.0 / jnp.maximum(jnp.abs(den[i]), jnp.exp(-m_t[i])) for i in rng]
    h_t = []
    for pi in range(len(pairs)):
        i0, i1 = 2 * pi, 2 * pi + 1
        num = jnp.where(lo_row, num_t[i0], num_t[i1])
        h_t.append((num + jnp.where(lo_row, w_in[i0], w_in[i1]) * qc_t[pi]) * jnp.where(lo_row, inv[i0], inv[i1]))
    for j in range(sps):
        base = 2 * n_pairs * j
        hf_ref[j] = jnp.concatenate(h_t[base:base + n_pairs], axis=0)
        hb_ref[j] = jnp.concatenate(h_t[base + n_pairs:base + 2 * n_pairs], axis=0)

    g_row = [bl[i] - b_row[i] + ig_row[i] for i in rng]
    m_new = [jnp.maximum(bl[i] + m_st[i], jnp.max(g_row[i], axis=-1, keepdims=True)) for i in rng]
    wc = [jnp.exp(bl[i] + m_st[i] - m_new[i]) for i in rng]
    ws_row = [jnp.exp(g_row[i] - m_new[i]) for i in rng]
    upd = [_dot((vt_p[heads[i][0]] * ws_row[i]).astype(BF16), k_p[heads[i][0]]) for i in rng]
    for pi, (j, d, hp) in enumerate(pairs):
        i0, i1 = 2 * pi, 2 * pi + 1
        block = jnp.where(lo_row & lo_lane, upd[i0], jnp.where(~lo_row & ~lo_lane, upd[i1], 0.0))
        c_s[j, d, hp] = jnp.where(lo_row, wc[i0], wc[i1]) * c_p[pi] + block
        ws_mat = jnp.where(row8 == 0, ws_row[i0], jnp.where(row8 == 1, ws_row[i1], 0.0))
        k_sum = _dot(ws_mat.astype(BF16), k_p[pi])
        row = n_pairs * d + hp
        n_s[j, row:row + 1, :] = (jnp.where(lo_lane, wc[i0], wc[i1]) * n_p[pi]
                                  + jnp.where(lo_lane, k_sum[0:1, :], k_sum[1:2, :]))
    for i in rng:
        j, d, hd = head_of(i)
        m_s[j, H_MLSTM * d + hd:H_MLSTM * d + hd + 1, :] = m_new[i]

    @pl.when(c == n_chunks - 1)
    def _():
        cout_ref[...] = c_s[...]
        nout_ref[...] = n_s[...]
        mout_ref[...] = m_s[...]


def _pair_states(c):
    b = c.shape[0]
    c = c.reshape(b, 2, H_MLSTM // 2, 2, HEAD_DIM, HEAD_DIM)
    zero = jnp.zeros_like(c[:, :, :, 0])
    top = jnp.concatenate([c[:, :, :, 0], zero], axis=-1)
    bottom = jnp.concatenate([zero, c[:, :, :, 1]], axis=-1)
    return jnp.concatenate([top, bottom], axis=-2)


def _unpair_states(cp):
    b = cp.shape[0]
    first = cp[:, :, :, :HEAD_DIM, :HEAD_DIM]
    second = cp[:, :, :, HEAD_DIM:, HEAD_DIM:]
    return jnp.stack([first, second], axis=3).reshape(b, 2, H_MLSTM, HEAD_DIM, HEAD_DIM)


def _mlstm(p, g, gt, vo, c0, n0, m0, tril, triu, n_seq, seq_len):
    lc = MLSTM_CHUNK
    nc = seq_len // lc
    n_st = 2 * H_MLSTM
    n_pairs = H_MLSTM // 2
    pair_w = 2 * HEAD_DIM
    sps = SEQS_PER_STEP
    p3 = p.reshape(n_seq, seq_len, P_KEEP)
    g3 = g.reshape(n_seq, seq_len, N_GATE_COLS)

    def fwd(c):
        return c

    def bwd(c):
        return nc - 1 - c

    def side(chunk):
        tok = lambda col: (lambda b, c: (b, chunk(c), col))
        specs = [
            pl.BlockSpec((sps, lc, C_W), tok(QC_BLK)),
            pl.BlockSpec((sps, lc, C_W), tok(KC_BLK)),
            pl.BlockSpec((sps, lc, N_GATE_COLS), tok(0)),
        ]
        for j in range(sps):
            lanes = lambda b, c, j=j: (0, (b * sps + j) * nc + chunk(c))
            specs += [pl.BlockSpec((N_GATE_COLS, lc), lanes), pl.BlockSpec((C_W, lc), lanes)]
        return specs

    state_specs = [
        pl.BlockSpec((sps, 2, n_pairs, pair_w, pair_w), lambda b, c: (b, 0, 0, 0, 0)),
        pl.BlockSpec((sps, 2 * n_pairs, pair_w), lambda b, c: (b, 0, 0)),
        pl.BlockSpec((sps, n_st, 1), lambda b, c: (b, 0, 0)),
    ]
    tri_spec = pl.BlockSpec((lc, lc), lambda b, c: (0, 0))
    operands = [p3, p3, g3] + [gt, vo] * sps
    hf, hb, c_out, n_out, m_out = pl.pallas_call(
        functools.partial(_mlstm_kernel, n_chunks=nc),
        grid=(n_seq // sps, nc),
        in_specs=side(fwd) + side(bwd) + state_specs + [tri_spec, tri_spec],
        out_specs=[
            pl.BlockSpec((sps, C_W, lc), lambda b, c: (b, 0, c)),
            pl.BlockSpec((sps, C_W, lc), lambda b, c: (b, 0, nc - 1 - c)),
        ] + state_specs,
        out_shape=[
            jax.ShapeDtypeStruct((n_seq, C_W, seq_len), F32),
            jax.ShapeDtypeStruct((n_seq, C_W, seq_len), F32),
            jax.ShapeDtypeStruct((n_seq, 2, n_pairs, pair_w, pair_w), F32),
            jax.ShapeDtypeStruct((n_seq, 2 * n_pairs, pair_w), F32),
            jax.ShapeDtypeStruct((n_seq, n_st, 1), F32),
        ],
        scratch_shapes=[
            pltpu.VMEM((sps, 2, n_pairs, pair_w, pair_w), F32),
            pltpu.VMEM((sps, 2 * n_pairs, pair_w), F32),
            pltpu.VMEM((sps, n_st, 1), F32),
        ],
        compiler_params=_params(("arbitrary", "arbitrary")),
        name="mlstm",
    )(*operands, *operands, _pair_states(c0), n0.reshape(n_seq, 2 * n_pairs, pair_w), m0.reshape(n_seq, n_st, 1),
      tril, triu)
    return (hf, hb, _unpair_states(c_out), n_out.reshape(n_seq, 2, H_MLSTM, HEAD_DIM),
            m_out.reshape(n_seq, 2, H_MLSTM))


def _head_norm(y, g, ones_blk):
    ysq = y * y
    hi = ysq.astype(BF16)
    lo = (ysq - hi.astype(F32)).astype(BF16)
    ss = _dot(hi, ones_blk) + _dot(lo, ones_blk)
    return y * lax.rsqrt(ss * (1.0 / HEAD_DIM) + EPS) * g


def _merge_kernel(att_ref, four_ref, hf_ref, hb_ref, oc_ref, x_ref, mod_ref, gh_ref, ghm_ref, wo_ref, g2_ref, wrt_ref,
                  ones_ref, xo_ref, h2_ref, afft_ref, *, row_tiles):
    gh = gh_ref[0]
    half = x_ref.shape[0] // 2
    parts = [slice(half * p, half * (p + 1)) for p in range(2)]
    ya = [_head_norm(att_ref[r, :], gh[:, :A_W], ones_ref[...]) for r in parts]
    four = [jnp.concatenate([four_ref[c, r, :] for c in range(B_W // LANE)], axis=1) for r in parts]
    yf = [_head_norm(f, gh[:, A_W:A_W + B_W], ones_ref[:B_W, :B_W]) for f in four]
    mem = [hf_ref[0, :, r] + hb_ref[0, :, r] for r in parts]
    ym_t = []
    for p, r in enumerate(parts):
        heads = []
        for hd in range(H_MLSTM):
            y = mem[p][HEAD_DIM * hd:HEAD_DIM * (hd + 1), :]
            heads.append(y * lax.rsqrt(jnp.mean(y * y, axis=0, keepdims=True) + EPS))
        ym_t.append(jnp.concatenate(heads, axis=0) * ghm_ref[0, :, r] * jax.nn.sigmoid(oc_ref[:, r]))
    out = [_dot(ya[p].astype(BF16), wo_ref[0, :A_W, :])
           + _dot(yf[p].astype(BF16), wo_ref[0, A_W:A_W + B_W, :])
           + _dot_tn(ym_t[p].astype(BF16), wo_ref[0, A_W + B_W:, :]) for p in range(2)]
    x = [x_ref[r, :] + mod_ref[0, 2:3, :] * out[p] for p, r in enumerate(parts)]
    y2 = [xp * lax.rsqrt(jnp.mean(xp * xp, axis=-1, keepdims=True) + EPS) * g2_ref[0] for xp in x]
    h2 = [(yp * (1.0 + mod_ref[0, 4:5, :]) + mod_ref[0, 3:4, :]).astype(BF16) for yp in y2]
    logits = [_dot_nt(wrt_ref[0], hp) for hp in h2]
    e = [jnp.exp(lg - jnp.max(lg, axis=0, keepdims=True)) for lg in logits]
    for p, r in enumerate(parts):
        xo_ref[r, :] = x[p]
        if row_tiles:
            h2_wide = h2[p].astype(F32)
            for s in range(N_SLAB):
                h2_ref[pl.ds(N_SLAB * half * p + s, half, stride=N_SLAB), :] = h2_wide[:, LANE * s:LANE * (s + 1)]
        else:
            h2_ref[r, :] = h2[p]
        afft_ref[:, r] = e[p] / jnp.sum(e[p], axis=0, keepdims=True)


def _merge(att, four, hf, hb, vo, x2d, mod, layer, g_head, g_mem, w_out_bf, g2, w_rt_bf, ones_blk, n_seq, seq_len,
           mod_seq_len, row_tiles):
    t = x2d.shape[0]
    tm = MERGE_TILE
    tiles_per_mod = mod_seq_len // tm
    tiles_per_seq = seq_len // tm
    row = lambda i: (i, 0)
    lay = lambda i: (layer, 0, 0)
    mem = lambda i: (i // tiles_per_seq, 0, i % tiles_per_seq)
    if row_tiles:
        h2_spec, h2_shape = pl.BlockSpec((N_SLAB * tm, LANE), row), jax.ShapeDtypeStruct((N_SLAB * t, LANE), F32)
    else:
        h2_spec, h2_shape = pl.BlockSpec((tm, D_MODEL), row), jax.ShapeDtypeStruct((t, D_MODEL), BF16)
    return pl.pallas_call(
        functools.partial(_merge_kernel, row_tiles=row_tiles),
        grid=(t // tm,),
        in_specs=[
            pl.BlockSpec((tm, A_W), row),
            pl.BlockSpec((B_W // LANE, tm, LANE), lambda i: (0, i, 0)),
            pl.BlockSpec((1, C_W, tm), mem),
            pl.BlockSpec((1, C_W, tm), mem),
            pl.BlockSpec((C_W, tm), lambda i: (1, i)),
            pl.BlockSpec((tm, D_MODEL), row),
            pl.BlockSpec((1, 6, D_MODEL), lambda i: (i // tiles_per_mod, 0, 0)),
            pl.BlockSpec((1, 1, D_MODEL), lay),
            pl.BlockSpec((1, C_W, tm), lay),
            pl.BlockSpec((1, D_MODEL, D_MODEL), lay),
            pl.BlockSpec((1, 1, D_MODEL), lay),
            pl.BlockSpec((1, N_EXPERTS, D_MODEL), lay),
            pl.BlockSpec((A_W, A_W), lambda i: (0, 0)),
        ],
        out_specs=[
            pl.BlockSpec((tm, D_MODEL), row),
            h2_spec,
            pl.BlockSpec((N_EXPERTS, tm), lambda i: (0, i)),
        ],
        out_shape=[
            jax.ShapeDtypeStruct((t, D_MODEL), F32),
            h2_shape,
            jax.ShapeDtypeStruct((N_EXPERTS, t), F32),
        ],
        compiler_params=_params(("arbitrary",)),
        name="merge",
    )(att, four, hf, hb, vo, x2d, mod, g_head, g_mem, w_out_bf, g2, w_rt_bf, ones_blk)


BISECT_STEPS = 48
TOKEN_CHUNK = 1024


TOKEN_SPLIT = 64


def _route_kernel(aff_ref, triu_ref, idx_ref, gs_ref, sp_ref, *blocked, ns, seq_len, cap):
    seqs = range(ns)
    aff = [aff_ref[:, seq_len * j:seq_len * (j + 1)] for j in seqs]

    def body(_, bounds):
        out = []
        for j in seqs:
            lo, hi = bounds[j]
            mid = 0.5 * (lo + hi)
            ge = jnp.sum(jnp.where(aff[j] >= mid, 1.0, 0.0), axis=1, keepdims=True) >= cap
            out.append((jnp.where(ge, mid, lo), jnp.where(ge, hi, mid)))
        return tuple(out)

    start = (jnp.zeros((N_EXPERTS, 1), F32), jnp.full((N_EXPERTS, 1), 2.0, F32))
    bounds = lax.fori_loop(0, BISECT_STEPS, body, tuple(start for _ in seqs))
    thr = [jnp.max(jnp.where(aff[j] < bounds[j][1], aff[j], -1.0), axis=1, keepdims=True) for j in seqs]
    need = [cap - jnp.sum(jnp.where(aff[j] > thr[j], 1.0, 0.0), axis=1, keepdims=True) for j in seqs]
    triu = triu_ref[...]
    eq_carry = [jnp.zeros((N_EXPERTS, 1), F32) for _ in seqs]
    pos_carry = [jnp.zeros((N_EXPERTS, 1), F32) for _ in seqs]
    n_blk = seq_len // 128
    lane = lax.broadcasted_iota(jnp.int32, (1, 128), 1)
    blocked_refs = blocked if blocked else None
    first = [jnp.zeros((N_EXPERTS, 128), F32) for _ in seqs]
    after = [jnp.full((N_EXPERTS, 128), 1e9, F32) for _ in seqs]
    for b in range(n_blk):
        for j in seqs:
            blk = aff[j][:, 128 * b:128 * (b + 1)]
            eq = blk == thr[j]
            eq_f = jnp.where(eq, 1.0, 0.0)
            eq_inc = _dot(eq_f.astype(BF16), triu) + eq_carry[j]
            sel = (blk > thr[j]) | (eq & (eq_inc - eq_f < need[j]))
            sel_f = jnp.where(sel, 1.0, 0.0)
            pos_inc = _dot(sel_f.astype(BF16), triu) + pos_carry[j]
            t0 = seq_len * j + 128 * b
            sp_ref[:, t0:t0 + 128] = jnp.where(sel, pos_inc - sel_f, -1.0).astype(jnp.int32)
            if blocked_refs:
                cum_s, aff_s, _ = blocked_refs
                r0 = (j * n_blk + b) * N_EXPERTS
                cum_s[r0:r0 + N_EXPERTS, :] = pos_inc - pos_carry[j]
                aff_s[r0:r0 + N_EXPERTS, :] = blk
                first[j] = jnp.where(lane == b, pos_carry[j], first[j])
                after[j] = jnp.where(lane == b, pos_inc[:, 127:128], after[j])
            eq_carry[j] = eq_inc[:, 127:128]
            pos_carry[j] = pos_inc[:, 127:128]

    if blocked_refs:
        cum_s, aff_s, bnd_s = blocked_refs
        for j in seqs:
            bnd_s[N_EXPERTS * j:N_EXPERTS * (j + 1), :] = first[j]
            bnd_s[N_EXPERTS * (ns + j):N_EXPERTS * (ns + j + 1), :] = after[j]
        slot_col = lax.broadcasted_iota(jnp.int32, (cap, 1), 0).astype(F32)
        lane_f = lane.astype(F32)

        def per_expert(e, carry):
            for j in seqs:
                first_row = bnd_s[pl.ds(N_EXPERTS * j + e, 1), :]
                after_row = bnd_s[pl.ds(N_EXPERTS * (ns + j) + e, 1), :]
                blk_of = jnp.sum(jnp.where(after_row <= slot_col, 1.0, 0.0), axis=1, keepdims=True)
                in_blk = lane_f == blk_of
                local = slot_col - jnp.sum(jnp.where(in_blk, first_row, 0.0), axis=1, keepdims=True)
                pick = jnp.where(in_blk, 1.0, 0.0)[:, :n_blk].astype(BF16)
                rows = pl.ds(N_EXPERTS * n_blk * j + e, n_blk, stride=N_EXPERTS)
                counts = _dot(pick, cum_s[rows, :].astype(BF16))
                tok_in = jnp.sum(jnp.where(counts <= local, 1.0, 0.0), axis=1, keepdims=True)
                a = aff_s[rows, :]
                a_hi = a.astype(BF16)
                a_mid = (a - a_hi.astype(F32)).astype(BF16)
                a_lo = (a - a_hi.astype(F32) - a_mid.astype(F32)).astype(BF16)
                gates = _dot(pick, a_hi) + _dot(pick, a_mid) + _dot(pick, a_lo)
                idx_ref[N_EXPERTS * j + e] = (N_SLAB * (128.0 * blk_of + tok_in)).astype(jnp.int32)
                gs_ref[e, cap * j:cap * (j + 1)] = jnp.sum(jnp.where(lane_f == tok_in, gates, 0.0), axis=1,
                                                           keepdims=True)
            return carry

        lax.fori_loop(0, N_EXPERTS, per_expert, 0)
        return

    tc = min(seq_len, TOKEN_CHUNK)
    slot = lax.broadcasted_iota(jnp.int32, (cap, tc), 0)
    part = lax.broadcasted_iota(jnp.int32, (8, tc), 0)
    tok = lax.broadcasted_iota(jnp.int32, (1, tc), 1).astype(F32)
    chunks = range(0, seq_len, tc)
    tok_hi = [jnp.floor((tok + float(t0)) * (1.0 / TOKEN_SPLIT)) for t0 in chunks]
    tok_lo = [tok + float(t0) - TOKEN_SPLIT * hi for t0, hi in zip(chunks, tok_hi)]

    def per_expert(e, carry):
        for j in seqs:
            acc = jnp.zeros((cap, 8), F32)
            for ci, t0 in enumerate(chunks):
                cols = slice(seq_len * j + t0, seq_len * j + t0 + tc)
                onehot = jnp.where(slot == sp_ref[pl.ds(e, 1), cols], 1.0, 0.0).astype(BF16)
                a = aff_ref[pl.ds(e, 1), cols]
                a_hi = a.astype(BF16).astype(F32)
                a_mid = (a - a_hi).astype(BF16).astype(F32)
                a_lo = a - a_hi - a_mid
                vals = jnp.where(part == 0, tok_hi[ci], jnp.where(part == 1, tok_lo[ci], jnp.where(
                    part == 2, a_hi, jnp.where(part == 3, a_mid, jnp.where(part == 4, a_lo, 0.0)))))
                acc = acc + _dot_nt(onehot, vals.astype(BF16))
            idx_ref[N_EXPERTS * j + e] = (N_SLAB * (TOKEN_SPLIT * acc[:, 0:1] + acc[:, 1:2])).astype(jnp.int32)
            gs_ref[e, cap * j:cap * (j + 1)] = acc[:, 2:3] + acc[:, 3:4] + acc[:, 4:5]
        return carry

    lax.fori_loop(0, N_EXPERTS, per_expert, 0)


def _route(afft, triu_bf, n_seq, seq_len, cap, ns):
    n_blk = seq_len // 128
    idx, gs, sp = pl.pallas_call(
        functools.partial(_route_kernel, ns=ns, seq_len=seq_len, cap=cap),
        grid=(n_seq // ns,),
        in_specs=[
            pl.BlockSpec((N_EXPERTS, ns * seq_len), lambda s: (0, s)),
            pl.BlockSpec((128, 128), lambda s: (0, 0)),
        ],
        out_specs=[
            pl.BlockSpec((ns * N_EXPERTS, cap, 1), lambda s: (s, 0, 0)),
            pl.BlockSpec((N_EXPERTS, ns * cap, 1), lambda s: (0, s, 0)),
            pl.BlockSpec((N_EXPERTS, ns * seq_len), lambda s: (0, s)),
        ],
        out_shape=[
            jax.ShapeDtypeStruct((n_seq * N_EXPERTS, cap, 1), jnp.int32),
            jax.ShapeDtypeStruct((N_EXPERTS, n_seq * cap, 1), F32),
            jax.ShapeDtypeStruct((N_EXPERTS, n_seq * seq_len), jnp.int32),
        ],
        scratch_shapes=[
            pltpu.VMEM((ns * n_blk * N_EXPERTS, 128), F32),
            pltpu.VMEM((ns * n_blk * N_EXPERTS, 128), F32),
            pltpu.VMEM((2 * ns * N_EXPERTS, 128), F32),
        ] if n_blk >= 16 else [],
        compiler_params=_params(("arbitrary",)),
        name="route",
    )(afft, triu_bf)
    return idx.reshape(n_seq * N_EXPERTS * cap), gs, sp


ROW_COPIES = 8


def _row_tile(first_row):
    return pl.ds(pl.multiple_of(first_row, N_SLAB), N_SLAB)


def _gather_kernel(idx_ref, src_ref, xs_ref, tile_ref, *, eb, cap):
    ei = pl.program_id(1)

    def per_expert(ee, carry):
        e = ei * eb + ee

        def rows(g, c):
            slot0 = g * ROW_COPIES
            for u in range(ROW_COPIES):
                tile_ref[_row_tile((slot0 + u) * N_SLAB), :] = src_ref[_row_tile(idx_ref[e * cap + slot0 + u]), :]
            return c

        lax.fori_loop(0, cap // ROW_COPIES, rows, 0)
        for s in range(N_SLAB):
            xs_ref[ee, :, LANE * s:LANE * (s + 1)] = tile_ref[pl.ds(s, cap, stride=N_SLAB), :].astype(BF16)
        return carry

    lax.fori_loop(0, eb, per_expert, 0)


def _gather(idx, h2_rows, n_seq, cap, eb):
    return pl.pallas_call(
        functools.partial(_gather_kernel, eb=eb, cap=cap),
        grid=(n_seq, N_EXPERTS // eb),
        in_specs=[
            pl.BlockSpec((N_EXPERTS * cap,), lambda s, e: (s,), memory_space=pltpu.SMEM),
            pl.BlockSpec((h2_rows.shape[0] // n_seq, LANE), lambda s, e: (s, 0)),
        ],
        out_specs=pl.BlockSpec((eb, cap, D_MODEL), lambda s, e: (e, s, 0)),
        out_shape=jax.ShapeDtypeStruct((N_EXPERTS, n_seq * cap, D_MODEL), BF16),
        scratch_shapes=[pltpu.VMEM((N_SLAB * cap, LANE), F32)],
        compiler_params=_params(("arbitrary", "arbitrary"), vmem_mib=56),
        name="gather",
    )(idx, h2_rows)


def _selection(sp_ref, cap):
    seq_len = sp_ref.shape[1]
    slot = lax.broadcasted_iota(jnp.int32, (cap, seq_len), 0)
    return jnp.concatenate([jnp.where(slot == sp_ref[e:e + 1, :], 1.0, 0.0) for e in range(N_EXPERTS)],
                           axis=0).astype(BF16)


def _gather_short_kernel(sp_ref, h_ref, xs_ref, *, cap):
    rows = _dot(_selection(sp_ref, cap), h_ref[...])
    for e in range(N_EXPERTS):
        xs_ref[e] = rows[cap * e:cap * (e + 1), :].astype(BF16)


def _gather_short(sp, h2, n_seq, seq_len, cap):
    return pl.pallas_call(
        functools.partial(_gather_short_kernel, cap=cap),
        grid=(n_seq,),
        in_specs=[
            pl.BlockSpec((N_EXPERTS, seq_len), lambda s: (0, s)),
            pl.BlockSpec((seq_len, D_MODEL), lambda s: (s, 0)),
        ],
        out_specs=pl.BlockSpec((N_EXPERTS, cap, D_MODEL), lambda s: (0, s, 0)),
        out_shape=jax.ShapeDtypeStruct((N_EXPERTS, n_seq * cap, D_MODEL), BF16),
        compiler_params=_params(("arbitrary",)),
        name="gather_short",
    )(sp, h2)


def _scatter_short_kernel(sp_ref, y_ref, x_ref, mod_ref, gf_ref, o_ref, *, cap, final):
    sel = _selection(sp_ref, cap)
    y = jnp.concatenate([y_ref[e] for e in range(N_EXPERTS)], axis=0)
    y_hi = y.astype(BF16)
    y_lo = (y - y_hi.astype(F32)).astype(BF16)
    moe = _dot_tn(sel, y_hi) + _dot_tn(sel, y_lo)
    x = x_ref[...] + mod_ref[0, 5:6, :] * moe
    if final:
        x = x * lax.rsqrt(jnp.mean(x * x, axis=-1, keepdims=True) + EPS) * gf_ref[...]
    o_ref[...] = x


def _scatter_short(sp, ys, x2d, mod, g_final, n_seq, seq_len, cap, final):
    return pl.pallas_call(
        functools.partial(_scatter_short_kernel, cap=cap, final=final),
        grid=(n_seq,),
        in_specs=[
            pl.BlockSpec((N_EXPERTS, seq_len), lambda s: (0, s)),
            pl.BlockSpec((N_EXPERTS, cap, D_MODEL), lambda s: (0, s, 0)),
            pl.BlockSpec((seq_len, D_MODEL), lambda s: (s, 0)),
            pl.BlockSpec((1, 6, D_MODEL), lambda s: (0, 0, 0)),
            pl.BlockSpec((1, D_MODEL), lambda s: (0, 0)),
        ],
        out_specs=pl.BlockSpec((seq_len, D_MODEL), lambda s: (s, 0)),
        out_shape=jax.ShapeDtypeStruct((n_seq * seq_len, D_MODEL), F32),
        compiler_params=_params(("arbitrary",)),
        name="scatter_short",
    )(sp, ys, x2d, mod, g_final)


def _expert_kernel(xc_ref, xl_ref, gc_ref, gl_ref, wg_ref, wu_ref, wd_ref, yc_ref, yl_ref, *, n_f):
    f = pl.program_id(1)

    @pl.when(f == 0)
    def _():
        yc_ref[...] = jnp.zeros_like(yc_ref)
        yl_ref[...] = jnp.zeros_like(yl_ref)

    wg = wg_ref[0, 0].astype(BF16)
    wu = wu_ref[0, 0].astype(BF16)
    wd = wd_ref[0, 0].astype(BF16)
    for x_ref, y_ref in ((xc_ref, yc_ref), (xl_ref, yl_ref)):
        x = x_ref[0]
        mid = (_silu(_dot(x, wg)) * _dot(x, wu)).astype(BF16)
        y_ref[0] += _dot(mid, wd)

    @pl.when(f == n_f - 1)
    def _():
        yc_ref[0] = yc_ref[0] * gc_ref[0]
        yl_ref[0] = yl_ref[0] * gl_ref[0]


def _experts(xs_c, xs_l, gs_c, gs_l, w_g, w_u, w_d, layer):
    rc, rl = xs_c.shape[1], xs_l.shape[1]
    tf = 1024
    n_f = EXPERT_FF // tf
    return pl.pallas_call(
        functools.partial(_expert_kernel, n_f=n_f),
        grid=(N_EXPERTS, n_f),
        in_specs=[
            pl.BlockSpec((1, rc, D_MODEL), lambda e, f: (e, 0, 0)),
            pl.BlockSpec((1, rl, D_MODEL), lambda e, f: (e, 0, 0)),
            pl.BlockSpec((1, rc, 1), lambda e, f: (e, 0, 0)),
            pl.BlockSpec((1, rl, 1), lambda e, f: (e, 0, 0)),
            pl.BlockSpec((1, 1, D_MODEL, tf), lambda e, f: (layer, e, 0, f)),
            pl.BlockSpec((1, 1, D_MODEL, tf), lambda e, f: (layer, e, 0, f)),
            pl.BlockSpec((1, 1, tf, D_MODEL), lambda e, f: (layer, e, f, 0)),
        ],
        out_specs=[
            pl.BlockSpec((1, rc, D_MODEL), lambda e, f: (e, 0, 0)),
            pl.BlockSpec((1, rl, D_MODEL), lambda e, f: (e, 0, 0)),
        ],
        out_shape=[
            jax.ShapeDtypeStruct((N_EXPERTS, rc, D_MODEL), F32),
            jax.ShapeDtypeStruct((N_EXPERTS, rl, D_MODEL), F32),
        ],
        compiler_params=_params(("arbitrary", "arbitrary"), vmem_mib=56),
        name="experts",
    )(xs_c, xs_l, gs_c, gs_l, w_g, w_u, w_d)


def _scatter_kernel(idx_ref, y_ref, x_ref, mod_ref, gf_ref, o_ref, acc_ref, tile_ref, *, eb, n_e, cap, tm, final):
    step = pl.program_id(1)

    @pl.when(step == 0)
    def _():
        acc_ref[...] = jnp.zeros_like(acc_ref)

    @pl.when(step < n_e)
    def _():
        def per_expert(ee, carry):
            e = step * eb + ee
            for s in range(N_SLAB):
                tile_ref[pl.ds(s, cap, stride=N_SLAB), :] = y_ref[ee, :, LANE * s:LANE * (s + 1)]

            def rows(g, c):
                slot0 = g * ROW_COPIES
                dst = [idx_ref[e * cap + slot0 + u] for u in range(ROW_COPIES)]
                new = [acc_ref[_row_tile(dst[u]), :] + tile_ref[_row_tile((slot0 + u) * N_SLAB), :]
                       for u in range(ROW_COPIES)]
                for u in range(ROW_COPIES):
                    acc_ref[_row_tile(dst[u]), :] = new[u]
                return c

            lax.fori_loop(0, cap // ROW_COPIES, rows, 0)
            return carry

        lax.fori_loop(0, eb, per_expert, 0)

    @pl.when(step >= n_e)
    def _():
        base = pl.multiple_of((step - n_e) * tm * N_SLAB, N_SLAB)
        moe = jnp.concatenate([acc_ref[pl.ds(base + s, tm, stride=N_SLAB), :] for s in range(N_SLAB)], axis=1)
        x = x_ref[...] + mod_ref[0, 5:6, :] * moe
        if final:
            x = x * lax.rsqrt(jnp.mean(x * x, axis=-1, keepdims=True) + EPS) * gf_ref[...]
        o_ref[...] = x


def _scatter(idx, ys, x2d, mod, g_final, n_seq, seq_len, cap, eb, tm, final):
    n_e = N_EXPERTS // eb
    n_out = seq_len // tm
    out_blk = lambda s, j: (s * n_out + jnp.maximum(j - n_e, 0), 0)
    return pl.pallas_call(
        functools.partial(_scatter_kernel, eb=eb, n_e=n_e, cap=cap, tm=tm, final=final),
        grid=(n_seq, n_e + n_out),
        in_specs=[
            pl.BlockSpec((N_EXPERTS * cap,), lambda s, j: (s,), memory_space=pltpu.SMEM),
            pl.BlockSpec((eb, cap, D_MODEL), lambda s, j: (jnp.minimum(j, n_e - 1), s, 0)),
            pl.BlockSpec((tm, D_MODEL), out_blk),
            pl.BlockSpec((1, 6, D_MODEL), lambda s, j: (s, 0, 0)),
            pl.BlockSpec((1, D_MODEL), lambda s, j: (0, 0)),
        ],
        out_specs=pl.BlockSpec((tm, D_MODEL), out_blk),
        out_shape=jax.ShapeDtypeStruct((n_seq * seq_len, D_MODEL), F32),
        scratch_shapes=[
            pltpu.VMEM((N_SLAB * seq_len, LANE), F32),
            pltpu.VMEM((N_SLAB * cap, LANE), F32),
        ],
        compiler_params=_params(("arbitrary", "arbitrary"), vmem_mib=56),
        name="scatter",
    )(idx, ys, x2d, mod, g_final)


def kernel(x_prompt, x_sample, c, cache_k, cache_v, state_C, state_n, state_m, c_ctx, w_ada, b_ada, g_norm1, g_norm2, w_in, b_gates, rpb, w_fourier, g_head, w_out, w_router, w_exp_gate, w_exp_up, w_exp_down, g_final):
    n_ctx, len_ctx, _ = x_prompt.shape
    n_lat, len_lat, _ = x_sample.shape
    past = cache_k.shape[2]
    cap_ctx = CAPACITY_FACTOR * len_ctx // N_EXPERTS
    cap_lat = CAPACITY_FACTOR * len_lat // N_EXPERTS

    w_in_bf = w_in.astype(BF16)
    w_gt_bf = jnp.swapaxes(w_in[:, :, P_COLS:], 1, 2).astype(BF16)
    vc0, oc0 = VC_BLK * C_W, OC_BLK * C_W
    w_vo_bf = jnp.swapaxes(jnp.concatenate([w_in[:, :, vc0:vc0 + C_W], w_in[:, :, oc0:oc0 + C_W]], axis=2),
                           1, 2).astype(BF16)
    bg_row = b_gates.reshape(DEPTH, 1, N_GATE_COLS).astype(F32)
    bg_col = b_gates.reshape(DEPTH, N_GATE_COLS, 1).astype(F32)
    w_out_bf = w_out.astype(BF16)
    w_rt_bf = jnp.swapaxes(w_router, 1, 2).astype(BF16)
    g1 = g_norm1.reshape(DEPTH, 1, D_MODEL)
    g2 = g_norm2.reshape(DEPTH, 1, D_MODEL)
    gh = g_head.reshape(DEPTH, 1, D_MODEL)
    g_mem = jnp.broadcast_to(gh[:, 0, A_W + B_W:, None], (DEPTH, C_W, MERGE_TILE))
    eye_g = jnp.eye(G_FOURIER, dtype=F32)
    wf_blk = jnp.einsum("lgcd,gh->lgchd", w_fourier, eye_g).reshape(DEPTH, B_W, B_W).astype(BF16)

    csc = _channel_dft()
    dft_ctx = _dft_mats(len_ctx)
    dft_side = _dft_mats(GRID_W)
    tw_cos, tw_sin = _twiddles(GRID_W, len_lat)
    bias_tiles = _nbr_bias_tiles(rpb)
    r = np.arange(MLSTM_CHUNK)
    tril = jnp.asarray(r[:, None] >= r[None, :], F32)
    triu = jnp.asarray(r[:, None] <= r[None, :], F32)
    triu_bf = triu.astype(BF16)
    hidx = np.arange(A_W) // HEAD_DIM
    ones_blk = jnp.asarray(hidx[:, None] == hidx[None, :], BF16)

    cvecs = jnp.concatenate([c_ctx[None, :], c, jnp.zeros((8 - 1 - n_lat, D_MODEL), F32)], axis=0)
    mod_all = _modulation(cvecs, w_ada, b_ada).reshape(DEPTH, 8, 6, D_MODEL)

    cache_k4 = cache_k.reshape(n_lat, DEPTH, past, A_W)
    cache_v4 = cache_v.reshape(n_lat, DEPTH, past, A_W)
    zero_c = jnp.zeros((n_ctx, 2, H_MLSTM, HEAD_DIM, HEAD_DIM), F32)
    zero_n = jnp.zeros((n_ctx, 2, H_MLSTM, HEAD_DIM), F32)
    zero_m = jnp.zeros((n_ctx, 2, H_MLSTM), F32)

    xc = x_prompt.reshape(n_ctx * len_ctx, D_MODEL)
    xl = x_sample.reshape(n_lat * len_lat, D_MODEL)
    gf = g_final.reshape(1, D_MODEL)
    new_k = jnp.zeros((n_ctx, DEPTH, len_ctx, A_W), F32)
    new_v = jnp.zeros((n_ctx, DEPTH, len_ctx, A_W), F32)
    cs, ns, ms = [], [], []
    for l in range(DEPTH):
        mod_c = mod_all[l, 0:1]
        mod_l = mod_all[l, 1:1 + n_lat]

        pc, gc, gtc, voc, abc, new_k, new_v = _inproj(xc, mod_c, l, g1, w_in_bf, w_gt_bf, w_vo_bf, bg_row, bg_col, csc,
                                                      n_ctx * len_ctx, False, (new_k, new_v))
        att_c = _ctx_attention(pc, n_ctx, len_ctx)
        four_c = _fourier(abc, dft_ctx[0], dft_ctx[1], wf_blk, l, n_ctx, len_ctx)
        hf_c, hb_c, c_new, n_new, m_new = _mlstm(pc, gc, gtc, voc, zero_c, zero_n, zero_m, tril, triu, n_ctx, len_ctx)
        xc, h2c, affc = _merge(att_c, four_c, hf_c, hb_c, voc, xc, mod_c, l, gh, g_mem, w_out_bf, g2, w_rt_bf,
                               ones_blk, n_ctx, len_ctx, n_ctx * len_ctx, False)
        cs.append(c_new)
        ns.append(n_new)
        ms.append(m_new)

        pq, gq, gtq, voq, abq = _inproj(xl, mod_l, l, g1, w_in_bf, w_gt_bf, w_vo_bf, bg_row, bg_col, csc, len_lat,
                                        True)
        att_l = _nbr_attention(pq, cache_k4, cache_v4, bias_tiles, l, n_lat, len_lat)
        four_l = _fourier_grid(abq, dft_side[0], dft_side[1], tw_cos, tw_sin, wf_blk, l, n_lat, GRID_W)
        hf_l, hb_l, _, _, _ = _mlstm(pq, gq, gtq, voq, state_C[:, l], state_n[:, l], state_m[:, l], tril, triu,
                                     n_lat, len_lat)
        xl, h2l, affl = _merge(att_l, four_l, hf_l, hb_l, voq, xl, mod_l, l, gh, g_mem, w_out_bf, g2, w_rt_bf,
                               ones_blk, n_lat, len_lat, len_lat, True)

        last = l == DEPTH - 1
        _, gs_c, sp_c = _route(affc, triu_bf, n_ctx, len_ctx, cap_ctx, n_ctx)
        idx_l, gs_l, _ = _route(affl, triu_bf, n_lat, len_lat, cap_lat, 1)
        xs_c = _gather_short(sp_c, h2c, n_ctx, len_ctx, cap_ctx)
        xs_l = _gather(idx_l, h2l, n_lat, cap_lat, 4)
        ys_c, ys_l = _experts(xs_c, xs_l, gs_c, gs_l, w_exp_gate, w_exp_up, w_exp_down, l)
        xc = _scatter_short(sp_c, ys_c, xc, mod_c, gf, n_ctx, len_ctx, cap_ctx, last)
        xl = _scatter(idx_l, ys_l, xl, mod_l, gf, n_lat, len_lat, cap_lat, 2, 512, last)

    y_prompt = xc.reshape(n_ctx, len_ctx, D_MODEL)
    y_sample = xl.reshape(n_lat, len_lat, D_MODEL)
    kv_shape = (n_ctx, DEPTH, len_ctx, H_ATT, HEAD_DIM)
    return (y_prompt, y_sample, new_k.reshape(kv_shape), new_v.reshape(kv_shape), jnp.stack(cs, axis=1),
            jnp.stack(ns, axis=1), jnp.stack(ms, axis=1))
```

```python
import functools

import numpy as np
import jax
import jax.numpy as jnp
from jax import lax
from jax.experimental import pallas as pl
from jax.experimental.pallas import tpu as pltpu

F32 = jnp.float32
BF16 = jnp.bfloat16

D_MODEL = 1024
DEPTH = 2
HEAD_DIM = 64
H_ATT = 8
G_FOURIER = 4
H_MLSTM = 4
A_W = H_ATT * HEAD_DIM
B_W = G_FOURIER * HEAD_DIM
C_W = H_MLSTM * HEAD_DIM
N_GATE_COLS = 16
P_COLS = 3 * A_W + B_W + 4 * C_W
IN_COLS = P_COLS + N_GATE_COLS
P_KEEP = 3 * A_W + B_W + 2 * C_W
GRID_W = 64
WIN_R = 8
WIN_C = 16
MLSTM_CHUNK = 128
N_EXPERTS = 16
CAPACITY_FACTOR = 2
EXPERT_FF = 2 * D_MODEL
EPS = 1e-6
NEG = -1e30

UB_OFF = 3 * A_W
QC_BLK, KC_BLK, VC_BLK, OC_BLK = 7, 8, 9, 10

LANE = 128
N_SLAB = D_MODEL // LANE
MERGE_TILE = 256
INPROJ_TILE = 512

NT_DIMS = (((1,), (1,)), ((), ()))
TN_DIMS = (((0,), (0,)), ((), ()))
MIB = 1024 * 1024


def _dot(a, b, precision=None):
    return jnp.dot(a, b, preferred_element_type=F32, precision=precision)


def _dot_nt(a, b, precision=None):
    return lax.dot_general(a, b, NT_DIMS, preferred_element_type=F32, precision=precision)


def _dot_tn(a, b):
    return lax.dot_general(a, b, TN_DIMS, preferred_element_type=F32)


def _params(sem, vmem_mib=48):
    return pltpu.CompilerParams(dimension_semantics=sem, vmem_limit_bytes=vmem_mib * MIB)


def _silu(x):
    return x * jax.nn.sigmoid(x)


def _log_sigmoid(x):
    return jnp.minimum(x, 0.0) - jnp.log1p(jnp.exp(-jnp.abs(x)))


def _mod_kernel(c_ref, w_ref, b_ref, o_ref):
    s = _silu(c_ref[...]).astype(BF16)
    o_ref[0] = _dot(s, w_ref[0].astype(BF16)) + b_ref[0]


def _modulation(cvecs, w_ada, b_ada):
    depth = w_ada.shape[0]
    tn = 1024
    return pl.pallas_call(
        _mod_kernel,
        grid=(depth, 6 * D_MODEL // tn),
        in_specs=[
            pl.BlockSpec((8, D_MODEL), lambda l, j: (0, 0)),
            pl.BlockSpec((1, D_MODEL, tn), lambda l, j: (l, 0, j)),
            pl.BlockSpec((1, 1, tn), lambda l, j: (l, 0, j)),
        ],
        out_specs=pl.BlockSpec((1, 8, tn), lambda l, j: (l, 0, j)),
        out_shape=jax.ShapeDtypeStruct((depth, 8, 6 * D_MODEL), F32),
        compiler_params=_params(("arbitrary", "arbitrary")),
        name="modulation",
    )(cvecs, w_ada, b_ada.reshape(depth, 1, 6 * D_MODEL))


def _inproj_kernel(x_ref, mod_ref, g1_ref, w_ref, wgt_ref, wvo_ref, bgr_ref, bgc_ref, csc_ref,
                   *rest, grid_rows, kv_seq_len):
    n_in = 2 if kv_seq_len else 0
    p_ref, g_ref, gt_ref, vo_ref, ab_ref = rest[n_in:n_in + 5]
    kv_refs = rest[n_in + 5:n_in + 5 + n_in]
    scratch = rest[n_in + 5 + n_in:]
    x = x_ref[...]
    y = x * lax.rsqrt(jnp.mean(x * x, axis=-1, keepdims=True) + EPS) * g1_ref[0]
    h = (y * (1.0 + mod_ref[0, 1:2, :]) + mod_ref[0, 0:1, :]).astype(BF16)
    for j in range(0, P_KEEP, 256):
        pj = _dot(h, w_ref[0, :, j:j + 256])
        p_ref[:, j:j + 256] = pj.astype(BF16)
        if kv_seq_len and A_W <= j < 3 * A_W:
            kv_ref = kv_refs[(j - A_W) // A_W]
            c0 = (j - A_W) % A_W
            for b in range(pj.shape[0] // kv_seq_len):
                kv_ref[b, 0, :, c0:c0 + 256] = pj[kv_seq_len * b:kv_seq_len * (b + 1), :]
        if j == UB_OFF:
            ab = _dot(pj.astype(BF16), csc_ref[...])
            if grid_rows:
                stage_ref, = scratch
                n_lt = 2 * B_W // LANE
                for c in range(n_lt):
                    stage_ref[c] = ab[:, LANE * c:LANE * (c + 1)]
                for n2 in range(GRID_W):
                    for c in range(n_lt):
                        col = 2 * B_W * n2 + LANE * c
                        ab_ref[:, col:col + LANE] = stage_ref[c, pl.ds(n2, grid_rows, stride=GRID_W), :]
            else:
                ab_ref[...] = ab.astype(BF16)
    g_ref[...] = _dot(h, w_ref[0, :, P_COLS:IN_COLS]) + bgr_ref[0]
    gt_ref[...] = _dot_nt(wgt_ref[0], h) + bgc_ref[0]
    vo_ref[...] = _dot_nt(wvo_ref[0], h)


def _inproj(x2d, mod, layer, g1, w_in_bf, w_gt_bf, w_vo_bf, bg_row, bg_col, csc, seq_len, grid_ab, kv_cache=None):
    t = x2d.shape[0]
    tm = INPROJ_TILE
    tiles_per_seq = seq_len // tm
    grid_rows = tm // GRID_W if grid_ab else 0
    if grid_ab:
        ab_spec = pl.BlockSpec((grid_rows, GRID_W * 2 * B_W), lambda i: (i, 0))
        ab_shape = jax.ShapeDtypeStruct((t // GRID_W, GRID_W * 2 * B_W), F32)
        scratch = [pltpu.VMEM((2 * B_W // LANE, tm, LANE), F32)]
    else:
        ab_spec = pl.BlockSpec((tm, 2 * B_W), lambda i: (i, 0))
        ab_shape = jax.ShapeDtypeStruct((t, 2 * B_W), BF16)
        scratch = []
    kv_seq_len = kv_cache[0].shape[2] if kv_cache else 0
    kv_in_specs, kv_out_specs, kv_shapes, aliases = [], [], [], {}
    if kv_cache:
        per_tile = tm // kv_seq_len
        kv_in_specs = [pl.BlockSpec(memory_space=pl.ANY)] * 2
        kv_out_specs = [pl.BlockSpec((per_tile, 1, kv_seq_len, A_W), lambda i: (i, layer, 0, 0))] * 2
        kv_shapes = [jax.ShapeDtypeStruct(a.shape, a.dtype) for a in kv_cache]
        aliases = {9: 5, 10: 6}
    return pl.pallas_call(
        functools.partial(_inproj_kernel, grid_rows=grid_rows, kv_seq_len=kv_seq_len),
        grid=(t // tm,),
        input_output_aliases=aliases,
        in_specs=[
            pl.BlockSpec((tm, D_MODEL), lambda i: (i, 0)),
            pl.BlockSpec((1, 6, D_MODEL), lambda i: (i // tiles_per_seq, 0, 0)),
            pl.BlockSpec((1, 1, D_MODEL), lambda i: (layer, 0, 0)),
            pl.BlockSpec((1, D_MODEL, IN_COLS), lambda i: (layer, 0, 0)),
            pl.BlockSpec((1, N_GATE_COLS, D_MODEL), lambda i: (layer, 0, 0)),
            pl.BlockSpec((1, 2 * C_W, D_MODEL), lambda i: (layer, 0, 0)),
            pl.BlockSpec((1, 1, N_GATE_COLS), lambda i: (layer, 0, 0)),
            pl.BlockSpec((1, N_GATE_COLS, 1), lambda i: (layer, 0, 0)),
            pl.BlockSpec((B_W, 2 * B_W), lambda i: (0, 0)),
        ] + kv_in_specs,
        out_specs=[
            pl.BlockSpec((tm, P_KEEP), lambda i: (i, 0)),
            pl.BlockSpec((tm, N_GATE_COLS), lambda i: (i, 0)),
            pl.BlockSpec((N_GATE_COLS, tm), lambda i: (0, i)),
            pl.BlockSpec((2 * C_W, tm), lambda i: (0, i)),
            ab_spec,
        ] + kv_out_specs,
        out_shape=[
            jax.ShapeDtypeStruct((t, P_KEEP), BF16),
            jax.ShapeDtypeStruct((t, N_GATE_COLS), F32),
            jax.ShapeDtypeStruct((N_GATE_COLS, t), F32),
            jax.ShapeDtypeStruct((2 * C_W, t), F32),
            ab_shape,
        ] + kv_shapes,
        scratch_shapes=scratch,
        compiler_params=_params(("arbitrary",)),
        name="inproj",
    )(x2d, mod, g1, w_in_bf, w_gt_bf, w_vo_bf, bg_row, bg_col, csc, *(kv_cache or ()))


def _ctx_attn_kernel(q_ref, k_ref, v_ref, o_ref):
    scale = HEAD_DIM ** -0.5
    heads = range(H_ATT)
    sl = [slice(HEAD_DIM * h, HEAD_DIM * (h + 1)) for h in heads]
    s = [_dot_nt((q_ref[:, sl[h]] * scale).astype(BF16), k_ref[:, sl[h]].astype(BF16)) for h in heads]
    e = [jnp.exp(s[h] - jnp.max(s[h], axis=-1, keepdims=True)) for h in heads]
    w = [e[h] * (1.0 / jnp.sum(e[h], axis=-1, keepdims=True)) for h in heads]
    o_ref[...] = jnp.concatenate([_dot(w[h].astype(BF16), v_ref[:, sl[h]].astype(BF16)) for h in heads], axis=1)


def _ctx_attention(p, n_seq, seq_len):
    return pl.pallas_call(
        _ctx_attn_kernel,
        grid=(n_seq,),
        in_specs=[
            pl.BlockSpec((seq_len, A_W), lambda b: (b, 0)),
            pl.BlockSpec((seq_len, A_W), lambda b: (b, 1)),
            pl.BlockSpec((seq_len, A_W), lambda b: (b, 2)),
        ],
        out_specs=pl.BlockSpec((seq_len, A_W), lambda b: (b, 0)),
        out_shape=jax.ShapeDtypeStruct((n_seq * seq_len, A_W), F32),
        compiler_params=_params(("arbitrary",)),
        name="ctx_attention",
    )(p, p, p)


Q_ROWS = 8
K_ROWS = 16
KEY_BLK = 256


NO_ROW = 2 * WIN_R - 1


def _nbr_bias_tiles(rpb):
    n_c = 2 * WIN_C - 1
    cq = np.arange(GRID_W)[:, None]
    ck = np.arange(GRID_W)[None, :]
    cs = np.clip(cq - WIN_C // 2, 0, GRID_W - WIN_C)
    col_ok = (ck >= cs) & (ck < cs + WIN_C)
    pick = np.where(col_ok, np.clip(ck - cq + WIN_C - 1, 0, n_c - 1), n_c)
    sel = np.zeros((2, GRID_W, 2 * GRID_W, n_c + 1), np.float32)
    for side in range(2):
        sel[side, cq, side * GRID_W + ck, pick] = 1.0
    ext = jnp.full(rpb.shape[:2] + (2 * WIN_R, n_c + 1), NEG, F32)
    ext = ext.at[:, :, :2 * WIN_R - 1, :n_c].set(rpb.astype(F32))
    return jnp.einsum("lhdm,sqkm->lhdsqk", ext, jnp.asarray(sel), precision=lax.Precision.HIGHEST)


def _nbr_attn_kernel(q_ref, k0_ref, k1_ref, k2_ref, k3_ref, v0_ref, v1_ref, v2_ref, v3_ref,
                     ck_ref, cv_ref, tab_ref, o_ref, *, n_rb):
    scale = HEAD_DIM ** -0.5
    k_refs = (k0_ref, k1_ref, k2_ref, k3_ref)
    v_refs = (v0_ref, v1_ref, v2_ref, v3_ref)
    rb = pl.program_id(1)
    rows = GRID_W

    rows_per_blk = KEY_BLK // GRID_W
    n_blk = K_ROWS // rows_per_blk
    half_rows = Q_ROWS // 2
    half_tok = half_rows * GRID_W

    def window(rb_s):
        key_row0 = min(max(Q_ROWS * rb_s - WIN_R // 2, 0), rows - K_ROWS)
        d = []
        for rq in range(Q_ROWS):
            r = Q_ROWS * rb_s + rq
            rs = min(max(r - WIN_R // 2, 0), rows - WIN_R)
            d.append([key_row0 + rk - r + WIN_R - 1 if rs <= key_row0 + rk < rs + WIN_R else NO_ROW
                      for rk in range(K_ROWS)])
        active = tuple(
            tuple(j for j in range(n_blk)
                  if any(d[rq][rk] != NO_ROW for rq in range(half_rows * half, half_rows * (half + 1))
                         for rk in range(rows_per_blk * j, rows_per_blk * (j + 1))))
            for half in range(2))
        return active, d

    def attend(active, d):
        units = [(hh, half) for hh in range(2) for half in range(2)]
        sl = [slice(HEAD_DIM * hh, HEAD_DIM * (hh + 1)) for hh in range(2)]
        rows_of = [slice(half_tok * half, half_tok * (half + 1)) for half in range(2)]
        q = [(q_ref[rows_of[half], sl[hh]] * scale).astype(BF16) for hh, half in units]
        s_ctx = [_dot_nt(q[u], ck_ref[0, 0, :, sl[hh]].astype(BF16)) for u, (hh, half) in enumerate(units)]
        s_loc = []
        for u, (hh, half) in enumerate(units):
            blocks = []
            for j in active[half]:
                bias = jnp.concatenate([
                    jnp.concatenate([
                        tab_ref[0, hh, d[rq][rows_per_blk * j + 2 * p], 0]
                        + tab_ref[0, hh, d[rq][rows_per_blk * j + 2 * p + 1], 1]
                        for p in range(rows_per_blk // 2)], axis=1)
                    for rq in range(half_rows * half, half_rows * (half + 1))], axis=0)
                blocks.append(_dot_nt(q[u], k_refs[j][:, sl[hh]].astype(BF16)) + bias)
            s_loc.append(blocks)
        m = [jnp.max(s_ctx[u], axis=-1, keepdims=True) for u in range(len(units))]
        for u in range(len(units)):
            for s_blk in s_loc[u]:
                m[u] = jnp.maximum(m[u], jnp.max(s_blk, axis=-1, keepdims=True))
        e_ctx = [jnp.exp(s_ctx[u] - m[u]) for u in range(len(units))]
        den = [jnp.sum(e_ctx[u], axis=-1, keepdims=True) for u in range(len(units))]
        num = [_dot(e_ctx[u].astype(BF16), cv_ref[0, 0, :, sl[hh]].astype(BF16)) for u, (hh, half) in enumerate(units)]
        for u, (hh, half) in enumerate(units):
            for j, s_blk in zip(active[half], s_loc[u]):
                e = jnp.exp(s_blk - m[u])
                den[u] = den[u] + jnp.sum(e, axis=-1, keepdims=True)
                num[u] = num[u] + _dot(e.astype(BF16), v_refs[j][:, sl[hh]].astype(BF16))
        out = [num[u] / den[u] for u in range(len(units))]
        o_ref[...] = jnp.concatenate([jnp.concatenate([out[2 * hh], out[2 * hh + 1]], axis=0) for hh in range(2)],
                                     axis=1)

    @pl.when(rb == 0)
    def _():
        attend(*window(0))

    @pl.when((rb > 0) & (rb < n_rb - 1))
    def _():
        attend(*window(1))

    @pl.when(rb == n_rb - 1)
    def _():
        attend(*window(n_rb - 1))


def _nbr_attention(p, cache_k4, cache_v4, bias_tiles, layer, n_seq, seq_len):
    q_tok = Q_ROWS * GRID_W
    n_rb = seq_len // q_tok
    kb_per_seq = seq_len // KEY_BLK
    max_base = kb_per_seq - 4

    def kmap(j, col0):
        def f(hp, rb, b):
            base = jnp.clip(2 * rb - 1, 0, max_base)
            return (b * kb_per_seq + base + j, col0 + hp)
        return f

    past = cache_k4.shape[2]
    in_specs = [pl.BlockSpec((q_tok, 128), lambda hp, rb, b: (b * n_rb + rb, hp))]
    in_specs += [pl.BlockSpec((KEY_BLK, 128), kmap(j, A_W // 128)) for j in range(4)]
    in_specs += [pl.BlockSpec((KEY_BLK, 128), kmap(j, 2 * A_W // 128)) for j in range(4)]
    in_specs += [
        pl.BlockSpec((1, 1, past, 128), lambda hp, rb, b: (b, layer, 0, hp)),
        pl.BlockSpec((1, 1, past, 128), lambda hp, rb, b: (b, layer, 0, hp)),
        pl.BlockSpec((1, 2, 2 * WIN_R, 2, GRID_W, 2 * GRID_W), lambda hp, rb, b: (layer, hp, 0, 0, 0, 0)),
    ]
    assert n_rb >= 3
    return pl.pallas_call(
        functools.partial(_nbr_attn_kernel, n_rb=n_rb),
        grid=(H_ATT // 2, n_rb, n_seq),
        in_specs=in_specs,
        out_specs=pl.BlockSpec((q_tok, 128), lambda hp, rb, b: (b * n_rb + rb, hp)),
        out_shape=jax.ShapeDtypeStruct((n_seq * seq_len, A_W), F32),
        compiler_params=_params(("arbitrary", "arbitrary", "arbitrary")),
        name="nbr_attention",
    )(p, p, p, p, p, p, p, p, p, cache_k4, cache_v4, bias_tiles)


def _dft_mats(n):
    idx = jnp.arange(n, dtype=jnp.int32)
    ang = ((idx[:, None] * idx[None, :]) % n).astype(F32) * (2.0 * np.pi / n)
    return jnp.cos(ang).astype(BF16), jnp.sin(ang).astype(BF16)


def _channel_dft():
    c = np.arange(HEAD_DIM)
    ang = 2.0 * np.pi * ((c[:, None] * c[None, :]) % HEAD_DIM) / HEAD_DIM
    eye = np.eye(G_FOURIER)
    mats = np.concatenate([np.kron(eye, np.cos(ang)), np.kron(eye, np.sin(ang))], axis=1)
    return jnp.asarray(mats, F32).astype(BF16)


def _fourier_kernel(c_ref, s_ref, ab_ref, wf_ref, o_ref, acc_ref, *, scale, n_k):
    k = pl.program_id(2)

    @pl.when(k == 0)
    def _():
        acc_ref[...] = jnp.zeros_like(acc_ref)

    acc_ref[...] += _dot(c_ref[...], ab_ref[:, :B_W]) - _dot(s_ref[...], ab_ref[:, B_W:])

    @pl.when(k == n_k - 1)
    def _():
        z = (acc_ref[...] * scale).astype(BF16)
        o = _dot(z, wf_ref[0])
        for c in range(B_W // LANE):
            o_ref[c] = o[:, LANE * c:LANE * (c + 1)]


def _fourier(ab, cmat, smat, wf_blk, layer, n_seq, seq_len):
    ti = min(seq_len, 512)
    tk = min(seq_len, 1024)
    n_i, n_k = seq_len // ti, seq_len // tk
    scale = float((seq_len * HEAD_DIM) ** -0.5)
    return pl.pallas_call(
        functools.partial(_fourier_kernel, scale=scale, n_k=n_k),
        grid=(n_seq, n_i, n_k),
        in_specs=[
            pl.BlockSpec((ti, tk), lambda s, i, k: (i, k)),
            pl.BlockSpec((ti, tk), lambda s, i, k: (i, k)),
            pl.BlockSpec((tk, 2 * B_W), lambda s, i, k: (s * n_k + k, 0)),
            pl.BlockSpec((1, B_W, B_W), lambda s, i, k: (layer, 0, 0)),
        ],
        out_specs=pl.BlockSpec((B_W // LANE, ti, LANE), lambda s, i, k: (0, s * n_i + i, 0)),
        out_shape=jax.ShapeDtypeStruct((B_W // LANE, n_seq * seq_len, LANE), F32),
        scratch_shapes=[pltpu.VMEM((ti, B_W), F32)],
        compiler_params=_params(("arbitrary", "arbitrary", "arbitrary")),
        name="fourier",
    )(cmat, smat, ab, wf_blk)


FS_GROUP = 8


def _twiddles(side, n):
    k1 = jnp.arange(side, dtype=jnp.int32)[:, None]
    n2 = jnp.arange(side, dtype=jnp.int32)[None, :]
    ang = (k1 * n2).astype(F32) * (2.0 * np.pi / n)
    return jnp.cos(ang), jnp.sin(ang)


def _fourier_grid_kernel(ab_ref, c_ref, s_ref, tc_ref, ts_ref, wf_ref, o_ref, y_ref, *, side, scale):
    cmat = c_ref[...]
    smat = s_ref[...]
    for g in range(side // FS_GROUP):
        ab = ab_ref[:, 2 * B_W * FS_GROUP * g:2 * B_W * FS_GROUP * (g + 1)].astype(BF16)
        m1 = _dot(cmat, ab)
        m2 = _dot(smat, ab)
        for t in range(FS_GROUP):
            n2 = FS_GROUP * g + t
            a0 = 2 * B_W * t
            yr = m1[:, a0:a0 + B_W] - m2[:, a0 + B_W:a0 + 2 * B_W]
            yi = -(m1[:, a0 + B_W:a0 + 2 * B_W] + m2[:, a0:a0 + B_W])
            ct = tc_ref[:, n2:n2 + 1]
            st = ts_ref[:, n2:n2 + 1]
            y = jnp.concatenate([yr * ct + yi * st, yi * ct - yr * st], axis=1)
            for c in range(2 * B_W // LANE):
                for hi in range(side // 8):
                    r0 = (hi * side + n2) * 8
                    y_ref[c, r0:r0 + 8, :] = y[8 * hi:8 * (hi + 1), LANE * c:LANE * (c + 1)]
    for g in range(side // FS_GROUP):
        zs = []
        for j in range(FS_GROUP):
            k1 = FS_GROUP * g + j
            rows = pl.ds((k1 // 8) * side * 8 + k1 % 8, side, stride=8)
            n_lt = B_W // LANE
            y_re = jnp.concatenate([y_ref[c, rows, :] for c in range(n_lt)], axis=1).astype(BF16)
            y_im = jnp.concatenate([y_ref[n_lt + c, rows, :] for c in range(n_lt)], axis=1).astype(BF16)
            zs.append(_dot(cmat, y_re) + _dot(smat, y_im))
        z = (jnp.concatenate(zs, axis=0) * scale).astype(BF16)
        o = _dot(z, wf_ref[0])
        for j in range(FS_GROUP):
            for c in range(B_W // LANE):
                o_ref[c, pl.ds(FS_GROUP * g + j, side, stride=side), :] = o[side * j:side * (j + 1),
                                                                            LANE * c:LANE * (c + 1)]


def _fourier_grid(ab_grid, cmat, smat, tw_cos, tw_sin, wf_blk, layer, n_seq, side):
    seq_len = side * side
    scale = float((seq_len * HEAD_DIM) ** -0.5)
    small = pl.BlockSpec((side, side), lambda s: (0, 0))
    return pl.pallas_call(
        functools.partial(_fourier_grid_kernel, side=side, scale=scale),
        grid=(n_seq,),
        in_specs=[
            pl.BlockSpec((side, side * 2 * B_W), lambda s: (s, 0)),
            small,
            small,
            small,
            small,
            pl.BlockSpec((1, B_W, B_W), lambda s: (layer, 0, 0)),
        ],
        out_specs=pl.BlockSpec((B_W // LANE, seq_len, LANE), lambda s: (0, s, 0)),
        out_shape=jax.ShapeDtypeStruct((B_W // LANE, n_seq * seq_len, LANE), F32),
        scratch_shapes=[pltpu.VMEM((2 * B_W // LANE, seq_len, LANE), F32)],
        compiler_params=_params(("arbitrary",), vmem_mib=56),
        name="fourier_grid",
    )(ab_grid, cmat, smat, tw_cos, tw_sin, wf_blk)


SEQS_PER_STEP = 2


def _mlstm_kernel(*refs, n_chunks):
    sps = SEQS_PER_STEP
    n_side = 3 + 2 * sps
    fwd, bwd = refs[:n_side], refs[n_side:2 * n_side]
    c0_ref, n0_ref, m0_ref, tril_ref, triu_ref = refs[2 * n_side:2 * n_side + 5]
    hf_ref, hb_ref, cout_ref, nout_ref, mout_ref, c_s, n_s, m_s = refs[2 * n_side + 5:]
    c = pl.program_id(1)
    hi = lax.Precision.HIGHEST
    lc = MLSTM_CHUNK
    pair_w = 2 * HEAD_DIM
    n_pairs = H_MLSTM // 2

    @pl.when(c == 0)
    def _():
        c_s[...] = c0_ref[...]
        n_s[...] = n0_ref[...]
        m_s[...] = m0_ref[...]

    lo_lane = lax.broadcasted_iota(jnp.int32, (1, pair_w), 1) < HEAD_DIM
    lo_row = lax.broadcasted_iota(jnp.int32, (pair_w, 1), 0) < HEAD_DIM
    row8 = lax.broadcasted_iota(jnp.int32, (8, 1), 0)
    cum_mask = [tril_ref[...], triu_ref[...]]
    keep_t = [triu_ref[...] > 0.5, tril_ref[...] > 0.5]
    pairs = [(j, d, hp) for j in range(sps) for d in range(2) for hp in range(n_pairs)]
    heads = [(pi, hh) for pi in range(len(pairs)) for hh in range(2)]
    rng = range(len(heads))

    pre = {}
    for j in range(sps):
        for d, side in enumerate((fwd, bwd)):
            q_ref, k_ref, g_ref = side[:3]
            gt_ref, vt_ref = side[3 + 2 * j], side[4 + 2 * j]
            go = 2 * H_MLSTM * d
            lf_cols = _log_sigmoid(g_ref[j, :, go + H_MLSTM:go + 2 * H_MLSTM])
            lf_rows = _log_sigmoid(gt_ref[go + H_MLSTM:go + 2 * H_MLSTM, :])
            b_cols = _dot(cum_mask[d], lf_cols, precision=hi)
            pre[j, d] = dict(
                a_cols=g_ref[j, :, go:go + H_MLSTM] - b_cols,
                ig_rows=gt_ref[go:go + H_MLSTM, :],
                b_rows=_dot_nt(lf_rows, cum_mask[d], precision=hi),
                q=q_ref[j].astype(BF16), k=(k_ref[j] * (HEAD_DIM ** -0.5)).astype(BF16), vt=vt_ref[...])

    def pair_cols(hp):
        return slice(pair_w * hp, pair_w * (hp + 1))

    q_p = [pre[j, d]["q"][:, pair_cols(hp)] for j, d, hp in pairs]
    k_p = [pre[j, d]["k"][:, pair_cols(hp)] for j, d, hp in pairs]
    vt_p = [pre[j, d]["vt"][pair_cols(hp), :] for j, d, hp in pairs]
    c_p = [c_s[j, d, hp] for j, d, hp in pairs]
    n_p = [n_s[j, n_pairs * d + hp:n_pairs * d + hp + 1, :] for j, d, hp in pairs]
    zero_k = jnp.zeros((lc, pair_w), BF16)
    k_h = [jnp.where(lo_lane, k_p[pi], zero_k) if hh == 0 else jnp.where(lo_lane, zero_k, k_p[pi])
           for pi, hh in heads]

    def head_of(i):
        pi, hh = heads[i]
        j, d, hp = pairs[pi]
        return j, d, 2 * hp + hh

    b_row = [pre[head_of(i)[0], head_of(i)[1]]["b_rows"][head_of(i)[2]:head_of(i)[2] + 1, :] for i in rng]
    ig_row = [pre[head_of(i)[0], head_of(i)[1]]["ig_rows"][head_of(i)[2]:head_of(i)[2] + 1, :] for i in rng]
    a_col = [pre[head_of(i)[0], head_of(i)[1]]["a_cols"][:, head_of(i)[2]:head_of(i)[2] + 1] for i in rng]
    m_st = [m_s[head_of(i)[0], H_MLSTM * head_of(i)[1] + head_of(i)[2]:H_MLSTM * head_of(i)[1] + head_of(i)[2] + 1, :]
            for i in rng]
    bl = [b_row[i][:, lc - 1:lc] if head_of(i)[1] == 0 else b_row[i][:, 0:1] for i in rng]

    d_t = [jnp.where(keep_t[head_of(i)[1]], b_row[i] + a_col[i], NEG) for i in rng]
    inter = [b_row[i] + m_st[i] for i in rng]
    m_t = [jnp.maximum(inter[i], jnp.max(d_t[i], axis=0, keepdims=True)) for i in rng]
    s_t = [_dot_nt(k_h[i], q_p[heads[i][0]]) * jnp.exp(d_t[i] - m_t[i]) for i in rng]
    w_in = [jnp.exp(inter[i] - m_t[i]) for i in rng]
    num_t = [_dot(vt_p[heads[i][0]].astype(BF16), s_t[i].astype(BF16)) for i in rng]
    qc_t = [_dot_nt(c_p[pi].astype(BF16), q_p[pi]) for pi in range(len(pairs))]
    n_mat = [jnp.where((row8 == 0) & lo_lane, n_p[pi], jnp.where((row8 == 1) & ~lo_lane, n_p[pi], 0.0))
             for pi in range(len(pairs))]
    nq = [_dot_nt(n_mat[pi].astype(BF16), q_p[pi]) for pi in range(len(pairs))]
    den = [jnp.sum(s_t[i], axis=0, keepdims=True) + w_in[i] * nq[heads[i][0]][heads[i][1]:heads[i][1] + 1, :]
           for i in rng]
    inv = [1.0 / jnp.maximum(jnp.abs(den[i]), jnp.exp(-m_t[i])) for i in rng]
    h_t = []
    for pi in range(len(pairs)):
        i0, i1 = 2 * pi, 2 * pi + 1
        num = jnp.where(lo_row, num_t[i0], num_t[i1])
        h_t.append((num + jnp.where(lo_row, w_in[i0], w_in[i1]) * qc_t[pi]) * jnp.where(lo_row, inv[i0], inv[i1]))
    for j in range(sps):
        base = 2 * n_pairs * j
        hf_ref[j] = jnp.concatenate(h_t[base:base + n_pairs], axis=0)
        hb_ref[j] = jnp.concatenate(h_t[base + n_pairs:base + 2 * n_pairs], axis=0)

    g_row = [bl[i] - b_row[i] + ig_row[i] for i in rng]
    m_new = [jnp.maximum(bl[i] + m_st[i], jnp.max(g_row[i], axis=-1, keepdims=True)) for i in rng]
    wc = [jnp.exp(bl[i] + m_st[i] - m_new[i]) for i in rng]
    ws_row = [jnp.exp(g_row[i] - m_new[i]) for i in rng]
    upd = [_dot((vt_p[heads[i][0]] * ws_row[i]).astype(BF16), k_p[heads[i][0]]) for i in rng]
    for pi, (j, d, hp) in enumerate(pairs):
        i0, i1 = 2 * pi, 2 * pi + 1
        block = jnp.where(lo_row & lo_lane, upd[i0], jnp.where(~lo_row & ~lo_lane, upd[i1], 0.0))
        c_s[j, d, hp] = jnp.where(lo_row, wc[i0], wc[i1]) * c_p[pi] + block
        ws_mat = jnp.where(row8 == 0, ws_row[i0], jnp.where(row8 == 1, ws_row[i1], 0.0))
        k_sum = _dot(ws_mat.astype(BF16), k_p[pi])
        row = n_pairs * d + hp
        n_s[j, row:row + 1, :] = (jnp.where(lo_lane, wc[i0], wc[i1]) * n_p[pi]
                                  + jnp.where(lo_lane, k_sum[0:1, :], k_sum[1:2, :]))
    for i in rng:
        j, d, hd = head_of(i)
        m_s[j, H_MLSTM * d + hd:H_MLSTM * d + hd + 1, :] = m_new[i]

    @pl.when(c == n_chunks - 1)
    def _():
        cout_ref[...] = c_s[...]
        nout_ref[...] = n_s[...]
        mout_ref[...] = m_s[...]


def _pair_states(c):
    b = c.shape[0]
    c = c.reshape(b, 2, H_MLSTM // 2, 2, HEAD_DIM, HEAD_DIM)
    zero = jnp.zeros_like(c[:, :, :, 0])
    top = jnp.concatenate([c[:, :, :, 0], zero], axis=-1)
    bottom = jnp.concatenate([zero, c[:, :, :, 1]], axis=-1)
    return jnp.concatenate([top, bottom], axis=-2)


def _unpair_states(cp):
    b = cp.shape[0]
    first = cp[:, :, :, :HEAD_DIM, :HEAD_DIM]
    second = cp[:, :, :, HEAD_DIM:, HEAD_DIM:]
    return jnp.stack([first, second], axis=3).reshape(b, 2, H_MLSTM, HEAD_DIM, HEAD_DIM)


def _mlstm(p, g, gt, vo, c0, n0, m0, tril, triu, n_seq, seq_len):
    lc = MLSTM_CHUNK
    nc = seq_len // lc
    n_st = 2 * H_MLSTM
    n_pairs = H_MLSTM // 2
    pair_w = 2 * HEAD_DIM
    sps = SEQS_PER_STEP
    p3 = p.reshape(n_seq, seq_len, P_KEEP)
    g3 = g.reshape(n_seq, seq_len, N_GATE_COLS)

    def fwd(c):
        return c

    def bwd(c):
        return nc - 1 - c

    def side(chunk):
        tok = lambda col: (lambda b, c: (b, chunk(c), col))
        specs = [
            pl.BlockSpec((sps, lc, C_W), tok(QC_BLK)),
            pl.BlockSpec((sps, lc, C_W), tok(KC_BLK)),
            pl.BlockSpec((sps, lc, N_GATE_COLS), tok(0)),
        ]
        for j in range(sps):
            lanes = lambda b, c, j=j: (0, (b * sps + j) * nc + chunk(c))
            specs += [pl.BlockSpec((N_GATE_COLS, lc), lanes), pl.BlockSpec((C_W, lc), lanes)]
        return specs

    state_specs = [
        pl.BlockSpec((sps, 2, n_pairs, pair_w, pair_w), lambda b, c: (b, 0, 0, 0, 0)),
        pl.BlockSpec((sps, 2 * n_pairs, pair_w), lambda b, c: (b, 0, 0)),
        pl.BlockSpec((sps, n_st, 1), lambda b, c: (b, 0, 0)),
    ]
    tri_spec = pl.BlockSpec((lc, lc), lambda b, c: (0, 0))
    operands = [p3, p3, g3] + [gt, vo] * sps
    hf, hb, c_out, n_out, m_out = pl.pallas_call(
        functools.partial(_mlstm_kernel, n_chunks=nc),
        grid=(n_seq // sps, nc),
        in_specs=side(fwd) + side(bwd) + state_specs + [tri_spec, tri_spec],
        out_specs=[
            pl.BlockSpec((sps, C_W, lc), lambda b, c: (b, 0, c)),
            pl.BlockSpec((sps, C_W, lc), lambda b, c: (b, 0, nc - 1 - c)),
        ] + state_specs,
        out_shape=[
            jax.ShapeDtypeStruct((n_seq, C_W, seq_len), F32),
            jax.ShapeDtypeStruct((n_seq, C_W, seq_len), F32),
            jax.ShapeDtypeStruct((n_seq, 2, n_pairs, pair_w, pair_w), F32),
            jax.ShapeDtypeStruct((n_seq, 2 * n_pairs, pair_w), F32),
            jax.ShapeDtypeStruct((n_seq, n_st, 1), F32),
        ],
        scratch_shapes=[
            pltpu.VMEM((sps, 2, n_pairs, pair_w, pair_w), F32),
            pltpu.VMEM((sps, 2 * n_pairs, pair_w), F32),
            pltpu.VMEM((sps, n_st, 1), F32),
        ],
        compiler_params=_params(("arbitrary", "arbitrary")),
        name="mlstm",
    )(*operands, *operands, _pair_states(c0), n0.reshape(n_seq, 2 * n_pairs, pair_w), m0.reshape(n_seq, n_st, 1),
      tril, triu)
    return (hf, hb, _unpair_states(c_out), n_out.reshape(n_seq, 2, H_MLSTM, HEAD_DIM),
            m_out.reshape(n_seq, 2, H_MLSTM))


def _head_norm(y, g, ones_blk):
    ysq = y * y
    hi = ysq.astype(BF16)
    lo = (ysq - hi.astype(F32)).astype(BF16)
    ss = _dot(hi, ones_blk) + _dot(lo, ones_blk)
    return y * lax.rsqrt(ss * (1.0 / HEAD_DIM) + EPS) * g


def _merge_kernel(att_ref, four_ref, hf_ref, hb_ref, oc_ref, x_ref, mod_ref, gh_ref, ghm_ref, wo_ref, g2_ref, wrt_ref,
                  ones_ref, xo_ref, h2_ref, afft_ref, *, row_tiles):
    gh = gh_ref[0]
    half = x_ref.shape[0] // 2
    parts = [slice(half * p, half * (p + 1)) for p in range(2)]
    ya = [_head_norm(att_ref[r, :], gh[:, :A_W], ones_ref[...]) for r in parts]
    four = [jnp.concatenate([four_ref[c, r, :] for c in range(B_W // LANE)], axis=1) for r in parts]
    yf = [_head_norm(f, gh[:, A_W:A_W + B_W], ones_ref[:B_W, :B_W]) for f in four]
    mem = [hf_ref[0, :, r] + hb_ref[0, :, r] for r in parts]
    ym_t = []
    for p, r in enumerate(parts):
        heads = []
        for hd in range(H_MLSTM):
            y = mem[p][HEAD_DIM * hd:HEAD_DIM * (hd + 1), :]
            heads.append(y * lax.rsqrt(jnp.mean(y * y, axis=0, keepdims=True) + EPS))
        ym_t.append(jnp.concatenate(heads, axis=0) * ghm_ref[0, :, r] * jax.nn.sigmoid(oc_ref[:, r]))
    out = [_dot(ya[p].astype(BF16), wo_ref[0, :A_W, :])
           + _dot(yf[p].astype(BF16), wo_ref[0, A_W:A_W + B_W, :])
           + _dot_tn(ym_t[p].astype(BF16), wo_ref[0, A_W + B_W:, :]) for p in range(2)]
    x = [x_ref[r, :] + mod_ref[0, 2:3, :] * out[p] for p, r in enumerate(parts)]
    y2 = [xp * lax.rsqrt(jnp.mean(xp * xp, axis=-1, keepdims=True) + EPS) * g2_ref[0] for xp in x]
    h2 = [(yp * (1.0 + mod_ref[0, 4:5, :]) + mod_ref[0, 3:4, :]).astype(BF16) for yp in y2]
    logits = [_dot_nt(wrt_ref[0], hp) for hp in h2]
    e = [jnp.exp(lg - jnp.max(lg, axis=0, keepdims=True)) for lg in logits]
    for p, r in enumerate(parts):
        xo_ref[r, :] = x[p]
        if row_tiles:
            h2_wide = h2[p].astype(F32)
            for s in range(N_SLAB):
                h2_ref[pl.ds(N_SLAB * half * p + s, half, stride=N_SLAB), :] = h2_wide[:, LANE * s:LANE * (s + 1)]
        else:
            h2_ref[r, :] = h2[p]
        afft_ref[:, r] = e[p] / jnp.sum(e[p], axis=0, keepdims=True)


def _merge(att, four, hf, hb, vo, x2d, mod, layer, g_head, g_mem, w_out_bf, g2, w_rt_bf, ones_blk, n_seq, seq_len,
           mod_seq_len, row_tiles):
    t = x2d.shape[0]
    tm = MERGE_TILE
    tiles_per_mod = mod_seq_len // tm
    tiles_per_seq = seq_len // tm
    row = lambda i: (i, 0)
    lay = lambda i: (layer, 0, 0)
    mem = lambda i: (i // tiles_per_seq, 0, i % tiles_per_seq)
    if row_tiles:
        h2_spec, h2_shape = pl.BlockSpec((N_SLAB * tm, LANE), row), jax.ShapeDtypeStruct((N_SLAB * t, LANE), F32)
    else:
        h2_spec, h2_shape = pl.BlockSpec((tm, D_MODEL), row), jax.ShapeDtypeStruct((t, D_MODEL), BF16)
    return pl.pallas_call(
        functools.partial(_merge_kernel, row_tiles=row_tiles),
        grid=(t // tm,),
        in_specs=[
            pl.BlockSpec((tm, A_W), row),
            pl.BlockSpec((B_W // LANE, tm, LANE), lambda i: (0, i, 0)),
            pl.BlockSpec((1, C_W, tm), mem),
            pl.BlockSpec((1, C_W, tm), mem),
            pl.BlockSpec((C_W, tm), lambda i: (1, i)),
            pl.BlockSpec((tm, D_MODEL), row),
            pl.BlockSpec((1, 6, D_MODEL), lambda i: (i // tiles_per_mod, 0, 0)),
            pl.BlockSpec((1, 1, D_MODEL), lay),
            pl.BlockSpec((1, C_W, tm), lay),
            pl.BlockSpec((1, D_MODEL, D_MODEL), lay),
            pl.BlockSpec((1, 1, D_MODEL), lay),
            pl.BlockSpec((1, N_EXPERTS, D_MODEL), lay),
            pl.BlockSpec((A_W, A_W), lambda i: (0, 0)),
        ],
        out_specs=[
            pl.BlockSpec((tm, D_MODEL), row),
            h2_spec,
            pl.BlockSpec((N_EXPERTS, tm), lambda i: (0, i)),
        ],
        out_shape=[
            jax.ShapeDtypeStruct((t, D_MODEL), F32),
            h2_shape,
            jax.ShapeDtypeStruct((N_EXPERTS, t), F32),
        ],
        compiler_params=_params(("arbitrary",)),
        name="merge",
    )(att, four, hf, hb, vo, x2d, mod, g_head, g_mem, w_out_bf, g2, w_rt_bf, ones_blk)


BISECT_STEPS = 48
TOKEN_CHUNK = 1024


TOKEN_SPLIT = 64


def _route_kernel(aff_ref, triu_ref, idx_ref, gs_ref, sp_ref, *blocked, ns, seq_len, cap):
    seqs = range(ns)
    aff = [aff_ref[:, seq_len * j:seq_len * (j + 1)] for j in seqs]

    def body(_, bounds):
        out = []
        for j in seqs:
            lo, hi = bounds[j]
            mid = 0.5 * (lo + hi)
            ge = jnp.sum(jnp.where(aff[j] >= mid, 1.0, 0.0), axis=1, keepdims=True) >= cap
            out.append((jnp.where(ge, mid, lo), jnp.where(ge, hi, mid)))
        return tuple(out)

    start = (jnp.zeros((N_EXPERTS, 1), F32), jnp.full((N_EXPERTS, 1), 2.0, F32))
    bounds = lax.fori_loop(0, BISECT_STEPS, body, tuple(start for _ in seqs))
    thr = [jnp.max(jnp.where(aff[j] < bounds[j][1], aff[j], -1.0), axis=1, keepdims=True) for j in seqs]
    need = [cap - jnp.sum(jnp.where(aff[j] > thr[j], 1.0, 0.0), axis=1, keepdims=True) for j in seqs]
    triu = triu_ref[...]
    eq_carry = [jnp.zeros((N_EXPERTS, 1), F32) for _ in seqs]
    pos_carry = [jnp.zeros((N_EXPERTS, 1), F32) for _ in seqs]
    n_blk = seq_len // 128
    lane = lax.broadcasted_iota(jnp.int32, (1, 128), 1)
    blocked_refs = blocked if blocked else None
    first = [jnp.zeros((N_EXPERTS, 128), F32) for _ in seqs]
    after = [jnp.full((N_EXPERTS, 128), 1e9, F32) for _ in seqs]
    for b in range(n_blk):
        for j in seqs:
            blk = aff[j][:, 128 * b:128 * (b + 1)]
            eq = blk == thr[j]
            eq_f = jnp.where(eq, 1.0, 0.0)
            eq_inc = _dot(eq_f.astype(BF16), triu) + eq_carry[j]
            sel = (blk > thr[j]) | (eq & (eq_inc - eq_f < need[j]))
            sel_f = jnp.where(sel, 1.0, 0.0)
            pos_inc = _dot(sel_f.astype(BF16), triu) + pos_carry[j]
            t0 = seq_len * j + 128 * b
            sp_ref[:, t0:t0 + 128] = jnp.where(sel, pos_inc - sel_f, -1.0).astype(jnp.int32)
            if blocked_refs:
                cum_s, aff_s, _ = blocked_refs
                r0 = (j * n_blk + b) * N_EXPERTS
                cum_s[r0:r0 + N_EXPERTS, :] = pos_inc - pos_carry[j]
                aff_s[r0:r0 + N_EXPERTS, :] = blk
                first[j] = jnp.where(lane == b, pos_carry[j], first[j])
                after[j] = jnp.where(lane == b, pos_inc[:, 127:128], after[j])
            eq_carry[j] = eq_inc[:, 127:128]
            pos_carry[j] = pos_inc[:, 127:128]

    if blocked_refs:
        cum_s, aff_s, bnd_s = blocked_refs
        for j in seqs:
            bnd_s[N_EXPERTS * j:N_EXPERTS * (j + 1), :] = first[j]
            bnd_s[N_EXPERTS * (ns + j):N_EXPERTS * (ns + j + 1), :] = after[j]
        slot_col = lax.broadcasted_iota(jnp.int32, (cap, 1), 0).astype(F32)
        lane_f = lane.astype(F32)

        def per_expert(e, carry):
            for j in seqs:
                first_row = bnd_s[pl.ds(N_EXPERTS * j + e, 1), :]
                after_row = bnd_s[pl.ds(N_EXPERTS * (ns + j) + e, 1), :]
                blk_of = jnp.sum(jnp.where(after_row <= slot_col, 1.0, 0.0), axis=1, keepdims=True)
                in_blk = lane_f == blk_of
                local = slot_col - jnp.sum(jnp.where(in_blk, first_row, 0.0), axis=1, keepdims=True)
                pick = jnp.where(in_blk, 1.0, 0.0)[:, :n_blk].astype(BF16)
                rows = pl.ds(N_EXPERTS * n_blk * j + e, n_blk, stride=N_EXPERTS)
                counts = _dot(pick, cum_s[rows, :].astype(BF16))
                tok_in = jnp.sum(jnp.where(counts <= local, 1.0, 0.0), axis=1, keepdims=True)
                a = aff_s[rows, :]
                a_hi = a.astype(BF16)
                a_mid = (a - a_hi.astype(F32)).astype(BF16)
                a_lo = (a - a_hi.astype(F32) - a_mid.astype(F32)).astype(BF16)
                gates = _dot(pick, a_hi) + _dot(pick, a_mid) + _dot(pick, a_lo)
                idx_ref[N_EXPERTS * j + e] = (N_SLAB * (128.0 * blk_of + tok_in)).astype(jnp.int32)
                gs_ref[e, cap * j:cap * (j + 1)] = jnp.sum(jnp.where(lane_f == tok_in, gates, 0.0), axis=1,
                                                           keepdims=True)
            return carry

        lax.fori_loop(0, N_EXPERTS, per_expert, 0)
        return

    tc = min(seq_len, TOKEN_CHUNK)
    slot = lax.broadcasted_iota(jnp.int32, (cap, tc), 0)
    part = lax.broadcasted_iota(jnp.int32, (8, tc), 0)
    tok = lax.broadcasted_iota(jnp.int32, (1, tc), 1).astype(F32)
    chunks = range(0, seq_len, tc)
    tok_hi = [jnp.floor((tok + float(t0)) * (1.0 / TOKEN_SPLIT)) for t0 in chunks]
    tok_lo = [tok + float(t0) - TOKEN_SPLIT * hi for t0, hi in zip(chunks, tok_hi)]

    def per_expert(e, carry):
        for j in seqs:
            acc = jnp.zeros((cap, 8), F32)
            for ci, t0 in enumerate(chunks):
                cols = slice(seq_len * j + t0, seq_len * j + t0 + tc)
                onehot = jnp.where(slot == sp_ref[pl.ds(e, 1), cols], 1.0, 0.0).astype(BF16)
                a = aff_ref[pl.ds(e, 1), cols]
                a_hi = a.astype(BF16).astype(F32)
                a_mid = (a - a_hi).astype(BF16).astype(F32)
                a_lo = a - a_hi - a_mid
                vals = jnp.where(part == 0, tok_hi[ci], jnp.where(part == 1, tok_lo[ci], jnp.where(
                    part == 2, a_hi, jnp.where(part == 3, a_mid, jnp.where(part == 4, a_lo, 0.0)))))
                acc = acc + _dot_nt(onehot, vals.astype(BF16))
            idx_ref[N_EXPERTS * j + e] = (N_SLAB * (TOKEN_SPLIT * acc[:, 0:1] + acc[:, 1:2])).astype(jnp.int32)
            gs_ref[e, cap * j:cap * (j + 1)] = acc[:, 2:3] + acc[:, 3:4] + acc[:, 4:5]
        return carry

    lax.fori_loop(0, N_EXPERTS, per_expert, 0)


def _route(afft, triu_bf, n_seq, seq_len, cap, ns):
    n_blk = seq_len // 128
    idx, gs, sp = pl.pallas_call(
        functools.partial(_route_kernel, ns=ns, seq_len=seq_len, cap=cap),
        grid=(n_seq // ns,),
        in_specs=[
            pl.BlockSpec((N_EXPERTS, ns * seq_len), lambda s: (0, s)),
            pl.BlockSpec((128, 128), lambda s: (0, 0)),
        ],
        out_specs=[
            pl.BlockSpec((ns * N_EXPERTS, cap, 1), lambda s: (s, 0, 0)),
            pl.BlockSpec((N_EXPERTS, ns * cap, 1), lambda s: (0, s, 0)),
            pl.BlockSpec((N_EXPERTS, ns * seq_len), lambda s: (0, s)),
        ],
        out_shape=[
            jax.ShapeDtypeStruct((n_seq * N_EXPERTS, cap, 1), jnp.int32),
            jax.ShapeDtypeStruct((N_EXPERTS, n_seq * cap, 1), F32),
            jax.ShapeDtypeStruct((N_EXPERTS, n_seq * seq_len), jnp.int32),
        ],
        scratch_shapes=[
            pltpu.VMEM((ns * n_blk * N_EXPERTS, 128), F32),
            pltpu.VMEM((ns * n_blk * N_EXPERTS, 128), F32),
            pltpu.VMEM((2 * ns * N_EXPERTS, 128), F32),
        ] if n_blk >= 16 else [],
        compiler_params=_params(("arbitrary",)),
        name="route",
    )(afft, triu_bf)
    return idx.reshape(n_seq * N_EXPERTS * cap), gs, sp


ROW_COPIES = 8


def _row_tile(first_row):
    return pl.ds(pl.multiple_of(first_row, N_SLAB), N_SLAB)


def _gather_kernel(idx_ref, src_ref, xs_ref, tile_ref, *, eb, cap):
    ei = pl.program_id(1)

    def per_expert(ee, carry):
        e = ei * eb + ee

        def rows(g, c):
            slot0 = g * ROW_COPIES
            for u in range(ROW_COPIES):
                tile_ref[_row_tile((slot0 + u) * N_SLAB), :] = src_ref[_row_tile(idx_ref[e * cap + slot0 + u]), :]
            return c

        lax.fori_loop(0, cap // ROW_COPIES, rows, 0)
        for s in range(N_SLAB):
            xs_ref[ee, :, LANE * s:LANE * (s + 1)] = tile_ref[pl.ds(s, cap, stride=N_SLAB), :].astype(BF16)
        return carry

    lax.fori_loop(0, eb, per_expert, 0)


def _gather(idx, h2_rows, n_seq, cap, eb):
    return pl.pallas_call(
        functools.partial(_gather_kernel, eb=eb, cap=cap),
        grid=(n_seq, N_EXPERTS // eb),
        in_specs=[
            pl.BlockSpec((N_EXPERTS * cap,), lambda s, e: (s,), memory_space=pltpu.SMEM),
            pl.BlockSpec((h2_rows.shape[0] // n_seq, LANE), lambda s, e: (s, 0)),
        ],
        out_specs=pl.BlockSpec((eb, cap, D_MODEL), lambda s, e: (e, s, 0)),
        out_shape=jax.ShapeDtypeStruct((N_EXPERTS, n_seq * cap, D_MODEL), BF16),
        scratch_shapes=[pltpu.VMEM((N_SLAB * cap, LANE), F32)],
        compiler_params=_params(("arbitrary", "arbitrary"), vmem_mib=56),
        name="gather",
    )(idx, h2_rows)


def _selection(sp_ref, cap):
    seq_len = sp_ref.shape[1]
    slot = lax.broadcasted_iota(jnp.int32, (cap, seq_len), 0)
    return jnp.concatenate([jnp.where(slot == sp_ref[e:e + 1, :], 1.0, 0.0) for e in range(N_EXPERTS)],
                           axis=0).astype(BF16)


def _gather_short_kernel(sp_ref, h_ref, xs_ref, *, cap):
    rows = _dot(_selection(sp_ref, cap), h_ref[...])
    for e in range(N_EXPERTS):
        xs_ref[e] = rows[cap * e:cap * (e + 1), :].astype(BF16)


def _gather_short(sp, h2, n_seq, seq_len, cap):
    return pl.pallas_call(
        functools.partial(_gather_short_kernel, cap=cap),
        grid=(n_seq,),
        in_specs=[
            pl.BlockSpec((N_EXPERTS, seq_len), lambda s: (0, s)),
            pl.BlockSpec((seq_len, D_MODEL), lambda s: (s, 0)),
        ],
        out_specs=pl.BlockSpec((N_EXPERTS, cap, D_MODEL), lambda s: (0, s, 0)),
        out_shape=jax.ShapeDtypeStruct((N_EXPERTS, n_seq * cap, D_MODEL), BF16),
        compiler_params=_params(("arbitrary",)),
        name="gather_short",
    )(sp, h2)


def _scatter_short_kernel(sp_ref, y_ref, x_ref, mod_ref, gf_ref, o_ref, *, cap, final):
    sel = _selection(sp_ref, cap)
    y = jnp.concatenate([y_ref[e] for e in range(N_EXPERTS)], axis=0)
    y_hi = y.astype(BF16)
    y_lo = (y - y_hi.astype(F32)).astype(BF16)
    moe = _dot_tn(sel, y_hi) + _dot_tn(sel, y_lo)
    x = x_ref[...] + mod_ref[0, 5:6, :] * moe
    if final:
        x = x * lax.rsqrt(jnp.mean(x * x, axis=-1, keepdims=True) + EPS) * gf_ref[...]
    o_ref[...] = x


def _scatter_short(sp, ys, x2d, mod, g_final, n_seq, seq_len, cap, final):
    return pl.pallas_call(
        functools.partial(_scatter_short_kernel, cap=cap, final=final),
        grid=(n_seq,),
        in_specs=[
            pl.BlockSpec((N_EXPERTS, seq_len), lambda s: (0, s)),
            pl.BlockSpec((N_EXPERTS, cap, D_MODEL), lambda s: (0, s, 0)),
            pl.BlockSpec((seq_len, D_MODEL), lambda s: (s, 0)),
            pl.BlockSpec((1, 6, D_MODEL), lambda s: (0, 0, 0)),
            pl.BlockSpec((1, D_MODEL), lambda s: (0, 0)),
        ],
        out_specs=pl.BlockSpec((seq_len, D_MODEL), lambda s: (s, 0)),
        out_shape=jax.ShapeDtypeStruct((n_seq * seq_len, D_MODEL), F32),
        compiler_params=_params(("arbitrary",)),
        name="scatter_short",
    )(sp, ys, x2d, mod, g_final)


def _expert_kernel(xc_ref, xl_ref, gc_ref, gl_ref, wg_ref, wu_ref, wd_ref, yc_ref, yl_ref, *, n_f):
    f = pl.program_id(1)

    @pl.when(f == 0)
    def _():
        yc_ref[...] = jnp.zeros_like(yc_ref)
        yl_ref[...] = jnp.zeros_like(yl_ref)

    wg = wg_ref[0, 0].astype(BF16)
    wu = wu_ref[0, 0].astype(BF16)
    wd = wd_ref[0, 0].astype(BF16)
    for x_ref, y_ref in ((xc_ref, yc_ref), (xl_ref, yl_ref)):
        x = x_ref[0]
        mid = (_silu(_dot(x, wg)) * _dot(x, wu)).astype(BF16)
        y_ref[0] += _dot(mid, wd)

    @pl.when(f == n_f - 1)
    def _():
        yc_ref[0] = yc_ref[0] * gc_ref[0]
        yl_ref[0] = yl_ref[0] * gl_ref[0]


def _experts(xs_c, xs_l, gs_c, gs_l, w_g, w_u, w_d, layer):
    rc, rl = xs_c.shape[1], xs_l.shape[1]
    tf = 1024
    n_f = EXPERT_FF // tf
    return pl.pallas_call(
        functools.partial(_expert_kernel, n_f=n_f),
        grid=(N_EXPERTS, n_f),
        in_specs=[
            pl.BlockSpec((1, rc, D_MODEL), lambda e, f: (e, 0, 0)),
            pl.BlockSpec((1, rl, D_MODEL), lambda e, f: (e, 0, 0)),
            pl.BlockSpec((1, rc, 1), lambda e, f: (e, 0, 0)),
            pl.BlockSpec((1, rl, 1), lambda e, f: (e, 0, 0)),
            pl.BlockSpec((1, 1, D_MODEL, tf), lambda e, f: (layer, e, 0, f)),
            pl.BlockSpec((1, 1, D_MODEL, tf), lambda e, f: (layer, e, 0, f)),
            pl.BlockSpec((1, 1, tf, D_MODEL), lambda e, f: (layer, e, f, 0)),
        ],
        out_specs=[
            pl.BlockSpec((1, rc, D_MODEL), lambda e, f: (e, 0, 0)),
            pl.BlockSpec((1, rl, D_MODEL), lambda e, f: (e, 0, 0)),
        ],
        out_shape=[
            jax.ShapeDtypeStruct((N_EXPERTS, rc, D_MODEL), F32),
            jax.ShapeDtypeStruct((N_EXPERTS, rl, D_MODEL), F32),
        ],
        compiler_params=_params(("arbitrary", "arbitrary"), vmem_mib=56),
        name="experts",
    )(xs_c, xs_l, gs_c, gs_l, w_g, w_u, w_d)


def _scatter_kernel(idx_ref, y_ref, x_ref, mod_ref, gf_ref, o_ref, acc_ref, tile_ref, *, eb, n_e, cap, tm, final):
    step = pl.program_id(1)

    @pl.when(step == 0)
    def _():
        acc_ref[...] = jnp.zeros_like(acc_ref)

    @pl.when(step < n_e)
    def _():
        def per_expert(ee, carry):
            e = step * eb + ee
            for s in range(N_SLAB):
                tile_ref[pl.ds(s, cap, stride=N_SLAB), :] = y_ref[ee, :, LANE * s:LANE * (s + 1)]

            def rows(g, c):
                slot0 = g * ROW_COPIES
                dst = [idx_ref[e * cap + slot0 + u] for u in range(ROW_COPIES)]
                new = [acc_ref[_row_tile(dst[u]), :] + tile_ref[_row_tile((slot0 + u) * N_SLAB), :]
                       for u in range(ROW_COPIES)]
                for u in range(ROW_COPIES):
                    acc_ref[_row_tile(dst[u]), :] = new[u]
                return c

            lax.fori_loop(0, cap // ROW_COPIES, rows, 0)
            return carry

        lax.fori_loop(0, eb, per_expert, 0)

    @pl.when(step >= n_e)
    def _():
        base = pl.multiple_of((step - n_e) * tm * N_SLAB, N_SLAB)
        moe = jnp.concatenate([acc_ref[pl.ds(base + s, tm, stride=N_SLAB), :] for s in range(N_SLAB)], axis=1)
        x = x_ref[...] + mod_ref[0, 5:6, :] * moe
        if final:
            x = x * lax.rsqrt(jnp.mean(x * x, axis=-1, keepdims=True) + EPS) * gf_ref[...]
        o_ref[...] = x


def _scatter(idx, ys, x2d, mod, g_final, n_seq, seq_len, cap, eb, tm, final):
    n_e = N_EXPERTS // eb
    n_out = seq_len // tm
    out_blk = lambda s, j: (s * n_out + jnp.maximum(j - n_e, 0), 0)
    return pl.pallas_call(
        functools.partial(_scatter_kernel, eb=eb, n_e=n_e, cap=cap, tm=tm, final=final),
        grid=(n_seq, n_e + n_out),
        in_specs=[
            pl.BlockSpec((N_EXPERTS * cap,), lambda s, j: (s,), memory_space=pltpu.SMEM),
            pl.BlockSpec((eb, cap, D_MODEL), lambda s, j: (jnp.minimum(j, n_e - 1), s, 0)),
            pl.BlockSpec((tm, D_MODEL), out_blk),
            pl.BlockSpec((1, 6, D_MODEL), lambda s, j: (s, 0, 0)),
            pl.BlockSpec((1, D_MODEL), lambda s, j: (0, 0)),
        ],
        out_specs=pl.BlockSpec((tm, D_MODEL), out_blk),
        out_shape=jax.ShapeDtypeStruct((n_seq * seq_len, D_MODEL), F32),
        scratch_shapes=[
            pltpu.VMEM((N_SLAB * seq_len, LANE), F32),
            pltpu.VMEM((N_SLAB * cap, LANE), F32),
        ],
        compiler_params=_params(("arbitrary", "arbitrary"), vmem_mib=56),
        name="scatter",
    )(idx, ys, x2d, mod, g_final)


def kernel(x_prompt, x_sample, c, cache_k, cache_v, state_C, state_n, state_m, c_ctx, w_ada, b_ada, g_norm1, g_norm2, w_in, b_gates, rpb, w_fourier, g_head, w_out, w_router, w_exp_gate, w_exp_up, w_exp_down, g_final):
    n_ctx, len_ctx, _ = x_prompt.shape
    n_lat, len_lat, _ = x_sample.shape
    past = cache_k.shape[2]
    cap_ctx = CAPACITY_FACTOR * len_ctx // N_EXPERTS
    cap_lat = CAPACITY_FACTOR * len_lat // N_EXPERTS

    w_in_bf = w_in.astype(BF16)
    w_gt_bf = jnp.swapaxes(w_in[:, :, P_COLS:], 1, 2).astype(BF16)
    vc0, oc0 = VC_BLK * C_W, OC_BLK * C_W
    w_vo_bf = jnp.swapaxes(jnp.concatenate([w_in[:, :, vc0:vc0 + C_W], w_in[:, :, oc0:oc0 + C_W]], axis=2),
                           1, 2).astype(BF16)
    bg_row = b_gates.reshape(DEPTH, 1, N_GATE_COLS).astype(F32)
    bg_col = b_gates.reshape(DEPTH, N_GATE_COLS, 1).astype(F32)
    w_out_bf = w_out.astype(BF16)
    w_rt_bf = jnp.swapaxes(w_router, 1, 2).astype(BF16)
    g1 = g_norm1.reshape(DEPTH, 1, D_MODEL)
    g2 = g_norm2.reshape(DEPTH, 1, D_MODEL)
    gh = g_head.reshape(DEPTH, 1, D_MODEL)
    g_mem = jnp.broadcast_to(gh[:, 0, A_W + B_W:, None], (DEPTH, C_W, MERGE_TILE))
    eye_g = jnp.eye(G_FOURIER, dtype=F32)
    wf_blk = jnp.einsum("lgcd,gh->lgchd", w_fourier, eye_g).reshape(DEPTH, B_W, B_W).astype(BF16)

    csc = _channel_dft()
    dft_ctx = _dft_mats(len_ctx)
    dft_side = _dft_mats(GRID_W)
    tw_cos, tw_sin = _twiddles(GRID_W, len_lat)
    bias_tiles = _nbr_bias_tiles(rpb)
    r = np.arange(MLSTM_CHUNK)
    tril = jnp.asarray(r[:, None] >= r[None, :], F32)
    triu = jnp.asarray(r[:, None] <= r[None, :], F32)
    triu_bf = triu.astype(BF16)
    hidx = np.arange(A_W) // HEAD_DIM
    ones_blk = jnp.asarray(hidx[:, None] == hidx[None, :], BF16)

    cvecs = jnp.concatenate([c_ctx[None, :], c, jnp.zeros((8 - 1 - n_lat, D_MODEL), F32)], axis=0)
    mod_all = _modulation(cvecs, w_ada, b_ada).reshape(DEPTH, 8, 6, D_MODEL)

    cache_k4 = cache_k.reshape(n_lat, DEPTH, past, A_W)
    cache_v4 = cache_v.reshape(n_lat, DEPTH, past, A_W)
    zero_c = jnp.zeros((n_ctx, 2, H_MLSTM, HEAD_DIM, HEAD_DIM), F32)
    zero_n = jnp.zeros((n_ctx, 2, H_MLSTM, HEAD_DIM), F32)
    zero_m = jnp.zeros((n_ctx, 2, H_MLSTM), F32)

    xc = x_prompt.reshape(n_ctx * len_ctx, D_MODEL)
    xl = x_sample.reshape(n_lat * len_lat, D_MODEL)
    gf = g_final.reshape(1, D_MODEL)
    new_k = jnp.zeros((n_ctx, DEPTH, len_ctx, A_W), F32)
    new_v = jnp.zeros((n_ctx, DEPTH, len_ctx, A_W), F32)
    cs, ns, ms = [], [], []
    for l in range(DEPTH):
        mod_c = mod_all[l, 0:1]
        mod_l = mod_all[l, 1:1 + n_lat]

        pc, gc, gtc, voc, abc, new_k, new_v = _inproj(xc, mod_c, l, g1, w_in_bf, w_gt_bf, w_vo_bf, bg_row, bg_col, csc,
                                                      n_ctx * len_ctx, False, (new_k, new_v))
        att_c = _ctx_attention(pc, n_ctx, len_ctx)
        four_c = _fourier(abc, dft_ctx[0], dft_ctx[1], wf_blk, l, n_ctx, len_ctx)
        hf_c, hb_c, c_new, n_new, m_new = _mlstm(pc, gc, gtc, voc, zero_c, zero_n, zero_m, tril, triu, n_ctx, len_ctx)
        xc, h2c, affc = _merge(att_c, four_c, hf_c, hb_c, voc, xc, mod_c, l, gh, g_mem, w_out_bf, g2, w_rt_bf,
                               ones_blk, n_ctx, len_ctx, n_ctx * len_ctx, False)
        cs.append(c_new)
        ns.append(n_new)
        ms.append(m_new)

        pq, gq, gtq, voq, abq = _inproj(xl, mod_l, l, g1, w_in_bf, w_gt_bf, w_vo_bf, bg_row, bg_col, csc, len_lat,
                                        True)
        att_l = _nbr_attention(pq, cache_k4, cache_v4, bias_tiles, l, n_lat, len_lat)
        four_l = _fourier_grid(abq, dft_side[0], dft_side[1], tw_cos, tw_sin, wf_blk, l, n_lat, GRID_W)
        hf_l, hb_l, _, _, _ = _mlstm(pq, gq, gtq, voq, state_C[:, l], state_n[:, l], state_m[:, l], tril, triu,
                                     n_lat, len_lat)
        xl, h2l, affl = _merge(att_l, four_l, hf_l, hb_l, voq, xl, mod_l, l, gh, g_mem, w_out_bf, g2, w_rt_bf,
                               ones_blk, n_lat, len_lat, len_lat, True)

        last = l == DEPTH - 1
        _, gs_c, sp_c = _route(affc, triu_bf, n_ctx, len_ctx, cap_ctx, n_ctx)
        idx_l, gs_l, _ = _route(affl, triu_bf, n_lat, len_lat, cap_lat, 1)
        xs_c = _gather_short(sp_c, h2c, n_ctx, len_ctx, cap_ctx)
        xs_l = _gather(idx_l, h2l, n_lat, cap_lat, 4)
        ys_c, ys_l = _experts(xs_c, xs_l, gs_c, gs_l, w_exp_gate, w_exp_up, w_exp_down, l)
        xc = _scatter_short(sp_c, ys_c, xc, mod_c, gf, n_ctx, len_ctx, cap_ctx, last)
        xl = _scatter(idx_l, ys_l, xl, mod_l, gf, n_lat, len_lat, cap_lat, 2, 512, last)

    y_prompt = xc.reshape(n_ctx, len_ctx, D_MODEL)
    y_sample = xl.reshape(n_lat, len_lat, D_MODEL)
    kv_shape = (n_ctx, DEPTH, len_ctx, H_ATT, HEAD_DIM)
    return (y_prompt, y_sample, new_k.reshape(kv_shape), new_v.reshape(kv_shape), jnp.stack(cs, axis=1),
            jnp.stack(ns, axis=1), jnp.stack(ms, axis=1))
```

```python
import functools

import numpy as np
import jax
import jax.numpy as jnp
from jax import lax
from jax.experimental import pallas as pl
from jax.experimental.pallas import tpu as pltpu

F32 = jnp.float32
BF16 = jnp.bfloat16

D_MODEL = 1024
DEPTH = 2
HEAD_DIM = 64
H_ATT = 8
G_FOURIER = 4
H_MLSTM = 4
A_W = H_ATT * HEAD_DIM
B_W = G_FOURIER * HEAD_DIM
C_W = H_MLSTM * HEAD_DIM
N_GATE_COLS = 16
P_COLS = 3 * A_W + B_W + 4 * C_W
IN_COLS = P_COLS + N_GATE_COLS
P_KEEP = 3 * A_W + B_W + 2 * C_W
GRID_W = 64
WIN_R = 8
WIN_C = 16
MLSTM_CHUNK = 128
N_EXPERTS = 16
CAPACITY_FACTOR = 2
EXPERT_FF = 2 * D_MODEL
EPS = 1e-6
NEG = -1e30

UB_OFF = 3 * A_W
QC_BLK, KC_BLK, VC_BLK, OC_BLK = 7, 8, 9, 10

LANE = 128
N_SLAB = D_MODEL // LANE
MERGE_TILE = 256
INPROJ_TILE = 512

NT_DIMS = (((1,), (1,)), ((), ()))
TN_DIMS = (((0,), (0,)), ((), ()))
MIB = 1024 * 1024


def _dot(a, b, precision=None):
    return jnp.dot(a, b, preferred_element_type=F32, precision=precision)


def _dot_nt(a, b, precision=None):
    return lax.dot_general(a, b, NT_DIMS, preferred_element_type=F32, precision=precision)


def _dot_tn(a, b):
    return lax.dot_general(a, b, TN_DIMS, preferred_element_type=F32)


def _params(sem, vmem_mib=48):
    return pltpu.CompilerParams(dimension_semantics=sem, vmem_limit_bytes=vmem_mib * MIB)


def _silu(x):
    return x * jax.nn.sigmoid(x)


def _log_sigmoid(x):
    return jnp.minimum(x, 0.0) - jnp.log1p(jnp.exp(-jnp.abs(x)))


def _mod_kernel(c_ref, w_ref, b_ref, o_ref):
    s = _silu(c_ref[...]).astype(BF16)
    o_ref[0] = _dot(s, w_ref[0].astype(BF16)) + b_ref[0]


def _modulation(cvecs, w_ada, b_ada):
    depth = w_ada.shape[0]
    tn = 1024
    return pl.pallas_call(
        _mod_kernel,
        grid=(depth, 6 * D_MODEL // tn),
        in_specs=[
            pl.BlockSpec((8, D_MODEL), lambda l, j: (0, 0)),
            pl.BlockSpec((1, D_MODEL, tn), lambda l, j: (l, 0, j)),
            pl.BlockSpec((1, 1, tn), lambda l, j: (l, 0, j)),
        ],
        out_specs=pl.BlockSpec((1, 8, tn), lambda l, j: (l, 0, j)),
        out_shape=jax.ShapeDtypeStruct((depth, 8, 6 * D_MODEL), F32),
        compiler_params=_params(("arbitrary", "arbitrary")),
        name="modulation",
    )(cvecs, w_ada, b_ada.reshape(depth, 1, 6 * D_MODEL))


def _inproj_kernel(x_ref, mod_ref, g1_ref, w_ref, wgt_ref, wvo_ref, bgr_ref, bgc_ref, csc_ref,
                   *rest, grid_rows, kv_seq_len):
    n_in = 2 if kv_seq_len else 0
    p_ref, g_ref, gt_ref, vo_ref, ab_ref = rest[n_in:n_in + 5]
    kv_refs = rest[n_in + 5:n_in + 5 + n_in]
    scratch = rest[n_in + 5 + n_in:]
    x = x_ref[...]
    y = x * lax.rsqrt(jnp.mean(x * x, axis=-1, keepdims=True) + EPS) * g1_ref[0]
    h = (y * (1.0 + mod_ref[0, 1:2, :]) + mod_ref[0, 0:1, :]).astype(BF16)
    for j in range(0, P_KEEP, 256):
        pj = _dot(h, w_ref[0, :, j:j + 256])
        p_ref[:, j:j + 256] = pj.astype(BF16)
        if kv_seq_len and A_W <= j < 3 * A_W:
            kv_ref = kv_refs[(j - A_W) // A_W]
            c0 = (j - A_W) % A_W
            for b in range(pj.shape[0] // kv_seq_len):
                kv_ref[b, 0, :, c0:c0 + 256] = pj[kv_seq_len * b:kv_seq_len * (b + 1), :]
        if j == UB_OFF:
            ab = _dot(pj.astype(BF16), csc_ref[...])
            if grid_rows:
                stage_ref, = scratch
                n_lt = 2 * B_W // LANE
                for c in range(n_lt):
                    stage_ref[c] = ab[:, LANE * c:LANE * (c + 1)]
                for n2 in range(GRID_W):
                    for c in range(n_lt):
                        col = 2 * B_W * n2 + LANE * c
                        ab_ref[:, col:col + LANE] = stage_ref[c, pl.ds(n2, grid_rows, stride=GRID_W), :]
            else:
                ab_ref[...] = ab.astype(BF16)
    g_ref[...] = _dot(h, w_ref[0, :, P_COLS:IN_COLS]) + bgr_ref[0]
    gt_ref[...] = _dot_nt(wgt_ref[0], h) + bgc_ref[0]
    vo_ref[...] = _dot_nt(wvo_ref[0], h)


def _inproj(x2d, mod, layer, g1, w_in_bf, w_gt_bf, w_vo_bf, bg_row, bg_col, csc, seq_len, grid_ab, kv_cache=None):
    t = x2d.shape[0]
    tm = INPROJ_TILE
    tiles_per_seq = seq_len // tm
    grid_rows = tm // GRID_W if grid_ab else 0
    if grid_ab:
        ab_spec = pl.BlockSpec((grid_rows, GRID_W * 2 * B_W), lambda i: (i, 0))
        ab_shape = jax.ShapeDtypeStruct((t // GRID_W, GRID_W * 2 * B_W), F32)
        scratch = [pltpu.VMEM((2 * B_W // LANE, tm, LANE), F32)]
    else:
        ab_spec = pl.BlockSpec((tm, 2 * B_W), lambda i: (i, 0))
        ab_shape = jax.ShapeDtypeStruct((t, 2 * B_W), BF16)
        scratch = []
    kv_seq_len = kv_cache[0].shape[2] if kv_cache else 0
    kv_in_specs, kv_out_specs, kv_shapes, aliases = [], [], [], {}
    if kv_cache:
        per_tile = tm // kv_seq_len
        kv_in_specs = [pl.BlockSpec(memory_space=pl.ANY)] * 2
        kv_out_specs = [pl.BlockSpec((per_tile, 1, kv_seq_len, A_W), lambda i: (i, layer, 0, 0))] * 2
        kv_shapes = [jax.ShapeDtypeStruct(a.shape, a.dtype) for a in kv_cache]
        aliases = {9: 5, 10: 6}
    return pl.pallas_call(
        functools.partial(_inproj_kernel, grid_rows=grid_rows, kv_seq_len=kv_seq_len),
        grid=(t // tm,),
        input_output_aliases=aliases,
        in_specs=[
            pl.BlockSpec((tm, D_MODEL), lambda i: (i, 0)),
            pl.BlockSpec((1, 6, D_MODEL), lambda i: (i // tiles_per_seq, 0, 0)),
            pl.BlockSpec((1, 1, D_MODEL), lambda i: (layer, 0, 0)),
            pl.BlockSpec((1, D_MODEL, IN_COLS), lambda i: (layer, 0, 0)),
            pl.BlockSpec((1, N_GATE_COLS, D_MODEL), lambda i: (layer, 0, 0)),
            pl.BlockSpec((1, 2 * C_W, D_MODEL), lambda i: (layer, 0, 0)),
            pl.BlockSpec((1, 1, N_GATE_COLS), lambda i: (layer, 0, 0)),
            pl.BlockSpec((1, N_GATE_COLS, 1), lambda i: (layer, 0, 0)),
            pl.BlockSpec((B_W, 2 * B_W), lambda i: (0, 0)),
        ] + kv_in_specs,
        out_specs=[
            pl.BlockSpec((tm, P_KEEP), lambda i: (i, 0)),
            pl.BlockSpec((tm, N_GATE_COLS), lambda i: (i, 0)),
            pl.BlockSpec((N_GATE_COLS, tm), lambda i: (0, i)),
            pl.BlockSpec((2 * C_W, tm), lambda i: (0, i)),
            ab_spec,
        ] + kv_out_specs,
        out_shape=[
            jax.ShapeDtypeStruct((t, P_KEEP), BF16),
            jax.ShapeDtypeStruct((t, N_GATE_COLS), F32),
            jax.ShapeDtypeStruct((N_GATE_COLS, t), F32),
            jax.ShapeDtypeStruct((2 * C_W, t), F32),
            ab_shape,
        ] + kv_shapes,
        scratch_shapes=scratch,
        compiler_params=_params(("arbitrary",)),
        name="inproj",
    )(x2d, mod, g1, w_in_bf, w_gt_bf, w_vo_bf, bg_row, bg_col, csc, *(kv_cache or ()))


def _ctx_attn_kernel(q_ref, k_ref, v_ref, o_ref):
    scale = HEAD_DIM ** -0.5
    heads = range(H_ATT)
    sl = [slice(HEAD_DIM * h, HEAD_DIM * (h + 1)) for h in heads]
    s = [_dot_nt((q_ref[:, sl[h]] * scale).astype(BF16), k_ref[:, sl[h]].astype(BF16)) for h in heads]
    e = [jnp.exp(s[h] - jnp.max(s[h], axis=-1, keepdims=True)) for h in heads]
    w = [e[h] * (1.0 / jnp.sum(e[h], axis=-1, keepdims=True)) for h in heads]
    o_ref[...] = jnp.concatenate([_dot(w[h].astype(BF16), v_ref[:, sl[h]].astype(BF16)) for h in heads], axis=1)


def _ctx_attention(p, n_seq, seq_len):
    return pl.pallas_call(
        _ctx_attn_kernel,
        grid=(n_seq,),
        in_specs=[
            pl.BlockSpec((seq_len, A_W), lambda b: (b, 0)),
            pl.BlockSpec((seq_len, A_W), lambda b: (b, 1)),
            pl.BlockSpec((seq_len, A_W), lambda b: (b, 2)),
        ],
        out_specs=pl.BlockSpec((seq_len, A_W), lambda b: (b, 0)),
        out_shape=jax.ShapeDtypeStruct((n_seq * seq_len, A_W), F32),
        compiler_params=_params(("arbitrary",)),
        name="ctx_attention",
    )(p, p, p)


Q_ROWS = 8
K_ROWS = 16
KEY_BLK = 256


NO_ROW = 2 * WIN_R - 1


def _nbr_bias_tiles(rpb):
    n_c = 2 * WIN_C - 1
    cq = np.arange(GRID_W)[:, None]
    ck = np.arange(GRID_W)[None, :]
    cs = np.clip(cq - WIN_C // 2, 0, GRID_W - WIN_C)
    col_ok = (ck >= cs) & (ck < cs + WIN_C)
    pick = np.where(col_ok, np.clip(ck - cq + WIN_C - 1, 0, n_c - 1), n_c)
    sel = np.zeros((2, GRID_W, 2 * GRID_W, n_c + 1), np.float32)
    for side in range(2):
        sel[side, cq, side * GRID_W + ck, pick] = 1.0
    ext = jnp.full(rpb.shape[:2] + (2 * WIN_R, n_c + 1), NEG, F32)
    ext = ext.at[:, :, :2 * WIN_R - 1, :n_c].set(rpb.astype(F32))
    return jnp.einsum("lhdm,sqkm->lhdsqk", ext, jnp.asarray(sel), precision=lax.Precision.HIGHEST)


def _nbr_attn_kernel(q_ref, k0_ref, k1_ref, k2_ref, k3_ref, v0_ref, v1_ref, v2_ref, v3_ref,
                     ck_ref, cv_ref, tab_ref, o_ref, *, n_rb):
    scale = HEAD_DIM ** -0.5
    k_refs = (k0_ref, k1_ref, k2_ref, k3_ref)
    v_refs = (v0_ref, v1_ref, v2_ref, v3_ref)
    rb = pl.program_id(1)
    rows = GRID_W

    rows_per_blk = KEY_BLK // GRID_W
    n_blk = K_ROWS // rows_per_blk
    half_rows = Q_ROWS // 2
    half_tok = half_rows * GRID_W

    def window(rb_s):
        key_row0 = min(max(Q_ROWS * rb_s - WIN_R // 2, 0), rows - K_ROWS)
        d = []
        for rq in range(Q_ROWS):
            r = Q_ROWS * rb_s + rq
            rs = min(max(r - WIN_R // 2, 0), rows - WIN_R)
            d.append([key_row0 + rk - r + WIN_R - 1 if rs <= key_row0 + rk < rs + WIN_R else NO_ROW
                      for rk in range(K_ROWS)])
        active = tuple(
            tuple(j for j in range(n_blk)
                  if any(d[rq][rk] != NO_ROW for rq in range(half_rows * half, half_rows * (half + 1))
                         for rk in range(rows_per_blk * j, rows_per_blk * (j + 1))))
            for half in range(2))
        return active, d

    def attend(active, d):
        units = [(hh, half) for hh in range(2) for half in range(2)]
        sl = [slice(HEAD_DIM * hh, HEAD_DIM * (hh + 1)) for hh in range(2)]
        rows_of = [slice(half_tok * half, half_tok * (half + 1)) for half in range(2)]
        q = [(q_ref[rows_of[half], sl[hh]] * scale).astype(BF16) for hh, half in units]
        s_ctx = [_dot_nt(q[u], ck_ref[0, 0, :, sl[hh]].astype(BF16)) for u, (hh, half) in enumerate(units)]
        s_loc = []
        for u, (hh, half) in enumerate(units):
            blocks = []
            for j in active[half]:
                bias = jnp.concatenate([
                    jnp.concatenate([
                        tab_ref[0, hh, d[rq][rows_per_blk * j + 2 * p], 0]
                        + tab_ref[0, hh, d[rq][rows_per_blk * j + 2 * p + 1], 1]
                        for p in range(rows_per_blk // 2)], axis=1)
                    for rq in range(half_rows * half, half_rows * (half + 1))], axis=0)
                blocks.append(_dot_nt(q[u], k_refs[j][:, sl[hh]].astype(BF16)) + bias)
            s_loc.append(blocks)
        m = [jnp.max(s_ctx[u], axis=-1, keepdims=True) for u in range(len(units))]
        for u in range(len(units)):
            for s_blk in s_loc[u]:
                m[u] = jnp.maximum(m[u], jnp.max(s_blk, axis=-1, keepdims=True))
        e_ctx = [jnp.exp(s_ctx[u] - m[u]) for u in range(len(units))]
        den = [jnp.sum(e_ctx[u], axis=-1, keepdims=True) for u in range(len(units))]
        num = [_dot(e_ctx[u].astype(BF16), cv_ref[0, 0, :, sl[hh]].astype(BF16)) for u, (hh, half) in enumerate(units)]
        for u, (hh, half) in enumerate(units):
            for j, s_blk in zip(active[half], s_loc[u]):
                e = jnp.exp(s_blk - m[u])
                den[u] = den[u] + jnp.sum(e, axis=-1, keepdims=True)
                num[u] = num[u] + _dot(e.astype(BF16), v_refs[j][:, sl[hh]].astype(BF16))
        out = [num[u] / den[u] for u in range(len(units))]
        o_ref[...] = jnp.concatenate([jnp.concatenate([out[2 * hh], out[2 * hh + 1]], axis=0) for hh in range(2)],
                                     axis=1)

    @pl.when(rb == 0)
    def _():
        attend(*window(0))

    @pl.when((rb > 0) & (rb < n_rb - 1))
    def _():
        attend(*window(1))

    @pl.when(rb == n_rb - 1)
    def _():
        attend(*window(n_rb - 1))


def _nbr_attention(p, cache_k4, cache_v4, bias_tiles, layer, n_seq, seq_len):
    q_tok = Q_ROWS * GRID_W
    n_rb = seq_len // q_tok
    kb_per_seq = seq_len // KEY_BLK
    max_base = kb_per_seq - 4

    def kmap(j, col0):
        def f(hp, rb, b):
            base = jnp.clip(2 * rb - 1, 0, max_base)
            return (b * kb_per_seq + base + j, col0 + hp)
        return f

    past = cache_k4.shape[2]
    in_specs = [pl.BlockSpec((q_tok, 128), lambda hp, rb, b: (b * n_rb + rb, hp))]
    in_specs += [pl.BlockSpec((KEY_BLK, 128), kmap(j, A_W // 128)) for j in range(4)]
    in_specs += [pl.BlockSpec((KEY_BLK, 128), kmap(j, 2 * A_W // 128)) for j in range(4)]
    in_specs += [
        pl.BlockSpec((1, 1, past, 128), lambda hp, rb, b: (b, layer, 0, hp)),
        pl.BlockSpec((1, 1, past, 128), lambda hp, rb, b: (b, layer, 0, hp)),
        pl.BlockSpec((1, 2, 2 * WIN_R, 2, GRID_W, 2 * GRID_W), lambda hp, rb, b: (layer, hp, 0, 0, 0, 0)),
    ]
    assert n_rb >= 3
    return pl.pallas_call(
        functools.partial(_nbr_attn_kernel, n_rb=n_rb),
        grid=(H_ATT // 2, n_rb, n_seq),
        in_specs=in_specs,
        out_specs=pl.BlockSpec((q_tok, 128), lambda hp, rb, b: (b * n_rb + rb, hp)),
        out_shape=jax.ShapeDtypeStruct((n_seq * seq_len, A_W), F32),
        compiler_params=_params(("arbitrary", "arbitrary", "arbitrary")),
        name="nbr_attention",
    )(p, p, p, p, p, p, p, p, p, cache_k4, cache_v4, bias_tiles)


def _dft_mats(n):
    idx = jnp.arange(n, dtype=jnp.int32)
    ang = ((idx[:, None] * idx[None, :]) % n).astype(F32) * (2.0 * np.pi / n)
    return jnp.cos(ang).astype(BF16), jnp.sin(ang).astype(BF16)


def _channel_dft():
    c = np.arange(HEAD_DIM)
    ang = 2.0 * np.pi * ((c[:, None] * c[None, :]) % HEAD_DIM) / HEAD_DIM
    eye = np.eye(G_FOURIER)
    mats = np.concatenate([np.kron(eye, np.cos(ang)), np.kron(eye, np.sin(ang))], axis=1)
    return jnp.asarray(mats, F32).astype(BF16)


def _fourier_kernel(c_ref, s_ref, ab_ref, wf_ref, o_ref, acc_ref, *, scale, n_k):
    k = pl.program_id(2)

    @pl.when(k == 0)
    def _():
        acc_ref[...] = jnp.zeros_like(acc_ref)

    acc_ref[...] += _dot(c_ref[...], ab_ref[:, :B_W]) - _dot(s_ref[...], ab_ref[:, B_W:])

    @pl.when(k == n_k - 1)
    def _():
        z = (acc_ref[...] * scale).astype(BF16)
        o = _dot(z, wf_ref[0])
        for c in range(B_W // LANE):
            o_ref[c] = o[:, LANE * c:LANE * (c + 1)]


def _fourier(ab, cmat, smat, wf_blk, layer, n_seq, seq_len):
    ti = min(seq_len, 512)
    tk = min(seq_len, 1024)
    n_i, n_k = seq_len // ti, seq_len // tk
    scale = float((seq_len * HEAD_DIM) ** -0.5)
    return pl.pallas_call(
        functools.partial(_fourier_kernel, scale=scale, n_k=n_k),
        grid=(n_seq, n_i, n_k),
        in_specs=[
            pl.BlockSpec((ti, tk), lambda s, i, k: (i, k)),
            pl.BlockSpec((ti, tk), lambda s, i, k: (i, k)),
            pl.BlockSpec((tk, 2 * B_W), lambda s, i, k: (s * n_k + k, 0)),
            pl.BlockSpec((1, B_W, B_W), lambda s, i, k: (layer, 0, 0)),
        ],
        out_specs=pl.BlockSpec((B_W // LANE, ti, LANE), lambda s, i, k: (0, s * n_i + i, 0)),
        out_shape=jax.ShapeDtypeStruct((B_W // LANE, n_seq * seq_len, LANE), F32),
        scratch_shapes=[pltpu.VMEM((ti, B_W), F32)],
        compiler_params=_params(("arbitrary", "arbitrary", "arbitrary")),
        name="fourier",
    )(cmat, smat, ab, wf_blk)


FS_GROUP = 8


def _twiddles(side, n):
    k1 = jnp.arange(side, dtype=jnp.int32)[:, None]
    n2 = jnp.arange(side, dtype=jnp.int32)[None, :]
    ang = (k1 * n2).astype(F32) * (2.0 * np.pi / n)
    return jnp.cos(ang), jnp.sin(ang)


def _fourier_grid_kernel(ab_ref, c_ref, s_ref, tc_ref, ts_ref, wf_ref, o_ref, y_ref, *, side, scale):
    cmat = c_ref[...]
    smat = s_ref[...]
    for g in range(side // FS_GROUP):
        ab = ab_ref[:, 2 * B_W * FS_GROUP * g:2 * B_W * FS_GROUP * (g + 1)].astype(BF16)
        m1 = _dot(cmat, ab)
        m2 = _dot(smat, ab)
        for t in range(FS_GROUP):
            n2 = FS_GROUP * g + t
            a0 = 2 * B_W * t
            yr = m1[:, a0:a0 + B_W] - m2[:, a0 + B_W:a0 + 2 * B_W]
            yi = -(m1[:, a0 + B_W:a0 + 2 * B_W] + m2[:, a0:a0 + B_W])
            ct = tc_ref[:, n2:n2 + 1]
            st = ts_ref[:, n2:n2 + 1]
            y = jnp.concatenate([yr * ct + yi * st, yi * ct - yr * st], axis=1)
            for c in range(2 * B_W // LANE):
                for hi in range(side // 8):
                    r0 = (hi * side + n2) * 8
                    y_ref[c, r0:r0 + 8, :] = y[8 * hi:8 * (hi + 1), LANE * c:LANE * (c + 1)]
    for g in range(side // FS_GROUP):
        zs = []
        for j in range(FS_GROUP):
            k1 = FS_GROUP * g + j
            rows = pl.ds((k1 // 8) * side * 8 + k1 % 8, side, stride=8)
            n_lt = B_W // LANE
            y_re = jnp.concatenate([y_ref[c, rows, :] for c in range(n_lt)], axis=1).astype(BF16)
            y_im = jnp.concatenate([y_ref[n_lt + c, rows, :] for c in range(n_lt)], axis=1).astype(BF16)
            zs.append(_dot(cmat, y_re) + _dot(smat, y_im))
        z = (jnp.concatenate(zs, axis=0) * scale).astype(BF16)
        o = _dot(z, wf_ref[0])
        for j in range(FS_GROUP):
            for c in range(B_W // LANE):
                o_ref[c, pl.ds(FS_GROUP * g + j, side, stride=side), :] = o[side * j:side * (j + 1),
                                                                            LANE * c:LANE * (c + 1)]


def _fourier_grid(ab_grid, cmat, smat, tw_cos, tw_sin, wf_blk, layer, n_seq, side):
    seq_len = side * side
    scale = float((seq_len * HEAD_DIM) ** -0.5)
    small = pl.BlockSpec((side, side), lambda s: (0, 0))
    return pl.pallas_call(
        functools.partial(_fourier_grid_kernel, side=side, scale=scale),
        grid=(n_seq,),
        in_specs=[
            pl.BlockSpec((side, side * 2 * B_W), lambda s: (s, 0)),
            small,
            small,
            small,
            small,
            pl.BlockSpec((1, B_W, B_W), lambda s: (layer, 0, 0)),
        ],
        out_specs=pl.BlockSpec((B_W // LANE, seq_len, LANE), lambda s: (0, s, 0)),
        out_shape=jax.ShapeDtypeStruct((B_W // LANE, n_seq * seq_len, LANE), F32),
        scratch_shapes=[pltpu.VMEM((2 * B_W // LANE, seq_len, LANE), F32)],
        compiler_params=_params(("arbitrary",), vmem_mib=56),
        name="fourier_grid",
    )(ab_grid, cmat, smat, tw_cos, tw_sin, wf_blk)


SEQS_PER_STEP = 2


def _mlstm_kernel(*refs, n_chunks):
    sps = SEQS_PER_STEP
    n_side = 3 + 2 * sps
    fwd, bwd = refs[:n_side], refs[n_side:2 * n_side]
    c0_ref, n0_ref, m0_ref, tril_ref, triu_ref = refs[2 * n_side:2 * n_side + 5]
    hf_ref, hb_ref, cout_ref, nout_ref, mout_ref, c_s, n_s, m_s = refs[2 * n_side + 5:]
    c = pl.program_id(1)
    hi = lax.Precision.HIGHEST
    lc = MLSTM_CHUNK
    pair_w = 2 * HEAD_DIM
    n_pairs = H_MLSTM // 2

    @pl.when(c == 0)
    def _():
        c_s[...] = c0_ref[...]
        n_s[...] = n0_ref[...]
        m_s[...] = m0_ref[...]

    lo_lane = lax.broadcasted_iota(jnp.int32, (1, pair_w), 1) < HEAD_DIM
    lo_row = lax.broadcasted_iota(jnp.int32, (pair_w, 1), 0) < HEAD_DIM
    row8 = lax.broadcasted_iota(jnp.int32, (8, 1), 0)
    cum_mask = [tril_ref[...], triu_ref[...]]
    keep_t = [triu_ref[...] > 0.5, tril_ref[...] > 0.5]
    pairs = [(j, d, hp) for j in range(sps) for d in range(2) for hp in range(n_pairs)]
    heads = [(pi, hh) for pi in range(len(pairs)) for hh in range(2)]
    rng = range(len(heads))

    pre = {}
    for j in range(sps):
        for d, side in enumerate((fwd, bwd)):
            q_ref, k_ref, g_ref = side[:3]
            gt_ref, vt_ref = side[3 + 2 * j], side[4 + 2 * j]
            go = 2 * H_MLSTM * d
            lf_cols = _log_sigmoid(g_ref[j, :, go + H_MLSTM:go + 2 * H_MLSTM])
            lf_rows = _log_sigmoid(gt_ref[go + H_MLSTM:go + 2 * H_MLSTM, :])
            b_cols = _dot(cum_mask[d], lf_cols, precision=hi)
            pre[j, d] = dict(
                a_cols=g_ref[j, :, go:go + H_MLSTM] - b_cols,
                ig_rows=gt_ref[go:go + H_MLSTM, :],
                b_rows=_dot_nt(lf_rows, cum_mask[d], precision=hi),
                q=q_ref[j].astype(BF16), k=(k_ref[j] * (HEAD_DIM ** -0.5)).astype(BF16), vt=vt_ref[...])

    def pair_cols(hp):
        return slice(pair_w * hp, pair_w * (hp + 1))

    q_p = [pre[j, d]["q"][:, pair_cols(hp)] for j, d, hp in pairs]
    k_p = [pre[j, d]["k"][:, pair_cols(hp)] for j, d, hp in pairs]
    vt_p = [pre[j, d]["vt"][pair_cols(hp), :] for j, d, hp in pairs]
    c_p = [c_s[j, d, hp] for j, d, hp in pairs]
    n_p = [n_s[j, n_pairs * d + hp:n_pairs * d + hp + 1, :] for j, d, hp in pairs]
    zero_k = jnp.zeros((lc, pair_w), BF16)
    k_h = [jnp.where(lo_lane, k_p[pi], zero_k) if hh == 0 else jnp.where(lo_lane, zero_k, k_p[pi])
           for pi, hh in heads]

    def head_of(i):
        pi, hh = heads[i]
        j, d, hp = pairs[pi]
        return j, d, 2 * hp + hh

    b_row = [pre[head_of(i)[0], head_of(i)[1]]["b_rows"][head_of(i)[2]:head_of(i)[2] + 1, :] for i in rng]
    ig_row = [pre[head_of(i)[0], head_of(i)[1]]["ig_rows"][head_of(i)[2]:head_of(i)[2] + 1, :] for i in rng]
    a_col = [pre[head_of(i)[0], head_of(i)[1]]["a_cols"][:, head_of(i)[2]:head_of(i)[2] + 1] for i in rng]
    m_st = [m_s[head_of(i)[0], H_MLSTM * head_of(i)[1] + head_of(i)[2]:H_MLSTM * head_of(i)[1] + head_of(i)[2] + 1, :]
            for i in rng]
    bl = [b_row[i][:, lc - 1:lc] if head_of(i)[1] == 0 else b_row[i][:, 0:1] for i in rng]

    d_t = [jnp.where(keep_t[head_of(i)[1]], b_row[i] + a_col[i], NEG) for i in rng]
    inter = [b_row[i] + m_st[i] for i in rng]
    m_t = [jnp.maximum(inter[i], jnp.max(d_t[i], axis=0, keepdims=True)) for i in rng]
    s_t = [_dot_nt(k_h[i], q_p[heads[i][0]]) * jnp.exp(d_t[i] - m_t[i]) for i in rng]
    w_in = [jnp.exp(inter[i] - m_t[i]) for i in rng]
    num_t = [_dot(vt_p[heads[i][0]].astype(BF16), s_t[i].astype(BF16)) for i in rng]
    qc_t = [_dot_nt(c_p[pi].astype(BF16), q_p[pi]) for pi in range(len(pairs))]
    n_mat = [jnp.where((row8 == 0) & lo_lane, n_p[pi], jnp.where((row8 == 1) & ~lo_lane, n_p[pi], 0.0))
             for pi in range(len(pairs))]
    nq = [_dot_nt(n_mat[pi].astype(BF16), q_p[pi]) for pi in range(len(pairs))]
    den = [jnp.sum(s_t[i], axis=0, keepdims=True) + w_in[i] * nq[heads[i][0]][heads[i][1]:heads[i][1] + 1, :]
           for i in rng]
    inv = [1.0 / jnp.maximum(jnp.abs(den[i]), jnp.exp(-m_t[i])) for i in rng]
    h_t = []
    for pi in range(len(pairs)):
        i0, i1 = 2 * pi, 2 * pi + 1
        num = jnp.where(lo_row, num_t[i0], num_t[i1])
        h_t.append((num + jnp.where(lo_row, w_in[i0], w_in[i1]) * qc_t[pi]) * jnp.where(lo_row, inv[i0], inv[i1]))
    for j in range(sps):
        base = 2 * n_pairs * j
        hf_ref[j] = jnp.concatenate(h_t[base:base + n_pairs], axis=0)
        hb_ref[j] = jnp.concatenate(h_t[base + n_pairs:base + 2 * n_pairs], axis=0)

    g_row = [bl[i] - b_row[i] + ig_row[i] for i in rng]
    m_new = [jnp.maximum(bl[i] + m_st[i], jnp.max(g_row[i], axis=-1, keepdims=True)) for i in rng]
    wc = [jnp.exp(bl[i] + m_st[i] - m_new[i]) for i in rng]
    ws_row = [jnp.exp(g_row[i] - m_new[i]) for i in rng]
    upd = [_dot((vt_p[heads[i][0]] * ws_row[i]).astype(BF16), k_p[heads[i][0]]) for i in rng]
    for pi, (j, d, hp) in enumerate(pairs):
        i0, i1 = 2 * pi, 2 * pi + 1
        block = jnp.where(lo_row & lo_lane, upd[i0], jnp.where(~lo_row & ~lo_lane, upd[i1], 0.0))
        c_s[j, d, hp] = jnp.where(lo_row, wc[i0], wc[i1]) * c_p[pi] + block
        ws_mat = jnp.where(row8 == 0, ws_row[i0], jnp.where(row8 == 1, ws_row[i1], 0.0))
        k_sum = _dot(ws_mat.astype(BF16), k_p[pi])
        row = n_pairs * d + hp
        n_s[j, row:row + 1, :] = (jnp.where(lo_lane, wc[i0], wc[i1]) * n_p[pi]
                                  + jnp.where(lo_lane, k_sum[0:1, :], k_sum[1:2, :]))
    for i in rng:
        j, d, hd = head_of(i)
        m_s[j, H_MLSTM * d + hd:H_MLSTM * d + hd + 1, :] = m_new[i]

    @pl.when(c == n_chunks - 1)
    def _():
        cout_ref[...] = c_s[...]
        nout_ref[...] = n_s[...]
        mout_ref[...] = m_s[...]


def _pair_states(c):
    b = c.shape[0]
    c = c.reshape(b, 2, H_MLSTM // 2, 2, HEAD_DIM, HEAD_DIM)
    zero = jnp.zeros_like(c[:, :, :, 0])
    top = jnp.concatenate([c[:, :, :, 0], zero], axis=-1)
    bottom = jnp.concatenate([zero, c[:, :, :, 1]], axis=-1)
    return jnp.concatenate([top, bottom], axis=-2)


def _unpair_states(cp):
    b = cp.shape[0]
    first = cp[:, :, :, :HEAD_DIM, :HEAD_DIM]
    second = cp[:, :, :, HEAD_DIM:, HEAD_DIM:]
    return jnp.stack([first, second], axis=3).reshape(b, 2, H_MLSTM, HEAD_DIM, HEAD_DIM)


def _mlstm(p, g, gt, vo, c0, n0, m0, tril, triu, n_seq, seq_len):
    lc = MLSTM_CHUNK
    nc = seq_len // lc
    n_st = 2 * H_MLSTM
    n_pairs = H_MLSTM // 2
    pair_w = 2 * HEAD_DIM
    sps = SEQS_PER_STEP
    p3 = p.reshape(n_seq, seq_len, P_KEEP)
    g3 = g.reshape(n_seq, seq_len, N_GATE_COLS)

    def fwd(c):
        return c

    def bwd(c):
        return nc - 1 - c

    def side(chunk):
        tok = lambda col: (lambda b, c: (b, chunk(c), col))
        specs = [
            pl.BlockSpec((sps, lc, C_W), tok(QC_BLK)),
            pl.BlockSpec((sps, lc, C_W), tok(KC_BLK)),
            pl.BlockSpec((sps, lc, N_GATE_COLS), tok(0)),
        ]
        for j in range(sps):
            lanes = lambda b, c, j=j: (0, (b * sps + j) * nc + chunk(c))
            specs += [pl.BlockSpec((N_GATE_COLS, lc), lanes), pl.BlockSpec((C_W, lc), lanes)]
        return specs

    state_specs = [
        pl.BlockSpec((sps, 2, n_pairs, pair_w, pair_w), lambda b, c: (b, 0, 0, 0, 0)),
        pl.BlockSpec((sps, 2 * n_pairs, pair_w), lambda b, c: (b, 0, 0)),
        pl.BlockSpec((sps, n_st, 1), lambda b, c: (b, 0, 0)),
    ]
    tri_spec = pl.BlockSpec((lc, lc), lambda b, c: (0, 0))
    operands = [p3, p3, g3] + [gt, vo] * sps
    hf, hb, c_out, n_out, m_out = pl.pallas_call(
        functools.partial(_mlstm_kernel, n_chunks=nc),
        grid=(n_seq // sps, nc),
        in_specs=side(fwd) + side(bwd) + state_specs + [tri_spec, tri_spec],
        out_specs=[
            pl.BlockSpec((sps, C_W, lc), lambda b, c: (b, 0, c)),
            pl.BlockSpec((sps, C_W, lc), lambda b, c: (b, 0, nc - 1 - c)),
        ] + state_specs,
        out_shape=[
            jax.ShapeDtypeStruct((n_seq, C_W, seq_len), F32),
            jax.ShapeDtypeStruct((n_seq, C_W, seq_len), F32),
            jax.ShapeDtypeStruct((n_seq, 2, n_pairs, pair_w, pair_w), F32),
            jax.ShapeDtypeStruct((n_seq, 2 * n_pairs, pair_w), F32),
            jax.ShapeDtypeStruct((n_seq, n_st, 1), F32),
        ],
        scratch_shapes=[
            pltpu.VMEM((sps, 2, n_pairs, pair_w, pair_w), F32),
            pltpu.VMEM((sps, 2 * n_pairs, pair_w), F32),
            pltpu.VMEM((sps, n_st, 1), F32),
        ],
        compiler_params=_params(("arbitrary", "arbitrary")),
        name="mlstm",
    )(*operands, *operands, _pair_states(c0), n0.reshape(n_seq, 2 * n_pairs, pair_w), m0.reshape(n_seq, n_st, 1),
      tril, triu)
    return (hf, hb, _unpair_states(c_out), n_out.reshape(n_seq, 2, H_MLSTM, HEAD_DIM),
            m_out.reshape(n_seq, 2, H_MLSTM))


def _head_norm(y, g, ones_blk):
    ysq = y * y
    hi = ysq.astype(BF16)
    lo = (ysq - hi.astype(F32)).astype(BF16)
    ss = _dot(hi, ones_blk) + _dot(lo, ones_blk)
    return y * lax.rsqrt(ss * (1.0 / HEAD_DIM) + EPS) * g


def _merge_kernel(att_ref, four_ref, hf_ref, hb_ref, oc_ref, x_ref, mod_ref, gh_ref, ghm_ref, wo_ref, g2_ref, wrt_ref,
                  ones_ref, xo_ref, h2_ref, afft_ref, *, row_tiles):
    gh = gh_ref[0]
    half = x_ref.shape[0] // 2
    parts = [slice(half * p, half * (p + 1)) for p in range(2)]
    ya = [_head_norm(att_ref[r, :], gh[:, :A_W], ones_ref[...]) for r in parts]
    four = [jnp.concatenate([four_ref[c, r, :] for c in range(B_W // LANE)], axis=1) for r in parts]
    yf = [_head_norm(f, gh[:, A_W:A_W + B_W], ones_ref[:B_W, :B_W]) for f in four]
    mem = [hf_ref[0, :, r] + hb_ref[0, :, r] for r in parts]
    ym_t = []
    for p, r in enumerate(parts):
        heads = []
        for hd in range(H_MLSTM):
            y = mem[p][HEAD_DIM * hd:HEAD_DIM * (hd + 1), :]
            heads.append(y * lax.rsqrt(jnp.mean(y * y, axis=0, keepdims=True) + EPS))
        ym_t.append(jnp.concatenate(heads, axis=0) * ghm_ref[0, :, r] * jax.nn.sigmoid(oc_ref[:, r]))
    out = [_dot(ya[p].astype(BF16), wo_ref[0, :A_W, :])
           + _dot(yf[p].astype(BF16), wo_ref[0, A_W:A_W + B_W, :])
           + _dot_tn(ym_t[p].astype(BF16), wo_ref[0, A_W + B_W:, :]) for p in range(2)]
    x = [x_ref[r, :] + mod_ref[0, 2:3, :] * out[p] for p, r in enumerate(parts)]
    y2 = [xp * lax.rsqrt(jnp.mean(xp * xp, axis=-1, keepdims=True) + EPS) * g2_ref[0] for xp in x]
    h2 = [(yp * (1.0 + mod_ref[0, 4:5, :]) + mod_ref[0, 3:4, :]).astype(BF16) for yp in y2]
    logits = [_dot_nt(wrt_ref[0], hp) for hp in h2]
    e = [jnp.exp(lg - jnp.max(lg, axis=0, keepdims=True)) for lg in logits]
    for p, r in enumerate(parts):
        xo_ref[r, :] = x[p]
        if row_tiles:
            h2_wide = h2[p].astype(F32)
            for s in range(N_SLAB):
                h2_ref[pl.ds(N_SLAB * half * p + s, half, stride=N_SLAB), :] = h2_wide[:, LANE * s:LANE * (s + 1)]
        else:
            h2_ref[r, :] = h2[p]
        afft_ref[:, r] = e[p] / jnp.sum(e[p], axis=0, keepdims=True)


def _merge(att, four, hf, hb, vo, x2d, mod, layer, g_head, g_mem, w_out_bf, g2, w_rt_bf, ones_blk, n_seq, seq_len,
           mod_seq_len, row_tiles):
    t = x2d.shape[0]
    tm = MERGE_TILE
    tiles_per_mod = mod_seq_len // tm
    tiles_per_seq = seq_len // tm
    row = lambda i: (i, 0)
    lay = lambda i: (layer, 0, 0)
    mem = lambda i: (i // tiles_per_seq, 0, i % tiles_per_seq)
    if row_tiles:
        h2_spec, h2_shape = pl.BlockSpec((N_SLAB * tm, LANE), row), jax.ShapeDtypeStruct((N_SLAB * t, LANE), F32)
    else:
        h2_spec, h2_shape = pl.BlockSpec((tm, D_MODEL), row), jax.ShapeDtypeStruct((t, D_MODEL), BF16)
    return pl.pallas_call(
        functools.partial(_merge_kernel, row_tiles=row_tiles),
        grid=(t // tm,),
        in_specs=[
            pl.BlockSpec((tm, A_W), row),
            pl.BlockSpec((B_W // LANE, tm, LANE), lambda i: (0, i, 0)),
            pl.BlockSpec((1, C_W, tm), mem),
            pl.BlockSpec((1, C_W, tm), mem),
            pl.BlockSpec((C_W, tm), lambda i: (1, i)),
            pl.BlockSpec((tm, D_MODEL), row),
            pl.BlockSpec((1, 6, D_MODEL), lambda i: (i // tiles_per_mod, 0, 0)),
            pl.BlockSpec((1, 1, D_MODEL), lay),
            pl.BlockSpec((1, C_W, tm), lay),
            pl.BlockSpec((1, D_MODEL, D_MODEL), lay),
            pl.BlockSpec((1, 1, D_MODEL), lay),
            pl.BlockSpec((1, N_EXPERTS, D_MODEL), lay),
            pl.BlockSpec((A_W, A_W), lambda i: (0, 0)),
        ],
        out_specs=[
            pl.BlockSpec((tm, D_MODEL), row),
            h2_spec,
            pl.BlockSpec((N_EXPERTS, tm), lambda i: (0, i)),
        ],
        out_shape=[
            jax.ShapeDtypeStruct((t, D_MODEL), F32),
            h2_shape,
            jax.ShapeDtypeStruct((N_EXPERTS, t), F32),
        ],
        compiler_params=_params(("arbitrary",)),
        name="merge",
    )(att, four, hf, hb, vo, x2d, mod, g_head, g_mem, w_out_bf, g2, w_rt_bf, ones_blk)


BISECT_STEPS = 48
TOKEN_CHUNK = 1024


TOKEN_SPLIT = 64


def _route_kernel(aff_ref, triu_ref, idx_ref, gs_ref, sp_ref, *blocked, ns, seq_len, cap):
    seqs = range(ns)
    aff = [aff_ref[:, seq_len * j:seq_len * (j + 1)] for j in seqs]

    def body(_, bounds):
        out = []
        for j in seqs:
            lo, hi = bounds[j]
            mid = 0.5 * (lo + hi)
            ge = jnp.sum(jnp.where(aff[j] >= mid, 1.0, 0.0), axis=1, keepdims=True) >= cap
            out.append((jnp.where(ge, mid, lo), jnp.where(ge, hi, mid)))
        return tuple(out)

    start = (jnp.zeros((N_EXPERTS, 1), F32), jnp.full((N_EXPERTS, 1), 2.0, F32))
    bounds = lax.fori_loop(0, BISECT_STEPS, body, tuple(start for _ in seqs))
    thr = [jnp.max(jnp.where(aff[j] < bounds[j][1], aff[j], -1.0), axis=1, keepdims=True) for j in seqs]
    need = [cap - jnp.sum(jnp.where(aff[j] > thr[j], 1.0, 0.0), axis=1, keepdims=True) for j in seqs]
    triu = triu_ref[...]
    eq_carry = [jnp.zeros((N_EXPERTS, 1), F32) for _ in seqs]
    pos_carry = [jnp.zeros((N_EXPERTS, 1), F32) for _ in seqs]
    n_blk = seq_len // 128
    lane = lax.broadcasted_iota(jnp.int32, (1, 128), 1)
    blocked_refs = blocked if blocked else None
    first = [jnp.zeros((N_EXPERTS, 128), F32) for _ in seqs]
    after = [jnp.full((N_EXPERTS, 128), 1e9, F32) for _ in seqs]
    for b in range(n_blk):
        for j in seqs:
            blk = aff[j][:, 128 * b:128 * (b + 1)]
            eq = blk == thr[j]
            eq_f = jnp.where(eq, 1.0, 0.0)
            eq_inc = _dot(eq_f.astype(BF16), triu) + eq_carry[j]
            sel = (blk > thr[j]) | (eq & (eq_inc - eq_f < need[j]))
            sel_f = jnp.where(sel, 1.0, 0.0)
            pos_inc = _dot(sel_f.astype(BF16), triu) + pos_carry[j]
            t0 = seq_len * j + 128 * b
            sp_ref[:, t0:t0 + 128] = jnp.where(sel, pos_inc - sel_f, -1.0).astype(jnp.int32)
            if blocked_refs:
                cum_s, aff_s, _ = blocked_refs
                r0 = (j * n_blk + b) * N_EXPERTS
                cum_s[r0:r0 + N_EXPERTS, :] = pos_inc - pos_carry[j]
                aff_s[r0:r0 + N_EXPERTS, :] = blk
                first[j] = jnp.where(lane == b, pos_carry[j], first[j])
                after[j] = jnp.where(lane == b, pos_inc[:, 127:128], after[j])
            eq_carry[j] = eq_inc[:, 127:128]
            pos_carry[j] = pos_inc[:, 127:128]

    if blocked_refs:
        cum_s, aff_s, bnd_s = blocked_refs
        for j in seqs:
            bnd_s[N_EXPERTS * j:N_EXPERTS * (j + 1), :] = first[j]
            bnd_s[N_EXPERTS * (ns + j):N_EXPERTS * (ns + j + 1), :] = after[j]
        slot_col = lax.broadcasted_iota(jnp.int32, (cap, 1), 0).astype(F32)
        lane_f = lane.astype(F32)

        def per_expert(e, carry):
            for j in seqs:
                first_row = bnd_s[pl.ds(N_EXPERTS * j + e, 1), :]
                after_row = bnd_s[pl.ds(N_EXPERTS * (ns + j) + e, 1), :]
                blk_of = jnp.sum(jnp.where(after_row <= slot_col, 1.0, 0.0), axis=1, keepdims=True)
                in_blk = lane_f == blk_of
                local = slot_col - jnp.sum(jnp.where(in_blk, first_row, 0.0), axis=1, keepdims=True)
                pick = jnp.where(in_blk, 1.0, 0.0)[:, :n_blk].astype(BF16)
                rows = pl.ds(N_EXPERTS * n_blk * j + e, n_blk, stride=N_EXPERTS)
                counts = _dot(pick, cum_s[rows, :].astype(BF16))
                tok_in = jnp.sum(jnp.where(counts <= local, 1.0, 0.0), axis=1, keepdims=True)
                a = aff_s[rows, :]
                a_hi = a.astype(BF16)
                a_mid = (a - a_hi.astype(F32)).astype(BF16)
                a_lo = (a - a_hi.astype(F32) - a_mid.astype(F32)).astype(BF16)
                gates = _dot(pick, a_hi) + _dot(pick, a_mid) + _dot(pick, a_lo)
                idx_ref[N_EXPERTS * j + e] = (N_SLAB * (128.0 * blk_of + tok_in)).astype(jnp.int32)
                gs_ref[e, cap * j:cap * (j + 1)] = jnp.sum(jnp.where(lane_f == tok_in, gates, 0.0), axis=1,
                                                           keepdims=True)
            return carry

        lax.fori_loop(0, N_EXPERTS, per_expert, 0)
        return

    tc = min(seq_len, TOKEN_CHUNK)
    slot = lax.broadcasted_iota(jnp.int32, (cap, tc), 0)
    part = lax.broadcasted_iota(jnp.int32, (8, tc), 0)
    tok = lax.broadcasted_iota(jnp.int32, (1, tc), 1).astype(F32)
    chunks = range(0, seq_len, tc)
    tok_hi = [jnp.floor((tok + float(t0)) * (1.0 / TOKEN_SPLIT)) for t0 in chunks]
    tok_lo = [tok + float(t0) - TOKEN_SPLIT * hi for t0, hi in zip(chunks, tok_hi)]

    def per_expert(e, carry):
        for j in seqs:
            acc = jnp.zeros((cap, 8), F32)
            for ci, t0 in enumerate(chunks):
                cols = slice(seq_len * j + t0, seq_len * j + t0 + tc)
                onehot = jnp.where(slot == sp_ref[pl.ds(e, 1), cols], 1.0, 0.0).astype(BF16)
                a = aff_ref[pl.ds(e, 1), cols]
                a_hi = a.astype(BF16).astype(F32)
                a_mid = (a - a_hi).astype(BF16).astype(F32)
                a_lo = a - a_hi - a_mid
                vals = jnp.where(part == 0, tok_hi[ci], jnp.where(part == 1, tok_lo[ci], jnp.where(
                    part == 2, a_hi, jnp.where(part == 3, a_mid, jnp.where(part == 4, a_lo, 0.0)))))
                acc = acc + _dot_nt(onehot, vals.astype(BF16))
            idx_ref[N_EXPERTS * j + e] = (N_SLAB * (TOKEN_SPLIT * acc[:, 0:1] + acc[:, 1:2])).astype(jnp.int32)
            gs_ref[e, cap * j:cap * (j + 1)] = acc[:, 2:3] + acc[:, 3:4] + acc[:, 4:5]
        return carry

    lax.fori_loop(0, N_EXPERTS, per_expert, 0)


def _route(afft, triu_bf, n_seq, seq_len, cap, ns):
    n_blk = seq_len // 128
    idx, gs, sp = pl.pallas_call(
        functools.partial(_route_kernel, ns=ns, seq_len=seq_len, cap=cap),
        grid=(n_seq // ns,),
        in_specs=[
            pl.BlockSpec((N_EXPERTS, ns * seq_len), lambda s: (0, s)),
            pl.BlockSpec((128, 128), lambda s: (0, 0)),
        ],
        out_specs=[
            pl.BlockSpec((ns * N_EXPERTS, cap, 1), lambda s: (s, 0, 0)),
            pl.BlockSpec((N_EXPERTS, ns * cap, 1), lambda s: (0, s, 0)),
            pl.BlockSpec((N_EXPERTS, ns * seq_len), lambda s: (0, s)),
        ],
        out_shape=[
            jax.ShapeDtypeStruct((n_seq * N_EXPERTS, cap, 1), jnp.int32),
            jax.ShapeDtypeStruct((N_EXPERTS, n_seq * cap, 1), F32),
            jax.ShapeDtypeStruct((N_EXPERTS, n_seq * seq_len), jnp.int32),
        ],
        scratch_shapes=[
            pltpu.VMEM((ns * n_blk * N_EXPERTS, 128), F32),
            pltpu.VMEM((ns * n_blk * N_EXPERTS, 128), F32),
            pltpu.VMEM((2 * ns * N_EXPERTS, 128), F32),
        ] if n_blk >= 16 else [],
        compiler_params=_params(("arbitrary",)),
        name="route",
    )(afft, triu_bf)
    return idx.reshape(n_seq * N_EXPERTS * cap), gs, sp


ROW_COPIES = 8


def _row_tile(first_row):
    return pl.ds(pl.multiple_of(first_row, N_SLAB), N_SLAB)


def _gather_kernel(idx_ref, src_ref, xs_ref, tile_ref, *, eb, cap):
    ei = pl.program_id(1)

    def per_expert(ee, carry):
        e = ei * eb + ee

        def rows(g, c):
            slot0 = g * ROW_COPIES
            for u in range(ROW_COPIES):
                tile_ref[_row_tile((slot0 + u) * N_SLAB), :] = src_ref[_row_tile(idx_ref[e * cap + slot0 + u]), :]
            return c

        lax.fori_loop(0, cap // ROW_COPIES, rows, 0)
        for s in range(N_SLAB):
            xs_ref[ee, :, LANE * s:LANE * (s + 1)] = tile_ref[pl.ds(s, cap, stride=N_SLAB), :].astype(BF16)
        return carry

    lax.fori_loop(0, eb, per_expert, 0)


def _gather(idx, h2_rows, n_seq, cap, eb):
    return pl.pallas_call(
        functools.partial(_gather_kernel, eb=eb, cap=cap),
        grid=(n_seq, N_EXPERTS // eb),
        in_specs=[
            pl.BlockSpec((N_EXPERTS * cap,), lambda s, e: (s,), memory_space=pltpu.SMEM),
            pl.BlockSpec((h2_rows.shape[0] // n_seq, LANE), lambda s, e: (s, 0)),
        ],
        out_specs=pl.BlockSpec((eb, cap, D_MODEL), lambda s, e: (e, s, 0)),
        out_shape=jax.ShapeDtypeStruct((N_EXPERTS, n_seq * cap, D_MODEL), BF16),
        scratch_shapes=[pltpu.VMEM((N_SLAB * cap, LANE), F32)],
        compiler_params=_params(("arbitrary", "arbitrary"), vmem_mib=56),
        name="gather",
    )(idx, h2_rows)


def _selection(sp_ref, cap):
    seq_len = sp_ref.shape[1]
    slot = lax.broadcasted_iota(jnp.int32, (cap, seq_len), 0)
    return jnp.concatenate([jnp.where(slot == sp_ref[e:e + 1, :], 1.0, 0.0) for e in range(N_EXPERTS)],
                           axis=0).astype(BF16)


def _gather_short_kernel(sp_ref, h_ref, xs_ref, *, cap):
    rows = _dot(_selection(sp_ref, cap), h_ref[...])
    for e in range(N_EXPERTS):
        xs_ref[e] = rows[cap * e:cap * (e + 1), :].astype(BF16)


def _gather_short(sp, h2, n_seq, seq_len, cap):
    return pl.pallas_call(
        functools.partial(_gather_short_kernel, cap=cap),
        grid=(n_seq,),
        in_specs=[
            pl.BlockSpec((N_EXPERTS, seq_len), lambda s: (0, s)),
            pl.BlockSpec((seq_len, D_MODEL), lambda s: (s, 0)),
        ],
        out_specs=pl.BlockSpec((N_EXPERTS, cap, D_MODEL), lambda s: (0, s, 0)),
        out_shape=jax.ShapeDtypeStruct((N_EXPERTS, n_seq * cap, D_MODEL), BF16),
        compiler_params=_params(("arbitrary",)),
        name="gather_short",
    )(sp, h2)


def _scatter_short_kernel(sp_ref, y_ref, x_ref, mod_ref, gf_ref, o_ref, *, cap, final):
    sel = _selection(sp_ref, cap)
    y = jnp.concatenate([y_ref[e] for e in range(N_EXPERTS)], axis=0)
    y_hi = y.astype(BF16)
    y_lo = (y - y_hi.astype(F32)).astype(BF16)
    moe = _dot_tn(sel, y_hi) + _dot_tn(sel, y_lo)
    x = x_ref[...] + mod_ref[0, 5:6, :] * moe
    if final:
        x = x * lax.rsqrt(jnp.mean(x * x, axis=-1, keepdims=True) + EPS) * gf_ref[...]
    o_ref[...] = x


def _scatter_short(sp, ys, x2d, mod, g_final, n_seq, seq_len, cap, final):
    return pl.pallas_call(
        functools.partial(_scatter_short_kernel, cap=cap, final=final),
        grid=(n_seq,),
        in_specs=[
            pl.BlockSpec((N_EXPERTS, seq_len), lambda s: (0, s)),
            pl.BlockSpec((N_EXPERTS, cap, D_MODEL), lambda s: (0, s, 0)),
            pl.BlockSpec((seq_len, D_MODEL), lambda s: (s, 0)),
            pl.BlockSpec((1, 6, D_MODEL), lambda s: (0, 0, 0)),
            pl.BlockSpec((1, D_MODEL), lambda s: (0, 0)),
        ],
        out_specs=pl.BlockSpec((seq_len, D_MODEL), lambda s: (s, 0)),
        out_shape=jax.ShapeDtypeStruct((n_seq * seq_len, D_MODEL), F32),
        compiler_params=_params(("arbitrary",)),
        name="scatter_short",
    )(sp, ys, x2d, mod, g_final)


def _expert_kernel(xc_ref, xl_ref, gc_ref, gl_ref, wg_ref, wu_ref, wd_ref, yc_ref, yl_ref, *, n_f):
    f = pl.program_id(1)

    @pl.when(f == 0)
    def _():
        yc_ref[...] = jnp.zeros_like(yc_ref)
        yl_ref[...] = jnp.zeros_like(yl_ref)

    wg = wg_ref[0, 0].astype(BF16)
    wu = wu_ref[0, 0].astype(BF16)
    wd = wd_ref[0, 0].astype(BF16)
    for x_ref, y_ref in ((xc_ref, yc_ref), (xl_ref, yl_ref)):
        x = x_ref[0]
        mid = (_silu(_dot(x, wg)) * _dot(x, wu)).astype(BF16)
        y_ref[0] += _dot(mid, wd)

    @pl.when(f == n_f - 1)
    def _():
        yc_ref[0] = yc_ref[0] * gc_ref[0]
        yl_ref[0] = yl_ref[0] * gl_ref[0]


def _experts(xs_c, xs_l, gs_c, gs_l, w_g, w_u, w_d, layer):
    rc, rl = xs_c.shape[1], xs_l.shape[1]
    tf = 1024
    n_f = EXPERT_FF // tf
    return pl.pallas_call(
        functools.partial(_expert_kernel, n_f=n_f),
        grid=(N_EXPERTS, n_f),
        in_specs=[
            pl.BlockSpec((1, rc, D_MODEL), lambda e, f: (e, 0, 0)),
            pl.BlockSpec((1, rl, D_MODEL), lambda e, f: (e, 0, 0)),
            pl.BlockSpec((1, rc, 1), lambda e, f: (e, 0, 0)),
            pl.BlockSpec((1, rl, 1), lambda e, f: (e, 0, 0)),
            pl.BlockSpec((1, 1, D_MODEL, tf), lambda e, f: (layer, e, 0, f)),
            pl.BlockSpec((1, 1, D_MODEL, tf), lambda e, f: (layer, e, 0, f)),
            pl.BlockSpec((1, 1, tf, D_MODEL), lambda e, f: (layer, e, f, 0)),
        ],
        out_specs=[
            pl.BlockSpec((1, rc, D_MODEL), lambda e, f: (e, 0, 0)),
            pl.BlockSpec((1, rl, D_MODEL), lambda e, f: (e, 0, 0)),
        ],
        out_shape=[
            jax.ShapeDtypeStruct((N_EXPERTS, rc, D_MODEL), F32),
            jax.ShapeDtypeStruct((N_EXPERTS, rl, D_MODEL), F32),
        ],
        compiler_params=_params(("arbitrary", "arbitrary"), vmem_mib=56),
        name="experts",
    )(xs_c, xs_l, gs_c, gs_l, w_g, w_u, w_d)


def _scatter_kernel(idx_ref, y_ref, x_ref, mod_ref, gf_ref, o_ref, acc_ref, tile_ref, *, eb, n_e, cap, tm, final):
    step = pl.program_id(1)

    @pl.when(step == 0)
    def _():
        acc_ref[...] = jnp.zeros_like(acc_ref)

    @pl.when(step < n_e)
    def _():
        def per_expert(ee, carry):
            e = step * eb + ee
            for s in range(N_SLAB):
                tile_ref[pl.ds(s, cap, stride=N_SLAB), :] = y_ref[ee, :, LANE * s:LANE * (s + 1)]

            def rows(g, c):
                slot0 = g * ROW_COPIES
                dst = [idx_ref[e * cap + slot0 + u] for u in range(ROW_COPIES)]
                new = [acc_ref[_row_tile(dst[u]), :] + tile_ref[_row_tile((slot0 + u) * N_SLAB), :]
                       for u in range(ROW_COPIES)]
                for u in range(ROW_COPIES):
                    acc_ref[_row_tile(dst[u]), :] = new[u]
                return c

            lax.fori_loop(0, cap // ROW_COPIES, rows, 0)
            return carry

        lax.fori_loop(0, eb, per_expert, 0)

    @pl.when(step >= n_e)
    def _():
        base = pl.multiple_of((step - n_e) * tm * N_SLAB, N_SLAB)
        moe = jnp.concatenate([acc_ref[pl.ds(base + s, tm, stride=N_SLAB), :] for s in range(N_SLAB)], axis=1)
        x = x_ref[...] + mod_ref[0, 5:6, :] * moe
        if final:
            x = x * lax.rsqrt(jnp.mean(x * x, axis=-1, keepdims=True) + EPS) * gf_ref[...]
        o_ref[...] = x


def _scatter(idx, ys, x2d, mod, g_final, n_seq, seq_len, cap, eb, tm, final):
    n_e = N_EXPERTS // eb
    n_out = seq_len // tm
    out_blk = lambda s, j: (s * n_out + jnp.maximum(j - n_e, 0), 0)
    return pl.pallas_call(
        functools.partial(_scatter_kernel, eb=eb, n_e=n_e, cap=cap, tm=tm, final=final),
        grid=(n_seq, n_e + n_out),
        in_specs=[
            pl.BlockSpec((N_EXPERTS * cap,), lambda s, j: (s,), memory_space=pltpu.SMEM),
            pl.BlockSpec((eb, cap, D_MODEL), lambda s, j: (jnp.minimum(j, n_e - 1), s, 0)),
            pl.BlockSpec((tm, D_MODEL), out_blk),
            pl.BlockSpec((1, 6, D_MODEL), lambda s, j: (s, 0, 0)),
            pl.BlockSpec((1, D_MODEL), lambda s, j: (0, 0)),
        ],
        out_specs=pl.BlockSpec((tm, D_MODEL), out_blk),
        out_shape=jax.ShapeDtypeStruct((n_seq * seq_len, D_MODEL), F32),
        scratch_shapes=[
            pltpu.VMEM((N_SLAB * seq_len, LANE), F32),
            pltpu.VMEM((N_SLAB * cap, LANE), F32),
        ],
        compiler_params=_params(("arbitrary", "arbitrary"), vmem_mib=56),
        name="scatter",
    )(idx, ys, x2d, mod, g_final)


def kernel(x_prompt, x_sample, c, cache_k, cache_v, state_C, state_n, state_m, c_ctx, w_ada, b_ada, g_norm1, g_norm2, w_in, b_gates, rpb, w_fourier, g_head, w_out, w_router, w_exp_gate, w_exp_up, w_exp_down, g_final):
    n_ctx, len_ctx, _ = x_prompt.shape
    n_lat, len_lat, _ = x_sample.shape
    past = cache_k.shape[2]
    cap_ctx = CAPACITY_FACTOR * len_ctx // N_EXPERTS
    cap_lat = CAPACITY_FACTOR * len_lat // N_EXPERTS

    w_in_bf = w_in.astype(BF16)
    w_gt_bf = jnp.swapaxes(w_in[:, :, P_COLS:], 1, 2).astype(BF16)
    vc0, oc0 = VC_BLK * C_W, OC_BLK * C_W
    w_vo_bf = jnp.swapaxes(jnp.concatenate([w_in[:, :, vc0:vc0 + C_W], w_in[:, :, oc0:oc0 + C_W]], axis=2),
                           1, 2).astype(BF16)
    bg_row = b_gates.reshape(DEPTH, 1, N_GATE_COLS).astype(F32)
    bg_col = b_gates.reshape(DEPTH, N_GATE_COLS, 1).astype(F32)
    w_out_bf = w_out.astype(BF16)
    w_rt_bf = jnp.swapaxes(w_router, 1, 2).astype(BF16)
    g1 = g_norm1.reshape(DEPTH, 1, D_MODEL)
    g2 = g_norm2.reshape(DEPTH, 1, D_MODEL)
    gh = g_head.reshape(DEPTH, 1, D_MODEL)
    g_mem = jnp.broadcast_to(gh[:, 0, A_W + B_W:, None], (DEPTH, C_W, MERGE_TILE))
    eye_g = jnp.eye(G_FOURIER, dtype=F32)
    wf_blk = jnp.einsum("lgcd,gh->lgchd", w_fourier, eye_g).reshape(DEPTH, B_W, B_W).astype(BF16)

    csc = _channel_dft()
    dft_ctx = _dft_mats(len_ctx)
    dft_side = _dft_mats(GRID_W)
    tw_cos, tw_sin = _twiddles(GRID_W, len_lat)
    bias_tiles = _nbr_bias_tiles(rpb)
    r = np.arange(MLSTM_CHUNK)
    tril = jnp.asarray(r[:, None] >= r[None, :], F32)
    triu = jnp.asarray(r[:, None] <= r[None, :], F32)
    triu_bf = triu.astype(BF16)
    hidx = np.arange(A_W) // HEAD_DIM
    ones_blk = jnp.asarray(hidx[:, None] == hidx[None, :], BF16)

    cvecs = jnp.concatenate([c_ctx[None, :], c, jnp.zeros((8 - 1 - n_lat, D_MODEL), F32)], axis=0)
    mod_all = _modulation(cvecs, w_ada, b_ada).reshape(DEPTH, 8, 6, D_MODEL)

    cache_k4 = cache_k.reshape(n_lat, DEPTH, past, A_W)
    cache_v4 = cache_v.reshape(n_lat, DEPTH, past, A_W)
    zero_c = jnp.zeros((n_ctx, 2, H_MLSTM, HEAD_DIM, HEAD_DIM), F32)
    zero_n = jnp.zeros((n_ctx, 2, H_MLSTM, HEAD_DIM), F32)
    zero_m = jnp.zeros((n_ctx, 2, H_MLSTM), F32)

    xc = x_prompt.reshape(n_ctx * len_ctx, D_MODEL)
    xl = x_sample.reshape(n_lat * len_lat, D_MODEL)
    gf = g_final.reshape(1, D_MODEL)
    new_k = jnp.zeros((n_ctx, DEPTH, len_ctx, A_W), F32)
    new_v = jnp.zeros((n_ctx, DEPTH, len_ctx, A_W), F32)
    cs, ns, ms = [], [], []
    for l in range(DEPTH):
        mod_c = mod_all[l, 0:1]
        mod_l = mod_all[l, 1:1 + n_lat]

        pc, gc, gtc, voc, abc, new_k, new_v = _inproj(xc, mod_c, l, g1, w_in_bf, w_gt_bf, w_vo_bf, bg_row, bg_col, csc,
                                                      n_ctx * len_ctx, False, (new_k, new_v))
        att_c = _ctx_attention(pc, n_ctx, len_ctx)
        four_c = _fourier(abc, dft_ctx[0], dft_ctx[1], wf_blk, l, n_ctx, len_ctx)
        hf_c, hb_c, c_new, n_new, m_new = _mlstm(pc, gc, gtc, voc, zero_c, zero_n, zero_m, tril, triu, n_ctx, len_ctx)
        xc, h2c, affc = _merge(att_c, four_c, hf_c, hb_c, voc, xc, mod_c, l, gh, g_mem, w_out_bf, g2, w_rt_bf,
                               ones_blk, n_ctx, len_ctx, n_ctx * len_ctx, False)
        cs.append(c_new)
        ns.append(n_new)
        ms.append(m_new)

        pq, gq, gtq, voq, abq = _inproj(xl, mod_l, l, g1, w_in_bf, w_gt_bf, w_vo_bf, bg_row, bg_col, csc, len_lat,
                                        True)
        att_l = _nbr_attention(pq, cache_k4, cache_v4, bias_tiles, l, n_lat, len_lat)
        four_l = _fourier_grid(abq, dft_side[0], dft_side[1], tw_cos, tw_sin, wf_blk, l, n_lat, GRID_W)
        hf_l, hb_l, _, _, _ = _mlstm(pq, gq, gtq, voq, state_C[:, l], state_n[:, l], state_m[:, l], tril, triu,
                                     n_lat, len_lat)
        xl, h2l, affl = _merge(att_l, four_l, hf_l, hb_l, voq, xl, mod_l, l, gh, g_mem, w_out_bf, g2, w_rt_bf,
                               ones_blk, n_lat, len_lat, len_lat, True)

        last = l == DEPTH - 1
        _, gs_c, sp_c = _route(affc, triu_bf, n_ctx, len_ctx, cap_ctx, n_ctx)
        idx_l, gs_l, _ = _route(affl, triu_bf, n_lat, len_lat, cap_lat, 1)
        xs_c = _gather_short(sp_c, h2c, n_ctx, len_ctx, cap_ctx)
        xs_l = _gather(idx_l, h2l, n_lat, cap_lat, 4)
        ys_c, ys_l = _experts(xs_c, xs_l, gs_c, gs_l, w_exp_gate, w_exp_up, w_exp_down, l)
        xc = _scatter_short(sp_c, ys_c, xc, mod_c, gf, n_ctx, len_ctx, cap_ctx, last)
        xl = _scatter(idx_l, ys_l, xl, mod_l, gf, n_lat, len_lat, cap_lat, 4, 1024, last)

    y_prompt = xc.reshape(n_ctx, len_ctx, D_MODEL)
    y_sample = xl.reshape(n_lat, len_lat, D_MODEL)
    kv_shape = (n_ctx, DEPTH, len_ctx, H_ATT, HEAD_DIM)
    return (y_prompt, y_sample, new_k.reshape(kv_shape), new_v.reshape(kv_shape), jnp.stack(cs, axis=1),
            jnp.stack(ns, axis=1), jnp.stack(ms, axis=1))
```

```python
import functools

import numpy as np
import jax
import jax.numpy as jnp
from jax import lax
from jax.experimental import pallas as pl
from jax.experimental.pallas import tpu as pltpu

F32 = jnp.float32
BF16 = jnp.bfloat16

D_MODEL = 1024
DEPTH = 2
HEAD_DIM = 64
H_ATT = 8
G_FOURIER = 4
H_MLSTM = 4
A_W = H_ATT * HEAD_DIM
B_W = G_FOURIER * HEAD_DIM
C_W = H_MLSTM * HEAD_DIM
N_GATE_COLS = 16
P_COLS = 3 * A_W + B_W + 4 * C_W
IN_COLS = P_COLS + N_GATE_COLS
P_KEEP = 3 * A_W + B_W + 2 * C_W
GRID_W = 64
WIN_R = 8
WIN_C = 16
MLSTM_CHUNK = 128
N_EXPERTS = 16
CAPACITY_FACTOR = 2
EXPERT_FF = 2 * D_MODEL
EPS = 1e-6
NEG = -1e30

UB_OFF = 3 * A_W
QC_BLK, KC_BLK, VC_BLK, OC_BLK = 7, 8, 9, 10

LANE = 128
N_SLAB = D_MODEL // LANE
MERGE_TILE = 256
INPROJ_TILE = 512

NT_DIMS = (((1,), (1,)), ((), ()))
TN_DIMS = (((0,), (0,)), ((), ()))
MIB = 1024 * 1024


def _dot(a, b, precision=None):
    return jnp.dot(a, b, preferred_element_type=F32, precision=precision)


def _dot_nt(a, b, precision=None):
    return lax.dot_general(a, b, NT_DIMS, preferred_element_type=F32, precision=precision)


def _dot_tn(a, b):
    return lax.dot_general(a, b, TN_DIMS, preferred_element_type=F32)


def _params(sem, vmem_mib=48):
    return pltpu.CompilerParams(dimension_semantics=sem, vmem_limit_bytes=vmem_mib * MIB)


def _silu(x):
    return x * jax.nn.sigmoid(x)


def _log_sigmoid(x):
    return jnp.minimum(x, 0.0) - jnp.log1p(jnp.exp(-jnp.abs(x)))


def _mod_kernel(c_ref, w_ref, b_ref, o_ref):
    s = _silu(c_ref[...]).astype(BF16)
    o_ref[0] = _dot(s, w_ref[0].astype(BF16)) + b_ref[0]


def _modulation(cvecs, w_ada, b_ada):
    depth = w_ada.shape[0]
    tn = 1024
    return pl.pallas_call(
        _mod_kernel,
        grid=(depth, 6 * D_MODEL // tn),
        in_specs=[
            pl.BlockSpec((8, D_MODEL), lambda l, j: (0, 0)),
            pl.BlockSpec((1, D_MODEL, tn), lambda l, j: (l, 0, j)),
            pl.BlockSpec((1, 1, tn), lambda l, j: (l, 0, j)),
        ],
        out_specs=pl.BlockSpec((1, 8, tn), lambda l, j: (l, 0, j)),
        out_shape=jax.ShapeDtypeStruct((depth, 8, 6 * D_MODEL), F32),
        compiler_params=_params(("arbitrary", "arbitrary")),
        name="modulation",
    )(cvecs, w_ada, b_ada.reshape(depth, 1, 6 * D_MODEL))


def _inproj_kernel(x_ref, mod_ref, g1_ref, w_ref, wgt_ref, wvo_ref, bgr_ref, bgc_ref, csc_ref,
                   *rest, grid_rows, kv_seq_len):
    n_in = 2 if kv_seq_len else 0
    p_ref, g_ref, gt_ref, vo_ref, ab_ref = rest[n_in:n_in + 5]
    kv_refs = rest[n_in + 5:n_in + 5 + n_in]
    scratch = rest[n_in + 5 + n_in:]
    x = x_ref[...]
    y = x * lax.rsqrt(jnp.mean(x * x, axis=-1, keepdims=True) + EPS) * g1_ref[0]
    h = (y * (1.0 + mod_ref[0, 1:2, :]) + mod_ref[0, 0:1, :]).astype(BF16)
    for j in range(0, P_KEEP, 256):
        pj = _dot(h, w_ref[0, :, j:j + 256])
        p_ref[:, j:j + 256] = pj.astype(BF16)
        if kv_seq_len and A_W <= j < 3 * A_W:
            kv_ref = kv_refs[(j - A_W) // A_W]
            c0 = (j - A_W) % A_W
            for b in range(pj.shape[0] // kv_seq_len):
                kv_ref[b, 0, :, c0:c0 + 256] = pj[kv_seq_len * b:kv_seq_len * (b + 1), :]
        if j == UB_OFF:
            ab = _dot(pj.astype(BF16), csc_ref[...])
            if grid_rows:
                stage_ref, = scratch
                n_lt = 2 * B_W // LANE
                for c in range(n_lt):
                    stage_ref[c] = ab[:, LANE * c:LANE * (c + 1)]
                for n2 in range(GRID_W):
                    for c in range(n_lt):
                        col = 2 * B_W * n2 + LANE * c
                        ab_ref[:, col:col + LANE] = stage_ref[c, pl.ds(n2, grid_rows, stride=GRID_W), :]
            else:
                ab_ref[...] = ab.astype(BF16)
    g_ref[...] = _dot(h, w_ref[0, :, P_COLS:IN_COLS]) + bgr_ref[0]
    gt_ref[...] = _dot_nt(wgt_ref[0], h) + bgc_ref[0]
    vo_ref[...] = _dot_nt(wvo_ref[0], h)


def _inproj(x2d, mod, layer, g1, w_in_bf, w_gt_bf, w_vo_bf, bg_row, bg_col, csc, seq_len, grid_ab, kv_cache=None):
    t = x2d.shape[0]
    tm = INPROJ_TILE
    tiles_per_seq = seq_len // tm
    grid_rows = tm // GRID_W if grid_ab else 0
    if grid_ab:
        ab_spec = pl.BlockSpec((grid_rows, GRID_W * 2 * B_W), lambda i: (i, 0))
        ab_shape = jax.ShapeDtypeStruct((t // GRID_W, GRID_W * 2 * B_W), F32)
        scratch = [pltpu.VMEM((2 * B_W // LANE, tm, LANE), F32)]
    else:
        ab_spec = pl.BlockSpec((tm, 2 * B_W), lambda i: (i, 0))
        ab_shape = jax.ShapeDtypeStruct((t, 2 * B_W), BF16)
        scratch = []
    kv_seq_len = kv_cache[0].shape[2] if kv_cache else 0
    kv_in_specs, kv_out_specs, kv_shapes, aliases = [], [], [], {}
    if kv_cache:
        per_tile = tm // kv_seq_len
        kv_in_specs = [pl.BlockSpec(memory_space=pl.ANY)] * 2
        kv_out_specs = [pl.BlockSpec((per_tile, 1, kv_seq_len, A_W), lambda i: (i, layer, 0, 0))] * 2
        kv_shapes = [jax.ShapeDtypeStruct(a.shape, a.dtype) for a in kv_cache]
        aliases = {9: 5, 10: 6}
    return pl.pallas_call(
        functools.partial(_inproj_kernel, grid_rows=grid_rows, kv_seq_len=kv_seq_len),
        grid=(t // tm,),
        input_output_aliases=aliases,
        in_specs=[
            pl.BlockSpec((tm, D_MODEL), lambda i: (i, 0)),
            pl.BlockSpec((1, 6, D_MODEL), lambda i: (i // tiles_per_seq, 0, 0)),
            pl.BlockSpec((1, 1, D_MODEL), lambda i: (layer, 0, 0)),
            pl.BlockSpec((1, D_MODEL, IN_COLS), lambda i: (layer, 0, 0)),
            pl.BlockSpec((1, N_GATE_COLS, D_MODEL), lambda i: (layer, 0, 0)),
            pl.BlockSpec((1, 2 * C_W, D_MODEL), lambda i: (layer, 0, 0)),
            pl.BlockSpec((1, 1, N_GATE_COLS), lambda i: (layer, 0, 0)),
            pl.BlockSpec((1, N_GATE_COLS, 1), lambda i: (layer, 0, 0)),
            pl.BlockSpec((B_W, 2 * B_W), lambda i: (0, 0)),
        ] + kv_in_specs,
        out_specs=[
            pl.BlockSpec((tm, P_KEEP), lambda i: (i, 0)),
            pl.BlockSpec((tm, N_GATE_COLS), lambda i: (i, 0)),
            pl.BlockSpec((N_GATE_COLS, tm), lambda i: (0, i)),
            pl.BlockSpec((2 * C_W, tm), lambda i: (0, i)),
            ab_spec,
        ] + kv_out_specs,
        out_shape=[
            jax.ShapeDtypeStruct((t, P_KEEP), BF16),
            jax.ShapeDtypeStruct((t, N_GATE_COLS), F32),
            jax.ShapeDtypeStruct((N_GATE_COLS, t), F32),
            jax.ShapeDtypeStruct((2 * C_W, t), F32),
            ab_shape,
        ] + kv_shapes,
        scratch_shapes=scratch,
        compiler_params=_params(("arbitrary",)),
        name="inproj",
    )(x2d, mod, g1, w_in_bf, w_gt_bf, w_vo_bf, bg_row, bg_col, csc, *(kv_cache or ()))


def _ctx_attn_kernel(q_ref, k_ref, v_ref, o_ref):
    scale = HEAD_DIM ** -0.5
    heads = range(H_ATT)
    sl = [slice(HEAD_DIM * h, HEAD_DIM * (h + 1)) for h in heads]
    s = [_dot_nt((q_ref[:, sl[h]] * scale).astype(BF16), k_ref[:, sl[h]].astype(BF16)) for h in heads]
    e = [jnp.exp(s[h] - jnp.max(s[h], axis=-1, keepdims=True)) for h in heads]
    w = [e[h] * (1.0 / jnp.sum(e[h], axis=-1, keepdims=True)) for h in heads]
    o_ref[...] = jnp.concatenate([_dot(w[h].astype(BF16), v_ref[:, sl[h]].astype(BF16)) for h in heads], axis=1)


def _ctx_attention(p, n_seq, seq_len):
    return pl.pallas_call(
        _ctx_attn_kernel,
        grid=(n_seq,),
        in_specs=[
            pl.BlockSpec((seq_len, A_W), lambda b: (b, 0)),
            pl.BlockSpec((seq_len, A_W), lambda b: (b, 1)),
            pl.BlockSpec((seq_len, A_W), lambda b: (b, 2)),
        ],
        out_specs=pl.BlockSpec((seq_len, A_W), lambda b: (b, 0)),
        out_shape=jax.ShapeDtypeStruct((n_seq * seq_len, A_W), F32),
        compiler_params=_params(("arbitrary",)),
        name="ctx_attention",
    )(p, p, p)


Q_ROWS = 8
K_ROWS = 16
KEY_BLK = 256


NO_ROW = 2 * WIN_R - 1


def _nbr_bias_tiles(rpb):
    n_c = 2 * WIN_C - 1
    cq = np.arange(GRID_W)[:, None]
    ck = np.arange(GRID_W)[None, :]
    cs = np.clip(cq - WIN_C // 2, 0, GRID_W - WIN_C)
    col_ok = (ck >= cs) & (ck < cs + WIN_C)
    pick = np.where(col_ok, np.clip(ck - cq + WIN_C - 1, 0, n_c - 1), n_c)
    sel = np.zeros((2, GRID_W, 2 * GRID_W, n_c + 1), np.float32)
    for side in range(2):
        sel[side, cq, side * GRID_W + ck, pick] = 1.0
    ext = jnp.full(rpb.shape[:2] + (2 * WIN_R, n_c + 1), NEG, F32)
    ext = ext.at[:, :, :2 * WIN_R - 1, :n_c].set(rpb.astype(F32))
    return jnp.einsum("lhdm,sqkm->lhdsqk", ext, jnp.asarray(sel), precision=lax.Precision.HIGHEST)


def _nbr_attn_kernel(q_ref, k0_ref, k1_ref, k2_ref, k3_ref, v0_ref, v1_ref, v2_ref, v3_ref,
                     ck_ref, cv_ref, tab_ref, o_ref, *, n_rb):
    scale = HEAD_DIM ** -0.5
    k_refs = (k0_ref, k1_ref, k2_ref, k3_ref)
    v_refs = (v0_ref, v1_ref, v2_ref, v3_ref)
    rb = pl.program_id(1)
    rows = GRID_W

    rows_per_blk = KEY_BLK // GRID_W
    n_blk = K_ROWS // rows_per_blk
    half_rows = Q_ROWS // 2
    half_tok = half_rows * GRID_W

    def window(rb_s):
        key_row0 = min(max(Q_ROWS * rb_s - WIN_R // 2, 0), rows - K_ROWS)
        d = []
        for rq in range(Q_ROWS):
            r = Q_ROWS * rb_s + rq
            rs = min(max(r - WIN_R // 2, 0), rows - WIN_R)
            d.append([key_row0 + rk - r + WIN_R - 1 if rs <= key_row0 + rk < rs + WIN_R else NO_ROW
                      for rk in range(K_ROWS)])
        active = tuple(
            tuple(j for j in range(n_blk)
                  if any(d[rq][rk] != NO_ROW for rq in range(half_rows * half, half_rows * (half + 1))
                         for rk in range(rows_per_blk * j, rows_per_blk * (j + 1))))
            for half in range(2))
        return active, d

    def attend(active, d):
        units = [(hh, half) for hh in range(2) for half in range(2)]
        sl = [slice(HEAD_DIM * hh, HEAD_DIM * (hh + 1)) for hh in range(2)]
        rows_of = [slice(half_tok * half, half_tok * (half + 1)) for half in range(2)]
        q = [(q_ref[rows_of[half], sl[hh]] * scale).astype(BF16) for hh, half in units]
        s_ctx = [_dot_nt(q[u], ck_ref[0, 0, :, sl[hh]].astype(BF16)) for u, (hh, half) in enumerate(units)]
        s_loc = []
        for u, (hh, half) in enumerate(units):
            blocks = []
            for j in active[half]:
                bias = jnp.concatenate([
                    jnp.concatenate([
                        tab_ref[0, hh, d[rq][rows_per_blk * j + 2 * p], 0]
                        + tab_ref[0, hh, d[rq][rows_per_blk * j + 2 * p + 1], 1]
                        for p in range(rows_per_blk // 2)], axis=1)
                    for rq in range(half_rows * half, half_rows * (half + 1))], axis=0)
                blocks.append(_dot_nt(q[u], k_refs[j][:, sl[hh]].astype(BF16)) + bias)
            s_loc.append(blocks)
        m = [jnp.max(s_ctx[u], axis=-1, keepdims=True) for u in range(len(units))]
        for u in range(len(units)):
            for s_blk in s_loc[u]:
                m[u] = jnp.maximum(m[u], jnp.max(s_blk, axis=-1, keepdims=True))
        e_ctx = [jnp.exp(s_ctx[u] - m[u]) for u in range(len(units))]
        den = [jnp.sum(e_ctx[u], axis=-1, keepdims=True) for u in range(len(units))]
        num = [_dot(e_ctx[u].astype(BF16), cv_ref[0, 0, :, sl[hh]].astype(BF16)) for u, (hh, half) in enumerate(units)]
        for u, (hh, half) in enumerate(units):
            for j, s_blk in zip(active[half], s_loc[u]):
                e = jnp.exp(s_blk - m[u])
                den[u] = den[u] + jnp.sum(e, axis=-1, keepdims=True)
                num[u] = num[u] + _dot(e.astype(BF16), v_refs[j][:, sl[hh]].astype(BF16))
        out = [num[u] / den[u] for u in range(len(units))]
        o_ref[...] = jnp.concatenate([jnp.concatenate([out[2 * hh], out[2 * hh + 1]], axis=0) for hh in range(2)],
                                     axis=1)

    @pl.when(rb == 0)
    def _():
        attend(*window(0))

    @pl.when((rb > 0) & (rb < n_rb - 1))
    def _():
        attend(*window(1))

    @pl.when(rb == n_rb - 1)
    def _():
        attend(*window(n_rb - 1))


def _nbr_attention(p, cache_k4, cache_v4, bias_tiles, layer, n_seq, seq_len):
    q_tok = Q_ROWS * GRID_W
    n_rb = seq_len // q_tok
    kb_per_seq = seq_len // KEY_BLK
    max_base = kb_per_seq - 4

    def kmap(j, col0):
        def f(hp, rb, b):
            base = jnp.clip(2 * rb - 1, 0, max_base)
            return (b * kb_per_seq + base + j, col0 + hp)
        return f

    past = cache_k4.shape[2]
    in_specs = [pl.BlockSpec((q_tok, 128), lambda hp, rb, b: (b * n_rb + rb, hp))]
    in_specs += [pl.BlockSpec((KEY_BLK, 128), kmap(j, A_W // 128)) for j in range(4)]
    in_specs += [pl.BlockSpec((KEY_BLK, 128), kmap(j, 2 * A_W // 128)) for j in range(4)]
    in_specs += [
        pl.BlockSpec((1, 1, past, 128), lambda hp, rb, b: (b, layer, 0, hp)),
        pl.BlockSpec((1, 1, past, 128), lambda hp, rb, b: (b, layer, 0, hp)),
        pl.BlockSpec((1, 2, 2 * WIN_R, 2, GRID_W, 2 * GRID_W), lambda hp, rb, b: (layer, hp, 0, 0, 0, 0)),
    ]
    assert n_rb >= 3
    return pl.pallas_call(
        functools.partial(_nbr_attn_kernel, n_rb=n_rb),
        grid=(H_ATT // 2, n_rb, n_seq),
        in_specs=in_specs,
        out_specs=pl.BlockSpec((q_tok, 128), lambda hp, rb, b: (b * n_rb + rb, hp)),
        out_shape=jax.ShapeDtypeStruct((n_seq * seq_len, A_W), F32),
        compiler_params=_params(("arbitrary", "arbitrary", "arbitrary")),
        name="nbr_attention",
    )(p, p, p, p, p, p, p, p, p, cache_k4, cache_v4, bias_tiles)


def _dft_mats(n):
    idx = jnp.arange(n, dtype=jnp.int32)
    ang = ((idx[:, None] * idx[None, :]) % n).astype(F32) * (2.0 * np.pi / n)
    return jnp.cos(ang).astype(BF16), jnp.sin(ang).astype(BF16)


def _channel_dft():
    c = np.arange(HEAD_DIM)
    ang = 2.0 * np.pi * ((c[:, None] * c[None, :]) % HEAD_DIM) / HEAD_DIM
    eye = np.eye(G_FOURIER)
    mats = np.concatenate([np.kron(eye, np.cos(ang)), np.kron(eye, np.sin(ang))], axis=1)
    return jnp.asarray(mats, F32).astype(BF16)


def _fourier_kernel(c_ref, s_ref, ab_ref, wf_ref, o_ref, acc_ref, *, scale, n_k):
    k = pl.program_id(2)

    @pl.when(k == 0)
    def _():
        acc_ref[...] = jnp.zeros_like(acc_ref)

    acc_ref[...] += _dot(c_ref[...], ab_ref[:, :B_W]) - _dot(s_ref[...], ab_ref[:, B_W:])

    @pl.when(k == n_k - 1)
    def _():
        z = (acc_ref[...] * scale).astype(BF16)
        o = _dot(z, wf_ref[0])
        for c in range(B_W // LANE):
            o_ref[c] = o[:, LANE * c:LANE * (c + 1)]


def _fourier(ab, cmat, smat, wf_blk, layer, n_seq, seq_len):
    ti = min(seq_len, 512)
    tk = min(seq_len, 1024)
    n_i, n_k = seq_len // ti, seq_len // tk
    scale = float((seq_len * HEAD_DIM) ** -0.5)
    return pl.pallas_call(
        functools.partial(_fourier_kernel, scale=scale, n_k=n_k),
        grid=(n_seq, n_i, n_k),
        in_specs=[
            pl.BlockSpec((ti, tk), lambda s, i, k: (i, k)),
            pl.BlockSpec((ti, tk), lambda s, i, k: (i, k)),
            pl.BlockSpec((tk, 2 * B_W), lambda s, i, k: (s * n_k + k, 0)),
            pl.BlockSpec((1, B_W, B_W), lambda s, i, k: (layer, 0, 0)),
        ],
        out_specs=pl.BlockSpec((B_W // LANE, ti, LANE), lambda s, i, k: (0, s * n_i + i, 0)),
        out_shape=jax.ShapeDtypeStruct((B_W // LANE, n_seq * seq_len, LANE), F32),
        scratch_shapes=[pltpu.VMEM((ti, B_W), F32)],
        compiler_params=_params(("arbitrary", "arbitrary", "arbitrary")),
        name="fourier",
    )(cmat, smat, ab, wf_blk)


FS_GROUP = 8


def _twiddles(side, n):
    k1 = jnp.arange(side, dtype=jnp.int32)[:, None]
    n2 = jnp.arange(side, dtype=jnp.int32)[None, :]
    ang = (k1 * n2).astype(F32) * (2.0 * np.pi / n)
    return jnp.cos(ang), jnp.sin(ang)


def _fourier_grid_kernel(ab_ref, c_ref, s_ref, tc_ref, ts_ref, wf_ref, o_ref, y_ref, *, side, scale):
    cmat = c_ref[...]
    smat = s_ref[...]
    for g in range(side // FS_GROUP):
        ab = ab_ref[:, 2 * B_W * FS_GROUP * g:2 * B_W * FS_GROUP * (g + 1)].astype(BF16)
        m1 = _dot(cmat, ab)
        m2 = _dot(smat, ab)
        for t in range(FS_GROUP):
            n2 = FS_GROUP * g + t
            a0 = 2 * B_W * t
            yr = m1[:, a0:a0 + B_W] - m2[:, a0 + B_W:a0 + 2 * B_W]
            yi = -(m1[:, a0 + B_W:a0 + 2 * B_W] + m2[:, a0:a0 + B_W])
            ct = tc_ref[:, n2:n2 + 1]
            st = ts_ref[:, n2:n2 + 1]
            y = jnp.concatenate([yr * ct + yi * st, yi * ct - yr * st], axis=1)
            for c in range(2 * B_W // LANE):
                for hi in range(side // 8):
                    r0 = (hi * side + n2) * 8
                    y_ref[c, r0:r0 + 8, :] = y[8 * hi:8 * (hi + 1), LANE * c:LANE * (c + 1)]
    for g in range(side // FS_GROUP):
        zs = []
        for j in range(FS_GROUP):
            k1 = FS_GROUP * g + j
            rows = pl.ds((k1 // 8) * side * 8 + k1 % 8, side, stride=8)
            n_lt = B_W // LANE
            y_re = jnp.concatenate([y_ref[c, rows, :] for c in range(n_lt)], axis=1).astype(BF16)
            y_im = jnp.concatenate([y_ref[n_lt + c, rows, :] for c in range(n_lt)], axis=1).astype(BF16)
            zs.append(_dot(cmat, y_re) + _dot(smat, y_im))
        z = (jnp.concatenate(zs, axis=0) * scale).astype(BF16)
        o = _dot(z, wf_ref[0])
        for j in range(FS_GROUP):
            for c in range(B_W // LANE):
                o_ref[c, pl.ds(FS_GROUP * g + j, side, stride=side), :] = o[side * j:side * (j + 1),
                                                                            LANE * c:LANE * (c + 1)]


def _fourier_grid(ab_grid, cmat, smat, tw_cos, tw_sin, wf_blk, layer, n_seq, side):
    seq_len = side * side
    scale = float((seq_len * HEAD_DIM) ** -0.5)
    small = pl.BlockSpec((side, side), lambda s: (0, 0))
    return pl.pallas_call(
        functools.partial(_fourier_grid_kernel, side=side, scale=scale),
        grid=(n_seq,),
        in_specs=[
            pl.BlockSpec((side, side * 2 * B_W), lambda s: (s, 0)),
            small,
            small,
            small,
            small,
            pl.BlockSpec((1, B_W, B_W), lambda s: (layer, 0, 0)),
        ],
        out_specs=pl.BlockSpec((B_W // LANE, seq_len, LANE), lambda s: (0, s, 0)),
        out_shape=jax.ShapeDtypeStruct((B_W // LANE, n_seq * seq_len, LANE), F32),
        scratch_shapes=[pltpu.VMEM((2 * B_W // LANE, seq_len, LANE), F32)],
        compiler_params=_params(("arbitrary",), vmem_mib=56),
        name="fourier_grid",
    )(ab_grid, cmat, smat, tw_cos, tw_sin, wf_blk)


SEQS_PER_STEP = 2


def _mlstm_kernel(*refs, n_chunks):
    sps = SEQS_PER_STEP
    n_side = 3 + 2 * sps
    fwd, bwd = refs[:n_side], refs[n_side:2 * n_side]
    c0_ref, n0_ref, m0_ref, tril_ref, triu_ref = refs[2 * n_side:2 * n_side + 5]
    hf_ref, hb_ref, cout_ref, nout_ref, mout_ref, c_s, n_s, m_s = refs[2 * n_side + 5:]
    c = pl.program_id(1)
    hi = lax.Precision.HIGHEST
    lc = MLSTM_CHUNK
    pair_w = 2 * HEAD_DIM
    n_pairs = H_MLSTM // 2

    @pl.when(c == 0)
    def _():
        c_s[...] = c0_ref[...]
        n_s[...] = n0_ref[...]
        m_s[...] = m0_ref[...]

    lo_lane = lax.broadcasted_iota(jnp.int32, (1, pair_w), 1) < HEAD_DIM
    lo_row = lax.broadcasted_iota(jnp.int32, (pair_w, 1), 0) < HEAD_DIM
    row8 = lax.broadcasted_iota(jnp.int32, (8, 1), 0)
    cum_mask = [tril_ref[...], triu_ref[...]]
    keep_t = [triu_ref[...] > 0.5, tril_ref[...] > 0.5]
    pairs = [(j, d, hp) for j in range(sps) for d in range(2) for hp in range(n_pairs)]
    heads = [(pi, hh) for pi in range(len(pairs)) for hh in range(2)]
    rng = range(len(heads))

    pre = {}
    for j in range(sps):
        for d, side in enumerate((fwd, bwd)):
            q_ref, k_ref, g_ref = side[:3]
            gt_ref, vt_ref = side[3 + 2 * j], side[4 + 2 * j]
            go = 2 * H_MLSTM * d
            lf_cols = _log_sigmoid(g_ref[j, :, go + H_MLSTM:go + 2 * H_MLSTM])
            lf_rows = _log_sigmoid(gt_ref[go + H_MLSTM:go + 2 * H_MLSTM, :])
            b_cols = _dot(cum_mask[d], lf_cols, precision=hi)
            pre[j, d] = dict(
                a_cols=g_ref[j, :, go:go + H_MLSTM] - b_cols,
                ig_rows=gt_ref[go:go + H_MLSTM, :],
                b_rows=_dot_nt(lf_rows, cum_mask[d], precision=hi),
                q=q_ref[j].astype(BF16), k=(k_ref[j] * (HEAD_DIM ** -0.5)).astype(BF16), vt=vt_ref[...])

    def pair_cols(hp):
        return slice(pair_w * hp, pair_w * (hp + 1))

    q_p = [pre[j, d]["q"][:, pair_cols(hp)] for j, d, hp in pairs]
    k_p = [pre[j, d]["k"][:, pair_cols(hp)] for j, d, hp in pairs]
    vt_p = [pre[j, d]["vt"][pair_cols(hp), :] for j, d, hp in pairs]
    c_p = [c_s[j, d, hp] for j, d, hp in pairs]
    n_p = [n_s[j, n_pairs * d + hp:n_pairs * d + hp + 1, :] for j, d, hp in pairs]
    zero_k = jnp.zeros((lc, pair_w), BF16)
    k_h = [jnp.where(lo_lane, k_p[pi], zero_k) if hh == 0 else jnp.where(lo_lane, zero_k, k_p[pi])
           for pi, hh in heads]

    def head_of(i):
        pi, hh = heads[i]
        j, d, hp = pairs[pi]
        return j, d, 2 * hp + hh

    b_row = [pre[head_of(i)[0], head_of(i)[1]]["b_rows"][head_of(i)[2]:head_of(i)[2] + 1, :] for i in rng]
    ig_row = [pre[head_of(i)[0], head_of(i)[1]]["ig_rows"][head_of(i)[2]:head_of(i)[2] + 1, :] for i in rng]
    a_col = [pre[head_of(i)[0], head_of(i)[1]]["a_cols"][:, head_of(i)[2]:head_of(i)[2] + 1] for i in rng]
    m_st = [m_s[head_of(i)[0], H_MLSTM * head_of(i)[1] + head_of(i)[2]:H_MLSTM * head_of(i)[1] + head_of(i)[2] + 1, :]
            for i in rng]
    bl = [b_row[i][:, lc - 1:lc] if head_of(i)[1] == 0 else b_row[i][:, 0:1] for i in rng]

    d_t = [jnp.where(keep_t[head_of(i)[1]], b_row[i] + a_col[i], NEG) for i in rng]
    inter = [b_row[i] + m_st[i] for i in rng]
    m_t = [jnp.maximum(inter[i], jnp.max(d_t[i], axis=0, keepdims=True)) for i in rng]
    s_t = [_dot_nt(k_h[i], q_p[heads[i][0]]) * jnp.exp(d_t[i] - m_t[i]) for i in rng]
    w_in = [jnp.exp(inter[i] - m_t[i]) for i in rng]
    num_t = [_dot(vt_p[heads[i][0]].astype(BF16), s_t[i].astype(BF16)) for i in rng]
    qc_t = [_dot_nt(c_p[pi].astype(BF16), q_p[pi]) for pi in range(len(pairs))]
    n_mat = [jnp.where((row8 == 0) & lo_lane, n_p[pi], jnp.where((row8 == 1) & ~lo_lane, n_p[pi], 0.0))
             for pi in range(len(pairs))]
    nq = [_dot_nt(n_mat[pi].astype(BF16), q_p[pi]) for pi in range(len(pairs))]
    den = [jnp.sum(s_t[i], axis=0, keepdims=True) + w_in[i] * nq[heads[i][0]][heads[i][1]:heads[i][1] + 1, :]
           for i in rng]
    inv = [1.0 / jnp.maximum(jnp.abs(den[i]), jnp.exp(-m_t[i])) for i in rng]
    h_t = []
    for pi in range(len(pairs)):
        i0, i1 = 2 * pi, 2 * pi + 1
        num = jnp.where(lo_row, num_t[i0], num_t[i1])
        h_t.append((num + jnp.where(lo_row, w_in[i0], w_in[i1]) * qc_t[pi]) * jnp.where(lo_row, inv[i0], inv[i1]))
    for j in range(sps):
        base = 2 * n_pairs * j
        hf_ref[j] = jnp.concatenate(h_t[base:base + n_pairs], axis=0)
        hb_ref[j] = jnp.concatenate(h_t[base + n_pairs:base + 2 * n_pairs], axis=0)

    g_row = [bl[i] - b_row[i] + ig_row[i] for i in rng]
    m_new = [jnp.maximum(bl[i] + m_st[i], jnp.max(g_row[i], axis=-1, keepdims=True)) for i in rng]
    wc = [jnp.exp(bl[i] + m_st[i] - m_new[i]) for i in rng]
    ws_row = [jnp.exp(g_row[i] - m_new[i]) for i in rng]
    upd = [_dot((vt_p[heads[i][0]] * ws_row[i]).astype(BF16), k_p[heads[i][0]]) for i in rng]
    for pi, (j, d, hp) in enumerate(pairs):
        i0, i1 = 2 * pi, 2 * pi + 1
        block = jnp.where(lo_row & lo_lane, upd[i0], jnp.where(~lo_row & ~lo_lane, upd[i1], 0.0))
        c_s[j, d, hp] = jnp.where(lo_row, wc[i0], wc[i1]) * c_p[pi] + block
        ws_mat = jnp.where(row8 == 0, ws_row[i0], jnp.where(row8 == 1, ws_row[i1], 0.0))
        k_sum = _dot(ws_mat.astype(BF16), k_p[pi])
        row = n_pairs * d + hp
        n_s[j, row:row + 1, :] = (jnp.where(lo_lane, wc[i0], wc[i1]) * n_p[pi]
                                  + jnp.where(lo_lane, k_sum[0:1, :], k_sum[1:2, :]))
    for i in rng:
        j, d, hd = head_of(i)
        m_s[j, H_MLSTM * d + hd:H_MLSTM * d + hd + 1, :] = m_new[i]

    @pl.when(c == n_chunks - 1)
    def _():
        cout_ref[...] = c_s[...]
        nout_ref[...] = n_s[...]
        mout_ref[...] = m_s[...]


def _pair_states(c):
    b = c.shape[0]
    c = c.reshape(b, 2, H_MLSTM // 2, 2, HEAD_DIM, HEAD_DIM)
    zero = jnp.zeros_like(c[:, :, :, 0])
    top = jnp.concatenate([c[:, :, :, 0], zero], axis=-1)
    bottom = jnp.concatenate([zero, c[:, :, :, 1]], axis=-1)
    return jnp.concatenate([top, bottom], axis=-2)


def _unpair_states(cp):
    b = cp.shape[0]
    first = cp[:, :, :, :HEAD_DIM, :HEAD_DIM]
    second = cp[:, :, :, HEAD_DIM:, HEAD_DIM:]
    return jnp.stack([first, second], axis=3).reshape(b, 2, H_MLSTM, HEAD_DIM, HEAD_DIM)


def _mlstm(p, g, gt, vo, c0, n0, m0, tril, triu, n_seq, seq_len):
    lc = MLSTM_CHUNK
    nc = seq_len // lc
    n_st = 2 * H_MLSTM
    n_pairs = H_MLSTM // 2
    pair_w = 2 * HEAD_DIM
    sps = SEQS_PER_STEP
    p3 = p.reshape(n_seq, seq_len, P_KEEP)
    g3 = g.reshape(n_seq, seq_len, N_GATE_COLS)

    def fwd(c):
        return c

    def bwd(c):
        return nc - 1 - c

    def side(chunk):
        tok = lambda col: (lambda b, c: (b, chunk(c), col))
        specs = [
            pl.BlockSpec((sps, lc, C_W), tok(QC_BLK)),
            pl.BlockSpec((sps, lc, C_W), tok(KC_BLK)),
            pl.BlockSpec((sps, lc, N_GATE_COLS), tok(0)),
        ]
        for j in range(sps):
            lanes = lambda b, c, j=j: (0, (b * sps + j) * nc + chunk(c))
            specs += [pl.BlockSpec((N_GATE_COLS, lc), lanes), pl.BlockSpec((C_W, lc), lanes)]
        return specs

    state_specs = [
        pl.BlockSpec((sps, 2, n_pairs, pair_w, pair_w), lambda b, c: (b, 0, 0, 0, 0)),
        pl.BlockSpec((sps, 2 * n_pairs, pair_w), lambda b, c: (b, 0, 0)),
        pl.BlockSpec((sps, n_st, 1), lambda b, c: (b, 0, 0)),
    ]
    tri_spec = pl.BlockSpec((lc, lc), lambda b, c: (0, 0))
    operands = [p3, p3, g3] + [gt, vo] * sps
    hf, hb, c_out, n_out, m_out = pl.pallas_call(
        functools.partial(_mlstm_kernel, n_chunks=nc),
        grid=(n_seq // sps, nc),
        in_specs=side(fwd) + side(bwd) + state_specs + [tri_spec, tri_spec],
        out_specs=[
            pl.BlockSpec((sps, C_W, lc), lambda b, c: (b, 0, c)),
            pl.BlockSpec((sps, C_W, lc), lambda b, c: (b, 0, nc - 1 - c)),
        ] + state_specs,
        out_shape=[
            jax.ShapeDtypeStruct((n_seq, C_W, seq_len), F32),
            jax.ShapeDtypeStruct((n_seq, C_W, seq_len), F32),
            jax.ShapeDtypeStruct((n_seq, 2, n_pairs, pair_w, pair_w), F32),
            jax.ShapeDtypeStruct((n_seq, 2 * n_pairs, pair_w), F32),
            jax.ShapeDtypeStruct((n_seq, n_st, 1), F32),
        ],
        scratch_shapes=[
            pltpu.VMEM((sps, 2, n_pairs, pair_w, pair_w), F32),
            pltpu.VMEM((sps, 2 * n_pairs, pair_w), F32),
            pltpu.VMEM((sps, n_st, 1), F32),
        ],
        compiler_params=_params(("arbitrary", "arbitrary")),
        name="mlstm",
    )(*operands, *operands, _pair_states(c0), n0.reshape(n_seq, 2 * n_pairs, pair_w), m0.reshape(n_seq, n_st, 1),
      tril, triu)
    return (hf, hb, _unpair_states(c_out), n_out.reshape(n_seq, 2, H_MLSTM, HEAD_DIM),
            m_out.reshape(n_seq, 2, H_MLSTM))


def _head_norm(y, g, ones_blk):
    ysq = y * y
    hi = ysq.astype(BF16)
    lo = (ysq - hi.astype(F32)).astype(BF16)
    ss = _dot(hi, ones_blk) + _dot(lo, ones_blk)
    return y * lax.rsqrt(ss * (1.0 / HEAD_DIM) + EPS) * g


def _merge_kernel(att_ref, four_ref, hf_ref, hb_ref, oc_ref, x_ref, mod_ref, gh_ref, ghm_ref, wo_ref, g2_ref, wrt_ref,
                  ones_ref, xo_ref, h2_ref, afft_ref, *, row_tiles):
    gh = gh_ref[0]
    half = x_ref.shape[0] // 2
    parts = [slice(half * p, half * (p + 1)) for p in range(2)]
    ya = [_head_norm(att_ref[r, :], gh[:, :A_W], ones_ref[...]) for r in parts]
    four = [jnp.concatenate([four_ref[c, r, :] for c in range(B_W // LANE)], axis=1) for r in parts]
    yf = [_head_norm(f, gh[:, A_W:A_W + B_W], ones_ref[:B_W, :B_W]) for f in four]
    mem = [hf_ref[0, :, r] + hb_ref[0, :, r] for r in parts]
    ym_t = []
    for p, r in enumerate(parts):
        heads = []
        for hd in range(H_MLSTM):
            y = mem[p][HEAD_DIM * hd:HEAD_DIM * (hd + 1), :]
            heads.append(y * lax.rsqrt(jnp.mean(y * y, axis=0, keepdims=True) + EPS))
        ym_t.append(jnp.concatenate(heads, axis=0) * ghm_ref[0, :, r] * jax.nn.sigmoid(oc_ref[:, r]))
    out = [_dot(ya[p].astype(BF16), wo_ref[0, :A_W, :])
           + _dot(yf[p].astype(BF16), wo_ref[0, A_W:A_W + B_W, :])
           + _dot_tn(ym_t[p].astype(BF16), wo_ref[0, A_W + B_W:, :]) for p in range(2)]
    x = [x_ref[r, :] + mod_ref[0, 2:3, :] * out[p] for p, r in enumerate(parts)]
    y2 = [xp * lax.rsqrt(jnp.mean(xp * xp, axis=-1, keepdims=True) + EPS) * g2_ref[0] for xp in x]
    h2 = [(yp * (1.0 + mod_ref[0, 4:5, :]) + mod_ref[0, 3:4, :]).astype(BF16) for yp in y2]
    logits = [_dot_nt(wrt_ref[0], hp) for hp in h2]
    e = [jnp.exp(lg - jnp.max(lg, axis=0, keepdims=True)) for lg in logits]
    for p, r in enumerate(parts):
        xo_ref[r, :] = x[p]
        if row_tiles:
            h2_wide = h2[p].astype(F32)
            for s in range(N_SLAB):
                h2_ref[pl.ds(N_SLAB * half * p + s, half, stride=N_SLAB), :] = h2_wide[:, LANE * s:LANE * (s + 1)]
        else:
            h2_ref[r, :] = h2[p]
        afft_ref[:, r] = e[p] / jnp.sum(e[p], axis=0, keepdims=True)


def _merge(att, four, hf, hb, vo, x2d, mod, layer, g_head, g_mem, w_out_bf, g2, w_rt_bf, ones_blk, n_seq, seq_len,
           mod_seq_len, row_tiles):
    t = x2d.shape[0]
    tm = MERGE_TILE
    tiles_per_mod = mod_seq_len // tm
    tiles_per_seq = seq_len // tm
    row = lambda i: (i, 0)
    lay = lambda i: (layer, 0, 0)
    mem = lambda i: (i // tiles_per_seq, 0, i % tiles_per_seq)
    if row_tiles:
        h2_spec, h2_shape = pl.BlockSpec((N_SLAB * tm, LANE), row), jax.ShapeDtypeStruct((N_SLAB * t, LANE), F32)
    else:
        h2_spec, h2_shape = pl.BlockSpec((tm, D_MODEL), row), jax.ShapeDtypeStruct((t, D_MODEL), BF16)
    return pl.pallas_call(
        functools.partial(_merge_kernel, row_tiles=row_tiles),
        grid=(t // tm,),
        in_specs=[
            pl.BlockSpec((tm, A_W), row),
            pl.BlockSpec((B_W // LANE, tm, LANE), lambda i: (0, i, 0)),
            pl.BlockSpec((1, C_W, tm), mem),
            pl.BlockSpec((1, C_W, tm), mem),
            pl.BlockSpec((C_W, tm), lambda i: (1, i)),
            pl.BlockSpec((tm, D_MODEL), row),
            pl.BlockSpec((1, 6, D_MODEL), lambda i: (i // tiles_per_mod, 0, 0)),
            pl.BlockSpec((1, 1, D_MODEL), lay),
            pl.BlockSpec((1, C_W, tm), lay),
            pl.BlockSpec((1, D_MODEL, D_MODEL), lay),
            pl.BlockSpec((1, 1, D_MODEL), lay),
            pl.BlockSpec((1, N_EXPERTS, D_MODEL), lay),
            pl.BlockSpec((A_W, A_W), lambda i: (0, 0)),
        ],
        out_specs=[
            pl.BlockSpec((tm, D_MODEL), row),
            h2_spec,
            pl.BlockSpec((N_EXPERTS, tm), lambda i: (0, i)),
        ],
        out_shape=[
            jax.ShapeDtypeStruct((t, D_MODEL), F32),
            h2_shape,
            jax.ShapeDtypeStruct((N_EXPERTS, t), F32),
        ],
        compiler_params=_params(("arbitrary",)),
        name="merge",
    )(att, four, hf, hb, vo, x2d, mod, g_head, g_mem, w_out_bf, g2, w_rt_bf, ones_blk)


BISECT_STEPS = 48
TOKEN_CHUNK = 1024


TOKEN_SPLIT = 64


def _route_kernel(aff_ref, triu_ref, idx_ref, gs_ref, sp_ref, *blocked, ns, seq_len, cap):
    seqs = range(ns)
    aff = [aff_ref[:, seq_len * j:seq_len * (j + 1)] for j in seqs]

    def body(_, bounds):
        out = []
        for j in seqs:
            lo, hi = bounds[j]
            mid = 0.5 * (lo + hi)
            ge = jnp.sum(jnp.where(aff[j] >= mid, 1.0, 0.0), axis=1, keepdims=True) >= cap
            out.append((jnp.where(ge, mid, lo), jnp.where(ge, hi, mid)))
        return tuple(out)

    start = (jnp.zeros((N_EXPERTS, 1), F32), jnp.full((N_EXPERTS, 1), 2.0, F32))
    bounds = lax.fori_loop(0, BISECT_STEPS, body, tuple(start for _ in seqs))
    thr = [jnp.max(jnp.where(aff[j] < bounds[j][1], aff[j], -1.0), axis=1, keepdims=True) for j in seqs]
    need = [cap - jnp.sum(jnp.where(aff[j] > thr[j], 1.0, 0.0), axis=1, keepdims=True) for j in seqs]
    triu = triu_ref[...]
    eq_carry = [jnp.zeros((N_EXPERTS, 1), F32) for _ in seqs]
    pos_carry = [jnp.zeros((N_EXPERTS, 1), F32) for _ in seqs]
    n_blk = seq_len // 128
    lane = lax.broadcasted_iota(jnp.int32, (1, 128), 1)
    blocked_refs = blocked if blocked else None
    first = [jnp.zeros((N_EXPERTS, 128), F32) for _ in seqs]
    after = [jnp.full((N_EXPERTS, 128), 1e9, F32) for _ in seqs]
    for b in range(n_blk):
        for j in seqs:
            blk = aff[j][:, 128 * b:128 * (b + 1)]
            eq = blk == thr[j]
            eq_f = jnp.where(eq, 1.0, 0.0)
            eq_inc = _dot(eq_f.astype(BF16), triu) + eq_carry[j]
            sel = (blk > thr[j]) | (eq & (eq_inc - eq_f < need[j]))
            sel_f = jnp.where(sel, 1.0, 0.0)
            pos_inc = _dot(sel_f.astype(BF16), triu) + pos_carry[j]
            t0 = seq_len * j + 128 * b
            sp_ref[:, t0:t0 + 128] = jnp.where(sel, pos_inc - sel_f, -1.0).astype(jnp.int32)
            if blocked_refs:
                cum_s, aff_s, _ = blocked_refs
                r0 = (j * n_blk + b) * N_EXPERTS
                cum_s[r0:r0 + N_EXPERTS, :] = pos_inc - pos_carry[j]
                aff_s[r0:r0 + N_EXPERTS, :] = blk
                first[j] = jnp.where(lane == b, pos_carry[j], first[j])
                after[j] = jnp.where(lane == b, pos_inc[:, 127:128], after[j])
            eq_carry[j] = eq_inc[:, 127:128]
            pos_carry[j] = pos_inc[:, 127:128]

    if blocked_refs:
        cum_s, aff_s, bnd_s = blocked_refs
        for j in seqs:
            bnd_s[N_EXPERTS * j:N_EXPERTS * (j + 1), :] = first[j]
            bnd_s[N_EXPERTS * (ns + j):N_EXPERTS * (ns + j + 1), :] = after[j]
        slot_col = lax.broadcasted_iota(jnp.int32, (cap, 1), 0).astype(F32)
        lane_f = lane.astype(F32)

        def per_expert(e, carry):
            for j in seqs:
                first_row = bnd_s[pl.ds(N_EXPERTS * j + e, 1), :]
                after_row = bnd_s[pl.ds(N_EXPERTS * (ns + j) + e, 1), :]
                blk_of = jnp.sum(jnp.where(after_row <= slot_col, 1.0, 0.0), axis=1, keepdims=True)
                in_blk = lane_f == blk_of
                local = slot_col - jnp.sum(jnp.where(in_blk, first_row, 0.0), axis=1, keepdims=True)
                pick = jnp.where(in_blk, 1.0, 0.0)[:, :n_blk].astype(BF16)
                rows = pl.ds(N_EXPERTS * n_blk * j + e, n_blk, stride=N_EXPERTS)
                counts = _dot(pick, cum_s[rows, :].astype(BF16))
                tok_in = jnp.sum(jnp.where(counts <= local, 1.0, 0.0), axis=1, keepdims=True)
                a = aff_s[rows, :]
                a_hi = a.astype(BF16)
                a_mid = (a - a_hi.astype(F32)).astype(BF16)
                a_lo = (a - a_hi.astype(F32) - a_mid.astype(F32)).astype(BF16)
                gates = _dot(pick, a_hi) + _dot(pick, a_mid) + _dot(pick, a_lo)
                idx_ref[N_EXPERTS * j + e] = (N_SLAB * (128.0 * blk_of + tok_in)).astype(jnp.int32)
                gs_ref[e, cap * j:cap * (j + 1)] = jnp.sum(jnp.where(lane_f == tok_in, gates, 0.0), axis=1,
                                                           keepdims=True)
            return carry

        lax.fori_loop(0, N_EXPERTS, per_expert, 0)
        return

    tc = min(seq_len, TOKEN_CHUNK)
    slot = lax.broadcasted_iota(jnp.int32, (cap, tc), 0)
    part = lax.broadcasted_iota(jnp.int32, (8, tc), 0)
    tok = lax.broadcasted_iota(jnp.int32, (1, tc), 1).astype(F32)
    chunks = range(0, seq_len, tc)
    tok_hi = [jnp.floor((tok + float(t0)) * (1.0 / TOKEN_SPLIT)) for t0 in chunks]
    tok_lo = [tok + float(t0) - TOKEN_SPLIT * hi for t0, hi in zip(chunks, tok_hi)]

    def per_expert(e, carry):
        for j in seqs:
            acc = jnp.zeros((cap, 8), F32)
            for ci, t0 in enumerate(chunks):
                cols = slice(seq_len * j + t0, seq_len * j + t0 + tc)
                onehot = jnp.where(slot == sp_ref[pl.ds(e, 1), cols], 1.0, 0.0).astype(BF16)
                a = aff_ref[pl.ds(e, 1), cols]
                a_hi = a.astype(BF16).astype(F32)
                a_mid = (a - a_hi).astype(BF16).astype(F32)
                a_lo = a - a_hi - a_mid
                vals = jnp.where(part == 0, tok_hi[ci], jnp.where(part == 1, tok_lo[ci], jnp.where(
                    part == 2, a_hi, jnp.where(part == 3, a_mid, jnp.where(part == 4, a_lo, 0.0)))))
                acc = acc + _dot_nt(onehot, vals.astype(BF16))
            idx_ref[N_EXPERTS * j + e] = (N_SLAB * (TOKEN_SPLIT * acc[:, 0:1] + acc[:, 1:2])).astype(jnp.int32)
            gs_ref[e, cap * j:cap * (j + 1)] = acc[:, 2:3] + acc[:, 3:4] + acc[:, 4:5]
        return carry

    lax.fori_loop(0, N_EXPERTS, per_expert, 0)


def _route(afft, triu_bf, n_seq, seq_len, cap, ns):
    n_blk = seq_len // 128
    idx, gs, sp = pl.pallas_call(
        functools.partial(_route_kernel, ns=ns, seq_len=seq_len, cap=cap),
        grid=(n_seq // ns,),
        in_specs=[
            pl.BlockSpec((N_EXPERTS, ns * seq_len), lambda s: (0, s)),
            pl.BlockSpec((128, 128), lambda s: (0, 0)),
        ],
        out_specs=[
            pl.BlockSpec((ns * N_EXPERTS, cap, 1), lambda s: (s, 0, 0)),
            pl.BlockSpec((N_EXPERTS, ns * cap, 1), lambda s: (0, s, 0)),
            pl.BlockSpec((N_EXPERTS, ns * seq_len), lambda s: (0, s)),
        ],
        out_shape=[
            jax.ShapeDtypeStruct((n_seq * N_EXPERTS, cap, 1), jnp.int32),
            jax.ShapeDtypeStruct((N_EXPERTS, n_seq * cap, 1), F32),
            jax.ShapeDtypeStruct((N_EXPERTS, n_seq * seq_len), jnp.int32),
        ],
        scratch_shapes=[
            pltpu.VMEM((ns * n_blk * N_EXPERTS, 128), F32),
            pltpu.VMEM((ns * n_blk * N_EXPERTS, 128), F32),
            pltpu.VMEM((2 * ns * N_EXPERTS, 128), F32),
        ] if n_blk >= 16 else [],
        compiler_params=_params(("arbitrary",)),
        name="route",
    )(afft, triu_bf)
    return idx.reshape(n_seq * N_EXPERTS * cap), gs, sp


ROW_COPIES = 8


def _row_tile(first_row):
    return pl.ds(pl.multiple_of(first_row, N_SLAB), N_SLAB)


def _gather_kernel(idx_ref, src_ref, xs_ref, tile_ref, *, eb, cap):
    ei = pl.program_id(1)

    def per_expert(ee, carry):
        e = ei * eb + ee

        def rows(g, c):
            slot0 = g * ROW_COPIES
            for u in range(ROW_COPIES):
                tile_ref[_row_tile((slot0 + u) * N_SLAB), :] = src_ref[_row_tile(idx_ref[e * cap + slot0 + u]), :]
            return c

        lax.fori_loop(0, cap // ROW_COPIES, rows, 0)
        for s in range(N_SLAB):
            xs_ref[ee, :, LANE * s:LANE * (s + 1)] = tile_ref[pl.ds(s, cap, stride=N_SLAB), :].astype(BF16)
        return carry

    lax.fori_loop(0, eb, per_expert, 0)


def _gather(idx, h2_rows, n_seq, cap, eb):
    return pl.pallas_call(
        functools.partial(_gather_kernel, eb=eb, cap=cap),
        grid=(n_seq, N_EXPERTS // eb),
        in_specs=[
            pl.BlockSpec((N_EXPERTS * cap,), lambda s, e: (s,), memory_space=pltpu.SMEM),
            pl.BlockSpec((h2_rows.shape[0] // n_seq, LANE), lambda s, e: (s, 0)),
        ],
        out_specs=pl.BlockSpec((eb, cap, D_MODEL), lambda s, e: (e, s, 0)),
        out_shape=jax.ShapeDtypeStruct((N_EXPERTS, n_seq * cap, D_MODEL), BF16),
        scratch_shapes=[pltpu.VMEM((N_SLAB * cap, LANE), F32)],
        compiler_params=_params(("arbitrary", "arbitrary"), vmem_mib=56),
        name="gather",
    )(idx, h2_rows)


def _selection(sp_ref, cap):
    seq_len = sp_ref.shape[1]
    slot = lax.broadcasted_iota(jnp.int32, (cap, seq_len), 0)
    return jnp.concatenate([jnp.where(slot == sp_ref[e:e + 1, :], 1.0, 0.0) for e in range(N_EXPERTS)],
                           axis=0).astype(BF16)


def _gather_short_kernel(sp_ref, h_ref, xs_ref, *, cap):
    rows = _dot(_selection(sp_ref, cap), h_ref[...])
    for e in range(N_EXPERTS):
        xs_ref[e] = rows[cap * e:cap * (e + 1), :].astype(BF16)


def _gather_short(sp, h2, n_seq, seq_len, cap):
    return pl.pallas_call(
        functools.partial(_gather_short_kernel, cap=cap),
        grid=(n_seq,),
        in_specs=[
            pl.BlockSpec((N_EXPERTS, seq_len), lambda s: (0, s)),
            pl.BlockSpec((seq_len, D_MODEL), lambda s: (s, 0)),
        ],
        out_specs=pl.BlockSpec((N_EXPERTS, cap, D_MODEL), lambda s: (0, s, 0)),
        out_shape=jax.ShapeDtypeStruct((N_EXPERTS, n_seq * cap, D_MODEL), BF16),
        compiler_params=_params(("arbitrary",)),
        name="gather_short",
    )(sp, h2)


def _scatter_short_kernel(sp_ref, y_ref, x_ref, mod_ref, gf_ref, o_ref, *, cap, final):
    sel = _selection(sp_ref, cap)
    y = jnp.concatenate([y_ref[e] for e in range(N_EXPERTS)], axis=0)
    y_hi = y.astype(BF16)
    y_lo = (y - y_hi.astype(F32)).astype(BF16)
    moe = _dot_tn(sel, y_hi) + _dot_tn(sel, y_lo)
    x = x_ref[...] + mod_ref[0, 5:6, :] * moe
    if final:
        x = x * lax.rsqrt(jnp.mean(x * x, axis=-1, keepdims=True) + EPS) * gf_ref[...]
    o_ref[...] = x


def _scatter_short(sp, ys, x2d, mod, g_final, n_seq, seq_len, cap, final):
    return pl.pallas_call(
        functools.partial(_scatter_short_kernel, cap=cap, final=final),
        grid=(n_seq,),
        in_specs=[
            pl.BlockSpec((N_EXPERTS, seq_len), lambda s: (0, s)),
            pl.BlockSpec((N_EXPERTS, cap, D_MODEL), lambda s: (0, s, 0)),
            pl.BlockSpec((seq_len, D_MODEL), lambda s: (s, 0)),
            pl.BlockSpec((1, 6, D_MODEL), lambda s: (0, 0, 0)),
            pl.BlockSpec((1, D_MODEL), lambda s: (0, 0)),
        ],
        out_specs=pl.BlockSpec((seq_len, D_MODEL), lambda s: (s, 0)),
        out_shape=jax.ShapeDtypeStruct((n_seq * seq_len, D_MODEL), F32),
        compiler_params=_params(("arbitrary",)),
        name="scatter_short",
    )(sp, ys, x2d, mod, g_final)


def _expert_kernel(xc_ref, xl_ref, gc_ref, gl_ref, wg_ref, wu_ref, wd_ref, yc_ref, yl_ref, *, n_f):
    f = pl.program_id(1)

    @pl.when(f == 0)
    def _():
        yc_ref[...] = jnp.zeros_like(yc_ref)
        yl_ref[...] = jnp.zeros_like(yl_ref)

    wg = wg_ref[0, 0].astype(BF16)
    wu = wu_ref[0, 0].astype(BF16)
    wd = wd_ref[0, 0].astype(BF16)
    for x_ref, y_ref in ((xc_ref, yc_ref), (xl_ref, yl_ref)):
        x = x_ref[0]
        mid = (_silu(_dot(x, wg)) * _dot(x, wu)).astype(BF16)
        y_ref[0] += _dot(mid, wd)

    @pl.when(f == n_f - 1)
    def _():
        yc_ref[0] = yc_ref[0] * gc_ref[0]
        yl_ref[0] = yl_ref[0] * gl_ref[0]


def _experts(xs_c, xs_l, gs_c, gs_l, w_g, w_u, w_d, layer):
    rc, rl = xs_c.shape[1], xs_l.shape[1]
    tf = 1024
    n_f = EXPERT_FF // tf
    return pl.pallas_call(
        functools.partial(_expert_kernel, n_f=n_f),
        grid=(N_EXPERTS, n_f),
        in_specs=[
            pl.BlockSpec((1, rc, D_MODEL), lambda e, f: (e, 0, 0)),
            pl.BlockSpec((1, rl, D_MODEL), lambda e, f: (e, 0, 0)),
            pl.BlockSpec((1, rc, 1), lambda e, f: (e, 0, 0)),
            pl.BlockSpec((1, rl, 1), lambda e, f: (e, 0, 0)),
            pl.BlockSpec((1, 1, D_MODEL, tf), lambda e, f: (layer, e, 0, f)),
            pl.BlockSpec((1, 1, D_MODEL, tf), lambda e, f: (layer, e, 0, f)),
            pl.BlockSpec((1, 1, tf, D_MODEL), lambda e, f: (layer, e, f, 0)),
        ],
        out_specs=[
            pl.BlockSpec((1, rc, D_MODEL), lambda e, f: (e, 0, 0)),
            pl.BlockSpec((1, rl, D_MODEL), lambda e, f: (e, 0, 0)),
        ],
        out_shape=[
            jax.ShapeDtypeStruct((N_EXPERTS, rc, D_MODEL), F32),
            jax.ShapeDtypeStruct((N_EXPERTS, rl, D_MODEL), F32),
        ],
        compiler_params=_params(("arbitrary", "arbitrary"), vmem_mib=56),
        name="experts",
    )(xs_c, xs_l, gs_c, gs_l, w_g, w_u, w_d)


def _scatter_kernel(idx_ref, y_ref, x_ref, mod_ref, gf_ref, o_ref, acc_ref, tile_ref, *, eb, n_e, cap, tm, final):
    step = pl.program_id(1)

    @pl.when(step == 0)
    def _():
        acc_ref[...] = jnp.zeros_like(acc_ref)

    @pl.when(step < n_e)
    def _():
        def per_expert(ee, carry):
            e = step * eb + ee
            for s in range(N_SLAB):
                tile_ref[pl.ds(s, cap, stride=N_SLAB), :] = y_ref[ee, :, LANE * s:LANE * (s + 1)]

            def rows(g, c):
                slot0 = g * ROW_COPIES
                dst = [idx_ref[e * cap + slot0 + u] for u in range(ROW_COPIES)]
                new = [acc_ref[_row_tile(dst[u]), :] + tile_ref[_row_tile((slot0 + u) * N_SLAB), :]
                       for u in range(ROW_COPIES)]
                for u in range(ROW_COPIES):
                    acc_ref[_row_tile(dst[u]), :] = new[u]
                return c

            lax.fori_loop(0, cap // ROW_COPIES, rows, 0)
            return carry

        lax.fori_loop(0, eb, per_expert, 0)

    @pl.when(step >= n_e)
    def _():
        base = pl.multiple_of((step - n_e) * tm * N_SLAB, N_SLAB)
        moe = jnp.concatenate([acc_ref[pl.ds(base + s, tm, stride=N_SLAB), :] for s in range(N_SLAB)], axis=1)
        x = x_ref[...] + mod_ref[0, 5:6, :] * moe
        if final:
            x = x * lax.rsqrt(jnp.mean(x * x, axis=-1, keepdims=True) + EPS) * gf_ref[...]
        o_ref[...] = x


def _scatter(idx, ys, x2d, mod, g_final, n_seq, seq_len, cap, eb, tm, final):
    n_e = N_EXPERTS // eb
    n_out = seq_len // tm
    out_blk = lambda s, j: (s * n_out + jnp.maximum(j - n_e, 0), 0)
    return pl.pallas_call(
        functools.partial(_scatter_kernel, eb=eb, n_e=n_e, cap=cap, tm=tm, final=final),
        grid=(n_seq, n_e + n_out),
        in_specs=[
            pl.BlockSpec((N_EXPERTS * cap,), lambda s, j: (s,), memory_space=pltpu.SMEM),
            pl.BlockSpec((eb, cap, D_MODEL), lambda s, j: (jnp.minimum(j, n_e - 1), s, 0)),
            pl.BlockSpec((tm, D_MODEL), out_blk),
            pl.BlockSpec((1, 6, D_MODEL), lambda s, j: (s, 0, 0)),
            pl.BlockSpec((1, D_MODEL), lambda s, j: (0, 0)),
        ],
        out_specs=pl.BlockSpec((tm, D_MODEL), out_blk),
        out_shape=jax.ShapeDtypeStruct((n_seq * seq_len, D_MODEL), F32),
        scratch_shapes=[
            pltpu.VMEM((N_SLAB * seq_len, LANE), F32),
            pltpu.VMEM((N_SLAB * cap, LANE), F32),
        ],
        compiler_params=_params(("arbitrary", "arbitrary"), vmem_mib=56),
        name="scatter",
    )(idx, ys, x2d, mod, g_final)


def kernel(x_prompt, x_sample, c, cache_k, cache_v, state_C, state_n, state_m, c_ctx, w_ada, b_ada, g_norm1, g_norm2, w_in, b_gates, rpb, w_fourier, g_head, w_out, w_router, w_exp_gate, w_exp_up, w_exp_down, g_final):
    n_ctx, len_ctx, _ = x_prompt.shape
    n_lat, len_lat, _ = x_sample.shape
    past = cache_k.shape[2]
    cap_ctx = CAPACITY_FACTOR * len_ctx // N_EXPERTS
    cap_lat = CAPACITY_FACTOR * len_lat // N_EXPERTS

    w_in_bf = w_in.astype(BF16)
    w_gt_bf = jnp.swapaxes(w_in[:, :, P_COLS:], 1, 2).astype(BF16)
    vc0, oc0 = VC_BLK * C_W, OC_BLK * C_W
    w_vo_bf = jnp.swapaxes(jnp.concatenate([w_in[:, :, vc0:vc0 + C_W], w_in[:, :, oc0:oc0 + C_W]], axis=2),
                           1, 2).astype(BF16)
    bg_row = b_gates.reshape(DEPTH, 1, N_GATE_COLS).astype(F32)
    bg_col = b_gates.reshape(DEPTH, N_GATE_COLS, 1).astype(F32)
    w_out_bf = w_out.astype(BF16)
    w_rt_bf = jnp.swapaxes(w_router, 1, 2).astype(BF16)
    g1 = g_norm1.reshape(DEPTH, 1, D_MODEL)
    g2 = g_norm2.reshape(DEPTH, 1, D_MODEL)
    gh = g_head.reshape(DEPTH, 1, D_MODEL)
    g_mem = jnp.broadcast_to(gh[:, 0, A_W + B_W:, None], (DEPTH, C_W, MERGE_TILE))
    eye_g = jnp.eye(G_FOURIER, dtype=F32)
    wf_blk = jnp.einsum("lgcd,gh->lgchd", w_fourier, eye_g).reshape(DEPTH, B_W, B_W).astype(BF16)

    csc = _channel_dft()
    dft_ctx = _dft_mats(len_ctx)
    dft_side = _dft_mats(GRID_W)
    tw_cos, tw_sin = _twiddles(GRID_W, len_lat)
    bias_tiles = _nbr_bias_tiles(rpb)
    r = np.arange(MLSTM_CHUNK)
    tril = jnp.asarray(r[:, None] >= r[None, :], F32)
    triu = jnp.asarray(r[:, None] <= r[None, :], F32)
    triu_bf = triu.astype(BF16)
    hidx = np.arange(A_W) // HEAD_DIM
    ones_blk = jnp.asarray(hidx[:, None] == hidx[None, :], BF16)

    cvecs = jnp.concatenate([c_ctx[None, :], c, jnp.zeros((8 - 1 - n_lat, D_MODEL), F32)], axis=0)
    mod_all = _modulation(cvecs, w_ada, b_ada).reshape(DEPTH, 8, 6, D_MODEL)

    cache_k4 = cache_k.reshape(n_lat, DEPTH, past, A_W)
    cache_v4 = cache_v.reshape(n_lat, DEPTH, past, A_W)
    zero_c = jnp.zeros((n_ctx, 2, H_MLSTM, HEAD_DIM, HEAD_DIM), F32)
    zero_n = jnp.zeros((n_ctx, 2, H_MLSTM, HEAD_DIM), F32)
    zero_m = jnp.zeros((n_ctx, 2, H_MLSTM), F32)

    xc = x_prompt.reshape(n_ctx * len_ctx, D_MODEL)
    xl = x_sample.reshape(n_lat * len_lat, D_MODEL)
    gf = g_final.reshape(1, D_MODEL)
    new_k = jnp.zeros((n_ctx, DEPTH, len_ctx, A_W), F32)
    new_v = jnp.zeros((n_ctx, DEPTH, len_ctx, A_W), F32)
    cs, ns, ms = [], [], []
    for l in range(DEPTH):
        mod_c = mod_all[l, 0:1]
        mod_l = mod_all[l, 1:1 + n_lat]

        pc, gc, gtc, voc, abc, new_k, new_v = _inproj(xc, mod_c, l, g1, w_in_bf, w_gt_bf, w_vo_bf, bg_row, bg_col, csc,
                                                      n_ctx * len_ctx, False, (new_k, new_v))
        att_c = _ctx_attention(pc, n_ctx, len_ctx)
        four_c = _fourier(abc, dft_ctx[0], dft_ctx[1], wf_blk, l, n_ctx, len_ctx)
        hf_c, hb_c, c_new, n_new, m_new = _mlstm(pc, gc, gtc, voc, zero_c, zero_n, zero_m, tril, triu, n_ctx, len_ctx)
        xc, h2c, affc = _merge(att_c, four_c, hf_c, hb_c, voc, xc, mod_c, l, gh, g_mem, w_out_bf, g2, w_rt_bf,
                               ones_blk, n_ctx, len_ctx, n_ctx * len_ctx, False)
        cs.append(c_new)
        ns.append(n_new)
        ms.append(m_new)

        pq, gq, gtq, voq, abq = _inproj(xl, mod_l, l, g1, w_in_bf, w_gt_bf, w_vo_bf, bg_row, bg_col, csc, len_lat,
                                        True)
        att_l = _nbr_attention(pq, cache_k4, cache_v4, bias_tiles, l, n_lat, len_lat)
        four_l = _fourier_grid(abq, dft_side[0], dft_side[1], tw_cos, tw_sin, wf_blk, l, n_lat, GRID_W)
        hf_l, hb_l, _, _, _ = _mlstm(pq, gq, gtq, voq, state_C[:, l], state_n[:, l], state_m[:, l], tril, triu,
                                     n_lat, len_lat)
        xl, h2l, affl = _merge(att_l, four_l, hf_l, hb_l, voq, xl, mod_l, l, gh, g_mem, w_out_bf, g2, w_rt_bf,
                               ones_blk, n_lat, len_lat, len_lat, True)

        last = l == DEPTH - 1
        _, gs_c, sp_c = _route(affc, triu_bf, n_ctx, len_ctx, cap_ctx, n_ctx)
        idx_l, gs_l, _ = _route(affl, triu_bf, n_lat, len_lat, cap_lat, n_lat)
        xs_c = _gather_short(sp_c, h2c, n_ctx, len_ctx, cap_ctx)
        xs_l = _gather(idx_l, h2l, n_lat, cap_lat, 4)
        ys_c, ys_l = _experts(xs_c, xs_l, gs_c, gs_l, w_exp_gate, w_exp_up, w_exp_down, l)
        xc = _scatter_short(sp_c, ys_c, xc, mod_c, gf, n_ctx, len_ctx, cap_ctx, last)
        xl = _scatter(idx_l, ys_l, xl, mod_l, gf, n_lat, len_lat, cap_lat, 4, 1024, last)

    y_prompt = xc.reshape(n_ctx, len_ctx, D_MODEL)
    y_sample = xl.reshape(n_lat, len_lat, D_MODEL)
    kv_shape = (n_ctx, DEPTH, len_ctx, H_ATT, HEAD_DIM)
    return (y_prompt, y_sample, new_k.reshape(kv_shape), new_v.reshape(kv_shape), jnp.stack(cs, axis=1),
            jnp.stack(ns, axis=1), jnp.stack(ms, axis=1))
```

```python
import functools

import numpy as np
import jax
import jax.numpy as jnp
from jax import lax
from jax.experimental import pallas as pl
from jax.experimental.pallas import tpu as pltpu

F32 = jnp.float32
BF16 = jnp.bfloat16

D_MODEL = 1024
DEPTH = 2
HEAD_DIM = 64
H_ATT = 8
G_FOURIER = 4
H_MLSTM = 4
A_W = H_ATT * HEAD_DIM
B_W = G_FOURIER * HEAD_DIM
C_W = H_MLSTM * HEAD_DIM
N_GATE_COLS = 16
P_COLS = 3 * A_W + B_W + 4 * C_W
IN_COLS = P_COLS + N_GATE_COLS
P_KEEP = 3 * A_W + B_W + 2 * C_W
GRID_W = 64
WIN_R = 8
WIN_C = 16
MLSTM_CHUNK = 128
N_EXPERTS = 16
CAPACITY_FACTOR = 2
EXPERT_FF = 2 * D_MODEL
EPS = 1e-6
NEG = -1e30

UB_OFF = 3 * A_W
QC_BLK, KC_BLK, VC_BLK, OC_BLK = 7, 8, 9, 10

LANE = 128
N_SLAB = D_MODEL // LANE
MERGE_TILE = 256
INPROJ_TILE = 512

NT_DIMS = (((1,), (1,)), ((), ()))
TN_DIMS = (((0,), (0,)), ((), ()))
MIB = 1024 * 1024


def _dot(a, b, precision=None):
    return jnp.dot(a, b, preferred_element_type=F32, precision=precision)


def _dot_nt(a, b, precision=None):
    return lax.dot_general(a, b, NT_DIMS, preferred_element_type=F32, precision=precision)


def _dot_tn(a, b):
    return lax.dot_general(a, b, TN_DIMS, preferred_element_type=F32)


def _params(sem, vmem_mib=48):
    return pltpu.CompilerParams(dimension_semantics=sem, vmem_limit_bytes=vmem_mib * MIB)


def _silu(x):
    return x * jax.nn.sigmoid(x)


def _log_sigmoid(x):
    return jnp.minimum(x, 0.0) - jnp.log1p(jnp.exp(-jnp.abs(x)))


def _mod_kernel(c_ref, w_ref, b_ref, o_ref):
    s = _silu(c_ref[...]).astype(BF16)
    o_ref[0] = _dot(s, w_ref[0].astype(BF16)) + b_ref[0]


def _modulation(cvecs, w_ada, b_ada):
    depth = w_ada.shape[0]
    tn = 1024
    return pl.pallas_call(
        _mod_kernel,
        grid=(depth, 6 * D_MODEL // tn),
        in_specs=[
            pl.BlockSpec((8, D_MODEL), lambda l, j: (0, 0)),
            pl.BlockSpec((1, D_MODEL, tn), lambda l, j: (l, 0, j)),
            pl.BlockSpec((1, 1, tn), lambda l, j: (l, 0, j)),
        ],
        out_specs=pl.BlockSpec((1, 8, tn), lambda l, j: (l, 0, j)),
        out_shape=jax.ShapeDtypeStruct((depth, 8, 6 * D_MODEL), F32),
        compiler_params=_params(("arbitrary", "arbitrary")),
        name="modulation",
    )(cvecs, w_ada, b_ada.reshape(depth, 1, 6 * D_MODEL))


def _inproj_kernel(x_ref, mod_ref, g1_ref, w_ref, wgt_ref, wvo_ref, bgr_ref, bgc_ref, csc_ref,
                   *rest, grid_rows, kv_seq_len):
    n_in = 2 if kv_seq_len else 0
    p_ref, g_ref, gt_ref, vo_ref, ab_ref = rest[n_in:n_in + 5]
    kv_refs = rest[n_in + 5:n_in + 5 + n_in]
    scratch = rest[n_in + 5 + n_in:]
    x = x_ref[...]
    y = x * lax.rsqrt(jnp.mean(x * x, axis=-1, keepdims=True) + EPS) * g1_ref[0]
    h = (y * (1.0 + mod_ref[0, 1:2, :]) + mod_ref[0, 0:1, :]).astype(BF16)
    for j in range(0, P_KEEP, 256):
        pj = _dot(h, w_ref[0, :, j:j + 256])
        p_ref[:, j:j + 256] = pj.astype(BF16)
        if kv_seq_len and A_W <= j < 3 * A_W:
            kv_ref = kv_refs[(j - A_W) // A_W]
            c0 = (j - A_W) % A_W
            for b in range(pj.shape[0] // kv_seq_len):
                kv_ref[b, 0, :, c0:c0 + 256] = pj[kv_seq_len * b:kv_seq_len * (b + 1), :]
        if j == UB_OFF:
            ab = _dot(pj.astype(BF16), csc_ref[...])
            if grid_rows:
                stage_ref, = scratch
                n_lt = 2 * B_W // LANE
                for c in range(n_lt):
                    stage_ref[c] = ab[:, LANE * c:LANE * (c + 1)]
                for n2 in range(GRID_W):
                    for c in range(n_lt):
                        col = 2 * B_W * n2 + LANE * c
                        ab_ref[:, col:col + LANE] = stage_ref[c, pl.ds(n2, grid_rows, stride=GRID_W), :]
            else:
                ab_ref[...] = ab.astype(BF16)
    g_ref[...] = _dot(h, w_ref[0, :, P_COLS:IN_COLS]) + bgr_ref[0]
    gt_ref[...] = _dot_nt(wgt_ref[0], h) + bgc_ref[0]
    vo_ref[...] = _dot_nt(wvo_ref[0], h)


def _inproj(x2d, mod, layer, g1, w_in_bf, w_gt_bf, w_vo_bf, bg_row, bg_col, csc, seq_len, grid_ab, kv_cache=None):
    t = x2d.shape[0]
    tm = INPROJ_TILE
    tiles_per_seq = seq_len // tm
    grid_rows = tm // GRID_W if grid_ab else 0
    if grid_ab:
        ab_spec = pl.BlockSpec((grid_rows, GRID_W * 2 * B_W), lambda i: (i, 0))
        ab_shape = jax.ShapeDtypeStruct((t // GRID_W, GRID_W * 2 * B_W), F32)
        scratch = [pltpu.VMEM((2 * B_W // LANE, tm, LANE), F32)]
    else:
        ab_spec = pl.BlockSpec((tm, 2 * B_W), lambda i: (i, 0))
        ab_shape = jax.ShapeDtypeStruct((t, 2 * B_W), BF16)
        scratch = []
    kv_seq_len = kv_cache[0].shape[2] if kv_cache else 0
    kv_in_specs, kv_out_specs, kv_shapes, aliases = [], [], [], {}
    if kv_cache:
        per_tile = tm // kv_seq_len
        kv_in_specs = [pl.BlockSpec(memory_space=pl.ANY)] * 2
        kv_out_specs = [pl.BlockSpec((per_tile, 1, kv_seq_len, A_W), lambda i: (i, layer, 0, 0))] * 2
        kv_shapes = [jax.ShapeDtypeStruct(a.shape, a.dtype) for a in kv_cache]
        aliases = {9: 5, 10: 6}
    return pl.pallas_call(
        functools.partial(_inproj_kernel, grid_rows=grid_rows, kv_seq_len=kv_seq_len),
        grid=(t // tm,),
        input_output_aliases=aliases,
        in_specs=[
            pl.BlockSpec((tm, D_MODEL), lambda i: (i, 0)),
            pl.BlockSpec((1, 6, D_MODEL), lambda i: (i // tiles_per_seq, 0, 0)),
            pl.BlockSpec((1, 1, D_MODEL), lambda i: (layer, 0, 0)),
            pl.BlockSpec((1, D_MODEL, IN_COLS), lambda i: (layer, 0, 0)),
            pl.BlockSpec((1, N_GATE_COLS, D_MODEL), lambda i: (layer, 0, 0)),
            pl.BlockSpec((1, 2 * C_W, D_MODEL), lambda i: (layer, 0, 0)),
            pl.BlockSpec((1, 1, N_GATE_COLS), lambda i: (layer, 0, 0)),
            pl.BlockSpec((1, N_GATE_COLS, 1), lambda i: (layer, 0, 0)),
            pl.BlockSpec((B_W, 2 * B_W), lambda i: (0, 0)),
        ] + kv_in_specs,
        out_specs=[
            pl.BlockSpec((tm, P_KEEP), lambda i: (i, 0)),
            pl.BlockSpec((tm, N_GATE_COLS), lambda i: (i, 0)),
            pl.BlockSpec((N_GATE_COLS, tm), lambda i: (0, i)),
            pl.BlockSpec((2 * C_W, tm), lambda i: (0, i)),
            ab_spec,
        ] + kv_out_specs,
        out_shape=[
            jax.ShapeDtypeStruct((t, P_KEEP), BF16),
            jax.ShapeDtypeStruct((t, N_GATE_COLS), F32),
            jax.ShapeDtypeStruct((N_GATE_COLS, t), F32),
            jax.ShapeDtypeStruct((2 * C_W, t), F32),
            ab_shape,
        ] + kv_shapes,
        scratch_shapes=scratch,
        compiler_params=_params(("arbitrary",)),
        name="inproj",
    )(x2d, mod, g1, w_in_bf, w_gt_bf, w_vo_bf, bg_row, bg_col, csc, *(kv_cache or ()))


def _ctx_attn_kernel(q_ref, k_ref, v_ref, o_ref):
    scale = HEAD_DIM ** -0.5
    heads = range(H_ATT)
    sl = [slice(HEAD_DIM * h, HEAD_DIM * (h + 1)) for h in heads]
    s = [_dot_nt((q_ref[:, sl[h]] * scale).astype(BF16), k_ref[:, sl[h]].astype(BF16)) for h in heads]
    e = [jnp.exp(s[h] - jnp.max(s[h], axis=-1, keepdims=True)) for h in heads]
    w = [e[h] * (1.0 / jnp.sum(e[h], axis=-1, keepdims=True)) for h in heads]
    o_ref[...] = jnp.concatenate([_dot(w[h].astype(BF16), v_ref[:, sl[h]].astype(BF16)) for h in heads], axis=1)


def _ctx_attention(p, n_seq, seq_len):
    return pl.pallas_call(
        _ctx_attn_kernel,
        grid=(n_seq,),
        in_specs=[
            pl.BlockSpec((seq_len, A_W), lambda b: (b, 0)),
            pl.BlockSpec((seq_len, A_W), lambda b: (b, 1)),
            pl.BlockSpec((seq_len, A_W), lambda b: (b, 2)),
        ],
        out_specs=pl.BlockSpec((seq_len, A_W), lambda b: (b, 0)),
        out_shape=jax.ShapeDtypeStruct((n_seq * seq_len, A_W), F32),
        compiler_params=_params(("arbitrary",)),
        name="ctx_attention",
    )(p, p, p)


Q_ROWS = 8
K_ROWS = 16
KEY_BLK = 256


NO_ROW = 2 * WIN_R - 1


def _nbr_bias_tiles(rpb):
    n_c = 2 * WIN_C - 1
    cq = np.arange(GRID_W)[:, None]
    ck = np.arange(GRID_W)[None, :]
    cs = np.clip(cq - WIN_C // 2, 0, GRID_W - WIN_C)
    col_ok = (ck >= cs) & (ck < cs + WIN_C)
    pick = np.where(col_ok, np.clip(ck - cq + WIN_C - 1, 0, n_c - 1), n_c)
    sel = np.zeros((2, GRID_W, 2 * GRID_W, n_c + 1), np.float32)
    for side in range(2):
        sel[side, cq, side * GRID_W + ck, pick] = 1.0
    ext = jnp.full(rpb.shape[:2] + (2 * WIN_R, n_c + 1), NEG, F32)
    ext = ext.at[:, :, :2 * WIN_R - 1, :n_c].set(rpb.astype(F32))
    return jnp.einsum("lhdm,sqkm->lhdsqk", ext, jnp.asarray(sel), precision=lax.Precision.HIGHEST)


def _nbr_attn_kernel(q_ref, k0_ref, k1_ref, k2_ref, k3_ref, v0_ref, v1_ref, v2_ref, v3_ref,
                     ck_ref, cv_ref, tab_ref, o_ref, *, n_rb):
    scale = HEAD_DIM ** -0.5
    k_refs = (k0_ref, k1_ref, k2_ref, k3_ref)
    v_refs = (v0_ref, v1_ref, v2_ref, v3_ref)
    rb = pl.program_id(1)
    rows = GRID_W

    rows_per_blk = KEY_BLK // GRID_W
    n_blk = K_ROWS // rows_per_blk
    half_rows = Q_ROWS // 2
    half_tok = half_rows * GRID_W

    def window(rb_s):
        key_row0 = min(max(Q_ROWS * rb_s - WIN_R // 2, 0), rows - K_ROWS)
        d = []
        for rq in range(Q_ROWS):
            r = Q_ROWS * rb_s + rq
            rs = min(max(r - WIN_R // 2, 0), rows - WIN_R)
            d.append([key_row0 + rk - r + WIN_R - 1 if rs <= key_row0 + rk < rs + WIN_R else NO_ROW
                      for rk in range(K_ROWS)])
        active = tuple(
            tuple(j for j in range(n_blk)
                  if any(d[rq][rk] != NO_ROW for rq in range(half_rows * half, half_rows * (half + 1))
                         for rk in range(rows_per_blk * j, rows_per_blk * (j + 1))))
            for half in range(2))
        return active, d

    def attend(active, d):
        units = [(hh, half) for hh in range(2) for half in range(2)]
        sl = [slice(HEAD_DIM * hh, HEAD_DIM * (hh + 1)) for hh in range(2)]
        rows_of = [slice(half_tok * half, half_tok * (half + 1)) for half in range(2)]
        q = [(q_ref[rows_of[half], sl[hh]] * scale).astype(BF16) for hh, half in units]
        s_ctx = [_dot_nt(q[u], ck_ref[0, 0, :, sl[hh]].astype(BF16)) for u, (hh, half) in enumerate(units)]
        s_loc = []
        for u, (hh, half) in enumerate(units):
            blocks = []
            for j in active[half]:
                bias = jnp.concatenate([
                    jnp.concatenate([
                        tab_ref[0, hh, d[rq][rows_per_blk * j + 2 * p], 0]
                        + tab_ref[0, hh, d[rq][rows_per_blk * j + 2 * p + 1], 1]
                        for p in range(rows_per_blk // 2)], axis=1)
                    for rq in range(half_rows * half, half_rows * (half + 1))], axis=0)
                blocks.append(_dot_nt(q[u], k_refs[j][:, sl[hh]].astype(BF16)) + bias)
            s_loc.append(blocks)
        m = [jnp.max(s_ctx[u], axis=-1, keepdims=True) for u in range(len(units))]
        for u in range(len(units)):
            for s_blk in s_loc[u]:
                m[u] = jnp.maximum(m[u], jnp.max(s_blk, axis=-1, keepdims=True))
        e_ctx = [jnp.exp(s_ctx[u] - m[u]) for u in range(len(units))]
        den = [jnp.sum(e_ctx[u], axis=-1, keepdims=True) for u in range(len(units))]
        num = [_dot(e_ctx[u].astype(BF16), cv_ref[0, 0, :, sl[hh]].astype(BF16)) for u, (hh, half) in enumerate(units)]
        for u, (hh, half) in enumerate(units):
            for j, s_blk in zip(active[half], s_loc[u]):
                e = jnp.exp(s_blk - m[u])
                den[u] = den[u] + jnp.sum(e, axis=-1, keepdims=True)
                num[u] = num[u] + _dot(e.astype(BF16), v_refs[j][:, sl[hh]].astype(BF16))
        out = [num[u] / den[u] for u in range(len(units))]
        o_ref[...] = jnp.concatenate([jnp.concatenate([out[2 * hh], out[2 * hh + 1]], axis=0) for hh in range(2)],
                                     axis=1)

    @pl.when(rb == 0)
    def _():
        attend(*window(0))

    @pl.when((rb > 0) & (rb < n_rb - 1))
    def _():
        attend(*window(1))

    @pl.when(rb == n_rb - 1)
    def _():
        attend(*window(n_rb - 1))


def _nbr_attention(p, cache_k4, cache_v4, bias_tiles, layer, n_seq, seq_len):
    q_tok = Q_ROWS * GRID_W
    n_rb = seq_len // q_tok
    kb_per_seq = seq_len // KEY_BLK
    max_base = kb_per_seq - 4

    def kmap(j, col0):
        def f(hp, rb, b):
            base = jnp.clip(2 * rb - 1, 0, max_base)
            return (b * kb_per_seq + base + j, col0 + hp)
        return f

    past = cache_k4.shape[2]
    in_specs = [pl.BlockSpec((q_tok, 128), lambda hp, rb, b: (b * n_rb + rb, hp))]
    in_specs += [pl.BlockSpec((KEY_BLK, 128), kmap(j, A_W // 128)) for j in range(4)]
    in_specs += [pl.BlockSpec((KEY_BLK, 128), kmap(j, 2 * A_W // 128)) for j in range(4)]
    in_specs += [
        pl.BlockSpec((1, 1, past, 128), lambda hp, rb, b: (b, layer, 0, hp)),
        pl.BlockSpec((1, 1, past, 128), lambda hp, rb, b: (b, layer, 0, hp)),
        pl.BlockSpec((1, 2, 2 * WIN_R, 2, GRID_W, 2 * GRID_W), lambda hp, rb, b: (layer, hp, 0, 0, 0, 0)),
    ]
    assert n_rb >= 3
    return pl.pallas_call(
        functools.partial(_nbr_attn_kernel, n_rb=n_rb),
        grid=(H_ATT // 2, n_rb, n_seq),
        in_specs=in_specs,
        out_specs=pl.BlockSpec((q_tok, 128), lambda hp, rb, b: (b * n_rb + rb, hp)),
        out_shape=jax.ShapeDtypeStruct((n_seq * seq_len, A_W), F32),
        compiler_params=_params(("arbitrary", "arbitrary", "arbitrary")),
        name="nbr_attention",
    )(p, p, p, p, p, p, p, p, p, cache_k4, cache_v4, bias_tiles)


def _dft_mats(n):
    idx = jnp.arange(n, dtype=jnp.int32)
    ang = ((idx[:, None] * idx[None, :]) % n).astype(F32) * (2.0 * np.pi / n)
    return jnp.cos(ang).astype(BF16), jnp.sin(ang).astype(BF16)


def _channel_dft():
    c = np.arange(HEAD_DIM)
    ang = 2.0 * np.pi * ((c[:, None] * c[None, :]) % HEAD_DIM) / HEAD_DIM
    eye = np.eye(G_FOURIER)
    mats = np.concatenate([np.kron(eye, np.cos(ang)), np.kron(eye, np.sin(ang))], axis=1)
    return jnp.asarray(mats, F32).astype(BF16)


def _fourier_kernel(c_ref, s_ref, ab_ref, wf_ref, o_ref, acc_ref, *, scale, n_k):
    k = pl.program_id(2)

    @pl.when(k == 0)
    def _():
        acc_ref[...] = jnp.zeros_like(acc_ref)

    acc_ref[...] += _dot(c_ref[...], ab_ref[:, :B_W]) - _dot(s_ref[...], ab_ref[:, B_W:])

    @pl.when(k == n_k - 1)
    def _():
        z = (acc_ref[...] * scale).astype(BF16)
        o = _dot(z, wf_ref[0])
        for c in range(B_W // LANE):
            o_ref[c] = o[:, LANE * c:LANE * (c + 1)]


def _fourier(ab, cmat, smat, wf_blk, layer, n_seq, seq_len):
    ti = min(seq_len, 512)
    tk = min(seq_len, 1024)
    n_i, n_k = seq_len // ti, seq_len // tk
    scale = float((seq_len * HEAD_DIM) ** -0.5)
    return pl.pallas_call(
        functools.partial(_fourier_kernel, scale=scale, n_k=n_k),
        grid=(n_seq, n_i, n_k),
        in_specs=[
            pl.BlockSpec((ti, tk), lambda s, i, k: (i, k)),
            pl.BlockSpec((ti, tk), lambda s, i, k: (i, k)),
            pl.BlockSpec((tk, 2 * B_W), lambda s, i, k: (s * n_k + k, 0)),
            pl.BlockSpec((1, B_W, B_W), lambda s, i, k: (layer, 0, 0)),
        ],
        out_specs=pl.BlockSpec((B_W // LANE, ti, LANE), lambda s, i, k: (0, s * n_i + i, 0)),
        out_shape=jax.ShapeDtypeStruct((B_W // LANE, n_seq * seq_len, LANE), F32),
        scratch_shapes=[pltpu.VMEM((ti, B_W), F32)],
        compiler_params=_params(("arbitrary", "arbitrary", "arbitrary")),
        name="fourier",
    )(cmat, smat, ab, wf_blk)


FS_GROUP = 8


def _twiddles(side, n):
    k1 = jnp.arange(side, dtype=jnp.int32)[:, None]
    n2 = jnp.arange(side, dtype=jnp.int32)[None, :]
    ang = (k1 * n2).astype(F32) * (2.0 * np.pi / n)
    return jnp.cos(ang), jnp.sin(ang)


def _fourier_grid_kernel(ab_ref, c_ref, s_ref, tc_ref, ts_ref, wf_ref, o_ref, y_ref, *, side, scale):
    cmat = c_ref[...]
    smat = s_ref[...]
    for g in range(side // FS_GROUP):
        ab = ab_ref[:, 2 * B_W * FS_GROUP * g:2 * B_W * FS_GROUP * (g + 1)].astype(BF16)
        m1 = _dot(cmat, ab)
        m2 = _dot(smat, ab)
        for t in range(FS_GROUP):
            n2 = FS_GROUP * g + t
            a0 = 2 * B_W * t
            yr = m1[:, a0:a0 + B_W] - m2[:, a0 + B_W:a0 + 2 * B_W]
            yi = -(m1[:, a0 + B_W:a0 + 2 * B_W] + m2[:, a0:a0 + B_W])
            ct = tc_ref[:, n2:n2 + 1]
            st = ts_ref[:, n2:n2 + 1]
            y = jnp.concatenate([yr * ct + yi * st, yi * ct - yr * st], axis=1)
            for c in range(2 * B_W // LANE):
                for hi in range(side // 8):
                    r0 = (hi * side + n2) * 8
                    y_ref[c, r0:r0 + 8, :] = y[8 * hi:8 * (hi + 1), LANE * c:LANE * (c + 1)]
    for g in range(side // FS_GROUP):
        zs = []
        for j in range(FS_GROUP):
            k1 = FS_GROUP * g + j
            rows = pl.ds((k1 // 8) * side * 8 + k1 % 8, side, stride=8)
            n_lt = B_W // LANE
            y_re = jnp.concatenate([y_ref[c, rows, :] for c in range(n_lt)], axis=1).astype(BF16)
            y_im = jnp.concatenate([y_ref[n_lt + c, rows, :] for c in range(n_lt)], axis=1).astype(BF16)
            zs.append(_dot(cmat, y_re) + _dot(smat, y_im))
        z = (jnp.concatenate(zs, axis=0) * scale).astype(BF16)
        o = _dot(z, wf_ref[0])
        for j in range(FS_GROUP):
            for c in range(B_W // LANE):
                o_ref[c, pl.ds(FS_GROUP * g + j, side, stride=side), :] = o[side * j:side * (j + 1),
                                                                            LANE * c:LANE * (c + 1)]


def _fourier_grid(ab_grid, cmat, smat, tw_cos, tw_sin, wf_blk, layer, n_seq, side):
    seq_len = side * side
    scale = float((seq_len * HEAD_DIM) ** -0.5)
    small = pl.BlockSpec((side, side), lambda s: (0, 0))
    return pl.pallas_call(
        functools.partial(_fourier_grid_kernel, side=side, scale=scale),
        grid=(n_seq,),
        in_specs=[
            pl.BlockSpec((side, side * 2 * B_W), lambda s: (s, 0)),
            small,
            small,
            small,
            small,
            pl.BlockSpec((1, B_W, B_W), lambda s: (layer, 0, 0)),
        ],
        out_specs=pl.BlockSpec((B_W // LANE, seq_len, LANE), lambda s: (0, s, 0)),
        out_shape=jax.ShapeDtypeStruct((B_W // LANE, n_seq * seq_len, LANE), F32),
        scratch_shapes=[pltpu.VMEM((2 * B_W // LANE, seq_len, LANE), F32)],
        compiler_params=_params(("arbitrary",), vmem_mib=56),
        name="fourier_grid",
    )(ab_grid, cmat, smat, tw_cos, tw_sin, wf_blk)


SEQS_PER_STEP = 2


def _mlstm_kernel(*refs, n_chunks):
    sps = SEQS_PER_STEP
    n_side = 3 + 2 * sps
    fwd, bwd = refs[:n_side], refs[n_side:2 * n_side]
    c0_ref, n0_ref, m0_ref, tril_ref, triu_ref = refs[2 * n_side:2 * n_side + 5]
    hf_ref, hb_ref, cout_ref, nout_ref, mout_ref, c_s, n_s, m_s = refs[2 * n_side + 5:]
    c = pl.program_id(1)
    hi = lax.Precision.HIGHEST
    lc = MLSTM_CHUNK
    pair_w = 2 * HEAD_DIM
    n_pairs = H_MLSTM // 2

    @pl.when(c == 0)
    def _():
        c_s[...] = c0_ref[...]
        n_s[...] = n0_ref[...]
        m_s[...] = m0_ref[...]

    lo_lane = lax.broadcasted_iota(jnp.int32, (1, pair_w), 1) < HEAD_DIM
    lo_row = lax.broadcasted_iota(jnp.int32, (pair_w, 1), 0) < HEAD_DIM
    row8 = lax.broadcasted_iota(jnp.int32, (8, 1), 0)
    cum_mask = [tril_ref[...], triu_ref[...]]
    keep_t = [triu_ref[...] > 0.5, tril_ref[...] > 0.5]
    pairs = [(j, d, hp) for j in range(sps) for d in range(2) for hp in range(n_pairs)]
    heads = [(pi, hh) for pi in range(len(pairs)) for hh in range(2)]
    rng = range(len(heads))

    pre = {}
    for j in range(sps):
        for d, side in enumerate((fwd, bwd)):
            q_ref, k_ref, g_ref = side[:3]
            gt_ref, vt_ref = side[3 + 2 * j], side[4 + 2 * j]
            go = 2 * H_MLSTM * d
            lf_cols = _log_sigmoid(g_ref[j, :, go + H_MLSTM:go + 2 * H_MLSTM])
            lf_rows = _log_sigmoid(gt_ref[go + H_MLSTM:go + 2 * H_MLSTM, :])
            b_cols = _dot(cum_mask[d], lf_cols, precision=hi)
            pre[j, d] = dict(
                a_cols=g_ref[j, :, go:go + H_MLSTM] - b_cols,
                ig_rows=gt_ref[go:go + H_MLSTM, :],
                b_rows=_dot_nt(lf_rows, cum_mask[d], precision=hi),
                q=q_ref[j].astype(BF16), k=(k_ref[j] * (HEAD_DIM ** -0.5)).astype(BF16), vt=vt_ref[...])

    def pair_cols(hp):
        return slice(pair_w * hp, pair_w * (hp + 1))

    q_p = [pre[j, d]["q"][:, pair_cols(hp)] for j, d, hp in pairs]
    k_p = [pre[j, d]["k"][:, pair_cols(hp)] for j, d, hp in pairs]
    vt_p = [pre[j, d]["vt"][pair_cols(hp), :] for j, d, hp in pairs]
    c_p = [c_s[j, d, hp] for j, d, hp in pairs]
    n_p = [n_s[j, n_pairs * d + hp:n_pairs * d + hp + 1, :] for j, d, hp in pairs]
    zero_k = jnp.zeros((lc, pair_w), BF16)
    k_h = [jnp.where(lo_lane, k_p[pi], zero_k) if hh == 0 else jnp.where(lo_lane, zero_k, k_p[pi])
           for pi, hh in heads]

    def head_of(i):
        pi, hh = heads[i]
        j, d, hp = pairs[pi]
        return j, d, 2 * hp + hh

    b_row = [pre[head_of(i)[0], head_of(i)[1]]["b_rows"][head_of(i)[2]:head_of(i)[2] + 1, :] for i in rng]
    ig_row = [pre[head_of(i)[0], head_of(i)[1]]["ig_rows"][head_of(i)[2]:head_of(i)[2] + 1, :] for i in rng]
    a_col = [pre[head_of(i)[0], head_of(i)[1]]["a_cols"][:, head_of(i)[2]:head_of(i)[2] + 1] for i in rng]
    m_st = [m_s[head_of(i)[0], H_MLSTM * head_of(i)[1] + head_of(i)[2]:H_MLSTM * head_of(i)[1] + head_of(i)[2] + 1, :]
            for i in rng]
    bl = [b_row[i][:, lc - 1:lc] if head_of(i)[1] == 0 else b_row[i][:, 0:1] for i in rng]

    d_t = [jnp.where(keep_t[head_of(i)[1]], b_row[i] + a_col[i], NEG) for i in rng]
    inter = [b_row[i] + m_st[i] for i in rng]
    m_t = [jnp.maximum(inter[i], jnp.max(d_t[i], axis=0, keepdims=True)) for i in rng]
    s_t = [_dot_nt(k_h[i], q_p[heads[i][0]]) * jnp.exp(d_t[i] - m_t[i]) for i in rng]
    w_in = [jnp.exp(inter[i] - m_t[i]) for i in rng]
    num_t = [_dot(vt_p[heads[i][0]].astype(BF16), s_t[i].astype(BF16)) for i in rng]
    qc_t = [_dot_nt(c_p[pi].astype(BF16), q_p[pi]) for pi in range(len(pairs))]
    n_mat = [jnp.where((row8 == 0) & lo_lane, n_p[pi], jnp.where((row8 == 1) & ~lo_lane, n_p[pi], 0.0))
             for pi in range(len(pairs))]
    nq = [_dot_nt(n_mat[pi].astype(BF16), q_p[pi]) for pi in range(len(pairs))]
    den = [jnp.sum(s_t[i], axis=0, keepdims=True) + w_in[i] * nq[heads[i][0]][heads[i][1]:heads[i][1] + 1, :]
           for i in rng]
    inv = [1.0 / jnp.maximum(jnp.abs(den[i]), jnp.exp(-m_t[i])) for i in rng]
    h_t = []
    for pi in range(len(pairs)):
        i0, i1 = 2 * pi, 2 * pi + 1
        num = jnp.where(lo_row, num_t[i0], num_t[i1])
        h_t.append((num + jnp.where(lo_row, w_in[i0], w_in[i1]) * qc_t[pi]) * jnp.where(lo_row, inv[i0], inv[i1]))
    for j in range(sps):
        base = 2 * n_pairs * j
        hf_ref[j] = jnp.concatenate(h_t[base:base + n_pairs], axis=0)
        hb_ref[j] = jnp.concatenate(h_t[base + n_pairs:base + 2 * n_pairs], axis=0)

    g_row = [bl[i] - b_row[i] + ig_row[i] for i in rng]
    m_new = [jnp.maximum(bl[i] + m_st[i], jnp.max(g_row[i], axis=-1, keepdims=True)) for i in rng]
    wc = [jnp.exp(bl[i] + m_st[i] - m_new[i]) for i in rng]
    ws_row = [jnp.exp(g_row[i] - m_new[i]) for i in rng]
    upd = [_dot((vt_p[heads[i][0]] * ws_row[i]).astype(BF16), k_p[heads[i][0]]) for i in rng]
    for pi, (j, d, hp) in enumerate(pairs):
        i0, i1 = 2 * pi, 2 * pi + 1
        block = jnp.where(lo_row & lo_lane, upd[i0], jnp.where(~lo_row & ~lo_lane, upd[i1], 0.0))
        c_s[j, d, hp] = jnp.where(lo_row, wc[i0], wc[i1]) * c_p[pi] + block
        ws_mat = jnp.where(row8 == 0, ws_row[i0], jnp.where(row8 == 1, ws_row[i1], 0.0))
        k_sum = _dot(ws_mat.astype(BF16), k_p[pi])
        row = n_pairs * d + hp
        n_s[j, row:row + 1, :] = (jnp.where(lo_lane, wc[i0], wc[i1]) * n_p[pi]
                                  + jnp.where(lo_lane, k_sum[0:1, :], k_sum[1:2, :]))
    for i in rng:
        j, d, hd = head_of(i)
        m_s[j, H_MLSTM * d + hd:H_MLSTM * d + hd + 1, :] = m_new[i]

    @pl.when(c == n_chunks - 1)
    def _():
        cout_ref[...] = c_s[...]
        nout_ref[...] = n_s[...]
        mout_ref[...] = m_s[...]


def _pair_states(c):
    b = c.shape[0]
    c = c.reshape(b, 2, H_MLSTM // 2, 2, HEAD_DIM, HEAD_DIM)
    zero = jnp.zeros_like(c[:, :, :, 0])
    top = jnp.concatenate([c[:, :, :, 0], zero], axis=-1)
    bottom = jnp.concatenate([zero, c[:, :, :, 1]], axis=-1)
    return jnp.concatenate([top, bottom], axis=-2)


def _unpair_states(cp):
    b = cp.shape[0]
    first = cp[:, :, :, :HEAD_DIM, :HEAD_DIM]
    second = cp[:, :, :, HEAD_DIM:, HEAD_DIM:]
    return jnp.stack([first, second], axis=3).reshape(b, 2, H_MLSTM, HEAD_DIM, HEAD_DIM)


def _mlstm(p, g, gt, vo, c0, n0, m0, tril, triu, n_seq, seq_len):
    lc = MLSTM_CHUNK
    nc = seq_len // lc
    n_st = 2 * H_MLSTM
    n_pairs = H_MLSTM // 2
    pair_w = 2 * HEAD_DIM
    sps = SEQS_PER_STEP
    p3 = p.reshape(n_seq, seq_len, P_KEEP)
    g3 = g.reshape(n_seq, seq_len, N_GATE_COLS)

    def fwd(c):
        return c

    def bwd(c):
        return nc - 1 - c

    def side(chunk):
        tok = lambda col: (lambda b, c: (b, chunk(c), col))
        specs = [
            pl.BlockSpec((sps, lc, C_W), tok(QC_BLK)),
            pl.BlockSpec((sps, lc, C_W), tok(KC_BLK)),
            pl.BlockSpec((sps, lc, N_GATE_COLS), tok(0)),
        ]
        for j in range(sps):
            lanes = lambda b, c, j=j: (0, (b * sps + j) * nc + chunk(c))
            specs += [pl.BlockSpec((N_GATE_COLS, lc), lanes), pl.BlockSpec((C_W, lc), lanes)]
        return specs

    state_specs = [
        pl.BlockSpec((sps, 2, n_pairs, pair_w, pair_w), lambda b, c: (b, 0, 0, 0, 0)),
        pl.BlockSpec((sps, 2 * n_pairs, pair_w), lambda b, c: (b, 0, 0)),
        pl.BlockSpec((sps, n_st, 1), lambda b, c: (b, 0, 0)),
    ]
    tri_spec = pl.BlockSpec((lc, lc), lambda b, c: (0, 0))
    operands = [p3, p3, g3] + [gt, vo] * sps
    hf, hb, c_out, n_out, m_out = pl.pallas_call(
        functools.partial(_mlstm_kernel, n_chunks=nc),
        grid=(n_seq // sps, nc),
        in_specs=side(fwd) + side(bwd) + state_specs + [tri_spec, tri_spec],
        out_specs=[
            pl.BlockSpec((sps, C_W, lc), lambda b, c: (b, 0, c)),
            pl.BlockSpec((sps, C_W, lc), lambda b, c: (b, 0, nc - 1 - c)),
        ] + state_specs,
        out_shape=[
            jax.ShapeDtypeStruct((n_seq, C_W, seq_len), F32),
            jax.ShapeDtypeStruct((n_seq, C_W, seq_len), F32),
            jax.ShapeDtypeStruct((n_seq, 2, n_pairs, pair_w, pair_w), F32),
            jax.ShapeDtypeStruct((n_seq, 2 * n_pairs, pair_w), F32),
            jax.ShapeDtypeStruct((n_seq, n_st, 1), F32),
        ],
        scratch_shapes=[
            pltpu.VMEM((sps, 2, n_pairs, pair_w, pair_w), F32),
            pltpu.VMEM((sps, 2 * n_pairs, pair_w), F32),
            pltpu.VMEM((sps, n_st, 1), F32),
        ],
        compiler_params=_params(("arbitrary", "arbitrary")),
        name="mlstm",
    )(*operands, *operands, _pair_states(c0), n0.reshape(n_seq, 2 * n_pairs, pair_w), m0.reshape(n_seq, n_st, 1),
      tril, triu)
    return (hf, hb, _unpair_states(c_out), n_out.reshape(n_seq, 2, H_MLSTM, HEAD_DIM),
            m_out.reshape(n_seq, 2, H_MLSTM))


def _head_norm(y, g, ones_blk):
    ysq = y * y
    hi = ysq.astype(BF16)
    lo = (ysq - hi.astype(F32)).astype(BF16)
    ss = _dot(hi, ones_blk) + _dot(lo, ones_blk)
    return y * lax.rsqrt(ss * (1.0 / HEAD_DIM) + EPS) * g


def _merge_kernel(att_ref, four_ref, hf_ref, hb_ref, oc_ref, x_ref, mod_ref, gh_ref, ghm_ref, wo_ref, g2_ref, wrt_ref,
                  ones_ref, xo_ref, h2_ref, afft_ref, *, row_tiles):
    gh = gh_ref[0]
    half = x_ref.shape[0] // 2
    parts = [slice(half * p, half * (p + 1)) for p in range(2)]
    ya = [_head_norm(att_ref[r, :], gh[:, :A_W], ones_ref[...]) for r in parts]
    four = [jnp.concatenate([four_ref[c, r, :] for c in range(B_W // LANE)], axis=1) for r in parts]
    yf = [_head_norm(f, gh[:, A_W:A_W + B_W], ones_ref[:B_W, :B_W]) for f in four]
    mem = [hf_ref[0, :, r] + hb_ref[0, :, r] for r in parts]
    ym_t = []
    for p, r in enumerate(parts):
        heads = []
        for hd in range(H_MLSTM):
            y = mem[p][HEAD_DIM * hd:HEAD_DIM * (hd + 1), :]
            heads.append(y * lax.rsqrt(jnp.mean(y * y, axis=0, keepdims=True) + EPS))
        ym_t.append(jnp.concatenate(heads, axis=0) * ghm_ref[0, :, r] * jax.nn.sigmoid(oc_ref[:, r]))
    out = [_dot(ya[p].astype(BF16), wo_ref[0, :A_W, :])
           + _dot(yf[p].astype(BF16), wo_ref[0, A_W:A_W + B_W, :])
           + _dot_tn(ym_t[p].astype(BF16), wo_ref[0, A_W + B_W:, :]) for p in range(2)]
    x = [x_ref[r, :] + mod_ref[0, 2:3, :] * out[p] for p, r in enumerate(parts)]
    y2 = [xp * lax.rsqrt(jnp.mean(xp * xp, axis=-1, keepdims=True) + EPS) * g2_ref[0] for xp in x]
    h2 = [(yp * (1.0 + mod_ref[0, 4:5, :]) + mod_ref[0, 3:4, :]).astype(BF16) for yp in y2]
    logits = [_dot_nt(wrt_ref[0], hp) for hp in h2]
    e = [jnp.exp(lg - jnp.max(lg, axis=0, keepdims=True)) for lg in logits]
    for p, r in enumerate(parts):
        xo_ref[r, :] = x[p]
        if row_tiles:
            h2_wide = h2[p].astype(F32)
            for s in range(N_SLAB):
                h2_ref[pl.ds(N_SLAB * half * p + s, half, stride=N_SLAB), :] = h2_wide[:, LANE * s:LANE * (s + 1)]
        else:
            h2_ref[r, :] = h2[p]
        afft_ref[:, r] = e[p] / jnp.sum(e[p], axis=0, keepdims=True)


def _merge(att, four, hf, hb, vo, x2d, mod, layer, g_head, g_mem, w_out_bf, g2, w_rt_bf, ones_blk, n_seq, seq_len,
           mod_seq_len, row_tiles):
    t = x2d.shape[0]
    tm = MERGE_TILE
    tiles_per_mod = mod_seq_len // tm
    tiles_per_seq = seq_len // tm
    row = lambda i: (i, 0)
    lay = lambda i: (layer, 0, 0)
    mem = lambda i: (i // tiles_per_seq, 0, i % tiles_per_seq)
    if row_tiles:
        h2_spec, h2_shape = pl.BlockSpec((N_SLAB * tm, LANE), row), jax.ShapeDtypeStruct((N_SLAB * t, LANE), F32)
    else:
        h2_spec, h2_shape = pl.BlockSpec((tm, D_MODEL), row), jax.ShapeDtypeStruct((t, D_MODEL), BF16)
    return pl.pallas_call(
        functools.partial(_merge_kernel, row_tiles=row_tiles),
        grid=(t // tm,),
        in_specs=[
            pl.BlockSpec((tm, A_W), row),
            pl.BlockSpec((B_W // LANE, tm, LANE), lambda i: (0, i, 0)),
            pl.BlockSpec((1, C_W, tm), mem),
            pl.BlockSpec((1, C_W, tm), mem),
            pl.BlockSpec((C_W, tm), lambda i: (1, i)),
            pl.BlockSpec((tm, D_MODEL), row),
            pl.BlockSpec((1, 6, D_MODEL), lambda i: (i // tiles_per_mod, 0, 0)),
            pl.BlockSpec((1, 1, D_MODEL), lay),
            pl.BlockSpec((1, C_W, tm), lay),
            pl.BlockSpec((1, D_MODEL, D_MODEL), lay),
            pl.BlockSpec((1, 1, D_MODEL), lay),
            pl.BlockSpec((1, N_EXPERTS, D_MODEL), lay),
            pl.BlockSpec((A_W, A_W), lambda i: (0, 0)),
        ],
        out_specs=[
            pl.BlockSpec((tm, D_MODEL), row),
            h2_spec,
            pl.BlockSpec((N_EXPERTS, tm), lambda i: (0, i)),
        ],
        out_shape=[
            jax.ShapeDtypeStruct((t, D_MODEL), F32),
            h2_shape,
            jax.ShapeDtypeStruct((N_EXPERTS, t), F32),
        ],
        compiler_params=_params(("arbitrary",)),
        name="merge",
    )(att, four, hf, hb, vo, x2d, mod, g_head, g_mem, w_out_bf, g2, w_rt_bf, ones_blk)


BISECT_STEPS = 48
TOKEN_CHUNK = 1024


TOKEN_SPLIT = 64


def _route_kernel(aff_ref, triu_ref, idx_ref, gs_ref, sp_ref, *blocked, ns, seq_len, cap):
    seqs = range(ns)
    aff = [aff_ref[:, seq_len * j:seq_len * (j + 1)] for j in seqs]

    def body(_, bounds):
        out = []
        for j in seqs:
            lo, hi = bounds[j]
            mid = 0.5 * (lo + hi)
            ge = jnp.sum(jnp.where(aff[j] >= mid, 1.0, 0.0), axis=1, keepdims=True) >= cap
            out.append((jnp.where(ge, mid, lo), jnp.where(ge, hi, mid)))
        return tuple(out)

    start = (jnp.zeros((N_EXPERTS, 1), F32), jnp.full((N_EXPERTS, 1), 2.0, F32))
    bounds = lax.fori_loop(0, BISECT_STEPS, body, tuple(start for _ in seqs))
    thr = [jnp.max(jnp.where(aff[j] < bounds[j][1], aff[j], -1.0), axis=1, keepdims=True) for j in seqs]
    need = [cap - jnp.sum(jnp.where(aff[j] > thr[j], 1.0, 0.0), axis=1, keepdims=True) for j in seqs]
    triu = triu_ref[...]
    eq_carry = [jnp.zeros((N_EXPERTS, 1), F32) for _ in seqs]
    pos_carry = [jnp.zeros((N_EXPERTS, 1), F32) for _ in seqs]
    n_blk = seq_len // 128
    lane = lax.broadcasted_iota(jnp.int32, (1, 128), 1)
    blocked_refs = blocked if blocked else None
    first = [jnp.zeros((N_EXPERTS, 128), F32) for _ in seqs]
    after = [jnp.full((N_EXPERTS, 128), 1e9, F32) for _ in seqs]
    for b in range(n_blk):
        for j in seqs:
            blk = aff[j][:, 128 * b:128 * (b + 1)]
            eq = blk == thr[j]
            eq_f = jnp.where(eq, 1.0, 0.0)
            eq_inc = _dot(eq_f.astype(BF16), triu) + eq_carry[j]
            sel = (blk > thr[j]) | (eq & (eq_inc - eq_f < need[j]))
            sel_f = jnp.where(sel, 1.0, 0.0)
            pos_inc = _dot(sel_f.astype(BF16), triu) + pos_carry[j]
            t0 = seq_len * j + 128 * b
            sp_ref[:, t0:t0 + 128] = jnp.where(sel, pos_inc - sel_f, -1.0).astype(jnp.int32)
            if blocked_refs:
                cum_s, aff_s, _ = blocked_refs
                r0 = (j * n_blk + b) * N_EXPERTS
                cum_s[r0:r0 + N_EXPERTS, :] = pos_inc - pos_carry[j]
                aff_s[r0:r0 + N_EXPERTS, :] = blk
                first[j] = jnp.where(lane == b, pos_carry[j], first[j])
                after[j] = jnp.where(lane == b, pos_inc[:, 127:128], after[j])
            eq_carry[j] = eq_inc[:, 127:128]
            pos_carry[j] = pos_inc[:, 127:128]

    if blocked_refs:
        cum_s, aff_s, bnd_s = blocked_refs
        for j in seqs:
            bnd_s[N_EXPERTS * j:N_EXPERTS * (j + 1), :] = first[j]
            bnd_s[N_EXPERTS * (ns + j):N_EXPERTS * (ns + j + 1), :] = after[j]
        slot_col = lax.broadcasted_iota(jnp.int32, (cap, 1), 0).astype(F32)
        lane_f = lane.astype(F32)

        def per_expert(e, carry):
            for j in seqs:
                first_row = bnd_s[pl.ds(N_EXPERTS * j + e, 1), :]
                after_row = bnd_s[pl.ds(N_EXPERTS * (ns + j) + e, 1), :]
                blk_of = jnp.sum(jnp.where(after_row <= slot_col, 1.0, 0.0), axis=1, keepdims=True)
                in_blk = lane_f == blk_of
                local = slot_col - jnp.sum(jnp.where(in_blk, first_row, 0.0), axis=1, keepdims=True)
                pick = jnp.where(in_blk, 1.0, 0.0)[:, :n_blk].astype(BF16)
                rows = pl.ds(N_EXPERTS * n_blk * j + e, n_blk, stride=N_EXPERTS)
                counts = _dot(pick, cum_s[rows, :].astype(BF16))
                tok_in = jnp.sum(jnp.where(counts <= local, 1.0, 0.0), axis=1, keepdims=True)
                a = aff_s[rows, :]
                a_hi = a.astype(BF16)
                a_mid = (a - a_hi.astype(F32)).astype(BF16)
                a_lo = (a - a_hi.astype(F32) - a_mid.astype(F32)).astype(BF16)
                gates = _dot(pick, a_hi) + _dot(pick, a_mid) + _dot(pick, a_lo)
                idx_ref[N_EXPERTS * j + e] = (N_SLAB * (128.0 * blk_of + tok_in)).astype(jnp.int32)
                gs_ref[e, cap * j:cap * (j + 1)] = jnp.sum(jnp.where(lane_f == tok_in, gates, 0.0), axis=1,
                                                           keepdims=True)
            return carry

        lax.fori_loop(0, N_EXPERTS, per_expert, 0)
        return

    tc = min(seq_len, TOKEN_CHUNK)
    slot = lax.broadcasted_iota(jnp.int32, (cap, tc), 0)
    part = lax.broadcasted_iota(jnp.int32, (8, tc), 0)
    tok = lax.broadcasted_iota(jnp.int32, (1, tc), 1).astype(F32)
    chunks = range(0, seq_len, tc)
    tok_hi = [jnp.floor((tok + float(t0)) * (1.0 / TOKEN_SPLIT)) for t0 in chunks]
    tok_lo = [tok + float(t0) - TOKEN_SPLIT * hi for t0, hi in zip(chunks, tok_hi)]

    def per_expert(e, carry):
        for j in seqs:
            acc = jnp.zeros((cap, 8), F32)
            for ci, t0 in enumerate(chunks):
                cols = slice(seq_len * j + t0, seq_len * j + t0 + tc)
                onehot = jnp.where(slot == sp_ref[pl.ds(e, 1), cols], 1.0, 0.0).astype(BF16)
                a = aff_ref[pl.ds(e, 1), cols]
                a_hi = a.astype(BF16).astype(F32)
                a_mid = (a - a_hi).astype(BF16).astype(F32)
                a_lo = a - a_hi - a_mid
                vals = jnp.where(part == 0, tok_hi[ci], jnp.where(part == 1, tok_lo[ci], jnp.where(
                    part == 2, a_hi, jnp.where(part == 3, a_mid, jnp.where(part == 4, a_lo, 0.0)))))
                acc = acc + _dot_nt(onehot, vals.astype(BF16))
            idx_ref[N_EXPERTS * j + e] = (N_SLAB * (TOKEN_SPLIT * acc[:, 0:1] + acc[:, 1:2])).astype(jnp.int32)
            gs_ref[e, cap * j:cap * (j + 1)] = acc[:, 2:3] + acc[:, 3:4] + acc[:, 4:5]
        return carry

    lax.fori_loop(0, N_EXPERTS, per_expert, 0)


def _route(afft, triu_bf, n_seq, seq_len, cap, ns):
    n_blk = seq_len // 128
    idx, gs, sp = pl.pallas_call(
        functools.partial(_route_kernel, ns=ns, seq_len=seq_len, cap=cap),
        grid=(n_seq // ns,),
        in_specs=[
            pl.BlockSpec((N_EXPERTS, ns * seq_len), lambda s: (0, s)),
            pl.BlockSpec((128, 128), lambda s: (0, 0)),
        ],
        out_specs=[
            pl.BlockSpec((ns * N_EXPERTS, cap, 1), lambda s: (s, 0, 0)),
            pl.BlockSpec((N_EXPERTS, ns * cap, 1), lambda s: (0, s, 0)),
            pl.BlockSpec((N_EXPERTS, ns * seq_len), lambda s: (0, s)),
        ],
        out_shape=[
            jax.ShapeDtypeStruct((n_seq * N_EXPERTS, cap, 1), jnp.int32),
            jax.ShapeDtypeStruct((N_EXPERTS, n_seq * cap, 1), F32),
            jax.ShapeDtypeStruct((N_EXPERTS, n_seq * seq_len), jnp.int32),
        ],
        scratch_shapes=[
            pltpu.VMEM((ns * n_blk * N_EXPERTS, 128), F32),
            pltpu.VMEM((ns * n_blk * N_EXPERTS, 128), F32),
            pltpu.VMEM((2 * ns * N_EXPERTS, 128), F32),
        ] if n_blk >= 16 else [],
        compiler_params=_params(("arbitrary",)),
        name="route",
    )(afft, triu_bf)
    return idx.reshape(n_seq * N_EXPERTS * cap), gs, sp


ROW_COPIES = 8
GATHER_COPIES = 16


def _row_tile(first_row):
    return pl.ds(pl.multiple_of(first_row, N_SLAB), N_SLAB)


def _gather_kernel(idx_ref, src_ref, xs_ref, tile_ref, *, eb, cap):
    ei = pl.program_id(1)

    def per_expert(ee, carry):
        e = ei * eb + ee

        def rows(g, c):
            slot0 = g * GATHER_COPIES
            for u in range(GATHER_COPIES):
                tile_ref[_row_tile((slot0 + u) * N_SLAB), :] = src_ref[_row_tile(idx_ref[e * cap + slot0 + u]), :]
            return c

        lax.fori_loop(0, cap // GATHER_COPIES, rows, 0)
        for s in range(N_SLAB):
            xs_ref[ee, :, LANE * s:LANE * (s + 1)] = tile_ref[pl.ds(s, cap, stride=N_SLAB), :].astype(BF16)
        return carry

    lax.fori_loop(0, eb, per_expert, 0)


def _gather(idx, h2_rows, n_seq, cap, eb):
    return pl.pallas_call(
        functools.partial(_gather_kernel, eb=eb, cap=cap),
        grid=(n_seq, N_EXPERTS // eb),
        in_specs=[
            pl.BlockSpec((N_EXPERTS * cap,), lambda s, e: (s,), memory_space=pltpu.SMEM),
            pl.BlockSpec((h2_rows.shape[0] // n_seq, LANE), lambda s, e: (s, 0)),
        ],
        out_specs=pl.BlockSpec((eb, cap, D_MODEL), lambda s, e: (e, s, 0)),
        out_shape=jax.ShapeDtypeStruct((N_EXPERTS, n_seq * cap, D_MODEL), BF16),
        scratch_shapes=[pltpu.VMEM((N_SLAB * cap, LANE), F32)],
        compiler_params=_params(("arbitrary", "arbitrary"), vmem_mib=56),
        name="gather",
    )(idx, h2_rows)


def _selection(sp_ref, cap):
    seq_len = sp_ref.shape[1]
    slot = lax.broadcasted_iota(jnp.int32, (cap, seq_len), 0)
    return jnp.concatenate([jnp.where(slot == sp_ref[e:e + 1, :], 1.0, 0.0) for e in range(N_EXPERTS)],
                           axis=0).astype(BF16)


def _gather_short_kernel(sp_ref, h_ref, xs_ref, *, cap):
    rows = _dot(_selection(sp_ref, cap), h_ref[...])
    for e in range(N_EXPERTS):
        xs_ref[e] = rows[cap * e:cap * (e + 1), :].astype(BF16)


def _gather_short(sp, h2, n_seq, seq_len, cap):
    return pl.pallas_call(
        functools.partial(_gather_short_kernel, cap=cap),
        grid=(n_seq,),
        in_specs=[
            pl.BlockSpec((N_EXPERTS, seq_len), lambda s: (0, s)),
            pl.BlockSpec((seq_len, D_MODEL), lambda s: (s, 0)),
        ],
        out_specs=pl.BlockSpec((N_EXPERTS, cap, D_MODEL), lambda s: (0, s, 0)),
        out_shape=jax.ShapeDtypeStruct((N_EXPERTS, n_seq * cap, D_MODEL), BF16),
        compiler_params=_params(("arbitrary",)),
        name="gather_short",
    )(sp, h2)


def _scatter_short_kernel(sp_ref, y_ref, x_ref, mod_ref, gf_ref, o_ref, *, cap, final):
    sel = _selection(sp_ref, cap)
    y = jnp.concatenate([y_ref[e] for e in range(N_EXPERTS)], axis=0)
    y_hi = y.astype(BF16)
    y_lo = (y - y_hi.astype(F32)).astype(BF16)
    moe = _dot_tn(sel, y_hi) + _dot_tn(sel, y_lo)
    x = x_ref[...] + mod_ref[0, 5:6, :] * moe
    if final:
        x = x * lax.rsqrt(jnp.mean(x * x, axis=-1, keepdims=True) + EPS) * gf_ref[...]
    o_ref[...] = x


def _scatter_short(sp, ys, x2d, mod, g_final, n_seq, seq_len, cap, final):
    return pl.pallas_call(
        functools.partial(_scatter_short_kernel, cap=cap, final=final),
        grid=(n_seq,),
        in_specs=[
            pl.BlockSpec((N_EXPERTS, seq_len), lambda s: (0, s)),
            pl.BlockSpec((N_EXPERTS, cap, D_MODEL), lambda s: (0, s, 0)),
            pl.BlockSpec((seq_len, D_MODEL), lambda s: (s, 0)),
            pl.BlockSpec((1, 6, D_MODEL), lambda s: (0, 0, 0)),
            pl.BlockSpec((1, D_MODEL), lambda s: (0, 0)),
        ],
        out_specs=pl.BlockSpec((seq_len, D_MODEL), lambda s: (s, 0)),
        out_shape=jax.ShapeDtypeStruct((n_seq * seq_len, D_MODEL), F32),
        compiler_params=_params(("arbitrary",)),
        name="scatter_short",
    )(sp, ys, x2d, mod, g_final)


def _expert_kernel(xc_ref, xl_ref, gc_ref, gl_ref, wg_ref, wu_ref, wd_ref, yc_ref, yl_ref, *, n_f):
    f = pl.program_id(1)

    @pl.when(f == 0)
    def _():
        yc_ref[...] = jnp.zeros_like(yc_ref)
        yl_ref[...] = jnp.zeros_like(yl_ref)

    wg = wg_ref[0, 0].astype(BF16)
    wu = wu_ref[0, 0].astype(BF16)
    wd = wd_ref[0, 0].astype(BF16)
    for x_ref, y_ref in ((xc_ref, yc_ref), (xl_ref, yl_ref)):
        x = x_ref[0]
        mid = (_silu(_dot(x, wg)) * _dot(x, wu)).astype(BF16)
        y_ref[0] += _dot(mid, wd)

    @pl.when(f == n_f - 1)
    def _():
        yc_ref[0] = yc_ref[0] * gc_ref[0]
        yl_ref[0] = yl_ref[0] * gl_ref[0]


def _experts(xs_c, xs_l, gs_c, gs_l, w_g, w_u, w_d, layer):
    rc, rl = xs_c.shape[1], xs_l.shape[1]
    tf = 1024
    n_f = EXPERT_FF // tf
    return pl.pallas_call(
        functools.partial(_expert_kernel, n_f=n_f),
        grid=(N_EXPERTS, n_f),
        in_specs=[
            pl.BlockSpec((1, rc, D_MODEL), lambda e, f: (e, 0, 0)),
            pl.BlockSpec((1, rl, D_MODEL), lambda e, f: (e, 0, 0)),
            pl.BlockSpec((1, rc, 1), lambda e, f: (e, 0, 0)),
            pl.BlockSpec((1, rl, 1), lambda e, f: (e, 0, 0)),
            pl.BlockSpec((1, 1, D_MODEL, tf), lambda e, f: (layer, e, 0, f)),
            pl.BlockSpec((1, 1, D_MODEL, tf), lambda e, f: (layer, e, 0, f)),
            pl.BlockSpec((1, 1, tf, D_MODEL), lambda e, f: (layer, e, f, 0)),
        ],
        out_specs=[
            pl.BlockSpec((1, rc, D_MODEL), lambda e, f: (e, 0, 0)),
            pl.BlockSpec((1, rl, D_MODEL), lambda e, f: (e, 0, 0)),
        ],
        out_shape=[
            jax.ShapeDtypeStruct((N_EXPERTS, rc, D_MODEL), F32),
            jax.ShapeDtypeStruct((N_EXPERTS, rl, D_MODEL), F32),
        ],
        compiler_params=_params(("arbitrary", "arbitrary"), vmem_mib=56),
        name="experts",
    )(xs_c, xs_l, gs_c, gs_l, w_g, w_u, w_d)


def _scatter_kernel(idx_ref, y_ref, x_ref, mod_ref, gf_ref, o_ref, acc_ref, tile_ref, *, eb, n_e, cap, tm, final):
    step = pl.program_id(1)

    @pl.when(step == 0)
    def _():
        acc_ref[...] = jnp.zeros_like(acc_ref)

    @pl.when(step < n_e)
    def _():
        def per_expert(ee, carry):
            e = step * eb + ee
            for s in range(N_SLAB):
                tile_ref[pl.ds(s, cap, stride=N_SLAB), :] = y_ref[ee, :, LANE * s:LANE * (s + 1)]

            def rows(g, c):
                slot0 = g * ROW_COPIES
                dst = [idx_ref[e * cap + slot0 + u] for u in range(ROW_COPIES)]
                new = [acc_ref[_row_tile(dst[u]), :] + tile_ref[_row_tile((slot0 + u) * N_SLAB), :]
                       for u in range(ROW_COPIES)]
                for u in range(ROW_COPIES):
                    acc_ref[_row_tile(dst[u]), :] = new[u]
                return c

            lax.fori_loop(0, cap // ROW_COPIES, rows, 0)
            return carry

        lax.fori_loop(0, eb, per_expert, 0)

    @pl.when(step >= n_e)
    def _():
        base = pl.multiple_of((step - n_e) * tm * N_SLAB, N_SLAB)
        moe = jnp.concatenate([acc_ref[pl.ds(base + s, tm, stride=N_SLAB), :] for s in range(N_SLAB)], axis=1)
        x = x_ref[...] + mod_ref[0, 5:6, :] * moe
        if final:
            x = x * lax.rsqrt(jnp.mean(x * x, axis=-1, keepdims=True) + EPS) * gf_ref[...]
        o_ref[...] = x


def _scatter(idx, ys, x2d, mod, g_final, n_seq, seq_len, cap, eb, tm, final):
    n_e = N_EXPERTS // eb
    n_out = seq_len // tm
    out_blk = lambda s, j: (s * n_out + jnp.maximum(j - n_e, 0), 0)
    return pl.pallas_call(
        functools.partial(_scatter_kernel, eb=eb, n_e=n_e, cap=cap, tm=tm, final=final),
        grid=(n_seq, n_e + n_out),
        in_specs=[
            pl.BlockSpec((N_EXPERTS * cap,), lambda s, j: (s,), memory_space=pltpu.SMEM),
            pl.BlockSpec((eb, cap, D_MODEL), lambda s, j: (jnp.minimum(j, n_e - 1), s, 0)),
            pl.BlockSpec((tm, D_MODEL), out_blk),
            pl.BlockSpec((1, 6, D_MODEL), lambda s, j: (s, 0, 0)),
            pl.BlockSpec((1, D_MODEL), lambda s, j: (0, 0)),
        ],
        out_specs=pl.BlockSpec((tm, D_MODEL), out_blk),
        out_shape=jax.ShapeDtypeStruct((n_seq * seq_len, D_MODEL), F32),
        scratch_shapes=[
            pltpu.VMEM((N_SLAB * seq_len, LANE), F32),
            pltpu.VMEM((N_SLAB * cap, LANE), F32),
        ],
        compiler_params=_params(("arbitrary", "arbitrary"), vmem_mib=56),
        name="scatter",
    )(idx, ys, x2d, mod, g_final)


def kernel(x_prompt, x_sample, c, cache_k, cache_v, state_C, state_n, state_m, c_ctx, w_ada, b_ada, g_norm1, g_norm2, w_in, b_gates, rpb, w_fourier, g_head, w_out, w_router, w_exp_gate, w_exp_up, w_exp_down, g_final):
    n_ctx, len_ctx, _ = x_prompt.shape
    n_lat, len_lat, _ = x_sample.shape
    past = cache_k.shape[2]
    cap_ctx = CAPACITY_FACTOR * len_ctx // N_EXPERTS
    cap_lat = CAPACITY_FACTOR * len_lat // N_EXPERTS

    w_in_bf = w_in.astype(BF16)
    w_gt_bf = jnp.swapaxes(w_in[:, :, P_COLS:], 1, 2).astype(BF16)
    vc0, oc0 = VC_BLK * C_W, OC_BLK * C_W
    w_vo_bf = jnp.swapaxes(jnp.concatenate([w_in[:, :, vc0:vc0 + C_W], w_in[:, :, oc0:oc0 + C_W]], axis=2),
                           1, 2).astype(BF16)
    bg_row = b_gates.reshape(DEPTH, 1, N_GATE_COLS).astype(F32)
    bg_col = b_gates.reshape(DEPTH, N_GATE_COLS, 1).astype(F32)
    w_out_bf = w_out.astype(BF16)
    w_rt_bf = jnp.swapaxes(w_router, 1, 2).astype(BF16)
    g1 = g_norm1.reshape(DEPTH, 1, D_MODEL)
    g2 = g_norm2.reshape(DEPTH, 1, D_MODEL)
    gh = g_head.reshape(DEPTH, 1, D_MODEL)
    g_mem = jnp.broadcast_to(gh[:, 0, A_W + B_W:, None], (DEPTH, C_W, MERGE_TILE))
    eye_g = jnp.eye(G_FOURIER, dtype=F32)
    wf_blk = jnp.einsum("lgcd,gh->lgchd", w_fourier, eye_g).reshape(DEPTH, B_W, B_W).astype(BF16)

    csc = _channel_dft()
    dft_ctx = _dft_mats(len_ctx)
    dft_side = _dft_mats(GRID_W)
    tw_cos, tw_sin = _twiddles(GRID_W, len_lat)
    bias_tiles = _nbr_bias_tiles(rpb)
    r = np.arange(MLSTM_CHUNK)
    tril = jnp.asarray(r[:, None] >= r[None, :], F32)
    triu = jnp.asarray(r[:, None] <= r[None, :], F32)
    triu_bf = triu.astype(BF16)
    hidx = np.arange(A_W) // HEAD_DIM
    ones_blk = jnp.asarray(hidx[:, None] == hidx[None, :], BF16)

    cvecs = jnp.concatenate([c_ctx[None, :], c, jnp.zeros((8 - 1 - n_lat, D_MODEL), F32)], axis=0)
    mod_all = _modulation(cvecs, w_ada, b_ada).reshape(DEPTH, 8, 6, D_MODEL)

    cache_k4 = cache_k.reshape(n_lat, DEPTH, past, A_W)
    cache_v4 = cache_v.reshape(n_lat, DEPTH, past, A_W)
    zero_c = jnp.zeros((n_ctx, 2, H_MLSTM, HEAD_DIM, HEAD_DIM), F32)
    zero_n = jnp.zeros((n_ctx, 2, H_MLSTM, HEAD_DIM), F32)
    zero_m = jnp.zeros((n_ctx, 2, H_MLSTM), F32)

    xc = x_prompt.reshape(n_ctx * len_ctx, D_MODEL)
    xl = x_sample.reshape(n_lat * len_lat, D_MODEL)
    gf = g_final.reshape(1, D_MODEL)
    new_k = jnp.zeros((n_ctx, DEPTH, len_ctx, A_W), F32)
    new_v = jnp.zeros((n_ctx, DEPTH, len_ctx, A_W), F32)
    cs, ns, ms = [], [], []
    for l in range(DEPTH):
        mod_c = mod_all[l, 0:1]
        mod_l = mod_all[l, 1:1 + n_lat]

        pc, gc, gtc, voc, abc, new_k, new_v = _inproj(xc, mod_c, l, g1, w_in_bf, w_gt_bf, w_vo_bf, bg_row, bg_col, csc,
                                                      n_ctx * len_ctx, False, (new_k, new_v))
        att_c = _ctx_attention(pc, n_ctx, len_ctx)
        four_c = _fourier(abc, dft_ctx[0], dft_ctx[1], wf_blk, l, n_ctx, len_ctx)
        hf_c, hb_c, c_new, n_new, m_new = _mlstm(pc, gc, gtc, voc, zero_c, zero_n, zero_m, tril, triu, n_ctx, len_ctx)
        xc, h2c, affc = _merge(att_c, four_c, hf_c, hb_c, voc, xc, mod_c, l, gh, g_mem, w_out_bf, g2, w_rt_bf,
                               ones_blk, n_ctx, len_ctx, n_ctx * len_ctx, False)
        cs.append(c_new)
        ns.append(n_new)
        ms.append(m_new)

        pq, gq, gtq, voq, abq = _inproj(xl, mod_l, l, g1, w_in_bf, w_gt_bf, w_vo_bf, bg_row, bg_col, csc, len_lat,
                                        True)
        att_l = _nbr_attention(pq, cache_k4, cache_v4, bias_tiles, l, n_lat, len_lat)
        four_l = _fourier_grid(abq, dft_side[0], dft_side[1], tw_cos, tw_sin, wf_blk, l, n_lat, GRID_W)
        hf_l, hb_l, _, _, _ = _mlstm(pq, gq, gtq, voq, state_C[:, l], state_n[:, l], state_m[:, l], tril, triu,
                                     n_lat, len_lat)
        xl, h2l, affl = _merge(att_l, four_l, hf_l, hb_l, voq, xl, mod_l, l, gh, g_mem, w_out_bf, g2, w_rt_bf,
                               ones_blk, n_lat, len_lat, len_lat, True)

        last = l == DEPTH - 1
        _, gs_c, sp_c = _route(affc, triu_bf, n_ctx, len_ctx, cap_ctx, n_ctx)
        idx_l, gs_l, _ = _route(affl, triu_bf, n_lat, len_lat, cap_lat, n_lat)
        xs_c = _gather_short(sp_c, h2c, n_ctx, len_ctx, cap_ctx)
        xs_l = _gather(idx_l, h2l, n_lat, cap_lat, 4)
        ys_c, ys_l = _experts(xs_c, xs_l, gs_c, gs_l, w_exp_gate, w_exp_up, w_exp_down, l)
        xc = _scatter_short(sp_c, ys_c, xc, mod_c, gf, n_ctx, len_ctx, cap_ctx, last)
        xl = _scatter(idx_l, ys_l, xl, mod_l, gf, n_lat, len_lat, cap_lat, 4, 1024, last)

    y_prompt = xc.reshape(n_ctx, len_ctx, D_MODEL)
    y_sample = xl.reshape(n_lat, len_lat, D_MODEL)
    kv_shape = (n_ctx, DEPTH, len_ctx, H_ATT, HEAD_DIM)
    return (y_prompt, y_sample, new_k.reshape(kv_shape), new_v.reshape(kv_shape), jnp.stack(cs, axis=1),
            jnp.stack(ns, axis=1), jnp.stack(ms, axis=1))
```
